```python
import math
import jax
import jax.numpy as jnp
from jax import lax

D_MODEL = 2048
BATCH = 4
SEQ = 2048
DEPTH = 2
DEC_BATCH = 128
DEC_SEQ = 1
PAST_LEN = 16384
PAGE_SIZE = 128

EPS = 1e-6
CHUNK = 64
MIX_WIDTH = D_MODEL
GLA_HEADS = 4
GLA_V = MIX_WIDTH // 2
GLA_DV = GLA_V // GLA_HEADS
GLA_DK = GLA_DV // 2
GLA_QK = GLA_HEADS * GLA_DK
GLA_RANK = 16
GLA_TAU = 16.0
RG_WIDTH = MIX_WIDTH - GLA_V
RG_BLOCKS = 8
RG_BS = RG_WIDTH // RG_BLOCKS
RG_C = 8.0
CONV_W = 4
ML_HEADS = 4
ML_WIDTH = MIX_WIDTH // 2
ML_DH = ML_WIDTH // ML_HEADS
ML_BS = 4
ML_NBLK = ML_WIDTH // ML_BS
S5_WIDTH = MIX_WIDTH - ML_WIDTH
S5_GROUP = 16
S5_GROUPS = S5_WIDTH // S5_GROUP
S5_P = 64
S5_DT_MIN = 0.001
S5_DT_MAX = 0.1
FFN_DIM = 128 * ((8 * D_MODEL // 3 + 127) // 128)
FFN_CONV_W = 3

kernel_name = 'hybrid_gla_rglru_mlstm_s5_step'


def rmsnorm(x, g):
    xf = x.astype(jnp.float32)
    y = xf * lax.rsqrt(jnp.mean(xf * xf, axis=-1, keepdims=True) + EPS)
    return (y * g.astype(jnp.float32)).astype(x.dtype)


def head_rmsnorm(o, g):
    Bsz, L, H, d = o.shape
    y = o * lax.rsqrt(jnp.mean(o * o, axis=-1, keepdims=True) + EPS)
    return y.reshape(Bsz, L, H * d) * g.astype(jnp.float32)


def split_cols(t, sizes):
    idx, acc = [], 0
    for s in sizes[:-1]:
        acc += s
        idx.append(acc)
    return jnp.split(t, idx, axis=-1)


def causal_dwconv(x, prev, w, b):
    W = w.shape[0]
    L = x.shape[1]
    xp = jnp.concatenate([prev.astype(x.dtype), x], axis=1)
    y = b.astype(x.dtype) + w[0].astype(x.dtype) * xp[:, :L]
    for j in range(1, W):
        y = y + w[j].astype(x.dtype) * xp[:, j:j + L]
    return y, xp[:, L:]


def blockdiag(x, w):
    Bsz, L, _ = x.shape
    nb, bs, _ = w.shape
    return jnp.einsum('blnc,ncd->blnd', x.reshape(Bsz, L, nb, bs), w).reshape(Bsz, L, nb * bs)


def chunk_len(L):
    return CHUNK if L % CHUNK == 0 else L


def to_chunks(t, C):
    Bsz, L = t.shape[:2]
    t = t.reshape((Bsz, L // C, C) + t.shape[2:])
    return jnp.moveaxis(jnp.moveaxis(t, 1, 0), 2, 3)


def from_chunks(t):
    N, Bsz, H, C = t.shape[:4]
    t = jnp.moveaxis(jnp.moveaxis(t, 3, 2), 0, 1)
    return t.reshape((Bsz, N * C, H) + t.shape[4:])


def linear_scan_combine(e1, e2):
    a1, b1 = e1
    a2, b2 = e2
    return a1 * a2, a2 * b1 + b2


def gla_recurrence(q, k, v, log_a, S0):
    L = q.shape[1]
    C = chunk_len(L)
    f32 = jnp.float32
    qc, kc, vc, gc = (to_chunks(t.astype(f32), C) for t in (q, k, v, log_a))
    causal = jnp.tril(jnp.ones((C, C), dtype=bool))

    def step(S, inp):
        qi, ki, vi, gi = inp
        b = jnp.cumsum(gi, axis=2)
        qd = qi * jnp.exp(b)
        kd = ki * jnp.exp(-b)
        att = jnp.where(causal, jnp.einsum('bhik,bhjk->bhij', qd, kd), 0.0)
        o = jnp.einsum('bhik,bhkv->bhiv', qd, S) + jnp.einsum('bhij,bhjv->bhiv', att, vi)
        b_last = b[:, :, -1:, :]
        S = jnp.exp(b_last[:, :, 0, :])[..., None] * S + jnp.einsum('bhjk,bhjv->bhkv', ki * jnp.exp(b_last - b), vi)
        return S, o

    S, o = lax.scan(step, S0.astype(f32), (qc, kc, vc, gc))
    return from_chunks(o), S


def rglru(x, h0, w_r, b_r, w_i, b_i, lam, start_pos):
    f32 = jnp.float32
    Bsz, L, W = x.shape
    xf = x.astype(f32)
    xb = xf.reshape(Bsz, L, RG_BLOCKS, RG_BS)
    r = jax.nn.sigmoid(jnp.einsum('blnc,ncd->blnd', xb, w_r) + b_r).reshape(Bsz, L, W)
    i = jax.nn.sigmoid(jnp.einsum('blnc,ncd->blnd', xb, w_i) + b_i).reshape(Bsz, L, W)
    log_a = -RG_C * r * jax.nn.softplus(-lam.astype(f32))
    a = jnp.exp(log_a)
    mult = jnp.sqrt(-jnp.expm1(2.0 * log_a))
    is_first = ((jnp.arange(L) + start_pos) == 0)[None, :, None]
    mult = jnp.where(is_first, 1.0, mult)
    a_cum, h = lax.associative_scan(linear_scan_combine, (a, mult * (i * xf)), axis=1)
    h = h + a_cum * h0.astype(f32)[:, None, :]
    return h, h[:, -1]


def mlstm_recurrence(q, k, v, i_pre, log_f, C0, n0, m0):
    f32 = jnp.float32
    L = q.shape[1]
    C = chunk_len(L)
    qc, kc, vc = (to_chunks(t.astype(f32), C) for t in (q, k, v))
    ic, fc = (to_chunks(t.astype(f32), C) for t in (i_pre, log_f))
    causal = jnp.tril(jnp.ones((C, C), dtype=bool))

    def step(carry, inp):
        Cs, ns, ms = carry
        qi, ki, vi, ii, fi = inp
        b = jnp.cumsum(fi, axis=-1)
        dmat = jnp.where(causal, b[..., :, None] - b[..., None, :] + ii[..., None, :], -jnp.inf)
        inter = b + ms[..., None]
        m = jnp.maximum(inter, jnp.max(dmat, axis=-1))
        g = jnp.exp(inter - m)
        s = jnp.einsum('bhtd,bhsd->bhts', qi, ki) * jnp.exp(dmat - m[..., None])
        num = g[..., None] * jnp.einsum('bhed,bhtd->bhte', Cs, qi) + jnp.einsum('bhts,bhse->bhte', s, vi)
        den = g * jnp.einsum('bhd,bhtd->bht', ns, qi) + jnp.sum(s, axis=-1)
        h = num / jnp.maximum(jnp.abs(den), jnp.exp(-m))[..., None]
        m_new = m[..., -1]
        wk = jnp.exp(b[..., -1:] - b + ii - m_new[..., None])
        gc = jnp.exp(inter[..., -1] - m_new)
        Cs = gc[..., None, None] * Cs + jnp.einsum('bhse,bhsd->bhed', vi, ki * wk[..., None])
        ns = gc[..., None] * ns + jnp.einsum('bhsd,bhs->bhd', ki, wk)
        return (Cs, ns, m_new), h

    (Cn, nn, mn), h = lax.scan(step, (C0.astype(f32), n0.astype(f32), m0.astype(f32)), (qc, kc, vc, ic, fc))
    return from_chunks(h), Cn, nn, mn


def s5_recurrence(u, h0_re, h0_im, lam_re, lam_im, log_dt, B_re, B_im, C_re, C_im):
    f32 = jnp.float32
    Bsz, L, _ = u.shape
    A = lax.complex(lam_re.astype(f32), lam_im.astype(f32))
    A_bar = jnp.exp(jnp.exp(log_dt.astype(f32))[:, None] * A)
    B_bar = ((A_bar - 1.0) / A)[..., None] * lax.complex(B_re.astype(f32), B_im.astype(f32))
    ug = u.astype(f32).reshape(Bsz, L, S5_GROUPS, S5_GROUP)
    bu = jnp.einsum('gpc,blgc->blgp', B_bar, ug)
    a_cum, h = lax.associative_scan(linear_scan_combine, (jnp.broadcast_to(A_bar, bu.shape), bu), axis=1)
    h = h + a_cum * lax.complex(h0_re.astype(f32), h0_im.astype(f32))[:, None]
    Cm = lax.complex(C_re.astype(f32), C_im.astype(f32))
    y = jnp.real(jnp.einsum('gcp,blgp->blgc', Cm, h)).reshape(Bsz, L, S5_WIDTH)
    return y, jnp.real(h[:, -1]), jnp.imag(h[:, -1])


def layer_even(x, st, start_pos, p):
    f32 = jnp.float32
    Bsz, L, _ = x.shape
    h = rmsnorm(x, p['g_mix0'])
    q, k, v, r, a_lr, rg_x, rg_g = split_cols(h @ p['w_in0'], (GLA_QK, GLA_QK, GLA_V, GLA_V, GLA_RANK, RG_WIDTH, RG_WIDTH))
    q = q.reshape(Bsz, L, GLA_HEADS, GLA_DK) * (GLA_DK ** -0.5)
    k = k.reshape(Bsz, L, GLA_HEADS, GLA_DK)
    v = v.reshape(Bsz, L, GLA_HEADS, GLA_DV)
    log_a = jax.nn.log_sigmoid((a_lr @ p['gla_w_alpha2']).astype(f32) + p['gla_b_alpha'].astype(f32)) / GLA_TAU
    o, S_new = gla_recurrence(q, k, v, log_a.reshape(Bsz, L, GLA_HEADS, GLA_DK), st['gla_S'])
    y_a = head_rmsnorm(o, p['gla_g_norm']) * jax.nn.silu(r.astype(f32))
    xc, rg_conv_new = causal_dwconv(rg_x, st['rg_conv'], p['rg_conv_w'], p['rg_conv_b'])
    hr, rg_h_new = rglru(xc, st['rg_h'], p['rg_w_r'], p['rg_b_r'], p['rg_w_i'], p['rg_b_i'], p['rg_lambda'], start_pos)
    y_b = hr * jax.nn.gelu(rg_g.astype(f32))
    x = x + jnp.concatenate([y_a, y_b], axis=-1).astype(x.dtype) @ p['w_out0']
    return x, {'gla_S': S_new, 'rg_h': rg_h_new, 'rg_conv': rg_conv_new}


def layer_odd(x, st, p):
    f32 = jnp.float32
    Bsz, L, _ = x.shape
    h = rmsnorm(x, p['g_mix1'])
    xm, om, us = split_cols(h @ p['w_in1'], (ML_WIDTH, ML_WIDTH, S5_WIDTH))
    xc, ml_conv_new = causal_dwconv(xm, st['ml_conv'], p['ml_conv_w'], p['ml_conv_b'])
    xc = jax.nn.silu(xc.astype(f32))
    q = blockdiag(xc, p['ml_wq'])
    k = blockdiag(xc, p['ml_wk']) * (ML_DH ** -0.5)
    v = blockdiag(xm.astype(f32), p['ml_wv'])
    qkv = jnp.concatenate([q, k, v], axis=-1)
    i_pre = qkv @ p['ml_w_igate'] + p['ml_b_igate']
    log_f = jax.nn.log_sigmoid(qkv @ p['ml_w_fgate'] + p['ml_b_fgate'])
    heads = lambda t: t.reshape(Bsz, L, ML_HEADS, ML_DH)
    hm, C_new, n_new, m_new = mlstm_recurrence(heads(q), heads(k), heads(v), i_pre, log_f, st['ml_C'], st['ml_n'], st['ml_m'])
    y_c = (head_rmsnorm(hm, p['ml_g_norm']) + p['ml_skip'] * xc) * jax.nn.sigmoid(om.astype(f32))
    uf = us.astype(f32)
    ys, s5_re_new, s5_im_new = s5_recurrence(uf, st['s5_re'], st['s5_im'], p['s5_lam_re'], p['s5_lam_im'], p['s5_log_dt'],
                                             p['s5_B_re'], p['s5_B_im'], p['s5_C_re'], p['s5_C_im'])
    ys = jax.nn.gelu(ys + p['s5_D'] * uf)
    y_d = ys * jax.nn.sigmoid(ys @ p['s5_w_glu'] + p['s5_b_glu'])
    x = x + jnp.concatenate([y_c, y_d], axis=-1).astype(x.dtype) @ p['w_out1']
    return x, {'ml_C': C_new, 'ml_n': n_new, 'ml_m': m_new, 'ml_conv': ml_conv_new, 's5_re': s5_re_new, 's5_im': s5_im_new}


def conv_ffn(h, buf, w_up, conv_w, conv_b, w_down):
    up, buf_new = causal_dwconv(h @ w_up, buf, conv_w, conv_b)
    gate, val = jnp.split(up, 2, axis=-1)
    return (jax.nn.gelu(gate) * val) @ w_down, buf_new


def trunk(x, st, start_pos, p):
    new_mix = {}
    ffn_bufs = []
    for layer in range(DEPTH):
        if layer % 2 == 0:
            x, upd = layer_even(x, st, start_pos, p)
        else:
            x, upd = layer_odd(x, st, p)
        new_mix.update(upd)
        y, buf = conv_ffn(rmsnorm(x, p['g_ffn'][layer]), st['ffn_conv'][layer], p['ffn_w_up'][layer],
                          p['ffn_conv_w'][layer], p['ffn_conv_b'][layer], p['ffn_w_down'][layer])
        x = x + y
        ffn_bufs.append(buf)
    return rmsnorm(x, p['g_final']), new_mix, jnp.stack(ffn_bufs, axis=0)


def setup_inputs(seed: int = 0) -> dict:
    key = jax.random.key(seed)
    ks = iter(jax.random.split(key, 64))
    f32 = jnp.float32

    def nrm(shape, scale):
        return scale * jax.random.normal(next(ks), shape, f32)

    def gain(shape):
        return 1.0 + nrm(shape, 0.01)

    in0 = 2 * GLA_QK + 2 * GLA_V + GLA_RANK + 2 * RG_WIDTH
    in1 = 2 * ML_WIDTH + S5_WIDTH
    a_c = jax.random.uniform(next(ks), (RG_WIDTH,), f32, 0.9, 0.999)
    sig = a_c ** (1.0 / RG_C)
    rg_lambda = jnp.log(sig) - jnp.log1p(-sig)
    n_idx = jnp.arange(S5_P, dtype=f32)
    return {
        'x_prompt': nrm((BATCH, SEQ, D_MODEL), 1.0),
        'x_sample': nrm((DEC_BATCH, DEC_SEQ, D_MODEL), 1.0),
        'state_gla_S': nrm((DEC_BATCH, GLA_HEADS, GLA_DK, GLA_DV), 0.1),
        'state_rglru_h': nrm((DEC_BATCH, RG_WIDTH), 0.5),
        'state_rglru_conv': nrm((DEC_BATCH, CONV_W - 1, RG_WIDTH), 1.0),
        'state_mlstm_C': nrm((DEC_BATCH, ML_HEADS, ML_DH, ML_DH), 0.05),
        'state_mlstm_n': nrm((DEC_BATCH, ML_HEADS, ML_DH), 0.1),
        'state_mlstm_m': nrm((DEC_BATCH, ML_HEADS), 1.0),
        'state_mlstm_conv': nrm((DEC_BATCH, CONV_W - 1, ML_WIDTH), 1.0),
        'state_s5_re': nrm((DEC_BATCH, S5_GROUPS, S5_P), 0.1),
        'state_s5_im': nrm((DEC_BATCH, S5_GROUPS, S5_P), 0.1),
        'state_ffn_conv': nrm((DEPTH, DEC_BATCH, FFN_CONV_W - 1, 2 * FFN_DIM), 1.0),
        'g_mix0': gain((D_MODEL,)),
        'w_in0': nrm((D_MODEL, in0), D_MODEL ** -0.5),
        'gla_w_alpha2': nrm((GLA_RANK, GLA_QK), GLA_RANK ** -0.5),
        'gla_b_alpha': nrm((GLA_QK,), 0.1),
        'gla_g_norm': gain((GLA_V,)),
        'rg_conv_w': nrm((CONV_W, RG_WIDTH), CONV_W ** -0.5),
        'rg_conv_b': nrm((RG_WIDTH,), 0.01),
        'rg_w_r': nrm((RG_BLOCKS, RG_BS, RG_BS), RG_BS ** -0.5),
        'rg_b_r': nrm((RG_BLOCKS, RG_BS), 0.1),
        'rg_w_i': nrm((RG_BLOCKS, RG_BS, RG_BS), RG_BS ** -0.5),
        'rg_b_i': nrm((RG_BLOCKS, RG_BS), 0.1),
        'rg_lambda': rg_lambda,
        'w_out0': nrm((MIX_WIDTH, D_MODEL), MIX_WIDTH ** -0.5),
        'g_mix1': gain((D_MODEL,)),
        'w_in1': nrm((D_MODEL, in1), D_MODEL ** -0.5),
        'ml_conv_w': nrm((CONV_W, ML_WIDTH), CONV_W ** -0.5),
        'ml_conv_b': nrm((ML_WIDTH,), 0.01),
        'ml_wq': nrm((ML_NBLK, ML_BS, ML_BS), ML_BS ** -0.5),
        'ml_wk': nrm((ML_NBLK, ML_BS, ML_BS), ML_BS ** -0.5),
        'ml_wv': nrm((ML_NBLK, ML_BS, ML_BS), ML_BS ** -0.5),
        'ml_w_igate': nrm((3 * ML_WIDTH, ML_HEADS), (3 * ML_WIDTH) ** -0.5),
        'ml_b_igate': nrm((ML_HEADS,), 0.1),
        'ml_w_fgate': nrm((3 * ML_WIDTH, ML_HEADS), (3 * ML_WIDTH) ** -0.5),
        'ml_b_fgate': jnp.linspace(3.0, 6.0, ML_HEADS, dtype=f32) + nrm((ML_HEADS,), 0.01),
        'ml_g_norm': gain((ML_WIDTH,)),
        'ml_skip': gain((ML_WIDTH,)),
        's5_lam_re': -0.5 + nrm((S5_GROUPS, S5_P), 0.01),
        's5_lam_im': math.pi * n_idx[None, :] + nrm((S5_GROUPS, S5_P), 0.01),
        's5_log_dt': jax.random.uniform(next(ks), (S5_GROUPS,), f32, math.log(S5_DT_MIN), math.log(S5_DT_MAX)),
        's5_B_re': nrm((S5_GROUPS, S5_P, S5_GROUP), (2.0 * S5_GROUP) ** -0.5),
        's5_B_im': nrm((S5_GROUPS, S5_P, S5_GROUP), (2.0 * S5_GROUP) ** -0.5),
        's5_C_re': nrm((S5_GROUPS, S5_GROUP, S5_P), (2.0 * S5_P) ** -0.5),
        's5_C_im': nrm((S5_GROUPS, S5_GROUP, S5_P), (2.0 * S5_P) ** -0.5),
        's5_D': nrm((S5_WIDTH,), 1.0),
        's5_w_glu': nrm((S5_WIDTH, S5_WIDTH), S5_WIDTH ** -0.5),
        's5_b_glu': nrm((S5_WIDTH,), 0.01),
        'w_out1': nrm((MIX_WIDTH, D_MODEL), MIX_WIDTH ** -0.5),
        'g_ffn': gain((DEPTH, D_MODEL)),
        'ffn_w_up': nrm((DEPTH, D_MODEL, 2 * FFN_DIM), D_MODEL ** -0.5),
        'ffn_conv_w': nrm((DEPTH, FFN_CONV_W, 2 * FFN_DIM), FFN_CONV_W ** -0.5),
        'ffn_conv_b': nrm((DEPTH, 2 * FFN_DIM), 0.01),
        'ffn_w_down': nrm((DEPTH, FFN_DIM, D_MODEL), FFN_DIM ** -0.5),
        'g_final': gain((D_MODEL,)),
    }


def reference(x_prompt, x_sample, state_gla_S, state_rglru_h, state_rglru_conv, state_mlstm_C, state_mlstm_n,
              state_mlstm_m, state_mlstm_conv, state_s5_re, state_s5_im, state_ffn_conv,
              g_mix0, w_in0, gla_w_alpha2, gla_b_alpha, gla_g_norm, rg_conv_w, rg_conv_b, rg_w_r, rg_b_r,
              rg_w_i, rg_b_i, rg_lambda, w_out0,
              g_mix1, w_in1, ml_conv_w, ml_conv_b, ml_wq, ml_wk, ml_wv, ml_w_igate, ml_b_igate, ml_w_fgate,
              ml_b_fgate, ml_g_norm, ml_skip, s5_lam_re, s5_lam_im, s5_log_dt, s5_B_re, s5_B_im, s5_C_re, s5_C_im,
              s5_D, s5_w_glu, s5_b_glu, w_out1,
              g_ffn, ffn_w_up, ffn_conv_w, ffn_conv_b, ffn_w_down, g_final):
    f32 = jnp.float32
    p = {
        'g_mix0': g_mix0, 'w_in0': w_in0, 'gla_w_alpha2': gla_w_alpha2, 'gla_b_alpha': gla_b_alpha,
        'gla_g_norm': gla_g_norm, 'rg_conv_w': rg_conv_w, 'rg_conv_b': rg_conv_b, 'rg_w_r': rg_w_r,
        'rg_b_r': rg_b_r, 'rg_w_i': rg_w_i, 'rg_b_i': rg_b_i, 'rg_lambda': rg_lambda, 'w_out0': w_out0,
        'g_mix1': g_mix1, 'w_in1': w_in1, 'ml_conv_w': ml_conv_w, 'ml_conv_b': ml_conv_b, 'ml_wq': ml_wq,
        'ml_wk': ml_wk, 'ml_wv': ml_wv, 'ml_w_igate': ml_w_igate, 'ml_b_igate': ml_b_igate,
        'ml_w_fgate': ml_w_fgate, 'ml_b_fgate': ml_b_fgate, 'ml_g_norm': ml_g_norm, 'ml_skip': ml_skip,
        's5_lam_re': s5_lam_re, 's5_lam_im': s5_lam_im, 's5_log_dt': s5_log_dt, 's5_B_re': s5_B_re,
        's5_B_im': s5_B_im, 's5_C_re': s5_C_re, 's5_C_im': s5_C_im, 's5_D': s5_D, 's5_w_glu': s5_w_glu,
        's5_b_glu': s5_b_glu, 'w_out1': w_out1, 'g_ffn': g_ffn, 'ffn_w_up': ffn_w_up,
        'ffn_conv_w': ffn_conv_w, 'ffn_conv_b': ffn_conv_b, 'ffn_w_down': ffn_w_down, 'g_final': g_final,
    }
    Bp = x_prompt.shape[0]
    st_p = {
        'gla_S': jnp.zeros((Bp, GLA_HEADS, GLA_DK, GLA_DV), f32),
        'rg_h': jnp.zeros((Bp, RG_WIDTH), f32),
        'rg_conv': jnp.zeros((Bp, CONV_W - 1, RG_WIDTH), x_prompt.dtype),
        'ml_C': jnp.zeros((Bp, ML_HEADS, ML_DH, ML_DH), f32),
        'ml_n': jnp.zeros((Bp, ML_HEADS, ML_DH), f32),
        'ml_m': jnp.zeros((Bp, ML_HEADS), f32),
        'ml_conv': jnp.zeros((Bp, CONV_W - 1, ML_WIDTH), x_prompt.dtype),
        's5_re': jnp.zeros((Bp, S5_GROUPS, S5_P), f32),
        's5_im': jnp.zeros((Bp, S5_GROUPS, S5_P), f32),
        'ffn_conv': jnp.zeros((DEPTH, Bp, FFN_CONV_W - 1, 2 * FFN_DIM), x_prompt.dtype),
    }
    st_s = {
        'gla_S': state_gla_S, 'rg_h': state_rglru_h, 'rg_conv': state_rglru_conv,
        'ml_C': state_mlstm_C, 'ml_n': state_mlstm_n, 'ml_m': state_mlstm_m, 'ml_conv': state_mlstm_conv,
        's5_re': state_s5_re, 's5_im': state_s5_im, 'ffn_conv': state_ffn_conv,
    }
    y_prompt, np_, ffn_p = trunk(x_prompt, st_p, 0, p)
    y_sample, ns_, ffn_s = trunk(x_sample, st_s, PAST_LEN, p)
    return (y_prompt, y_sample,
            np_['gla_S'], ns_['gla_S'], np_['rg_h'], ns_['rg_h'], np_['rg_conv'], ns_['rg_conv'],
            np_['ml_C'], ns_['ml_C'], np_['ml_n'], ns_['ml_n'], np_['ml_m'], ns_['ml_m'],
            np_['ml_conv'], ns_['ml_conv'], np_['s5_re'], ns_['s5_re'], np_['s5_im'], ns_['s5_im'],
            ffn_p, ffn_s)
```

```python
import functools

import jax
import jax.numpy as jnp
from jax import lax
from jax.experimental import pallas as pl
from jax.experimental.pallas import tpu as pltpu

f32 = jnp.float32
bf16 = jnp.bfloat16

EPS = 1e-6
CHUNK = 64
GLA_TAU = 16.0
RG_C = 8.0
PAST_LEN = 16384
LANES = 128
SUBLANES = 8
HALO = 16
MIB = 1024 * 1024
HIGHEST = lax.Precision.HIGHEST


def _params(sem, vmem_mib):
    return pltpu.CompilerParams(dimension_semantics=sem, vmem_limit_bytes=int(vmem_mib * MIB))


def _dot(a, b):
    return jnp.dot(a, b, preferred_element_type=f32)


def _dot_nt(a, b):
    return lax.dot_general(a, b, (((1,), (1,)), ((), ())), preferred_element_type=f32)


def _dot_tn(a, b):
    return lax.dot_general(a, b, (((0,), (0,)), ((), ())), preferred_element_type=f32)


def _rms(x, g):
    return x * lax.rsqrt(jnp.mean(x * x, axis=-1, keepdims=True) + EPS) * g


def _eye(n):
    return lax.broadcasted_iota(jnp.int32, (n, n), 0) == lax.broadcasted_iota(jnp.int32, (n, n), 1)


def _col_from_row(row):
    n = row.shape[1]
    return jnp.sum(jnp.where(_eye(n), jnp.broadcast_to(row, (n, n)), 0.0), axis=1, keepdims=True)


def _row_from_col(col):
    n = col.shape[0]
    return jnp.sum(jnp.where(_eye(n), jnp.broadcast_to(col, (n, n)), 0.0), axis=0, keepdims=True)


def _shifted(prev8, x, back):
    xx = jnp.concatenate([prev8, x], axis=0)
    n = x.shape[0]
    return xx[SUBLANES - back:SUBLANES - back + n]


def _norm_matmul(x, g, w, *, tm, tn):
    M, D = x.shape
    N = w.shape[1]
    rc = min(tm, 256)

    def body(x_ref, g_ref, w_ref, o_ref, xn_ref):
        @pl.when(pl.program_id(1) == 0)
        def _():
            def chunk(r, c):
                rows = pl.ds(pl.multiple_of(r * rc, rc), rc)
                xn_ref[rows, :] = _rms(x_ref[rows, :], g_ref[...]).astype(bf16)
                return c
            lax.fori_loop(0, tm // rc, chunk, 0)
        o_ref[...] = _dot(xn_ref[...], w_ref[...].astype(bf16))

    return pl.pallas_call(
        body,
        grid=(M // tm, N // tn),
        in_specs=[pl.BlockSpec((tm, D), lambda i, j: (i, 0)),
                  pl.BlockSpec((1, D), lambda i, j: (0, 0)),
                  pl.BlockSpec((D, tn), lambda i, j: (0, j))],
        out_specs=pl.BlockSpec((tm, tn), lambda i, j: (i, j)),
        out_shape=jax.ShapeDtypeStruct((M, N), f32),
        scratch_shapes=[pltpu.VMEM((tm, D), bf16)],
        compiler_params=_params(("parallel", "arbitrary"), 48),
        name="norm_matmul",
    )(x, g.reshape(1, D), w)


def _out_proj(ya, yb, w, res, *, tm, tn):
    M, Ka = ya.shape
    Kb = yb.shape[1]
    N = w.shape[1]
    assert Ka == Kb and w.shape[0] == Ka + Kb

    def body(ya_ref, yb_ref, wa_ref, wb_ref, r_ref, o_ref):
        o_ref[...] = (r_ref[...] + _dot(ya_ref[...], wa_ref[...].astype(bf16))
                      + _dot(yb_ref[...], wb_ref[...].astype(bf16)))

    return pl.pallas_call(
        body,
        grid=(M // tm, N // tn),
        in_specs=[pl.BlockSpec((tm, Ka), lambda i, j: (i, 0)),
                  pl.BlockSpec((tm, Kb), lambda i, j: (i, 0)),
                  pl.BlockSpec((Ka, tn), lambda i, j: (0, j)),
                  pl.BlockSpec((Kb, tn), lambda i, j: (1, j)),
                  pl.BlockSpec((tm, tn), lambda i, j: (i, j))],
        out_specs=pl.BlockSpec((tm, tn), lambda i, j: (i, j)),
        out_shape=jax.ShapeDtypeStruct((M, N), f32),
        compiler_params=_params(("parallel", "arbitrary"), 48),
        name="out_proj",
    )(ya, yb, w, w, res)


def _ffn_tiles(F):
    nt = F // LANES
    nsteps = (nt + 1) // 2
    ta = [2 * j for j in range(nsteps)]
    tb = [min(2 * j + 1, nt - 1) for j in range(nsteps)]
    return nt, nsteps, ta, tb


def _ffn_col_index(F):
    nt, nsteps, ta, tb = _ffn_tiles(F)
    lane = jnp.arange(LANES)
    rows = []
    for j in range(nsteps):
        rows.append(jnp.concatenate([ta[j] * LANES + lane, tb[j] * LANES + lane,
                                     F + ta[j] * LANES + lane, F + tb[j] * LANES + lane]))
    return jnp.stack(rows)


def _ffn_uncol_index(F):
    nt, nsteps, ta, tb = _ffn_tiles(F)
    pos = [0] * (2 * nt)
    for j in range(nsteps):
        pos[ta[j]] = j * 4 + 0
        pos[nt + ta[j]] = j * 4 + 2
        if 2 * j + 1 <= nt - 1:
            pos[tb[j]] = j * 4 + 1
            pos[nt + tb[j]] = j * 4 + 3
    lane = jnp.arange(LANES)
    return jnp.concatenate([p * LANES + lane for p in pos])


def _conv_ffn(x, g, w_up, conv_w, conv_b, w_down, *, seq_len, tm, prev=None, final_g=None):
    M, D = x.shape
    F = w_down.shape[0]
    nt, nsteps, ta, tb = _ffn_tiles(F)
    odd_tail = nt % 2 == 1
    sample = prev is not None
    rc = min(tm, 512)
    col_idx = _ffn_col_index(F)
    cw_steps = jnp.transpose(conv_w[:, col_idx], (1, 0, 2))
    cb_steps = conv_b[col_idx][:, None, :]
    tiles_per_seq = max(seq_len // tm, 1)
    W4 = 4 * LANES

    def body(*refs):
        it = iter(refs)
        x_ref = next(it)
        xh_ref = None if sample else next(it)
        prev_ref = next(it) if sample else None
        g_ref = next(it)
        ga_ref, gb_ref, va_ref, vb_ref = next(it), next(it), next(it), next(it)
        cw_ref, cb_ref = next(it), next(it)
        da_ref, db_ref = next(it), next(it)
        fg_ref = next(it) if final_g is not None else None
        o_ref, tail_ref = next(it), next(it)
        xn_ref, wc_ref, wd_ref = next(it), next(it), next(it)
        i = pl.program_id(0)
        j = pl.program_id(1)

        @pl.when(j == 0)
        def _():
            if sample:
                xn_ref[0:HALO, :] = jnp.zeros((HALO, D), bf16)
            else:
                keep = (i % tiles_per_seq != 0).astype(f32)
                hist = _rms(xh_ref[...], g_ref[...]) * keep
                xn_ref[0:HALO, :] = jnp.concatenate([jnp.zeros_like(hist), hist], axis=0).astype(bf16)

            def chunk(r, c):
                rows = pl.ds(pl.multiple_of(r * rc, rc), rc)
                xr = x_ref[rows, :]
                o_ref[rows, :] = xr
                xn_ref[pl.ds(pl.multiple_of(HALO + r * rc, HALO), rc), :] = _rms(xr, g_ref[...]).astype(bf16)
                return c
            lax.fori_loop(0, tm // rc, chunk, 0)

        wc_ref[:, 0 * LANES:1 * LANES] = ga_ref[...].astype(bf16)
        wc_ref[:, 1 * LANES:2 * LANES] = gb_ref[...].astype(bf16)
        wc_ref[:, 2 * LANES:3 * LANES] = va_ref[...].astype(bf16)
        wc_ref[:, 3 * LANES:4 * LANES] = vb_ref[...].astype(bf16)
        wd_ref[0:LANES, :] = da_ref[...].astype(bf16)
        wd_ref[LANES:2 * LANES, :] = db_ref[...].astype(bf16)
        cw = cw_ref[0]
        cb = cb_ref[0]
        lane = lax.broadcasted_iota(jnp.int32, (1, 2 * LANES), 1)
        dup = jnp.logical_and(j == nsteps - 1, jnp.logical_and(lane >= LANES, odd_tail))

        def chunk(r, c):
            r0 = pl.multiple_of(r * rc, rc)
            rows = pl.ds(r0, rc)
            if sample:
                up = _dot(xn_ref[pl.ds(HALO + r0, rc), :], wc_ref[...])
                conv = cb + cw[0:1] * prev_ref[0, 0, rows, :] + cw[1:2] * prev_ref[0, 1, rows, :] + cw[2:3] * up
                tail_ref[0, rows, :] = up
            else:
                xx = _dot(xn_ref[pl.ds(r0, rc + HALO), :], wc_ref[...])
                up = xx[HALO:]
                conv = cb + cw[0:1] * xx[HALO - 2:HALO - 2 + rc] + cw[1:2] * xx[HALO - 1:HALO - 1 + rc] + cw[2:3] * up

                @pl.when(r == tm // rc - 1)
                def _():
                    tail_ref[0, 0] = up[rc - SUBLANES:]
            h = jax.nn.gelu(conv[:, :2 * LANES]) * conv[:, 2 * LANES:]
            h = jnp.where(dup, 0.0, h)
            o_ref[rows, :] += _dot(h.astype(bf16), wd_ref[...])
            return c
        lax.fori_loop(0, tm // rc, chunk, 0)

        if final_g is not None:
            @pl.when(j == nsteps - 1)
            def _():
                def chunk2(r, c):
                    rows = pl.ds(pl.multiple_of(r * rc, rc), rc)
                    o_ref[rows, :] = _rms(o_ref[rows, :], fg_ref[...])
                    return c
                lax.fori_loop(0, tm // rc, chunk2, 0)

    ta_arr = jnp.asarray(ta, jnp.int32)
    tb_arr = jnp.asarray(tb, jnp.int32)

    def tile_a(j):
        return 2 * j

    def tile_b(j):
        return jnp.minimum(2 * j + 1, nt - 1)

    in_specs = [pl.BlockSpec((tm, D), lambda i, j: (i, 0))]
    args = [x]
    if sample:
        prev_steps = jnp.transpose(prev[:, :, col_idx], (2, 0, 1, 3))
        in_specs.append(pl.BlockSpec((1, 2, tm, W4), lambda i, j: (j, 0, i, 0)))
        args.append(prev_steps)
    else:
        in_specs.append(pl.BlockSpec((SUBLANES, D), lambda i, j: (jnp.maximum(i * (tm // SUBLANES) - 1, 0), 0)))
        args.append(x)
    in_specs += [pl.BlockSpec((1, D), lambda i, j: (0, 0)),
                 pl.BlockSpec((D, LANES), lambda i, j: (0, tile_a(j))),
                 pl.BlockSpec((D, LANES), lambda i, j: (0, tile_b(j))),
                 pl.BlockSpec((D, LANES), lambda i, j: (0, nt + tile_a(j))),
                 pl.BlockSpec((D, LANES), lambda i, j: (0, nt + tile_b(j))),
                 pl.BlockSpec((1, 3, W4), lambda i, j: (j, 0, 0)),
                 pl.BlockSpec((1, 1, W4), lambda i, j: (j, 0, 0)),
                 pl.BlockSpec((LANES, D), lambda i, j: (tile_a(j), 0)),
                 pl.BlockSpec((LANES, D), lambda i, j: (tile_b(j), 0))]
    args += [g.reshape(1, D), w_up, w_up, w_up, w_up, cw_steps, cb_steps, w_down, w_down]
    if final_g is not None:
        in_specs.append(pl.BlockSpec((1, D), lambda i, j: (0, 0)))
        args.append(final_g.reshape(1, D))
    del ta_arr, tb_arr
    if sample:
        tail_shape = (nsteps, M, W4)
        tail_spec = pl.BlockSpec((1, tm, W4), lambda i, j: (j, i, 0))
    else:
        tail_shape = (M // tm, nsteps, SUBLANES, W4)
        tail_spec = pl.BlockSpec((1, 1, SUBLANES, W4), lambda i, j: (i, j, 0, 0))
    out, tail = pl.pallas_call(
        body,
        grid=(M // tm, nsteps),
        in_specs=in_specs,
        out_specs=[pl.BlockSpec((tm, D), lambda i, j: (i, 0)), tail_spec],
        out_shape=[jax.ShapeDtypeStruct((M, D), f32), jax.ShapeDtypeStruct(tail_shape, f32)],
        scratch_shapes=[pltpu.VMEM((tm + HALO, D), bf16),
                        pltpu.VMEM((D, W4), bf16),
                        pltpu.VMEM((2 * LANES, D), bf16)],
        compiler_params=_params(("parallel", "arbitrary"), 56),
        name="conv_ffn_sample" if sample else "conv_ffn",
    )(*args)
    uncol = _ffn_uncol_index(F)
    if sample:
        up_rows = jnp.transpose(tail, (1, 0, 2)).reshape(M, nsteps * W4)[:, uncol]
        new_buf = jnp.stack([prev[1], up_rows], axis=1)
    else:
        nseq = M // seq_len
        last = tail.reshape(nseq, tiles_per_seq, nsteps, SUBLANES, W4)[:, -1, :, SUBLANES - 2:, :]
        new_buf = jnp.transpose(last, (0, 2, 1, 3)).reshape(nseq, 2, nsteps * W4)[:, :, uncol]
    return out, new_buf


def _gla_prompt(p0, alr, w2p, b_alpha, g_norm, *, B, L, H, DK, DV, tb):
    NT = L // tb
    NC = tb // CHUNK
    C = CHUNK
    scale = DK ** -0.5
    kv = DV // DK
    qk_blocks = H
    v_off = 2 * H * DK // DV

    def body(q_ref, k_ref, v_ref, r_ref, a_ref, w2_ref, ba_ref, gn_ref, y_ref, s_out_ref, s_scr):
        t = pl.program_id(2)

        @pl.when(t == 0)
        def _():
            s_scr[...] = jnp.zeros_like(s_scr)

        w2 = w2_ref[...].astype(bf16)
        ii = lax.broadcasted_iota(jnp.int32, (C, C), 0)
        jj = lax.broadcasted_iota(jnp.int32, (C, C), 1)
        causal = ii >= jj
        tril = causal.astype(f32)

        def chunk(c, carry):
            rows = pl.ds(pl.multiple_of(c * C, C), C)
            z = _dot(a_ref[rows, :].astype(bf16), w2) + ba_ref[...]
            gl = jax.nn.log_sigmoid(z) * (1.0 / GLA_TAU)
            bc = jnp.dot(tril, gl, precision=HIGHEST, preferred_element_type=f32)
            q = q_ref[rows, :] * scale
            k = k_ref[rows, :]
            v = v_ref[rows, :].astype(bf16)
            qd = (q * jnp.exp(bc)).astype(bf16)
            kd = (k * jnp.exp(-bc)).astype(bf16)
            att = jnp.where(causal, _dot_nt(qd, kd), 0.0)
            s = s_scr[...]
            o = _dot(qd, s.astype(bf16)) + _dot(att.astype(bf16), v)
            bl = bc[C - 1:C, :]
            kdec = (k * jnp.exp(bl - bc)).astype(bf16)
            s_scr[...] = _col_from_row(jnp.exp(bl)) * s + _dot_tn(kdec, v)
            rr = r_ref[rows, :]
            y_ref[rows, :] = (_rms(o, gn_ref[...]) * (rr * jax.nn.sigmoid(rr))).astype(bf16)
            return carry
        lax.fori_loop(0, NC, chunk, 0)

        @pl.when(t == NT - 1)
        def _():
            s_out_ref[0, 0] = s_scr[...]

    del kv
    return pl.pallas_call(
        body,
        grid=(B, H, NT),
        in_specs=[pl.BlockSpec((tb, DK), lambda b, h, t: (b * NT + t, h)),
                  pl.BlockSpec((tb, DK), lambda b, h, t: (b * NT + t, qk_blocks + h)),
                  pl.BlockSpec((tb, DV), lambda b, h, t: (b * NT + t, v_off + h)),
                  pl.BlockSpec((tb, DV), lambda b, h, t: (b * NT + t, v_off + H + h)),
                  pl.BlockSpec((tb, LANES), lambda b, h, t: (b * NT + t, 0)),
                  pl.BlockSpec((LANES, DK), lambda b, h, t: (0, h)),
                  pl.BlockSpec((1, DK), lambda b, h, t: (0, h)),
                  pl.BlockSpec((1, DV), lambda b, h, t: (0, h))],
        out_specs=[pl.BlockSpec((tb, DV), lambda b, h, t: (b * NT + t, h)),
                   pl.BlockSpec((1, 1, DK, DV), lambda b, h, t: (b, h, 0, 0))],
        out_shape=[jax.ShapeDtypeStruct((B * L, H * DV), bf16),
                   jax.ShapeDtypeStruct((B, H, DK, DV), f32)],
        scratch_shapes=[pltpu.VMEM((DK, DV), f32)],
        compiler_params=_params(("parallel", "parallel", "arbitrary"), 32),
        name="gla_prompt",
    )(p0, p0, p0, p0, alr, w2p, b_alpha.reshape(1, -1), g_norm.reshape(1, -1))


def _gla_sample(p0, alr, w2p, b_alpha, g_norm, s0, *, H, DK, DV):
    Bs = p0.shape[0]
    scale = DK ** -0.5
    qkw = H * DK
    vw = H * DV
    assert vw % qkw == 0
    p3 = p0.reshape(Bs, 1, p0.shape[1])

    def body(q_ref, k_ref, v_ref, r_ref, a_ref, w2_ref, ba_ref, gn_ref, s_ref, y_ref, so_ref):
        a8 = jnp.broadcast_to(a_ref[...], (SUBLANES, LANES)).astype(bf16)
        z = _dot(a8, w2_ref[...].astype(bf16))[0:1] + ba_ref[...]
        gl = jax.nn.log_sigmoid(z) * (1.0 / GLA_TAU)
        for h in range(H):
            ks = slice(h * DK, (h + 1) * DK)
            vs = slice(h * DV, (h + 1) * DV)
            a_col = _col_from_row(jnp.exp(gl[:, ks]))
            k_col = _col_from_row(k_ref[:, ks])
            q_col = _col_from_row(q_ref[:, ks] * scale)
            sn = a_col * s_ref[h] + k_col * v_ref[:, vs]
            so_ref[h] = sn
            o = jnp.sum(q_col * sn, axis=0, keepdims=True)
            rr = r_ref[:, vs]
            y_ref[:, vs] = (_rms(o, gn_ref[:, vs]) * (rr * jax.nn.sigmoid(rr))).astype(bf16)

    v_blk = 2 * qkw // vw
    y, s_new = pl.pallas_call(
        body,
        grid=(Bs,),
        in_specs=[pl.BlockSpec((None, 1, qkw), lambda b: (b, 0, 0)),
                  pl.BlockSpec((None, 1, qkw), lambda b: (b, 0, 1)),
                  pl.BlockSpec((None, 1, vw), lambda b: (b, 0, v_blk)),
                  pl.BlockSpec((None, 1, vw), lambda b: (b, 0, v_blk + 1)),
                  pl.BlockSpec((None, 1, LANES), lambda b: (b, 0, 0)),
                  pl.BlockSpec((LANES, qkw), lambda b: (0, 0)),
                  pl.BlockSpec((1, qkw), lambda b: (0, 0)),
                  pl.BlockSpec((1, vw), lambda b: (0, 0)),
                  pl.BlockSpec((None, H, DK, DV), lambda b: (b, 0, 0, 0))],
        out_specs=[pl.BlockSpec((None, 1, vw), lambda b: (b, 0, 0)),
                   pl.BlockSpec((None, H, DK, DV), lambda b: (b, 0, 0, 0))],
        out_shape=[jax.ShapeDtypeStruct((Bs, 1, vw), bf16),
                   jax.ShapeDtypeStruct((Bs, H, DK, DV), f32)],
        compiler_params=_params(("parallel",), 32),
        name="gla_sample",
    )(p3, p3, p3, p3, alr.reshape(Bs, 1, LANES), w2p, b_alpha.reshape(1, -1), g_norm.reshape(1, -1), s0)
    return y.reshape(Bs, vw), s_new


def _rg_gates(xc, wr, br, wi, bi, sp):
    xb = xc.astype(bf16)
    r = jax.nn.sigmoid(_dot(xb, wr) + br)
    i = jax.nn.sigmoid(_dot(xb, wi) + bi)
    log_a = -RG_C * r * sp
    a = jnp.exp(log_a)
    mult = jnp.sqrt(1.0 - jnp.exp(2.0 * log_a))
    return a, mult, i


def _rglru_prompt(p0, conv_w, conv_b, w_r, b_r, w_i, b_i, lam, *, B, L, x_blk, g_blk):
    M = B * L
    NB, BS, _ = w_r.shape
    assert BS == LANES
    rc = min(256, L)

    def body(x_ref, gg_ref, cw_ref, cb_ref, wr_ref, br_ref, wi_ref, bi_ref, lam_ref, y_ref, hl_ref, a_scr, b_scr):
        wr = wr_ref[...].astype(bf16)
        wi = wi_ref[...].astype(bf16)
        sp = jax.nn.softplus(-lam_ref[...])
        cw = cw_ref[...]

        def chunk(c, carry):
            r0 = pl.multiple_of(c * rc, rc)
            rows = pl.ds(r0, rc)
            x = x_ref[rows, :]
            start = (r0 % L) == 0
            prev = x_ref[pl.ds(pl.multiple_of(jnp.maximum(r0 - SUBLANES, 0), SUBLANES), SUBLANES), :]
            prev = jnp.where(start, 0.0, prev)
            xc = (cb_ref[...] + cw[0:1] * _shifted(prev, x, 3) + cw[1:2] * _shifted(prev, x, 2)
                  + cw[2:3] * _shifted(prev, x, 1) + cw[3:4] * x)
            a, mult, ig = _rg_gates(xc, wr, br_ref[...], wi, bi_ref[...], sp)
            pos = (r0 + lax.broadcasted_iota(jnp.int32, (rc, 1), 0)) % L
            mult = jnp.where(pos == 0, 1.0, mult)
            a_scr[rows, :] = a
            b_scr[rows, :] = mult * (ig * xc)
            return carry
        lax.fori_loop(0, M // rc, chunk, 0)

        def step(t, hs):
            new = []
            for b in range(B):
                row = pl.ds(b * L + t, 1)
                h = a_scr[row, :] * hs[b] + b_scr[row, :]
                b_scr[row, :] = h
                new.append(h)
            return tuple(new)
        hs = lax.fori_loop(0, L, step, tuple(jnp.zeros((1, LANES), f32) for _ in range(B)))
        hl_ref[...] = jnp.concatenate(hs, axis=0)

        def outc(c, carry):
            rows = pl.ds(pl.multiple_of(c * rc, rc), rc)
            y_ref[rows, :] = (b_scr[rows, :] * jax.nn.gelu(gg_ref[rows, :])).astype(bf16)
            return carry
        lax.fori_loop(0, M // rc, outc, 0)

    W = NB * BS
    return pl.pallas_call(
        body,
        grid=(NB,),
        in_specs=[pl.BlockSpec((M, LANES), lambda n: (0, x_blk + n)),
                  pl.BlockSpec((M, LANES), lambda n: (0, g_blk + n)),
                  pl.BlockSpec((4, LANES), lambda n: (0, n)),
                  pl.BlockSpec((1, LANES), lambda n: (0, n)),
                  pl.BlockSpec((None, BS, BS), lambda n: (n, 0, 0)),
                  pl.BlockSpec((1, LANES), lambda n: (0, n)),
                  pl.BlockSpec((None, BS, BS), lambda n: (n, 0, 0)),
                  pl.BlockSpec((1, LANES), lambda n: (0, n)),
                  pl.BlockSpec((1, LANES), lambda n: (0, n))],
        out_specs=[pl.BlockSpec((M, LANES), lambda n: (0, n)),
                   pl.BlockSpec((B, LANES), lambda n: (0, n))],
        out_shape=[jax.ShapeDtypeStruct((M, W), bf16), jax.ShapeDtypeStruct((B, W), f32)],
        scratch_shapes=[pltpu.VMEM((M, LANES), f32), pltpu.VMEM((M, LANES), f32)],
        compiler_params=_params(("parallel",), 48),
        name="rglru_prompt",
    )(p0, p0, conv_w, conv_b.reshape(1, W), w_r, b_r.reshape(1, W), w_i, b_i.reshape(1, W), lam.reshape(1, W))


def _rglru_sample(p0, conv_state, h0, conv_w, conv_b, w_r, b_r, w_i, b_i, lam, *, x_blk, g_blk):
    Bs = p0.shape[0]
    NB, BS, _ = w_r.shape
    W = NB * BS
    s0, s1, s2 = conv_state[:, 0], conv_state[:, 1], conv_state[:, 2]

    def body(x_ref, gg_ref, s0_ref, s1_ref, s2_ref, h0_ref, cw_ref, cb_ref, wr_ref, br_ref, wi_ref, bi_ref, lam_ref,
             y_ref, h_ref):
        cw = cw_ref[...]
        x = x_ref[...]
        xc = cb_ref[...] + cw[0:1] * s0_ref[...] + cw[1:2] * s1_ref[...] + cw[2:3] * s2_ref[...] + cw[3:4] * x
        sp = jax.nn.softplus(-lam_ref[...])
        a, mult, ig = _rg_gates(xc, wr_ref[...].astype(bf16), br_ref[...], wi_ref[...].astype(bf16), bi_ref[...], sp)
        if PAST_LEN == 0:
            mult = jnp.ones_like(mult)
        h = a * h0_ref[...] + mult * (ig * xc)
        h_ref[...] = h
        y_ref[...] = (h * jax.nn.gelu(gg_ref[...])).astype(bf16)

    blk = lambda n: (0, n)
    vec = pl.BlockSpec((1, LANES), blk)
    mat = pl.BlockSpec((Bs, LANES), blk)
    y, h = pl.pallas_call(
        body,
        grid=(NB,),
        in_specs=[pl.BlockSpec((Bs, LANES), lambda n: (0, x_blk + n)),
                  pl.BlockSpec((Bs, LANES), lambda n: (0, g_blk + n)),
                  mat, mat, mat, mat,
                  pl.BlockSpec((4, LANES), blk), vec,
                  pl.BlockSpec((None, BS, BS), lambda n: (n, 0, 0)), vec,
                  pl.BlockSpec((None, BS, BS), lambda n: (n, 0, 0)), vec, vec],
        out_specs=[mat, mat],
        out_shape=[jax.ShapeDtypeStruct((Bs, W), bf16), jax.ShapeDtypeStruct((Bs, W), f32)],
        compiler_params=_params(("parallel",), 32),
        name="rglru_sample",
    )(p0, p0, s0, s1, s2, h0, conv_w, conv_b.reshape(1, W), w_r, b_r.reshape(1, W), w_i, b_i.reshape(1, W),
      lam.reshape(1, W))
    return y, h


def _blockdiag_tiles(w, tile):
    nblk, bs, _ = w.shape
    per = tile // bs
    nt = nblk // per
    w4 = w.reshape(nt, per, bs, bs)
    eye = jnp.eye(per, dtype=w.dtype)
    return jnp.einsum('tncd,nm->tncmd', w4, eye).reshape(nt, tile, tile)


def _mlstm_stage1(p1, conv_w, conv_b, wq_t, wk_t, wv_t, wg, bg, *, seq_len, tr, H, DH, conv_state=None):
    M = p1.shape[0]
    W = H * DH
    NTL, TL, _ = wq_t.shape
    sample = conv_state is not None
    tiles_per_seq = max(seq_len // tr, 1)
    kscale = DH ** -0.5

    def body(*refs):
        it = iter(refs)
        x_ref = next(it)
        if sample:
            s0_ref, s1_ref, s2_ref = next(it), next(it), next(it)
        else:
            xh_ref = next(it)
        cw_ref, cb_ref, wq_ref, wk_ref, wv_ref, wg_ref, bg_ref = (next(it) for _ in range(7))
        q_ref, k_ref, v_ref, g_ref, xc_ref = (next(it) for _ in range(5))
        cw = cw_ref[...]
        x = x_ref[...]
        if sample:
            conv = cb_ref[...] + cw[0:1] * s0_ref[...] + cw[1:2] * s1_ref[...] + cw[2:3] * s2_ref[...] + cw[3:4] * x
        else:
            keep = (pl.program_id(0) % tiles_per_seq != 0).astype(f32)
            prev = xh_ref[...] * keep
            conv = (cb_ref[...] + cw[0:1] * _shifted(prev, x, 3) + cw[1:2] * _shifted(prev, x, 2)
                    + cw[2:3] * _shifted(prev, x, 1) + cw[3:4] * x)
        xc = conv * jax.nn.sigmoid(conv)
        xc_ref[...] = xc
        xcb = xc.astype(bf16)
        xb = x.astype(bf16)
        qs, ks, vs = [], [], []
        for t in range(NTL):
            cs = slice(t * TL, (t + 1) * TL)
            qs.append(_dot(xcb[:, cs], wq_ref[t]).astype(bf16))
            ks.append((_dot(xcb[:, cs], wk_ref[t]) * kscale).astype(bf16))
            vs.append(_dot(xb[:, cs], wv_ref[t]).astype(bf16))
        q = jnp.concatenate(qs, axis=1)
        k = jnp.concatenate(ks, axis=1)
        v = jnp.concatenate(vs, axis=1)
        q_ref[...] = q
        k_ref[...] = k
        v_ref[...] = v
        gt = _dot(q, wg_ref[0:W, :]) + _dot(k, wg_ref[W:2 * W, :]) + _dot(v, wg_ref[2 * W:3 * W, :]) + bg_ref[...]
        lane = lax.broadcasted_iota(jnp.int32, gt.shape, 1)
        g_ref[...] = jnp.where(jnp.logical_and(lane >= H, lane < 2 * H), jax.nn.log_sigmoid(gt), gt)

    row = lambda i: (i, 0)
    const2 = lambda i: (0, 0)
    const3 = lambda i: (0, 0, 0)
    in_specs = [pl.BlockSpec((tr, W), row)]
    args = [p1]
    if sample:
        in_specs += [pl.BlockSpec((tr, W), row)] * 3
        args += [conv_state[:, 0], conv_state[:, 1], conv_state[:, 2]]
    else:
        in_specs.append(pl.BlockSpec((SUBLANES, W), lambda i: (jnp.maximum(i * (tr // SUBLANES) - 1, 0), 0)))
        args.append(p1)
    in_specs += [pl.BlockSpec((4, W), const2), pl.BlockSpec((1, W), const2),
                 pl.BlockSpec((NTL, TL, TL), const3), pl.BlockSpec((NTL, TL, TL), const3),
                 pl.BlockSpec((NTL, TL, TL), const3),
                 pl.BlockSpec((3 * W, LANES), const2), pl.BlockSpec((1, LANES), const2)]
    args += [conv_w, conv_b.reshape(1, W), wq_t, wk_t, wv_t, wg, bg]
    return pl.pallas_call(
        body,
        grid=(M // tr,),
        in_specs=in_specs,
        out_specs=[pl.BlockSpec((tr, W), row)] * 3 + [pl.BlockSpec((tr, LANES), row), pl.BlockSpec((tr, W), row)],
        out_shape=[jax.ShapeDtypeStruct((M, W), bf16)] * 3
        + [jax.ShapeDtypeStruct((M, LANES), f32), jax.ShapeDtypeStruct((M, W), f32)],
        compiler_params=_params(("parallel",), 48),
        name="mlstm_stage1_sample" if sample else "mlstm_stage1",
    )(*args)


def _mlstm_prompt(q, k, v, gates, xc, p1, g_norm, skip, *, B, L, H, DH, tb):
    NT = L // tb
    NC = tb // CHUNK
    C = CHUNK

    def body(q_ref, k_ref, v_ref, g_ref, xc_ref, om_ref, gn_ref, sk_ref,
             y_ref, c_out, n_out, m_out, c_scr, n_scr, m_scr):
        hh = pl.program_id(1)
        t = pl.program_id(2)

        @pl.when(t == 0)
        def _():
            c_scr[...] = jnp.zeros_like(c_scr)
            n_scr[...] = jnp.zeros_like(n_scr)
            m_scr[...] = jnp.zeros_like(m_scr)

        ii = lax.broadcasted_iota(jnp.int32, (C, C), 0)
        jj = lax.broadcasted_iota(jnp.int32, (C, C), 1)
        causal = ii >= jj
        tril = causal.astype(f32)
        lane = lax.broadcasted_iota(jnp.int32, (C, LANES), 1)

        def chunk(c, carry):
            rows = pl.ds(pl.multiple_of(c * C, C), C)
            gts = g_ref[rows, :]
            gcum = jnp.dot(tril, gts, precision=HIGHEST, preferred_element_type=f32)
            i_col = jnp.sum(jnp.where(lane == hh, gts, 0.0), axis=1, keepdims=True)
            b_col = jnp.sum(jnp.where(lane == H + hh, gcum, 0.0), axis=1, keepdims=True)
            i_row = _row_from_col(i_col)
            b_row = _row_from_col(b_col)
            m_prev = m_scr[...]
            dmat = jnp.where(causal, b_col - b_row + i_row, -jnp.inf)
            inter = b_col + m_prev
            m_col = jnp.maximum(inter, jnp.max(dmat, axis=1, keepdims=True))
            g_col = jnp.exp(inter - m_col)
            qb = q_ref[rows, :]
            kb = k_ref[rows, :]
            vb = v_ref[rows, :]
            s = _dot_nt(qb, kb) * jnp.exp(dmat - m_col)
            cs = c_scr[...]
            ns = n_scr[...]
            num = g_col * _dot_nt(qb, cs.astype(bf16)) + _dot(s.astype(bf16), vb)
            den = g_col * jnp.sum(qb.astype(f32) * ns, axis=1, keepdims=True) + jnp.sum(s, axis=1, keepdims=True)
            hm = num / jnp.maximum(jnp.abs(den), jnp.exp(-m_col))
            m_new = m_col[C - 1:C, :]
            wk = jnp.exp(b_col[C - 1:C, :] - b_col + i_col - m_new)
            gc = jnp.exp(inter[C - 1:C, :] - m_new)
            kw = kb.astype(f32) * wk
            c_scr[...] = gc * cs + _dot_tn(vb, kw.astype(bf16))
            n_scr[...] = gc * ns + jnp.sum(kw, axis=0, keepdims=True)
            m_scr[...] = m_new
            y = (_rms(hm, gn_ref[...]) + sk_ref[...] * xc_ref[rows, :]) * jax.nn.sigmoid(om_ref[rows, :])
            y_ref[rows, :] = y.astype(bf16)
            return carry
        lax.fori_loop(0, NC, chunk, 0)

        @pl.when(t == NT - 1)
        def _():
            c_out[0, 0] = c_scr[...]
            n_out[0, 0] = n_scr[...]
            m_out[0, 0] = m_scr[...]

    blk = lambda b, h, t: (b * NT + t, h)
    W = H * DH
    y, c_new, n_new, m_new = pl.pallas_call(
        body,
        grid=(B, H, NT),
        in_specs=[pl.BlockSpec((tb, DH), blk), pl.BlockSpec((tb, DH), blk), pl.BlockSpec((tb, DH), blk),
                  pl.BlockSpec((tb, LANES), lambda b, h, t: (b * NT + t, 0)),
                  pl.BlockSpec((tb, DH), blk),
                  pl.BlockSpec((tb, DH), lambda b, h, t: (b * NT + t, H + h)),
                  pl.BlockSpec((1, DH), lambda b, h, t: (0, h)),
                  pl.BlockSpec((1, DH), lambda b, h, t: (0, h))],
        out_specs=[pl.BlockSpec((tb, DH), blk),
                   pl.BlockSpec((1, 1, DH, DH), lambda b, h, t: (b, h, 0, 0)),
                   pl.BlockSpec((1, 1, 1, DH), lambda b, h, t: (b, h, 0, 0)),
                   pl.BlockSpec((1, 1, 1, 1), lambda b, h, t: (b, h, 0, 0))],
        out_shape=[jax.ShapeDtypeStruct((B * L, W), bf16),
                   jax.ShapeDtypeStruct((B, H, DH, DH), f32),
                   jax.ShapeDtypeStruct((B, H, 1, DH), f32),
                   jax.ShapeDtypeStruct((B, H, 1, 1), f32)],
        scratch_shapes=[pltpu.VMEM((DH, DH), f32), pltpu.VMEM((1, DH), f32), pltpu.VMEM((1, 1), f32)],
        compiler_params=_params(("parallel", "parallel", "arbitrary"), 32),
        name="mlstm_prompt",
    )(q, k, v, gates, xc, p1, g_norm.reshape(1, W), skip.reshape(1, W))
    return y, c_new, n_new.reshape(B, H, DH), m_new.reshape(B, H)


def _mlstm_sample(q, k, v, gates, xc, p1, g_norm, skip, c0, n0, m0, *, H, DH):
    Bs = q.shape[0]
    W = H * DH
    r3 = lambda a: a.reshape(Bs, 1, a.shape[-1])

    def body(q_ref, k_ref, v_ref, g_ref, xc_ref, om_ref, gn_ref, sk_ref, c_ref, n_ref, m_ref,
             y_ref, c_out, n_out, m_out):
        gts = g_ref[...]
        for h in range(H):
            cs = slice(h * DH, (h + 1) * DH)
            ig = gts[:, h:h + 1]
            fg = gts[:, H + h:H + h + 1]
            inter = fg + m_ref[:, h:h + 1]
            m = jnp.maximum(inter, ig)
            g = jnp.exp(inter - m)
            qr = q_ref[:, cs].astype(f32)
            kw = k_ref[:, cs].astype(f32) * jnp.exp(ig - m)
            v_col = _col_from_row(v_ref[:, cs].astype(f32))
            cn = g * c_ref[h] + v_col * kw
            nn = g * n_ref[:, cs] + kw
            c_out[h] = cn
            n_out[:, cs] = nn
            m_out[:, h:h + 1] = m
            num = _row_from_col(jnp.sum(cn * qr, axis=1, keepdims=True))
            den = jnp.sum(nn * qr, axis=1, keepdims=True)
            hm = num / jnp.maximum(jnp.abs(den), jnp.exp(-m))
            y = (_rms(hm, gn_ref[:, cs]) + sk_ref[:, cs] * xc_ref[:, cs]) * jax.nn.sigmoid(om_ref[:, cs])
            y_ref[:, cs] = y.astype(bf16)

    per = lambda b: (b, 0, 0)
    const2 = lambda b: (0, 0)
    y, c_new, n_new, m_new = pl.pallas_call(
        body,
        grid=(Bs,),
        in_specs=[pl.BlockSpec((None, 1, W), per), pl.BlockSpec((None, 1, W), per), pl.BlockSpec((None, 1, W), per),
                  pl.BlockSpec((None, 1, LANES), per), pl.BlockSpec((None, 1, W), per),
                  pl.BlockSpec((None, 1, W), lambda b: (b, 0, 1)),
                  pl.BlockSpec((1, W), const2), pl.BlockSpec((1, W), const2),
                  pl.BlockSpec((None, H, DH, DH), lambda b: (b, 0, 0, 0)),
                  pl.BlockSpec((None, 1, W), per), pl.BlockSpec((None, 1, H), per)],
        out_specs=[pl.BlockSpec((None, 1, W), per),
                   pl.BlockSpec((None, H, DH, DH), lambda b: (b, 0, 0, 0)),
                   pl.BlockSpec((None, 1, W), per), pl.BlockSpec((None, 1, H), per)],
        out_shape=[jax.ShapeDtypeStruct((Bs, 1, W), bf16), jax.ShapeDtypeStruct((Bs, H, DH, DH), f32),
                   jax.ShapeDtypeStruct((Bs, 1, W), f32), jax.ShapeDtypeStruct((Bs, 1, H), f32)],
        compiler_params=_params(("parallel",), 32),
        name="mlstm_sample",
    )(r3(q), r3(k), r3(v), r3(gates), r3(xc), r3(p1), g_norm.reshape(1, W), skip.reshape(1, W),
      c0, n0.reshape(Bs, 1, W), m0.reshape(Bs, 1, H))
    return y.reshape(Bs, W), c_new, n_new.reshape(Bs, H, DH), m_new.reshape(Bs, H)


S5_TILES = 8


def _s5_layouts(lam_re, lam_im, log_dt, b_re, b_im, c_re, c_im):
    G, P = lam_re.shape
    GC = b_re.shape[2]
    T = S5_TILES
    gpt = G // T
    ns = G * P
    flat = lambda a: a.reshape(ns)
    ldt = jnp.broadcast_to(log_dt[:, None], (G, P))
    rows = [flat(a).reshape(T, 1, ns // T) for a in (lam_re, lam_im, ldt)]
    eye = jnp.eye(gpt, dtype=f32)
    bbd = [jnp.einsum('jgpc,gh->jgchp', a.reshape(T, gpt, P, GC), eye).reshape(T, gpt * GC, gpt * P) for a in (b_re, b_im)]
    cbd = [jnp.einsum('jgcp,gh->jgphc', a.reshape(T, gpt, GC, P), eye).reshape(T, gpt * P, gpt * GC) for a in (c_re, c_im)]
    return rows, bbd, cbd


def _s5_discretise(lre, lim, ldt):
    dt = jnp.exp(ldt)
    mag = jnp.exp(dt * lre)
    ar = mag * jnp.cos(dt * lim)
    ai = mag * jnp.sin(dt * lim)
    den = lre * lre + lim * lim
    cr = ((ar - 1.0) * lre + ai * lim) / den
    ci = (ai * lre - (ar - 1.0) * lim) / den
    return ar, ai, cr, ci


def _s5_mixer(p1, u_blk, rows, bbd, cbd, d_skip, w_glu, b_glu, *, B, L, tb, state=None):
    T = S5_TILES
    lre_r, lim_r, ldt_r = rows
    SW = lre_r.shape[2]
    CW = bbd[0].shape[1]
    W = T * CW
    NS = T * SW
    KT = SW // LANES
    sample = state is not None
    NT = 1 if sample else L // tb
    M = B * L

    def body(*refs):
        it = iter(refs)
        u_ref = next(it)
        if sample:
            x0r_ref, x0i_ref = next(it), next(it)
        lre_ref, lim_ref, ldt_ref = next(it), next(it), next(it)
        bre_ref, bim_ref, cre_ref, cim_ref = next(it), next(it), next(it), next(it)
        d_ref, wg_ref, bgl_ref = next(it), next(it), next(it)
        y_ref, xr_out, xi_out = next(it), next(it), next(it)
        bbr, bbi, cbr, cbi, wgb, ar_scr, ai_scr = (next(it) for _ in range(7))
        if not sample:
            sre, sim, xr_c, xi_c, yacc = (next(it) for _ in range(5))
        first = jnp.logical_and(pl.program_id(0) == 0, pl.program_id(1) == 0)

        @pl.when(first)
        def _():
            for j in range(T):
                ar, ai, cr, ci = _s5_discretise(lre_ref[j], lim_ref[j], ldt_ref[j])
                ar_scr[j] = ar
                ai_scr[j] = ai
                br = bre_ref[j]
                bi = bim_ref[j]
                bbr[j] = (cr * br - ci * bi).astype(bf16)
                bbi[j] = (cr * bi + ci * br).astype(bf16)
                cbr[j] = cre_ref[j].astype(bf16)
                cbi[j] = cim_ref[j].astype(bf16)
            wgb[...] = wg_ref[...].astype(bf16)

        u = u_ref[...]
        ub = u.astype(bf16)
        ys = []
        if sample:
            for j in range(T):
                cs = slice(j * SW, (j + 1) * SW)
                uj = ub[:, j * CW:(j + 1) * CW]
                ar = ar_scr[j]
                ai = ai_scr[j]
                x0r = x0r_ref[:, cs]
                x0i = x0i_ref[:, cs]
                xr = ar * x0r - ai * x0i + _dot(uj, bbr[j])
                xi = ar * x0i + ai * x0r + _dot(uj, bbi[j])
                xr_out[:, cs] = xr
                xi_out[:, cs] = xi
                ys.append(_dot(xr.astype(bf16), cbr[j]) - _dot(xi.astype(bf16), cbi[j]))
            y = jnp.concatenate(ys, axis=1)
        else:
            t = pl.program_id(1)

            @pl.when(t == 0)
            def _():
                xr_c[...] = jnp.zeros_like(xr_c)
                xi_c[...] = jnp.zeros_like(xi_c)

            for j in range(T):
                uj = ub[:, j * CW:(j + 1) * CW]
                r = _dot(uj, bbr[j])
                im = _dot(uj, bbi[j])
                for kk in range(KT):
                    sre[kk, pl.ds(j, tb, stride=T), :] = r[:, kk * LANES:(kk + 1) * LANES]
                    sim[kk, pl.ds(j, tb, stride=T), :] = im[:, kk * LANES:(kk + 1) * LANES]
            a_r = [jnp.concatenate([ar_scr[j][:, kk * LANES:(kk + 1) * LANES] for j in range(T)], axis=0) for kk in range(KT)]
            a_i = [jnp.concatenate([ai_scr[j][:, kk * LANES:(kk + 1) * LANES] for j in range(T)], axis=0) for kk in range(KT)]

            def step(s, carry):
                xr, xi = carry
                row = pl.ds(pl.multiple_of(s * T, T), T)
                nr, ni = [], []
                for kk in range(KT):
                    r_ = a_r[kk] * xr[kk] - a_i[kk] * xi[kk] + sre[kk, row, :]
                    i_ = a_r[kk] * xi[kk] + a_i[kk] * xr[kk] + sim[kk, row, :]
                    sre[kk, row, :] = r_
                    sim[kk, row, :] = i_
                    nr.append(r_)
                    ni.append(i_)
                return tuple(nr), tuple(ni)
            xr0 = tuple(xr_c[kk] for kk in range(KT))
            xi0 = tuple(xi_c[kk] for kk in range(KT))
            xr, xi = lax.fori_loop(0, tb, step, (xr0, xi0))
            for kk in range(KT):
                xr_c[kk] = xr[kk]
                xi_c[kk] = xi[kk]

            @pl.when(t == NT - 1)
            def _():
                for kk in range(KT):
                    xr_out[kk] = xr[kk]
                    xi_out[kk] = xi[kk]

            for j in range(T):
                xrj = jnp.concatenate([sre[kk, pl.ds(j, tb, stride=T), :] for kk in range(KT)], axis=1).astype(bf16)
                xij = jnp.concatenate([sim[kk, pl.ds(j, tb, stride=T), :] for kk in range(KT)], axis=1).astype(bf16)
                yacc[:, j * CW:(j + 1) * CW] = _dot(xrj, cbr[j]) - _dot(xij, cbi[j])
            y = yacc[...]
        ysk = jax.nn.gelu(y + d_ref[...] * u)
        z = _dot(ysk.astype(bf16), wgb[...]) + bgl_ref[...]
        y_ref[...] = (ysk * jax.nn.sigmoid(z)).astype(bf16)

    c3 = lambda b, t: (0, 0, 0)
    c2 = lambda b, t: (0, 0)
    in_specs = [pl.BlockSpec((tb, W), lambda b, t: (b * NT + t, u_blk))]
    args = [p1]
    if sample:
        in_specs += [pl.BlockSpec((tb, NS), lambda b, t: (b, 0))] * 2
        args += [state[0], state[1]]
    in_specs += [pl.BlockSpec((T, 1, SW), c3)] * 3
    in_specs += [pl.BlockSpec((T, CW, SW), c3)] * 2 + [pl.BlockSpec((T, SW, CW), c3)] * 2
    in_specs += [pl.BlockSpec((1, W), c2), pl.BlockSpec((W, W), c2), pl.BlockSpec((1, W), c2)]
    args += [lre_r, lim_r, ldt_r, bbd[0], bbd[1], cbd[0], cbd[1], d_skip.reshape(1, W), w_glu, b_glu.reshape(1, W)]
    scratch = [pltpu.VMEM((T, CW, SW), bf16), pltpu.VMEM((T, CW, SW), bf16),
               pltpu.VMEM((T, SW, CW), bf16), pltpu.VMEM((T, SW, CW), bf16),
               pltpu.VMEM((W, W), bf16), pltpu.VMEM((T, 1, SW), f32), pltpu.VMEM((T, 1, SW), f32)]
    if sample:
        grid = (M // tb, 1)
        st_spec = pl.BlockSpec((tb, NS), lambda b, t: (b, 0))
        st_shape = jax.ShapeDtypeStruct((M, NS), f32)
    else:
        grid = (B, NT)
        st_spec = pl.BlockSpec((None, KT, T, LANES), lambda b, t: (b, 0, 0, 0))
        st_shape = jax.ShapeDtypeStruct((B, KT, T, LANES), f32)
        scratch += [pltpu.VMEM((KT, tb * T, LANES), f32), pltpu.VMEM((KT, tb * T, LANES), f32),
                    pltpu.VMEM((KT, T, LANES), f32), pltpu.VMEM((KT, T, LANES), f32), pltpu.VMEM((tb, W), f32)]
    y, xr, xi = pl.pallas_call(
        body,
        grid=grid,
        in_specs=in_specs,
        out_specs=[pl.BlockSpec((tb, W), lambda b, t: (b * NT + t, 0)), st_spec, st_spec],
        out_shape=[jax.ShapeDtypeStruct((M, W), bf16), st_shape, st_shape],
        scratch_shapes=scratch,
        compiler_params=_params(("arbitrary", "arbitrary"), 56),
        name="s5_sample" if sample else "s5_prompt",
    )(*args)
    if not sample:
        xr = jnp.transpose(xr, (0, 2, 1, 3)).reshape(B, NS)
        xi = jnp.transpose(xi, (0, 2, 1, 3)).reshape(B, NS)
    return y, xr, xi


def _trunk(x3, st, w, *, sample):
    B, L, D = x3.shape
    M = B * L
    x = x3.reshape(M, D)
    H_g, DK, DV = st['gla_S'].shape[1:] if sample else w['gla_dims']
    H_m, DH = w['ml_dims']
    G, P = w['s5_dims']
    tm = M if sample else min(1024, L)
    tf = M if sample else min(512, L)
    tb = min(512, L)
    out = {}

    p0 = _norm_matmul(x, w['g_mix0'], w['w_in0_main'], tm=tm, tn=512)
    alr = _norm_matmul(x, w['g_mix0'], w['w_in0_alr'], tm=tm, tn=LANES)
    x_blk = (2 * H_g * DK + 2 * H_g * DV) // LANES
    W_rg = w['rg_lambda'].shape[0]
    g_blk = x_blk + W_rg // LANES
    rg_x = p0[:, x_blk * LANES:x_blk * LANES + W_rg]
    if sample:
        ya, out['gla_S'] = _gla_sample(p0, alr, w['gla_w2p'], w['gla_b_alpha'], w['gla_g_norm'], st['gla_S'],
                                       H=H_g, DK=DK, DV=DV)
        yb, out['rg_h'] = _rglru_sample(p0, st['rg_conv'], st['rg_h'], w['rg_conv_w'], w['rg_conv_b'], w['rg_w_r'],
                                        w['rg_b_r'], w['rg_w_i'], w['rg_b_i'], w['rg_lambda'], x_blk=x_blk, g_blk=g_blk)
        out['rg_conv'] = jnp.concatenate([st['rg_conv'][:, 1:], rg_x[:, None, :]], axis=1)
    else:
        ya, out['gla_S'] = _gla_prompt(p0, alr, w['gla_w2p'], w['gla_b_alpha'], w['gla_g_norm'],
                                       B=B, L=L, H=H_g, DK=DK, DV=DV, tb=tb)
        yb, out['rg_h'] = _rglru_prompt(p0, w['rg_conv_w'], w['rg_conv_b'], w['rg_w_r'], w['rg_b_r'], w['rg_w_i'],
                                        w['rg_b_i'], w['rg_lambda'], B=B, L=L, x_blk=x_blk, g_blk=g_blk)
        out['rg_conv'] = rg_x.reshape(B, L, W_rg)[:, L - 3:]
    x = _out_proj(ya, yb, w['w_out0'], x, tm=tm, tn=512)
    x, ffn0 = _conv_ffn(x, w['g_ffn'][0], w['ffn_w_up'][0], w['ffn_conv_w'][0], w['ffn_conv_b'][0], w['ffn_w_down'][0],
                        seq_len=L, tm=tf, prev=jnp.transpose(st['ffn_conv'][0], (1, 0, 2)) if sample else None)

    W_ml = H_m * DH
    p1 = _norm_matmul(x, w['g_mix1'], w['w_in1'], tm=tm, tn=512)
    xm = p1[:, :W_ml]
    q, k, v, gates, xc = _mlstm_stage1(p1, w['ml_conv_w'], w['ml_conv_b'], w['ml_wq_t'], w['ml_wk_t'], w['ml_wv_t'],
                                       w['ml_wg'], w['ml_bg'], seq_len=L, tr=M if sample else min(256, L), H=H_m, DH=DH,
                                       conv_state=st['ml_conv'] if sample else None)
    u_blk = 2 * W_ml // (G * w['s5_gc'])
    if sample:
        yc, out['ml_C'], out['ml_n'], out['ml_m'] = _mlstm_sample(
            q, k, v, gates, xc, p1, w['ml_g_norm'], w['ml_skip'], st['ml_C'], st['ml_n'], st['ml_m'], H=H_m, DH=DH)
        out['ml_conv'] = jnp.concatenate([st['ml_conv'][:, 1:], xm[:, None, :]], axis=1)
        yd, s5r, s5i = _s5_mixer(p1, u_blk, w['s5_rows'], w['s5_bbd'], w['s5_cbd'], w['s5_D'], w['s5_w_glu'],
                                 w['s5_b_glu'], B=B, L=1, tb=B,
                                 state=(st['s5_re'].reshape(B, G * P), st['s5_im'].reshape(B, G * P)))
    else:
        yc, out['ml_C'], out['ml_n'], out['ml_m'] = _mlstm_prompt(
            q, k, v, gates, xc, p1, w['ml_g_norm'], w['ml_skip'], B=B, L=L, H=H_m, DH=DH, tb=tb)
        out['ml_conv'] = xm.reshape(B, L, W_ml)[:, L - 3:]
        yd, s5r, s5i = _s5_mixer(p1, u_blk, w['s5_rows'], w['s5_bbd'], w['s5_cbd'], w['s5_D'], w['s5_w_glu'],
                                 w['s5_b_glu'], B=B, L=L, tb=min(256, L))
    out['s5_re'] = s5r.reshape(B, G, P)
    out['s5_im'] = s5i.reshape(B, G, P)
    x = _out_proj(yc, yd, w['w_out1'], x, tm=tm, tn=512)
    x, ffn1 = _conv_ffn(x, w['g_ffn'][1], w['ffn_w_up'][1], w['ffn_conv_w'][1], w['ffn_conv_b'][1], w['ffn_w_down'][1],
                        seq_len=L, tm=tf, prev=jnp.transpose(st['ffn_conv'][1], (1, 0, 2)) if sample else None,
                        final_g=w['g_final'])
    out['ffn_conv'] = jnp.stack([ffn0, ffn1], axis=0)
    return x.reshape(B, L, D), out


def kernel(x_prompt, x_sample, state_gla_S, state_rglru_h, state_rglru_conv, state_mlstm_C, state_mlstm_n, state_mlstm_m, state_mlstm_conv, state_s5_re, state_s5_im, state_ffn_conv, g_mix0, w_in0, gla_w_alpha2, gla_b_alpha, gla_g_norm, rg_conv_w, rg_conv_b, rg_w_r, rg_b_r, rg_w_i, rg_b_i, rg_lambda, w_out0, g_mix1, w_in1, ml_conv_w, ml_conv_b, ml_wq, ml_wk, ml_wv, ml_w_igate, ml_b_igate, ml_w_fgate, ml_b_fgate, ml_g_norm, ml_skip, s5_lam_re, s5_lam_im, s5_log_dt, s5_B_re, s5_B_im, s5_C_re, s5_C_im, s5_D, s5_w_glu, s5_b_glu, w_out1, g_ffn, ffn_w_up, ffn_conv_w, ffn_conv_b, ffn_w_down, g_final):
    _, H_g, DK, DV = state_gla_S.shape
    _, H_m, DH, _ = state_mlstm_C.shape
    G, P = s5_lam_re.shape
    rank = gla_w_alpha2.shape[0]
    n_main = 2 * H_g * DK + 2 * H_g * DV
    w_in0_main = jnp.concatenate([w_in0[:, :n_main], w_in0[:, n_main + rank:]], axis=1)
    w_in0_alr = jnp.pad(w_in0[:, n_main:n_main + rank], ((0, 0), (0, LANES - rank)))
    gla_w2p = jnp.pad(gla_w_alpha2, ((0, LANES - rank), (0, 0)))
    ml_tile = 256
    ml_wg = jnp.pad(jnp.concatenate([ml_w_igate, ml_w_fgate], axis=1), ((0, 0), (0, LANES - 2 * H_m))).astype(bf16)
    ml_bg = jnp.pad(jnp.concatenate([ml_b_igate, ml_b_fgate]), (0, LANES - 2 * H_m)).reshape(1, LANES)
    rows, bbd, cbd = _s5_layouts(s5_lam_re, s5_lam_im, s5_log_dt, s5_B_re, s5_B_im, s5_C_re, s5_C_im)
    w = dict(
        g_mix0=g_mix0, w_in0_main=w_in0_main, w_in0_alr=w_in0_alr, gla_w2p=gla_w2p, gla_b_alpha=gla_b_alpha,
        gla_g_norm=gla_g_norm, gla_dims=(H_g, DK, DV), rg_conv_w=rg_conv_w, rg_conv_b=rg_conv_b, rg_w_r=rg_w_r,
        rg_b_r=rg_b_r, rg_w_i=rg_w_i, rg_b_i=rg_b_i, rg_lambda=rg_lambda, w_out0=w_out0,
        g_mix1=g_mix1, w_in1=w_in1, ml_conv_w=ml_conv_w, ml_conv_b=ml_conv_b,
        ml_wq_t=_blockdiag_tiles(ml_wq, ml_tile).astype(bf16), ml_wk_t=_blockdiag_tiles(ml_wk, ml_tile).astype(bf16),
        ml_wv_t=_blockdiag_tiles(ml_wv, ml_tile).astype(bf16), ml_wg=ml_wg, ml_bg=ml_bg, ml_g_norm=ml_g_norm,
        ml_skip=ml_skip, ml_dims=(H_m, DH), s5_dims=(G, P), s5_gc=s5_B_re.shape[2], s5_rows=rows, s5_bbd=bbd,
        s5_cbd=cbd, s5_D=s5_D, s5_w_glu=s5_w_glu, s5_b_glu=s5_b_glu, w_out1=w_out1,
        g_ffn=g_ffn, ffn_w_up=ffn_w_up, ffn_conv_w=ffn_conv_w, ffn_conv_b=ffn_conv_b, ffn_w_down=ffn_w_down,
        g_final=g_final)
    st_s = dict(gla_S=state_gla_S, rg_h=state_rglru_h, rg_conv=state_rglru_conv, ml_C=state_mlstm_C,
                ml_n=state_mlstm_n, ml_m=state_mlstm_m, ml_conv=state_mlstm_conv, s5_re=state_s5_re,
                s5_im=state_s5_im, ffn_conv=state_ffn_conv)
    y_p, np_ = _trunk(x_prompt, None, w, sample=False)
    y_s, ns_ = _trunk(x_sample, st_s, w, sample=True)
    names = ('gla_S', 'rg_h', 'rg_conv', 'ml_C', 'ml_n', 'ml_m', 'ml_conv', 's5_re', 's5_im', 'ffn_conv')
    outs = [y_p, y_s]
    for nme in names:
        outs += [np_[nme], ns_[nme]]
    return tuple(outs)
```

```python
import functools

import jax
import jax.numpy as jnp
from jax import lax
from jax.experimental import pallas as pl
from jax.experimental.pallas import tpu as pltpu

f32 = jnp.float32
bf16 = jnp.bfloat16

EPS = 1e-6
CHUNK = 64
GLA_TAU = 16.0
RG_C = 8.0
PAST_LEN = 16384
LANES = 128
SUBLANES = 8
HALO = 16
MIB = 1024 * 1024
HIGHEST = lax.Precision.HIGHEST


def _params(sem, vmem_mib):
    return pltpu.CompilerParams(dimension_semantics=sem, vmem_limit_bytes=int(vmem_mib * MIB))


def _dot(a, b):
    return jnp.dot(a, b, preferred_element_type=f32)


def _dot_nt(a, b):
    return lax.dot_general(a, b, (((1,), (1,)), ((), ())), preferred_element_type=f32)


def _dot_tn(a, b):
    return lax.dot_general(a, b, (((0,), (0,)), ((), ())), preferred_element_type=f32)


def _rms(x, g):
    return x * lax.rsqrt(jnp.mean(x * x, axis=-1, keepdims=True) + EPS) * g


def _eye(n):
    return lax.broadcasted_iota(jnp.int32, (n, n), 0) == lax.broadcasted_iota(jnp.int32, (n, n), 1)


def _col_from_row(row):
    n = row.shape[1]
    return jnp.sum(jnp.where(_eye(n), jnp.broadcast_to(row, (n, n)), 0.0), axis=1, keepdims=True)


def _row_from_col(col):
    n = col.shape[0]
    return jnp.sum(jnp.where(_eye(n), jnp.broadcast_to(col, (n, n)), 0.0), axis=0, keepdims=True)


def _shifted(prev8, x, back):
    xx = jnp.concatenate([prev8, x], axis=0)
    n = x.shape[0]
    return xx[SUBLANES - back:SUBLANES - back + n]


def _norm_matmul(x, g, w, *, tm, tn):
    M, D = x.shape
    N = w.shape[1]
    rc = min(tm, 256)

    def body(x_ref, g_ref, w_ref, o_ref, xn_ref):
        @pl.when(pl.program_id(1) == 0)
        def _():
            def chunk(r, c):
                rows = pl.ds(pl.multiple_of(r * rc, rc), rc)
                xn_ref[rows, :] = _rms(x_ref[rows, :], g_ref[...]).astype(bf16)
                return c
            lax.fori_loop(0, tm // rc, chunk, 0)
        o_ref[...] = _dot(xn_ref[...], w_ref[...])

    return pl.pallas_call(
        body,
        grid=(M // tm, N // tn),
        in_specs=[pl.BlockSpec((tm, D), lambda i, j: (i, 0)),
                  pl.BlockSpec((1, D), lambda i, j: (0, 0)),
                  pl.BlockSpec((D, tn), lambda i, j: (0, j))],
        out_specs=pl.BlockSpec((tm, tn), lambda i, j: (i, j)),
        out_shape=jax.ShapeDtypeStruct((M, N), f32),
        scratch_shapes=[pltpu.VMEM((tm, D), bf16)],
        compiler_params=_params(("parallel", "arbitrary"), 48),
        name="norm_matmul",
    )(x, g.reshape(1, D), w)


def _out_proj(ya, yb, w, res, *, tm, tn):
    M, Ka = ya.shape
    Kb = yb.shape[1]
    N = w.shape[1]
    assert Ka == Kb and w.shape[0] == Ka + Kb

    def body(ya_ref, yb_ref, wa_ref, wb_ref, r_ref, o_ref):
        o_ref[...] = r_ref[...] + _dot(ya_ref[...], wa_ref[...]) + _dot(yb_ref[...], wb_ref[...])

    return pl.pallas_call(
        body,
        grid=(M // tm, N // tn),
        in_specs=[pl.BlockSpec((tm, Ka), lambda i, j: (i, 0)),
                  pl.BlockSpec((tm, Kb), lambda i, j: (i, 0)),
                  pl.BlockSpec((Ka, tn), lambda i, j: (0, j)),
                  pl.BlockSpec((Kb, tn), lambda i, j: (1, j)),
                  pl.BlockSpec((tm, tn), lambda i, j: (i, j))],
        out_specs=pl.BlockSpec((tm, tn), lambda i, j: (i, j)),
        out_shape=jax.ShapeDtypeStruct((M, N), f32),
        compiler_params=_params(("parallel", "arbitrary"), 48),
        name="out_proj",
    )(ya, yb, w, w, res)


FFN_STEP = 256


def _ffn_to_steps(a, F):
    nsteps = -(-F // FFN_STEP)
    lead = a.shape[:-1]
    nd = len(lead)
    gv = a.reshape(lead + (2, F))
    gv = jnp.pad(gv, [(0, 0)] * (nd + 1) + [(0, nsteps * FFN_STEP - F)])
    gv = gv.reshape(lead + (2, nsteps, FFN_STEP))
    gv = jnp.transpose(gv, (nd + 1,) + tuple(range(nd)) + (nd, nd + 2))
    return gv.reshape((nsteps,) + lead + (2 * FFN_STEP,))


def _ffn_from_steps(a, F):
    lead = a.shape[:-2]
    nsteps = a.shape[-2]
    gv = a.reshape(lead + (nsteps, 2, FFN_STEP))
    gv = jnp.swapaxes(gv, -3, -2).reshape(lead + (2, nsteps * FFN_STEP))[..., :F]
    return gv.reshape(lead + (2 * F,))


def _ffn_prepare(w_up, conv_w, conv_b, w_down):
    F, D = w_down.shape
    nsteps = -(-F // FFN_STEP)
    wu = _ffn_to_steps(w_up, F).astype(bf16)
    wd = jnp.pad(w_down, ((0, nsteps * FFN_STEP - F), (0, 0))).reshape(nsteps, FFN_STEP, D).astype(bf16)
    cw = _ffn_to_steps(conv_w, F)
    cb = _ffn_to_steps(conv_b, F).reshape(nsteps, 1, 2 * FFN_STEP)
    return dict(wu=wu, wd=wd, cw=cw, cb=cb, F=F)


def _conv_ffn(x, g, fw, *, seq_len, tm, prev=None, final_g=None):
    M, D = x.shape
    F = fw['F']
    nsteps = fw['wu'].shape[0]
    sample = prev is not None
    rc = min(tm, 512)
    tiles_per_seq = max(seq_len // tm, 1)
    W2 = 2 * FFN_STEP

    def body(*refs):
        it = iter(refs)
        x_ref = next(it)
        xh_ref = None if sample else next(it)
        prev_ref = next(it) if sample else None
        g_ref = next(it)
        wu_ref, cw_ref, cb_ref, wd_ref = next(it), next(it), next(it), next(it)
        fg_ref = next(it) if final_g is not None else None
        o_ref, tail_ref = next(it), next(it)
        xn_ref = next(it)
        i = pl.program_id(0)
        j = pl.program_id(1)

        @pl.when(j == 0)
        def _():
            if sample:
                xn_ref[0:HALO, :] = jnp.zeros((HALO, D), bf16)
            else:
                keep = (i % tiles_per_seq != 0).astype(f32)
                hist = _rms(xh_ref[...], g_ref[...]) * keep
                xn_ref[0:HALO, :] = jnp.concatenate([jnp.zeros_like(hist), hist], axis=0).astype(bf16)

            def chunk(r, c):
                rows = pl.ds(pl.multiple_of(r * rc, rc), rc)
                xr = x_ref[rows, :]
                o_ref[rows, :] = xr
                xn_ref[pl.ds(pl.multiple_of(HALO + r * rc, HALO), rc), :] = _rms(xr, g_ref[...]).astype(bf16)
                return c
            lax.fori_loop(0, tm // rc, chunk, 0)

        cw = cw_ref[0]
        cb = cb_ref[0]
        for r in range(tm // rc):
            r0 = r * rc
            rows = pl.ds(r0, rc)
            if sample:
                up = _dot(xn_ref[pl.ds(HALO + r0, rc), :], wu_ref[...])
                conv = cb + cw[0:1] * prev_ref[0, 0, rows, :] + cw[1:2] * prev_ref[0, 1, rows, :] + cw[2:3] * up
                tail_ref[0, rows, :] = up
            else:
                xx = _dot(xn_ref[pl.ds(r0, rc + HALO), :], wu_ref[...])
                up = xx[HALO:]
                conv = cb + cw[0:1] * xx[HALO - 2:HALO - 2 + rc] + cw[1:2] * xx[HALO - 1:HALO - 1 + rc] + cw[2:3] * up
                if r == tm // rc - 1:
                    tail_ref[0, 0] = up[rc - SUBLANES:]
            h = jax.nn.gelu(conv[:, :FFN_STEP]) * conv[:, FFN_STEP:]
            o_ref[rows, :] += _dot(h.astype(bf16), wd_ref[...])

        if final_g is not None:
            @pl.when(j == nsteps - 1)
            def _():
                def chunk2(r, c):
                    rows = pl.ds(pl.multiple_of(r * rc, rc), rc)
                    o_ref[rows, :] = _rms(o_ref[rows, :], fg_ref[...])
                    return c
                lax.fori_loop(0, tm // rc, chunk2, 0)

    in_specs = [pl.BlockSpec((tm, D), lambda i, j: (i, 0))]
    args = [x]
    if sample:
        in_specs.append(pl.BlockSpec((1, 2, tm, W2), lambda i, j: (j, 0, i, 0)))
        args.append(_ffn_to_steps(prev, F))
    else:
        in_specs.append(pl.BlockSpec((SUBLANES, D), lambda i, j: (jnp.maximum(i * (tm // SUBLANES) - 1, 0), 0)))
        args.append(x)
    in_specs += [pl.BlockSpec((1, D), lambda i, j: (0, 0)),
                 pl.BlockSpec((None, D, W2), lambda i, j: (j, 0, 0)),
                 pl.BlockSpec((1, 3, W2), lambda i, j: (j, 0, 0)),
                 pl.BlockSpec((1, 1, W2), lambda i, j: (j, 0, 0)),
                 pl.BlockSpec((None, FFN_STEP, D), lambda i, j: (j, 0, 0))]
    args += [g.reshape(1, D), fw['wu'], fw['cw'], fw['cb'], fw['wd']]
    if final_g is not None:
        in_specs.append(pl.BlockSpec((1, D), lambda i, j: (0, 0)))
        args.append(final_g.reshape(1, D))
    if sample:
        tail_shape = (nsteps, M, W2)
        tail_spec = pl.BlockSpec((1, tm, W2), lambda i, j: (j, i, 0))
    else:
        tail_shape = (M // tm, nsteps, SUBLANES, W2)
        tail_spec = pl.BlockSpec((1, 1, SUBLANES, W2), lambda i, j: (i, j, 0, 0))
    out, tail = pl.pallas_call(
        body,
        grid=(M // tm, nsteps),
        in_specs=in_specs,
        out_specs=[pl.BlockSpec((tm, D), lambda i, j: (i, 0)), tail_spec],
        out_shape=[jax.ShapeDtypeStruct((M, D), f32), jax.ShapeDtypeStruct(tail_shape, f32)],
        scratch_shapes=[pltpu.VMEM((tm + HALO, D), bf16)],
        compiler_params=_params(("parallel", "arbitrary"), 56),
        name="conv_ffn_sample" if sample else "conv_ffn",
    )(*args)
    if sample:
        up_rows = _ffn_from_steps(jnp.transpose(tail, (1, 0, 2)), F)
        new_buf = jnp.stack([prev[1], up_rows], axis=1)
    else:
        nseq = M // seq_len
        last = tail.reshape(nseq, tiles_per_seq, nsteps, SUBLANES, W2)[:, -1, :, SUBLANES - 2:, :]
        new_buf = _ffn_from_steps(jnp.transpose(last, (0, 2, 1, 3)), F)
    return out, new_buf


def _gla_prompt(p0, alr, w2p, b_alpha, g_norm, *, B, L, H, DK, DV, tb):
    NT = L // tb
    NC = tb // CHUNK
    C = CHUNK
    scale = DK ** -0.5
    kv = DV // DK
    qk_blocks = H
    v_off = 2 * H * DK // DV

    def body(q_ref, k_ref, v_ref, r_ref, a_ref, w2_ref, ba_ref, gn_ref, y_ref, s_out_ref, s_scr):
        t = pl.program_id(2)

        @pl.when(t == 0)
        def _():
            s_scr[...] = jnp.zeros_like(s_scr)

        w2 = w2_ref[...].astype(bf16)
        ii = lax.broadcasted_iota(jnp.int32, (C, C), 0)
        jj = lax.broadcasted_iota(jnp.int32, (C, C), 1)
        causal = ii >= jj
        tril = causal.astype(f32)

        def chunk(c, carry):
            rows = pl.ds(pl.multiple_of(c * C, C), C)
            z = _dot(a_ref[rows, :].astype(bf16), w2) + ba_ref[...]
            gl = jax.nn.log_sigmoid(z) * (1.0 / GLA_TAU)
            bc = jnp.dot(tril, gl, precision=HIGHEST, preferred_element_type=f32)
            q = q_ref[rows, :] * scale
            k = k_ref[rows, :]
            v = v_ref[rows, :].astype(bf16)
            qd = (q * jnp.exp(bc)).astype(bf16)
            kd = (k * jnp.exp(-bc)).astype(bf16)
            att = jnp.where(causal, _dot_nt(qd, kd), 0.0)
            s = s_scr[...]
            o = _dot(qd, s.astype(bf16)) + _dot(att.astype(bf16), v)
            bl = bc[C - 1:C, :]
            kdec = (k * jnp.exp(bl - bc)).astype(bf16)
            s_scr[...] = _col_from_row(jnp.exp(bl)) * s + _dot_tn(kdec, v)
            rr = r_ref[rows, :]
            y_ref[rows, :] = (_rms(o, gn_ref[...]) * (rr * jax.nn.sigmoid(rr))).astype(bf16)
            return carry
        lax.fori_loop(0, NC, chunk, 0)

        @pl.when(t == NT - 1)
        def _():
            s_out_ref[0, 0] = s_scr[...]

    del kv
    return pl.pallas_call(
        body,
        grid=(B, H, NT),
        in_specs=[pl.BlockSpec((tb, DK), lambda b, h, t: (b * NT + t, h)),
                  pl.BlockSpec((tb, DK), lambda b, h, t: (b * NT + t, qk_blocks + h)),
                  pl.BlockSpec((tb, DV), lambda b, h, t: (b * NT + t, v_off + h)),
                  pl.BlockSpec((tb, DV), lambda b, h, t: (b * NT + t, v_off + H + h)),
                  pl.BlockSpec((tb, LANES), lambda b, h, t: (b * NT + t, 0)),
                  pl.BlockSpec((LANES, DK), lambda b, h, t: (0, h)),
                  pl.BlockSpec((1, DK), lambda b, h, t: (0, h)),
                  pl.BlockSpec((1, DV), lambda b, h, t: (0, h))],
        out_specs=[pl.BlockSpec((tb, DV), lambda b, h, t: (b * NT + t, h)),
                   pl.BlockSpec((1, 1, DK, DV), lambda b, h, t: (b, h, 0, 0))],
        out_shape=[jax.ShapeDtypeStruct((B * L, H * DV), bf16),
                   jax.ShapeDtypeStruct((B, H, DK, DV), f32)],
        scratch_shapes=[pltpu.VMEM((DK, DV), f32)],
        compiler_params=_params(("parallel", "parallel", "arbitrary"), 32),
        name="gla_prompt",
    )(p0, p0, p0, p0, alr, w2p, b_alpha.reshape(1, -1), g_norm.reshape(1, -1))


def _gla_sample(p0, alr, w2p, b_alpha, g_norm, s0, *, H, DK, DV):
    Bs = p0.shape[0]
    scale = DK ** -0.5
    qkw = H * DK
    vw = H * DV
    assert vw % qkw == 0
    p3 = p0.reshape(Bs, 1, p0.shape[1])

    def body(q_ref, k_ref, v_ref, r_ref, a_ref, w2_ref, ba_ref, gn_ref, s_ref, y_ref, so_ref):
        a8 = jnp.broadcast_to(a_ref[...], (SUBLANES, LANES)).astype(bf16)
        z = _dot(a8, w2_ref[...].astype(bf16))[0:1] + ba_ref[...]
        gl = jax.nn.log_sigmoid(z) * (1.0 / GLA_TAU)
        for h in range(H):
            ks = slice(h * DK, (h + 1) * DK)
            vs = slice(h * DV, (h + 1) * DV)
            a_col = _col_from_row(jnp.exp(gl[:, ks]))
            k_col = _col_from_row(k_ref[:, ks])
            q_col = _col_from_row(q_ref[:, ks] * scale)
            sn = a_col * s_ref[h] + k_col * v_ref[:, vs]
            so_ref[h] = sn
            o = jnp.sum(q_col * sn, axis=0, keepdims=True)
            rr = r_ref[:, vs]
            y_ref[:, vs] = (_rms(o, gn_ref[:, vs]) * (rr * jax.nn.sigmoid(rr))).astype(bf16)

    v_blk = 2 * qkw // vw
    y, s_new = pl.pallas_call(
        body,
        grid=(Bs,),
        in_specs=[pl.BlockSpec((None, 1, qkw), lambda b: (b, 0, 0)),
                  pl.BlockSpec((None, 1, qkw), lambda b: (b, 0, 1)),
                  pl.BlockSpec((None, 1, vw), lambda b: (b, 0, v_blk)),
                  pl.BlockSpec((None, 1, vw), lambda b: (b, 0, v_blk + 1)),
                  pl.BlockSpec((None, 1, LANES), lambda b: (b, 0, 0)),
                  pl.BlockSpec((LANES, qkw), lambda b: (0, 0)),
                  pl.BlockSpec((1, qkw), lambda b: (0, 0)),
                  pl.BlockSpec((1, vw), lambda b: (0, 0)),
                  pl.BlockSpec((None, H, DK, DV), lambda b: (b, 0, 0, 0))],
        out_specs=[pl.BlockSpec((None, 1, vw), lambda b: (b, 0, 0)),
                   pl.BlockSpec((None, H, DK, DV), lambda b: (b, 0, 0, 0))],
        out_shape=[jax.ShapeDtypeStruct((Bs, 1, vw), bf16),
                   jax.ShapeDtypeStruct((Bs, H, DK, DV), f32)],
        compiler_params=_params(("parallel",), 32),
        name="gla_sample",
    )(p3, p3, p3, p3, alr.reshape(Bs, 1, LANES), w2p, b_alpha.reshape(1, -1), g_norm.reshape(1, -1), s0)
    return y.reshape(Bs, vw), s_new


def _rg_gates(xc, wr, br, wi, bi, sp):
    xb = xc.astype(bf16)
    r = jax.nn.sigmoid(_dot(xb, wr) + br)
    i = jax.nn.sigmoid(_dot(xb, wi) + bi)
    log_a = -RG_C * r * sp
    a = jnp.exp(log_a)
    mult = jnp.sqrt(1.0 - jnp.exp(2.0 * log_a))
    return a, mult, i


def _rglru_prompt(p0, conv_w, conv_b, w_r, b_r, w_i, b_i, lam, *, B, L, x_blk, g_blk):
    M = B * L
    NB, BS, _ = w_r.shape
    assert BS == LANES
    rc = min(256, L)

    def body(x_ref, gg_ref, cw_ref, cb_ref, wr_ref, br_ref, wi_ref, bi_ref, lam_ref, y_ref, hl_ref, a_scr, b_scr):
        wr = wr_ref[...].astype(bf16)
        wi = wi_ref[...].astype(bf16)
        sp = jax.nn.softplus(-lam_ref[...])
        cw = cw_ref[...]

        def chunk(c, carry):
            r0 = pl.multiple_of(c * rc, rc)
            rows = pl.ds(r0, rc)
            x = x_ref[rows, :]
            start = (r0 % L) == 0
            prev = x_ref[pl.ds(pl.multiple_of(jnp.maximum(r0 - SUBLANES, 0), SUBLANES), SUBLANES), :]
            prev = jnp.where(start, 0.0, prev)
            xc = (cb_ref[...] + cw[0:1] * _shifted(prev, x, 3) + cw[1:2] * _shifted(prev, x, 2)
                  + cw[2:3] * _shifted(prev, x, 1) + cw[3:4] * x)
            a, mult, ig = _rg_gates(xc, wr, br_ref[...], wi, bi_ref[...], sp)
            pos = (r0 + lax.broadcasted_iota(jnp.int32, (rc, 1), 0)) % L
            mult = jnp.where(pos == 0, 1.0, mult)
            a_scr[rows, :] = a
            b_scr[rows, :] = mult * (ig * xc)
            return carry
        lax.fori_loop(0, M // rc, chunk, 0)

        def step(t, hs):
            new = []
            for b in range(B):
                row = pl.ds(b * L + t, 1)
                h = a_scr[row, :] * hs[b] + b_scr[row, :]
                b_scr[row, :] = h
                new.append(h)
            return tuple(new)
        hs = lax.fori_loop(0, L, step, tuple(jnp.zeros((1, LANES), f32) for _ in range(B)))
        hl_ref[...] = jnp.concatenate(hs, axis=0)

        def outc(c, carry):
            rows = pl.ds(pl.multiple_of(c * rc, rc), rc)
            y_ref[rows, :] = (b_scr[rows, :] * jax.nn.gelu(gg_ref[rows, :])).astype(bf16)
            return carry
        lax.fori_loop(0, M // rc, outc, 0)

    W = NB * BS
    return pl.pallas_call(
        body,
        grid=(NB,),
        in_specs=[pl.BlockSpec((M, LANES), lambda n: (0, x_blk + n)),
                  pl.BlockSpec((M, LANES), lambda n: (0, g_blk + n)),
                  pl.BlockSpec((4, LANES), lambda n: (0, n)),
                  pl.BlockSpec((1, LANES), lambda n: (0, n)),
                  pl.BlockSpec((None, BS, BS), lambda n: (n, 0, 0)),
                  pl.BlockSpec((1, LANES), lambda n: (0, n)),
                  pl.BlockSpec((None, BS, BS), lambda n: (n, 0, 0)),
                  pl.BlockSpec((1, LANES), lambda n: (0, n)),
                  pl.BlockSpec((1, LANES), lambda n: (0, n))],
        out_specs=[pl.BlockSpec((M, LANES), lambda n: (0, n)),
                   pl.BlockSpec((B, LANES), lambda n: (0, n))],
        out_shape=[jax.ShapeDtypeStruct((M, W), bf16), jax.ShapeDtypeStruct((B, W), f32)],
        scratch_shapes=[pltpu.VMEM((M, LANES), f32), pltpu.VMEM((M, LANES), f32)],
        compiler_params=_params(("parallel",), 48),
        name="rglru_prompt",
    )(p0, p0, conv_w, conv_b.reshape(1, W), w_r, b_r.reshape(1, W), w_i, b_i.reshape(1, W), lam.reshape(1, W))


def _rglru_sample(p0, conv_state, h0, conv_w, conv_b, w_r, b_r, w_i, b_i, lam, *, x_blk, g_blk):
    Bs = p0.shape[0]
    NB, BS, _ = w_r.shape
    W = NB * BS
    s0, s1, s2 = conv_state[:, 0], conv_state[:, 1], conv_state[:, 2]

    def body(x_ref, gg_ref, s0_ref, s1_ref, s2_ref, h0_ref, cw_ref, cb_ref, wr_ref, br_ref, wi_ref, bi_ref, lam_ref,
             y_ref, h_ref):
        cw = cw_ref[...]
        x = x_ref[...]
        xc = cb_ref[...] + cw[0:1] * s0_ref[...] + cw[1:2] * s1_ref[...] + cw[2:3] * s2_ref[...] + cw[3:4] * x
        sp = jax.nn.softplus(-lam_ref[...])
        a, mult, ig = _rg_gates(xc, wr_ref[...].astype(bf16), br_ref[...], wi_ref[...].astype(bf16), bi_ref[...], sp)
        if PAST_LEN == 0:
            mult = jnp.ones_like(mult)
        h = a * h0_ref[...] + mult * (ig * xc)
        h_ref[...] = h
        y_ref[...] = (h * jax.nn.gelu(gg_ref[...])).astype(bf16)

    blk = lambda n: (0, n)
    vec = pl.BlockSpec((1, LANES), blk)
    mat = pl.BlockSpec((Bs, LANES), blk)
    y, h = pl.pallas_call(
        body,
        grid=(NB,),
        in_specs=[pl.BlockSpec((Bs, LANES), lambda n: (0, x_blk + n)),
                  pl.BlockSpec((Bs, LANES), lambda n: (0, g_blk + n)),
                  mat, mat, mat, mat,
                  pl.BlockSpec((4, LANES), blk), vec,
                  pl.BlockSpec((None, BS, BS), lambda n: (n, 0, 0)), vec,
                  pl.BlockSpec((None, BS, BS), lambda n: (n, 0, 0)), vec, vec],
        out_specs=[mat, mat],
        out_shape=[jax.ShapeDtypeStruct((Bs, W), bf16), jax.ShapeDtypeStruct((Bs, W), f32)],
        compiler_params=_params(("parallel",), 32),
        name="rglru_sample",
    )(p0, p0, s0, s1, s2, h0, conv_w, conv_b.reshape(1, W), w_r, b_r.reshape(1, W), w_i, b_i.reshape(1, W),
      lam.reshape(1, W))
    return y, h


def _blockdiag_tiles(w, tile):
    nblk, bs, _ = w.shape
    per = tile // bs
    nt = nblk // per
    w4 = w.reshape(nt, per, bs, bs)
    eye = jnp.eye(per, dtype=w.dtype)
    return jnp.einsum('tncd,nm->tncmd', w4, eye).reshape(nt, tile, tile)


def _mlstm_stage1(p1, conv_w, conv_b, wq_t, wk_t, wv_t, wg, bg, *, seq_len, tr, H, DH, conv_state=None):
    M = p1.shape[0]
    W = H * DH
    NTL, TL, _ = wq_t.shape
    sample = conv_state is not None
    tiles_per_seq = max(seq_len // tr, 1)
    kscale = DH ** -0.5

    def body(*refs):
        it = iter(refs)
        x_ref = next(it)
        if sample:
            s0_ref, s1_ref, s2_ref = next(it), next(it), next(it)
        else:
            xh_ref = next(it)
        cw_ref, cb_ref, wq_ref, wk_ref, wv_ref, wg_ref, bg_ref = (next(it) for _ in range(7))
        q_ref, k_ref, v_ref, g_ref, xc_ref = (next(it) for _ in range(5))
        cw = cw_ref[...]
        x = x_ref[...]
        if sample:
            conv = cb_ref[...] + cw[0:1] * s0_ref[...] + cw[1:2] * s1_ref[...] + cw[2:3] * s2_ref[...] + cw[3:4] * x
        else:
            keep = (pl.program_id(0) % tiles_per_seq != 0).astype(f32)
            prev = xh_ref[...] * keep
            conv = (cb_ref[...] + cw[0:1] * _shifted(prev, x, 3) + cw[1:2] * _shifted(prev, x, 2)
                    + cw[2:3] * _shifted(prev, x, 1) + cw[3:4] * x)
        xc = conv * jax.nn.sigmoid(conv)
        xc_ref[...] = xc
        xcb = xc.astype(bf16)
        xb = x.astype(bf16)
        qs, ks, vs = [], [], []
        for t in range(NTL):
            cs = slice(t * TL, (t + 1) * TL)
            qs.append(_dot(xcb[:, cs], wq_ref[t]).astype(bf16))
            ks.append((_dot(xcb[:, cs], wk_ref[t]) * kscale).astype(bf16))
            vs.append(_dot(xb[:, cs], wv_ref[t]).astype(bf16))
        q = jnp.concatenate(qs, axis=1)
        k = jnp.concatenate(ks, axis=1)
        v = jnp.concatenate(vs, axis=1)
        q_ref[...] = q
        k_ref[...] = k
        v_ref[...] = v
        gt = _dot(q, wg_ref[0:W, :]) + _dot(k, wg_ref[W:2 * W, :]) + _dot(v, wg_ref[2 * W:3 * W, :]) + bg_ref[...]
        lane = lax.broadcasted_iota(jnp.int32, gt.shape, 1)
        g_ref[...] = jnp.where(jnp.logical_and(lane >= H, lane < 2 * H), jax.nn.log_sigmoid(gt), gt)

    row = lambda i: (i, 0)
    const2 = lambda i: (0, 0)
    const3 = lambda i: (0, 0, 0)
    in_specs = [pl.BlockSpec((tr, W), row)]
    args = [p1]
    if sample:
        in_specs += [pl.BlockSpec((tr, W), row)] * 3
        args += [conv_state[:, 0], conv_state[:, 1], conv_state[:, 2]]
    else:
        in_specs.append(pl.BlockSpec((SUBLANES, W), lambda i: (jnp.maximum(i * (tr // SUBLANES) - 1, 0), 0)))
        args.append(p1)
    in_specs += [pl.BlockSpec((4, W), const2), pl.BlockSpec((1, W), const2),
                 pl.BlockSpec((NTL, TL, TL), const3), pl.BlockSpec((NTL, TL, TL), const3),
                 pl.BlockSpec((NTL, TL, TL), const3),
                 pl.BlockSpec((3 * W, LANES), const2), pl.BlockSpec((1, LANES), const2)]
    args += [conv_w, conv_b.reshape(1, W), wq_t, wk_t, wv_t, wg, bg]
    return pl.pallas_call(
        body,
        grid=(M // tr,),
        in_specs=in_specs,
        out_specs=[pl.BlockSpec((tr, W), row)] * 3 + [pl.BlockSpec((tr, LANES), row), pl.BlockSpec((tr, W), row)],
        out_shape=[jax.ShapeDtypeStruct((M, W), bf16)] * 3
        + [jax.ShapeDtypeStruct((M, LANES), f32), jax.ShapeDtypeStruct((M, W), f32)],
        compiler_params=_params(("parallel",), 48),
        name="mlstm_stage1_sample" if sample else "mlstm_stage1",
    )(*args)


def _mlstm_prompt(q, k, v, gates, xc, p1, g_norm, skip, *, B, L, H, DH, tb):
    NT = L // tb
    NC = tb // CHUNK
    C = CHUNK

    def body(q_ref, k_ref, v_ref, g_ref, xc_ref, om_ref, gn_ref, sk_ref,
             y_ref, c_out, n_out, m_out, c_scr, n_scr, m_scr):
        hh = pl.program_id(1)
        t = pl.program_id(2)

        @pl.when(t == 0)
        def _():
            c_scr[...] = jnp.zeros_like(c_scr)
            n_scr[...] = jnp.zeros_like(n_scr)
            m_scr[...] = jnp.zeros_like(m_scr)

        ii = lax.broadcasted_iota(jnp.int32, (C, C), 0)
        jj = lax.broadcasted_iota(jnp.int32, (C, C), 1)
        causal = ii >= jj
        tril = causal.astype(f32)
        lane = lax.broadcasted_iota(jnp.int32, (C, LANES), 1)

        def chunk(c, carry):
            rows = pl.ds(pl.multiple_of(c * C, C), C)
            gts = g_ref[rows, :]
            gcum = jnp.dot(tril, gts, precision=HIGHEST, preferred_element_type=f32)
            i_col = jnp.sum(jnp.where(lane == hh, gts, 0.0), axis=1, keepdims=True)
            b_col = jnp.sum(jnp.where(lane == H + hh, gcum, 0.0), axis=1, keepdims=True)
            i_row = _row_from_col(i_col)
            b_row = _row_from_col(b_col)
            m_prev = m_scr[...]
            dmat = jnp.where(causal, b_col - b_row + i_row, -jnp.inf)
            inter = b_col + m_prev
            m_col = jnp.maximum(inter, jnp.max(dmat, axis=1, keepdims=True))
            g_col = jnp.exp(inter - m_col)
            qb = q_ref[rows, :]
            kb = k_ref[rows, :]
            vb = v_ref[rows, :]
            s = _dot_nt(qb, kb) * jnp.exp(dmat - m_col)
            cs = c_scr[...]
            ns = n_scr[...]
            num = g_col * _dot_nt(qb, cs.astype(bf16)) + _dot(s.astype(bf16), vb)
            den = g_col * jnp.sum(qb.astype(f32) * ns, axis=1, keepdims=True) + jnp.sum(s, axis=1, keepdims=True)
            hm = num / jnp.maximum(jnp.abs(den), jnp.exp(-m_col))
            m_new = m_col[C - 1:C, :]
            wk = jnp.exp(b_col[C - 1:C, :] - b_col + i_col - m_new)
            gc = jnp.exp(inter[C - 1:C, :] - m_new)
            kw = kb.astype(f32) * wk
            c_scr[...] = gc * cs + _dot_tn(vb, kw.astype(bf16))
            n_scr[...] = gc * ns + jnp.sum(kw, axis=0, keepdims=True)
            m_scr[...] = m_new
            y = (_rms(hm, gn_ref[...]) + sk_ref[...] * xc_ref[rows, :]) * jax.nn.sigmoid(om_ref[rows, :])
            y_ref[rows, :] = y.astype(bf16)
            return carry
        lax.fori_loop(0, NC, chunk, 0)

        @pl.when(t == NT - 1)
        def _():
            c_out[0, 0] = c_scr[...]
            n_out[0, 0] = n_scr[...]
            m_out[0, 0] = m_scr[...]

    blk = lambda b, h, t: (b * NT + t, h)
    W = H * DH
    y, c_new, n_new, m_new = pl.pallas_call(
        body,
        grid=(B, H, NT),
        in_specs=[pl.BlockSpec((tb, DH), blk), pl.BlockSpec((tb, DH), blk), pl.BlockSpec((tb, DH), blk),
                  pl.BlockSpec((tb, LANES), lambda b, h, t: (b * NT + t, 0)),
                  pl.BlockSpec((tb, DH), blk),
                  pl.BlockSpec((tb, DH), lambda b, h, t: (b * NT + t, H + h)),
                  pl.BlockSpec((1, DH), lambda b, h, t: (0, h)),
                  pl.BlockSpec((1, DH), lambda b, h, t: (0, h))],
        out_specs=[pl.BlockSpec((tb, DH), blk),
                   pl.BlockSpec((1, 1, DH, DH), lambda b, h, t: (b, h, 0, 0)),
                   pl.BlockSpec((1, 1, 1, DH), lambda b, h, t: (b, h, 0, 0)),
                   pl.BlockSpec((1, 1, 1, 1), lambda b, h, t: (b, h, 0, 0))],
        out_shape=[jax.ShapeDtypeStruct((B * L, W), bf16),
                   jax.ShapeDtypeStruct((B, H, DH, DH), f32),
                   jax.ShapeDtypeStruct((B, H, 1, DH), f32),
                   jax.ShapeDtypeStruct((B, H, 1, 1), f32)],
        scratch_shapes=[pltpu.VMEM((DH, DH), f32), pltpu.VMEM((1, DH), f32), pltpu.VMEM((1, 1), f32)],
        compiler_params=_params(("parallel", "parallel", "arbitrary"), 32),
        name="mlstm_prompt",
    )(q, k, v, gates, xc, p1, g_norm.reshape(1, W), skip.reshape(1, W))
    return y, c_new, n_new.reshape(B, H, DH), m_new.reshape(B, H)


def _mlstm_sample(q, k, v, gates, xc, p1, g_norm, skip, c0, n0, m0, *, H, DH):
    Bs = q.shape[0]
    W = H * DH
    r3 = lambda a: a.reshape(Bs, 1, a.shape[-1])

    def body(q_ref, k_ref, v_ref, g_ref, xc_ref, om_ref, gn_ref, sk_ref, c_ref, n_ref, m_ref,
             y_ref, c_out, n_out, m_out):
        gts = g_ref[...]
        for h in range(H):
            cs = slice(h * DH, (h + 1) * DH)
            ig = gts[:, h:h + 1]
            fg = gts[:, H + h:H + h + 1]
            inter = fg + m_ref[:, h:h + 1]
            m = jnp.maximum(inter, ig)
            g = jnp.exp(inter - m)
            qr = q_ref[:, cs].astype(f32)
            kw = k_ref[:, cs].astype(f32) * jnp.exp(ig - m)
            v_col = _col_from_row(v_ref[:, cs].astype(f32))
            cn = g * c_ref[h] + v_col * kw
            nn = g * n_ref[:, cs] + kw
            c_out[h] = cn
            n_out[:, cs] = nn
            m_out[:, h:h + 1] = m
            num = _row_from_col(jnp.sum(cn * qr, axis=1, keepdims=True))
            den = jnp.sum(nn * qr, axis=1, keepdims=True)
            hm = num / jnp.maximum(jnp.abs(den), jnp.exp(-m))
            y = (_rms(hm, gn_ref[:, cs]) + sk_ref[:, cs] * xc_ref[:, cs]) * jax.nn.sigmoid(om_ref[:, cs])
            y_ref[:, cs] = y.astype(bf16)

    per = lambda b: (b, 0, 0)
    const2 = lambda b: (0, 0)
    y, c_new, n_new, m_new = pl.pallas_call(
        body,
        grid=(Bs,),
        in_specs=[pl.BlockSpec((None, 1, W), per), pl.BlockSpec((None, 1, W), per), pl.BlockSpec((None, 1, W), per),
                  pl.BlockSpec((None, 1, LANES), per), pl.BlockSpec((None, 1, W), per),
                  pl.BlockSpec((None, 1, W), lambda b: (b, 0, 1)),
                  pl.BlockSpec((1, W), const2), pl.BlockSpec((1, W), const2),
                  pl.BlockSpec((None, H, DH, DH), lambda b: (b, 0, 0, 0)),
                  pl.BlockSpec((None, 1, W), per), pl.BlockSpec((None, 1, H), per)],
        out_specs=[pl.BlockSpec((None, 1, W), per),
                   pl.BlockSpec((None, H, DH, DH), lambda b: (b, 0, 0, 0)),
                   pl.BlockSpec((None, 1, W), per), pl.BlockSpec((None, 1, H), per)],
        out_shape=[jax.ShapeDtypeStruct((Bs, 1, W), bf16), jax.ShapeDtypeStruct((Bs, H, DH, DH), f32),
                   jax.ShapeDtypeStruct((Bs, 1, W), f32), jax.ShapeDtypeStruct((Bs, 1, H), f32)],
        compiler_params=_params(("parallel",), 32),
        name="mlstm_sample",
    )(r3(q), r3(k), r3(v), r3(gates), r3(xc), r3(p1), g_norm.reshape(1, W), skip.reshape(1, W),
      c0, n0.reshape(Bs, 1, W), m0.reshape(Bs, 1, H))
    return y.reshape(Bs, W), c_new, n_new.reshape(Bs, H, DH), m_new.reshape(Bs, H)


S5_TILES = 8


def _s5_layouts(lam_re, lam_im, log_dt, b_re, b_im, c_re, c_im):
    G, P = lam_re.shape
    GC = b_re.shape[2]
    T = S5_TILES
    gpt = G // T
    ns = G * P
    flat = lambda a: a.reshape(ns)
    ldt = jnp.broadcast_to(log_dt[:, None], (G, P))
    rows = [flat(a).reshape(T, 1, ns // T) for a in (lam_re, lam_im, ldt)]
    eye = jnp.eye(gpt, dtype=f32)
    bbd = [jnp.einsum('jgpc,gh->jgchp', a.reshape(T, gpt, P, GC), eye).reshape(T, gpt * GC, gpt * P) for a in (b_re, b_im)]
    cbd = [jnp.einsum('jgcp,gh->jgphc', a.reshape(T, gpt, GC, P), eye).reshape(T, gpt * P, gpt * GC) for a in (c_re, c_im)]
    return rows, bbd, cbd


def _s5_discretise(lre, lim, ldt):
    dt = jnp.exp(ldt)
    mag = jnp.exp(dt * lre)
    ar = mag * jnp.cos(dt * lim)
    ai = mag * jnp.sin(dt * lim)
    den = lre * lre + lim * lim
    cr = ((ar - 1.0) * lre + ai * lim) / den
    ci = (ai * lre - (ar - 1.0) * lim) / den
    return ar, ai, cr, ci


def _s5_mixer(p1, u_blk, rows, bbd, cbd, d_skip, w_glu, b_glu, *, B, L, tb, state=None):
    T = S5_TILES
    lre_r, lim_r, ldt_r = rows
    SW = lre_r.shape[2]
    CW = bbd[0].shape[1]
    W = T * CW
    NS = T * SW
    KT = SW // LANES
    sample = state is not None
    NT = 1 if sample else L // tb
    M = B * L

    def body(*refs):
        it = iter(refs)
        u_ref = next(it)
        if sample:
            x0r_ref, x0i_ref = next(it), next(it)
        lre_ref, lim_ref, ldt_ref = next(it), next(it), next(it)
        bre_ref, bim_ref, cre_ref, cim_ref = next(it), next(it), next(it), next(it)
        d_ref, wg_ref, bgl_ref = next(it), next(it), next(it)
        y_ref, xr_out, xi_out = next(it), next(it), next(it)
        bbr, bbi, cbr, cbi, ar_scr, ai_scr = (next(it) for _ in range(6))
        if not sample:
            sre, sim, xr_c, xi_c, yacc = (next(it) for _ in range(5))
        first = jnp.logical_and(pl.program_id(0) == 0, pl.program_id(1) == 0)

        @pl.when(first)
        def _():
            for j in range(T):
                ar, ai, cr, ci = _s5_discretise(lre_ref[j], lim_ref[j], ldt_ref[j])
                ar_scr[j] = ar
                ai_scr[j] = ai
                br = bre_ref[j]
                bi = bim_ref[j]
                bbr[j] = (cr * br - ci * bi).astype(bf16)
                bbi[j] = (cr * bi + ci * br).astype(bf16)
                cbr[j] = cre_ref[j].astype(bf16)
                cbi[j] = cim_ref[j].astype(bf16)

        u = u_ref[...]
        ub = u.astype(bf16)
        ys = []
        if sample:
            for j in range(T):
                cs = slice(j * SW, (j + 1) * SW)
                uj = ub[:, j * CW:(j + 1) * CW]
                ar = ar_scr[j]
                ai = ai_scr[j]
                x0r = x0r_ref[:, cs]
                x0i = x0i_ref[:, cs]
                xr = ar * x0r - ai * x0i + _dot(uj, bbr[j])
                xi = ar * x0i + ai * x0r + _dot(uj, bbi[j])
                xr_out[:, cs] = xr
                xi_out[:, cs] = xi
                ys.append(_dot(xr.astype(bf16), cbr[j]) - _dot(xi.astype(bf16), cbi[j]))
            y = jnp.concatenate(ys, axis=1)
        else:
            t = pl.program_id(1)

            @pl.when(t == 0)
            def _():
                xr_c[...] = jnp.zeros_like(xr_c)
                xi_c[...] = jnp.zeros_like(xi_c)

            for j in range(T):
                uj = ub[:, j * CW:(j + 1) * CW]
                r = _dot(uj, bbr[j])
                im = _dot(uj, bbi[j])
                for kk in range(KT):
                    sre[kk, pl.ds(j, tb, stride=T), :] = r[:, kk * LANES:(kk + 1) * LANES]
                    sim[kk, pl.ds(j, tb, stride=T), :] = im[:, kk * LANES:(kk + 1) * LANES]
            a_r = [jnp.concatenate([ar_scr[j][:, kk * LANES:(kk + 1) * LANES] for j in range(T)], axis=0) for kk in range(KT)]
            a_i = [jnp.concatenate([ai_scr[j][:, kk * LANES:(kk + 1) * LANES] for j in range(T)], axis=0) for kk in range(KT)]

            def step(s, carry):
                xr, xi = carry
                row = pl.ds(pl.multiple_of(s * T, T), T)
                nr, ni = [], []
                for kk in range(KT):
                    r_ = a_r[kk] * xr[kk] - a_i[kk] * xi[kk] + sre[kk, row, :]
                    i_ = a_r[kk] * xi[kk] + a_i[kk] * xr[kk] + sim[kk, row, :]
                    sre[kk, row, :] = r_
                    sim[kk, row, :] = i_
                    nr.append(r_)
                    ni.append(i_)
                return tuple(nr), tuple(ni)
            xr0 = tuple(xr_c[kk] for kk in range(KT))
            xi0 = tuple(xi_c[kk] for kk in range(KT))
            xr, xi = lax.fori_loop(0, tb, step, (xr0, xi0))
            for kk in range(KT):
                xr_c[kk] = xr[kk]
                xi_c[kk] = xi[kk]

            @pl.when(t == NT - 1)
            def _():
                for kk in range(KT):
                    xr_out[kk] = xr[kk]
                    xi_out[kk] = xi[kk]

            for j in range(T):
                xrj = jnp.concatenate([sre[kk, pl.ds(j, tb, stride=T), :] for kk in range(KT)], axis=1).astype(bf16)
                xij = jnp.concatenate([sim[kk, pl.ds(j, tb, stride=T), :] for kk in range(KT)], axis=1).astype(bf16)
                yacc[:, j * CW:(j + 1) * CW] = _dot(xrj, cbr[j]) - _dot(xij, cbi[j])
            y = yacc[...]
        ysk = jax.nn.gelu(y + d_ref[...] * u)
        z = _dot(ysk.astype(bf16), wg_ref[...]) + bgl_ref[...]
        y_ref[...] = (ysk * jax.nn.sigmoid(z)).astype(bf16)

    c3 = lambda b, t: (0, 0, 0)
    c2 = lambda b, t: (0, 0)
    in_specs = [pl.BlockSpec((tb, W), lambda b, t: (b * NT + t, u_blk))]
    args = [p1]
    if sample:
        in_specs += [pl.BlockSpec((tb, NS), lambda b, t: (b, 0))] * 2
        args += [state[0], state[1]]
    in_specs += [pl.BlockSpec((T, 1, SW), c3)] * 3
    in_specs += [pl.BlockSpec((T, CW, SW), c3)] * 2 + [pl.BlockSpec((T, SW, CW), c3)] * 2
    in_specs += [pl.BlockSpec((1, W), c2), pl.BlockSpec((W, W), c2), pl.BlockSpec((1, W), c2)]
    args += [lre_r, lim_r, ldt_r, bbd[0], bbd[1], cbd[0], cbd[1], d_skip.reshape(1, W), w_glu, b_glu.reshape(1, W)]
    scratch = [pltpu.VMEM((T, CW, SW), bf16), pltpu.VMEM((T, CW, SW), bf16),
               pltpu.VMEM((T, SW, CW), bf16), pltpu.VMEM((T, SW, CW), bf16),
               pltpu.VMEM((T, 1, SW), f32), pltpu.VMEM((T, 1, SW), f32)]
    if sample:
        grid = (M // tb, 1)
        st_spec = pl.BlockSpec((tb, NS), lambda b, t: (b, 0))
        st_shape = jax.ShapeDtypeStruct((M, NS), f32)
    else:
        grid = (B, NT)
        st_spec = pl.BlockSpec((None, KT, T, LANES), lambda b, t: (b, 0, 0, 0))
        st_shape = jax.ShapeDtypeStruct((B, KT, T, LANES), f32)
        scratch += [pltpu.VMEM((KT, tb * T, LANES), f32), pltpu.VMEM((KT, tb * T, LANES), f32),
                    pltpu.VMEM((KT, T, LANES), f32), pltpu.VMEM((KT, T, LANES), f32), pltpu.VMEM((tb, W), f32)]
    y, xr, xi = pl.pallas_call(
        body,
        grid=grid,
        in_specs=in_specs,
        out_specs=[pl.BlockSpec((tb, W), lambda b, t: (b * NT + t, 0)), st_spec, st_spec],
        out_shape=[jax.ShapeDtypeStruct((M, W), bf16), st_shape, st_shape],
        scratch_shapes=scratch,
        compiler_params=_params(("arbitrary", "arbitrary"), 56),
        name="s5_sample" if sample else "s5_prompt",
    )(*args)
    if not sample:
        xr = jnp.transpose(xr, (0, 2, 1, 3)).reshape(B, NS)
        xi = jnp.transpose(xi, (0, 2, 1, 3)).reshape(B, NS)
    return y, xr, xi


def _trunk(x3, st, w, *, sample):
    B, L, D = x3.shape
    M = B * L
    x = x3.reshape(M, D)
    H_g, DK, DV = st['gla_S'].shape[1:] if sample else w['gla_dims']
    H_m, DH = w['ml_dims']
    G, P = w['s5_dims']
    tm = M if sample else min(1024, L)
    tf = M if sample else min(1024, L)
    tb = min(512, L)
    out = {}

    p0 = _norm_matmul(x, w['g_mix0'], w['w_in0_main'], tm=tm, tn=512)
    alr = _norm_matmul(x, w['g_mix0'], w['w_in0_alr'], tm=tm, tn=LANES)
    x_blk = (2 * H_g * DK + 2 * H_g * DV) // LANES
    W_rg = w['rg_lambda'].shape[0]
    g_blk = x_blk + W_rg // LANES
    rg_x = p0[:, x_blk * LANES:x_blk * LANES + W_rg]
    if sample:
        ya, out['gla_S'] = _gla_sample(p0, alr, w['gla_w2p'], w['gla_b_alpha'], w['gla_g_norm'], st['gla_S'],
                                       H=H_g, DK=DK, DV=DV)
        yb, out['rg_h'] = _rglru_sample(p0, st['rg_conv'], st['rg_h'], w['rg_conv_w'], w['rg_conv_b'], w['rg_w_r'],
                                        w['rg_b_r'], w['rg_w_i'], w['rg_b_i'], w['rg_lambda'], x_blk=x_blk, g_blk=g_blk)
        out['rg_conv'] = jnp.concatenate([st['rg_conv'][:, 1:], rg_x[:, None, :]], axis=1)
    else:
        ya, out['gla_S'] = _gla_prompt(p0, alr, w['gla_w2p'], w['gla_b_alpha'], w['gla_g_norm'],
                                       B=B, L=L, H=H_g, DK=DK, DV=DV, tb=tb)
        yb, out['rg_h'] = _rglru_prompt(p0, w['rg_conv_w'], w['rg_conv_b'], w['rg_w_r'], w['rg_b_r'], w['rg_w_i'],
                                        w['rg_b_i'], w['rg_lambda'], B=B, L=L, x_blk=x_blk, g_blk=g_blk)
        out['rg_conv'] = rg_x.reshape(B, L, W_rg)[:, L - 3:]
    x = _out_proj(ya, yb, w['w_out0'], x, tm=tm, tn=512)
    x, ffn0 = _conv_ffn(x, w['g_ffn'][0], w['ffn'][0], seq_len=L, tm=tf, prev=jnp.transpose(st['ffn_conv'][0], (1, 0, 2)) if sample else None)

    W_ml = H_m * DH
    p1 = _norm_matmul(x, w['g_mix1'], w['w_in1'], tm=tm, tn=512)
    xm = p1[:, :W_ml]
    q, k, v, gates, xc = _mlstm_stage1(p1, w['ml_conv_w'], w['ml_conv_b'], w['ml_wq_t'], w['ml_wk_t'], w['ml_wv_t'],
                                       w['ml_wg'], w['ml_bg'], seq_len=L, tr=M if sample else min(256, L), H=H_m, DH=DH,
                                       conv_state=st['ml_conv'] if sample else None)
    u_blk = 2 * W_ml // (G * w['s5_gc'])
    if sample:
        yc, out['ml_C'], out['ml_n'], out['ml_m'] = _mlstm_sample(
            q, k, v, gates, xc, p1, w['ml_g_norm'], w['ml_skip'], st['ml_C'], st['ml_n'], st['ml_m'], H=H_m, DH=DH)
        out['ml_conv'] = jnp.concatenate([st['ml_conv'][:, 1:], xm[:, None, :]], axis=1)
        yd, s5r, s5i = _s5_mixer(p1, u_blk, w['s5_rows'], w['s5_bbd'], w['s5_cbd'], w['s5_D'], w['s5_w_glu'],
                                 w['s5_b_glu'], B=B, L=1, tb=B,
                                 state=(st['s5_re'].reshape(B, G * P), st['s5_im'].reshape(B, G * P)))
    else:
        yc, out['ml_C'], out['ml_n'], out['ml_m'] = _mlstm_prompt(
            q, k, v, gates, xc, p1, w['ml_g_norm'], w['ml_skip'], B=B, L=L, H=H_m, DH=DH, tb=tb)
        out['ml_conv'] = xm.reshape(B, L, W_ml)[:, L - 3:]
        yd, s5r, s5i = _s5_mixer(p1, u_blk, w['s5_rows'], w['s5_bbd'], w['s5_cbd'], w['s5_D'], w['s5_w_glu'],
                                 w['s5_b_glu'], B=B, L=L, tb=min(256, L))
    out['s5_re'] = s5r.reshape(B, G, P)
    out['s5_im'] = s5i.reshape(B, G, P)
    x = _out_proj(yc, yd, w['w_out1'], x, tm=tm, tn=512)
    x, ffn1 = _conv_ffn(x, w['g_ffn'][1], w['ffn'][1], seq_len=L, tm=tf, prev=jnp.transpose(st['ffn_conv'][1], (1, 0, 2)) if sample else None,
                        final_g=w['g_final'])
    out['ffn_conv'] = jnp.stack([ffn0, ffn1], axis=0)
    return x.reshape(B, L, D), out


def kernel(x_prompt, x_sample, state_gla_S, state_rglru_h, state_rglru_conv, state_mlstm_C, state_mlstm_n, state_mlstm_m, state_mlstm_conv, state_s5_re, state_s5_im, state_ffn_conv, g_mix0, w_in0, gla_w_alpha2, gla_b_alpha, gla_g_norm, rg_conv_w, rg_conv_b, rg_w_r, rg_b_r, rg_w_i, rg_b_i, rg_lambda, w_out0, g_mix1, w_in1, ml_conv_w, ml_conv_b, ml_wq, ml_wk, ml_wv, ml_w_igate, ml_b_igate, ml_w_fgate, ml_b_fgate, ml_g_norm, ml_skip, s5_lam_re, s5_lam_im, s5_log_dt, s5_B_re, s5_B_im, s5_C_re, s5_C_im, s5_D, s5_w_glu, s5_b_glu, w_out1, g_ffn, ffn_w_up, ffn_conv_w, ffn_conv_b, ffn_w_down, g_final):
    _, H_g, DK, DV = state_gla_S.shape
    _, H_m, DH, _ = state_mlstm_C.shape
    G, P = s5_lam_re.shape
    rank = gla_w_alpha2.shape[0]
    n_main = 2 * H_g * DK + 2 * H_g * DV
    w_in0_main = jnp.concatenate([w_in0[:, :n_main], w_in0[:, n_main + rank:]], axis=1).astype(bf16)
    w_in0_alr = jnp.pad(w_in0[:, n_main:n_main + rank], ((0, 0), (0, LANES - rank))).astype(bf16)
    gla_w2p = jnp.pad(gla_w_alpha2, ((0, LANES - rank), (0, 0)))
    ml_tile = 256
    ml_wg = jnp.pad(jnp.concatenate([ml_w_igate, ml_w_fgate], axis=1), ((0, 0), (0, LANES - 2 * H_m))).astype(bf16)
    ml_bg = jnp.pad(jnp.concatenate([ml_b_igate, ml_b_fgate]), (0, LANES - 2 * H_m)).reshape(1, LANES)
    rows, bbd, cbd = _s5_layouts(s5_lam_re, s5_lam_im, s5_log_dt, s5_B_re, s5_B_im, s5_C_re, s5_C_im)
    w = dict(
        g_mix0=g_mix0, w_in0_main=w_in0_main, w_in0_alr=w_in0_alr, gla_w2p=gla_w2p, gla_b_alpha=gla_b_alpha,
        gla_g_norm=gla_g_norm, gla_dims=(H_g, DK, DV), rg_conv_w=rg_conv_w, rg_conv_b=rg_conv_b, rg_w_r=rg_w_r,
        rg_b_r=rg_b_r, rg_w_i=rg_w_i, rg_b_i=rg_b_i, rg_lambda=rg_lambda, w_out0=w_out0.astype(bf16),
        g_mix1=g_mix1, w_in1=w_in1.astype(bf16), ml_conv_w=ml_conv_w, ml_conv_b=ml_conv_b,
        ml_wq_t=_blockdiag_tiles(ml_wq, ml_tile).astype(bf16), ml_wk_t=_blockdiag_tiles(ml_wk, ml_tile).astype(bf16),
        ml_wv_t=_blockdiag_tiles(ml_wv, ml_tile).astype(bf16), ml_wg=ml_wg, ml_bg=ml_bg, ml_g_norm=ml_g_norm,
        ml_skip=ml_skip, ml_dims=(H_m, DH), s5_dims=(G, P), s5_gc=s5_B_re.shape[2], s5_rows=rows, s5_bbd=bbd,
        s5_cbd=cbd, s5_D=s5_D, s5_w_glu=s5_w_glu.astype(bf16), s5_b_glu=s5_b_glu, w_out1=w_out1.astype(bf16),
        g_ffn=g_ffn, g_final=g_final,
        ffn=[_ffn_prepare(ffn_w_up[l], ffn_conv_w[l], ffn_conv_b[l], ffn_w_down[l]) for l in range(g_ffn.shape[0])])
    st_s = dict(gla_S=state_gla_S, rg_h=state_rglru_h, rg_conv=state_rglru_conv, ml_C=state_mlstm_C,
                ml_n=state_mlstm_n, ml_m=state_mlstm_m, ml_conv=state_mlstm_conv, s5_re=state_s5_re,
                s5_im=state_s5_im, ffn_conv=state_ffn_conv)
    y_p, np_ = _trunk(x_prompt, None, w, sample=False)
    y_s, ns_ = _trunk(x_sample, st_s, w, sample=True)
    names = ('gla_S', 'rg_h', 'rg_conv', 'ml_C', 'ml_n', 'ml_m', 'ml_conv', 's5_re', 's5_im', 'ffn_conv')
    outs = [y_p, y_s]
    for nme in names:
        outs += [np_[nme], ns_[nme]]
    return tuple(outs)
```

```python
import functools

import jax
import jax.numpy as jnp
from jax import lax
from jax.experimental import pallas as pl
from jax.experimental.pallas import tpu as pltpu

f32 = jnp.float32
bf16 = jnp.bfloat16

EPS = 1e-6
CHUNK = 64
GLA_TAU = 16.0
RG_C = 8.0
PAST_LEN = 16384
LANES = 128
SUBLANES = 8
HALO = 16
MIB = 1024 * 1024


def _params(sem, vmem_mib):
    return pltpu.CompilerParams(dimension_semantics=sem, vmem_limit_bytes=int(vmem_mib * MIB))


def _dot(a, b):
    return jnp.dot(a, b, preferred_element_type=f32)


def _dot_nt(a, b):
    return lax.dot_general(a, b, (((1,), (1,)), ((), ())), preferred_element_type=f32)


def _dot_tn(a, b):
    return lax.dot_general(a, b, (((0,), (0,)), ((), ())), preferred_element_type=f32)


def _rms(x, g):
    return x * lax.rsqrt(jnp.mean(x * x, axis=-1, keepdims=True) + EPS) * g


def _eye(n):
    return lax.broadcasted_iota(jnp.int32, (n, n), 0) == lax.broadcasted_iota(jnp.int32, (n, n), 1)


def _col_from_row(row):
    n = row.shape[1]
    return jnp.sum(jnp.where(_eye(n), jnp.broadcast_to(row, (n, n)), 0.0), axis=1, keepdims=True)


def _row_from_col(col):
    n = col.shape[0]
    return jnp.sum(jnp.where(_eye(n), jnp.broadcast_to(col, (n, n)), 0.0), axis=0, keepdims=True)


def _chunk_cumsum(x, chunk):
    pos = lax.broadcasted_iota(jnp.int32, (x.shape[0], 1), 0) % chunk
    step = 1
    while step < chunk:
        x = x + jnp.where(pos >= step, pltpu.roll(x, step, 0), 0.0)
        step *= 2
    return x


def _shifted(prev8, x, back):
    xx = jnp.concatenate([prev8, x], axis=0)
    n = x.shape[0]
    return xx[SUBLANES - back:SUBLANES - back + n]


def _norm_matmul(x, g, w, *, tm, tn):
    M, D = x.shape
    N = w.shape[1]
    rc = min(tm, 256)

    def body(x_ref, g_ref, w_ref, o_ref, xn_ref):
        @pl.when(pl.program_id(1) == 0)
        def _():
            def chunk(r, c):
                rows = pl.ds(pl.multiple_of(r * rc, rc), rc)
                xn_ref[rows, :] = _rms(x_ref[rows, :], g_ref[...]).astype(bf16)
                return c
            lax.fori_loop(0, tm // rc, chunk, 0)
        o_ref[...] = _dot(xn_ref[...], w_ref[...])

    return pl.pallas_call(
        body,
        grid=(M // tm, N // tn),
        in_specs=[pl.BlockSpec((tm, D), lambda i, j: (i, 0)),
                  pl.BlockSpec((1, D), lambda i, j: (0, 0)),
                  pl.BlockSpec((D, tn), lambda i, j: (0, j))],
        out_specs=pl.BlockSpec((tm, tn), lambda i, j: (i, j)),
        out_shape=jax.ShapeDtypeStruct((M, N), f32),
        scratch_shapes=[pltpu.VMEM((tm, D), bf16)],
        compiler_params=_params(("parallel", "arbitrary"), 48),
        name="norm_matmul",
    )(x, g.reshape(1, D), w)


def _out_proj(ya, yb, w, res, *, tm, tn):
    M, Ka = ya.shape
    Kb = yb.shape[1]
    N = w.shape[1]
    assert Ka == Kb and w.shape[0] == Ka + Kb

    def body(ya_ref, yb_ref, wa_ref, wb_ref, r_ref, o_ref):
        o_ref[...] = (r_ref[...] + _dot(ya_ref[...].astype(bf16), wa_ref[...])
                      + _dot(yb_ref[...].astype(bf16), wb_ref[...]))

    return pl.pallas_call(
        body,
        grid=(M // tm, N // tn),
        in_specs=[pl.BlockSpec((tm, Ka), lambda i, j: (i, 0)),
                  pl.BlockSpec((tm, Kb), lambda i, j: (i, 0)),
                  pl.BlockSpec((Ka, tn), lambda i, j: (0, j)),
                  pl.BlockSpec((Kb, tn), lambda i, j: (1, j)),
                  pl.BlockSpec((tm, tn), lambda i, j: (i, j))],
        out_specs=pl.BlockSpec((tm, tn), lambda i, j: (i, j)),
        out_shape=jax.ShapeDtypeStruct((M, N), f32),
        compiler_params=_params(("parallel", "arbitrary"), 48),
        name="out_proj",
    )(ya, yb, w, w, res)


FFN_STEP = 256


def _ffn_to_steps(a, F):
    nsteps = -(-F // FFN_STEP)
    lead = a.shape[:-1]
    nd = len(lead)
    gv = a.reshape(lead + (2, F))
    gv = jnp.pad(gv, [(0, 0)] * (nd + 1) + [(0, nsteps * FFN_STEP - F)])
    gv = gv.reshape(lead + (2, nsteps, FFN_STEP))
    gv = jnp.transpose(gv, (nd + 1,) + tuple(range(nd)) + (nd, nd + 2))
    return gv.reshape((nsteps,) + lead + (2 * FFN_STEP,))


def _ffn_from_steps(a, F):
    lead = a.shape[:-2]
    nsteps = a.shape[-2]
    gv = a.reshape(lead + (nsteps, 2, FFN_STEP))
    gv = jnp.swapaxes(gv, -3, -2).reshape(lead + (2, nsteps * FFN_STEP))[..., :F]
    return gv.reshape(lead + (2 * F,))


def _ffn_prepare(w_up, conv_w, conv_b, w_down):
    NL, F, D = w_down.shape
    assert F % LANES == 0 and FFN_STEP == 2 * LANES
    nt = F // LANES
    nsteps = -(-F // FFN_STEP)

    def body(ga, gb, va, vb, da, db, wu_o, wd_o):
        keep_b = 2 * pl.program_id(1) + 1 < nt
        wu_o[:, 0 * LANES:1 * LANES] = ga[...].astype(bf16)
        wu_o[:, 1 * LANES:2 * LANES] = jnp.where(keep_b, gb[...], 0.0).astype(bf16)
        wu_o[:, 2 * LANES:3 * LANES] = va[...].astype(bf16)
        wu_o[:, 3 * LANES:4 * LANES] = jnp.where(keep_b, vb[...], 0.0).astype(bf16)
        wd_o[0:LANES, :] = da[...].astype(bf16)
        wd_o[LANES:2 * LANES, :] = jnp.where(keep_b, db[...], 0.0).astype(bf16)

    ta = lambda j: 2 * j
    tb = lambda j: jnp.minimum(2 * j + 1, nt - 1)
    wu, wd = pl.pallas_call(
        body,
        grid=(NL, nsteps),
        in_specs=[pl.BlockSpec((None, D, LANES), lambda l, j: (l, 0, ta(j))),
                  pl.BlockSpec((None, D, LANES), lambda l, j: (l, 0, tb(j))),
                  pl.BlockSpec((None, D, LANES), lambda l, j: (l, 0, nt + ta(j))),
                  pl.BlockSpec((None, D, LANES), lambda l, j: (l, 0, nt + tb(j))),
                  pl.BlockSpec((None, LANES, D), lambda l, j: (l, ta(j), 0)),
                  pl.BlockSpec((None, LANES, D), lambda l, j: (l, tb(j), 0))],
        out_specs=[pl.BlockSpec((None, None, D, 2 * FFN_STEP), lambda l, j: (l, j, 0, 0)),
                   pl.BlockSpec((None, None, FFN_STEP, D), lambda l, j: (l, j, 0, 0))],
        out_shape=[jax.ShapeDtypeStruct((NL, nsteps, D, 2 * FFN_STEP), bf16),
                   jax.ShapeDtypeStruct((NL, nsteps, FFN_STEP, D), bf16)],
        compiler_params=_params(("parallel", "parallel"), 32),
        name="ffn_weight_layout",
    )(w_up, w_up, w_up, w_up, w_down, w_down)
    return [dict(wu=wu, wd=wd, layer=l, F=F, cw=_ffn_to_steps(conv_w[l], F),
                 cb=_ffn_to_steps(conv_b[l], F).reshape(nsteps, 1, 2 * FFN_STEP)) for l in range(NL)]


def _conv_ffn(x, g, fw, *, seq_len, tm, prev=None, final_g=None):
    M, D = x.shape
    F = fw['F']
    layer = fw['layer']
    nsteps = fw['wu'].shape[1]
    sample = prev is not None
    rc = min(tm, 512)
    tiles_per_seq = max(seq_len // tm, 1)
    W2 = 2 * FFN_STEP

    def body(*refs):
        it = iter(refs)
        x_ref = next(it)
        xh_ref = None if sample else next(it)
        prev_ref = next(it) if sample else None
        g_ref = next(it)
        wu_ref, cw_ref, cb_ref, wd_ref = next(it), next(it), next(it), next(it)
        fg_ref = next(it) if final_g is not None else None
        o_ref, tail_ref = next(it), next(it)
        xn_ref = next(it)
        i = pl.program_id(0)
        j = pl.program_id(1)

        @pl.when(j == 0)
        def _():
            if sample:
                xn_ref[0:HALO, :] = jnp.zeros((HALO, D), bf16)
            else:
                keep = (i % tiles_per_seq != 0).astype(f32)
                hist = _rms(xh_ref[...], g_ref[...]) * keep
                xn_ref[0:HALO, :] = jnp.concatenate([jnp.zeros_like(hist), hist], axis=0).astype(bf16)

            def chunk(r, c):
                rows = pl.ds(pl.multiple_of(r * rc, rc), rc)
                xr = x_ref[rows, :]
                o_ref[rows, :] = xr
                xn_ref[pl.ds(pl.multiple_of(HALO + r * rc, HALO), rc), :] = _rms(xr, g_ref[...]).astype(bf16)
                return c
            lax.fori_loop(0, tm // rc, chunk, 0)

        cw = cw_ref[0]
        cb = cb_ref[0]
        for r in range(tm // rc):
            r0 = r * rc
            rows = pl.ds(r0, rc)
            if sample:
                up = _dot(xn_ref[pl.ds(HALO + r0, rc), :], wu_ref[...])
                conv = cb + cw[0:1] * prev_ref[0, 0, rows, :] + cw[1:2] * prev_ref[0, 1, rows, :] + cw[2:3] * up
                tail_ref[0, rows, :] = up
            else:
                xx = _dot(xn_ref[pl.ds(r0, rc + HALO), :], wu_ref[...])
                up = xx[HALO:]
                conv = cb + cw[0:1] * xx[HALO - 2:HALO - 2 + rc] + cw[1:2] * xx[HALO - 1:HALO - 1 + rc] + cw[2:3] * up
                if r == tm // rc - 1:
                    tail_ref[0, 0] = up[rc - SUBLANES:]
            h = jax.nn.gelu(conv[:, :FFN_STEP]) * conv[:, FFN_STEP:]
            o_ref[rows, :] += _dot(h.astype(bf16), wd_ref[...])

        if final_g is not None:
            @pl.when(j == nsteps - 1)
            def _():
                def chunk2(r, c):
                    rows = pl.ds(pl.multiple_of(r * rc, rc), rc)
                    o_ref[rows, :] = _rms(o_ref[rows, :], fg_ref[...])
                    return c
                lax.fori_loop(0, tm // rc, chunk2, 0)

    in_specs = [pl.BlockSpec((tm, D), lambda i, j: (i, 0))]
    args = [x]
    if sample:
        in_specs.append(pl.BlockSpec((1, 2, tm, W2), lambda i, j: (j, 0, i, 0)))
        args.append(_ffn_to_steps(prev, F))
    else:
        in_specs.append(pl.BlockSpec((SUBLANES, D), lambda i, j: (jnp.maximum(i * (tm // SUBLANES) - 1, 0), 0)))
        args.append(x)
    in_specs += [pl.BlockSpec((1, D), lambda i, j: (0, 0)),
                 pl.BlockSpec((None, None, D, W2), lambda i, j: (layer, j, 0, 0)),
                 pl.BlockSpec((1, 3, W2), lambda i, j: (j, 0, 0)),
                 pl.BlockSpec((1, 1, W2), lambda i, j: (j, 0, 0)),
                 pl.BlockSpec((None, None, FFN_STEP, D), lambda i, j: (layer, j, 0, 0))]
    args += [g.reshape(1, D), fw['wu'], fw['cw'], fw['cb'], fw['wd']]
    if final_g is not None:
        in_specs.append(pl.BlockSpec((1, D), lambda i, j: (0, 0)))
        args.append(final_g.reshape(1, D))
    if sample:
        tail_shape = (nsteps, M, W2)
        tail_spec = pl.BlockSpec((1, tm, W2), lambda i, j: (j, i, 0))
    else:
        tail_shape = (M // tm, nsteps, SUBLANES, W2)
        tail_spec = pl.BlockSpec((1, 1, SUBLANES, W2), lambda i, j: (i, j, 0, 0))
    out, tail = pl.pallas_call(
        body,
        grid=(M // tm, nsteps),
        in_specs=in_specs,
        out_specs=[pl.BlockSpec((tm, D), lambda i, j: (i, 0)), tail_spec],
        out_shape=[jax.ShapeDtypeStruct((M, D), f32), jax.ShapeDtypeStruct(tail_shape, f32)],
        scratch_shapes=[pltpu.VMEM((tm + HALO, D), bf16)],
        compiler_params=_params(("parallel", "arbitrary"), 56),
        name="conv_ffn_sample" if sample else "conv_ffn",
    )(*args)
    if sample:
        up_rows = _ffn_from_steps(jnp.transpose(tail, (1, 0, 2)), F)
        new_buf = jnp.stack([prev[1], up_rows], axis=1)
    else:
        nseq = M // seq_len
        last = tail.reshape(nseq, tiles_per_seq, nsteps, SUBLANES, W2)[:, -1, :, SUBLANES - 2:, :]
        new_buf = _ffn_from_steps(jnp.transpose(last, (0, 2, 1, 3)), F)
    return out, new_buf


def _gla_prompt(p0, alr, w2p, b_alpha, g_norm, *, B, L, H, DK, DV, tb):
    NT = L // tb
    NC = tb // CHUNK
    C = CHUNK
    scale = DK ** -0.5
    qk_blocks = H
    v_off = 2 * H * DK // DV

    def body(q_ref, k_ref, v_ref, r_ref, a_ref, w2_ref, ba_ref, gn_ref, y_ref, s_out_ref, s_scr):
        t = pl.program_id(2)

        @pl.when(t == 0)
        def _():
            s_scr[...] = jnp.zeros_like(s_scr)

        z = _dot(a_ref[...].astype(bf16), w2_ref[...].astype(bf16)) + ba_ref[...]
        gl = jax.nn.log_sigmoid(z) * (1.0 / GLA_TAU)
        bc3 = _chunk_cumsum(gl, C).reshape(NC, C, DK)
        bl3 = bc3[:, C - 1:C, :]
        q3 = (q_ref[...] * scale).reshape(NC, C, DK)
        k3 = k_ref[...].reshape(NC, C, DK)
        v3 = v_ref[...].astype(bf16).reshape(NC, C, DV)
        qd3 = (q3 * jnp.exp(bc3)).astype(bf16)
        kd3 = (k3 * jnp.exp(-bc3)).astype(bf16)
        kdec3 = (k3 * jnp.exp(bl3 - bc3)).astype(bf16)
        causal = (lax.broadcasted_iota(jnp.int32, (1, C, C), 1) >= lax.broadcasted_iota(jnp.int32, (1, C, C), 2))
        att = jnp.where(causal, jnp.einsum('cik,cjk->cij', qd3, kd3, preferred_element_type=f32), 0.0)
        intra = jnp.einsum('cij,cjv->civ', att.astype(bf16), v3, preferred_element_type=f32)
        ds = jnp.einsum('cjk,cjv->ckv', kdec3, v3, preferred_element_type=f32)
        s = s_scr[...]
        s_in = []
        for c in range(NC):
            s_in.append(s)
            s = _col_from_row(jnp.exp(bl3[c])) * s + ds[c]
        s_scr[...] = s
        s_all = jnp.stack(s_in).astype(bf16)
        o = (intra + jnp.einsum('cik,ckv->civ', qd3, s_all, preferred_element_type=f32)).reshape(tb, DV)
        rr = r_ref[...]
        y_ref[...] = (_rms(o, gn_ref[...]) * (rr * jax.nn.sigmoid(rr))).astype(bf16)

        @pl.when(t == NT - 1)
        def _():
            s_out_ref[0, 0] = s
    return pl.pallas_call(
        body,
        grid=(B, H, NT),
        in_specs=[pl.BlockSpec((tb, DK), lambda b, h, t: (b * NT + t, h)),
                  pl.BlockSpec((tb, DK), lambda b, h, t: (b * NT + t, qk_blocks + h)),
                  pl.BlockSpec((tb, DV), lambda b, h, t: (b * NT + t, v_off + h)),
                  pl.BlockSpec((tb, DV), lambda b, h, t: (b * NT + t, v_off + H + h)),
                  pl.BlockSpec((tb, LANES), lambda b, h, t: (b * NT + t, 0)),
                  pl.BlockSpec((LANES, DK), lambda b, h, t: (0, h)),
                  pl.BlockSpec((1, DK), lambda b, h, t: (0, h)),
                  pl.BlockSpec((1, DV), lambda b, h, t: (0, h))],
        out_specs=[pl.BlockSpec((tb, DV), lambda b, h, t: (b * NT + t, h)),
                   pl.BlockSpec((1, 1, DK, DV), lambda b, h, t: (b, h, 0, 0))],
        out_shape=[jax.ShapeDtypeStruct((B * L, H * DV), bf16),
                   jax.ShapeDtypeStruct((B, H, DK, DV), f32)],
        scratch_shapes=[pltpu.VMEM((DK, DV), f32)],
        compiler_params=_params(("parallel", "parallel", "arbitrary"), 32),
        name="gla_prompt",
    )(p0, p0, p0, p0, alr, w2p, b_alpha.reshape(1, -1), g_norm.reshape(1, -1))


def _gla_sample(p0, alr, w2p, b_alpha, g_norm, s0, *, H, DK, DV):
    Bs = p0.shape[0]
    scale = DK ** -0.5
    qkw = H * DK
    vw = H * DV
    assert vw % qkw == 0
    SB = SUBLANES

    def body(q_ref, k_ref, v_ref, r_ref, a_ref, w2_ref, ba_ref, gn_ref, s_ref, y_ref, so_ref, gl_scr):
        z = _dot(a_ref[...].astype(bf16), w2_ref[...].astype(bf16)) + ba_ref[...]
        gl_scr[...] = jax.nn.log_sigmoid(z) * (1.0 / GLA_TAU)

        for s in range(SB):
            row = slice(s, s + 1)
            for h in range(H):
                ks = slice(h * DK, (h + 1) * DK)
                vs = slice(h * DV, (h + 1) * DV)
                a_col = _col_from_row(jnp.exp(gl_scr[row, ks]))
                k_col = _col_from_row(k_ref[row, ks])
                q_col = _col_from_row(q_ref[row, ks] * scale)
                sn = a_col * s_ref[s, h] + k_col * v_ref[row, vs]
                so_ref[s, h] = sn
                o = jnp.sum(q_col * sn, axis=0, keepdims=True)
                rr = r_ref[row, vs]
                y_ref[row, vs] = _rms(o, gn_ref[:, vs]) * (rr * jax.nn.sigmoid(rr))

    v_blk = 2 * qkw // vw
    return pl.pallas_call(
        body,
        grid=(Bs // SB,),
        in_specs=[pl.BlockSpec((SB, qkw), lambda b: (b, 0)),
                  pl.BlockSpec((SB, qkw), lambda b: (b, 1)),
                  pl.BlockSpec((SB, vw), lambda b: (b, v_blk)),
                  pl.BlockSpec((SB, vw), lambda b: (b, v_blk + 1)),
                  pl.BlockSpec((SB, LANES), lambda b: (b, 0)),
                  pl.BlockSpec((LANES, qkw), lambda b: (0, 0)),
                  pl.BlockSpec((1, qkw), lambda b: (0, 0)),
                  pl.BlockSpec((1, vw), lambda b: (0, 0)),
                  pl.BlockSpec((SB, H, DK, DV), lambda b: (b, 0, 0, 0))],
        out_specs=[pl.BlockSpec((SB, vw), lambda b: (b, 0)),
                   pl.BlockSpec((SB, H, DK, DV), lambda b: (b, 0, 0, 0))],
        out_shape=[jax.ShapeDtypeStruct((Bs, vw), f32),
                   jax.ShapeDtypeStruct((Bs, H, DK, DV), f32)],
        scratch_shapes=[pltpu.VMEM((SB, qkw), f32)],
        compiler_params=_params(("parallel",), 32),
        name="gla_sample",
    )(p0, p0, p0, p0, alr, w2p, b_alpha.reshape(1, -1), g_norm.reshape(1, -1), s0)


def _rg_gates(xc, wr, br, wi, bi, sp):
    xb = xc.astype(bf16)
    r = jax.nn.sigmoid(_dot(xb, wr) + br)
    i = jax.nn.sigmoid(_dot(xb, wi) + bi)
    log_a = -RG_C * r * sp
    a = jnp.exp(log_a)
    mult = jnp.sqrt(1.0 - jnp.exp(2.0 * log_a))
    return a, mult, i


def _rglru_prompt(p0, conv_w, conv_b, w_r, b_r, w_i, b_i, lam, *, B, L, x_blk, g_blk):
    M = B * L
    NB, BS, _ = w_r.shape
    assert BS == LANES
    rc = min(256, L)

    def body(x_ref, gg_ref, cw_ref, cb_ref, wr_ref, br_ref, wi_ref, bi_ref, lam_ref, y_ref, hl_ref, a_scr, b_scr):
        wr = wr_ref[...].astype(bf16)
        wi = wi_ref[...].astype(bf16)
        sp = jax.nn.softplus(-lam_ref[...])
        cw = cw_ref[...]

        def chunk(c, carry):
            r0 = pl.multiple_of(c * rc, rc)
            rows = pl.ds(r0, rc)
            x = x_ref[rows, :]
            start = (r0 % L) == 0
            prev = x_ref[pl.ds(pl.multiple_of(jnp.maximum(r0 - SUBLANES, 0), SUBLANES), SUBLANES), :]
            prev = jnp.where(start, 0.0, prev)
            xc = (cb_ref[...] + cw[0:1] * _shifted(prev, x, 3) + cw[1:2] * _shifted(prev, x, 2)
                  + cw[2:3] * _shifted(prev, x, 1) + cw[3:4] * x)
            a, mult, ig = _rg_gates(xc, wr, br_ref[...], wi, bi_ref[...], sp)
            pos = (r0 + lax.broadcasted_iota(jnp.int32, (rc, 1), 0)) % L
            mult = jnp.where(pos == 0, 1.0, mult)
            a_scr[rows, :] = a
            b_scr[rows, :] = mult * (ig * xc)
            return carry
        lax.fori_loop(0, M // rc, chunk, 0)

        def step(t, hs):
            new = []
            for b in range(B):
                row = pl.ds(b * L + t, 1)
                h = a_scr[row, :] * hs[b] + b_scr[row, :]
                b_scr[row, :] = h
                new.append(h)
            return tuple(new)
        hs = lax.fori_loop(0, L, step, tuple(jnp.zeros((1, LANES), f32) for _ in range(B)))
        hl_ref[...] = jnp.concatenate(hs, axis=0)

        def outc(c, carry):
            rows = pl.ds(pl.multiple_of(c * rc, rc), rc)
            y_ref[rows, :] = (b_scr[rows, :] * jax.nn.gelu(gg_ref[rows, :])).astype(bf16)
            return carry
        lax.fori_loop(0, M // rc, outc, 0)

    W = NB * BS
    return pl.pallas_call(
        body,
        grid=(NB,),
        in_specs=[pl.BlockSpec((M, LANES), lambda n: (0, x_blk + n)),
                  pl.BlockSpec((M, LANES), lambda n: (0, g_blk + n)),
                  pl.BlockSpec((4, LANES), lambda n: (0, n)),
                  pl.BlockSpec((1, LANES), lambda n: (0, n)),
                  pl.BlockSpec((None, BS, BS), lambda n: (n, 0, 0)),
                  pl.BlockSpec((1, LANES), lambda n: (0, n)),
                  pl.BlockSpec((None, BS, BS), lambda n: (n, 0, 0)),
                  pl.BlockSpec((1, LANES), lambda n: (0, n)),
                  pl.BlockSpec((1, LANES), lambda n: (0, n))],
        out_specs=[pl.BlockSpec((M, LANES), lambda n: (0, n)),
                   pl.BlockSpec((B, LANES), lambda n: (0, n))],
        out_shape=[jax.ShapeDtypeStruct((M, W), bf16), jax.ShapeDtypeStruct((B, W), f32)],
        scratch_shapes=[pltpu.VMEM((M, LANES), f32), pltpu.VMEM((M, LANES), f32)],
        compiler_params=_params(("parallel",), 48),
        name="rglru_prompt",
    )(p0, p0, conv_w, conv_b.reshape(1, W), w_r, b_r.reshape(1, W), w_i, b_i.reshape(1, W), lam.reshape(1, W))


def _rglru_sample(p0, conv_state, h0, conv_w, conv_b, w_r, b_r, w_i, b_i, lam, *, x_blk, g_blk):
    Bs = p0.shape[0]
    NB, BS, _ = w_r.shape
    W = NB * BS
    s0, s1, s2 = conv_state[:, 0], conv_state[:, 1], conv_state[:, 2]

    def body(x_ref, gg_ref, s0_ref, s1_ref, s2_ref, h0_ref, cw_ref, cb_ref, wr_ref, br_ref, wi_ref, bi_ref, lam_ref,
             y_ref, h_ref):
        cw = cw_ref[...]
        x = x_ref[...]
        xc = cb_ref[...] + cw[0:1] * s0_ref[...] + cw[1:2] * s1_ref[...] + cw[2:3] * s2_ref[...] + cw[3:4] * x
        sp = jax.nn.softplus(-lam_ref[...])
        a, mult, ig = _rg_gates(xc, wr_ref[...].astype(bf16), br_ref[...], wi_ref[...].astype(bf16), bi_ref[...], sp)
        if PAST_LEN == 0:
            mult = jnp.ones_like(mult)
        h = a * h0_ref[...] + mult * (ig * xc)
        h_ref[...] = h
        y_ref[...] = (h * jax.nn.gelu(gg_ref[...])).astype(bf16)

    blk = lambda n: (0, n)
    vec = pl.BlockSpec((1, LANES), blk)
    mat = pl.BlockSpec((Bs, LANES), blk)
    y, h = pl.pallas_call(
        body,
        grid=(NB,),
        in_specs=[pl.BlockSpec((Bs, LANES), lambda n: (0, x_blk + n)),
                  pl.BlockSpec((Bs, LANES), lambda n: (0, g_blk + n)),
                  mat, mat, mat, mat,
                  pl.BlockSpec((4, LANES), blk), vec,
                  pl.BlockSpec((None, BS, BS), lambda n: (n, 0, 0)), vec,
                  pl.BlockSpec((None, BS, BS), lambda n: (n, 0, 0)), vec, vec],
        out_specs=[mat, mat],
        out_shape=[jax.ShapeDtypeStruct((Bs, W), bf16), jax.ShapeDtypeStruct((Bs, W), f32)],
        compiler_params=_params(("parallel",), 32),
        name="rglru_sample",
    )(p0, p0, s0, s1, s2, h0, conv_w, conv_b.reshape(1, W), w_r, b_r.reshape(1, W), w_i, b_i.reshape(1, W),
      lam.reshape(1, W))
    return y, h


def _blockdiag_tiles(w, tile):
    nblk, bs, _ = w.shape
    per = tile // bs
    nt = nblk // per
    w4 = w.reshape(nt, per, bs, bs)
    eye = jnp.eye(per, dtype=w.dtype)
    return jnp.einsum('tncd,nm->tncmd', w4, eye).reshape(nt, tile, tile)


def _mlstm_stage1(p1, conv_w, conv_b, wq_t, wk_t, wv_t, wg, bg, *, seq_len, tr, H, DH, conv_state=None):
    M = p1.shape[0]
    W = H * DH
    NTL, TL, _ = wq_t.shape
    sample = conv_state is not None
    tiles_per_seq = max(seq_len // tr, 1)
    kscale = DH ** -0.5

    def body(*refs):
        it = iter(refs)
        x_ref = next(it)
        if sample:
            s0_ref, s1_ref, s2_ref = next(it), next(it), next(it)
        else:
            xh_ref = next(it)
        cw_ref, cb_ref, wq_ref, wk_ref, wv_ref, wg_ref, bg_ref = (next(it) for _ in range(7))
        q_ref, k_ref, v_ref, g_ref, xc_ref = (next(it) for _ in range(5))
        cw = cw_ref[...]
        x = x_ref[...]
        if sample:
            conv = cb_ref[...] + cw[0:1] * s0_ref[...] + cw[1:2] * s1_ref[...] + cw[2:3] * s2_ref[...] + cw[3:4] * x
        else:
            keep = (pl.program_id(0) % tiles_per_seq != 0).astype(f32)
            prev = xh_ref[...] * keep
            conv = (cb_ref[...] + cw[0:1] * _shifted(prev, x, 3) + cw[1:2] * _shifted(prev, x, 2)
                    + cw[2:3] * _shifted(prev, x, 1) + cw[3:4] * x)
        xc = conv * jax.nn.sigmoid(conv)
        xc_ref[...] = xc
        xcb = xc.astype(bf16)
        xb = x.astype(bf16)
        qs, ks, vs = [], [], []
        for t in range(NTL):
            cs = slice(t * TL, (t + 1) * TL)
            qs.append(_dot(xcb[:, cs], wq_ref[t]))
            ks.append(_dot(xcb[:, cs], wk_ref[t]) * kscale)
            vs.append(_dot(xb[:, cs], wv_ref[t]))
        q = jnp.concatenate(qs, axis=1)
        k = jnp.concatenate(ks, axis=1)
        v = jnp.concatenate(vs, axis=1)
        q_ref[...] = q.astype(q_ref.dtype)
        k_ref[...] = k.astype(k_ref.dtype)
        v_ref[...] = v.astype(v_ref.dtype)
        gt = (_dot(q.astype(bf16), wg_ref[0:W, :]) + _dot(k.astype(bf16), wg_ref[W:2 * W, :])
              + _dot(v.astype(bf16), wg_ref[2 * W:3 * W, :]) + bg_ref[...])
        lane = lax.broadcasted_iota(jnp.int32, gt.shape, 1)
        g_ref[...] = jnp.where(jnp.logical_and(lane >= H, lane < 2 * H), jax.nn.log_sigmoid(gt), gt)

    row = lambda i: (i, 0)
    const2 = lambda i: (0, 0)
    const3 = lambda i: (0, 0, 0)
    in_specs = [pl.BlockSpec((tr, W), row)]
    args = [p1]
    if sample:
        in_specs += [pl.BlockSpec((tr, W), row)] * 3
        args += [conv_state[:, 0], conv_state[:, 1], conv_state[:, 2]]
    else:
        in_specs.append(pl.BlockSpec((SUBLANES, W), lambda i: (jnp.maximum(i * (tr // SUBLANES) - 1, 0), 0)))
        args.append(p1)
    in_specs += [pl.BlockSpec((4, W), const2), pl.BlockSpec((1, W), const2),
                 pl.BlockSpec((NTL, TL, TL), const3), pl.BlockSpec((NTL, TL, TL), const3),
                 pl.BlockSpec((NTL, TL, TL), const3),
                 pl.BlockSpec((3 * W, LANES), const2), pl.BlockSpec((1, LANES), const2)]
    args += [conv_w, conv_b.reshape(1, W), wq_t, wk_t, wv_t, wg, bg]
    return pl.pallas_call(
        body,
        grid=(M // tr,),
        in_specs=in_specs,
        out_specs=[pl.BlockSpec((tr, W), row)] * 3 + [pl.BlockSpec((tr, LANES), row), pl.BlockSpec((tr, W), row)],
        out_shape=[jax.ShapeDtypeStruct((M, W), f32 if sample else bf16)] * 3
        + [jax.ShapeDtypeStruct((M, LANES), f32), jax.ShapeDtypeStruct((M, W), f32)],
        compiler_params=_params(("parallel",), 48),
        name="mlstm_stage1_sample" if sample else "mlstm_stage1",
    )(*args)


def _mlstm_prompt(q, k, v, gates, xc, p1, g_norm, skip, *, B, L, H, DH, tb):
    NT = L // tb
    NC = tb // CHUNK
    C = CHUNK

    def body(q_ref, k_ref, v_ref, g_ref, xc_ref, om_ref, gn_ref, sk_ref,
             y_ref, c_out, n_out, m_out, c_scr, n_scr, m_scr):
        hh = pl.program_id(1)
        t = pl.program_id(2)

        @pl.when(t == 0)
        def _():
            c_scr[...] = jnp.zeros_like(c_scr)
            n_scr[...] = jnp.zeros_like(n_scr)
            m_scr[...] = jnp.zeros_like(m_scr)

        gts = g_ref[...]
        lane = lax.broadcasted_iota(jnp.int32, (tb, LANES), 1)
        i_col = jnp.sum(jnp.where(lane == hh, gts, 0.0), axis=1, keepdims=True)
        b_col = jnp.sum(jnp.where(lane == H + hh, _chunk_cumsum(gts, C), 0.0), axis=1, keepdims=True)
        b3 = b_col.reshape(NC, C, 1)
        i3 = i_col.reshape(NC, C, 1)
        ii = lax.broadcasted_iota(jnp.int32, (1, C, C), 1)
        jj = lax.broadcasted_iota(jnp.int32, (1, C, C), 2)
        eye = ii == jj
        causal = ii >= jj
        as_row = lambda col3: jnp.sum(jnp.where(eye, jnp.broadcast_to(col3, (NC, C, C)), 0.0), axis=1, keepdims=True)
        dmat = jnp.where(causal, b3 - as_row(b3) + as_row(i3), -jnp.inf)
        rmax = jnp.max(dmat, axis=2, keepdims=True)
        b_last = b3[:, C - 1:C, :]
        m_prev = m_scr[...]
        m_in = []
        for c in range(NC):
            m_in.append(m_prev)
            m_prev = jnp.maximum(b_last[c] + m_prev, rmax[c][C - 1:C, :])
        m_scr[...] = m_prev
        inter = b3 + jnp.stack(m_in)
        m_col = jnp.maximum(inter, rmax)
        g_col = jnp.exp(inter - m_col)
        q3 = q_ref[...].reshape(NC, C, DH)
        k3 = k_ref[...].reshape(NC, C, DH)
        v3 = v_ref[...].reshape(NC, C, DH)
        s = jnp.einsum('cid,cjd->cij', q3, k3, preferred_element_type=f32) * jnp.exp(dmat - m_col)
        num = jnp.einsum('cij,cje->cie', s.astype(bf16), v3, preferred_element_type=f32)
        den = jnp.sum(s, axis=2, keepdims=True)
        m_new = m_col[:, C - 1:C, :]
        wk = jnp.exp(b_last - b3 + i3 - m_new)
        gc = jnp.exp(inter[:, C - 1:C, :] - m_new)
        kw = k3.astype(f32) * wk
        dc = jnp.einsum('cse,csd->ced', v3, kw.astype(bf16), preferred_element_type=f32)
        dn = jnp.sum(kw, axis=1, keepdims=True)
        cs = c_scr[...]
        ns = n_scr[...]
        c_in, n_in = [], []
        for c in range(NC):
            c_in.append(cs)
            n_in.append(ns)
            cs = gc[c] * cs + dc[c]
            ns = gc[c] * ns + dn[c]
        c_scr[...] = cs
        n_scr[...] = ns
        c_all = jnp.stack(c_in).astype(bf16)
        n_all = jnp.stack(n_in)
        num = num + g_col * jnp.einsum('cid,ced->cie', q3, c_all, preferred_element_type=f32)
        den = den + g_col * jnp.sum(q3.astype(f32) * n_all, axis=2, keepdims=True)
        hm = (num / jnp.maximum(jnp.abs(den), jnp.exp(-m_col))).reshape(tb, DH)
        y = (_rms(hm, gn_ref[...]) + sk_ref[...] * xc_ref[...]) * jax.nn.sigmoid(om_ref[...])
        y_ref[...] = y.astype(bf16)

        @pl.when(t == NT - 1)
        def _():
            c_out[0, 0] = cs
            n_out[0, 0] = ns
            m_out[0, 0] = m_prev

    blk = lambda b, h, t: (b * NT + t, h)
    W = H * DH
    y, c_new, n_new, m_new = pl.pallas_call(
        body,
        grid=(B, H, NT),
        in_specs=[pl.BlockSpec((tb, DH), blk), pl.BlockSpec((tb, DH), blk), pl.BlockSpec((tb, DH), blk),
                  pl.BlockSpec((tb, LANES), lambda b, h, t: (b * NT + t, 0)),
                  pl.BlockSpec((tb, DH), blk),
                  pl.BlockSpec((tb, DH), lambda b, h, t: (b * NT + t, H + h)),
                  pl.BlockSpec((1, DH), lambda b, h, t: (0, h)),
                  pl.BlockSpec((1, DH), lambda b, h, t: (0, h))],
        out_specs=[pl.BlockSpec((tb, DH), blk),
                   pl.BlockSpec((1, 1, DH, DH), lambda b, h, t: (b, h, 0, 0)),
                   pl.BlockSpec((1, 1, 1, DH), lambda b, h, t: (b, h, 0, 0)),
                   pl.BlockSpec((1, 1, 1, 1), lambda b, h, t: (b, h, 0, 0))],
        out_shape=[jax.ShapeDtypeStruct((B * L, W), bf16),
                   jax.ShapeDtypeStruct((B, H, DH, DH), f32),
                   jax.ShapeDtypeStruct((B, H, 1, DH), f32),
                   jax.ShapeDtypeStruct((B, H, 1, 1), f32)],
        scratch_shapes=[pltpu.VMEM((DH, DH), f32), pltpu.VMEM((1, DH), f32), pltpu.VMEM((1, 1), f32)],
        compiler_params=_params(("parallel", "parallel", "arbitrary"), 32),
        name="mlstm_prompt",
    )(q, k, v, gates, xc, p1, g_norm.reshape(1, W), skip.reshape(1, W))
    return y, c_new, n_new.reshape(B, H, DH), m_new.reshape(B, H)


def _mlstm_sample(q, k, v, gates, xc, p1, g_norm, skip, c0, n0, m0, *, H, DH):
    Bs = q.shape[0]
    W = H * DH
    SB = SUBLANES

    def body(q_ref, k_ref, v_ref, g_ref, xc_ref, om_ref, gn_ref, sk_ref, c_ref, n_ref, m_ref,
             y_ref, c_out, n_out, m_out):
        for s in range(SB):
            row = slice(s, s + 1)
            for h in range(H):
                cs = slice(h * DH, (h + 1) * DH)
                ig = g_ref[row, h:h + 1]
                fg = g_ref[row, H + h:H + h + 1]
                inter = fg + m_ref[row, h:h + 1]
                m = jnp.maximum(inter, ig)
                g = jnp.exp(inter - m)
                qr = q_ref[row, cs]
                kw = k_ref[row, cs] * jnp.exp(ig - m)
                v_col = _col_from_row(v_ref[row, cs])
                cn = g * c_ref[s, h] + v_col * kw
                nn = g * n_ref[row, cs] + kw
                c_out[s, h] = cn
                n_out[row, cs] = nn
                m_out[row, h:h + 1] = m
                num = _row_from_col(jnp.sum(cn * qr, axis=1, keepdims=True))
                den = jnp.sum(nn * qr, axis=1, keepdims=True)
                hm = num / jnp.maximum(jnp.abs(den), jnp.exp(-m))
                y_ref[row, cs] = ((_rms(hm, gn_ref[:, cs]) + sk_ref[:, cs] * xc_ref[row, cs])
                                  * jax.nn.sigmoid(om_ref[row, cs]))

    per = lambda b: (b, 0)
    const2 = lambda b: (0, 0)
    y, c_new, n_new, m_new = pl.pallas_call(
        body,
        grid=(Bs // SB,),
        in_specs=[pl.BlockSpec((SB, W), per), pl.BlockSpec((SB, W), per), pl.BlockSpec((SB, W), per),
                  pl.BlockSpec((SB, LANES), per), pl.BlockSpec((SB, W), per),
                  pl.BlockSpec((SB, W), lambda b: (b, 1)),
                  pl.BlockSpec((1, W), const2), pl.BlockSpec((1, W), const2),
                  pl.BlockSpec((SB, H, DH, DH), lambda b: (b, 0, 0, 0)),
                  pl.BlockSpec((SB, W), per), pl.BlockSpec((SB, H), per)],
        out_specs=[pl.BlockSpec((SB, W), per),
                   pl.BlockSpec((SB, H, DH, DH), lambda b: (b, 0, 0, 0)),
                   pl.BlockSpec((SB, W), per), pl.BlockSpec((SB, H), per)],
        out_shape=[jax.ShapeDtypeStruct((Bs, W), f32), jax.ShapeDtypeStruct((Bs, H, DH, DH), f32),
                   jax.ShapeDtypeStruct((Bs, W), f32), jax.ShapeDtypeStruct((Bs, H), f32)],
        compiler_params=_params(("parallel",), 48),
        name="mlstm_sample",
    )(q, k, v, gates, xc, p1, g_norm.reshape(1, W), skip.reshape(1, W), c0, n0.reshape(Bs, W), m0)
    return y, c_new, n_new.reshape(Bs, H, DH), m_new


S5_TILES = 8


def _s5_layouts(lam_re, lam_im, log_dt, b_re, b_im, c_re, c_im):
    G, P = lam_re.shape
    GC = b_re.shape[2]
    T = S5_TILES
    gpt = G // T
    ns = G * P
    flat = lambda a: a.reshape(ns)
    ldt = jnp.broadcast_to(log_dt[:, None], (G, P))
    rows = [flat(a).reshape(T, 1, ns // T) for a in (lam_re, lam_im, ldt)]
    eye = jnp.eye(gpt, dtype=f32)
    bbd = [jnp.einsum('jgpc,gh->jgchp', a.reshape(T, gpt, P, GC), eye).reshape(T, gpt * GC, gpt * P) for a in (b_re, b_im)]
    cbd = [jnp.einsum('jgcp,gh->jgphc', a.reshape(T, gpt, GC, P), eye).reshape(T, gpt * P, gpt * GC) for a in (c_re, c_im)]
    return rows, bbd, cbd


def _s5_discretise(lre, lim, ldt):
    dt = jnp.exp(ldt)
    mag = jnp.exp(dt * lre)
    ar = mag * jnp.cos(dt * lim)
    ai = mag * jnp.sin(dt * lim)
    den = lre * lre + lim * lim
    cr = ((ar - 1.0) * lre + ai * lim) / den
    ci = (ai * lre - (ar - 1.0) * lim) / den
    return ar, ai, cr, ci


def _s5_mixer(p1, u_blk, rows, bbd, cbd, d_skip, w_glu, b_glu, *, B, L, tb, state=None):
    T = S5_TILES
    lre_r, lim_r, ldt_r = rows
    SW = lre_r.shape[2]
    CW = bbd[0].shape[1]
    W = T * CW
    NS = T * SW
    KT = SW // LANES
    sample = state is not None
    NT = 1 if sample else L // tb
    M = B * L

    def body(*refs):
        it = iter(refs)
        u_ref = next(it)
        if sample:
            x0r_ref, x0i_ref = next(it), next(it)
        lre_ref, lim_ref, ldt_ref = next(it), next(it), next(it)
        bre_ref, bim_ref, cre_ref, cim_ref = next(it), next(it), next(it), next(it)
        d_ref, wg_ref, bgl_ref = next(it), next(it), next(it)
        y_ref, xr_out, xi_out = next(it), next(it), next(it)
        bbr, bbi, cbr, cbi, ar_scr, ai_scr = (next(it) for _ in range(6))
        if not sample:
            sre, sim, xr_c, xi_c, yacc = (next(it) for _ in range(5))
        first = jnp.logical_and(pl.program_id(0) == 0, pl.program_id(1) == 0)

        @pl.when(first)
        def _():
            for j in range(T):
                ar, ai, cr, ci = _s5_discretise(lre_ref[j], lim_ref[j], ldt_ref[j])
                ar_scr[j] = ar
                ai_scr[j] = ai
                br = bre_ref[j]
                bi = bim_ref[j]
                bbr[j] = (cr * br - ci * bi).astype(bf16)
                bbi[j] = (cr * bi + ci * br).astype(bf16)
                cbr[j] = cre_ref[j].astype(bf16)
                cbi[j] = cim_ref[j].astype(bf16)

        u = u_ref[...]
        ub = u.astype(bf16)
        ys = []
        if sample:
            for j in range(T):
                cs = slice(j * SW, (j + 1) * SW)
                uj = ub[:, j * CW:(j + 1) * CW]
                ar = ar_scr[j]
                ai = ai_scr[j]
                x0r = x0r_ref[:, cs]
                x0i = x0i_ref[:, cs]
                xr = ar * x0r - ai * x0i + _dot(uj, bbr[j])
                xi = ar * x0i + ai * x0r + _dot(uj, bbi[j])
                xr_out[:, cs] = xr
                xi_out[:, cs] = xi
                ys.append(_dot(xr.astype(bf16), cbr[j]) - _dot(xi.astype(bf16), cbi[j]))
            y = jnp.concatenate(ys, axis=1)
        else:
            t = pl.program_id(1)

            @pl.when(t == 0)
            def _():
                xr_c[...] = jnp.zeros_like(xr_c)
                xi_c[...] = jnp.zeros_like(xi_c)

            for j in range(T):
                uj = ub[:, j * CW:(j + 1) * CW]
                r = _dot(uj, bbr[j])
                im = _dot(uj, bbi[j])
                for kk in range(KT):
                    sre[kk, pl.ds(j, tb, stride=T), :] = r[:, kk * LANES:(kk + 1) * LANES]
                    sim[kk, pl.ds(j, tb, stride=T), :] = im[:, kk * LANES:(kk + 1) * LANES]
            a_r = [jnp.concatenate([ar_scr[j][:, kk * LANES:(kk + 1) * LANES] for j in range(T)], axis=0) for kk in range(KT)]
            a_i = [jnp.concatenate([ai_scr[j][:, kk * LANES:(kk + 1) * LANES] for j in range(T)], axis=0) for kk in range(KT)]

            def step(s, carry):
                xr, xi = carry
                row = pl.ds(pl.multiple_of(s * T, T), T)
                nr, ni = [], []
                for kk in range(KT):
                    r_ = a_r[kk] * xr[kk] - a_i[kk] * xi[kk] + sre[kk, row, :]
                    i_ = a_r[kk] * xi[kk] + a_i[kk] * xr[kk] + sim[kk, row, :]
                    sre[kk, row, :] = r_
                    sim[kk, row, :] = i_
                    nr.append(r_)
                    ni.append(i_)
                return tuple(nr), tuple(ni)
            xr0 = tuple(xr_c[kk] for kk in range(KT))
            xi0 = tuple(xi_c[kk] for kk in range(KT))
            xr, xi = lax.fori_loop(0, tb, step, (xr0, xi0))
            for kk in range(KT):
                xr_c[kk] = xr[kk]
                xi_c[kk] = xi[kk]

            @pl.when(t == NT - 1)
            def _():
                for kk in range(KT):
                    xr_out[kk] = xr[kk]
                    xi_out[kk] = xi[kk]

            for j in range(T):
                xrj = jnp.concatenate([sre[kk, pl.ds(j, tb, stride=T), :] for kk in range(KT)], axis=1).astype(bf16)
                xij = jnp.concatenate([sim[kk, pl.ds(j, tb, stride=T), :] for kk in range(KT)], axis=1).astype(bf16)
                yacc[:, j * CW:(j + 1) * CW] = _dot(xrj, cbr[j]) - _dot(xij, cbi[j])
            y = yacc[...]
        ysk = jax.nn.gelu(y + d_ref[...] * u)
        z = _dot(ysk.astype(bf16), wg_ref[...]) + bgl_ref[...]
        y_ref[...] = (ysk * jax.nn.sigmoid(z)).astype(bf16)

    c3 = lambda b, t: (0, 0, 0)
    c2 = lambda b, t: (0, 0)
    in_specs = [pl.BlockSpec((tb, W), lambda b, t: (b * NT + t, u_blk))]
    args = [p1]
    if sample:
        in_specs += [pl.BlockSpec((tb, NS), lambda b, t: (b, 0))] * 2
        args += [state[0], state[1]]
    in_specs += [pl.BlockSpec((T, 1, SW), c3)] * 3
    in_specs += [pl.BlockSpec((T, CW, SW), c3)] * 2 + [pl.BlockSpec((T, SW, CW), c3)] * 2
    in_specs += [pl.BlockSpec((1, W), c2), pl.BlockSpec((W, W), c2), pl.BlockSpec((1, W), c2)]
    args += [lre_r, lim_r, ldt_r, bbd[0], bbd[1], cbd[0], cbd[1], d_skip.reshape(1, W), w_glu, b_glu.reshape(1, W)]
    scratch = [pltpu.VMEM((T, CW, SW), bf16), pltpu.VMEM((T, CW, SW), bf16),
               pltpu.VMEM((T, SW, CW), bf16), pltpu.VMEM((T, SW, CW), bf16),
               pltpu.VMEM((T, 1, SW), f32), pltpu.VMEM((T, 1, SW), f32)]
    if sample:
        grid = (M // tb, 1)
        st_spec = pl.BlockSpec((tb, NS), lambda b, t: (b, 0))
        st_shape = jax.ShapeDtypeStruct((M, NS), f32)
    else:
        grid = (B, NT)
        st_spec = pl.BlockSpec((None, KT, T, LANES), lambda b, t: (b, 0, 0, 0))
        st_shape = jax.ShapeDtypeStruct((B, KT, T, LANES), f32)
        scratch += [pltpu.VMEM((KT, tb * T, LANES), f32), pltpu.VMEM((KT, tb * T, LANES), f32),
                    pltpu.VMEM((KT, T, LANES), f32), pltpu.VMEM((KT, T, LANES), f32), pltpu.VMEM((tb, W), f32)]
    y, xr, xi = pl.pallas_call(
        body,
        grid=grid,
        in_specs=in_specs,
        out_specs=[pl.BlockSpec((tb, W), lambda b, t: (b * NT + t, 0)), st_spec, st_spec],
        out_shape=[jax.ShapeDtypeStruct((M, W), bf16), st_shape, st_shape],
        scratch_shapes=scratch,
        compiler_params=_params(("arbitrary", "arbitrary"), 56),
        name="s5_sample" if sample else "s5_prompt",
    )(*args)
    if not sample:
        xr = jnp.transpose(xr, (0, 2, 1, 3)).reshape(B, NS)
        xi = jnp.transpose(xi, (0, 2, 1, 3)).reshape(B, NS)
    return y, xr, xi


def _trunk(x3, st, w, *, sample):
    B, L, D = x3.shape
    M = B * L
    x = x3.reshape(M, D)
    H_g, DK, DV = st['gla_S'].shape[1:] if sample else w['gla_dims']
    H_m, DH = w['ml_dims']
    G, P = w['s5_dims']
    tm = M if sample else min(1024, L)
    tf = M if sample else min(1024, L)
    tb = min(512, L)
    out = {}

    p0 = _norm_matmul(x, w['g_mix0'], w['w_in0_main'], tm=tm, tn=512)
    alr = _norm_matmul(x, w['g_mix0'], w['w_in0_alr'], tm=tm, tn=LANES)
    x_blk = (2 * H_g * DK + 2 * H_g * DV) // LANES
    W_rg = w['rg_lambda'].shape[0]
    g_blk = x_blk + W_rg // LANES
    rg_tail = p0.reshape(B, L, p0.shape[1])[:, max(L - 3, 0):, x_blk * LANES:x_blk * LANES + W_rg]
    if sample:
        ya, out['gla_S'] = _gla_sample(p0, alr, w['gla_w2p'], w['gla_b_alpha'], w['gla_g_norm'], st['gla_S'],
                                       H=H_g, DK=DK, DV=DV)
        yb, out['rg_h'] = _rglru_sample(p0, st['rg_conv'], st['rg_h'], w['rg_conv_w'], w['rg_conv_b'], w['rg_w_r'],
                                        w['rg_b_r'], w['rg_w_i'], w['rg_b_i'], w['rg_lambda'], x_blk=x_blk, g_blk=g_blk)
        out['rg_conv'] = jnp.concatenate([st['rg_conv'][:, 1:], rg_tail], axis=1)
    else:
        ya, out['gla_S'] = _gla_prompt(p0, alr, w['gla_w2p'], w['gla_b_alpha'], w['gla_g_norm'],
                                       B=B, L=L, H=H_g, DK=DK, DV=DV, tb=tb)
        yb, out['rg_h'] = _rglru_prompt(p0, w['rg_conv_w'], w['rg_conv_b'], w['rg_w_r'], w['rg_b_r'], w['rg_w_i'],
                                        w['rg_b_i'], w['rg_lambda'], B=B, L=L, x_blk=x_blk, g_blk=g_blk)
        out['rg_conv'] = rg_tail
    x = _out_proj(ya, yb, w['w_out0'], x, tm=tm, tn=512)
    x, ffn0 = _conv_ffn(x, w['g_ffn'][0], w['ffn'][0], seq_len=L, tm=tf, prev=jnp.transpose(st['ffn_conv'][0], (1, 0, 2)) if sample else None)

    W_ml = H_m * DH
    p1 = _norm_matmul(x, w['g_mix1'], w['w_in1'], tm=tm, tn=512)
    xm_tail = p1.reshape(B, L, p1.shape[1])[:, max(L - 3, 0):, :W_ml]
    q, k, v, gates, xc = _mlstm_stage1(p1, w['ml_conv_w'], w['ml_conv_b'], w['ml_wq_t'], w['ml_wk_t'], w['ml_wv_t'],
                                       w['ml_wg'], w['ml_bg'], seq_len=L, tr=M if sample else min(256, L), H=H_m, DH=DH,
                                       conv_state=st['ml_conv'] if sample else None)
    u_blk = 2 * W_ml // (G * w['s5_gc'])
    if sample:
        yc, out['ml_C'], out['ml_n'], out['ml_m'] = _mlstm_sample(
            q, k, v, gates, xc, p1, w['ml_g_norm'], w['ml_skip'], st['ml_C'], st['ml_n'], st['ml_m'], H=H_m, DH=DH)
        out['ml_conv'] = jnp.concatenate([st['ml_conv'][:, 1:], xm_tail], axis=1)
        yd, s5r, s5i = _s5_mixer(p1, u_blk, w['s5_rows'], w['s5_bbd'], w['s5_cbd'], w['s5_D'], w['s5_w_glu'],
                                 w['s5_b_glu'], B=B, L=1, tb=B,
                                 state=(st['s5_re'].reshape(B, G * P), st['s5_im'].reshape(B, G * P)))
    else:
        yc, out['ml_C'], out['ml_n'], out['ml_m'] = _mlstm_prompt(
            q, k, v, gates, xc, p1, w['ml_g_norm'], w['ml_skip'], B=B, L=L, H=H_m, DH=DH, tb=tb)
        out['ml_conv'] = xm_tail
        yd, s5r, s5i = _s5_mixer(p1, u_blk, w['s5_rows'], w['s5_bbd'], w['s5_cbd'], w['s5_D'], w['s5_w_glu'],
                                 w['s5_b_glu'], B=B, L=L, tb=min(256, L))
    out['s5_re'] = s5r.reshape(B, G, P)
    out['s5_im'] = s5i.reshape(B, G, P)
    x = _out_proj(yc, yd, w['w_out1'], x, tm=tm, tn=512)
    x, ffn1 = _conv_ffn(x, w['g_ffn'][1], w['ffn'][1], seq_len=L, tm=tf, prev=jnp.transpose(st['ffn_conv'][1], (1, 0, 2)) if sample else None,
                        final_g=w['g_final'])
    out['ffn_conv'] = jnp.stack([ffn0, ffn1], axis=0)
    return x.reshape(B, L, D), out


def kernel(x_prompt, x_sample, state_gla_S, state_rglru_h, state_rglru_conv, state_mlstm_C, state_mlstm_n, state_mlstm_m, state_mlstm_conv, state_s5_re, state_s5_im, state_ffn_conv, g_mix0, w_in0, gla_w_alpha2, gla_b_alpha, gla_g_norm, rg_conv_w, rg_conv_b, rg_w_r, rg_b_r, rg_w_i, rg_b_i, rg_lambda, w_out0, g_mix1, w_in1, ml_conv_w, ml_conv_b, ml_wq, ml_wk, ml_wv, ml_w_igate, ml_b_igate, ml_w_fgate, ml_b_fgate, ml_g_norm, ml_skip, s5_lam_re, s5_lam_im, s5_log_dt, s5_B_re, s5_B_im, s5_C_re, s5_C_im, s5_D, s5_w_glu, s5_b_glu, w_out1, g_ffn, ffn_w_up, ffn_conv_w, ffn_conv_b, ffn_w_down, g_final):
    _, H_g, DK, DV = state_gla_S.shape
    _, H_m, DH, _ = state_mlstm_C.shape
    G, P = s5_lam_re.shape
    rank = gla_w_alpha2.shape[0]
    n_main = 2 * H_g * DK + 2 * H_g * DV
    w_in0_main = jnp.concatenate([w_in0[:, :n_main], w_in0[:, n_main + rank:]], axis=1).astype(bf16)
    w_in0_alr = jnp.pad(w_in0[:, n_main:n_main + rank], ((0, 0), (0, LANES - rank))).astype(bf16)
    gla_w2p = jnp.pad(gla_w_alpha2, ((0, LANES - rank), (0, 0)))
    ml_tile = 256
    ml_wg = jnp.pad(jnp.concatenate([ml_w_igate, ml_w_fgate], axis=1), ((0, 0), (0, LANES - 2 * H_m))).astype(bf16)
    ml_bg = jnp.pad(jnp.concatenate([ml_b_igate, ml_b_fgate]), (0, LANES - 2 * H_m)).reshape(1, LANES)
    rows, bbd, cbd = _s5_layouts(s5_lam_re, s5_lam_im, s5_log_dt, s5_B_re, s5_B_im, s5_C_re, s5_C_im)
    w = dict(
        g_mix0=g_mix0, w_in0_main=w_in0_main, w_in0_alr=w_in0_alr, gla_w2p=gla_w2p, gla_b_alpha=gla_b_alpha,
        gla_g_norm=gla_g_norm, gla_dims=(H_g, DK, DV), rg_conv_w=rg_conv_w, rg_conv_b=rg_conv_b, rg_w_r=rg_w_r,
        rg_b_r=rg_b_r, rg_w_i=rg_w_i, rg_b_i=rg_b_i, rg_lambda=rg_lambda, w_out0=w_out0.astype(bf16),
        g_mix1=g_mix1, w_in1=w_in1.astype(bf16), ml_conv_w=ml_conv_w, ml_conv_b=ml_conv_b,
        ml_wq_t=_blockdiag_tiles(ml_wq, ml_tile).astype(bf16), ml_wk_t=_blockdiag_tiles(ml_wk, ml_tile).astype(bf16),
        ml_wv_t=_blockdiag_tiles(ml_wv, ml_tile).astype(bf16), ml_wg=ml_wg, ml_bg=ml_bg, ml_g_norm=ml_g_norm,
        ml_skip=ml_skip, ml_dims=(H_m, DH), s5_dims=(G, P), s5_gc=s5_B_re.shape[2], s5_rows=rows, s5_bbd=bbd,
        s5_cbd=cbd, s5_D=s5_D, s5_w_glu=s5_w_glu.astype(bf16), s5_b_glu=s5_b_glu, w_out1=w_out1.astype(bf16),
        g_ffn=g_ffn, g_final=g_final,
        ffn=_ffn_prepare(ffn_w_up, ffn_conv_w, ffn_conv_b, ffn_w_down))
    st_s = dict(gla_S=state_gla_S, rg_h=state_rglru_h, rg_conv=state_rglru_conv, ml_C=state_mlstm_C,
                ml_n=state_mlstm_n, ml_m=state_mlstm_m, ml_conv=state_mlstm_conv, s5_re=state_s5_re,
                s5_im=state_s5_im, ffn_conv=state_ffn_conv)
    y_p, np_ = _trunk(x_prompt, None, w, sample=False)
    y_s, ns_ = _trunk(x_sample, st_s, w, sample=True)
    names = ('gla_S', 'rg_h', 'rg_conv', 'ml_C', 'ml_n', 'ml_m', 'ml_conv', 's5_re', 's5_im', 'ffn_conv')
    outs = [y_p, y_s]
    for nme in names:
        outs += [np_[nme], ns_[nme]]
    return tuple(outs)
```

```python
import functools

import jax
import jax.numpy as jnp
from jax import lax
from jax.experimental import pallas as pl
from jax.experimental.pallas import tpu as pltpu

f32 = jnp.float32
bf16 = jnp.bfloat16

EPS = 1e-6
CHUNK = 64
GLA_TAU = 16.0
RG_C = 8.0
PAST_LEN = 16384
LANES = 128
SUBLANES = 8
HALO = 16
MIB = 1024 * 1024


def _params(sem, vmem_mib):
    return pltpu.CompilerParams(dimension_semantics=sem, vmem_limit_bytes=int(vmem_mib * MIB))


def _dot(a, b):
    return jnp.dot(a, b, preferred_element_type=f32)


def _dot_nt(a, b):
    return lax.dot_general(a, b, (((1,), (1,)), ((), ())), preferred_element_type=f32)


def _dot_tn(a, b):
    return lax.dot_general(a, b, (((0,), (0,)), ((), ())), preferred_element_type=f32)


def _rms(x, g):
    return x * lax.rsqrt(jnp.mean(x * x, axis=-1, keepdims=True) + EPS) * g


def _eye(n):
    return lax.broadcasted_iota(jnp.int32, (n, n), 0) == lax.broadcasted_iota(jnp.int32, (n, n), 1)


def _col_from_row(row):
    n = row.shape[1]
    return jnp.sum(jnp.where(_eye(n), jnp.broadcast_to(row, (n, n)), 0.0), axis=1, keepdims=True)


def _row_from_col(col):
    n = col.shape[0]
    return jnp.sum(jnp.where(_eye(n), jnp.broadcast_to(col, (n, n)), 0.0), axis=0, keepdims=True)


def _chunk_cumsum(x, chunk):
    pos = lax.broadcasted_iota(jnp.int32, (x.shape[0], 1), 0) % chunk
    step = 1
    while step < chunk:
        x = x + jnp.where(pos >= step, pltpu.roll(x, step, 0), 0.0)
        step *= 2
    return x


def _shifted(prev8, x, back):
    xx = jnp.concatenate([prev8, x], axis=0)
    n = x.shape[0]
    return xx[SUBLANES - back:SUBLANES - back + n]


def _norm_matmul(x, g, w, *, tm, tn):
    M, D = x.shape
    N = w.shape[1]
    rc = min(tm, 256)

    def body(x_ref, g_ref, w_ref, o_ref, xn_ref):
        @pl.when(pl.program_id(1) == 0)
        def _():
            def chunk(r, c):
                rows = pl.ds(pl.multiple_of(r * rc, rc), rc)
                xn_ref[rows, :] = _rms(x_ref[rows, :], g_ref[...]).astype(bf16)
                return c
            lax.fori_loop(0, tm // rc, chunk, 0)
        o_ref[...] = _dot(xn_ref[...], w_ref[...])

    return pl.pallas_call(
        body,
        grid=(M // tm, N // tn),
        in_specs=[pl.BlockSpec((tm, D), lambda i, j: (i, 0)),
                  pl.BlockSpec((1, D), lambda i, j: (0, 0)),
                  pl.BlockSpec((D, tn), lambda i, j: (0, j))],
        out_specs=pl.BlockSpec((tm, tn), lambda i, j: (i, j)),
        out_shape=jax.ShapeDtypeStruct((M, N), f32),
        scratch_shapes=[pltpu.VMEM((tm, D), bf16)],
        compiler_params=_params(("parallel", "arbitrary"), 48),
        name="norm_matmul",
    )(x, g.reshape(1, D), w)


def _out_proj(ya, yb, w, res, *, tm, tn):
    M, Ka = ya.shape
    Kb = yb.shape[1]
    N = w.shape[1]
    assert Ka == Kb and w.shape[0] == Ka + Kb

    def body(ya_ref, yb_ref, wa_ref, wb_ref, r_ref, o_ref):
        o_ref[...] = (r_ref[...] + _dot(ya_ref[...].astype(bf16), wa_ref[...])
                      + _dot(yb_ref[...].astype(bf16), wb_ref[...]))

    return pl.pallas_call(
        body,
        grid=(M // tm, N // tn),
        in_specs=[pl.BlockSpec((tm, Ka), lambda i, j: (i, 0)),
                  pl.BlockSpec((tm, Kb), lambda i, j: (i, 0)),
                  pl.BlockSpec((Ka, tn), lambda i, j: (0, j)),
                  pl.BlockSpec((Kb, tn), lambda i, j: (1, j)),
                  pl.BlockSpec((tm, tn), lambda i, j: (i, j))],
        out_specs=pl.BlockSpec((tm, tn), lambda i, j: (i, j)),
        out_shape=jax.ShapeDtypeStruct((M, N), f32),
        compiler_params=_params(("parallel", "arbitrary"), 48),
        name="out_proj",
    )(ya, yb, w, w, res)


FFN_STEP = 256


def _ffn_to_steps(a, F):
    nsteps = -(-F // FFN_STEP)
    lead = a.shape[:-1]
    nd = len(lead)
    gv = a.reshape(lead + (2, F))
    gv = jnp.pad(gv, [(0, 0)] * (nd + 1) + [(0, nsteps * FFN_STEP - F)])
    gv = gv.reshape(lead + (2, nsteps, FFN_STEP))
    gv = jnp.transpose(gv, (nd + 1,) + tuple(range(nd)) + (nd, nd + 2))
    return gv.reshape((nsteps,) + lead + (2 * FFN_STEP,))


def _ffn_prepare(w_up, conv_w, conv_b, w_down):
    NL, F, D = w_down.shape
    assert F % LANES == 0 and FFN_STEP == 2 * LANES
    nt = F // LANES
    nsteps = -(-F // FFN_STEP)

    def body(ga, gb, va, vb, da, db, wu_o, wd_o):
        keep_b = 2 * pl.program_id(1) + 1 < nt
        wu_o[:, 0 * LANES:1 * LANES] = ga[...].astype(bf16)
        wu_o[:, 1 * LANES:2 * LANES] = jnp.where(keep_b, gb[...], 0.0).astype(bf16)
        wu_o[:, 2 * LANES:3 * LANES] = va[...].astype(bf16)
        wu_o[:, 3 * LANES:4 * LANES] = jnp.where(keep_b, vb[...], 0.0).astype(bf16)
        wd_o[0:LANES, :] = da[...].astype(bf16)
        wd_o[LANES:2 * LANES, :] = jnp.where(keep_b, db[...], 0.0).astype(bf16)

    ta = lambda j: 2 * j
    tb = lambda j: jnp.minimum(2 * j + 1, nt - 1)
    wu, wd = pl.pallas_call(
        body,
        grid=(NL, nsteps),
        in_specs=[pl.BlockSpec((None, D, LANES), lambda l, j: (l, 0, ta(j))),
                  pl.BlockSpec((None, D, LANES), lambda l, j: (l, 0, tb(j))),
                  pl.BlockSpec((None, D, LANES), lambda l, j: (l, 0, nt + ta(j))),
                  pl.BlockSpec((None, D, LANES), lambda l, j: (l, 0, nt + tb(j))),
                  pl.BlockSpec((None, LANES, D), lambda l, j: (l, ta(j), 0)),
                  pl.BlockSpec((None, LANES, D), lambda l, j: (l, tb(j), 0))],
        out_specs=[pl.BlockSpec((None, None, D, 2 * FFN_STEP), lambda l, j: (l, j, 0, 0)),
                   pl.BlockSpec((None, None, FFN_STEP, D), lambda l, j: (l, j, 0, 0))],
        out_shape=[jax.ShapeDtypeStruct((NL, nsteps, D, 2 * FFN_STEP), bf16),
                   jax.ShapeDtypeStruct((NL, nsteps, FFN_STEP, D), bf16)],
        compiler_params=_params(("parallel", "parallel"), 32),
        name="ffn_weight_layout",
    )(w_up, w_up, w_up, w_up, w_down, w_down)
    return [dict(wu=wu, wd=wd, layer=l, F=F, cw=_ffn_to_steps(conv_w[l], F),
                 cb=_ffn_to_steps(conv_b[l], F).reshape(nsteps, 1, 2 * FFN_STEP)) for l in range(NL)]


def _conv_ffn(x, g, fw, *, seq_len, tm, prev=None, final_g=None):
    M, D = x.shape
    F = fw['F']
    layer = fw['layer']
    nsteps = fw['wu'].shape[1]
    sample = prev is not None
    rc = min(tm, 512)
    tiles_per_seq = max(seq_len // tm, 1)
    W2 = 2 * FFN_STEP

    def body(*refs):
        it = iter(refs)
        x_ref = next(it)
        xh_ref = None if sample else next(it)
        prev_refs = [next(it) for _ in range(4)] if sample else None
        g_ref = next(it)
        wu_ref, cw_ref, cb_ref, wd_ref = next(it), next(it), next(it), next(it)
        fg_ref = next(it) if final_g is not None else None
        o_ref, tail_g_ref, tail_v_ref = next(it), next(it), next(it)
        xn_ref, up_scr = next(it), next(it)
        i = pl.program_id(0)
        j = pl.program_id(1)

        @pl.when(j == 0)
        def _():
            if sample:
                xn_ref[0:HALO, :] = jnp.zeros((HALO, D), bf16)
            else:
                keep = (i % tiles_per_seq != 0).astype(f32)
                hist = _rms(xh_ref[...], g_ref[...]) * keep
                xn_ref[0:HALO, :] = jnp.concatenate([jnp.zeros_like(hist), hist], axis=0).astype(bf16)

            def chunk(r, c):
                rows = pl.ds(pl.multiple_of(r * rc, rc), rc)
                xr = x_ref[rows, :]
                o_ref[rows, :] = xr
                xn_ref[pl.ds(pl.multiple_of(HALO + r * rc, HALO), rc), :] = _rms(xr, g_ref[...]).astype(bf16)
                return c
            lax.fori_loop(0, tm // rc, chunk, 0)

        cw = cw_ref[0]
        cb = cb_ref[0]
        nchunk = tm // rc
        for r in range(nchunk):
            if sample:
                up_scr[r, HALO:, :] = _dot(xn_ref[pl.ds(HALO + r * rc, rc), :], wu_ref[...])
            else:
                up_scr[r] = _dot(xn_ref[pl.ds(r * rc, rc + HALO), :], wu_ref[...])
        for r in range(nchunk):
            rows = pl.ds(r * rc, rc)
            up = up_scr[r, HALO:, :]
            if sample:
                p0 = jnp.concatenate([p[0, rows, :] for p in prev_refs], axis=1)
                p1 = jnp.concatenate([p[1, rows, :] for p in prev_refs], axis=1)
                conv = cb + cw[0:1] * p0 + cw[1:2] * p1 + cw[2:3] * up
                tail_g_ref[rows, :] = up[:, :FFN_STEP]
                tail_v_ref[rows, :] = up[:, FFN_STEP:]
            else:
                conv = (cb + cw[0:1] * up_scr[r, HALO - 2:HALO - 2 + rc, :]
                        + cw[1:2] * up_scr[r, HALO - 1:HALO - 1 + rc, :] + cw[2:3] * up)
                if r == nchunk - 1:
                    tail_g_ref[...] = up[rc - SUBLANES:, :FFN_STEP]
                    tail_v_ref[...] = up[rc - SUBLANES:, FFN_STEP:]
            h = jax.nn.gelu(conv[:, :FFN_STEP]) * conv[:, FFN_STEP:]
            o_ref[rows, :] += _dot(h.astype(bf16), wd_ref[...])

        if final_g is not None:
            @pl.when(j == nsteps - 1)
            def _():
                def chunk2(r, c):
                    rows = pl.ds(pl.multiple_of(r * rc, rc), rc)
                    o_ref[rows, :] = _rms(o_ref[rows, :], fg_ref[...])
                    return c
                lax.fori_loop(0, tm // rc, chunk2, 0)

    in_specs = [pl.BlockSpec((tm, D), lambda i, j: (i, 0))]
    args = [x]
    if sample:
        nt = F // LANES
        last = 2 * nt - 1
        for off in (0, 1, nt, nt + 1):
            in_specs.append(pl.BlockSpec((2, tm, LANES), lambda i, j, off=off: (0, i, jnp.minimum(2 * j + off, last))))
            args.append(prev)
    else:
        in_specs.append(pl.BlockSpec((SUBLANES, D), lambda i, j: (jnp.maximum(i * (tm // SUBLANES) - 1, 0), 0)))
        args.append(x)
    in_specs += [pl.BlockSpec((1, D), lambda i, j: (0, 0)),
                 pl.BlockSpec((None, None, D, W2), lambda i, j: (layer, j, 0, 0)),
                 pl.BlockSpec((1, 3, W2), lambda i, j: (j, 0, 0)),
                 pl.BlockSpec((1, 1, W2), lambda i, j: (j, 0, 0)),
                 pl.BlockSpec((None, None, FFN_STEP, D), lambda i, j: (layer, j, 0, 0))]
    args += [g.reshape(1, D), fw['wu'], fw['cw'], fw['cb'], fw['wd']]
    if final_g is not None:
        in_specs.append(pl.BlockSpec((1, D), lambda i, j: (0, 0)))
        args.append(final_g.reshape(1, D))
    FP = nsteps * FFN_STEP
    if sample:
        tail_shape = (M, FP)
        tail_spec = pl.BlockSpec((tm, FFN_STEP), lambda i, j: (i, j))
    else:
        tail_shape = (M // tm, SUBLANES, FP)
        tail_spec = pl.BlockSpec((None, SUBLANES, FFN_STEP), lambda i, j: (i, 0, j))
    out, tail_g, tail_v = pl.pallas_call(
        body,
        grid=(M // tm, nsteps),
        in_specs=in_specs,
        out_specs=[pl.BlockSpec((tm, D), lambda i, j: (i, 0)), tail_spec, tail_spec],
        out_shape=[jax.ShapeDtypeStruct((M, D), f32), jax.ShapeDtypeStruct(tail_shape, f32),
                   jax.ShapeDtypeStruct(tail_shape, f32)],
        scratch_shapes=[pltpu.VMEM((tm + HALO, D), bf16), pltpu.VMEM((tm // rc, rc + HALO, W2), f32)],
        compiler_params=_params(("parallel", "arbitrary"), 56),
        name="conv_ffn_sample" if sample else "conv_ffn",
    )(*args)
    if sample:
        up_rows = jnp.concatenate([tail_g[:, :F], tail_v[:, :F]], axis=1)
        new_buf = jnp.stack([prev[1], up_rows], axis=1)
    else:
        nseq = M // seq_len
        pick = lambda t: t.reshape(nseq, tiles_per_seq, SUBLANES, FP)[:, -1, SUBLANES - 2:, :F]
        new_buf = jnp.concatenate([pick(tail_g), pick(tail_v)], axis=-1)
    return out, new_buf


def _gla_prompt(p0, alr, w2p, b_alpha, g_norm, *, B, L, H, DK, DV, tb):
    NT = L // tb
    NC = tb // CHUNK
    C = CHUNK
    scale = DK ** -0.5
    qk_blocks = H
    v_off = 2 * H * DK // DV

    def body(q_ref, k_ref, v_ref, r_ref, a_ref, w2_ref, ba_ref, gn_ref, y_ref, s_out_ref, s_scr):
        t = pl.program_id(2)

        @pl.when(t == 0)
        def _():
            s_scr[...] = jnp.zeros_like(s_scr)

        z = _dot(a_ref[...].astype(bf16), w2_ref[...].astype(bf16)) + ba_ref[...]
        gl = jax.nn.log_sigmoid(z) * (1.0 / GLA_TAU)
        bc3 = _chunk_cumsum(gl, C).reshape(NC, C, DK)
        bl3 = bc3[:, C - 1:C, :]
        q3 = (q_ref[...] * scale).reshape(NC, C, DK)
        k3 = k_ref[...].reshape(NC, C, DK)
        v3 = v_ref[...].astype(bf16).reshape(NC, C, DV)
        qd3 = (q3 * jnp.exp(bc3)).astype(bf16)
        kd3 = (k3 * jnp.exp(-bc3)).astype(bf16)
        kdec3 = (k3 * jnp.exp(bl3 - bc3)).astype(bf16)
        causal = (lax.broadcasted_iota(jnp.int32, (1, C, C), 1) >= lax.broadcasted_iota(jnp.int32, (1, C, C), 2))
        att = jnp.where(causal, jnp.einsum('cik,cjk->cij', qd3, kd3, preferred_element_type=f32), 0.0)
        intra = jnp.einsum('cij,cjv->civ', att.astype(bf16), v3, preferred_element_type=f32)
        ds = jnp.einsum('cjk,cjv->ckv', kdec3, v3, preferred_element_type=f32)
        s = s_scr[...]
        s_in = []
        for c in range(NC):
            s_in.append(s)
            s = _col_from_row(jnp.exp(bl3[c])) * s + ds[c]
        s_scr[...] = s
        s_all = jnp.stack(s_in).astype(bf16)
        o = (intra + jnp.einsum('cik,ckv->civ', qd3, s_all, preferred_element_type=f32)).reshape(tb, DV)
        rr = r_ref[...]
        y_ref[...] = (_rms(o, gn_ref[...]) * (rr * jax.nn.sigmoid(rr))).astype(bf16)

        @pl.when(t == NT - 1)
        def _():
            s_out_ref[0, 0] = s
    return pl.pallas_call(
        body,
        grid=(B, H, NT),
        in_specs=[pl.BlockSpec((tb, DK), lambda b, h, t: (b * NT + t, h)),
                  pl.BlockSpec((tb, DK), lambda b, h, t: (b * NT + t, qk_blocks + h)),
                  pl.BlockSpec((tb, DV), lambda b, h, t: (b * NT + t, v_off + h)),
                  pl.BlockSpec((tb, DV), lambda b, h, t: (b * NT + t, v_off + H + h)),
                  pl.BlockSpec((tb, LANES), lambda b, h, t: (b * NT + t, 0)),
                  pl.BlockSpec((LANES, DK), lambda b, h, t: (0, h)),
                  pl.BlockSpec((1, DK), lambda b, h, t: (0, h)),
                  pl.BlockSpec((1, DV), lambda b, h, t: (0, h))],
        out_specs=[pl.BlockSpec((tb, DV), lambda b, h, t: (b * NT + t, h)),
                   pl.BlockSpec((1, 1, DK, DV), lambda b, h, t: (b, h, 0, 0))],
        out_shape=[jax.ShapeDtypeStruct((B * L, H * DV), bf16),
                   jax.ShapeDtypeStruct((B, H, DK, DV), f32)],
        scratch_shapes=[pltpu.VMEM((DK, DV), f32)],
        compiler_params=_params(("parallel", "parallel", "arbitrary"), 32),
        name="gla_prompt",
    )(p0, p0, p0, p0, alr, w2p, b_alpha.reshape(1, -1), g_norm.reshape(1, -1))


def _gla_sample(p0, alr, w2p, b_alpha, g_norm, s0, *, H, DK, DV):
    Bs = p0.shape[0]
    scale = DK ** -0.5
    qkw = H * DK
    vw = H * DV
    assert vw % qkw == 0
    SB = SUBLANES

    def body(q_ref, k_ref, v_ref, r_ref, a_ref, w2_ref, ba_ref, gn_ref, s_ref, y_ref, so_ref, gl_scr):
        z = _dot(a_ref[...].astype(bf16), w2_ref[...].astype(bf16)) + ba_ref[...]
        gl_scr[...] = jax.nn.log_sigmoid(z) * (1.0 / GLA_TAU)

        for h in range(H):
            ks = slice(h * DK, (h + 1) * DK)
            vs = slice(h * DV, (h + 1) * DV)
            a_t = jnp.exp(gl_scr[:, ks]).T
            k_t = k_ref[:, ks].T
            q_t = (q_ref[:, ks] * scale).T
            v = v_ref[:, vs]
            outs = []
            for s in range(SB):
                sn = a_t[:, s:s + 1] * s_ref[s, h] + k_t[:, s:s + 1] * v[s:s + 1, :]
                so_ref[s, h] = sn
                outs.append(jnp.sum(q_t[:, s:s + 1] * sn, axis=0, keepdims=True))
            o = jnp.concatenate(outs, axis=0)
            rr = r_ref[:, vs]
            y_ref[:, vs] = _rms(o, gn_ref[:, vs]) * (rr * jax.nn.sigmoid(rr))

    v_blk = 2 * qkw // vw
    return pl.pallas_call(
        body,
        grid=(Bs // SB,),
        in_specs=[pl.BlockSpec((SB, qkw), lambda b: (b, 0)),
                  pl.BlockSpec((SB, qkw), lambda b: (b, 1)),
                  pl.BlockSpec((SB, vw), lambda b: (b, v_blk)),
                  pl.BlockSpec((SB, vw), lambda b: (b, v_blk + 1)),
                  pl.BlockSpec((SB, LANES), lambda b: (b, 0)),
                  pl.BlockSpec((LANES, qkw), lambda b: (0, 0)),
                  pl.BlockSpec((1, qkw), lambda b: (0, 0)),
                  pl.BlockSpec((1, vw), lambda b: (0, 0)),
                  pl.BlockSpec((SB, H, DK, DV), lambda b: (b, 0, 0, 0))],
        out_specs=[pl.BlockSpec((SB, vw), lambda b: (b, 0)),
                   pl.BlockSpec((SB, H, DK, DV), lambda b: (b, 0, 0, 0))],
        out_shape=[jax.ShapeDtypeStruct((Bs, vw), f32),
                   jax.ShapeDtypeStruct((Bs, H, DK, DV), f32)],
        scratch_shapes=[pltpu.VMEM((SB, qkw), f32)],
        compiler_params=_params(("parallel",), 32),
        name="gla_sample",
    )(p0, p0, p0, p0, alr, w2p, b_alpha.reshape(1, -1), g_norm.reshape(1, -1), s0)


def _rg_gates(xc, wr, br, wi, bi, sp):
    xb = xc.astype(bf16)
    r = jax.nn.sigmoid(_dot(xb, wr) + br)
    i = jax.nn.sigmoid(_dot(xb, wi) + bi)
    log_a = -RG_C * r * sp
    a = jnp.exp(log_a)
    mult = jnp.sqrt(1.0 - jnp.exp(2.0 * log_a))
    return a, mult, i


def _rglru_prompt(p0, conv_w, conv_b, w_r, b_r, w_i, b_i, lam, *, B, L, x_blk, g_blk):
    M = B * L
    NB, BS, _ = w_r.shape
    assert BS == LANES
    rc = min(256, L)

    def body(x_ref, gg_ref, cw_ref, cb_ref, wr_ref, br_ref, wi_ref, bi_ref, lam_ref, y_ref, hl_ref, a_scr, b_scr):
        wr = wr_ref[...].astype(bf16)
        wi = wi_ref[...].astype(bf16)
        sp = jax.nn.softplus(-lam_ref[...])
        cw = cw_ref[...]

        def chunk(c, carry):
            r0 = pl.multiple_of(c * rc, rc)
            rows = pl.ds(r0, rc)
            x = x_ref[rows, :]
            start = (r0 % L) == 0
            prev = x_ref[pl.ds(pl.multiple_of(jnp.maximum(r0 - SUBLANES, 0), SUBLANES), SUBLANES), :]
            prev = jnp.where(start, 0.0, prev)
            xc = (cb_ref[...] + cw[0:1] * _shifted(prev, x, 3) + cw[1:2] * _shifted(prev, x, 2)
                  + cw[2:3] * _shifted(prev, x, 1) + cw[3:4] * x)
            a, mult, ig = _rg_gates(xc, wr, br_ref[...], wi, bi_ref[...], sp)
            pos = (r0 + lax.broadcasted_iota(jnp.int32, (rc, 1), 0)) % L
            mult = jnp.where(pos == 0, 1.0, mult)
            a_scr[rows, :] = a
            b_scr[rows, :] = mult * (ig * xc)
            return carry
        lax.fori_loop(0, M // rc, chunk, 0)

        def step(t, hs):
            new = []
            for b in range(B):
                row = pl.ds(b * L + t, 1)
                h = a_scr[row, :] * hs[b] + b_scr[row, :]
                b_scr[row, :] = h
                new.append(h)
            return tuple(new)
        hs = lax.fori_loop(0, L, step, tuple(jnp.zeros((1, LANES), f32) for _ in range(B)), unroll=8)
        hl_ref[...] = jnp.concatenate(hs, axis=0)

        def outc(c, carry):
            rows = pl.ds(pl.multiple_of(c * rc, rc), rc)
            y_ref[rows, :] = (b_scr[rows, :] * jax.nn.gelu(gg_ref[rows, :])).astype(bf16)
            return carry
        lax.fori_loop(0, M // rc, outc, 0)

    W = NB * BS
    return pl.pallas_call(
        body,
        grid=(NB,),
        in_specs=[pl.BlockSpec((M, LANES), lambda n: (0, x_blk + n)),
                  pl.BlockSpec((M, LANES), lambda n: (0, g_blk + n)),
                  pl.BlockSpec((4, LANES), lambda n: (0, n)),
                  pl.BlockSpec((1, LANES), lambda n: (0, n)),
                  pl.BlockSpec((None, BS, BS), lambda n: (n, 0, 0)),
                  pl.BlockSpec((1, LANES), lambda n: (0, n)),
                  pl.BlockSpec((None, BS, BS), lambda n: (n, 0, 0)),
                  pl.BlockSpec((1, LANES), lambda n: (0, n)),
                  pl.BlockSpec((1, LANES), lambda n: (0, n))],
        out_specs=[pl.BlockSpec((M, LANES), lambda n: (0, n)),
                   pl.BlockSpec((B, LANES), lambda n: (0, n))],
        out_shape=[jax.ShapeDtypeStruct((M, W), bf16), jax.ShapeDtypeStruct((B, W), f32)],
        scratch_shapes=[pltpu.VMEM((M, LANES), f32), pltpu.VMEM((M, LANES), f32)],
        compiler_params=_params(("parallel",), 48),
        name="rglru_prompt",
    )(p0, p0, conv_w, conv_b.reshape(1, W), w_r, b_r.reshape(1, W), w_i, b_i.reshape(1, W), lam.reshape(1, W))


def _rglru_sample(p0, conv_state, h0, conv_w, conv_b, w_r, b_r, w_i, b_i, lam, *, x_blk, g_blk):
    Bs = p0.shape[0]
    NB, BS, _ = w_r.shape
    W = NB * BS
    s0, s1, s2 = conv_state[:, 0], conv_state[:, 1], conv_state[:, 2]

    def body(x_ref, gg_ref, s0_ref, s1_ref, s2_ref, h0_ref, cw_ref, cb_ref, wr_ref, br_ref, wi_ref, bi_ref, lam_ref,
             y_ref, h_ref):
        cw = cw_ref[...]
        x = x_ref[...]
        xc = cb_ref[...] + cw[0:1] * s0_ref[...] + cw[1:2] * s1_ref[...] + cw[2:3] * s2_ref[...] + cw[3:4] * x
        sp = jax.nn.softplus(-lam_ref[...])
        a, mult, ig = _rg_gates(xc, wr_ref[...].astype(bf16), br_ref[...], wi_ref[...].astype(bf16), bi_ref[...], sp)
        if PAST_LEN == 0:
            mult = jnp.ones_like(mult)
        h = a * h0_ref[...] + mult * (ig * xc)
        h_ref[...] = h
        y_ref[...] = (h * jax.nn.gelu(gg_ref[...])).astype(bf16)

    blk = lambda n: (0, n)
    vec = pl.BlockSpec((1, LANES), blk)
    mat = pl.BlockSpec((Bs, LANES), blk)
    y, h = pl.pallas_call(
        body,
        grid=(NB,),
        in_specs=[pl.BlockSpec((Bs, LANES), lambda n: (0, x_blk + n)),
                  pl.BlockSpec((Bs, LANES), lambda n: (0, g_blk + n)),
                  mat, mat, mat, mat,
                  pl.BlockSpec((4, LANES), blk), vec,
                  pl.BlockSpec((None, BS, BS), lambda n: (n, 0, 0)), vec,
                  pl.BlockSpec((None, BS, BS), lambda n: (n, 0, 0)), vec, vec],
        out_specs=[mat, mat],
        out_shape=[jax.ShapeDtypeStruct((Bs, W), bf16), jax.ShapeDtypeStruct((Bs, W), f32)],
        compiler_params=_params(("parallel",), 32),
        name="rglru_sample",
    )(p0, p0, s0, s1, s2, h0, conv_w, conv_b.reshape(1, W), w_r, b_r.reshape(1, W), w_i, b_i.reshape(1, W),
      lam.reshape(1, W))
    return y, h


def _blockdiag_tiles(w, tile):
    nblk, bs, _ = w.shape
    per = tile // bs
    nt = nblk // per
    rows = jnp.tile(w.reshape(nt, tile, bs), (1, 1, per))
    on_diag = (jnp.arange(tile)[:, None] // bs) == (jnp.arange(tile)[None, :] // bs)
    return jnp.where(on_diag[None], rows, 0.0)


def _mlstm_stage1(p1, conv_w, conv_b, wq_t, wk_t, wv_t, wg, bg, *, seq_len, tr, H, DH, conv_state=None):
    M = p1.shape[0]
    W = H * DH
    NTL, TL, _ = wq_t.shape
    sample = conv_state is not None
    tiles_per_seq = max(seq_len // tr, 1)
    kscale = DH ** -0.5

    def body(*refs):
        it = iter(refs)
        x_ref = next(it)
        if sample:
            s0_ref, s1_ref, s2_ref = next(it), next(it), next(it)
        else:
            xh_ref = next(it)
        cw_ref, cb_ref, wq_ref, wk_ref, wv_ref, wg_ref, bg_ref = (next(it) for _ in range(7))
        q_ref, k_ref, v_ref, g_ref, xc_ref = (next(it) for _ in range(5))
        cw = cw_ref[...]
        x = x_ref[...]
        if sample:
            conv = cb_ref[...] + cw[0:1] * s0_ref[...] + cw[1:2] * s1_ref[...] + cw[2:3] * s2_ref[...] + cw[3:4] * x
        else:
            keep = (pl.program_id(0) % tiles_per_seq != 0).astype(f32)
            prev = xh_ref[...] * keep
            conv = (cb_ref[...] + cw[0:1] * _shifted(prev, x, 3) + cw[1:2] * _shifted(prev, x, 2)
                    + cw[2:3] * _shifted(prev, x, 1) + cw[3:4] * x)
        xc = conv * jax.nn.sigmoid(conv)
        xc_ref[...] = xc
        xcb = xc.astype(bf16)
        xb = x.astype(bf16)
        qs, ks, vs = [], [], []
        for t in range(NTL):
            cs = slice(t * TL, (t + 1) * TL)
            qs.append(_dot(xcb[:, cs], wq_ref[t]))
            ks.append(_dot(xcb[:, cs], wk_ref[t]) * kscale)
            vs.append(_dot(xb[:, cs], wv_ref[t]))
        q = jnp.concatenate(qs, axis=1)
        k = jnp.concatenate(ks, axis=1)
        v = jnp.concatenate(vs, axis=1)
        q_ref[...] = q.astype(q_ref.dtype)
        k_ref[...] = k.astype(k_ref.dtype)
        v_ref[...] = v.astype(v_ref.dtype)
        gt = (_dot(q.astype(bf16), wg_ref[0:W, :]) + _dot(k.astype(bf16), wg_ref[W:2 * W, :])
              + _dot(v.astype(bf16), wg_ref[2 * W:3 * W, :]) + bg_ref[...])
        lane = lax.broadcasted_iota(jnp.int32, gt.shape, 1)
        g_ref[...] = jnp.where(jnp.logical_and(lane >= H, lane < 2 * H), jax.nn.log_sigmoid(gt), gt)

    row = lambda i: (i, 0)
    const2 = lambda i: (0, 0)
    const3 = lambda i: (0, 0, 0)
    in_specs = [pl.BlockSpec((tr, W), row)]
    args = [p1]
    if sample:
        in_specs += [pl.BlockSpec((tr, W), row)] * 3
        args += [conv_state[:, 0], conv_state[:, 1], conv_state[:, 2]]
    else:
        in_specs.append(pl.BlockSpec((SUBLANES, W), lambda i: (jnp.maximum(i * (tr // SUBLANES) - 1, 0), 0)))
        args.append(p1)
    in_specs += [pl.BlockSpec((4, W), const2), pl.BlockSpec((1, W), const2),
                 pl.BlockSpec((NTL, TL, TL), const3), pl.BlockSpec((NTL, TL, TL), const3),
                 pl.BlockSpec((NTL, TL, TL), const3),
                 pl.BlockSpec((3 * W, LANES), const2), pl.BlockSpec((1, LANES), const2)]
    args += [conv_w, conv_b.reshape(1, W), wq_t, wk_t, wv_t, wg, bg]
    return pl.pallas_call(
        body,
        grid=(M // tr,),
        in_specs=in_specs,
        out_specs=[pl.BlockSpec((tr, W), row)] * 3 + [pl.BlockSpec((tr, LANES), row), pl.BlockSpec((tr, W), row)],
        out_shape=[jax.ShapeDtypeStruct((M, W), f32 if sample else bf16)] * 3
        + [jax.ShapeDtypeStruct((M, LANES), f32), jax.ShapeDtypeStruct((M, W), f32)],
        compiler_params=_params(("parallel",), 48),
        name="mlstm_stage1_sample" if sample else "mlstm_stage1",
    )(*args)


def _mlstm_prompt(q, k, v, gates, xc, p1, g_norm, skip, *, B, L, H, DH, tb):
    NT = L // tb
    NC = tb // CHUNK
    C = CHUNK

    def body(q_ref, k_ref, v_ref, g_ref, xc_ref, om_ref, gn_ref, sk_ref,
             y_ref, c_out, n_out, m_out, c_scr, n_scr, m_scr):
        hh = pl.program_id(1)
        t = pl.program_id(2)

        @pl.when(t == 0)
        def _():
            c_scr[...] = jnp.zeros_like(c_scr)
            n_scr[...] = jnp.zeros_like(n_scr)
            m_scr[...] = jnp.zeros_like(m_scr)

        gts = g_ref[...]
        lane = lax.broadcasted_iota(jnp.int32, (tb, LANES), 1)
        i_col = jnp.sum(jnp.where(lane == hh, gts, 0.0), axis=1, keepdims=True)
        b_col = jnp.sum(jnp.where(lane == H + hh, _chunk_cumsum(gts, C), 0.0), axis=1, keepdims=True)
        b3 = b_col.reshape(NC, C, 1)
        i3 = i_col.reshape(NC, C, 1)
        ii = lax.broadcasted_iota(jnp.int32, (1, C, C), 1)
        jj = lax.broadcasted_iota(jnp.int32, (1, C, C), 2)
        eye = ii == jj
        causal = ii >= jj
        as_row = lambda col3: jnp.sum(jnp.where(eye, jnp.broadcast_to(col3, (NC, C, C)), 0.0), axis=1, keepdims=True)
        dmat = jnp.where(causal, b3 - as_row(b3) + as_row(i3), -jnp.inf)
        rmax = jnp.max(dmat, axis=2, keepdims=True)
        b_last = b3[:, C - 1:C, :]
        m_prev = m_scr[...]
        m_in = []
        for c in range(NC):
            m_in.append(m_prev)
            m_prev = jnp.maximum(b_last[c] + m_prev, rmax[c][C - 1:C, :])
        m_scr[...] = m_prev
        inter = b3 + jnp.stack(m_in)
        m_col = jnp.maximum(inter, rmax)
        g_col = jnp.exp(inter - m_col)
        q3 = q_ref[...].reshape(NC, C, DH)
        k3 = k_ref[...].reshape(NC, C, DH)
        v3 = v_ref[...].reshape(NC, C, DH)
        s = jnp.einsum('cid,cjd->cij', q3, k3, preferred_element_type=f32) * jnp.exp(dmat - m_col)
        num = jnp.einsum('cij,cje->cie', s.astype(bf16), v3, preferred_element_type=f32)
        den = jnp.sum(s, axis=2, keepdims=True)
        m_new = m_col[:, C - 1:C, :]
        wk = jnp.exp(b_last - b3 + i3 - m_new)
        gc = jnp.exp(inter[:, C - 1:C, :] - m_new)
        kw = k3.astype(f32) * wk
        dc = jnp.einsum('cse,csd->ced', v3, kw.astype(bf16), preferred_element_type=f32)
        dn = jnp.sum(kw, axis=1, keepdims=True)
        cs = c_scr[...]
        ns = n_scr[...]
        c_in, n_in = [], []
        for c in range(NC):
            c_in.append(cs)
            n_in.append(ns)
            cs = gc[c] * cs + dc[c]
            ns = gc[c] * ns + dn[c]
        c_scr[...] = cs
        n_scr[...] = ns
        c_all = jnp.stack(c_in).astype(bf16)
        n_all = jnp.stack(n_in)
        num = num + g_col * jnp.einsum('cid,ced->cie', q3, c_all, preferred_element_type=f32)
        den = den + g_col * jnp.sum(q3.astype(f32) * n_all, axis=2, keepdims=True)
        hm = (num / jnp.maximum(jnp.abs(den), jnp.exp(-m_col))).reshape(tb, DH)
        y = (_rms(hm, gn_ref[...]) + sk_ref[...] * xc_ref[...]) * jax.nn.sigmoid(om_ref[...])
        y_ref[...] = y.astype(bf16)

        @pl.when(t == NT - 1)
        def _():
            c_out[0, 0] = cs
            n_out[0, 0] = ns
            m_out[0, 0] = m_prev

    blk = lambda b, h, t: (b * NT + t, h)
    W = H * DH
    y, c_new, n_new, m_new = pl.pallas_call(
        body,
        grid=(B, H, NT),
        in_specs=[pl.BlockSpec((tb, DH), blk), pl.BlockSpec((tb, DH), blk), pl.BlockSpec((tb, DH), blk),
                  pl.BlockSpec((tb, LANES), lambda b, h, t: (b * NT + t, 0)),
                  pl.BlockSpec((tb, DH), blk),
                  pl.BlockSpec((tb, DH), lambda b, h, t: (b * NT + t, H + h)),
                  pl.BlockSpec((1, DH), lambda b, h, t: (0, h)),
                  pl.BlockSpec((1, DH), lambda b, h, t: (0, h))],
        out_specs=[pl.BlockSpec((tb, DH), blk),
                   pl.BlockSpec((1, 1, DH, DH), lambda b, h, t: (b, h, 0, 0)),
                   pl.BlockSpec((1, 1, 1, DH), lambda b, h, t: (b, h, 0, 0)),
                   pl.BlockSpec((1, 1, 1, 1), lambda b, h, t: (b, h, 0, 0))],
        out_shape=[jax.ShapeDtypeStruct((B * L, W), bf16),
                   jax.ShapeDtypeStruct((B, H, DH, DH), f32),
                   jax.ShapeDtypeStruct((B, H, 1, DH), f32),
                   jax.ShapeDtypeStruct((B, H, 1, 1), f32)],
        scratch_shapes=[pltpu.VMEM((DH, DH), f32), pltpu.VMEM((1, DH), f32), pltpu.VMEM((1, 1), f32)],
        compiler_params=_params(("parallel", "parallel", "arbitrary"), 32),
        name="mlstm_prompt",
    )(q, k, v, gates, xc, p1, g_norm.reshape(1, W), skip.reshape(1, W))
    return y, c_new, n_new.reshape(B, H, DH), m_new.reshape(B, H)


def _mlstm_sample(q, k, v, gates, xc, p1, g_norm, skip, c0, n0, m0, *, H, DH):
    Bs = q.shape[0]
    W = H * DH
    SB = SUBLANES

    def body(q_ref, k_ref, v_ref, g_ref, xc_ref, om_ref, gn_ref, sk_ref, c_ref, n_ref, m_ref,
             y_ref, c_out, n_out, m_out):
        gts = g_ref[...]
        for h in range(H):
            cs = slice(h * DH, (h + 1) * DH)
            ig = gts[:, h:h + 1]
            fg = gts[:, H + h:H + h + 1]
            inter = fg + m_ref[:, h:h + 1]
            m = jnp.maximum(inter, ig)
            g = jnp.exp(inter - m)
            m_out[:, h:h + 1] = m
            q = q_ref[:, cs]
            kw = k_ref[:, cs] * jnp.exp(ig - m)
            nn = g * n_ref[:, cs] + kw
            n_out[:, cs] = nn
            den = jnp.sum(nn * q, axis=1, keepdims=True)
            v_t = v_ref[:, cs].T
            qb = q.astype(bf16)
            nums = []
            for s in range(SB):
                cn = g[s:s + 1, :] * c_ref[s, h] + v_t[:, s:s + 1] * kw[s:s + 1, :]
                c_out[s, h] = cn
                nums.append(_dot_nt(qb, cn.astype(bf16))[s:s + 1, :])
            num = jnp.concatenate(nums, axis=0)
            hm = num / jnp.maximum(jnp.abs(den), jnp.exp(-m))
            y_ref[:, cs] = ((_rms(hm, gn_ref[:, cs]) + sk_ref[:, cs] * xc_ref[:, cs])
                            * jax.nn.sigmoid(om_ref[:, cs]))

    per = lambda b: (b, 0)
    const2 = lambda b: (0, 0)
    y, c_new, n_new, m_new = pl.pallas_call(
        body,
        grid=(Bs // SB,),
        in_specs=[pl.BlockSpec((SB, W), per), pl.BlockSpec((SB, W), per), pl.BlockSpec((SB, W), per),
                  pl.BlockSpec((SB, LANES), per), pl.BlockSpec((SB, W), per),
                  pl.BlockSpec((SB, W), lambda b: (b, 1)),
                  pl.BlockSpec((1, W), const2), pl.BlockSpec((1, W), const2),
                  pl.BlockSpec((SB, H, DH, DH), lambda b: (b, 0, 0, 0)),
                  pl.BlockSpec((SB, W), per), pl.BlockSpec((SB, H), per)],
        out_specs=[pl.BlockSpec((SB, W), per),
                   pl.BlockSpec((SB, H, DH, DH), lambda b: (b, 0, 0, 0)),
                   pl.BlockSpec((SB, W), per), pl.BlockSpec((SB, H), per)],
        out_shape=[jax.ShapeDtypeStruct((Bs, W), f32), jax.ShapeDtypeStruct((Bs, H, DH, DH), f32),
                   jax.ShapeDtypeStruct((Bs, W), f32), jax.ShapeDtypeStruct((Bs, H), f32)],
        compiler_params=_params(("parallel",), 48),
        name="mlstm_sample",
    )(q, k, v, gates, xc, p1, g_norm.reshape(1, W), skip.reshape(1, W), c0, n0.reshape(Bs, W), m0)
    return y, c_new, n_new.reshape(Bs, H, DH), m_new


S5_TILES = 8


def _s5_layouts(lam_re, lam_im, log_dt, b_re, b_im, c_re, c_im):
    G, P = lam_re.shape
    GC = b_re.shape[2]
    T = S5_TILES
    gpt = G // T
    ns = G * P
    flat = lambda a: a.reshape(ns)
    ldt = jnp.broadcast_to(log_dt[:, None], (G, P))
    rows = [flat(a).reshape(T, 1, ns // T) for a in (lam_re, lam_im, ldt)]
    eye = jnp.eye(gpt, dtype=f32)
    bbd = [jnp.einsum('jgpc,gh->jgchp', a.reshape(T, gpt, P, GC), eye).reshape(T, gpt * GC, gpt * P) for a in (b_re, b_im)]
    cbd = [jnp.einsum('jgcp,gh->jgphc', a.reshape(T, gpt, GC, P), eye).reshape(T, gpt * P, gpt * GC) for a in (c_re, c_im)]
    return rows, bbd, cbd


def _s5_discretise(lre, lim, ldt):
    dt = jnp.exp(ldt)
    mag = jnp.exp(dt * lre)
    ar = mag * jnp.cos(dt * lim)
    ai = mag * jnp.sin(dt * lim)
    den = lre * lre + lim * lim
    cr = ((ar - 1.0) * lre + ai * lim) / den
    ci = (ai * lre - (ar - 1.0) * lim) / den
    return ar, ai, cr, ci


def _s5_mixer(p1, u_blk, rows, bbd, cbd, d_skip, w_glu, b_glu, *, B, L, tb, state=None):
    T = S5_TILES
    lre_r, lim_r, ldt_r = rows
    SW = lre_r.shape[2]
    CW = bbd[0].shape[1]
    W = T * CW
    NS = T * SW
    KT = SW // LANES
    sample = state is not None
    NT = 1 if sample else L // tb
    M = B * L

    def body(*refs):
        it = iter(refs)
        u_ref = next(it)
        if sample:
            x0r_ref, x0i_ref = next(it), next(it)
        lre_ref, lim_ref, ldt_ref = next(it), next(it), next(it)
        bre_ref, bim_ref, cre_ref, cim_ref = next(it), next(it), next(it), next(it)
        d_ref, wg_ref, bgl_ref = next(it), next(it), next(it)
        y_ref, xr_out, xi_out = next(it), next(it), next(it)
        bbr, bbi, cbr, cbi, ar_scr, ai_scr = (next(it) for _ in range(6))
        if not sample:
            sre, sim, xr_c, xi_c, yacc = (next(it) for _ in range(5))
        first = jnp.logical_and(pl.program_id(0) == 0, pl.program_id(1) == 0)

        @pl.when(first)
        def _():
            for j in range(T):
                ar, ai, cr, ci = _s5_discretise(lre_ref[j], lim_ref[j], ldt_ref[j])
                ar_scr[j] = ar
                ai_scr[j] = ai
                br = bre_ref[j]
                bi = bim_ref[j]
                bbr[j] = (cr * br - ci * bi).astype(bf16)
                bbi[j] = (cr * bi + ci * br).astype(bf16)
                cbr[j] = cre_ref[j].astype(bf16)
                cbi[j] = cim_ref[j].astype(bf16)

        u = u_ref[...]
        ub = u.astype(bf16)
        ys = []
        if sample:
            for j in range(T):
                cs = slice(j * SW, (j + 1) * SW)
                uj = ub[:, j * CW:(j + 1) * CW]
                ar = ar_scr[j]
                ai = ai_scr[j]
                x0r = x0r_ref[:, cs]
                x0i = x0i_ref[:, cs]
                xr = ar * x0r - ai * x0i + _dot(uj, bbr[j])
                xi = ar * x0i + ai * x0r + _dot(uj, bbi[j])
                xr_out[:, cs] = xr
                xi_out[:, cs] = xi
                ys.append(_dot(xr.astype(bf16), cbr[j]) - _dot(xi.astype(bf16), cbi[j]))
            y = jnp.concatenate(ys, axis=1)
        else:
            t = pl.program_id(1)

            @pl.when(t == 0)
            def _():
                xr_c[...] = jnp.zeros_like(xr_c)
                xi_c[...] = jnp.zeros_like(xi_c)

            for j in range(T):
                uj = ub[:, j * CW:(j + 1) * CW]
                r = _dot(uj, bbr[j])
                im = _dot(uj, bbi[j])
                for kk in range(KT):
                    sre[kk, pl.ds(j, tb, stride=T), :] = r[:, kk * LANES:(kk + 1) * LANES]
                    sim[kk, pl.ds(j, tb, stride=T), :] = im[:, kk * LANES:(kk + 1) * LANES]
            a_r = [jnp.concatenate([ar_scr[j][:, kk * LANES:(kk + 1) * LANES] for j in range(T)], axis=0) for kk in range(KT)]
            a_i = [jnp.concatenate([ai_scr[j][:, kk * LANES:(kk + 1) * LANES] for j in range(T)], axis=0) for kk in range(KT)]

            def step(s, carry):
                xr, xi = carry
                row = pl.ds(pl.multiple_of(s * T, T), T)
                nr, ni = [], []
                for kk in range(KT):
                    r_ = a_r[kk] * xr[kk] - a_i[kk] * xi[kk] + sre[kk, row, :]
                    i_ = a_r[kk] * xi[kk] + a_i[kk] * xr[kk] + sim[kk, row, :]
                    sre[kk, row, :] = r_
                    sim[kk, row, :] = i_
                    nr.append(r_)
                    ni.append(i_)
                return tuple(nr), tuple(ni)
            xr0 = tuple(xr_c[kk] for kk in range(KT))
            xi0 = tuple(xi_c[kk] for kk in range(KT))
            xr, xi = lax.fori_loop(0, tb, step, (xr0, xi0), unroll=4)
            for kk in range(KT):
                xr_c[kk] = xr[kk]
                xi_c[kk] = xi[kk]

            @pl.when(t == NT - 1)
            def _():
                for kk in range(KT):
                    xr_out[kk] = xr[kk]
                    xi_out[kk] = xi[kk]

            for j in range(T):
                xrj = jnp.concatenate([sre[kk, pl.ds(j, tb, stride=T), :] for kk in range(KT)], axis=1).astype(bf16)
                xij = jnp.concatenate([sim[kk, pl.ds(j, tb, stride=T), :] for kk in range(KT)], axis=1).astype(bf16)
                yacc[:, j * CW:(j + 1) * CW] = _dot(xrj, cbr[j]) - _dot(xij, cbi[j])
            y = yacc[...]
        ysk = jax.nn.gelu(y + d_ref[...] * u)
        z = _dot(ysk.astype(bf16), wg_ref[...]) + bgl_ref[...]
        y_ref[...] = (ysk * jax.nn.sigmoid(z)).astype(bf16)

    c3 = lambda b, t: (0, 0, 0)
    c2 = lambda b, t: (0, 0)
    in_specs = [pl.BlockSpec((tb, W), lambda b, t: (b * NT + t, u_blk))]
    args = [p1]
    if sample:
        in_specs += [pl.BlockSpec((tb, NS), lambda b, t: (b, 0))] * 2
        args += [state[0], state[1]]
    in_specs += [pl.BlockSpec((T, 1, SW), c3)] * 3
    in_specs += [pl.BlockSpec((T, CW, SW), c3)] * 2 + [pl.BlockSpec((T, SW, CW), c3)] * 2
    in_specs += [pl.BlockSpec((1, W), c2), pl.BlockSpec((W, W), c2), pl.BlockSpec((1, W), c2)]
    args += [lre_r, lim_r, ldt_r, bbd[0], bbd[1], cbd[0], cbd[1], d_skip.reshape(1, W), w_glu, b_glu.reshape(1, W)]
    scratch = [pltpu.VMEM((T, CW, SW), bf16), pltpu.VMEM((T, CW, SW), bf16),
               pltpu.VMEM((T, SW, CW), bf16), pltpu.VMEM((T, SW, CW), bf16),
               pltpu.VMEM((T, 1, SW), f32), pltpu.VMEM((T, 1, SW), f32)]
    if sample:
        grid = (M // tb, 1)
        st_spec = pl.BlockSpec((tb, NS), lambda b, t: (b, 0))
        st_shape = jax.ShapeDtypeStruct((M, NS), f32)
    else:
        grid = (B, NT)
        st_spec = pl.BlockSpec((None, KT, T, LANES), lambda b, t: (b, 0, 0, 0))
        st_shape = jax.ShapeDtypeStruct((B, KT, T, LANES), f32)
        scratch += [pltpu.VMEM((KT, tb * T, LANES), f32), pltpu.VMEM((KT, tb * T, LANES), f32),
                    pltpu.VMEM((KT, T, LANES), f32), pltpu.VMEM((KT, T, LANES), f32), pltpu.VMEM((tb, W), f32)]
    y, xr, xi = pl.pallas_call(
        body,
        grid=grid,
        in_specs=in_specs,
        out_specs=[pl.BlockSpec((tb, W), lambda b, t: (b * NT + t, 0)), st_spec, st_spec],
        out_shape=[jax.ShapeDtypeStruct((M, W), bf16), st_shape, st_shape],
        scratch_shapes=scratch,
        compiler_params=_params(("arbitrary", "arbitrary"), 56),
        name="s5_sample" if sample else "s5_prompt",
    )(*args)
    if not sample:
        xr = jnp.transpose(xr, (0, 2, 1, 3)).reshape(B, NS)
        xi = jnp.transpose(xi, (0, 2, 1, 3)).reshape(B, NS)
    return y, xr, xi


def _trunk(x3, st, w, *, sample):
    B, L, D = x3.shape
    M = B * L
    x = x3.reshape(M, D)
    H_g, DK, DV = st['gla_S'].shape[1:] if sample else w['gla_dims']
    H_m, DH = w['ml_dims']
    G, P = w['s5_dims']
    tm = M if sample else min(1024, L)
    tf = M if sample else min(1024, L)
    tb = min(512, L)
    out = {}

    p0 = _norm_matmul(x, w['g_mix0'], w['w_in0_main'], tm=tm, tn=512)
    alr = _norm_matmul(x, w['g_mix0'], w['w_in0_alr'], tm=tm, tn=LANES)
    x_blk = (2 * H_g * DK + 2 * H_g * DV) // LANES
    W_rg = w['rg_lambda'].shape[0]
    g_blk = x_blk + W_rg // LANES
    rg_tail = p0.reshape(B, L, p0.shape[1])[:, max(L - 3, 0):, x_blk * LANES:x_blk * LANES + W_rg]
    if sample:
        ya, out['gla_S'] = _gla_sample(p0, alr, w['gla_w2p'], w['gla_b_alpha'], w['gla_g_norm'], st['gla_S'],
                                       H=H_g, DK=DK, DV=DV)
        yb, out['rg_h'] = _rglru_sample(p0, st['rg_conv'], st['rg_h'], w['rg_conv_w'], w['rg_conv_b'], w['rg_w_r'],
                                        w['rg_b_r'], w['rg_w_i'], w['rg_b_i'], w['rg_lambda'], x_blk=x_blk, g_blk=g_blk)
        out['rg_conv'] = jnp.concatenate([st['rg_conv'][:, 1:], rg_tail], axis=1)
    else:
        ya, out['gla_S'] = _gla_prompt(p0, alr, w['gla_w2p'], w['gla_b_alpha'], w['gla_g_norm'],
                                       B=B, L=L, H=H_g, DK=DK, DV=DV, tb=tb)
        yb, out['rg_h'] = _rglru_prompt(p0, w['rg_conv_w'], w['rg_conv_b'], w['rg_w_r'], w['rg_b_r'], w['rg_w_i'],
                                        w['rg_b_i'], w['rg_lambda'], B=B, L=L, x_blk=x_blk, g_blk=g_blk)
        out['rg_conv'] = rg_tail
    x = _out_proj(ya, yb, w['w_out0'], x, tm=tm, tn=512)
    x, ffn0 = _conv_ffn(x, w['g_ffn'][0], w['ffn'][0], seq_len=L, tm=tf, prev=jnp.transpose(st['ffn_conv'][0], (1, 0, 2)) if sample else None)

    W_ml = H_m * DH
    p1 = _norm_matmul(x, w['g_mix1'], w['w_in1'], tm=tm, tn=512)
    xm_tail = p1.reshape(B, L, p1.shape[1])[:, max(L - 3, 0):, :W_ml]
    q, k, v, gates, xc = _mlstm_stage1(p1, w['ml_conv_w'], w['ml_conv_b'], w['ml_wq_t'], w['ml_wk_t'], w['ml_wv_t'],
                                       w['ml_wg'], w['ml_bg'], seq_len=L, tr=M if sample else min(256, L), H=H_m, DH=DH,
                                       conv_state=st['ml_conv'] if sample else None)
    u_blk = 2 * W_ml // (G * w['s5_gc'])
    if sample:
        yc, out['ml_C'], out['ml_n'], out['ml_m'] = _mlstm_sample(
            q, k, v, gates, xc, p1, w['ml_g_norm'], w['ml_skip'], st['ml_C'], st['ml_n'], st['ml_m'], H=H_m, DH=DH)
        out['ml_conv'] = jnp.concatenate([st['ml_conv'][:, 1:], xm_tail], axis=1)
        yd, s5r, s5i = _s5_mixer(p1, u_blk, w['s5_rows'], w['s5_bbd'], w['s5_cbd'], w['s5_D'], w['s5_w_glu'],
                                 w['s5_b_glu'], B=B, L=1, tb=B,
                                 state=(st['s5_re'].reshape(B, G * P), st['s5_im'].reshape(B, G * P)))
    else:
        yc, out['ml_C'], out['ml_n'], out['ml_m'] = _mlstm_prompt(
            q, k, v, gates, xc, p1, w['ml_g_norm'], w['ml_skip'], B=B, L=L, H=H_m, DH=DH, tb=tb)
        out['ml_conv'] = xm_tail
        yd, s5r, s5i = _s5_mixer(p1, u_blk, w['s5_rows'], w['s5_bbd'], w['s5_cbd'], w['s5_D'], w['s5_w_glu'],
                                 w['s5_b_glu'], B=B, L=L, tb=min(256, L))
    out['s5_re'] = s5r.reshape(B, G, P)
    out['s5_im'] = s5i.reshape(B, G, P)
    x = _out_proj(yc, yd, w['w_out1'], x, tm=tm, tn=512)
    x, ffn1 = _conv_ffn(x, w['g_ffn'][1], w['ffn'][1], seq_len=L, tm=tf, prev=jnp.transpose(st['ffn_conv'][1], (1, 0, 2)) if sample else None,
                        final_g=w['g_final'])
    out['ffn_conv'] = jnp.stack([ffn0, ffn1], axis=0)
    return x.reshape(B, L, D), out


def kernel(x_prompt, x_sample, state_gla_S, state_rglru_h, state_rglru_conv, state_mlstm_C, state_mlstm_n, state_mlstm_m, state_mlstm_conv, state_s5_re, state_s5_im, state_ffn_conv, g_mix0, w_in0, gla_w_alpha2, gla_b_alpha, gla_g_norm, rg_conv_w, rg_conv_b, rg_w_r, rg_b_r, rg_w_i, rg_b_i, rg_lambda, w_out0, g_mix1, w_in1, ml_conv_w, ml_conv_b, ml_wq, ml_wk, ml_wv, ml_w_igate, ml_b_igate, ml_w_fgate, ml_b_fgate, ml_g_norm, ml_skip, s5_lam_re, s5_lam_im, s5_log_dt, s5_B_re, s5_B_im, s5_C_re, s5_C_im, s5_D, s5_w_glu, s5_b_glu, w_out1, g_ffn, ffn_w_up, ffn_conv_w, ffn_conv_b, ffn_w_down, g_final):
    _, H_g, DK, DV = state_gla_S.shape
    _, H_m, DH, _ = state_mlstm_C.shape
    G, P = s5_lam_re.shape
    rank = gla_w_alpha2.shape[0]
    n_main = 2 * H_g * DK + 2 * H_g * DV
    w_in0_main = jnp.concatenate([w_in0[:, :n_main], w_in0[:, n_main + rank:]], axis=1).astype(bf16)
    w_in0_alr = jnp.pad(w_in0[:, n_main:n_main + rank], ((0, 0), (0, LANES - rank))).astype(bf16)
    gla_w2p = jnp.pad(gla_w_alpha2, ((0, LANES - rank), (0, 0)))
    ml_tile = 256
    ml_wg = jnp.pad(jnp.concatenate([ml_w_igate, ml_w_fgate], axis=1), ((0, 0), (0, LANES - 2 * H_m))).astype(bf16)
    ml_bg = jnp.pad(jnp.concatenate([ml_b_igate, ml_b_fgate]), (0, LANES - 2 * H_m)).reshape(1, LANES)
    rows, bbd, cbd = _s5_layouts(s5_lam_re, s5_lam_im, s5_log_dt, s5_B_re, s5_B_im, s5_C_re, s5_C_im)
    w = dict(
        g_mix0=g_mix0, w_in0_main=w_in0_main, w_in0_alr=w_in0_alr, gla_w2p=gla_w2p, gla_b_alpha=gla_b_alpha,
        gla_g_norm=gla_g_norm, gla_dims=(H_g, DK, DV), rg_conv_w=rg_conv_w, rg_conv_b=rg_conv_b, rg_w_r=rg_w_r,
        rg_b_r=rg_b_r, rg_w_i=rg_w_i, rg_b_i=rg_b_i, rg_lambda=rg_lambda, w_out0=w_out0.astype(bf16),
        g_mix1=g_mix1, w_in1=w_in1.astype(bf16), ml_conv_w=ml_conv_w, ml_conv_b=ml_conv_b,
        ml_wq_t=_blockdiag_tiles(ml_wq, ml_tile).astype(bf16), ml_wk_t=_blockdiag_tiles(ml_wk, ml_tile).astype(bf16),
        ml_wv_t=_blockdiag_tiles(ml_wv, ml_tile).astype(bf16), ml_wg=ml_wg, ml_bg=ml_bg, ml_g_norm=ml_g_norm,
        ml_skip=ml_skip, ml_dims=(H_m, DH), s5_dims=(G, P), s5_gc=s5_B_re.shape[2], s5_rows=rows, s5_bbd=bbd,
        s5_cbd=cbd, s5_D=s5_D, s5_w_glu=s5_w_glu.astype(bf16), s5_b_glu=s5_b_glu, w_out1=w_out1.astype(bf16),
        g_ffn=g_ffn, g_final=g_final,
        ffn=_ffn_prepare(ffn_w_up, ffn_conv_w, ffn_conv_b, ffn_w_down))
    st_s = dict(gla_S=state_gla_S, rg_h=state_rglru_h, rg_conv=state_rglru_conv, ml_C=state_mlstm_C,
                ml_n=state_mlstm_n, ml_m=state_mlstm_m, ml_conv=state_mlstm_conv, s5_re=state_s5_re,
                s5_im=state_s5_im, ffn_conv=state_ffn_conv)
    y_p, np_ = _trunk(x_prompt, None, w, sample=False)
    y_s, ns_ = _trunk(x_sample, st_s, w, sample=True)
    names = ('gla_S', 'rg_h', 'rg_conv', 'ml_C', 'ml_n', 'ml_m', 'ml_conv', 's5_re', 's5_im', 'ffn_conv')
    outs = [y_p, y_s]
    for nme in names:
        outs += [np_[nme], ns_[nme]]
    return tuple(outs)
```

```python
import functools

import jax
import jax.numpy as jnp
from jax import lax
from jax.experimental import pallas as pl
from jax.experimental.pallas import tpu as pltpu

f32 = jnp.float32
bf16 = jnp.bfloat16

EPS = 1e-6
CHUNK = 64
GLA_TAU = 16.0
RG_C = 8.0
PAST_LEN = 16384
LANES = 128
SUBLANES = 8
HALO = 16
MIB = 1024 * 1024


def _params(sem, vmem_mib):
    return pltpu.CompilerParams(dimension_semantics=sem, vmem_limit_bytes=int(vmem_mib * MIB))


def _dot(a, b):
    return jnp.dot(a, b, preferred_element_type=f32)


def _dot_nt(a, b):
    return lax.dot_general(a, b, (((1,), (1,)), ((), ())), preferred_element_type=f32)


def _dot_tn(a, b):
    return lax.dot_general(a, b, (((0,), (0,)), ((), ())), preferred_element_type=f32)


def _rms(x, g):
    return x * lax.rsqrt(jnp.mean(x * x, axis=-1, keepdims=True) + EPS) * g


def _eye(n):
    return lax.broadcasted_iota(jnp.int32, (n, n), 0) == lax.broadcasted_iota(jnp.int32, (n, n), 1)


def _col_from_row(row):
    n = row.shape[1]
    return jnp.sum(jnp.where(_eye(n), jnp.broadcast_to(row, (n, n)), 0.0), axis=1, keepdims=True)


def _row_from_col(col):
    n = col.shape[0]
    return jnp.sum(jnp.where(_eye(n), jnp.broadcast_to(col, (n, n)), 0.0), axis=0, keepdims=True)


def _chunk_cumsum(x, chunk):
    pos = lax.broadcasted_iota(jnp.int32, (x.shape[0], 1), 0) % chunk
    step = 1
    while step < chunk:
        x = x + jnp.where(pos >= step, pltpu.roll(x, step, 0), 0.0)
        step *= 2
    return x


def _shifted(prev8, x, back):
    xx = jnp.concatenate([prev8, x], axis=0)
    n = x.shape[0]
    return xx[SUBLANES - back:SUBLANES - back + n]


def _norm_matmul(x, g, w, *, tm, tn, w_side=None):
    M, D = x.shape
    N = w.shape[1]
    rc = min(tm, 256)
    side = w_side is not None

    def body(*refs):
        if side:
            x_ref, g_ref, w_ref, ws_ref, o_ref, os_ref, xn_ref = refs
        else:
            x_ref, g_ref, w_ref, o_ref, xn_ref = refs

        @pl.when(pl.program_id(1) == 0)
        def _():
            def chunk(r, c):
                rows = pl.ds(pl.multiple_of(r * rc, rc), rc)
                xn_ref[rows, :] = _rms(x_ref[rows, :], g_ref[...]).astype(bf16)
                return c
            lax.fori_loop(0, tm // rc, chunk, 0)
            if side:
                os_ref[...] = _dot(xn_ref[...], ws_ref[...])
        o_ref[...] = _dot(xn_ref[...], w_ref[...])

    in_specs = [pl.BlockSpec((tm, D), lambda i, j: (i, 0)),
                pl.BlockSpec((1, D), lambda i, j: (0, 0)),
                pl.BlockSpec((D, tn), lambda i, j: (0, j))]
    out_specs = [pl.BlockSpec((tm, tn), lambda i, j: (i, j))]
    out_shape = [jax.ShapeDtypeStruct((M, N), f32)]
    args = [x, g.reshape(1, D), w]
    if side:
        ns = w_side.shape[1]
        in_specs.append(pl.BlockSpec((D, ns), lambda i, j: (0, 0)))
        out_specs.append(pl.BlockSpec((tm, ns), lambda i, j: (i, 0)))
        out_shape.append(jax.ShapeDtypeStruct((M, ns), f32))
        args.append(w_side)
    outs = pl.pallas_call(
        body,
        grid=(M // tm, N // tn),
        in_specs=in_specs,
        out_specs=out_specs,
        out_shape=out_shape,
        scratch_shapes=[pltpu.VMEM((tm, D), bf16)],
        compiler_params=_params(("parallel", "arbitrary"), 48),
        name="norm_matmul",
    )(*args)
    return outs if side else outs[0]


def _out_proj(ya, yb, w, res, *, tm, tn):
    M, Ka = ya.shape
    Kb = yb.shape[1]
    N = w.shape[1]
    assert Ka == Kb and w.shape[0] == Ka + Kb

    def body(ya_ref, yb_ref, wa_ref, wb_ref, r_ref, o_ref):
        o_ref[...] = (r_ref[...] + _dot(ya_ref[...].astype(bf16), wa_ref[...])
                      + _dot(yb_ref[...].astype(bf16), wb_ref[...]))

    return pl.pallas_call(
        body,
        grid=(M // tm, N // tn),
        in_specs=[pl.BlockSpec((tm, Ka), lambda i, j: (i, 0)),
                  pl.BlockSpec((tm, Kb), lambda i, j: (i, 0)),
                  pl.BlockSpec((Ka, tn), lambda i, j: (0, j)),
                  pl.BlockSpec((Kb, tn), lambda i, j: (1, j)),
                  pl.BlockSpec((tm, tn), lambda i, j: (i, j))],
        out_specs=pl.BlockSpec((tm, tn), lambda i, j: (i, j)),
        out_shape=jax.ShapeDtypeStruct((M, N), f32),
        compiler_params=_params(("parallel", "arbitrary"), 48),
        name="out_proj",
    )(ya, yb, w, w, res)


FFN_STEP = 512


def _ffn_to_steps(a, F):
    nsteps = -(-F // FFN_STEP)
    lead = a.shape[:-1]
    nd = len(lead)
    gv = a.reshape(lead + (2, F))
    gv = jnp.pad(gv, [(0, 0)] * (nd + 1) + [(0, nsteps * FFN_STEP - F)])
    gv = gv.reshape(lead + (2, nsteps, FFN_STEP))
    gv = jnp.transpose(gv, (nd + 1,) + tuple(range(nd)) + (nd, nd + 2))
    return gv.reshape((nsteps,) + lead + (2 * FFN_STEP,))


def _ffn_prepare(w_up, conv_w, conv_b, w_down):
    NL, F, D = w_down.shape
    assert F % LANES == 0 and FFN_STEP % LANES == 0
    nt = F // LANES
    tp = FFN_STEP // LANES
    nsteps = -(-F // FFN_STEP)

    def body(*refs):
        g_in, v_in, d_in = refs[:tp], refs[tp:2 * tp], refs[2 * tp:3 * tp]
        wu_o, wd_o = refs[3 * tp:]
        for t in range(tp):
            keep = tp * pl.program_id(1) + t < nt
            cols = slice(t * LANES, (t + 1) * LANES)
            wu_o[:, cols] = jnp.where(keep, g_in[t][...], 0.0).astype(bf16)
            wu_o[:, FFN_STEP + t * LANES:FFN_STEP + (t + 1) * LANES] = jnp.where(keep, v_in[t][...], 0.0).astype(bf16)
            wd_o[cols, :] = jnp.where(keep, d_in[t][...], 0.0).astype(bf16)

    tile = lambda j, t: jnp.minimum(tp * j + t, nt - 1)
    in_specs = ([pl.BlockSpec((None, D, LANES), lambda l, j, t=t: (l, 0, tile(j, t))) for t in range(tp)]
                + [pl.BlockSpec((None, D, LANES), lambda l, j, t=t: (l, 0, nt + tile(j, t))) for t in range(tp)]
                + [pl.BlockSpec((None, LANES, D), lambda l, j, t=t: (l, tile(j, t), 0)) for t in range(tp)])
    wu, wd = pl.pallas_call(
        body,
        grid=(NL, nsteps),
        in_specs=in_specs,
        out_specs=[pl.BlockSpec((None, None, D, 2 * FFN_STEP), lambda l, j: (l, j, 0, 0)),
                   pl.BlockSpec((None, None, FFN_STEP, D), lambda l, j: (l, j, 0, 0))],
        out_shape=[jax.ShapeDtypeStruct((NL, nsteps, D, 2 * FFN_STEP), bf16),
                   jax.ShapeDtypeStruct((NL, nsteps, FFN_STEP, D), bf16)],
        compiler_params=_params(("parallel", "parallel"), 40),
        name="ffn_weight_layout",
    )(*([w_up] * (2 * tp) + [w_down] * tp))
    return [dict(wu=wu, wd=wd, layer=l, F=F, cw=_ffn_to_steps(conv_w[l], F),
                 cb=_ffn_to_steps(conv_b[l], F).reshape(nsteps, 1, 2 * FFN_STEP)) for l in range(NL)]


def _conv_ffn(x, g, fw, *, seq_len, tm, prev=None, final_g=None):
    M, D = x.shape
    F = fw['F']
    layer = fw['layer']
    nsteps = fw['wu'].shape[1]
    sample = prev is not None
    rc = min(tm, 256)
    tiles_per_seq = max(seq_len // tm, 1)
    W2 = 2 * FFN_STEP

    def body(*refs):
        it = iter(refs)
        x_ref = next(it)
        xh_ref = None if sample else next(it)
        prev_refs = [next(it) for _ in range(2 * FFN_STEP // LANES)] if sample else None
        g_ref = next(it)
        wu_ref, cw_ref, cb_ref, wd_ref = next(it), next(it), next(it), next(it)
        fg_ref = next(it) if final_g is not None else None
        o_ref, tail_g_ref, tail_v_ref = next(it), next(it), next(it)
        xn_ref, up_scr = next(it), next(it)
        i = pl.program_id(0)
        j = pl.program_id(1)

        @pl.when(j == 0)
        def _():
            if sample:
                xn_ref[0:HALO, :] = jnp.zeros((HALO, D), bf16)
            else:
                keep = (i % tiles_per_seq != 0).astype(f32)
                hist = _rms(xh_ref[...], g_ref[...]) * keep
                xn_ref[0:HALO, :] = jnp.concatenate([jnp.zeros_like(hist), hist], axis=0).astype(bf16)

            def chunk(r, c):
                rows = pl.ds(pl.multiple_of(r * rc, rc), rc)
                xr = x_ref[rows, :]
                o_ref[rows, :] = xr
                xn_ref[pl.ds(pl.multiple_of(HALO + r * rc, HALO), rc), :] = _rms(xr, g_ref[...]).astype(bf16)
                return c
            lax.fori_loop(0, tm // rc, chunk, 0)

        cw = cw_ref[0]
        cb = cb_ref[0]
        nchunk = tm // rc
        for r in range(nchunk):
            if sample:
                up_scr[r, HALO:, :] = _dot(xn_ref[pl.ds(HALO + r * rc, rc), :], wu_ref[...])
            else:
                up_scr[r] = _dot(xn_ref[pl.ds(r * rc, rc + HALO), :], wu_ref[...])
        for r in range(nchunk):
            rows = pl.ds(r * rc, rc)
            up = up_scr[r, HALO:, :]
            if sample:
                p0 = jnp.concatenate([p[0, rows, :] for p in prev_refs], axis=1)
                p1 = jnp.concatenate([p[1, rows, :] for p in prev_refs], axis=1)
                conv = cb + cw[0:1] * p0 + cw[1:2] * p1 + cw[2:3] * up
                tail_g_ref[rows, :] = up[:, :FFN_STEP]
                tail_v_ref[rows, :] = up[:, FFN_STEP:]
            else:
                conv = (cb + cw[0:1] * up_scr[r, HALO - 2:HALO - 2 + rc, :]
                        + cw[1:2] * up_scr[r, HALO - 1:HALO - 1 + rc, :] + cw[2:3] * up)
                if r == nchunk - 1:
                    tail_g_ref[...] = up[rc - SUBLANES:, :FFN_STEP]
                    tail_v_ref[...] = up[rc - SUBLANES:, FFN_STEP:]
            h = jax.nn.gelu(conv[:, :FFN_STEP]) * conv[:, FFN_STEP:]
            o_ref[rows, :] += _dot(h.astype(bf16), wd_ref[...])

        if final_g is not None:
            @pl.when(j == nsteps - 1)
            def _():
                def chunk2(r, c):
                    rows = pl.ds(pl.multiple_of(r * rc, rc), rc)
                    o_ref[rows, :] = _rms(o_ref[rows, :], fg_ref[...])
                    return c
                lax.fori_loop(0, tm // rc, chunk2, 0)

    in_specs = [pl.BlockSpec((tm, D), lambda i, j: (i, 0))]
    args = [x]
    if sample:
        nt = F // LANES
        tp = FFN_STEP // LANES
        last = 2 * nt - 1
        for off in list(range(tp)) + [nt + t for t in range(tp)]:
            in_specs.append(pl.BlockSpec((2, tm, LANES), lambda i, j, off=off: (0, i, jnp.minimum(tp * j + off, last))))
            args.append(prev)
    else:
        in_specs.append(pl.BlockSpec((SUBLANES, D), lambda i, j: (jnp.maximum(i * (tm // SUBLANES) - 1, 0), 0)))
        args.append(x)
    in_specs += [pl.BlockSpec((1, D), lambda i, j: (0, 0)),
                 pl.BlockSpec((None, None, D, W2), lambda i, j: (layer, j, 0, 0)),
                 pl.BlockSpec((1, 3, W2), lambda i, j: (j, 0, 0)),
                 pl.BlockSpec((1, 1, W2), lambda i, j: (j, 0, 0)),
                 pl.BlockSpec((None, None, FFN_STEP, D), lambda i, j: (layer, j, 0, 0))]
    args += [g.reshape(1, D), fw['wu'], fw['cw'], fw['cb'], fw['wd']]
    if final_g is not None:
        in_specs.append(pl.BlockSpec((1, D), lambda i, j: (0, 0)))
        args.append(final_g.reshape(1, D))
    FP = nsteps * FFN_STEP
    if sample:
        tail_shape = (M, FP)
        tail_spec = pl.BlockSpec((tm, FFN_STEP), lambda i, j: (i, j))
    else:
        tail_shape = (M // tm, SUBLANES, FP)
        tail_spec = pl.BlockSpec((None, SUBLANES, FFN_STEP), lambda i, j: (i, 0, j))
    out, tail_g, tail_v = pl.pallas_call(
        body,
        grid=(M // tm, nsteps),
        in_specs=in_specs,
        out_specs=[pl.BlockSpec((tm, D), lambda i, j: (i, 0)), tail_spec, tail_spec],
        out_shape=[jax.ShapeDtypeStruct((M, D), f32), jax.ShapeDtypeStruct(tail_shape, f32),
                   jax.ShapeDtypeStruct(tail_shape, f32)],
        scratch_shapes=[pltpu.VMEM((tm + HALO, D), bf16), pltpu.VMEM((tm // rc, rc + HALO, W2), f32)],
        compiler_params=_params(("parallel", "arbitrary"), 56),
        name="conv_ffn_sample" if sample else "conv_ffn",
    )(*args)
    if sample:
        up_rows = jnp.concatenate([tail_g[:, :F], tail_v[:, :F]], axis=1)
        new_buf = jnp.stack([prev[1], up_rows], axis=1)
    else:
        nseq = M // seq_len
        pick = lambda t: t.reshape(nseq, tiles_per_seq, SUBLANES, FP)[:, -1, SUBLANES - 2:, :F]
        new_buf = jnp.concatenate([pick(tail_g), pick(tail_v)], axis=-1)
    return out, new_buf


def _gla_prompt(p0, alr, w2p, b_alpha, g_norm, *, B, L, H, DK, DV, tb):
    NT = L // tb
    NC = tb // CHUNK
    C = CHUNK
    scale = DK ** -0.5
    qk_blocks = H
    v_off = 2 * H * DK // DV

    def body(q_ref, k_ref, v_ref, r_ref, a_ref, w2_ref, ba_ref, gn_ref, y_ref, s_out_ref, s_scr):
        t = pl.program_id(2)

        @pl.when(t == 0)
        def _():
            s_scr[...] = jnp.zeros_like(s_scr)

        z = _dot(a_ref[...].astype(bf16), w2_ref[...].astype(bf16)) + ba_ref[...]
        gl = jax.nn.log_sigmoid(z) * (1.0 / GLA_TAU)
        bc3 = _chunk_cumsum(gl, C).reshape(NC, C, DK)
        bl3 = bc3[:, C - 1:C, :]
        q3 = (q_ref[...] * scale).reshape(NC, C, DK)
        k3 = k_ref[...].reshape(NC, C, DK)
        v3 = v_ref[...].astype(bf16).reshape(NC, C, DV)
        qd3 = (q3 * jnp.exp(bc3)).astype(bf16)
        kd3 = (k3 * jnp.exp(-bc3)).astype(bf16)
        kdec3 = (k3 * jnp.exp(bl3 - bc3)).astype(bf16)
        causal = (lax.broadcasted_iota(jnp.int32, (1, C, C), 1) >= lax.broadcasted_iota(jnp.int32, (1, C, C), 2))
        att = jnp.where(causal, jnp.einsum('cik,cjk->cij', qd3, kd3, preferred_element_type=f32), 0.0)
        intra = jnp.einsum('cij,cjv->civ', att.astype(bf16), v3, preferred_element_type=f32)
        ds = jnp.einsum('cjk,cjv->ckv', kdec3, v3, preferred_element_type=f32)
        s = s_scr[...]
        s_in = []
        for c in range(NC):
            s_in.append(s)
            s = _col_from_row(jnp.exp(bl3[c])) * s + ds[c]
        s_scr[...] = s
        s_all = jnp.stack(s_in).astype(bf16)
        o = (intra + jnp.einsum('cik,ckv->civ', qd3, s_all, preferred_element_type=f32)).reshape(tb, DV)
        rr = r_ref[...]
        y_ref[...] = (_rms(o, gn_ref[...]) * (rr * jax.nn.sigmoid(rr))).astype(bf16)

        @pl.when(t == NT - 1)
        def _():
            s_out_ref[0, 0] = s
    return pl.pallas_call(
        body,
        grid=(B, H, NT),
        in_specs=[pl.BlockSpec((tb, DK), lambda b, h, t: (b * NT + t, h)),
                  pl.BlockSpec((tb, DK), lambda b, h, t: (b * NT + t, qk_blocks + h)),
                  pl.BlockSpec((tb, DV), lambda b, h, t: (b * NT + t, v_off + h)),
                  pl.BlockSpec((tb, DV), lambda b, h, t: (b * NT + t, v_off + H + h)),
                  pl.BlockSpec((tb, LANES), lambda b, h, t: (b * NT + t, 0)),
                  pl.BlockSpec((LANES, DK), lambda b, h, t: (0, h)),
                  pl.BlockSpec((1, DK), lambda b, h, t: (0, h)),
                  pl.BlockSpec((1, DV), lambda b, h, t: (0, h))],
        out_specs=[pl.BlockSpec((tb, DV), lambda b, h, t: (b * NT + t, h)),
                   pl.BlockSpec((1, 1, DK, DV), lambda b, h, t: (b, h, 0, 0))],
        out_shape=[jax.ShapeDtypeStruct((B * L, H * DV), bf16),
                   jax.ShapeDtypeStruct((B, H, DK, DV), f32)],
        scratch_shapes=[pltpu.VMEM((DK, DV), f32)],
        compiler_params=_params(("parallel", "parallel", "arbitrary"), 32),
        name="gla_prompt",
    )(p0, p0, p0, p0, alr, w2p, b_alpha.reshape(1, -1), g_norm.reshape(1, -1))


def _gla_sample(p0, alr, w2p, b_alpha, g_norm, s0, *, H, DK, DV):
    Bs = p0.shape[0]
    scale = DK ** -0.5
    qkw = H * DK
    vw = H * DV
    assert vw % qkw == 0
    SB = SUBLANES

    def body(q_ref, k_ref, v_ref, r_ref, a_ref, w2_ref, ba_ref, gn_ref, s_ref, y_ref, so_ref, gl_scr):
        z = _dot(a_ref[...].astype(bf16), w2_ref[...].astype(bf16)) + ba_ref[...]
        gl_scr[...] = jax.nn.log_sigmoid(z) * (1.0 / GLA_TAU)

        for h in range(H):
            ks = slice(h * DK, (h + 1) * DK)
            vs = slice(h * DV, (h + 1) * DV)
            a_t = jnp.exp(gl_scr[:, ks]).T
            k_t = k_ref[:, ks].T
            q_t = (q_ref[:, ks] * scale).T
            v = v_ref[:, vs]
            outs = []
            for s in range(SB):
                sn = a_t[:, s:s + 1] * s_ref[s, h] + k_t[:, s:s + 1] * v[s:s + 1, :]
                so_ref[s, h] = sn
                outs.append(jnp.sum(q_t[:, s:s + 1] * sn, axis=0, keepdims=True))
            o = jnp.concatenate(outs, axis=0)
            rr = r_ref[:, vs]
            y_ref[:, vs] = _rms(o, gn_ref[:, vs]) * (rr * jax.nn.sigmoid(rr))

    v_blk = 2 * qkw // vw
    return pl.pallas_call(
        body,
        grid=(Bs // SB,),
        in_specs=[pl.BlockSpec((SB, qkw), lambda b: (b, 0)),
                  pl.BlockSpec((SB, qkw), lambda b: (b, 1)),
                  pl.BlockSpec((SB, vw), lambda b: (b, v_blk)),
                  pl.BlockSpec((SB, vw), lambda b: (b, v_blk + 1)),
                  pl.BlockSpec((SB, LANES), lambda b: (b, 0)),
                  pl.BlockSpec((LANES, qkw), lambda b: (0, 0)),
                  pl.BlockSpec((1, qkw), lambda b: (0, 0)),
                  pl.BlockSpec((1, vw), lambda b: (0, 0)),
                  pl.BlockSpec((SB, H, DK, DV), lambda b: (b, 0, 0, 0))],
        out_specs=[pl.BlockSpec((SB, vw), lambda b: (b, 0)),
                   pl.BlockSpec((SB, H, DK, DV), lambda b: (b, 0, 0, 0))],
        out_shape=[jax.ShapeDtypeStruct((Bs, vw), f32),
                   jax.ShapeDtypeStruct((Bs, H, DK, DV), f32)],
        scratch_shapes=[pltpu.VMEM((SB, qkw), f32)],
        compiler_params=_params(("parallel",), 32),
        name="gla_sample",
    )(p0, p0, p0, p0, alr, w2p, b_alpha.reshape(1, -1), g_norm.reshape(1, -1), s0)


def _rg_gates(xc, wr, br, wi, bi, sp):
    xb = xc.astype(bf16)
    r = jax.nn.sigmoid(_dot(xb, wr) + br)
    i = jax.nn.sigmoid(_dot(xb, wi) + bi)
    log_a = -RG_C * r * sp
    a = jnp.exp(log_a)
    mult = jnp.sqrt(1.0 - jnp.exp(2.0 * log_a))
    return a, mult, i


def _rglru_prompt(p0, conv_w, conv_b, w_r, b_r, w_i, b_i, lam, *, B, L, x_blk, g_blk):
    M = B * L
    NB, BS, _ = w_r.shape
    assert BS == LANES
    rc = min(256, L)

    def body(x_ref, gg_ref, cw_ref, cb_ref, wr_ref, br_ref, wi_ref, bi_ref, lam_ref, y_ref, hl_ref, a_scr, b_scr):
        wr = wr_ref[...].astype(bf16)
        wi = wi_ref[...].astype(bf16)
        sp = jax.nn.softplus(-lam_ref[...])
        cw = cw_ref[...]

        def chunk(c, carry):
            r0 = pl.multiple_of(c * rc, rc)
            rows = pl.ds(r0, rc)
            x = x_ref[rows, :]
            start = (r0 % L) == 0
            prev = x_ref[pl.ds(pl.multiple_of(jnp.maximum(r0 - SUBLANES, 0), SUBLANES), SUBLANES), :]
            prev = jnp.where(start, 0.0, prev)
            xc = (cb_ref[...] + cw[0:1] * _shifted(prev, x, 3) + cw[1:2] * _shifted(prev, x, 2)
                  + cw[2:3] * _shifted(prev, x, 1) + cw[3:4] * x)
            a, mult, ig = _rg_gates(xc, wr, br_ref[...], wi, bi_ref[...], sp)
            pos = (r0 + lax.broadcasted_iota(jnp.int32, (rc, 1), 0)) % L
            mult = jnp.where(pos == 0, 1.0, mult)
            a_scr[rows, :] = a
            b_scr[rows, :] = mult * (ig * xc)
            return carry
        lax.fori_loop(0, M // rc, chunk, 0)

        def step(t, hs):
            new = []
            for b in range(B):
                row = pl.ds(b * L + t, 1)
                h = a_scr[row, :] * hs[b] + b_scr[row, :]
                b_scr[row, :] = h
                new.append(h)
            return tuple(new)
        hs = lax.fori_loop(0, L, step, tuple(jnp.zeros((1, LANES), f32) for _ in range(B)), unroll=8)
        hl_ref[...] = jnp.concatenate(hs, axis=0)

        def outc(c, carry):
            rows = pl.ds(pl.multiple_of(c * rc, rc), rc)
            y_ref[rows, :] = (b_scr[rows, :] * jax.nn.gelu(gg_ref[rows, :])).astype(bf16)
            return carry
        lax.fori_loop(0, M // rc, outc, 0)

    W = NB * BS
    return pl.pallas_call(
        body,
        grid=(NB,),
        in_specs=[pl.BlockSpec((M, LANES), lambda n: (0, x_blk + n)),
                  pl.BlockSpec((M, LANES), lambda n: (0, g_blk + n)),
                  pl.BlockSpec((4, LANES), lambda n: (0, n)),
                  pl.BlockSpec((1, LANES), lambda n: (0, n)),
                  pl.BlockSpec((None, BS, BS), lambda n: (n, 0, 0)),
                  pl.BlockSpec((1, LANES), lambda n: (0, n)),
                  pl.BlockSpec((None, BS, BS), lambda n: (n, 0, 0)),
                  pl.BlockSpec((1, LANES), lambda n: (0, n)),
                  pl.BlockSpec((1, LANES), lambda n: (0, n))],
        out_specs=[pl.BlockSpec((M, LANES), lambda n: (0, n)),
                   pl.BlockSpec((B, LANES), lambda n: (0, n))],
        out_shape=[jax.ShapeDtypeStruct((M, W), bf16), jax.ShapeDtypeStruct((B, W), f32)],
        scratch_shapes=[pltpu.VMEM((M, LANES), f32), pltpu.VMEM((M, LANES), f32)],
        compiler_params=_params(("parallel",), 48),
        name="rglru_prompt",
    )(p0, p0, conv_w, conv_b.reshape(1, W), w_r, b_r.reshape(1, W), w_i, b_i.reshape(1, W), lam.reshape(1, W))


def _rglru_sample(p0, conv_state, h0, conv_w, conv_b, w_r, b_r, w_i, b_i, lam, *, x_blk, g_blk):
    Bs = p0.shape[0]
    NB, BS, _ = w_r.shape
    W = NB * BS
    s0, s1, s2 = conv_state[:, 0], conv_state[:, 1], conv_state[:, 2]

    def body(x_ref, gg_ref, s0_ref, s1_ref, s2_ref, h0_ref, cw_ref, cb_ref, wr_ref, br_ref, wi_ref, bi_ref, lam_ref,
             y_ref, h_ref):
        cw = cw_ref[...]
        x = x_ref[...]
        xc = cb_ref[...] + cw[0:1] * s0_ref[...] + cw[1:2] * s1_ref[...] + cw[2:3] * s2_ref[...] + cw[3:4] * x
        sp = jax.nn.softplus(-lam_ref[...])
        a, mult, ig = _rg_gates(xc, wr_ref[...].astype(bf16), br_ref[...], wi_ref[...].astype(bf16), bi_ref[...], sp)
        if PAST_LEN == 0:
            mult = jnp.ones_like(mult)
        h = a * h0_ref[...] + mult * (ig * xc)
        h_ref[...] = h
        y_ref[...] = (h * jax.nn.gelu(gg_ref[...])).astype(bf16)

    blk = lambda n: (0, n)
    vec = pl.BlockSpec((1, LANES), blk)
    mat = pl.BlockSpec((Bs, LANES), blk)
    y, h = pl.pallas_call(
        body,
        grid=(NB,),
        in_specs=[pl.BlockSpec((Bs, LANES), lambda n: (0, x_blk + n)),
                  pl.BlockSpec((Bs, LANES), lambda n: (0, g_blk + n)),
                  mat, mat, mat, mat,
                  pl.BlockSpec((4, LANES), blk), vec,
                  pl.BlockSpec((None, BS, BS), lambda n: (n, 0, 0)), vec,
                  pl.BlockSpec((None, BS, BS), lambda n: (n, 0, 0)), vec, vec],
        out_specs=[mat, mat],
        out_shape=[jax.ShapeDtypeStruct((Bs, W), bf16), jax.ShapeDtypeStruct((Bs, W), f32)],
        compiler_params=_params(("parallel",), 32),
        name="rglru_sample",
    )(p0, p0, s0, s1, s2, h0, conv_w, conv_b.reshape(1, W), w_r, b_r.reshape(1, W), w_i, b_i.reshape(1, W),
      lam.reshape(1, W))
    return y, h


def _blockdiag_tiles(w, tile):
    nblk, bs, _ = w.shape
    per = tile // bs
    nt = nblk // per
    rows = jnp.tile(w.reshape(nt, tile, bs), (1, 1, per))
    on_diag = (jnp.arange(tile)[:, None] // bs) == (jnp.arange(tile)[None, :] // bs)
    return jnp.where(on_diag[None], rows, 0.0)


def _mlstm_stage1(p1, conv_w, conv_b, wq_t, wk_t, wv_t, wg, bg, *, seq_len, tr, H, DH, conv_state=None):
    M = p1.shape[0]
    W = H * DH
    NTL, TL, _ = wq_t.shape
    sample = conv_state is not None
    tiles_per_seq = max(seq_len // tr, 1)
    kscale = DH ** -0.5

    def body(*refs):
        it = iter(refs)
        x_ref = next(it)
        if sample:
            s0_ref, s1_ref, s2_ref = next(it), next(it), next(it)
        else:
            xh_ref = next(it)
        cw_ref, cb_ref, wq_ref, wk_ref, wv_ref, wg_ref, bg_ref = (next(it) for _ in range(7))
        q_ref, k_ref, v_ref, g_ref, xc_ref = (next(it) for _ in range(5))
        cw = cw_ref[...]
        x = x_ref[...]
        if sample:
            conv = cb_ref[...] + cw[0:1] * s0_ref[...] + cw[1:2] * s1_ref[...] + cw[2:3] * s2_ref[...] + cw[3:4] * x
        else:
            keep = (pl.program_id(0) % tiles_per_seq != 0).astype(f32)
            prev = xh_ref[...] * keep
            conv = (cb_ref[...] + cw[0:1] * _shifted(prev, x, 3) + cw[1:2] * _shifted(prev, x, 2)
                    + cw[2:3] * _shifted(prev, x, 1) + cw[3:4] * x)
        xc = conv * jax.nn.sigmoid(conv)
        xc_ref[...] = xc
        xcb = xc.astype(bf16)
        xb = x.astype(bf16)
        qs, ks, vs = [], [], []
        for t in range(NTL):
            cs = slice(t * TL, (t + 1) * TL)
            qs.append(_dot(xcb[:, cs], wq_ref[t]))
            ks.append(_dot(xcb[:, cs], wk_ref[t]) * kscale)
            vs.append(_dot(xb[:, cs], wv_ref[t]))
        q = jnp.concatenate(qs, axis=1)
        k = jnp.concatenate(ks, axis=1)
        v = jnp.concatenate(vs, axis=1)
        q_ref[...] = q.astype(q_ref.dtype)
        k_ref[...] = k.astype(k_ref.dtype)
        v_ref[...] = v.astype(v_ref.dtype)
        gt = (_dot(q.astype(bf16), wg_ref[0:W, :]) + _dot(k.astype(bf16), wg_ref[W:2 * W, :])
              + _dot(v.astype(bf16), wg_ref[2 * W:3 * W, :]) + bg_ref[...])
        lane = lax.broadcasted_iota(jnp.int32, gt.shape, 1)
        g_ref[...] = jnp.where(jnp.logical_and(lane >= H, lane < 2 * H), jax.nn.log_sigmoid(gt), gt)

    row = lambda i: (i, 0)
    const2 = lambda i: (0, 0)
    const3 = lambda i: (0, 0, 0)
    in_specs = [pl.BlockSpec((tr, W), row)]
    args = [p1]
    if sample:
        in_specs += [pl.BlockSpec((tr, W), row)] * 3
        args += [conv_state[:, 0], conv_state[:, 1], conv_state[:, 2]]
    else:
        in_specs.append(pl.BlockSpec((SUBLANES, W), lambda i: (jnp.maximum(i * (tr // SUBLANES) - 1, 0), 0)))
        args.append(p1)
    in_specs += [pl.BlockSpec((4, W), const2), pl.BlockSpec((1, W), const2),
                 pl.BlockSpec((NTL, TL, TL), const3), pl.BlockSpec((NTL, TL, TL), const3),
                 pl.BlockSpec((NTL, TL, TL), const3),
                 pl.BlockSpec((3 * W, LANES), const2), pl.BlockSpec((1, LANES), const2)]
    args += [conv_w, conv_b.reshape(1, W), wq_t, wk_t, wv_t, wg, bg]
    return pl.pallas_call(
        body,
        grid=(M // tr,),
        in_specs=in_specs,
        out_specs=[pl.BlockSpec((tr, W), row)] * 3 + [pl.BlockSpec((tr, LANES), row), pl.BlockSpec((tr, W), row)],
        out_shape=[jax.ShapeDtypeStruct((M, W), f32 if sample else bf16)] * 3
        + [jax.ShapeDtypeStruct((M, LANES), f32), jax.ShapeDtypeStruct((M, W), f32)],
        compiler_params=_params(("parallel",), 48),
        name="mlstm_stage1_sample" if sample else "mlstm_stage1",
    )(*args)


def _mlstm_prompt(q, k, v, gates, xc, p1, g_norm, skip, *, B, L, H, DH, tb):
    NT = L // tb
    NC = tb // CHUNK
    C = CHUNK

    def body(q_ref, k_ref, v_ref, g_ref, xc_ref, om_ref, gn_ref, sk_ref,
             y_ref, c_out, n_out, m_out, c_scr, n_scr, m_scr):
        hh = pl.program_id(1)
        t = pl.program_id(2)

        @pl.when(t == 0)
        def _():
            c_scr[...] = jnp.zeros_like(c_scr)
            n_scr[...] = jnp.zeros_like(n_scr)
            m_scr[...] = jnp.zeros_like(m_scr)

        gts = g_ref[...]
        lane = lax.broadcasted_iota(jnp.int32, (tb, LANES), 1)
        i_col = jnp.sum(jnp.where(lane == hh, gts, 0.0), axis=1, keepdims=True)
        b_col = jnp.sum(jnp.where(lane == H + hh, _chunk_cumsum(gts, C), 0.0), axis=1, keepdims=True)
        b3 = b_col.reshape(NC, C, 1)
        i3 = i_col.reshape(NC, C, 1)
        ii = lax.broadcasted_iota(jnp.int32, (1, C, C), 1)
        jj = lax.broadcasted_iota(jnp.int32, (1, C, C), 2)
        eye = ii == jj
        causal = ii >= jj
        as_row = lambda col3: jnp.sum(jnp.where(eye, jnp.broadcast_to(col3, (NC, C, C)), 0.0), axis=1, keepdims=True)
        dmat = jnp.where(causal, b3 - as_row(b3) + as_row(i3), -jnp.inf)
        rmax = jnp.max(dmat, axis=2, keepdims=True)
        b_last = b3[:, C - 1:C, :]
        m_prev = m_scr[...]
        m_in = []
        for c in range(NC):
            m_in.append(m_prev)
            m_prev = jnp.maximum(b_last[c] + m_prev, rmax[c][C - 1:C, :])
        m_scr[...] = m_prev
        inter = b3 + jnp.stack(m_in)
        m_col = jnp.maximum(inter, rmax)
        g_col = jnp.exp(inter - m_col)
        q3 = q_ref[...].reshape(NC, C, DH)
        k3 = k_ref[...].reshape(NC, C, DH)
        v3 = v_ref[...].reshape(NC, C, DH)
        s = jnp.einsum('cid,cjd->cij', q3, k3, preferred_element_type=f32) * jnp.exp(dmat - m_col)
        num = jnp.einsum('cij,cje->cie', s.astype(bf16), v3, preferred_element_type=f32)
        den = jnp.sum(s, axis=2, keepdims=True)
        m_new = m_col[:, C - 1:C, :]
        wk = jnp.exp(b_last - b3 + i3 - m_new)
        gc = jnp.exp(inter[:, C - 1:C, :] - m_new)
        kw = k3.astype(f32) * wk
        dc = jnp.einsum('cse,csd->ced', v3, kw.astype(bf16), preferred_element_type=f32)
        dn = jnp.sum(kw, axis=1, keepdims=True)
        cs = c_scr[...]
        ns = n_scr[...]
        c_in, n_in = [], []
        for c in range(NC):
            c_in.append(cs)
            n_in.append(ns)
            cs = gc[c] * cs + dc[c]
            ns = gc[c] * ns + dn[c]
        c_scr[...] = cs
        n_scr[...] = ns
        c_all = jnp.stack(c_in).astype(bf16)
        n_all = jnp.stack(n_in)
        num = num + g_col * jnp.einsum('cid,ced->cie', q3, c_all, preferred_element_type=f32)
        den = den + g_col * jnp.sum(q3.astype(f32) * n_all, axis=2, keepdims=True)
        hm = (num / jnp.maximum(jnp.abs(den), jnp.exp(-m_col))).reshape(tb, DH)
        y = (_rms(hm, gn_ref[...]) + sk_ref[...] * xc_ref[...]) * jax.nn.sigmoid(om_ref[...])
        y_ref[...] = y.astype(bf16)

        @pl.when(t == NT - 1)
        def _():
            c_out[0, 0] = cs
            n_out[0, 0] = ns
            m_out[0, 0] = m_prev

    blk = lambda b, h, t: (b * NT + t, h)
    W = H * DH
    y, c_new, n_new, m_new = pl.pallas_call(
        body,
        grid=(B, H, NT),
        in_specs=[pl.BlockSpec((tb, DH), blk), pl.BlockSpec((tb, DH), blk), pl.BlockSpec((tb, DH), blk),
                  pl.BlockSpec((tb, LANES), lambda b, h, t: (b * NT + t, 0)),
                  pl.BlockSpec((tb, DH), blk),
                  pl.BlockSpec((tb, DH), lambda b, h, t: (b * NT + t, H + h)),
                  pl.BlockSpec((1, DH), lambda b, h, t: (0, h)),
                  pl.BlockSpec((1, DH), lambda b, h, t: (0, h))],
        out_specs=[pl.BlockSpec((tb, DH), blk),
                   pl.BlockSpec((1, 1, DH, DH), lambda b, h, t: (b, h, 0, 0)),
                   pl.BlockSpec((1, 1, 1, DH), lambda b, h, t: (b, h, 0, 0)),
                   pl.BlockSpec((1, 1, 1, 1), lambda b, h, t: (b, h, 0, 0))],
        out_shape=[jax.ShapeDtypeStruct((B * L, W), bf16),
                   jax.ShapeDtypeStruct((B, H, DH, DH), f32),
                   jax.ShapeDtypeStruct((B, H, 1, DH), f32),
                   jax.ShapeDtypeStruct((B, H, 1, 1), f32)],
        scratch_shapes=[pltpu.VMEM((DH, DH), f32), pltpu.VMEM((1, DH), f32), pltpu.VMEM((1, 1), f32)],
        compiler_params=_params(("parallel", "parallel", "arbitrary"), 32),
        name="mlstm_prompt",
    )(q, k, v, gates, xc, p1, g_norm.reshape(1, W), skip.reshape(1, W))
    return y, c_new, n_new.reshape(B, H, DH), m_new.reshape(B, H)


def _mlstm_sample(q, k, v, gates, xc, p1, g_norm, skip, c0, n0, m0, *, H, DH):
    Bs = q.shape[0]
    W = H * DH
    SB = SUBLANES

    def body(q_ref, k_ref, v_ref, g_ref, xc_ref, om_ref, gn_ref, sk_ref, c_ref, n_ref, m_ref,
             y_ref, c_out, n_out, m_out):
        gts = g_ref[...]
        for h in range(H):
            cs = slice(h * DH, (h + 1) * DH)
            ig = gts[:, h:h + 1]
            fg = gts[:, H + h:H + h + 1]
            inter = fg + m_ref[:, h:h + 1]
            m = jnp.maximum(inter, ig)
            g = jnp.exp(inter - m)
            m_out[:, h:h + 1] = m
            q = q_ref[:, cs]
            kw = k_ref[:, cs] * jnp.exp(ig - m)
            nn = g * n_ref[:, cs] + kw
            n_out[:, cs] = nn
            den = jnp.sum(nn * q, axis=1, keepdims=True)
            v_t = v_ref[:, cs].T
            qb = q.astype(bf16)
            nums = []
            for s in range(SB):
                cn = g[s:s + 1, :] * c_ref[s, h] + v_t[:, s:s + 1] * kw[s:s + 1, :]
                c_out[s, h] = cn
                nums.append(_dot_nt(qb, cn.astype(bf16))[s:s + 1, :])
            num = jnp.concatenate(nums, axis=0)
            hm = num / jnp.maximum(jnp.abs(den), jnp.exp(-m))
            y_ref[:, cs] = ((_rms(hm, gn_ref[:, cs]) + sk_ref[:, cs] * xc_ref[:, cs])
                            * jax.nn.sigmoid(om_ref[:, cs]))

    per = lambda b: (b, 0)
    const2 = lambda b: (0, 0)
    y, c_new, n_new, m_new = pl.pallas_call(
        body,
        grid=(Bs // SB,),
        in_specs=[pl.BlockSpec((SB, W), per), pl.BlockSpec((SB, W), per), pl.BlockSpec((SB, W), per),
                  pl.BlockSpec((SB, LANES), per), pl.BlockSpec((SB, W), per),
                  pl.BlockSpec((SB, W), lambda b: (b, 1)),
                  pl.BlockSpec((1, W), const2), pl.BlockSpec((1, W), const2),
                  pl.BlockSpec((SB, H, DH, DH), lambda b: (b, 0, 0, 0)),
                  pl.BlockSpec((SB, W), per), pl.BlockSpec((SB, H), per)],
        out_specs=[pl.BlockSpec((SB, W), per),
                   pl.BlockSpec((SB, H, DH, DH), lambda b: (b, 0, 0, 0)),
                   pl.BlockSpec((SB, W), per), pl.BlockSpec((SB, H), per)],
        out_shape=[jax.ShapeDtypeStruct((Bs, W), f32), jax.ShapeDtypeStruct((Bs, H, DH, DH), f32),
                   jax.ShapeDtypeStruct((Bs, W), f32), jax.ShapeDtypeStruct((Bs, H), f32)],
        compiler_params=_params(("parallel",), 48),
        name="mlstm_sample",
    )(q, k, v, gates, xc, p1, g_norm.reshape(1, W), skip.reshape(1, W), c0, n0.reshape(Bs, W), m0)
    return y, c_new, n_new.reshape(Bs, H, DH), m_new


S5_TILES = 8


def _s5_layouts(lam_re, lam_im, log_dt, b_re, b_im, c_re, c_im):
    G, P = lam_re.shape
    GC = b_re.shape[2]
    T = S5_TILES
    gpt = G // T
    ns = G * P
    flat = lambda a: a.reshape(ns)
    ldt = jnp.broadcast_to(log_dt[:, None], (G, P))
    rows = [flat(a).reshape(T, 1, ns // T) for a in (lam_re, lam_im, ldt)]
    eye = jnp.eye(gpt, dtype=f32)
    bbd = [jnp.einsum('jgpc,gh->jgchp', a.reshape(T, gpt, P, GC), eye).reshape(T, gpt * GC, gpt * P) for a in (b_re, b_im)]
    cbd = [jnp.einsum('jgcp,gh->jgphc', a.reshape(T, gpt, GC, P), eye).reshape(T, gpt * P, gpt * GC) for a in (c_re, c_im)]
    return rows, bbd, cbd


def _s5_discretise(lre, lim, ldt):
    dt = jnp.exp(ldt)
    mag = jnp.exp(dt * lre)
    ar = mag * jnp.cos(dt * lim)
    ai = mag * jnp.sin(dt * lim)
    den = lre * lre + lim * lim
    cr = ((ar - 1.0) * lre + ai * lim) / den
    ci = (ai * lre - (ar - 1.0) * lim) / den
    return ar, ai, cr, ci


def _s5_mixer(p1, u_blk, rows, bbd, cbd, d_skip, w_glu, b_glu, *, B, L, tb, state=None):
    T = S5_TILES
    lre_r, lim_r, ldt_r = rows
    SW = lre_r.shape[2]
    CW = bbd[0].shape[1]
    W = T * CW
    NS = T * SW
    KT = SW // LANES
    sample = state is not None
    NT = 1 if sample else L // tb
    M = B * L

    def body(*refs):
        it = iter(refs)
        u_ref = next(it)
        if sample:
            x0r_ref, x0i_ref = next(it), next(it)
        lre_ref, lim_ref, ldt_ref = next(it), next(it), next(it)
        bre_ref, bim_ref, cre_ref, cim_ref = next(it), next(it), next(it), next(it)
        d_ref, wg_ref, bgl_ref = next(it), next(it), next(it)
        y_ref, xr_out, xi_out = next(it), next(it), next(it)
        bbr, bbi, cpair, ar_scr, ai_scr = (next(it) for _ in range(5))
        if not sample:
            sre, sim, xr_c, xi_c, yacc = (next(it) for _ in range(5))
        first = jnp.logical_and(pl.program_id(0) == 0, pl.program_id(1) == 0)

        @pl.when(first)
        def _():
            for j in range(T):
                ar, ai, cr, ci = _s5_discretise(lre_ref[j], lim_ref[j], ldt_ref[j])
                ar_scr[j] = ar
                ai_scr[j] = ai
                br = bre_ref[j]
                bi = bim_ref[j]
                bbr[j] = (cr * br - ci * bi).astype(bf16)
                bbi[j] = (cr * bi + ci * br).astype(bf16)
            for jp in range(T // 2):
                cpair[jp] = jnp.zeros((4 * SW, 2 * CW), bf16)
                for half in range(2):
                    j = 2 * jp + half
                    r0 = 2 * half * SW
                    cols = slice(half * CW, (half + 1) * CW)
                    cpair[jp, r0:r0 + SW, cols] = cre_ref[j].astype(bf16)
                    cpair[jp, r0 + SW:r0 + 2 * SW, cols] = (-cim_ref[j]).astype(bf16)

        u = u_ref[...]
        ub = u.astype(bf16)
        ys = []
        if sample:
            for j in range(T):
                cs = slice(j * SW, (j + 1) * SW)
                uj = ub[:, j * CW:(j + 1) * CW]
                ar = ar_scr[j]
                ai = ai_scr[j]
                x0r = x0r_ref[:, cs]
                x0i = x0i_ref[:, cs]
                xr = ar * x0r - ai * x0i + _dot(uj, bbr[j])
                xi = ar * x0i + ai * x0r + _dot(uj, bbi[j])
                xr_out[:, cs] = xr
                xi_out[:, cs] = xi
                ys += [xr.astype(bf16), xi.astype(bf16)]
            y = jnp.concatenate([_dot(jnp.concatenate(ys[4 * jp:4 * jp + 4], axis=1), cpair[jp]) for jp in range(T // 2)],
                                axis=1)
        else:
            t = pl.program_id(1)

            @pl.when(t == 0)
            def _():
                xr_c[...] = jnp.zeros_like(xr_c)
                xi_c[...] = jnp.zeros_like(xi_c)

            for j in range(T):
                uj = ub[:, j * CW:(j + 1) * CW]
                r = _dot(uj, bbr[j])
                im = _dot(uj, bbi[j])
                for kk in range(KT):
                    sre[kk, pl.ds(j, tb, stride=T), :] = r[:, kk * LANES:(kk + 1) * LANES]
                    sim[kk, pl.ds(j, tb, stride=T), :] = im[:, kk * LANES:(kk + 1) * LANES]
            a_r = [jnp.concatenate([ar_scr[j][:, kk * LANES:(kk + 1) * LANES] for j in range(T)], axis=0) for kk in range(KT)]
            a_i = [jnp.concatenate([ai_scr[j][:, kk * LANES:(kk + 1) * LANES] for j in range(T)], axis=0) for kk in range(KT)]

            def step(s, carry):
                xr, xi = carry
                row = pl.ds(pl.multiple_of(s * T, T), T)
                nr, ni = [], []
                for kk in range(KT):
                    r_ = a_r[kk] * xr[kk] - a_i[kk] * xi[kk] + sre[kk, row, :]
                    i_ = a_r[kk] * xi[kk] + a_i[kk] * xr[kk] + sim[kk, row, :]
                    sre[kk, row, :] = r_
                    sim[kk, row, :] = i_
                    nr.append(r_)
                    ni.append(i_)
                return tuple(nr), tuple(ni)
            xr0 = tuple(xr_c[kk] for kk in range(KT))
            xi0 = tuple(xi_c[kk] for kk in range(KT))
            xr, xi = lax.fori_loop(0, tb, step, (xr0, xi0), unroll=4)
            for kk in range(KT):
                xr_c[kk] = xr[kk]
                xi_c[kk] = xi[kk]

            @pl.when(t == NT - 1)
            def _():
                for kk in range(KT):
                    xr_out[kk] = xr[kk]
                    xi_out[kk] = xi[kk]

            for jp in range(T // 2):
                parts = []
                for j in (2 * jp, 2 * jp + 1):
                    parts += [sre[kk, pl.ds(j, tb, stride=T), :] for kk in range(KT)]
                    parts += [sim[kk, pl.ds(j, tb, stride=T), :] for kk in range(KT)]
                xp = jnp.concatenate(parts, axis=1).astype(bf16)
                yacc[:, 2 * jp * CW:(2 * jp + 2) * CW] = _dot(xp, cpair[jp])
            y = yacc[...]
        ysk = jax.nn.gelu(y + d_ref[...] * u)
        z = _dot(ysk.astype(bf16), wg_ref[...]) + bgl_ref[...]
        y_ref[...] = (ysk * jax.nn.sigmoid(z)).astype(bf16)

    c3 = lambda b, t: (0, 0, 0)
    c2 = lambda b, t: (0, 0)
    in_specs = [pl.BlockSpec((tb, W), lambda b, t: (b * NT + t, u_blk))]
    args = [p1]
    if sample:
        in_specs += [pl.BlockSpec((tb, NS), lambda b, t: (b, 0))] * 2
        args += [state[0], state[1]]
    in_specs += [pl.BlockSpec((T, 1, SW), c3)] * 3
    in_specs += [pl.BlockSpec((T, CW, SW), c3)] * 2 + [pl.BlockSpec((T, SW, CW), c3)] * 2
    in_specs += [pl.BlockSpec((1, W), c2), pl.BlockSpec((W, W), c2), pl.BlockSpec((1, W), c2)]
    args += [lre_r, lim_r, ldt_r, bbd[0], bbd[1], cbd[0], cbd[1], d_skip.reshape(1, W), w_glu, b_glu.reshape(1, W)]
    scratch = [pltpu.VMEM((T, CW, SW), bf16), pltpu.VMEM((T, CW, SW), bf16),
               pltpu.VMEM((T // 2, 4 * SW, 2 * CW), bf16),
               pltpu.VMEM((T, 1, SW), f32), pltpu.VMEM((T, 1, SW), f32)]
    if sample:
        grid = (M // tb, 1)
        st_spec = pl.BlockSpec((tb, NS), lambda b, t: (b, 0))
        st_shape = jax.ShapeDtypeStruct((M, NS), f32)
    else:
        grid = (B, NT)
        st_spec = pl.BlockSpec((None, KT, T, LANES), lambda b, t: (b, 0, 0, 0))
        st_shape = jax.ShapeDtypeStruct((B, KT, T, LANES), f32)
        scratch += [pltpu.VMEM((KT, tb * T, LANES), f32), pltpu.VMEM((KT, tb * T, LANES), f32),
                    pltpu.VMEM((KT, T, LANES), f32), pltpu.VMEM((KT, T, LANES), f32), pltpu.VMEM((tb, W), f32)]
    y, xr, xi = pl.pallas_call(
        body,
        grid=grid,
        in_specs=in_specs,
        out_specs=[pl.BlockSpec((tb, W), lambda b, t: (b * NT + t, 0)), st_spec, st_spec],
        out_shape=[jax.ShapeDtypeStruct((M, W), bf16), st_shape, st_shape],
        scratch_shapes=scratch,
        compiler_params=_params(("arbitrary", "arbitrary"), 56),
        name="s5_sample" if sample else "s5_prompt",
    )(*args)
    if not sample:
        xr = jnp.transpose(xr, (0, 2, 1, 3)).reshape(B, NS)
        xi = jnp.transpose(xi, (0, 2, 1, 3)).reshape(B, NS)
    return y, xr, xi


def _trunk(x3, st, w, *, sample):
    B, L, D = x3.shape
    M = B * L
    x = x3.reshape(M, D)
    H_g, DK, DV = st['gla_S'].shape[1:] if sample else w['gla_dims']
    H_m, DH = w['ml_dims']
    G, P = w['s5_dims']
    tm = M if sample else min(1024, L)
    tf = M if sample else min(1024, L)
    tb = min(512, L)
    out = {}

    p0, alr = _norm_matmul(x, w['g_mix0'], w['w_in0_main'], tm=tm, tn=512, w_side=w['w_in0_alr'])
    x_blk = (2 * H_g * DK + 2 * H_g * DV) // LANES
    W_rg = w['rg_lambda'].shape[0]
    g_blk = x_blk + W_rg // LANES
    rg_tail = p0.reshape(B, L, p0.shape[1])[:, max(L - 3, 0):, x_blk * LANES:x_blk * LANES + W_rg]
    if sample:
        ya, out['gla_S'] = _gla_sample(p0, alr, w['gla_w2p'], w['gla_b_alpha'], w['gla_g_norm'], st['gla_S'],
                                       H=H_g, DK=DK, DV=DV)
        yb, out['rg_h'] = _rglru_sample(p0, st['rg_conv'], st['rg_h'], w['rg_conv_w'], w['rg_conv_b'], w['rg_w_r'],
                                        w['rg_b_r'], w['rg_w_i'], w['rg_b_i'], w['rg_lambda'], x_blk=x_blk, g_blk=g_blk)
        out['rg_conv'] = jnp.concatenate([st['rg_conv'][:, 1:], rg_tail], axis=1)
    else:
        ya, out['gla_S'] = _gla_prompt(p0, alr, w['gla_w2p'], w['gla_b_alpha'], w['gla_g_norm'],
                                       B=B, L=L, H=H_g, DK=DK, DV=DV, tb=tb)
        yb, out['rg_h'] = _rglru_prompt(p0, w['rg_conv_w'], w['rg_conv_b'], w['rg_w_r'], w['rg_b_r'], w['rg_w_i'],
                                        w['rg_b_i'], w['rg_lambda'], B=B, L=L, x_blk=x_blk, g_blk=g_blk)
        out['rg_conv'] = rg_tail
    x = _out_proj(ya, yb, w['w_out0'], x, tm=tm, tn=512)
    x, ffn0 = _conv_ffn(x, w['g_ffn'][0], w['ffn'][0], seq_len=L, tm=tf, prev=jnp.transpose(st['ffn_conv'][0], (1, 0, 2)) if sample else None)

    W_ml = H_m * DH
    p1 = _norm_matmul(x, w['g_mix1'], w['w_in1'], tm=tm, tn=512)
    xm_tail = p1.reshape(B, L, p1.shape[1])[:, max(L - 3, 0):, :W_ml]
    q, k, v, gates, xc = _mlstm_stage1(p1, w['ml_conv_w'], w['ml_conv_b'], w['ml_wq_t'], w['ml_wk_t'], w['ml_wv_t'],
                                       w['ml_wg'], w['ml_bg'], seq_len=L, tr=M if sample else min(256, L), H=H_m, DH=DH,
                                       conv_state=st['ml_conv'] if sample else None)
    u_blk = 2 * W_ml // (G * w['s5_gc'])
    if sample:
        yc, out['ml_C'], out['ml_n'], out['ml_m'] = _mlstm_sample(
            q, k, v, gates, xc, p1, w['ml_g_norm'], w['ml_skip'], st['ml_C'], st['ml_n'], st['ml_m'], H=H_m, DH=DH)
        out['ml_conv'] = jnp.concatenate([st['ml_conv'][:, 1:], xm_tail], axis=1)
        yd, s5r, s5i = _s5_mixer(p1, u_blk, w['s5_rows'], w['s5_bbd'], w['s5_cbd'], w['s5_D'], w['s5_w_glu'],
                                 w['s5_b_glu'], B=B, L=1, tb=B,
                                 state=(st['s5_re'].reshape(B, G * P), st['s5_im'].reshape(B, G * P)))
    else:
        yc, out['ml_C'], out['ml_n'], out['ml_m'] = _mlstm_prompt(
            q, k, v, gates, xc, p1, w['ml_g_norm'], w['ml_skip'], B=B, L=L, H=H_m, DH=DH, tb=tb)
        out['ml_conv'] = xm_tail
        yd, s5r, s5i = _s5_mixer(p1, u_blk, w['s5_rows'], w['s5_bbd'], w['s5_cbd'], w['s5_D'], w['s5_w_glu'],
                                 w['s5_b_glu'], B=B, L=L, tb=min(256, L))
    out['s5_re'] = s5r.reshape(B, G, P)
    out['s5_im'] = s5i.reshape(B, G, P)
    x = _out_proj(yc, yd, w['w_out1'], x, tm=tm, tn=512)
    x, ffn1 = _conv_ffn(x, w['g_ffn'][1], w['ffn'][1], seq_len=L, tm=tf, prev=jnp.transpose(st['ffn_conv'][1], (1, 0, 2)) if sample else None,
                        final_g=w['g_final'])
    out['ffn_conv'] = jnp.stack([ffn0, ffn1], axis=0)
    return x.reshape(B, L, D), out


def kernel(x_prompt, x_sample, state_gla_S, state_rglru_h, state_rglru_conv, state_mlstm_C, state_mlstm_n, state_mlstm_m, state_mlstm_conv, state_s5_re, state_s5_im, state_ffn_conv, g_mix0, w_in0, gla_w_alpha2, gla_b_alpha, gla_g_norm, rg_conv_w, rg_conv_b, rg_w_r, rg_b_r, rg_w_i, rg_b_i, rg_lambda, w_out0, g_mix1, w_in1, ml_conv_w, ml_conv_b, ml_wq, ml_wk, ml_wv, ml_w_igate, ml_b_igate, ml_w_fgate, ml_b_fgate, ml_g_norm, ml_skip, s5_lam_re, s5_lam_im, s5_log_dt, s5_B_re, s5_B_im, s5_C_re, s5_C_im, s5_D, s5_w_glu, s5_b_glu, w_out1, g_ffn, ffn_w_up, ffn_conv_w, ffn_conv_b, ffn_w_down, g_final):
    _, H_g, DK, DV = state_gla_S.shape
    _, H_m, DH, _ = state_mlstm_C.shape
    G, P = s5_lam_re.shape
    rank = gla_w_alpha2.shape[0]
    n_main = 2 * H_g * DK + 2 * H_g * DV
    w_in0_main = jnp.concatenate([w_in0[:, :n_main], w_in0[:, n_main + rank:]], axis=1).astype(bf16)
    w_in0_alr = jnp.pad(w_in0[:, n_main:n_main + rank], ((0, 0), (0, LANES - rank))).astype(bf16)
    gla_w2p = jnp.pad(gla_w_alpha2, ((0, LANES - rank), (0, 0)))
    ml_tile = 256
    ml_wg = jnp.pad(jnp.concatenate([ml_w_igate, ml_w_fgate], axis=1), ((0, 0), (0, LANES - 2 * H_m))).astype(bf16)
    ml_bg = jnp.pad(jnp.concatenate([ml_b_igate, ml_b_fgate]), (0, LANES - 2 * H_m)).reshape(1, LANES)
    rows, bbd, cbd = _s5_layouts(s5_lam_re, s5_lam_im, s5_log_dt, s5_B_re, s5_B_im, s5_C_re, s5_C_im)
    w = dict(
        g_mix0=g_mix0, w_in0_main=w_in0_main, w_in0_alr=w_in0_alr, gla_w2p=gla_w2p, gla_b_alpha=gla_b_alpha,
        gla_g_norm=gla_g_norm, gla_dims=(H_g, DK, DV), rg_conv_w=rg_conv_w, rg_conv_b=rg_conv_b, rg_w_r=rg_w_r,
        rg_b_r=rg_b_r, rg_w_i=rg_w_i, rg_b_i=rg_b_i, rg_lambda=rg_lambda, w_out0=w_out0.astype(bf16),
        g_mix1=g_mix1, w_in1=w_in1.astype(bf16), ml_conv_w=ml_conv_w, ml_conv_b=ml_conv_b,
        ml_wq_t=_blockdiag_tiles(ml_wq, ml_tile).astype(bf16), ml_wk_t=_blockdiag_tiles(ml_wk, ml_tile).astype(bf16),
        ml_wv_t=_blockdiag_tiles(ml_wv, ml_tile).astype(bf16), ml_wg=ml_wg, ml_bg=ml_bg, ml_g_norm=ml_g_norm,
        ml_skip=ml_skip, ml_dims=(H_m, DH), s5_dims=(G, P), s5_gc=s5_B_re.shape[2], s5_rows=rows, s5_bbd=bbd,
        s5_cbd=cbd, s5_D=s5_D, s5_w_glu=s5_w_glu.astype(bf16), s5_b_glu=s5_b_glu, w_out1=w_out1.astype(bf16),
        g_ffn=g_ffn, g_final=g_final,
        ffn=_ffn_prepare(ffn_w_up, ffn_conv_w, ffn_conv_b, ffn_w_down))
    st_s = dict(gla_S=state_gla_S, rg_h=state_rglru_h, rg_conv=state_rglru_conv, ml_C=state_mlstm_C,
                ml_n=state_mlstm_n, ml_m=state_mlstm_m, ml_conv=state_mlstm_conv, s5_re=state_s5_re,
                s5_im=state_s5_im, ffn_conv=state_ffn_conv)
    y_p, np_ = _trunk(x_prompt, None, w, sample=False)
    y_s, ns_ = _trunk(x_sample, st_s, w, sample=True)
    names = ('gla_S', 'rg_h', 'rg_conv', 'ml_C', 'ml_n', 'ml_m', 'ml_conv', 's5_re', 's5_im', 'ffn_conv')
    outs = [y_p, y_s]
    for nme in names:
        outs += [np_[nme], ns_[nme]]
    return tuple(outs)
```

```python
import functools

import jax
import jax.numpy as jnp
from jax import lax
from jax.experimental import pallas as pl
from jax.experimental.pallas import tpu as pltpu

f32 = jnp.float32
bf16 = jnp.bfloat16

EPS = 1e-6
CHUNK = 64
GLA_TAU = 16.0
RG_C = 8.0
PAST_LEN = 16384
LANES = 128
SUBLANES = 8
HALO = 16
MIB = 1024 * 1024


def _params(sem, vmem_mib):
    return pltpu.CompilerParams(dimension_semantics=sem, vmem_limit_bytes=int(vmem_mib * MIB))


def _dot(a, b):
    return jnp.dot(a, b, preferred_element_type=f32)


def _dot_nt(a, b):
    return lax.dot_general(a, b, (((1,), (1,)), ((), ())), preferred_element_type=f32)


def _dot_tn(a, b):
    return lax.dot_general(a, b, (((0,), (0,)), ((), ())), preferred_element_type=f32)


def _rms(x, g):
    return x * lax.rsqrt(jnp.mean(x * x, axis=-1, keepdims=True) + EPS) * g


def _eye(n):
    return lax.broadcasted_iota(jnp.int32, (n, n), 0) == lax.broadcasted_iota(jnp.int32, (n, n), 1)


def _col_from_row(row):
    n = row.shape[1]
    return jnp.sum(jnp.where(_eye(n), jnp.broadcast_to(row, (n, n)), 0.0), axis=1, keepdims=True)


def _row_from_col(col):
    n = col.shape[0]
    return jnp.sum(jnp.where(_eye(n), jnp.broadcast_to(col, (n, n)), 0.0), axis=0, keepdims=True)


def _chunk_cumsum(x, chunk):
    pos = lax.broadcasted_iota(jnp.int32, (x.shape[0], 1), 0) % chunk
    step = 1
    while step < chunk:
        x = x + jnp.where(pos >= step, pltpu.roll(x, step, 0), 0.0)
        step *= 2
    return x


def _shifted(prev8, x, back):
    xx = jnp.concatenate([prev8, x], axis=0)
    n = x.shape[0]
    return xx[SUBLANES - back:SUBLANES - back + n]


def _norm_matmul(x, g, ws, *, tm, tn, w_side=None):
    M, D = x.shape
    nblk = [w.shape[1] // tn for w in ws]
    start = [sum(nblk[:s]) for s in range(len(ws) + 1)]
    N = tn * start[-1]
    rc = min(tm, 256)
    side = w_side is not None

    def body(*refs):
        x_ref, g_ref = refs[:2]
        w_refs = refs[2:2 + len(ws)]
        rest = refs[2 + len(ws):]
        if side:
            ws_ref, o_ref, os_ref, xn_ref = rest
        else:
            o_ref, xn_ref = rest
        j = pl.program_id(1)

        @pl.when(j == 0)
        def _():
            def chunk(r, c):
                rows = pl.ds(pl.multiple_of(r * rc, rc), rc)
                xn_ref[rows, :] = _rms(x_ref[rows, :], g_ref[...]).astype(bf16)
                return c
            lax.fori_loop(0, tm // rc, chunk, 0)
            if side:
                os_ref[...] = _dot(xn_ref[...], ws_ref[...])

        if len(ws) == 1:
            o_ref[...] = _dot(xn_ref[...], w_refs[0][...])
        else:
            for s, w_ref in enumerate(w_refs):
                @pl.when(jnp.logical_and(j >= start[s], j < start[s + 1]))
                def _(w_ref=w_ref):
                    o_ref[...] = _dot(xn_ref[...], w_ref[...])

    in_specs = [pl.BlockSpec((tm, D), lambda i, j: (i, 0)),
                pl.BlockSpec((1, D), lambda i, j: (0, 0))]
    in_specs += [pl.BlockSpec((D, tn), lambda i, j, s=s: (0, jnp.clip(j - start[s], 0, nblk[s] - 1)))
                 for s in range(len(ws))]
    out_specs = [pl.BlockSpec((tm, tn), lambda i, j: (i, j))]
    out_shape = [jax.ShapeDtypeStruct((M, N), f32)]
    args = [x, g.reshape(1, D)] + list(ws)
    if side:
        ns = w_side.shape[1]
        in_specs.append(pl.BlockSpec((D, ns), lambda i, j: (0, 0)))
        out_specs.append(pl.BlockSpec((tm, ns), lambda i, j: (i, 0)))
        out_shape.append(jax.ShapeDtypeStruct((M, ns), f32))
        args.append(w_side)
    outs = pl.pallas_call(
        body,
        grid=(M // tm, N // tn),
        in_specs=in_specs,
        out_specs=out_specs,
        out_shape=out_shape,
        scratch_shapes=[pltpu.VMEM((tm, D), bf16)],
        compiler_params=_params(("parallel", "arbitrary"), 48),
        name="norm_matmul",
    )(*args)
    return outs if side else outs[0]


def _out_proj(ya, yb, w, res, *, tm, tn):
    M, Ka = ya.shape
    Kb = yb.shape[1]
    N = w.shape[1]
    assert Ka == Kb and w.shape[0] == Ka + Kb

    def body(ya_ref, yb_ref, wa_ref, wb_ref, r_ref, o_ref):
        o_ref[...] = (r_ref[...] + _dot(ya_ref[...].astype(bf16), wa_ref[...])
                      + _dot(yb_ref[...].astype(bf16), wb_ref[...]))

    return pl.pallas_call(
        body,
        grid=(M // tm, N // tn),
        in_specs=[pl.BlockSpec((tm, Ka), lambda i, j: (i, 0)),
                  pl.BlockSpec((tm, Kb), lambda i, j: (i, 0)),
                  pl.BlockSpec((Ka, tn), lambda i, j: (0, j)),
                  pl.BlockSpec((Kb, tn), lambda i, j: (1, j)),
                  pl.BlockSpec((tm, tn), lambda i, j: (i, j))],
        out_specs=pl.BlockSpec((tm, tn), lambda i, j: (i, j)),
        out_shape=jax.ShapeDtypeStruct((M, N), f32),
        compiler_params=_params(("parallel", "arbitrary"), 48),
        name="out_proj",
    )(ya, yb, w, w, res)


FFN_STEP = 512


def _ffn_to_steps(a, F):
    nsteps = -(-F // FFN_STEP)
    lead = a.shape[:-1]
    nd = len(lead)
    gv = a.reshape(lead + (2, F))
    gv = jnp.pad(gv, [(0, 0)] * (nd + 1) + [(0, nsteps * FFN_STEP - F)])
    gv = gv.reshape(lead + (2, nsteps, FFN_STEP))
    gv = jnp.transpose(gv, (nd + 1,) + tuple(range(nd)) + (nd, nd + 2))
    return gv.reshape((nsteps,) + lead + (2 * FFN_STEP,))


def _ffn_prepare(w_up, conv_w, conv_b, w_down):
    NL, F, D = w_down.shape
    assert F % LANES == 0 and FFN_STEP % LANES == 0
    nt = F // LANES
    tp = FFN_STEP // LANES
    nsteps = -(-F // FFN_STEP)

    def body(*refs):
        g_in, v_in, d_in = refs[:tp], refs[tp:2 * tp], refs[2 * tp:3 * tp]
        wu_o, wd_o = refs[3 * tp:]
        for t in range(tp):
            keep = tp * pl.program_id(1) + t < nt
            cols = slice(t * LANES, (t + 1) * LANES)
            wu_o[:, cols] = jnp.where(keep, g_in[t][...], 0.0).astype(bf16)
            wu_o[:, FFN_STEP + t * LANES:FFN_STEP + (t + 1) * LANES] = jnp.where(keep, v_in[t][...], 0.0).astype(bf16)
            wd_o[cols, :] = jnp.where(keep, d_in[t][...], 0.0).astype(bf16)

    tile = lambda j, t: jnp.minimum(tp * j + t, nt - 1)
    in_specs = ([pl.BlockSpec((None, D, LANES), lambda l, j, t=t: (l, 0, tile(j, t))) for t in range(tp)]
                + [pl.BlockSpec((None, D, LANES), lambda l, j, t=t: (l, 0, nt + tile(j, t))) for t in range(tp)]
                + [pl.BlockSpec((None, LANES, D), lambda l, j, t=t: (l, tile(j, t), 0)) for t in range(tp)])
    wu, wd = pl.pallas_call(
        body,
        grid=(NL, nsteps),
        in_specs=in_specs,
        out_specs=[pl.BlockSpec((None, None, D, 2 * FFN_STEP), lambda l, j: (l, j, 0, 0)),
                   pl.BlockSpec((None, None, FFN_STEP, D), lambda l, j: (l, j, 0, 0))],
        out_shape=[jax.ShapeDtypeStruct((NL, nsteps, D, 2 * FFN_STEP), bf16),
                   jax.ShapeDtypeStruct((NL, nsteps, FFN_STEP, D), bf16)],
        compiler_params=_params(("parallel", "parallel"), 40),
        name="ffn_weight_layout",
    )(*([w_up] * (2 * tp) + [w_down] * tp))
    return [dict(wu=wu, wd=wd, layer=l, F=F, cw=_ffn_to_steps(conv_w[l], F),
                 cb=_ffn_to_steps(conv_b[l], F).reshape(nsteps, 1, 2 * FFN_STEP)) for l in range(NL)]


def _conv_ffn(x, g, fw, *, seq_len, tm, prev=None, final_g=None):
    M, D = x.shape
    F = fw['F']
    layer = fw['layer']
    nsteps = fw['wu'].shape[1]
    sample = prev is not None
    rc = min(tm, 256)
    tiles_per_seq = max(seq_len // tm, 1)
    W2 = 2 * FFN_STEP

    def body(*refs):
        it = iter(refs)
        x_ref = next(it)
        xh_ref = None if sample else next(it)
        prev_refs = [next(it) for _ in range(2 * FFN_STEP // LANES)] if sample else None
        g_ref = next(it)
        wu_ref, cw_ref, cb_ref, wd_ref = next(it), next(it), next(it), next(it)
        fg_ref = next(it) if final_g is not None else None
        o_ref, tail_g_ref, tail_v_ref = next(it), next(it), next(it)
        xn_ref, up_scr = next(it), next(it)
        i = pl.program_id(0)
        j = pl.program_id(1)

        @pl.when(j == 0)
        def _():
            if sample:
                xn_ref[0:HALO, :] = jnp.zeros((HALO, D), bf16)
            else:
                keep = (i % tiles_per_seq != 0).astype(f32)
                hist = _rms(xh_ref[...], g_ref[...]) * keep
                xn_ref[0:HALO, :] = jnp.concatenate([jnp.zeros_like(hist), hist], axis=0).astype(bf16)

            def chunk(r, c):
                rows = pl.ds(pl.multiple_of(r * rc, rc), rc)
                xr = x_ref[rows, :]
                o_ref[rows, :] = xr
                xn_ref[pl.ds(pl.multiple_of(HALO + r * rc, HALO), rc), :] = _rms(xr, g_ref[...]).astype(bf16)
                return c
            lax.fori_loop(0, tm // rc, chunk, 0)

        cw = cw_ref[0]
        cb = cb_ref[0]
        nchunk = tm // rc
        for r in range(nchunk):
            if sample:
                up_scr[r, HALO:, :] = _dot(xn_ref[pl.ds(HALO + r * rc, rc), :], wu_ref[...])
            else:
                up_scr[r] = _dot(xn_ref[pl.ds(r * rc, rc + HALO), :], wu_ref[...])
        for r in range(nchunk):
            rows = pl.ds(r * rc, rc)
            up = up_scr[r, HALO:, :]
            if sample:
                p0 = jnp.concatenate([p[rows, 0, :] for p in prev_refs], axis=1)
                p1 = jnp.concatenate([p[rows, 1, :] for p in prev_refs], axis=1)
                conv = cb + cw[0:1] * p0 + cw[1:2] * p1 + cw[2:3] * up
                tail_g_ref[rows, :] = up[:, :FFN_STEP]
                tail_v_ref[rows, :] = up[:, FFN_STEP:]
            else:
                conv = (cb + cw[0:1] * up_scr[r, HALO - 2:HALO - 2 + rc, :]
                        + cw[1:2] * up_scr[r, HALO - 1:HALO - 1 + rc, :] + cw[2:3] * up)
                if r == nchunk - 1:
                    tail_g_ref[...] = up[rc - SUBLANES:, :FFN_STEP]
                    tail_v_ref[...] = up[rc - SUBLANES:, FFN_STEP:]
            h = jax.nn.gelu(conv[:, :FFN_STEP]) * conv[:, FFN_STEP:]
            o_ref[rows, :] += _dot(h.astype(bf16), wd_ref[...])

        if final_g is not None:
            @pl.when(j == nsteps - 1)
            def _():
                def chunk2(r, c):
                    rows = pl.ds(pl.multiple_of(r * rc, rc), rc)
                    o_ref[rows, :] = _rms(o_ref[rows, :], fg_ref[...])
                    return c
                lax.fori_loop(0, tm // rc, chunk2, 0)

    in_specs = [pl.BlockSpec((tm, D), lambda i, j: (i, 0))]
    args = [x]
    if sample:
        nt = F // LANES
        tp = FFN_STEP // LANES
        last = 2 * nt - 1
        for off in list(range(tp)) + [nt + t for t in range(tp)]:
            in_specs.append(pl.BlockSpec((None, tm, 2, LANES),
                                         lambda i, j, off=off: (layer, i, 0, jnp.minimum(tp * j + off, last))))
            args.append(prev)
    else:
        in_specs.append(pl.BlockSpec((SUBLANES, D), lambda i, j: (jnp.maximum(i * (tm // SUBLANES) - 1, 0), 0)))
        args.append(x)
    in_specs += [pl.BlockSpec((1, D), lambda i, j: (0, 0)),
                 pl.BlockSpec((None, None, D, W2), lambda i, j: (layer, j, 0, 0)),
                 pl.BlockSpec((1, 3, W2), lambda i, j: (j, 0, 0)),
                 pl.BlockSpec((1, 1, W2), lambda i, j: (j, 0, 0)),
                 pl.BlockSpec((None, None, FFN_STEP, D), lambda i, j: (layer, j, 0, 0))]
    args += [g.reshape(1, D), fw['wu'], fw['cw'], fw['cb'], fw['wd']]
    if final_g is not None:
        in_specs.append(pl.BlockSpec((1, D), lambda i, j: (0, 0)))
        args.append(final_g.reshape(1, D))
    FP = nsteps * FFN_STEP
    if sample:
        tail_shape = (M, FP)
        tail_spec = pl.BlockSpec((tm, FFN_STEP), lambda i, j: (i, j))
    else:
        tail_shape = (M // tm, SUBLANES, FP)
        tail_spec = pl.BlockSpec((None, SUBLANES, FFN_STEP), lambda i, j: (i, 0, j))
    out, tail_g, tail_v = pl.pallas_call(
        body,
        grid=(M // tm, nsteps),
        in_specs=in_specs,
        out_specs=[pl.BlockSpec((tm, D), lambda i, j: (i, 0)), tail_spec, tail_spec],
        out_shape=[jax.ShapeDtypeStruct((M, D), f32), jax.ShapeDtypeStruct(tail_shape, f32),
                   jax.ShapeDtypeStruct(tail_shape, f32)],
        scratch_shapes=[pltpu.VMEM((tm + HALO, D), bf16), pltpu.VMEM((tm // rc, rc + HALO, W2), f32)],
        compiler_params=_params(("parallel", "arbitrary"), 56),
        name="conv_ffn_sample" if sample else "conv_ffn",
    )(*args)
    if sample:
        up_rows = jnp.concatenate([tail_g[:, :F], tail_v[:, :F]], axis=1)
        new_buf = jnp.stack([prev[layer, :, 1, :], up_rows], axis=1)
    else:
        nseq = M // seq_len
        pick = lambda t: t.reshape(nseq, tiles_per_seq, SUBLANES, FP)[:, -1, SUBLANES - 2:, :F]
        new_buf = jnp.concatenate([pick(tail_g), pick(tail_v)], axis=-1)
    return out, new_buf


def _gla_prompt(p0, alr, w2p, b_alpha, g_norm, *, B, L, H, DK, DV, tb):
    NT = L // tb
    NC = tb // CHUNK
    C = CHUNK
    scale = DK ** -0.5
    qk_blocks = H
    v_off = 2 * H * DK // DV

    def body(q_ref, k_ref, v_ref, r_ref, a_ref, w2_ref, ba_ref, gn_ref, y_ref, s_out_ref, s_scr):
        t = pl.program_id(2)

        @pl.when(t == 0)
        def _():
            s_scr[...] = jnp.zeros_like(s_scr)

        z = _dot(a_ref[...].astype(bf16), w2_ref[...].astype(bf16)) + ba_ref[...]
        gl = jax.nn.log_sigmoid(z) * (1.0 / GLA_TAU)
        bc3 = _chunk_cumsum(gl, C).reshape(NC, C, DK)
        bl3 = bc3[:, C - 1:C, :]
        q3 = (q_ref[...] * scale).reshape(NC, C, DK)
        k3 = k_ref[...].reshape(NC, C, DK)
        v3 = v_ref[...].astype(bf16).reshape(NC, C, DV)
        qd3 = (q3 * jnp.exp(bc3)).astype(bf16)
        kd3 = (k3 * jnp.exp(-bc3)).astype(bf16)
        kdec3 = (k3 * jnp.exp(bl3 - bc3)).astype(bf16)
        causal = (lax.broadcasted_iota(jnp.int32, (1, C, C), 1) >= lax.broadcasted_iota(jnp.int32, (1, C, C), 2))
        att = jnp.where(causal, jnp.einsum('cik,cjk->cij', qd3, kd3, preferred_element_type=f32), 0.0)
        intra = jnp.einsum('cij,cjv->civ', att.astype(bf16), v3, preferred_element_type=f32)
        ds = jnp.einsum('cjk,cjv->ckv', kdec3, v3, preferred_element_type=f32)
        s = s_scr[...]
        s_in = []
        for c in range(NC):
            s_in.append(s)
            s = _col_from_row(jnp.exp(bl3[c])) * s + ds[c]
        s_scr[...] = s
        s_all = jnp.stack(s_in).astype(bf16)
        o = (intra + jnp.einsum('cik,ckv->civ', qd3, s_all, preferred_element_type=f32)).reshape(tb, DV)
        rr = r_ref[...]
        y_ref[...] = (_rms(o, gn_ref[...]) * (rr * jax.nn.sigmoid(rr))).astype(bf16)

        @pl.when(t == NT - 1)
        def _():
            s_out_ref[0, 0] = s
    return pl.pallas_call(
        body,
        grid=(B, H, NT),
        in_specs=[pl.BlockSpec((tb, DK), lambda b, h, t: (b * NT + t, h)),
                  pl.BlockSpec((tb, DK), lambda b, h, t: (b * NT + t, qk_blocks + h)),
                  pl.BlockSpec((tb, DV), lambda b, h, t: (b * NT + t, v_off + h)),
                  pl.BlockSpec((tb, DV), lambda b, h, t: (b * NT + t, v_off + H + h)),
                  pl.BlockSpec((tb, LANES), lambda b, h, t: (b * NT + t, 0)),
                  pl.BlockSpec((LANES, DK), lambda b, h, t: (0, h)),
                  pl.BlockSpec((1, DK), lambda b, h, t: (0, h)),
                  pl.BlockSpec((1, DV), lambda b, h, t: (0, h))],
        out_specs=[pl.BlockSpec((tb, DV), lambda b, h, t: (b * NT + t, h)),
                   pl.BlockSpec((1, 1, DK, DV), lambda b, h, t: (b, h, 0, 0))],
        out_shape=[jax.ShapeDtypeStruct((B * L, H * DV), bf16),
                   jax.ShapeDtypeStruct((B, H, DK, DV), f32)],
        scratch_shapes=[pltpu.VMEM((DK, DV), f32)],
        compiler_params=_params(("parallel", "parallel", "arbitrary"), 32),
        name="gla_prompt",
    )(p0, p0, p0, p0, alr, w2p, b_alpha.reshape(1, -1), g_norm.reshape(1, -1))


def _gla_sample(p0, alr, w2p, b_alpha, g_norm, s0, *, H, DK, DV):
    Bs = p0.shape[0]
    scale = DK ** -0.5
    qkw = H * DK
    vw = H * DV
    assert vw % qkw == 0
    SB = SUBLANES

    def body(q_ref, k_ref, v_ref, r_ref, a_ref, w2_ref, ba_ref, gn_ref, s_ref, y_ref, so_ref, gl_scr):
        z = _dot(a_ref[...].astype(bf16), w2_ref[...].astype(bf16)) + ba_ref[...]
        gl_scr[...] = jax.nn.log_sigmoid(z) * (1.0 / GLA_TAU)

        for h in range(H):
            ks = slice(h * DK, (h + 1) * DK)
            vs = slice(h * DV, (h + 1) * DV)
            a_t = jnp.exp(gl_scr[:, ks]).T
            k_t = k_ref[:, ks].T
            q_t = (q_ref[:, ks] * scale).T
            v = v_ref[:, vs]
            outs = []
            for s in range(SB):
                sn = a_t[:, s:s + 1] * s_ref[s, h] + k_t[:, s:s + 1] * v[s:s + 1, :]
                so_ref[s, h] = sn
                outs.append(jnp.sum(q_t[:, s:s + 1] * sn, axis=0, keepdims=True))
            o = jnp.concatenate(outs, axis=0)
            rr = r_ref[:, vs]
            y_ref[:, vs] = _rms(o, gn_ref[:, vs]) * (rr * jax.nn.sigmoid(rr))

    v_blk = 2 * qkw // vw
    return pl.pallas_call(
        body,
        grid=(Bs // SB,),
        in_specs=[pl.BlockSpec((SB, qkw), lambda b: (b, 0)),
                  pl.BlockSpec((SB, qkw), lambda b: (b, 1)),
                  pl.BlockSpec((SB, vw), lambda b: (b, v_blk)),
                  pl.BlockSpec((SB, vw), lambda b: (b, v_blk + 1)),
                  pl.BlockSpec((SB, LANES), lambda b: (b, 0)),
                  pl.BlockSpec((LANES, qkw), lambda b: (0, 0)),
                  pl.BlockSpec((1, qkw), lambda b: (0, 0)),
                  pl.BlockSpec((1, vw), lambda b: (0, 0)),
                  pl.BlockSpec((SB, H, DK, DV), lambda b: (b, 0, 0, 0))],
        out_specs=[pl.BlockSpec((SB, vw), lambda b: (b, 0)),
                   pl.BlockSpec((SB, H, DK, DV), lambda b: (b, 0, 0, 0))],
        out_shape=[jax.ShapeDtypeStruct((Bs, vw), f32),
                   jax.ShapeDtypeStruct((Bs, H, DK, DV), f32)],
        scratch_shapes=[pltpu.VMEM((SB, qkw), f32)],
        compiler_params=_params(("parallel",), 32),
        name="gla_sample",
    )(p0, p0, p0, p0, alr, w2p, b_alpha.reshape(1, -1), g_norm.reshape(1, -1), s0)


def _rg_gates(xc, wr, br, wi, bi, sp):
    xb = xc.astype(bf16)
    r = jax.nn.sigmoid(_dot(xb, wr) + br)
    i = jax.nn.sigmoid(_dot(xb, wi) + bi)
    log_a = -RG_C * r * sp
    a = jnp.exp(log_a)
    mult = jnp.sqrt(1.0 - a * a)
    return a, mult, i


def _rglru_prompt(p0, conv_w, conv_b, w_r, b_r, w_i, b_i, lam, *, B, L, x_blk, g_blk):
    M = B * L
    NB, BS, _ = w_r.shape
    assert BS == LANES
    rc = min(256, L)

    def body(x_ref, gg_ref, cw_ref, cb_ref, wr_ref, br_ref, wi_ref, bi_ref, lam_ref, y_ref, hl_ref, a_scr, b_scr):
        wr = wr_ref[...].astype(bf16)
        wi = wi_ref[...].astype(bf16)
        sp = jax.nn.softplus(-lam_ref[...])
        cw = cw_ref[...]

        def chunk(c, carry):
            r0 = pl.multiple_of(c * rc, rc)
            rows = pl.ds(r0, rc)
            x = x_ref[rows, :]
            start = (r0 % L) == 0
            prev = x_ref[pl.ds(pl.multiple_of(jnp.maximum(r0 - SUBLANES, 0), SUBLANES), SUBLANES), :]
            prev = jnp.where(start, 0.0, prev)
            xc = (cb_ref[...] + cw[0:1] * _shifted(prev, x, 3) + cw[1:2] * _shifted(prev, x, 2)
                  + cw[2:3] * _shifted(prev, x, 1) + cw[3:4] * x)
            a, mult, ig = _rg_gates(xc, wr, br_ref[...], wi, bi_ref[...], sp)
            pos = (r0 + lax.broadcasted_iota(jnp.int32, (rc, 1), 0)) % L
            mult = jnp.where(pos == 0, 1.0, mult)
            a_scr[rows, :] = a
            b_scr[rows, :] = mult * (ig * xc)
            return carry
        lax.fori_loop(0, M // rc, chunk, 0)

        def step(t, hs):
            new = []
            for b in range(B):
                row = pl.ds(b * L + t, 1)
                h = a_scr[row, :] * hs[b] + b_scr[row, :]
                b_scr[row, :] = h
                new.append(h)
            return tuple(new)
        hs = lax.fori_loop(0, L, step, tuple(jnp.zeros((1, LANES), f32) for _ in range(B)), unroll=8)
        hl_ref[...] = jnp.concatenate(hs, axis=0)

        def outc(c, carry):
            rows = pl.ds(pl.multiple_of(c * rc, rc), rc)
            y_ref[rows, :] = (b_scr[rows, :] * jax.nn.gelu(gg_ref[rows, :])).astype(bf16)
            return carry
        lax.fori_loop(0, M // rc, outc, 0)

    W = NB * BS
    return pl.pallas_call(
        body,
        grid=(NB,),
        in_specs=[pl.BlockSpec((M, LANES), lambda n: (0, x_blk + n)),
                  pl.BlockSpec((M, LANES), lambda n: (0, g_blk + n)),
                  pl.BlockSpec((4, LANES), lambda n: (0, n)),
                  pl.BlockSpec((1, LANES), lambda n: (0, n)),
                  pl.BlockSpec((None, BS, BS), lambda n: (n, 0, 0)),
                  pl.BlockSpec((1, LANES), lambda n: (0, n)),
                  pl.BlockSpec((None, BS, BS), lambda n: (n, 0, 0)),
                  pl.BlockSpec((1, LANES), lambda n: (0, n)),
                  pl.BlockSpec((1, LANES), lambda n: (0, n))],
        out_specs=[pl.BlockSpec((M, LANES), lambda n: (0, n)),
                   pl.BlockSpec((B, LANES), lambda n: (0, n))],
        out_shape=[jax.ShapeDtypeStruct((M, W), bf16), jax.ShapeDtypeStruct((B, W), f32)],
        scratch_shapes=[pltpu.VMEM((M, LANES), f32), pltpu.VMEM((M, LANES), f32)],
        compiler_params=_params(("parallel",), 48),
        name="rglru_prompt",
    )(p0, p0, conv_w, conv_b.reshape(1, W), w_r, b_r.reshape(1, W), w_i, b_i.reshape(1, W), lam.reshape(1, W))


def _rglru_sample(p0, conv_state, h0, conv_w, conv_b, w_r, b_r, w_i, b_i, lam, *, x_blk, g_blk):
    Bs = p0.shape[0]
    NB, BS, _ = w_r.shape
    W = NB * BS
    s0, s1, s2 = conv_state[:, 0], conv_state[:, 1], conv_state[:, 2]

    def body(x_ref, gg_ref, s0_ref, s1_ref, s2_ref, h0_ref, cw_ref, cb_ref, wr_ref, br_ref, wi_ref, bi_ref, lam_ref,
             y_ref, h_ref):
        cw = cw_ref[...]
        x = x_ref[...]
        xc = cb_ref[...] + cw[0:1] * s0_ref[...] + cw[1:2] * s1_ref[...] + cw[2:3] * s2_ref[...] + cw[3:4] * x
        sp = jax.nn.softplus(-lam_ref[...])
        a, mult, ig = _rg_gates(xc, wr_ref[...].astype(bf16), br_ref[...], wi_ref[...].astype(bf16), bi_ref[...], sp)
        if PAST_LEN == 0:
            mult = jnp.ones_like(mult)
        h = a * h0_ref[...] + mult * (ig * xc)
        h_ref[...] = h
        y_ref[...] = (h * jax.nn.gelu(gg_ref[...])).astype(bf16)

    blk = lambda n: (0, n)
    vec = pl.BlockSpec((1, LANES), blk)
    mat = pl.BlockSpec((Bs, LANES), blk)
    y, h = pl.pallas_call(
        body,
        grid=(NB,),
        in_specs=[pl.BlockSpec((Bs, LANES), lambda n: (0, x_blk + n)),
                  pl.BlockSpec((Bs, LANES), lambda n: (0, g_blk + n)),
                  mat, mat, mat, mat,
                  pl.BlockSpec((4, LANES), blk), vec,
                  pl.BlockSpec((None, BS, BS), lambda n: (n, 0, 0)), vec,
                  pl.BlockSpec((None, BS, BS), lambda n: (n, 0, 0)), vec, vec],
        out_specs=[mat, mat],
        out_shape=[jax.ShapeDtypeStruct((Bs, W), bf16), jax.ShapeDtypeStruct((Bs, W), f32)],
        compiler_params=_params(("parallel",), 32),
        name="rglru_sample",
    )(p0, p0, s0, s1, s2, h0, conv_w, conv_b.reshape(1, W), w_r, b_r.reshape(1, W), w_i, b_i.reshape(1, W),
      lam.reshape(1, W))
    return y, h


def _blockdiag_tiles(w, tile):
    nblk, bs, _ = w.shape
    per = tile // bs
    nt = nblk // per
    rows = jnp.tile(w.reshape(nt, tile, bs), (1, 1, per))
    on_diag = (jnp.arange(tile)[:, None] // bs) == (jnp.arange(tile)[None, :] // bs)
    return jnp.where(on_diag[None], rows, 0.0)


def _mlstm_stage1(p1, conv_w, conv_b, wq_t, wk_t, wv_t, wg, bg, *, seq_len, tr, H, DH, conv_state=None):
    M = p1.shape[0]
    W = H * DH
    NTL, TL, _ = wq_t.shape
    sample = conv_state is not None
    tiles_per_seq = max(seq_len // tr, 1)
    kscale = DH ** -0.5

    def body(*refs):
        it = iter(refs)
        x_ref = next(it)
        if sample:
            s0_ref, s1_ref, s2_ref = next(it), next(it), next(it)
        else:
            xh_ref = next(it)
        cw_ref, cb_ref, wq_ref, wk_ref, wv_ref, wg_ref, bg_ref = (next(it) for _ in range(7))
        q_ref, k_ref, v_ref, g_ref, xc_ref = (next(it) for _ in range(5))
        cw = cw_ref[...]
        x = x_ref[...]
        if sample:
            conv = cb_ref[...] + cw[0:1] * s0_ref[...] + cw[1:2] * s1_ref[...] + cw[2:3] * s2_ref[...] + cw[3:4] * x
        else:
            keep = (pl.program_id(0) % tiles_per_seq != 0).astype(f32)
            prev = xh_ref[...] * keep
            conv = (cb_ref[...] + cw[0:1] * _shifted(prev, x, 3) + cw[1:2] * _shifted(prev, x, 2)
                    + cw[2:3] * _shifted(prev, x, 1) + cw[3:4] * x)
        xc = conv * jax.nn.sigmoid(conv)
        xc_ref[...] = xc
        xcb = xc.astype(bf16)
        xb = x.astype(bf16)
        qs, ks, vs = [], [], []
        for t in range(NTL):
            cs = slice(t * TL, (t + 1) * TL)
            qs.append(_dot(xcb[:, cs], wq_ref[t]))
            ks.append(_dot(xcb[:, cs], wk_ref[t]) * kscale)
            vs.append(_dot(xb[:, cs], wv_ref[t]))
        q = jnp.concatenate(qs, axis=1)
        k = jnp.concatenate(ks, axis=1)
        v = jnp.concatenate(vs, axis=1)
        q_ref[...] = q.astype(q_ref.dtype)
        k_ref[...] = k.astype(k_ref.dtype)
        v_ref[...] = v.astype(v_ref.dtype)
        gt = (_dot(q.astype(bf16), wg_ref[0:W, :]) + _dot(k.astype(bf16), wg_ref[W:2 * W, :])
              + _dot(v.astype(bf16), wg_ref[2 * W:3 * W, :]) + bg_ref[...])
        lane = lax.broadcasted_iota(jnp.int32, gt.shape, 1)
        g_ref[...] = jnp.where(jnp.logical_and(lane >= H, lane < 2 * H), jax.nn.log_sigmoid(gt), gt)

    row = lambda i: (i, 0)
    const2 = lambda i: (0, 0)
    const3 = lambda i: (0, 0, 0)
    in_specs = [pl.BlockSpec((tr, W), row)]
    args = [p1]
    if sample:
        in_specs += [pl.BlockSpec((tr, W), row)] * 3
        args += [conv_state[:, 0], conv_state[:, 1], conv_state[:, 2]]
    else:
        in_specs.append(pl.BlockSpec((SUBLANES, W), lambda i: (jnp.maximum(i * (tr // SUBLANES) - 1, 0), 0)))
        args.append(p1)
    in_specs += [pl.BlockSpec((4, W), const2), pl.BlockSpec((1, W), const2),
                 pl.BlockSpec((NTL, TL, TL), const3), pl.BlockSpec((NTL, TL, TL), const3),
                 pl.BlockSpec((NTL, TL, TL), const3),
                 pl.BlockSpec((3 * W, LANES), const2), pl.BlockSpec((1, LANES), const2)]
    args += [conv_w, conv_b.reshape(1, W), wq_t, wk_t, wv_t, wg, bg]
    return pl.pallas_call(
        body,
        grid=(M // tr,),
        in_specs=in_specs,
        out_specs=[pl.BlockSpec((tr, W), row)] * 3 + [pl.BlockSpec((tr, LANES), row), pl.BlockSpec((tr, W), row)],
        out_shape=[jax.ShapeDtypeStruct((M, W), f32 if sample else bf16)] * 3
        + [jax.ShapeDtypeStruct((M, LANES), f32), jax.ShapeDtypeStruct((M, W), f32)],
        compiler_params=_params(("parallel",), 48),
        name="mlstm_stage1_sample" if sample else "mlstm_stage1",
    )(*args)


def _mlstm_prompt(q, k, v, gates, xc, p1, g_norm, skip, *, B, L, H, DH, tb):
    NT = L // tb
    NC = tb // CHUNK
    C = CHUNK

    def body(q_ref, k_ref, v_ref, g_ref, xc_ref, om_ref, gn_ref, sk_ref,
             y_ref, c_out, n_out, m_out, c_scr, n_scr, m_scr):
        hh = pl.program_id(1)
        t = pl.program_id(2)

        @pl.when(t == 0)
        def _():
            c_scr[...] = jnp.zeros_like(c_scr)
            n_scr[...] = jnp.zeros_like(n_scr)
            m_scr[...] = jnp.zeros_like(m_scr)

        gts = g_ref[...]
        lane = lax.broadcasted_iota(jnp.int32, (tb, LANES), 1)
        i_col = jnp.sum(jnp.where(lane == hh, gts, 0.0), axis=1, keepdims=True)
        b_col = jnp.sum(jnp.where(lane == H + hh, _chunk_cumsum(gts, C), 0.0), axis=1, keepdims=True)
        b3 = b_col.reshape(NC, C, 1)
        i3 = i_col.reshape(NC, C, 1)
        ii = lax.broadcasted_iota(jnp.int32, (1, C, C), 1)
        jj = lax.broadcasted_iota(jnp.int32, (1, C, C), 2)
        eye = ii == jj
        causal = ii >= jj
        as_row = lambda col3: jnp.sum(jnp.where(eye, jnp.broadcast_to(col3, (NC, C, C)), 0.0), axis=1, keepdims=True)
        dmat = jnp.where(causal, b3 - as_row(b3) + as_row(i3), -jnp.inf)
        rmax = jnp.max(dmat, axis=2, keepdims=True)
        b_last = b3[:, C - 1:C, :]
        m_prev = m_scr[...]
        m_in = []
        for c in range(NC):
            m_in.append(m_prev)
            m_prev = jnp.maximum(b_last[c] + m_prev, rmax[c][C - 1:C, :])
        m_scr[...] = m_prev
        inter = b3 + jnp.stack(m_in)
        m_col = jnp.maximum(inter, rmax)
        g_col = jnp.exp(inter - m_col)
        q3 = q_ref[...].reshape(NC, C, DH)
        k3 = k_ref[...].reshape(NC, C, DH)
        v3 = v_ref[...].reshape(NC, C, DH)
        s = jnp.einsum('cid,cjd->cij', q3, k3, preferred_element_type=f32) * jnp.exp(dmat - m_col)
        num = jnp.einsum('cij,cje->cie', s.astype(bf16), v3, preferred_element_type=f32)
        den = jnp.sum(s, axis=2, keepdims=True)
        m_new = m_col[:, C - 1:C, :]
        wk = jnp.exp(b_last - b3 + i3 - m_new)
        gc = jnp.exp(inter[:, C - 1:C, :] - m_new)
        kw = k3.astype(f32) * wk
        dc = jnp.einsum('cse,csd->ced', v3, kw.astype(bf16), preferred_element_type=f32)
        dn = jnp.sum(kw, axis=1, keepdims=True)
        cs = c_scr[...]
        ns = n_scr[...]
        c_in, n_in = [], []
        for c in range(NC):
            c_in.append(cs)
            n_in.append(ns)
            cs = gc[c] * cs + dc[c]
            ns = gc[c] * ns + dn[c]
        c_scr[...] = cs
        n_scr[...] = ns
        c_all = jnp.stack(c_in).astype(bf16)
        n_all = jnp.stack(n_in)
        num = num + g_col * jnp.einsum('cid,ced->cie', q3, c_all, preferred_element_type=f32)
        den = den + g_col * jnp.sum(q3.astype(f32) * n_all, axis=2, keepdims=True)
        hm = (num / jnp.maximum(jnp.abs(den), jnp.exp(-m_col))).reshape(tb, DH)
        y = (_rms(hm, gn_ref[...]) + sk_ref[...] * xc_ref[...]) * jax.nn.sigmoid(om_ref[...])
        y_ref[...] = y.astype(bf16)

        @pl.when(t == NT - 1)
        def _():
            c_out[0, 0] = cs
            n_out[0, 0] = ns
            m_out[0, 0] = m_prev

    blk = lambda b, h, t: (b * NT + t, h)
    W = H * DH
    y, c_new, n_new, m_new = pl.pallas_call(
        body,
        grid=(B, H, NT),
        in_specs=[pl.BlockSpec((tb, DH), blk), pl.BlockSpec((tb, DH), blk), pl.BlockSpec((tb, DH), blk),
                  pl.BlockSpec((tb, LANES), lambda b, h, t: (b * NT + t, 0)),
                  pl.BlockSpec((tb, DH), blk),
                  pl.BlockSpec((tb, DH), lambda b, h, t: (b * NT + t, H + h)),
                  pl.BlockSpec((1, DH), lambda b, h, t: (0, h)),
                  pl.BlockSpec((1, DH), lambda b, h, t: (0, h))],
        out_specs=[pl.BlockSpec((tb, DH), blk),
                   pl.BlockSpec((1, 1, DH, DH), lambda b, h, t: (b, h, 0, 0)),
                   pl.BlockSpec((1, 1, 1, DH), lambda b, h, t: (b, h, 0, 0)),
                   pl.BlockSpec((1, 1, 1, 1), lambda b, h, t: (b, h, 0, 0))],
        out_shape=[jax.ShapeDtypeStruct((B * L, W), bf16),
                   jax.ShapeDtypeStruct((B, H, DH, DH), f32),
                   jax.ShapeDtypeStruct((B, H, 1, DH), f32),
                   jax.ShapeDtypeStruct((B, H, 1, 1), f32)],
        scratch_shapes=[pltpu.VMEM((DH, DH), f32), pltpu.VMEM((1, DH), f32), pltpu.VMEM((1, 1), f32)],
        compiler_params=_params(("parallel", "parallel", "arbitrary"), 32),
        name="mlstm_prompt",
    )(q, k, v, gates, xc, p1, g_norm.reshape(1, W), skip.reshape(1, W))
    return y, c_new, n_new.reshape(B, H, DH), m_new.reshape(B, H)


def _mlstm_sample(q, k, v, gates, xc, p1, g_norm, skip, c0, n0, m0, *, H, DH):
    Bs = q.shape[0]
    W = H * DH
    SB = SUBLANES

    def body(q_ref, k_ref, v_ref, g_ref, xc_ref, om_ref, gn_ref, sk_ref, c_ref, n_ref, m_ref,
             y_ref, c_out, n_out, m_out):
        gts = g_ref[...]
        for h in range(H):
            cs = slice(h * DH, (h + 1) * DH)
            ig = gts[:, h:h + 1]
            fg = gts[:, H + h:H + h + 1]
            inter = fg + m_ref[:, h:h + 1]
            m = jnp.maximum(inter, ig)
            g = jnp.exp(inter - m)
            m_out[:, h:h + 1] = m
            q = q_ref[:, cs]
            kw = k_ref[:, cs] * jnp.exp(ig - m)
            nn = g * n_ref[:, cs] + kw
            n_out[:, cs] = nn
            den = jnp.sum(nn * q, axis=1, keepdims=True)
            v_t = v_ref[:, cs].T
            qb = q.astype(bf16)
            nums = []
            for s in range(SB):
                cn = g[s:s + 1, :] * c_ref[s, h] + v_t[:, s:s + 1] * kw[s:s + 1, :]
                c_out[s, h] = cn
                nums.append(_dot_nt(qb, cn.astype(bf16))[s:s + 1, :])
            num = jnp.concatenate(nums, axis=0)
            hm = num / jnp.maximum(jnp.abs(den), jnp.exp(-m))
            y_ref[:, cs] = ((_rms(hm, gn_ref[:, cs]) + sk_ref[:, cs] * xc_ref[:, cs])
                            * jax.nn.sigmoid(om_ref[:, cs]))

    per = lambda b: (b, 0)
    const2 = lambda b: (0, 0)
    y, c_new, n_new, m_new = pl.pallas_call(
        body,
        grid=(Bs // SB,),
        in_specs=[pl.BlockSpec((SB, W), per), pl.BlockSpec((SB, W), per), pl.BlockSpec((SB, W), per),
                  pl.BlockSpec((SB, LANES), per), pl.BlockSpec((SB, W), per),
                  pl.BlockSpec((SB, W), lambda b: (b, 1)),
                  pl.BlockSpec((1, W), const2), pl.BlockSpec((1, W), const2),
                  pl.BlockSpec((SB, H, DH, DH), lambda b: (b, 0, 0, 0)),
                  pl.BlockSpec((SB, W), per), pl.BlockSpec((SB, H), per)],
        out_specs=[pl.BlockSpec((SB, W), per),
                   pl.BlockSpec((SB, H, DH, DH), lambda b: (b, 0, 0, 0)),
                   pl.BlockSpec((SB, W), per), pl.BlockSpec((SB, H), per)],
        out_shape=[jax.ShapeDtypeStruct((Bs, W), f32), jax.ShapeDtypeStruct((Bs, H, DH, DH), f32),
                   jax.ShapeDtypeStruct((Bs, W), f32), jax.ShapeDtypeStruct((Bs, H), f32)],
        compiler_params=_params(("parallel",), 48),
        name="mlstm_sample",
    )(q, k, v, gates, xc, p1, g_norm.reshape(1, W), skip.reshape(1, W), c0, n0.reshape(Bs, W), m0)
    return y, c_new, n_new.reshape(Bs, H, DH), m_new


S5_TILES = 8


def _s5_layouts(lam_re, lam_im, log_dt, b_re, b_im, c_re, c_im):
    G, P = lam_re.shape
    GC = b_re.shape[2]
    T = S5_TILES
    gpt = G // T
    ns = G * P
    flat = lambda a: a.reshape(ns)
    ldt = jnp.broadcast_to(log_dt[:, None], (G, P))
    rows = [flat(a).reshape(T, 1, ns // T) for a in (lam_re, lam_im, ldt)]
    eye = jnp.eye(gpt, dtype=f32)
    bbd = [jnp.einsum('jgpc,gh->jgchp', a.reshape(T, gpt, P, GC), eye).reshape(T, gpt * GC, gpt * P) for a in (b_re, b_im)]
    cbd = [jnp.einsum('jgcp,gh->jgphc', a.reshape(T, gpt, GC, P), eye).reshape(T, gpt * P, gpt * GC) for a in (c_re, c_im)]
    return rows, bbd, cbd


def _s5_discretise(lre, lim, ldt):
    dt = jnp.exp(ldt)
    mag = jnp.exp(dt * lre)
    ar = mag * jnp.cos(dt * lim)
    ai = mag * jnp.sin(dt * lim)
    den = lre * lre + lim * lim
    cr = ((ar - 1.0) * lre + ai * lim) / den
    ci = (ai * lre - (ar - 1.0) * lim) / den
    return ar, ai, cr, ci


def _s5_mixer(p1, u_blk, rows, bbd, cbd, d_skip, w_glu, b_glu, *, B, L, tb, state=None):
    T = S5_TILES
    lre_r, lim_r, ldt_r = rows
    SW = lre_r.shape[2]
    CW = bbd[0].shape[1]
    W = T * CW
    NS = T * SW
    KT = SW // LANES
    sample = state is not None
    NT = 1 if sample else L // tb
    M = B * L

    def body(*refs):
        it = iter(refs)
        u_ref = next(it)
        if sample:
            x0r_ref, x0i_ref = next(it), next(it)
        lre_ref, lim_ref, ldt_ref = next(it), next(it), next(it)
        bre_ref, bim_ref, cre_ref, cim_ref = next(it), next(it), next(it), next(it)
        d_ref, wg_ref, bgl_ref = next(it), next(it), next(it)
        y_ref, xr_out, xi_out = next(it), next(it), next(it)
        bbr, bbi, cpair, ar_scr, ai_scr = (next(it) for _ in range(5))
        if not sample:
            sre, sim, xr_c, xi_c, yacc = (next(it) for _ in range(5))
        first = jnp.logical_and(pl.program_id(0) == 0, pl.program_id(1) == 0)

        @pl.when(first)
        def _():
            for j in range(T):
                ar, ai, cr, ci = _s5_discretise(lre_ref[j], lim_ref[j], ldt_ref[j])
                ar_scr[j] = ar
                ai_scr[j] = ai
                br = bre_ref[j]
                bi = bim_ref[j]
                bbr[j] = (cr * br - ci * bi).astype(bf16)
                bbi[j] = (cr * bi + ci * br).astype(bf16)
            for jp in range(T // 2):
                cpair[jp] = jnp.zeros((4 * SW, 2 * CW), bf16)
                for half in range(2):
                    j = 2 * jp + half
                    r0 = 2 * half * SW
                    cols = slice(half * CW, (half + 1) * CW)
                    cpair[jp, r0:r0 + SW, cols] = cre_ref[j].astype(bf16)
                    cpair[jp, r0 + SW:r0 + 2 * SW, cols] = (-cim_ref[j]).astype(bf16)

        u = u_ref[...]
        ub = u.astype(bf16)
        ys = []
        if sample:
            for j in range(T):
                cs = slice(j * SW, (j + 1) * SW)
                uj = ub[:, j * CW:(j + 1) * CW]
                ar = ar_scr[j]
                ai = ai_scr[j]
                x0r = x0r_ref[:, cs]
                x0i = x0i_ref[:, cs]
                xr = ar * x0r - ai * x0i + _dot(uj, bbr[j])
                xi = ar * x0i + ai * x0r + _dot(uj, bbi[j])
                xr_out[:, cs] = xr
                xi_out[:, cs] = xi
                ys += [xr.astype(bf16), xi.astype(bf16)]
            y = jnp.concatenate([_dot(jnp.concatenate(ys[4 * jp:4 * jp + 4], axis=1), cpair[jp]) for jp in range(T // 2)],
                                axis=1)
        else:
            t = pl.program_id(1)

            @pl.when(t == 0)
            def _():
                xr_c[...] = jnp.zeros_like(xr_c)
                xi_c[...] = jnp.zeros_like(xi_c)

            for j in range(T):
                uj = ub[:, j * CW:(j + 1) * CW]
                r = _dot(uj, bbr[j])
                im = _dot(uj, bbi[j])
                for kk in range(KT):
                    sre[kk, pl.ds(j, tb, stride=T), :] = r[:, kk * LANES:(kk + 1) * LANES]
                    sim[kk, pl.ds(j, tb, stride=T), :] = im[:, kk * LANES:(kk + 1) * LANES]
            a_r = [jnp.concatenate([ar_scr[j][:, kk * LANES:(kk + 1) * LANES] for j in range(T)], axis=0) for kk in range(KT)]
            a_i = [jnp.concatenate([ai_scr[j][:, kk * LANES:(kk + 1) * LANES] for j in range(T)], axis=0) for kk in range(KT)]

            def step(s, carry):
                xr, xi = carry
                row = pl.ds(pl.multiple_of(s * T, T), T)
                nr, ni = [], []
                for kk in range(KT):
                    r_ = a_r[kk] * xr[kk] - a_i[kk] * xi[kk] + sre[kk, row, :]
                    i_ = a_r[kk] * xi[kk] + a_i[kk] * xr[kk] + sim[kk, row, :]
                    sre[kk, row, :] = r_
                    sim[kk, row, :] = i_
                    nr.append(r_)
                    ni.append(i_)
                return tuple(nr), tuple(ni)
            xr0 = tuple(xr_c[kk] for kk in range(KT))
            xi0 = tuple(xi_c[kk] for kk in range(KT))
            xr, xi = lax.fori_loop(0, tb, step, (xr0, xi0), unroll=4)
            for kk in range(KT):
                xr_c[kk] = xr[kk]
                xi_c[kk] = xi[kk]

            @pl.when(t == NT - 1)
            def _():
                for kk in range(KT):
                    xr_out[kk] = xr[kk]
                    xi_out[kk] = xi[kk]

            for jp in range(T // 2):
                parts = []
                for j in (2 * jp, 2 * jp + 1):
                    parts += [sre[kk, pl.ds(j, tb, stride=T), :] for kk in range(KT)]
                    parts += [sim[kk, pl.ds(j, tb, stride=T), :] for kk in range(KT)]
                xp = jnp.concatenate(parts, axis=1).astype(bf16)
                yacc[:, 2 * jp * CW:(2 * jp + 2) * CW] = _dot(xp, cpair[jp])
            y = yacc[...]
        ysk = jax.nn.gelu(y + d_ref[...] * u)
        z = _dot(ysk.astype(bf16), wg_ref[...]) + bgl_ref[...]
        y_ref[...] = (ysk * jax.nn.sigmoid(z)).astype(bf16)

    c3 = lambda b, t: (0, 0, 0)
    c2 = lambda b, t: (0, 0)
    in_specs = [pl.BlockSpec((tb, W), lambda b, t: (b * NT + t, u_blk))]
    args = [p1]
    if sample:
        in_specs += [pl.BlockSpec((tb, NS), lambda b, t: (b, 0))] * 2
        args += [state[0], state[1]]
    in_specs += [pl.BlockSpec((T, 1, SW), c3)] * 3
    in_specs += [pl.BlockSpec((T, CW, SW), c3)] * 2 + [pl.BlockSpec((T, SW, CW), c3)] * 2
    in_specs += [pl.BlockSpec((1, W), c2), pl.BlockSpec((W, W), c2), pl.BlockSpec((1, W), c2)]
    args += [lre_r, lim_r, ldt_r, bbd[0], bbd[1], cbd[0], cbd[1], d_skip.reshape(1, W), w_glu, b_glu.reshape(1, W)]
    scratch = [pltpu.VMEM((T, CW, SW), bf16), pltpu.VMEM((T, CW, SW), bf16),
               pltpu.VMEM((T // 2, 4 * SW, 2 * CW), bf16),
               pltpu.VMEM((T, 1, SW), f32), pltpu.VMEM((T, 1, SW), f32)]
    if sample:
        grid = (M // tb, 1)
        st_spec = pl.BlockSpec((tb, NS), lambda b, t: (b, 0))
        st_shape = jax.ShapeDtypeStruct((M, NS), f32)
    else:
        grid = (B, NT)
        st_spec = pl.BlockSpec((None, KT, T, LANES), lambda b, t: (b, 0, 0, 0))
        st_shape = jax.ShapeDtypeStruct((B, KT, T, LANES), f32)
        scratch += [pltpu.VMEM((KT, tb * T, LANES), f32), pltpu.VMEM((KT, tb * T, LANES), f32),
                    pltpu.VMEM((KT, T, LANES), f32), pltpu.VMEM((KT, T, LANES), f32), pltpu.VMEM((tb, W), f32)]
    y, xr, xi = pl.pallas_call(
        body,
        grid=grid,
        in_specs=in_specs,
        out_specs=[pl.BlockSpec((tb, W), lambda b, t: (b * NT + t, 0)), st_spec, st_spec],
        out_shape=[jax.ShapeDtypeStruct((M, W), bf16), st_shape, st_shape],
        scratch_shapes=scratch,
        compiler_params=_params(("arbitrary", "arbitrary"), 56),
        name="s5_sample" if sample else "s5_prompt",
    )(*args)
    if not sample:
        xr = jnp.transpose(xr, (0, 2, 1, 3)).reshape(B, NS)
        xi = jnp.transpose(xi, (0, 2, 1, 3)).reshape(B, NS)
    return y, xr, xi


def _trunk(x3, st, w, *, sample):
    B, L, D = x3.shape
    M = B * L
    x = x3.reshape(M, D)
    H_g, DK, DV = st['gla_S'].shape[1:] if sample else w['gla_dims']
    H_m, DH = w['ml_dims']
    G, P = w['s5_dims']
    tm = M if sample else min(1024, L)
    tf = M if sample else min(1024, L)
    to = M if sample else min(2048, M)
    tb = min(512, L)
    out = {}

    p0, alr = _norm_matmul(x, w['g_mix0'], w['w_in0_main'], tm=tm, tn=512, w_side=w['w_in0_alr'])
    x_blk = (2 * H_g * DK + 2 * H_g * DV) // LANES
    W_rg = w['rg_lambda'].shape[0]
    g_blk = x_blk + W_rg // LANES
    rg_tail = p0.reshape(B, L, p0.shape[1])[:, max(L - 3, 0):, x_blk * LANES:x_blk * LANES + W_rg]
    if sample:
        ya, out['gla_S'] = _gla_sample(p0, alr, w['gla_w2p'], w['gla_b_alpha'], w['gla_g_norm'], st['gla_S'],
                                       H=H_g, DK=DK, DV=DV)
        yb, out['rg_h'] = _rglru_sample(p0, st['rg_conv'], st['rg_h'], w['rg_conv_w'], w['rg_conv_b'], w['rg_w_r'],
                                        w['rg_b_r'], w['rg_w_i'], w['rg_b_i'], w['rg_lambda'], x_blk=x_blk, g_blk=g_blk)
        out['rg_conv'] = jnp.concatenate([st['rg_conv'][:, 1:], rg_tail], axis=1)
    else:
        ya, out['gla_S'] = _gla_prompt(p0, alr, w['gla_w2p'], w['gla_b_alpha'], w['gla_g_norm'],
                                       B=B, L=L, H=H_g, DK=DK, DV=DV, tb=tb)
        yb, out['rg_h'] = _rglru_prompt(p0, w['rg_conv_w'], w['rg_conv_b'], w['rg_w_r'], w['rg_b_r'], w['rg_w_i'],
                                        w['rg_b_i'], w['rg_lambda'], B=B, L=L, x_blk=x_blk, g_blk=g_blk)
        out['rg_conv'] = rg_tail
    x = _out_proj(ya, yb, w['w_out0'], x, tm=to, tn=512)
    x, ffn0 = _conv_ffn(x, w['g_ffn'][0], w['ffn'][0], seq_len=L, tm=tf, prev=st['ffn_conv'] if sample else None)

    W_ml = H_m * DH
    p1 = _norm_matmul(x, w['g_mix1'], [w['w_in1']], tm=tm, tn=512)
    xm_tail = p1.reshape(B, L, p1.shape[1])[:, max(L - 3, 0):, :W_ml]
    q, k, v, gates, xc = _mlstm_stage1(p1, w['ml_conv_w'], w['ml_conv_b'], w['ml_wq_t'], w['ml_wk_t'], w['ml_wv_t'],
                                       w['ml_wg'], w['ml_bg'], seq_len=L, tr=M if sample else min(256, L), H=H_m, DH=DH,
                                       conv_state=st['ml_conv'] if sample else None)
    u_blk = 2 * W_ml // (G * w['s5_gc'])
    if sample:
        yc, out['ml_C'], out['ml_n'], out['ml_m'] = _mlstm_sample(
            q, k, v, gates, xc, p1, w['ml_g_norm'], w['ml_skip'], st['ml_C'], st['ml_n'], st['ml_m'], H=H_m, DH=DH)
        out['ml_conv'] = jnp.concatenate([st['ml_conv'][:, 1:], xm_tail], axis=1)
        yd, s5r, s5i = _s5_mixer(p1, u_blk, w['s5_rows'], w['s5_bbd'], w['s5_cbd'], w['s5_D'], w['s5_w_glu'],
                                 w['s5_b_glu'], B=B, L=1, tb=B,
                                 state=(st['s5_re'].reshape(B, G * P), st['s5_im'].reshape(B, G * P)))
    else:
        yc, out['ml_C'], out['ml_n'], out['ml_m'] = _mlstm_prompt(
            q, k, v, gates, xc, p1, w['ml_g_norm'], w['ml_skip'], B=B, L=L, H=H_m, DH=DH, tb=tb)
        out['ml_conv'] = xm_tail
        yd, s5r, s5i = _s5_mixer(p1, u_blk, w['s5_rows'], w['s5_bbd'], w['s5_cbd'], w['s5_D'], w['s5_w_glu'],
                                 w['s5_b_glu'], B=B, L=L, tb=min(256, L))
    out['s5_re'] = s5r.reshape(B, G, P)
    out['s5_im'] = s5i.reshape(B, G, P)
    x = _out_proj(yc, yd, w['w_out1'], x, tm=to, tn=512)
    x, ffn1 = _conv_ffn(x, w['g_ffn'][1], w['ffn'][1], seq_len=L, tm=tf, prev=st['ffn_conv'] if sample else None,
                        final_g=w['g_final'])
    out['ffn_conv'] = jnp.stack([ffn0, ffn1], axis=0)
    return x.reshape(B, L, D), out


def kernel(x_prompt, x_sample, state_gla_S, state_rglru_h, state_rglru_conv, state_mlstm_C, state_mlstm_n, state_mlstm_m, state_mlstm_conv, state_s5_re, state_s5_im, state_ffn_conv, g_mix0, w_in0, gla_w_alpha2, gla_b_alpha, gla_g_norm, rg_conv_w, rg_conv_b, rg_w_r, rg_b_r, rg_w_i, rg_b_i, rg_lambda, w_out0, g_mix1, w_in1, ml_conv_w, ml_conv_b, ml_wq, ml_wk, ml_wv, ml_w_igate, ml_b_igate, ml_w_fgate, ml_b_fgate, ml_g_norm, ml_skip, s5_lam_re, s5_lam_im, s5_log_dt, s5_B_re, s5_B_im, s5_C_re, s5_C_im, s5_D, s5_w_glu, s5_b_glu, w_out1, g_ffn, ffn_w_up, ffn_conv_w, ffn_conv_b, ffn_w_down, g_final):
    _, H_g, DK, DV = state_gla_S.shape
    _, H_m, DH, _ = state_mlstm_C.shape
    G, P = s5_lam_re.shape
    rank = gla_w_alpha2.shape[0]
    n_main = 2 * H_g * DK + 2 * H_g * DV
    w_in0_main = [w_in0[:, :n_main].astype(bf16), w_in0[:, n_main + rank:].astype(bf16)]
    w_in0_alr = jnp.pad(w_in0[:, n_main:n_main + rank], ((0, 0), (0, LANES - rank))).astype(bf16)
    gla_w2p = jnp.pad(gla_w_alpha2, ((0, LANES - rank), (0, 0)))
    ml_tile = 256
    ml_wg = jnp.pad(jnp.concatenate([ml_w_igate, ml_w_fgate], axis=1), ((0, 0), (0, LANES - 2 * H_m))).astype(bf16)
    ml_bg = jnp.pad(jnp.concatenate([ml_b_igate, ml_b_fgate]), (0, LANES - 2 * H_m)).reshape(1, LANES)
    rows, bbd, cbd = _s5_layouts(s5_lam_re, s5_lam_im, s5_log_dt, s5_B_re, s5_B_im, s5_C_re, s5_C_im)
    w = dict(
        g_mix0=g_mix0, w_in0_main=w_in0_main, w_in0_alr=w_in0_alr, gla_w2p=gla_w2p, gla_b_alpha=gla_b_alpha,
        gla_g_norm=gla_g_norm, gla_dims=(H_g, DK, DV), rg_conv_w=rg_conv_w, rg_conv_b=rg_conv_b, rg_w_r=rg_w_r,
        rg_b_r=rg_b_r, rg_w_i=rg_w_i, rg_b_i=rg_b_i, rg_lambda=rg_lambda, w_out0=w_out0.astype(bf16),
        g_mix1=g_mix1, w_in1=w_in1.astype(bf16), ml_conv_w=ml_conv_w, ml_conv_b=ml_conv_b,
        ml_wq_t=_blockdiag_tiles(ml_wq, ml_tile).astype(bf16), ml_wk_t=_blockdiag_tiles(ml_wk, ml_tile).astype(bf16),
        ml_wv_t=_blockdiag_tiles(ml_wv, ml_tile).astype(bf16), ml_wg=ml_wg, ml_bg=ml_bg, ml_g_norm=ml_g_norm,
        ml_skip=ml_skip, ml_dims=(H_m, DH), s5_dims=(G, P), s5_gc=s5_B_re.shape[2], s5_rows=rows, s5_bbd=bbd,
        s5_cbd=cbd, s5_D=s5_D, s5_w_glu=s5_w_glu.astype(bf16), s5_b_glu=s5_b_glu, w_out1=w_out1.astype(bf16),
        g_ffn=g_ffn, g_final=g_final,
        ffn=_ffn_prepare(ffn_w_up, ffn_conv_w, ffn_conv_b, ffn_w_down))
    st_s = dict(gla_S=state_gla_S, rg_h=state_rglru_h, rg_conv=state_rglru_conv, ml_C=state_mlstm_C,
                ml_n=state_mlstm_n, ml_m=state_mlstm_m, ml_conv=state_mlstm_conv, s5_re=state_s5_re,
                s5_im=state_s5_im, ffn_conv=state_ffn_conv)
    y_p, np_ = _trunk(x_prompt, None, w, sample=False)
    y_s, ns_ = _trunk(x_sample, st_s, w, sample=True)
    names = ('gla_S', 'rg_h', 'rg_conv', 'ml_C', 'ml_n', 'ml_m', 'ml_conv', 's5_re', 's5_im', 'ffn_conv')
    outs = [y_p, y_s]
    for nme in names:
        outs += [np_[nme], ns_[nme]]
    return tuple(outs)
```

```python
import functools

import jax
import jax.numpy as jnp
from jax import lax
from jax.experimental import pallas as pl
from jax.experimental.pallas import tpu as pltpu

f32 = jnp.float32
bf16 = jnp.bfloat16

EPS = 1e-6
CHUNK = 64
GLA_TAU = 16.0
RG_C = 8.0
PAST_LEN = 16384
LANES = 128
SUBLANES = 8
HALO = 16
MIB = 1024 * 1024


def _params(sem, vmem_mib):
    return pltpu.CompilerParams(dimension_semantics=sem, vmem_limit_bytes=int(vmem_mib * MIB))


def _dot(a, b):
    return jnp.dot(a, b, preferred_element_type=f32)


def _dot_nt(a, b):
    return lax.dot_general(a, b, (((1,), (1,)), ((), ())), preferred_element_type=f32)


def _dot_tn(a, b):
    return lax.dot_general(a, b, (((0,), (0,)), ((), ())), preferred_element_type=f32)


def _rms(x, g):
    return x * lax.rsqrt(jnp.mean(x * x, axis=-1, keepdims=True) + EPS) * g


def _eye(n):
    return lax.broadcasted_iota(jnp.int32, (n, n), 0) == lax.broadcasted_iota(jnp.int32, (n, n), 1)


def _col_from_row(row):
    n = row.shape[1]
    return jnp.sum(jnp.where(_eye(n), jnp.broadcast_to(row, (n, n)), 0.0), axis=1, keepdims=True)


def _row_from_col(col):
    n = col.shape[0]
    return jnp.sum(jnp.where(_eye(n), jnp.broadcast_to(col, (n, n)), 0.0), axis=0, keepdims=True)


def _chunk_cumsum(x, chunk):
    pos = lax.broadcasted_iota(jnp.int32, (x.shape[0], 1), 0) % chunk
    step = 1
    while step < chunk:
        x = x + jnp.where(pos >= step, pltpu.roll(x, step, 0), 0.0)
        step *= 2
    return x


def _shifted(prev8, x, back):
    xx = jnp.concatenate([prev8, x], axis=0)
    n = x.shape[0]
    return xx[SUBLANES - back:SUBLANES - back + n]


def _norm_matmul(x, g, ws, *, tm, tn, w_side=None):
    M, D = x.shape
    nblk = [w.shape[1] // tn for w in ws]
    start = [sum(nblk[:s]) for s in range(len(ws) + 1)]
    N = tn * start[-1]
    rc = min(tm, 256)
    side = w_side is not None

    def body(*refs):
        x_ref, g_ref = refs[:2]
        w_refs = refs[2:2 + len(ws)]
        rest = refs[2 + len(ws):]
        if side:
            ws_ref, o_ref, os_ref, xn_ref = rest
        else:
            o_ref, xn_ref = rest
        j = pl.program_id(1)

        @pl.when(j == 0)
        def _():
            def chunk(r, c):
                rows = pl.ds(pl.multiple_of(r * rc, rc), rc)
                xn_ref[rows, :] = _rms(x_ref[rows, :], g_ref[...]).astype(bf16)
                return c
            lax.fori_loop(0, tm // rc, chunk, 0)
            if side:
                os_ref[...] = _dot(xn_ref[...], ws_ref[...])

        if len(ws) == 1:
            o_ref[...] = _dot(xn_ref[...], w_refs[0][...])
        else:
            for s, w_ref in enumerate(w_refs):
                @pl.when(jnp.logical_and(j >= start[s], j < start[s + 1]))
                def _(w_ref=w_ref):
                    o_ref[...] = _dot(xn_ref[...], w_ref[...])

    in_specs = [pl.BlockSpec((tm, D), lambda i, j: (i, 0)),
                pl.BlockSpec((1, D), lambda i, j: (0, 0))]
    in_specs += [pl.BlockSpec((D, tn), lambda i, j, s=s: (0, jnp.clip(j - start[s], 0, nblk[s] - 1)))
                 for s in range(len(ws))]
    out_specs = [pl.BlockSpec((tm, tn), lambda i, j: (i, j))]
    out_shape = [jax.ShapeDtypeStruct((M, N), f32)]
    args = [x, g.reshape(1, D)] + list(ws)
    if side:
        ns = w_side.shape[1]
        in_specs.append(pl.BlockSpec((D, ns), lambda i, j: (0, 0)))
        out_specs.append(pl.BlockSpec((tm, ns), lambda i, j: (i, 0)))
        out_shape.append(jax.ShapeDtypeStruct((M, ns), f32))
        args.append(w_side)
    outs = pl.pallas_call(
        body,
        grid=(M // tm, N // tn),
        in_specs=in_specs,
        out_specs=out_specs,
        out_shape=out_shape,
        scratch_shapes=[pltpu.VMEM((tm, D), bf16)],
        compiler_params=_params(("parallel", "arbitrary"), 56),
        name="norm_matmul",
    )(*args)
    return outs if side else outs[0]


def _out_proj(ya, yb, w, res, *, tm, tn):
    M, Ka = ya.shape
    Kb = yb.shape[1]
    N = w.shape[1]
    assert Ka == Kb and w.shape[0] == Ka + Kb

    def body(ya_ref, yb_ref, wa_ref, wb_ref, r_ref, o_ref):
        o_ref[...] = (r_ref[...] + _dot(ya_ref[...].astype(bf16), wa_ref[...])
                      + _dot(yb_ref[...].astype(bf16), wb_ref[...]))

    return pl.pallas_call(
        body,
        grid=(M // tm, N // tn),
        in_specs=[pl.BlockSpec((tm, Ka), lambda i, j: (i, 0)),
                  pl.BlockSpec((tm, Kb), lambda i, j: (i, 0)),
                  pl.BlockSpec((Ka, tn), lambda i, j: (0, j)),
                  pl.BlockSpec((Kb, tn), lambda i, j: (1, j)),
                  pl.BlockSpec((tm, tn), lambda i, j: (i, j))],
        out_specs=pl.BlockSpec((tm, tn), lambda i, j: (i, j)),
        out_shape=jax.ShapeDtypeStruct((M, N), f32),
        compiler_params=_params(("parallel", "arbitrary"), 48),
        name="out_proj",
    )(ya, yb, w, w, res)


FFN_STEP = 512


def _ffn_to_steps(a, F):
    nsteps = -(-F // FFN_STEP)
    lead = a.shape[:-1]
    nd = len(lead)
    gv = a.reshape(lead + (2, F))
    gv = jnp.pad(gv, [(0, 0)] * (nd + 1) + [(0, nsteps * FFN_STEP - F)])
    gv = gv.reshape(lead + (2, nsteps, FFN_STEP))
    gv = jnp.transpose(gv, (nd + 1,) + tuple(range(nd)) + (nd, nd + 2))
    return gv.reshape((nsteps,) + lead + (2 * FFN_STEP,))


def _ffn_prepare(w_up, conv_w, conv_b, w_down):
    NL, F, D = w_down.shape
    assert F % LANES == 0 and FFN_STEP % LANES == 0
    nt = F // LANES
    tp = FFN_STEP // LANES
    nsteps = -(-F // FFN_STEP)

    def body(*refs):
        g_in, v_in, d_in = refs[:tp], refs[tp:2 * tp], refs[2 * tp:3 * tp]
        wu_o, wd_o = refs[3 * tp:]
        for t in range(tp):
            keep = tp * pl.program_id(1) + t < nt
            cols = slice(t * LANES, (t + 1) * LANES)
            wu_o[:, cols] = jnp.where(keep, g_in[t][...], 0.0).astype(bf16)
            wu_o[:, FFN_STEP + t * LANES:FFN_STEP + (t + 1) * LANES] = jnp.where(keep, v_in[t][...], 0.0).astype(bf16)
            wd_o[cols, :] = jnp.where(keep, d_in[t][...], 0.0).astype(bf16)

    tile = lambda j, t: jnp.minimum(tp * j + t, nt - 1)
    in_specs = ([pl.BlockSpec((None, D, LANES), lambda l, j, t=t: (l, 0, tile(j, t))) for t in range(tp)]
                + [pl.BlockSpec((None, D, LANES), lambda l, j, t=t: (l, 0, nt + tile(j, t))) for t in range(tp)]
                + [pl.BlockSpec((None, LANES, D), lambda l, j, t=t: (l, tile(j, t), 0)) for t in range(tp)])
    wu, wd = pl.pallas_call(
        body,
        grid=(NL, nsteps),
        in_specs=in_specs,
        out_specs=[pl.BlockSpec((None, None, D, 2 * FFN_STEP), lambda l, j: (l, j, 0, 0)),
                   pl.BlockSpec((None, None, FFN_STEP, D), lambda l, j: (l, j, 0, 0))],
        out_shape=[jax.ShapeDtypeStruct((NL, nsteps, D, 2 * FFN_STEP), bf16),
                   jax.ShapeDtypeStruct((NL, nsteps, FFN_STEP, D), bf16)],
        compiler_params=_params(("parallel", "parallel"), 40),
        name="ffn_weight_layout",
    )(*([w_up] * (2 * tp) + [w_down] * tp))
    return [dict(wu=wu, wd=wd, layer=l, F=F, cw=_ffn_to_steps(conv_w[l], F),
                 cb=_ffn_to_steps(conv_b[l], F).reshape(nsteps, 1, 2 * FFN_STEP)) for l in range(NL)]


def _conv_ffn(x, g, fw, *, seq_len, tm, prev=None, final_g=None):
    M, D = x.shape
    F = fw['F']
    layer = fw['layer']
    nsteps = fw['wu'].shape[1]
    sample = prev is not None
    rc = min(tm, 256)
    tiles_per_seq = max(seq_len // tm, 1)
    W2 = 2 * FFN_STEP

    def body(*refs):
        it = iter(refs)
        x_ref = next(it)
        xh_ref = None if sample else next(it)
        prev_refs = [next(it) for _ in range(2 * FFN_STEP // LANES)] if sample else None
        g_ref = next(it)
        wu_ref, cw_ref, cb_ref, wd_ref = next(it), next(it), next(it), next(it)
        fg_ref = next(it) if final_g is not None else None
        o_ref, tail_g_ref, tail_v_ref = next(it), next(it), next(it)
        xn_ref, up_scr = next(it), next(it)
        i = pl.program_id(0)
        j = pl.program_id(1)

        @pl.when(j == 0)
        def _():
            if sample:
                xn_ref[0:HALO, :] = jnp.zeros((HALO, D), bf16)
            else:
                keep = (i % tiles_per_seq != 0).astype(f32)
                hist = _rms(xh_ref[...], g_ref[...]) * keep
                xn_ref[0:HALO, :] = jnp.concatenate([jnp.zeros_like(hist), hist], axis=0).astype(bf16)

            def chunk(r, c):
                rows = pl.ds(pl.multiple_of(r * rc, rc), rc)
                xr = x_ref[rows, :]
                o_ref[rows, :] = xr
                xn_ref[pl.ds(pl.multiple_of(HALO + r * rc, HALO), rc), :] = _rms(xr, g_ref[...]).astype(bf16)
                return c
            lax.fori_loop(0, tm // rc, chunk, 0)

        cw = cw_ref[0]
        cb = cb_ref[0]
        nchunk = tm // rc
        if not sample:
            up_scr[0:HALO, :] = _dot(xn_ref[0:HALO, :], wu_ref[...])
        for r in range(nchunk):
            lo = HALO + r * rc
            up_scr[lo:lo + rc, :] = _dot(xn_ref[lo:lo + rc, :], wu_ref[...])
        for r in range(nchunk):
            rows = pl.ds(r * rc, rc)
            lo = HALO + r * rc
            up = up_scr[lo:lo + rc, :]
            if sample:
                p0 = jnp.concatenate([p[rows, 0, :] for p in prev_refs], axis=1)
                p1 = jnp.concatenate([p[rows, 1, :] for p in prev_refs], axis=1)
                conv = cb + cw[0:1] * p0 + cw[1:2] * p1 + cw[2:3] * up
                tail_g_ref[rows, :] = up[:, :FFN_STEP]
                tail_v_ref[rows, :] = up[:, FFN_STEP:]
            else:
                conv = cb + cw[0:1] * up_scr[lo - 2:lo - 2 + rc, :] + cw[1:2] * up_scr[lo - 1:lo - 1 + rc, :] + cw[2:3] * up
                if r == nchunk - 1:
                    tail_g_ref[...] = up[rc - SUBLANES:, :FFN_STEP]
                    tail_v_ref[...] = up[rc - SUBLANES:, FFN_STEP:]
            h = jax.nn.gelu(conv[:, :FFN_STEP]) * conv[:, FFN_STEP:]
            o_ref[rows, :] += _dot(h.astype(bf16), wd_ref[...])

        if final_g is not None:
            @pl.when(j == nsteps - 1)
            def _():
                def chunk2(r, c):
                    rows = pl.ds(pl.multiple_of(r * rc, rc), rc)
                    o_ref[rows, :] = _rms(o_ref[rows, :], fg_ref[...])
                    return c
                lax.fori_loop(0, tm // rc, chunk2, 0)

    in_specs = [pl.BlockSpec((tm, D), lambda i, j: (i, 0))]
    args = [x]
    if sample:
        nt = F // LANES
        tp = FFN_STEP // LANES
        last = 2 * nt - 1
        for off in list(range(tp)) + [nt + t for t in range(tp)]:
            in_specs.append(pl.BlockSpec((None, tm, 2, LANES),
                                         lambda i, j, off=off: (layer, i, 0, jnp.minimum(tp * j + off, last))))
            args.append(prev)
    else:
        in_specs.append(pl.BlockSpec((SUBLANES, D), lambda i, j: (jnp.maximum(i * (tm // SUBLANES) - 1, 0), 0)))
        args.append(x)
    in_specs += [pl.BlockSpec((1, D), lambda i, j: (0, 0)),
                 pl.BlockSpec((None, None, D, W2), lambda i, j: (layer, j, 0, 0)),
                 pl.BlockSpec((1, 3, W2), lambda i, j: (j, 0, 0)),
                 pl.BlockSpec((1, 1, W2), lambda i, j: (j, 0, 0)),
                 pl.BlockSpec((None, None, FFN_STEP, D), lambda i, j: (layer, j, 0, 0))]
    args += [g.reshape(1, D), fw['wu'], fw['cw'], fw['cb'], fw['wd']]
    if final_g is not None:
        in_specs.append(pl.BlockSpec((1, D), lambda i, j: (0, 0)))
        args.append(final_g.reshape(1, D))
    FP = nsteps * FFN_STEP
    if sample:
        tail_shape = (M, FP)
        tail_spec = pl.BlockSpec((tm, FFN_STEP), lambda i, j: (i, j))
    else:
        tail_shape = (M // tm, SUBLANES, FP)
        tail_spec = pl.BlockSpec((None, SUBLANES, FFN_STEP), lambda i, j: (i, 0, j))
    out, tail_g, tail_v = pl.pallas_call(
        body,
        grid=(M // tm, nsteps),
        in_specs=in_specs,
        out_specs=[pl.BlockSpec((tm, D), lambda i, j: (i, 0)), tail_spec, tail_spec],
        out_shape=[jax.ShapeDtypeStruct((M, D), f32), jax.ShapeDtypeStruct(tail_shape, f32),
                   jax.ShapeDtypeStruct(tail_shape, f32)],
        scratch_shapes=[pltpu.VMEM((tm + HALO, D), bf16), pltpu.VMEM((tm + HALO, W2), f32)],
        compiler_params=_params(("parallel", "arbitrary"), 56),
        name="conv_ffn_sample" if sample else "conv_ffn",
    )(*args)
    if sample:
        up_rows = jnp.concatenate([tail_g[:, :F], tail_v[:, :F]], axis=1)
        new_buf = jnp.stack([prev[layer, :, 1, :], up_rows], axis=1)
    else:
        nseq = M // seq_len
        pick = lambda t: t.reshape(nseq, tiles_per_seq, SUBLANES, FP)[:, -1, SUBLANES - 2:, :F]
        new_buf = jnp.concatenate([pick(tail_g), pick(tail_v)], axis=-1)
    return out, new_buf


def _gla_prompt(p0, alr, w2p, b_alpha, g_norm, *, B, L, H, DK, DV, tb):
    NT = L // tb
    NC = tb // CHUNK
    C = CHUNK
    scale = DK ** -0.5
    qk_blocks = H
    v_off = 2 * H * DK // DV

    def body(q_ref, k_ref, v_ref, r_ref, a_ref, w2_ref, ba_ref, gn_ref, y_ref, s_out_ref, s_scr):
        t = pl.program_id(2)

        @pl.when(t == 0)
        def _():
            s_scr[...] = jnp.zeros_like(s_scr)

        z = _dot(a_ref[...].astype(bf16), w2_ref[...].astype(bf16)) + ba_ref[...]
        gl = jax.nn.log_sigmoid(z) * (1.0 / GLA_TAU)
        bc3 = _chunk_cumsum(gl, C).reshape(NC, C, DK)
        bl3 = bc3[:, C - 1:C, :]
        q3 = (q_ref[...] * scale).reshape(NC, C, DK)
        k3 = k_ref[...].reshape(NC, C, DK)
        v3 = v_ref[...].astype(bf16).reshape(NC, C, DV)
        qd3 = (q3 * jnp.exp(bc3)).astype(bf16)
        kd3 = (k3 * jnp.exp(-bc3)).astype(bf16)
        kdec3 = (k3 * jnp.exp(bl3 - bc3)).astype(bf16)
        causal = (lax.broadcasted_iota(jnp.int32, (1, C, C), 1) >= lax.broadcasted_iota(jnp.int32, (1, C, C), 2))
        att = jnp.where(causal, jnp.einsum('cik,cjk->cij', qd3, kd3, preferred_element_type=f32), 0.0)
        intra = jnp.einsum('cij,cjv->civ', att.astype(bf16), v3, preferred_element_type=f32)
        ds = jnp.einsum('cjk,cjv->ckv', kdec3, v3, preferred_element_type=f32)
        s = s_scr[...]
        s_in = []
        for c in range(NC):
            s_in.append(s)
            s = _col_from_row(jnp.exp(bl3[c])) * s + ds[c]
        s_scr[...] = s
        s_all = jnp.stack(s_in).astype(bf16)
        o = (intra + jnp.einsum('cik,ckv->civ', qd3, s_all, preferred_element_type=f32)).reshape(tb, DV)
        rr = r_ref[...]
        y_ref[...] = (_rms(o, gn_ref[...]) * (rr * jax.nn.sigmoid(rr))).astype(bf16)

        @pl.when(t == NT - 1)
        def _():
            s_out_ref[0, 0] = s
    return pl.pallas_call(
        body,
        grid=(B, H, NT),
        in_specs=[pl.BlockSpec((tb, DK), lambda b, h, t: (b * NT + t, h)),
                  pl.BlockSpec((tb, DK), lambda b, h, t: (b * NT + t, qk_blocks + h)),
                  pl.BlockSpec((tb, DV), lambda b, h, t: (b * NT + t, v_off + h)),
                  pl.BlockSpec((tb, DV), lambda b, h, t: (b * NT + t, v_off + H + h)),
                  pl.BlockSpec((tb, LANES), lambda b, h, t: (b * NT + t, 0)),
                  pl.BlockSpec((LANES, DK), lambda b, h, t: (0, h)),
                  pl.BlockSpec((1, DK), lambda b, h, t: (0, h)),
                  pl.BlockSpec((1, DV), lambda b, h, t: (0, h))],
        out_specs=[pl.BlockSpec((tb, DV), lambda b, h, t: (b * NT + t, h)),
                   pl.BlockSpec((1, 1, DK, DV), lambda b, h, t: (b, h, 0, 0))],
        out_shape=[jax.ShapeDtypeStruct((B * L, H * DV), bf16),
                   jax.ShapeDtypeStruct((B, H, DK, DV), f32)],
        scratch_shapes=[pltpu.VMEM((DK, DV), f32)],
        compiler_params=_params(("parallel", "parallel", "arbitrary"), 32),
        name="gla_prompt",
    )(p0, p0, p0, p0, alr, w2p, b_alpha.reshape(1, -1), g_norm.reshape(1, -1))


def _gla_sample(p0, alr, w2p, b_alpha, g_norm, s0, *, H, DK, DV):
    Bs = p0.shape[0]
    scale = DK ** -0.5
    qkw = H * DK
    vw = H * DV
    assert vw % qkw == 0
    SB = SUBLANES

    def body(q_ref, k_ref, v_ref, r_ref, a_ref, w2_ref, ba_ref, gn_ref, s_ref, y_ref, so_ref, gl_scr):
        z = _dot(a_ref[...].astype(bf16), w2_ref[...].astype(bf16)) + ba_ref[...]
        gl_scr[...] = jax.nn.log_sigmoid(z) * (1.0 / GLA_TAU)

        for h in range(H):
            ks = slice(h * DK, (h + 1) * DK)
            vs = slice(h * DV, (h + 1) * DV)
            a_t = jnp.exp(gl_scr[:, ks]).T
            k_t = k_ref[:, ks].T
            q_t = (q_ref[:, ks] * scale).T
            v = v_ref[:, vs]
            outs = []
            for s in range(SB):
                sn = a_t[:, s:s + 1] * s_ref[s, h] + k_t[:, s:s + 1] * v[s:s + 1, :]
                so_ref[s, h] = sn
                outs.append(jnp.sum(q_t[:, s:s + 1] * sn, axis=0, keepdims=True))
            o = jnp.concatenate(outs, axis=0)
            rr = r_ref[:, vs]
            y_ref[:, vs] = _rms(o, gn_ref[:, vs]) * (rr * jax.nn.sigmoid(rr))

    v_blk = 2 * qkw // vw
    return pl.pallas_call(
        body,
        grid=(Bs // SB,),
        in_specs=[pl.BlockSpec((SB, qkw), lambda b: (b, 0)),
                  pl.BlockSpec((SB, qkw), lambda b: (b, 1)),
                  pl.BlockSpec((SB, vw), lambda b: (b, v_blk)),
                  pl.BlockSpec((SB, vw), lambda b: (b, v_blk + 1)),
                  pl.BlockSpec((SB, LANES), lambda b: (b, 0)),
                  pl.BlockSpec((LANES, qkw), lambda b: (0, 0)),
                  pl.BlockSpec((1, qkw), lambda b: (0, 0)),
                  pl.BlockSpec((1, vw), lambda b: (0, 0)),
                  pl.BlockSpec((SB, H, DK, DV), lambda b: (b, 0, 0, 0))],
        out_specs=[pl.BlockSpec((SB, vw), lambda b: (b, 0)),
                   pl.BlockSpec((SB, H, DK, DV), lambda b: (b, 0, 0, 0))],
        out_shape=[jax.ShapeDtypeStruct((Bs, vw), f32),
                   jax.ShapeDtypeStruct((Bs, H, DK, DV), f32)],
        scratch_shapes=[pltpu.VMEM((SB, qkw), f32)],
        compiler_params=_params(("parallel",), 32),
        name="gla_sample",
    )(p0, p0, p0, p0, alr, w2p, b_alpha.reshape(1, -1), g_norm.reshape(1, -1), s0)


def _rg_gates(xc, wr, br, wi, bi, sp):
    xb = xc.astype(bf16)
    r = jax.nn.sigmoid(_dot(xb, wr) + br)
    i = jax.nn.sigmoid(_dot(xb, wi) + bi)
    log_a = -RG_C * r * sp
    a = jnp.exp(log_a)
    mult = jnp.sqrt(1.0 - a * a)
    return a, mult, i


def _rglru_prompt(p0, conv_w, conv_b, w_r, b_r, w_i, b_i, lam, *, B, L, x_blk, g_blk):
    M = B * L
    NB, BS, _ = w_r.shape
    assert BS == LANES
    rc = min(256, L)

    def body(x_ref, gg_ref, cw_ref, cb_ref, wr_ref, br_ref, wi_ref, bi_ref, lam_ref, y_ref, hl_ref, a_scr, b_scr, xs_scr):
        wr = wr_ref[...].astype(bf16)
        wi = wi_ref[...].astype(bf16)
        sp = jax.nn.softplus(-lam_ref[...])
        cw = cw_ref[...]

        def chunk(c, carry):
            r0 = pl.multiple_of(c * rc, rc)
            rows = pl.ds(r0, rc)
            x = x_ref[rows, :]
            start = (r0 % L) == 0
            prev = x_ref[pl.ds(pl.multiple_of(jnp.maximum(r0 - SUBLANES, 0), SUBLANES), SUBLANES), :]
            xs_scr[0:SUBLANES, :] = jnp.where(start, 0.0, prev)
            xs_scr[SUBLANES:, :] = x
            xc = (cb_ref[...] + cw[0:1] * xs_scr[SUBLANES - 3:SUBLANES - 3 + rc, :]
                  + cw[1:2] * xs_scr[SUBLANES - 2:SUBLANES - 2 + rc, :]
                  + cw[2:3] * xs_scr[SUBLANES - 1:SUBLANES - 1 + rc, :] + cw[3:4] * x)
            a, mult, ig = _rg_gates(xc, wr, br_ref[...], wi, bi_ref[...], sp)
            first = jnp.logical_and(start, lax.broadcasted_iota(jnp.int32, (rc, 1), 0) == 0)
            mult = jnp.where(first, 1.0, mult)
            a_scr[rows, :] = a
            b_scr[rows, :] = mult * (ig * xc)
            return carry
        lax.fori_loop(0, M // rc, chunk, 0)

        def step(t, hs):
            new = []
            for b in range(B):
                row = pl.ds(b * L + t, 1)
                h = a_scr[row, :] * hs[b] + b_scr[row, :]
                b_scr[row, :] = h
                new.append(h)
            return tuple(new)
        hs = lax.fori_loop(0, L, step, tuple(jnp.zeros((1, LANES), f32) for _ in range(B)), unroll=8)
        hl_ref[...] = jnp.concatenate(hs, axis=0)

        def outc(c, carry):
            rows = pl.ds(pl.multiple_of(c * rc, rc), rc)
            y_ref[rows, :] = (b_scr[rows, :] * jax.nn.gelu(gg_ref[rows, :])).astype(bf16)
            return carry
        lax.fori_loop(0, M // rc, outc, 0)

    W = NB * BS
    return pl.pallas_call(
        body,
        grid=(NB,),
        in_specs=[pl.BlockSpec((M, LANES), lambda n: (0, x_blk + n)),
                  pl.BlockSpec((M, LANES), lambda n: (0, g_blk + n)),
                  pl.BlockSpec((4, LANES), lambda n: (0, n)),
                  pl.BlockSpec((1, LANES), lambda n: (0, n)),
                  pl.BlockSpec((None, BS, BS), lambda n: (n, 0, 0)),
                  pl.BlockSpec((1, LANES), lambda n: (0, n)),
                  pl.BlockSpec((None, BS, BS), lambda n: (n, 0, 0)),
                  pl.BlockSpec((1, LANES), lambda n: (0, n)),
                  pl.BlockSpec((1, LANES), lambda n: (0, n))],
        out_specs=[pl.BlockSpec((M, LANES), lambda n: (0, n)),
                   pl.BlockSpec((B, LANES), lambda n: (0, n))],
        out_shape=[jax.ShapeDtypeStruct((M, W), bf16), jax.ShapeDtypeStruct((B, W), f32)],
        scratch_shapes=[pltpu.VMEM((M, LANES), f32), pltpu.VMEM((M, LANES), f32), pltpu.VMEM((rc + SUBLANES, LANES), f32)],
        compiler_params=_params(("parallel",), 48),
        name="rglru_prompt",
    )(p0, p0, conv_w, conv_b.reshape(1, W), w_r, b_r.reshape(1, W), w_i, b_i.reshape(1, W), lam.reshape(1, W))


def _rglru_sample(p0, conv_state, h0, conv_w, conv_b, w_r, b_r, w_i, b_i, lam, *, x_blk, g_blk):
    Bs = p0.shape[0]
    NB, BS, _ = w_r.shape
    W = NB * BS
    s0, s1, s2 = conv_state[:, 0], conv_state[:, 1], conv_state[:, 2]

    def body(x_ref, gg_ref, s0_ref, s1_ref, s2_ref, h0_ref, cw_ref, cb_ref, wr_ref, br_ref, wi_ref, bi_ref, lam_ref,
             y_ref, h_ref):
        cw = cw_ref[...]
        x = x_ref[...]
        xc = cb_ref[...] + cw[0:1] * s0_ref[...] + cw[1:2] * s1_ref[...] + cw[2:3] * s2_ref[...] + cw[3:4] * x
        sp = jax.nn.softplus(-lam_ref[...])
        a, mult, ig = _rg_gates(xc, wr_ref[...].astype(bf16), br_ref[...], wi_ref[...].astype(bf16), bi_ref[...], sp)
        if PAST_LEN == 0:
            mult = jnp.ones_like(mult)
        h = a * h0_ref[...] + mult * (ig * xc)
        h_ref[...] = h
        y_ref[...] = (h * jax.nn.gelu(gg_ref[...])).astype(bf16)

    blk = lambda n: (0, n)
    vec = pl.BlockSpec((1, LANES), blk)
    mat = pl.BlockSpec((Bs, LANES), blk)
    y, h = pl.pallas_call(
        body,
        grid=(NB,),
        in_specs=[pl.BlockSpec((Bs, LANES), lambda n: (0, x_blk + n)),
                  pl.BlockSpec((Bs, LANES), lambda n: (0, g_blk + n)),
                  mat, mat, mat, mat,
                  pl.BlockSpec((4, LANES), blk), vec,
                  pl.BlockSpec((None, BS, BS), lambda n: (n, 0, 0)), vec,
                  pl.BlockSpec((None, BS, BS), lambda n: (n, 0, 0)), vec, vec],
        out_specs=[mat, mat],
        out_shape=[jax.ShapeDtypeStruct((Bs, W), bf16), jax.ShapeDtypeStruct((Bs, W), f32)],
        compiler_params=_params(("parallel",), 32),
        name="rglru_sample",
    )(p0, p0, s0, s1, s2, h0, conv_w, conv_b.reshape(1, W), w_r, b_r.reshape(1, W), w_i, b_i.reshape(1, W),
      lam.reshape(1, W))
    return y, h


def _blockdiag_tiles(w, tile):
    nblk, bs, _ = w.shape
    per = tile // bs
    nt = nblk // per
    rows = jnp.tile(w.reshape(nt, tile, bs), (1, 1, per))
    on_diag = (jnp.arange(tile)[:, None] // bs) == (jnp.arange(tile)[None, :] // bs)
    return jnp.where(on_diag[None], rows, 0.0)


def _mlstm_stage1(p1, conv_w, conv_b, wq_t, wk_t, wv_t, wg, bg, *, seq_len, tr, H, DH, conv_state=None):
    M = p1.shape[0]
    W = H * DH
    NTL, TL, _ = wq_t.shape
    sample = conv_state is not None
    tiles_per_seq = max(seq_len // tr, 1)
    kscale = DH ** -0.5

    def body(*refs):
        it = iter(refs)
        x_ref = next(it)
        if sample:
            s0_ref, s1_ref, s2_ref = next(it), next(it), next(it)
        else:
            xh_ref = next(it)
        cw_ref, cb_ref, wq_ref, wk_ref, wv_ref, wg_ref, bg_ref = (next(it) for _ in range(7))
        q_ref, k_ref, v_ref, g_ref, xc_ref = (next(it) for _ in range(5))
        cw = cw_ref[...]
        x = x_ref[...]
        if sample:
            conv = cb_ref[...] + cw[0:1] * s0_ref[...] + cw[1:2] * s1_ref[...] + cw[2:3] * s2_ref[...] + cw[3:4] * x
        else:
            keep = (pl.program_id(0) % tiles_per_seq != 0).astype(f32)
            prev = xh_ref[...] * keep
            conv = (cb_ref[...] + cw[0:1] * _shifted(prev, x, 3) + cw[1:2] * _shifted(prev, x, 2)
                    + cw[2:3] * _shifted(prev, x, 1) + cw[3:4] * x)
        xc = conv * jax.nn.sigmoid(conv)
        xc_ref[...] = xc
        xcb = xc.astype(bf16)
        xb = x.astype(bf16)
        qs, ks, vs = [], [], []
        for t in range(NTL):
            cs = slice(t * TL, (t + 1) * TL)
            qs.append(_dot(xcb[:, cs], wq_ref[t]))
            ks.append(_dot(xcb[:, cs], wk_ref[t]) * kscale)
            vs.append(_dot(xb[:, cs], wv_ref[t]))
        q = jnp.concatenate(qs, axis=1)
        k = jnp.concatenate(ks, axis=1)
        v = jnp.concatenate(vs, axis=1)
        q_ref[...] = q.astype(q_ref.dtype)
        k_ref[...] = k.astype(k_ref.dtype)
        v_ref[...] = v.astype(v_ref.dtype)
        gt = (_dot(q.astype(bf16), wg_ref[0:W, :]) + _dot(k.astype(bf16), wg_ref[W:2 * W, :])
              + _dot(v.astype(bf16), wg_ref[2 * W:3 * W, :]) + bg_ref[...])
        lane = lax.broadcasted_iota(jnp.int32, gt.shape, 1)
        g_ref[...] = jnp.where(jnp.logical_and(lane >= H, lane < 2 * H), jax.nn.log_sigmoid(gt), gt)

    row = lambda i: (i, 0)
    const2 = lambda i: (0, 0)
    const3 = lambda i: (0, 0, 0)
    in_specs = [pl.BlockSpec((tr, W), row)]
    args = [p1]
    if sample:
        in_specs += [pl.BlockSpec((tr, W), row)] * 3
        args += [conv_state[:, 0], conv_state[:, 1], conv_state[:, 2]]
    else:
        in_specs.append(pl.BlockSpec((SUBLANES, W), lambda i: (jnp.maximum(i * (tr // SUBLANES) - 1, 0), 0)))
        args.append(p1)
    in_specs += [pl.BlockSpec((4, W), const2), pl.BlockSpec((1, W), const2),
                 pl.BlockSpec((NTL, TL, TL), const3), pl.BlockSpec((NTL, TL, TL), const3),
                 pl.BlockSpec((NTL, TL, TL), const3),
                 pl.BlockSpec((3 * W, LANES), const2), pl.BlockSpec((1, LANES), const2)]
    args += [conv_w, conv_b.reshape(1, W), wq_t, wk_t, wv_t, wg, bg]
    return pl.pallas_call(
        body,
        grid=(M // tr,),
        in_specs=in_specs,
        out_specs=[pl.BlockSpec((tr, W), row)] * 3 + [pl.BlockSpec((tr, LANES), row), pl.BlockSpec((tr, W), row)],
        out_shape=[jax.ShapeDtypeStruct((M, W), f32 if sample else bf16)] * 3
        + [jax.ShapeDtypeStruct((M, LANES), f32), jax.ShapeDtypeStruct((M, W), f32)],
        compiler_params=_params(("parallel",), 48),
        name="mlstm_stage1_sample" if sample else "mlstm_stage1",
    )(*args)


def _mlstm_prompt(q, k, v, gates, xc, p1, g_norm, skip, *, B, L, H, DH, tb):
    NT = L // tb
    NC = tb // CHUNK
    C = CHUNK

    def body(q_ref, k_ref, v_ref, g_ref, xc_ref, om_ref, gn_ref, sk_ref,
             y_ref, c_out, n_out, m_out, c_scr, n_scr, m_scr):
        hh = pl.program_id(1)
        t = pl.program_id(2)

        @pl.when(t == 0)
        def _():
            c_scr[...] = jnp.zeros_like(c_scr)
            n_scr[...] = jnp.zeros_like(n_scr)
            m_scr[...] = jnp.zeros_like(m_scr)

        gts = g_ref[...]
        lane = lax.broadcasted_iota(jnp.int32, (tb, LANES), 1)
        i_col = jnp.sum(jnp.where(lane == hh, gts, 0.0), axis=1, keepdims=True)
        b_col = jnp.sum(jnp.where(lane == H + hh, _chunk_cumsum(gts, C), 0.0), axis=1, keepdims=True)
        b3 = b_col.reshape(NC, C, 1)
        i3 = i_col.reshape(NC, C, 1)
        ii = lax.broadcasted_iota(jnp.int32, (1, C, C), 1)
        jj = lax.broadcasted_iota(jnp.int32, (1, C, C), 2)
        eye = ii == jj
        causal = ii >= jj
        as_row = lambda col3: jnp.sum(jnp.where(eye, jnp.broadcast_to(col3, (NC, C, C)), 0.0), axis=1, keepdims=True)
        dmat = jnp.where(causal, b3 - as_row(b3) + as_row(i3), -jnp.inf)
        rmax = jnp.max(dmat, axis=2, keepdims=True)
        b_last = b3[:, C - 1:C, :]
        m_prev = m_scr[...]
        m_in = []
        for c in range(NC):
            m_in.append(m_prev)
            m_prev = jnp.maximum(b_last[c] + m_prev, rmax[c][C - 1:C, :])
        m_scr[...] = m_prev
        inter = b3 + jnp.stack(m_in)
        m_col = jnp.maximum(inter, rmax)
        g_col = jnp.exp(inter - m_col)
        q3 = q_ref[...].reshape(NC, C, DH)
        k3 = k_ref[...].reshape(NC, C, DH)
        v3 = v_ref[...].reshape(NC, C, DH)
        s = jnp.einsum('cid,cjd->cij', q3, k3, preferred_element_type=f32) * jnp.exp(dmat - m_col)
        num = jnp.einsum('cij,cje->cie', s.astype(bf16), v3, preferred_element_type=f32)
        den = jnp.sum(s, axis=2, keepdims=True)
        m_new = m_col[:, C - 1:C, :]
        wk = jnp.exp(b_last - b3 + i3 - m_new)
        gc = jnp.exp(inter[:, C - 1:C, :] - m_new)
        kw = k3.astype(f32) * wk
        dc = jnp.einsum('cse,csd->ced', v3, kw.astype(bf16), preferred_element_type=f32)
        dn = jnp.sum(kw, axis=1, keepdims=True)
        cs = c_scr[...]
        ns = n_scr[...]
        c_in, n_in = [], []
        for c in range(NC):
            c_in.append(cs)
            n_in.append(ns)
            cs = gc[c] * cs + dc[c]
            ns = gc[c] * ns + dn[c]
        c_scr[...] = cs
        n_scr[...] = ns
        c_all = jnp.stack(c_in).astype(bf16)
        n_all = jnp.stack(n_in)
        num = num + g_col * jnp.einsum('cid,ced->cie', q3, c_all, preferred_element_type=f32)
        den = den + g_col * jnp.sum(q3.astype(f32) * n_all, axis=2, keepdims=True)
        hm = (num / jnp.maximum(jnp.abs(den), jnp.exp(-m_col))).reshape(tb, DH)
        y = (_rms(hm, gn_ref[...]) + sk_ref[...] * xc_ref[...]) * jax.nn.sigmoid(om_ref[...])
        y_ref[...] = y.astype(bf16)

        @pl.when(t == NT - 1)
        def _():
            c_out[0, 0] = cs
            n_out[0, 0] = ns
            m_out[0, 0] = m_prev

    blk = lambda b, h, t: (b * NT + t, h)
    W = H * DH
    y, c_new, n_new, m_new = pl.pallas_call(
        body,
        grid=(B, H, NT),
        in_specs=[pl.BlockSpec((tb, DH), blk), pl.BlockSpec((tb, DH), blk), pl.BlockSpec((tb, DH), blk),
                  pl.BlockSpec((tb, LANES), lambda b, h, t: (b * NT + t, 0)),
                  pl.BlockSpec((tb, DH), blk),
                  pl.BlockSpec((tb, DH), lambda b, h, t: (b * NT + t, H + h)),
                  pl.BlockSpec((1, DH), lambda b, h, t: (0, h)),
                  pl.BlockSpec((1, DH), lambda b, h, t: (0, h))],
        out_specs=[pl.BlockSpec((tb, DH), blk),
                   pl.BlockSpec((1, 1, DH, DH), lambda b, h, t: (b, h, 0, 0)),
                   pl.BlockSpec((1, 1, 1, DH), lambda b, h, t: (b, h, 0, 0)),
                   pl.BlockSpec((1, 1, 1, 1), lambda b, h, t: (b, h, 0, 0))],
        out_shape=[jax.ShapeDtypeStruct((B * L, W), bf16),
                   jax.ShapeDtypeStruct((B, H, DH, DH), f32),
                   jax.ShapeDtypeStruct((B, H, 1, DH), f32),
                   jax.ShapeDtypeStruct((B, H, 1, 1), f32)],
        scratch_shapes=[pltpu.VMEM((DH, DH), f32), pltpu.VMEM((1, DH), f32), pltpu.VMEM((1, 1), f32)],
        compiler_params=_params(("parallel", "parallel", "arbitrary"), 32),
        name="mlstm_prompt",
    )(q, k, v, gates, xc, p1, g_norm.reshape(1, W), skip.reshape(1, W))
    return y, c_new, n_new.reshape(B, H, DH), m_new.reshape(B, H)


def _mlstm_sample(q, k, v, gates, xc, p1, g_norm, skip, c0, n0, m0, *, H, DH):
    Bs = q.shape[0]
    W = H * DH
    SB = SUBLANES

    def body(q_ref, k_ref, v_ref, g_ref, xc_ref, om_ref, gn_ref, sk_ref, c_ref, n_ref, m_ref,
             y_ref, c_out, n_out, m_out):
        gts = g_ref[...]
        for h in range(H):
            cs = slice(h * DH, (h + 1) * DH)
            ig = gts[:, h:h + 1]
            fg = gts[:, H + h:H + h + 1]
            inter = fg + m_ref[:, h:h + 1]
            m = jnp.maximum(inter, ig)
            g = jnp.exp(inter - m)
            m_out[:, h:h + 1] = m
            q = q_ref[:, cs]
            kw = k_ref[:, cs] * jnp.exp(ig - m)
            nn = g * n_ref[:, cs] + kw
            n_out[:, cs] = nn
            den = jnp.sum(nn * q, axis=1, keepdims=True)
            v_t = v_ref[:, cs].T
            qb = q.astype(bf16)
            nums = []
            for s in range(SB):
                cn = g[s:s + 1, :] * c_ref[s, h] + v_t[:, s:s + 1] * kw[s:s + 1, :]
                c_out[s, h] = cn
                nums.append(_dot_nt(qb, cn.astype(bf16))[s:s + 1, :])
            num = jnp.concatenate(nums, axis=0)
            hm = num / jnp.maximum(jnp.abs(den), jnp.exp(-m))
            y_ref[:, cs] = ((_rms(hm, gn_ref[:, cs]) + sk_ref[:, cs] * xc_ref[:, cs])
                            * jax.nn.sigmoid(om_ref[:, cs]))

    per = lambda b: (b, 0)
    const2 = lambda b: (0, 0)
    y, c_new, n_new, m_new = pl.pallas_call(
        body,
        grid=(Bs // SB,),
        in_specs=[pl.BlockSpec((SB, W), per), pl.BlockSpec((SB, W), per), pl.BlockSpec((SB, W), per),
                  pl.BlockSpec((SB, LANES), per), pl.BlockSpec((SB, W), per),
                  pl.BlockSpec((SB, W), lambda b: (b, 1)),
                  pl.BlockSpec((1, W), const2), pl.BlockSpec((1, W), const2),
                  pl.BlockSpec((SB, H, DH, DH), lambda b: (b, 0, 0, 0)),
                  pl.BlockSpec((SB, W), per), pl.BlockSpec((SB, H), per)],
        out_specs=[pl.BlockSpec((SB, W), per),
                   pl.BlockSpec((SB, H, DH, DH), lambda b: (b, 0, 0, 0)),
                   pl.BlockSpec((SB, W), per), pl.BlockSpec((SB, H), per)],
        out_shape=[jax.ShapeDtypeStruct((Bs, W), f32), jax.ShapeDtypeStruct((Bs, H, DH, DH), f32),
                   jax.ShapeDtypeStruct((Bs, W), f32), jax.ShapeDtypeStruct((Bs, H), f32)],
        compiler_params=_params(("parallel",), 48),
        name="mlstm_sample",
    )(q, k, v, gates, xc, p1, g_norm.reshape(1, W), skip.reshape(1, W), c0, n0.reshape(Bs, W), m0)
    return y, c_new, n_new.reshape(Bs, H, DH), m_new


S5_TILES = 8


def _s5_layouts(lam_re, lam_im, log_dt, b_re, b_im, c_re, c_im):
    G, P = lam_re.shape
    GC = b_re.shape[2]
    T = S5_TILES
    gpt = G // T
    ns = G * P
    flat = lambda a: a.reshape(ns)
    ldt = jnp.broadcast_to(log_dt[:, None], (G, P))
    rows = [flat(a).reshape(T, 1, ns // T) for a in (lam_re, lam_im, ldt)]
    eye = jnp.eye(gpt, dtype=f32)
    bbd = [jnp.einsum('jgpc,gh->jgchp', a.reshape(T, gpt, P, GC), eye).reshape(T, gpt * GC, gpt * P) for a in (b_re, b_im)]
    cbd = [jnp.einsum('jgcp,gh->jgphc', a.reshape(T, gpt, GC, P), eye).reshape(T, gpt * P, gpt * GC) for a in (c_re, c_im)]
    return rows, bbd, cbd


def _s5_discretise(lre, lim, ldt):
    dt = jnp.exp(ldt)
    mag = jnp.exp(dt * lre)
    ar = mag * jnp.cos(dt * lim)
    ai = mag * jnp.sin(dt * lim)
    den = lre * lre + lim * lim
    cr = ((ar - 1.0) * lre + ai * lim) / den
    ci = (ai * lre - (ar - 1.0) * lim) / den
    return ar, ai, cr, ci


def _s5_mixer(p1, u_blk, rows, bbd, cbd, d_skip, w_glu, b_glu, *, B, L, tb, state=None):
    T = S5_TILES
    lre_r, lim_r, ldt_r = rows
    SW = lre_r.shape[2]
    CW = bbd[0].shape[1]
    W = T * CW
    NS = T * SW
    KT = SW // LANES
    sample = state is not None
    NT = 1 if sample else L // tb
    M = B * L

    def body(*refs):
        it = iter(refs)
        u_ref = next(it)
        if sample:
            x0r_ref, x0i_ref = next(it), next(it)
        lre_ref, lim_ref, ldt_ref = next(it), next(it), next(it)
        bre_ref, bim_ref, cre_ref, cim_ref = next(it), next(it), next(it), next(it)
        d_ref, wg_ref, bgl_ref = next(it), next(it), next(it)
        y_ref, xr_out, xi_out = next(it), next(it), next(it)
        bbr, bbi, cpair, ar_scr, ai_scr = (next(it) for _ in range(5))
        if not sample:
            sre, sim, xr_c, xi_c, yacc = (next(it) for _ in range(5))
        first = jnp.logical_and(pl.program_id(0) == 0, pl.program_id(1) == 0)

        @pl.when(first)
        def _():
            for j in range(T):
                ar, ai, cr, ci = _s5_discretise(lre_ref[j], lim_ref[j], ldt_ref[j])
                ar_scr[j] = ar
                ai_scr[j] = ai
                br = bre_ref[j]
                bi = bim_ref[j]
                bbr[j] = (cr * br - ci * bi).astype(bf16)
                bbi[j] = (cr * bi + ci * br).astype(bf16)
            for jp in range(T // 2):
                cpair[jp] = jnp.zeros((4 * SW, 2 * CW), bf16)
                for half in range(2):
                    j = 2 * jp + half
                    r0 = 2 * half * SW
                    cols = slice(half * CW, (half + 1) * CW)
                    cpair[jp, r0:r0 + SW, cols] = cre_ref[j].astype(bf16)
                    cpair[jp, r0 + SW:r0 + 2 * SW, cols] = (-cim_ref[j]).astype(bf16)

        u = u_ref[...]
        ub = u.astype(bf16)
        ys = []
        if sample:
            for j in range(T):
                cs = slice(j * SW, (j + 1) * SW)
                uj = ub[:, j * CW:(j + 1) * CW]
                ar = ar_scr[j]
                ai = ai_scr[j]
                x0r = x0r_ref[:, cs]
                x0i = x0i_ref[:, cs]
                xr = ar * x0r - ai * x0i + _dot(uj, bbr[j])
                xi = ar * x0i + ai * x0r + _dot(uj, bbi[j])
                xr_out[:, cs] = xr
                xi_out[:, cs] = xi
                ys += [xr.astype(bf16), xi.astype(bf16)]
            y = jnp.concatenate([_dot(jnp.concatenate(ys[4 * jp:4 * jp + 4], axis=1), cpair[jp]) for jp in range(T // 2)],
                                axis=1)
        else:
            t = pl.program_id(1)

            @pl.when(t == 0)
            def _():
                xr_c[...] = jnp.zeros_like(xr_c)
                xi_c[...] = jnp.zeros_like(xi_c)

            for j in range(T):
                uj = ub[:, j * CW:(j + 1) * CW]
                r = _dot(uj, bbr[j])
                im = _dot(uj, bbi[j])
                for kk in range(KT):
                    sre[kk, pl.ds(j, tb, stride=T), :] = r[:, kk * LANES:(kk + 1) * LANES]
                    sim[kk, pl.ds(j, tb, stride=T), :] = im[:, kk * LANES:(kk + 1) * LANES]
            a_r = [jnp.concatenate([ar_scr[j][:, kk * LANES:(kk + 1) * LANES] for j in range(T)], axis=0) for kk in range(KT)]
            a_i = [jnp.concatenate([ai_scr[j][:, kk * LANES:(kk + 1) * LANES] for j in range(T)], axis=0) for kk in range(KT)]

            def step(s, carry):
                xr, xi = carry
                row = pl.ds(pl.multiple_of(s * T, T), T)
                nr, ni = [], []
                for kk in range(KT):
                    r_ = a_r[kk] * xr[kk] - a_i[kk] * xi[kk] + sre[kk, row, :]
                    i_ = a_r[kk] * xi[kk] + a_i[kk] * xr[kk] + sim[kk, row, :]
                    sre[kk, row, :] = r_
                    sim[kk, row, :] = i_
                    nr.append(r_)
                    ni.append(i_)
                return tuple(nr), tuple(ni)
            xr0 = tuple(xr_c[kk] for kk in range(KT))
            xi0 = tuple(xi_c[kk] for kk in range(KT))
            xr, xi = lax.fori_loop(0, tb, step, (xr0, xi0), unroll=4)
            for kk in range(KT):
                xr_c[kk] = xr[kk]
                xi_c[kk] = xi[kk]

            @pl.when(t == NT - 1)
            def _():
                for kk in range(KT):
                    xr_out[kk] = xr[kk]
                    xi_out[kk] = xi[kk]

            for jp in range(T // 2):
                parts = []
                for j in (2 * jp, 2 * jp + 1):
                    parts += [sre[kk, pl.ds(j, tb, stride=T), :] for kk in range(KT)]
                    parts += [sim[kk, pl.ds(j, tb, stride=T), :] for kk in range(KT)]
                xp = jnp.concatenate(parts, axis=1).astype(bf16)
                yacc[:, 2 * jp * CW:(2 * jp + 2) * CW] = _dot(xp, cpair[jp])
            y = yacc[...]
        ysk = jax.nn.gelu(y + d_ref[...] * u)
        z = _dot(ysk.astype(bf16), wg_ref[...]) + bgl_ref[...]
        y_ref[...] = (ysk * jax.nn.sigmoid(z)).astype(bf16)

    c3 = lambda b, t: (0, 0, 0)
    c2 = lambda b, t: (0, 0)
    in_specs = [pl.BlockSpec((tb, W), lambda b, t: (b * NT + t, u_blk))]
    args = [p1]
    if sample:
        in_specs += [pl.BlockSpec((tb, NS), lambda b, t: (b, 0))] * 2
        args += [state[0], state[1]]
    in_specs += [pl.BlockSpec((T, 1, SW), c3)] * 3
    in_specs += [pl.BlockSpec((T, CW, SW), c3)] * 2 + [pl.BlockSpec((T, SW, CW), c3)] * 2
    in_specs += [pl.BlockSpec((1, W), c2), pl.BlockSpec((W, W), c2), pl.BlockSpec((1, W), c2)]
    args += [lre_r, lim_r, ldt_r, bbd[0], bbd[1], cbd[0], cbd[1], d_skip.reshape(1, W), w_glu, b_glu.reshape(1, W)]
    scratch = [pltpu.VMEM((T, CW, SW), bf16), pltpu.VMEM((T, CW, SW), bf16),
               pltpu.VMEM((T // 2, 4 * SW, 2 * CW), bf16),
               pltpu.VMEM((T, 1, SW), f32), pltpu.VMEM((T, 1, SW), f32)]
    if sample:
        grid = (M // tb, 1)
        st_spec = pl.BlockSpec((tb, NS), lambda b, t: (b, 0))
        st_shape = jax.ShapeDtypeStruct((M, NS), f32)
    else:
        grid = (B, NT)
        st_spec = pl.BlockSpec((None, KT, T, LANES), lambda b, t: (b, 0, 0, 0))
        st_shape = jax.ShapeDtypeStruct((B, KT, T, LANES), f32)
        scratch += [pltpu.VMEM((KT, tb * T, LANES), f32), pltpu.VMEM((KT, tb * T, LANES), f32),
                    pltpu.VMEM((KT, T, LANES), f32), pltpu.VMEM((KT, T, LANES), f32), pltpu.VMEM((tb, W), f32)]
    y, xr, xi = pl.pallas_call(
        body,
        grid=grid,
        in_specs=in_specs,
        out_specs=[pl.BlockSpec((tb, W), lambda b, t: (b * NT + t, 0)), st_spec, st_spec],
        out_shape=[jax.ShapeDtypeStruct((M, W), bf16), st_shape, st_shape],
        scratch_shapes=scratch,
        compiler_params=_params(("arbitrary", "arbitrary"), 56),
        name="s5_sample" if sample else "s5_prompt",
    )(*args)
    if not sample:
        xr = jnp.transpose(xr, (0, 2, 1, 3)).reshape(B, NS)
        xi = jnp.transpose(xi, (0, 2, 1, 3)).reshape(B, NS)
    return y, xr, xi


def _trunk(x3, st, w, *, sample):
    B, L, D = x3.shape
    M = B * L
    x = x3.reshape(M, D)
    H_g, DK, DV = st['gla_S'].shape[1:] if sample else w['gla_dims']
    H_m, DH = w['ml_dims']
    G, P = w['s5_dims']
    tm = M if sample else min(1024, L)
    tf = M if sample else min(1024, L)
    to = M if sample else min(2048, M)
    tb = min(1024, L)
    out = {}

    p0, alr = _norm_matmul(x, w['g_mix0'], w['w_in0_main'], tm=tm, tn=1024, w_side=w['w_in0_alr'])
    x_blk = (2 * H_g * DK + 2 * H_g * DV) // LANES
    W_rg = w['rg_lambda'].shape[0]
    g_blk = x_blk + W_rg // LANES
    rg_tail = p0.reshape(B, L, p0.shape[1])[:, max(L - 3, 0):, x_blk * LANES:x_blk * LANES + W_rg]
    if sample:
        ya, out['gla_S'] = _gla_sample(p0, alr, w['gla_w2p'], w['gla_b_alpha'], w['gla_g_norm'], st['gla_S'],
                                       H=H_g, DK=DK, DV=DV)
        yb, out['rg_h'] = _rglru_sample(p0, st['rg_conv'], st['rg_h'], w['rg_conv_w'], w['rg_conv_b'], w['rg_w_r'],
                                        w['rg_b_r'], w['rg_w_i'], w['rg_b_i'], w['rg_lambda'], x_blk=x_blk, g_blk=g_blk)
        out['rg_conv'] = jnp.concatenate([st['rg_conv'][:, 1:], rg_tail], axis=1)
    else:
        ya, out['gla_S'] = _gla_prompt(p0, alr, w['gla_w2p'], w['gla_b_alpha'], w['gla_g_norm'],
                                       B=B, L=L, H=H_g, DK=DK, DV=DV, tb=tb)
        yb, out['rg_h'] = _rglru_prompt(p0, w['rg_conv_w'], w['rg_conv_b'], w['rg_w_r'], w['rg_b_r'], w['rg_w_i'],
                                        w['rg_b_i'], w['rg_lambda'], B=B, L=L, x_blk=x_blk, g_blk=g_blk)
        out['rg_conv'] = rg_tail
    x = _out_proj(ya, yb, w['w_out0'], x, tm=to, tn=512)
    x, ffn0 = _conv_ffn(x, w['g_ffn'][0], w['ffn'][0], seq_len=L, tm=tf, prev=st['ffn_conv'] if sample else None)

    W_ml = H_m * DH
    p1 = _norm_matmul(x, w['g_mix1'], [w['w_in1']], tm=tm, tn=1024)
    xm_tail = p1.reshape(B, L, p1.shape[1])[:, max(L - 3, 0):, :W_ml]
    q, k, v, gates, xc = _mlstm_stage1(p1, w['ml_conv_w'], w['ml_conv_b'], w['ml_wq_t'], w['ml_wk_t'], w['ml_wv_t'],
                                       w['ml_wg'], w['ml_bg'], seq_len=L, tr=M if sample else min(256, L), H=H_m, DH=DH,
                                       conv_state=st['ml_conv'] if sample else None)
    u_blk = 2 * W_ml // (G * w['s5_gc'])
    if sample:
        yc, out['ml_C'], out['ml_n'], out['ml_m'] = _mlstm_sample(
            q, k, v, gates, xc, p1, w['ml_g_norm'], w['ml_skip'], st['ml_C'], st['ml_n'], st['ml_m'], H=H_m, DH=DH)
        out['ml_conv'] = jnp.concatenate([st['ml_conv'][:, 1:], xm_tail], axis=1)
        yd, s5r, s5i = _s5_mixer(p1, u_blk, w['s5_rows'], w['s5_bbd'], w['s5_cbd'], w['s5_D'], w['s5_w_glu'],
                                 w['s5_b_glu'], B=B, L=1, tb=B,
                                 state=(st['s5_re'].reshape(B, G * P), st['s5_im'].reshape(B, G * P)))
    else:
        yc, out['ml_C'], out['ml_n'], out['ml_m'] = _mlstm_prompt(
            q, k, v, gates, xc, p1, w['ml_g_norm'], w['ml_skip'], B=B, L=L, H=H_m, DH=DH, tb=tb)
        out['ml_conv'] = xm_tail
        yd, s5r, s5i = _s5_mixer(p1, u_blk, w['s5_rows'], w['s5_bbd'], w['s5_cbd'], w['s5_D'], w['s5_w_glu'],
                                 w['s5_b_glu'], B=B, L=L, tb=min(512, L))
    out['s5_re'] = s5r.reshape(B, G, P)
    out['s5_im'] = s5i.reshape(B, G, P)
    x = _out_proj(yc, yd, w['w_out1'], x, tm=to, tn=512)
    x, ffn1 = _conv_ffn(x, w['g_ffn'][1], w['ffn'][1], seq_len=L, tm=tf, prev=st['ffn_conv'] if sample else None,
                        final_g=w['g_final'])
    out['ffn_conv'] = jnp.stack([ffn0, ffn1], axis=0)
    return x.reshape(B, L, D), out


def kernel(x_prompt, x_sample, state_gla_S, state_rglru_h, state_rglru_conv, state_mlstm_C, state_mlstm_n, state_mlstm_m, state_mlstm_conv, state_s5_re, state_s5_im, state_ffn_conv, g_mix0, w_in0, gla_w_alpha2, gla_b_alpha, gla_g_norm, rg_conv_w, rg_conv_b, rg_w_r, rg_b_r, rg_w_i, rg_b_i, rg_lambda, w_out0, g_mix1, w_in1, ml_conv_w, ml_conv_b, ml_wq, ml_wk, ml_wv, ml_w_igate, ml_b_igate, ml_w_fgate, ml_b_fgate, ml_g_norm, ml_skip, s5_lam_re, s5_lam_im, s5_log_dt, s5_B_re, s5_B_im, s5_C_re, s5_C_im, s5_D, s5_w_glu, s5_b_glu, w_out1, g_ffn, ffn_w_up, ffn_conv_w, ffn_conv_b, ffn_w_down, g_final):
    _, H_g, DK, DV = state_gla_S.shape
    _, H_m, DH, _ = state_mlstm_C.shape
    G, P = s5_lam_re.shape
    rank = gla_w_alpha2.shape[0]
    n_main = 2 * H_g * DK + 2 * H_g * DV
    w_in0_main = [w_in0[:, :n_main].astype(bf16), w_in0[:, n_main + rank:].astype(bf16)]
    w_in0_alr = jnp.pad(w_in0[:, n_main:n_main + rank], ((0, 0), (0, LANES - rank))).astype(bf16)
    gla_w2p = jnp.pad(gla_w_alpha2, ((0, LANES - rank), (0, 0)))
    ml_tile = 256
    ml_wg = jnp.pad(jnp.concatenate([ml_w_igate, ml_w_fgate], axis=1), ((0, 0), (0, LANES - 2 * H_m))).astype(bf16)
    ml_bg = jnp.pad(jnp.concatenate([ml_b_igate, ml_b_fgate]), (0, LANES - 2 * H_m)).reshape(1, LANES)
    rows, bbd, cbd = _s5_layouts(s5_lam_re, s5_lam_im, s5_log_dt, s5_B_re, s5_B_im, s5_C_re, s5_C_im)
    w = dict(
        g_mix0=g_mix0, w_in0_main=w_in0_main, w_in0_alr=w_in0_alr, gla_w2p=gla_w2p, gla_b_alpha=gla_b_alpha,
        gla_g_norm=gla_g_norm, gla_dims=(H_g, DK, DV), rg_conv_w=rg_conv_w, rg_conv_b=rg_conv_b, rg_w_r=rg_w_r,
        rg_b_r=rg_b_r, rg_w_i=rg_w_i, rg_b_i=rg_b_i, rg_lambda=rg_lambda, w_out0=w_out0.astype(bf16),
        g_mix1=g_mix1, w_in1=w_in1.astype(bf16), ml_conv_w=ml_conv_w, ml_conv_b=ml_conv_b,
        ml_wq_t=_blockdiag_tiles(ml_wq, ml_tile).astype(bf16), ml_wk_t=_blockdiag_tiles(ml_wk, ml_tile).astype(bf16),
        ml_wv_t=_blockdiag_tiles(ml_wv, ml_tile).astype(bf16), ml_wg=ml_wg, ml_bg=ml_bg, ml_g_norm=ml_g_norm,
        ml_skip=ml_skip, ml_dims=(H_m, DH), s5_dims=(G, P), s5_gc=s5_B_re.shape[2], s5_rows=rows, s5_bbd=bbd,
        s5_cbd=cbd, s5_D=s5_D, s5_w_glu=s5_w_glu.astype(bf16), s5_b_glu=s5_b_glu, w_out1=w_out1.astype(bf16),
        g_ffn=g_ffn, g_final=g_final,
        ffn=_ffn_prepare(ffn_w_up, ffn_conv_w, ffn_conv_b, ffn_w_down))
    st_s = dict(gla_S=state_gla_S, rg_h=state_rglru_h, rg_conv=state_rglru_conv, ml_C=state_mlstm_C,
                ml_n=state_mlstm_n, ml_m=state_mlstm_m, ml_conv=state_mlstm_conv, s5_re=state_s5_re,
                s5_im=state_s5_im, ffn_conv=state_ffn_conv)
    y_p, np_ = _trunk(x_prompt, None, w, sample=False)
    y_s, ns_ = _trunk(x_sample, st_s, w, sample=True)
    names = ('gla_S', 'rg_h', 'rg_conv', 'ml_C', 'ml_n', 'ml_m', 'ml_conv', 's5_re', 's5_im', 'ffn_conv')
    outs = [y_p, y_s]
    for nme in names:
        outs += [np_[nme], ns_[nme]]
    return tuple(outs)
```

```python
import functools

import jax
import jax.numpy as jnp
from jax import lax
from jax.experimental import pallas as pl
from jax.experimental.pallas import tpu as pltpu

f32 = jnp.float32
bf16 = jnp.bfloat16

EPS = 1e-6
CHUNK = 64
GLA_TAU = 16.0
RG_C = 8.0
PAST_LEN = 16384
LANES = 128
SUBLANES = 8
HALO = 16
MIB = 1024 * 1024


def _params(sem, vmem_mib):
    return pltpu.CompilerParams(dimension_semantics=sem, vmem_limit_bytes=int(vmem_mib * MIB))


def _dot(a, b):
    return jnp.dot(a, b, preferred_element_type=f32)


def _dot_nt(a, b):
    return lax.dot_general(a, b, (((1,), (1,)), ((), ())), preferred_element_type=f32)


def _dot_tn(a, b):
    return lax.dot_general(a, b, (((0,), (0,)), ((), ())), preferred_element_type=f32)


def _rms(x, g):
    return x * lax.rsqrt(jnp.mean(x * x, axis=-1, keepdims=True) + EPS) * g


def _eye(n):
    return lax.broadcasted_iota(jnp.int32, (n, n), 0) == lax.broadcasted_iota(jnp.int32, (n, n), 1)


def _col_from_row(row):
    n = row.shape[1]
    return jnp.sum(jnp.where(_eye(n), jnp.broadcast_to(row, (n, n)), 0.0), axis=1, keepdims=True)


def _chunk_cumsum(x, chunk):
    pos = lax.broadcasted_iota(jnp.int32, (x.shape[0], 1), 0) % chunk
    step = 1
    while step < chunk:
        x = x + jnp.where(pos >= step, pltpu.roll(x, step, 0), 0.0)
        step *= 2
    return x


def _norm_matmul(x, g, ws, *, tm, tn, w_side=None):
    M, D = x.shape
    nblk = [w.shape[1] // tn for w in ws]
    start = [sum(nblk[:s]) for s in range(len(ws) + 1)]
    N = tn * start[-1]
    rc = min(tm, 256)
    side = w_side is not None

    def body(*refs):
        x_ref, g_ref = refs[:2]
        w_refs = refs[2:2 + len(ws)]
        rest = refs[2 + len(ws):]
        if side:
            ws_ref, o_ref, os_ref, xn_ref = rest
        else:
            o_ref, xn_ref = rest
        j = pl.program_id(1)

        @pl.when(j == 0)
        def _():
            def chunk(r, c):
                rows = pl.ds(pl.multiple_of(r * rc, rc), rc)
                xn_ref[rows, :] = _rms(x_ref[rows, :], g_ref[...]).astype(bf16)
                return c
            lax.fori_loop(0, tm // rc, chunk, 0)
            if side:
                os_ref[...] = _dot(xn_ref[...], ws_ref[...])

        if len(ws) == 1:
            o_ref[...] = _dot(xn_ref[...], w_refs[0][...])
        else:
            for s, w_ref in enumerate(w_refs):
                @pl.when(jnp.logical_and(j >= start[s], j < start[s + 1]))
                def _(w_ref=w_ref):
                    o_ref[...] = _dot(xn_ref[...], w_ref[...])

    in_specs = [pl.BlockSpec((tm, D), lambda i, j: (i, 0)),
                pl.BlockSpec((1, D), lambda i, j: (0, 0))]
    in_specs += [pl.BlockSpec((D, tn), lambda i, j, s=s: (0, jnp.clip(j - start[s], 0, nblk[s] - 1)))
                 for s in range(len(ws))]
    out_specs = [pl.BlockSpec((tm, tn), lambda i, j: (i, j))]
    out_shape = [jax.ShapeDtypeStruct((M, N), f32)]
    args = [x, g.reshape(1, D)] + list(ws)
    if side:
        ns = w_side.shape[1]
        in_specs.append(pl.BlockSpec((D, ns), lambda i, j: (0, 0)))
        out_specs.append(pl.BlockSpec((tm, ns), lambda i, j: (i, 0)))
        out_shape.append(jax.ShapeDtypeStruct((M, ns), f32))
        args.append(w_side)
    outs = pl.pallas_call(
        body,
        grid=(M // tm, N // tn),
        in_specs=in_specs,
        out_specs=out_specs,
        out_shape=out_shape,
        scratch_shapes=[pltpu.VMEM((tm, D), bf16)],
        compiler_params=_params(("parallel", "arbitrary"), 56),
        name="norm_matmul",
    )(*args)
    return outs if side else outs[0]


def _out_proj(ya, yb, w, res, *, tm, tn):
    M, Ka = ya.shape
    Kb = yb.shape[1]
    N = w.shape[1]
    assert Ka == Kb and w.shape[0] == Ka + Kb

    def body(ya_ref, yb_ref, wa_ref, wb_ref, r_ref, o_ref):
        o_ref[...] = (r_ref[...] + _dot(ya_ref[...].astype(bf16), wa_ref[...])
                      + _dot(yb_ref[...].astype(bf16), wb_ref[...]))

    return pl.pallas_call(
        body,
        grid=(M // tm, N // tn),
        in_specs=[pl.BlockSpec((tm, Ka), lambda i, j: (i, 0)),
                  pl.BlockSpec((tm, Kb), lambda i, j: (i, 0)),
                  pl.BlockSpec((Ka, tn), lambda i, j: (0, j)),
                  pl.BlockSpec((Kb, tn), lambda i, j: (1, j)),
                  pl.BlockSpec((tm, tn), lambda i, j: (i, j))],
        out_specs=pl.BlockSpec((tm, tn), lambda i, j: (i, j)),
        out_shape=jax.ShapeDtypeStruct((M, N), f32),
        compiler_params=_params(("parallel", "arbitrary"), 48),
        name="out_proj",
    )(ya, yb, w, w, res)


FFN_STEP = 512


def _ffn_to_steps(a, F):
    nsteps = -(-F // FFN_STEP)
    lead = a.shape[:-1]
    nd = len(lead)
    gv = a.reshape(lead + (2, F))
    gv = jnp.pad(gv, [(0, 0)] * (nd + 1) + [(0, nsteps * FFN_STEP - F)])
    gv = gv.reshape(lead + (2, nsteps, FFN_STEP))
    gv = jnp.transpose(gv, (nd + 1,) + tuple(range(nd)) + (nd, nd + 2))
    return gv.reshape((nsteps,) + lead + (2 * FFN_STEP,))


def _ffn_prepare(w_up, conv_w, conv_b, w_down):
    NL, F, D = w_down.shape
    assert F % LANES == 0 and FFN_STEP % LANES == 0
    nt = F // LANES
    tp = FFN_STEP // LANES
    nsteps = -(-F // FFN_STEP)

    def body(*refs):
        g_in, v_in, d_in = refs[:tp], refs[tp:2 * tp], refs[2 * tp:3 * tp]
        wu_o, wd_o = refs[3 * tp:]
        for t in range(tp):
            keep = tp * pl.program_id(1) + t < nt
            cols = slice(t * LANES, (t + 1) * LANES)
            wu_o[:, cols] = jnp.where(keep, g_in[t][...], 0.0).astype(bf16)
            wu_o[:, FFN_STEP + t * LANES:FFN_STEP + (t + 1) * LANES] = jnp.where(keep, v_in[t][...], 0.0).astype(bf16)
            wd_o[cols, :] = jnp.where(keep, d_in[t][...], 0.0).astype(bf16)

    tile = lambda j, t: jnp.minimum(tp * j + t, nt - 1)
    in_specs = ([pl.BlockSpec((None, D, LANES), lambda l, j, t=t: (l, 0, tile(j, t))) for t in range(tp)]
                + [pl.BlockSpec((None, D, LANES), lambda l, j, t=t: (l, 0, nt + tile(j, t))) for t in range(tp)]
                + [pl.BlockSpec((None, LANES, D), lambda l, j, t=t: (l, tile(j, t), 0)) for t in range(tp)])
    wu, wd = pl.pallas_call(
        body,
        grid=(NL, nsteps),
        in_specs=in_specs,
        out_specs=[pl.BlockSpec((None, None, D, 2 * FFN_STEP), lambda l, j: (l, j, 0, 0)),
                   pl.BlockSpec((None, None, FFN_STEP, D), lambda l, j: (l, j, 0, 0))],
        out_shape=[jax.ShapeDtypeStruct((NL, nsteps, D, 2 * FFN_STEP), bf16),
                   jax.ShapeDtypeStruct((NL, nsteps, FFN_STEP, D), bf16)],
        compiler_params=_params(("parallel", "parallel"), 40),
        name="ffn_weight_layout",
    )(*([w_up] * (2 * tp) + [w_down] * tp))
    return [dict(wu=wu, wd=wd, layer=l, F=F, cw=_ffn_to_steps(conv_w[l], F),
                 cb=_ffn_to_steps(conv_b[l], F).reshape(nsteps, 1, 2 * FFN_STEP)) for l in range(NL)]


def _conv_ffn(x, g, fw, *, seq_len, tm, prev=None, final_g=None):
    M, D = x.shape
    F = fw['F']
    layer = fw['layer']
    nsteps = fw['wu'].shape[1]
    sample = prev is not None
    rc = min(tm, 256)
    tiles_per_seq = max(seq_len // tm, 1)
    W2 = 2 * FFN_STEP

    def body(*refs):
        it = iter(refs)
        x_ref = next(it)
        xh_ref = None if sample else next(it)
        prev_refs = [next(it) for _ in range(2 * FFN_STEP // LANES)] if sample else None
        g_ref = next(it)
        wu_ref, cw_ref, cb_ref, wd_ref = next(it), next(it), next(it), next(it)
        fg_ref = next(it) if final_g is not None else None
        o_ref, tail_g_ref, tail_v_ref = next(it), next(it), next(it)
        xn_ref, up_scr = next(it), next(it)
        i = pl.program_id(0)
        j = pl.program_id(1)

        @pl.when(j == 0)
        def _():
            if sample:
                xn_ref[0:HALO, :] = jnp.zeros((HALO, D), bf16)
            else:
                keep = (i % tiles_per_seq != 0).astype(f32)
                hist = _rms(xh_ref[...], g_ref[...]) * keep
                xn_ref[0:HALO, :] = jnp.concatenate([jnp.zeros_like(hist), hist], axis=0).astype(bf16)

            def chunk(r, c):
                rows = pl.ds(pl.multiple_of(r * rc, rc), rc)
                xr = x_ref[rows, :]
                o_ref[rows, :] = xr
                xn_ref[pl.ds(pl.multiple_of(HALO + r * rc, HALO), rc), :] = _rms(xr, g_ref[...]).astype(bf16)
                return c
            lax.fori_loop(0, tm // rc, chunk, 0)

        cw = cw_ref[0]
        cb = cb_ref[0]
        nchunk = tm // rc
        for r in range(nchunk):
            if sample:
                up_scr[r, HALO:, :] = _dot(xn_ref[pl.ds(HALO + r * rc, rc), :], wu_ref[...])
            else:
                up_scr[r] = _dot(xn_ref[pl.ds(r * rc, rc + HALO), :], wu_ref[...])
        for r in range(nchunk):
            rows = pl.ds(r * rc, rc)
            up = up_scr[r, HALO:, :]
            if sample:
                p0 = jnp.concatenate([p[rows, 0, :] for p in prev_refs], axis=1)
                p1 = jnp.concatenate([p[rows, 1, :] for p in prev_refs], axis=1)
                conv = cb + cw[0:1] * p0 + cw[1:2] * p1 + cw[2:3] * up
                tail_g_ref[rows, :] = up[:, :FFN_STEP]
                tail_v_ref[rows, :] = up[:, FFN_STEP:]
            else:
                conv = (cb + cw[0:1] * up_scr[r, HALO - 2:HALO - 2 + rc, :]
                        + cw[1:2] * up_scr[r, HALO - 1:HALO - 1 + rc, :] + cw[2:3] * up)
                if r == nchunk - 1:
                    tail_g_ref[...] = up[rc - SUBLANES:, :FFN_STEP]
                    tail_v_ref[...] = up[rc - SUBLANES:, FFN_STEP:]
            h = jax.nn.gelu(conv[:, :FFN_STEP]) * conv[:, FFN_STEP:]
            o_ref[rows, :] += _dot(h.astype(bf16), wd_ref[...])

        if final_g is not None:
            @pl.when(j == nsteps - 1)
            def _():
                def chunk2(r, c):
                    rows = pl.ds(pl.multiple_of(r * rc, rc), rc)
                    o_ref[rows, :] = _rms(o_ref[rows, :], fg_ref[...])
                    return c
                lax.fori_loop(0, tm // rc, chunk2, 0)

    in_specs = [pl.BlockSpec((tm, D), lambda i, j: (i, 0))]
    args = [x]
    if sample:
        nt = F // LANES
        tp = FFN_STEP // LANES
        last = 2 * nt - 1
        for off in list(range(tp)) + [nt + t for t in range(tp)]:
            in_specs.append(pl.BlockSpec((None, tm, 2, LANES),
                                         lambda i, j, off=off: (layer, i, 0, jnp.minimum(tp * j + off, last))))
            args.append(prev)
    else:
        in_specs.append(pl.BlockSpec((SUBLANES, D), lambda i, j: (jnp.maximum(i * (tm // SUBLANES) - 1, 0), 0)))
        args.append(x)
    in_specs += [pl.BlockSpec((1, D), lambda i, j: (0, 0)),
                 pl.BlockSpec((None, None, D, W2), lambda i, j: (layer, j, 0, 0)),
                 pl.BlockSpec((1, 3, W2), lambda i, j: (j, 0, 0)),
                 pl.BlockSpec((1, 1, W2), lambda i, j: (j, 0, 0)),
                 pl.BlockSpec((None, None, FFN_STEP, D), lambda i, j: (layer, j, 0, 0))]
    args += [g.reshape(1, D), fw['wu'], fw['cw'], fw['cb'], fw['wd']]
    if final_g is not None:
        in_specs.append(pl.BlockSpec((1, D), lambda i, j: (0, 0)))
        args.append(final_g.reshape(1, D))
    FP = nsteps * FFN_STEP
    if sample:
        tail_shape = (M, FP)
        tail_spec = pl.BlockSpec((tm, FFN_STEP), lambda i, j: (i, j))
    else:
        tail_shape = (M // tm, SUBLANES, FP)
        tail_spec = pl.BlockSpec((None, SUBLANES, FFN_STEP), lambda i, j: (i, 0, j))
    out, tail_g, tail_v = pl.pallas_call(
        body,
        grid=(M // tm, nsteps),
        in_specs=in_specs,
        out_specs=[pl.BlockSpec((tm, D), lambda i, j: (i, 0)), tail_spec, tail_spec],
        out_shape=[jax.ShapeDtypeStruct((M, D), f32), jax.ShapeDtypeStruct(tail_shape, f32),
                   jax.ShapeDtypeStruct(tail_shape, f32)],
        scratch_shapes=[pltpu.VMEM((tm + HALO, D), bf16), pltpu.VMEM((tm // rc, rc + HALO, W2), f32)],
        compiler_params=_params(("parallel", "arbitrary"), 56),
        name="conv_ffn_sample" if sample else "conv_ffn",
    )(*args)
    if sample:
        up_rows = jnp.concatenate([tail_g[:, :F], tail_v[:, :F]], axis=1)
        new_buf = jnp.stack([prev[layer, :, 1, :], up_rows], axis=1)
    else:
        nseq = M // seq_len
        pick = lambda t: t.reshape(nseq, tiles_per_seq, SUBLANES, FP)[:, -1, SUBLANES - 2:, :F]
        new_buf = jnp.concatenate([pick(tail_g), pick(tail_v)], axis=-1)
    return out, new_buf


def _gla_prompt(p0, alr, w2p, b_alpha, g_norm, *, B, L, H, DK, DV, tb):
    NT = L // tb
    NC = tb // CHUNK
    C = CHUNK
    scale = DK ** -0.5
    qk_blocks = H
    v_off = 2 * H * DK // DV

    def body(q_ref, k_ref, v_ref, r_ref, a_ref, w2_ref, ba_ref, gn_ref, y_ref, s_out_ref, s_scr):
        t = pl.program_id(2)

        @pl.when(t == 0)
        def _():
            s_scr[...] = jnp.zeros_like(s_scr)

        z = _dot(a_ref[...].astype(bf16), w2_ref[...].astype(bf16)) + ba_ref[...]
        gl = jax.nn.log_sigmoid(z) * (1.0 / GLA_TAU)
        bc3 = _chunk_cumsum(gl, C).reshape(NC, C, DK)
        bl3 = bc3[:, C - 1:C, :]
        q3 = (q_ref[...] * scale).reshape(NC, C, DK)
        k3 = k_ref[...].reshape(NC, C, DK)
        v3 = v_ref[...].astype(bf16).reshape(NC, C, DV)
        qd3 = (q3 * jnp.exp(bc3)).astype(bf16)
        kd3 = (k3 * jnp.exp(-bc3)).astype(bf16)
        kdec3 = (k3 * jnp.exp(bl3 - bc3)).astype(bf16)
        causal = (lax.broadcasted_iota(jnp.int32, (1, C, C), 1) >= lax.broadcasted_iota(jnp.int32, (1, C, C), 2))
        att = jnp.where(causal, jnp.einsum('cik,cjk->cij', qd3, kd3, preferred_element_type=f32), 0.0)
        intra = jnp.einsum('cij,cjv->civ', att.astype(bf16), v3, preferred_element_type=f32)
        ds = jnp.einsum('cjk,cjv->ckv', kdec3, v3, preferred_element_type=f32)
        s = s_scr[...]
        s_in = []
        for c in range(NC):
            s_in.append(s)
            s = _col_from_row(jnp.exp(bl3[c])) * s + ds[c]
        s_scr[...] = s
        s_all = jnp.stack(s_in).astype(bf16)
        o = (intra + jnp.einsum('cik,ckv->civ', qd3, s_all, preferred_element_type=f32)).reshape(tb, DV)
        rr = r_ref[...]
        y_ref[...] = (_rms(o, gn_ref[...]) * (rr * jax.nn.sigmoid(rr))).astype(bf16)

        @pl.when(t == NT - 1)
        def _():
            s_out_ref[0, 0] = s
    return pl.pallas_call(
        body,
        grid=(B, H, NT),
        in_specs=[pl.BlockSpec((tb, DK), lambda b, h, t: (b * NT + t, h)),
                  pl.BlockSpec((tb, DK), lambda b, h, t: (b * NT + t, qk_blocks + h)),
                  pl.BlockSpec((tb, DV), lambda b, h, t: (b * NT + t, v_off + h)),
                  pl.BlockSpec((tb, DV), lambda b, h, t: (b * NT + t, v_off + H + h)),
                  pl.BlockSpec((tb, LANES), lambda b, h, t: (b * NT + t, 0)),
                  pl.BlockSpec((LANES, DK), lambda b, h, t: (0, h)),
                  pl.BlockSpec((1, DK), lambda b, h, t: (0, h)),
                  pl.BlockSpec((1, DV), lambda b, h, t: (0, h))],
        out_specs=[pl.BlockSpec((tb, DV), lambda b, h, t: (b * NT + t, h)),
                   pl.BlockSpec((1, 1, DK, DV), lambda b, h, t: (b, h, 0, 0))],
        out_shape=[jax.ShapeDtypeStruct((B * L, H * DV), bf16),
                   jax.ShapeDtypeStruct((B, H, DK, DV), f32)],
        scratch_shapes=[pltpu.VMEM((DK, DV), f32)],
        compiler_params=_params(("parallel", "parallel", "arbitrary"), 32),
        name="gla_prompt",
    )(p0, p0, p0, p0, alr, w2p, b_alpha.reshape(1, -1), g_norm.reshape(1, -1))


def _gla_sample(p0, alr, w2p, b_alpha, g_norm, s0, *, H, DK, DV):
    Bs = p0.shape[0]
    scale = DK ** -0.5
    qkw = H * DK
    vw = H * DV
    assert vw % qkw == 0
    SB = SUBLANES

    def body(q_ref, k_ref, v_ref, r_ref, a_ref, w2_ref, ba_ref, gn_ref, s_ref, y_ref, so_ref, gl_scr):
        z = _dot(a_ref[...].astype(bf16), w2_ref[...].astype(bf16)) + ba_ref[...]
        gl_scr[...] = jax.nn.log_sigmoid(z) * (1.0 / GLA_TAU)

        for h in range(H):
            ks = slice(h * DK, (h + 1) * DK)
            vs = slice(h * DV, (h + 1) * DV)
            a_t = jnp.exp(gl_scr[:, ks]).T
            k_t = k_ref[:, ks].T
            q_t = (q_ref[:, ks] * scale).T
            v = v_ref[:, vs]
            outs = []
            for s in range(SB):
                sn = a_t[:, s:s + 1] * s_ref[s, h] + k_t[:, s:s + 1] * v[s:s + 1, :]
                so_ref[s, h] = sn
                outs.append(jnp.sum(q_t[:, s:s + 1] * sn, axis=0, keepdims=True))
            o = jnp.concatenate(outs, axis=0)
            rr = r_ref[:, vs]
            y_ref[:, vs] = _rms(o, gn_ref[:, vs]) * (rr * jax.nn.sigmoid(rr))

    v_blk = 2 * qkw // vw
    return pl.pallas_call(
        body,
        grid=(Bs // SB,),
        in_specs=[pl.BlockSpec((SB, qkw), lambda b: (b, 0)),
                  pl.BlockSpec((SB, qkw), lambda b: (b, 1)),
                  pl.BlockSpec((SB, vw), lambda b: (b, v_blk)),
                  pl.BlockSpec((SB, vw), lambda b: (b, v_blk + 1)),
                  pl.BlockSpec((SB, LANES), lambda b: (b, 0)),
                  pl.BlockSpec((LANES, qkw), lambda b: (0, 0)),
                  pl.BlockSpec((1, qkw), lambda b: (0, 0)),
                  pl.BlockSpec((1, vw), lambda b: (0, 0)),
                  pl.BlockSpec((SB, H, DK, DV), lambda b: (b, 0, 0, 0))],
        out_specs=[pl.BlockSpec((SB, vw), lambda b: (b, 0)),
                   pl.BlockSpec((SB, H, DK, DV), lambda b: (b, 0, 0, 0))],
        out_shape=[jax.ShapeDtypeStruct((Bs, vw), f32),
                   jax.ShapeDtypeStruct((Bs, H, DK, DV), f32)],
        scratch_shapes=[pltpu.VMEM((SB, qkw), f32)],
        compiler_params=_params(("parallel",), 32),
        name="gla_sample",
    )(p0, p0, p0, p0, alr, w2p, b_alpha.reshape(1, -1), g_norm.reshape(1, -1), s0)


def _rg_gates(xc, wr, br, wi, bi, sp):
    xb = xc.astype(bf16)
    r = jax.nn.sigmoid(_dot(xb, wr) + br)
    i = jax.nn.sigmoid(_dot(xb, wi) + bi)
    log_a = -RG_C * r * sp
    a = jnp.exp(log_a)
    mult = jnp.sqrt(1.0 - a * a)
    return a, mult, i


def _rglru_prompt(p0, conv_w, conv_b, w_r, b_r, w_i, b_i, lam, *, B, L, x_blk, g_blk):
    M = B * L
    NB, BS, _ = w_r.shape
    assert BS == LANES
    rc = min(256, L)

    def body(x_ref, gg_ref, cw_ref, cb_ref, wr_ref, br_ref, wi_ref, bi_ref, lam_ref, y_ref, hl_ref, a_scr, b_scr, xs_scr):
        wr = wr_ref[...].astype(bf16)
        wi = wi_ref[...].astype(bf16)
        sp = jax.nn.softplus(-lam_ref[...])
        cw = cw_ref[...]

        def chunk(c, carry):
            r0 = pl.multiple_of(c * rc, rc)
            rows = pl.ds(r0, rc)
            x = x_ref[rows, :]
            start = (r0 % L) == 0
            prev = x_ref[pl.ds(pl.multiple_of(jnp.maximum(r0 - SUBLANES, 0), SUBLANES), SUBLANES), :]
            xs_scr[0:SUBLANES, :] = jnp.where(start, 0.0, prev)
            xs_scr[SUBLANES:, :] = x
            xc = (cb_ref[...] + cw[0:1] * xs_scr[SUBLANES - 3:SUBLANES - 3 + rc, :]
                  + cw[1:2] * xs_scr[SUBLANES - 2:SUBLANES - 2 + rc, :]
                  + cw[2:3] * xs_scr[SUBLANES - 1:SUBLANES - 1 + rc, :] + cw[3:4] * x)
            a, mult, ig = _rg_gates(xc, wr, br_ref[...], wi, bi_ref[...], sp)
            first = jnp.logical_and(start, lax.broadcasted_iota(jnp.int32, (rc, 1), 0) == 0)
            mult = jnp.where(first, 1.0, mult)
            a_scr[rows, :] = a
            b_scr[rows, :] = mult * (ig * xc)
            return carry
        lax.fori_loop(0, M // rc, chunk, 0)

        def step(t, hs):
            new = []
            for b in range(B):
                row = pl.ds(b * L + t, 1)
                h = a_scr[row, :] * hs[b] + b_scr[row, :]
                b_scr[row, :] = h
                new.append(h)
            return tuple(new)
        hs = lax.fori_loop(0, L, step, tuple(jnp.zeros((1, LANES), f32) for _ in range(B)), unroll=8)
        hl_ref[...] = jnp.concatenate(hs, axis=0)

        def outc(c, carry):
            rows = pl.ds(pl.multiple_of(c * rc, rc), rc)
            y_ref[rows, :] = (b_scr[rows, :] * jax.nn.gelu(gg_ref[rows, :])).astype(bf16)
            return carry
        lax.fori_loop(0, M // rc, outc, 0)

    W = NB * BS
    return pl.pallas_call(
        body,
        grid=(NB,),
        in_specs=[pl.BlockSpec((M, LANES), lambda n: (0, x_blk + n)),
                  pl.BlockSpec((M, LANES), lambda n: (0, g_blk + n)),
                  pl.BlockSpec((4, LANES), lambda n: (0, n)),
                  pl.BlockSpec((1, LANES), lambda n: (0, n)),
                  pl.BlockSpec((None, BS, BS), lambda n: (n, 0, 0)),
                  pl.BlockSpec((1, LANES), lambda n: (0, n)),
                  pl.BlockSpec((None, BS, BS), lambda n: (n, 0, 0)),
                  pl.BlockSpec((1, LANES), lambda n: (0, n)),
                  pl.BlockSpec((1, LANES), lambda n: (0, n))],
        out_specs=[pl.BlockSpec((M, LANES), lambda n: (0, n)),
                   pl.BlockSpec((B, LANES), lambda n: (0, n))],
        out_shape=[jax.ShapeDtypeStruct((M, W), bf16), jax.ShapeDtypeStruct((B, W), f32)],
        scratch_shapes=[pltpu.VMEM((M, LANES), f32), pltpu.VMEM((M, LANES), f32), pltpu.VMEM((rc + SUBLANES, LANES), f32)],
        compiler_params=_params(("parallel",), 48),
        name="rglru_prompt",
    )(p0, p0, conv_w, conv_b.reshape(1, W), w_r, b_r.reshape(1, W), w_i, b_i.reshape(1, W), lam.reshape(1, W))


def _rglru_sample(p0, conv_state, h0, conv_w, conv_b, w_r, b_r, w_i, b_i, lam, *, x_blk, g_blk):
    Bs = p0.shape[0]
    NB, BS, _ = w_r.shape
    W = NB * BS
    s0, s1, s2 = conv_state[:, 0], conv_state[:, 1], conv_state[:, 2]

    def body(x_ref, gg_ref, s0_ref, s1_ref, s2_ref, h0_ref, cw_ref, cb_ref, wr_ref, br_ref, wi_ref, bi_ref, lam_ref,
             y_ref, h_ref):
        cw = cw_ref[...]
        x = x_ref[...]
        xc = cb_ref[...] + cw[0:1] * s0_ref[...] + cw[1:2] * s1_ref[...] + cw[2:3] * s2_ref[...] + cw[3:4] * x
        sp = jax.nn.softplus(-lam_ref[...])
        a, mult, ig = _rg_gates(xc, wr_ref[...].astype(bf16), br_ref[...], wi_ref[...].astype(bf16), bi_ref[...], sp)
        if PAST_LEN == 0:
            mult = jnp.ones_like(mult)
        h = a * h0_ref[...] + mult * (ig * xc)
        h_ref[...] = h
        y_ref[...] = (h * jax.nn.gelu(gg_ref[...])).astype(bf16)

    blk = lambda n: (0, n)
    vec = pl.BlockSpec((1, LANES), blk)
    mat = pl.BlockSpec((Bs, LANES), blk)
    y, h = pl.pallas_call(
        body,
        grid=(NB,),
        in_specs=[pl.BlockSpec((Bs, LANES), lambda n: (0, x_blk + n)),
                  pl.BlockSpec((Bs, LANES), lambda n: (0, g_blk + n)),
                  mat, mat, mat, mat,
                  pl.BlockSpec((4, LANES), blk), vec,
                  pl.BlockSpec((None, BS, BS), lambda n: (n, 0, 0)), vec,
                  pl.BlockSpec((None, BS, BS), lambda n: (n, 0, 0)), vec, vec],
        out_specs=[mat, mat],
        out_shape=[jax.ShapeDtypeStruct((Bs, W), bf16), jax.ShapeDtypeStruct((Bs, W), f32)],
        compiler_params=_params(("parallel",), 32),
        name="rglru_sample",
    )(p0, p0, s0, s1, s2, h0, conv_w, conv_b.reshape(1, W), w_r, b_r.reshape(1, W), w_i, b_i.reshape(1, W),
      lam.reshape(1, W))
    return y, h


def _blockdiag_tiles(w, tile):
    nblk, bs, _ = w.shape
    per = tile // bs
    nt = nblk // per
    rows = jnp.tile(w.reshape(nt, tile, bs), (1, 1, per))
    on_diag = (jnp.arange(tile)[:, None] // bs) == (jnp.arange(tile)[None, :] // bs)
    return jnp.where(on_diag[None], rows, 0.0)


def _mlstm_stage1(p1, conv_w, conv_b, wq_t, wk_t, wv_t, wg, bg, *, seq_len, tr, H, DH, conv_state=None):
    M = p1.shape[0]
    W = H * DH
    NTL, TL, _ = wq_t.shape
    sample = conv_state is not None
    tiles_per_seq = max(seq_len // tr, 1)
    kscale = DH ** -0.5

    def body(*refs):
        it = iter(refs)
        x_ref = next(it)
        if sample:
            s0_ref, s1_ref, s2_ref = next(it), next(it), next(it)
        else:
            xh_ref = next(it)
        cw_ref, cb_ref, wq_ref, wk_ref, wv_ref, wg_ref, bg_ref = (next(it) for _ in range(7))
        q_ref, k_ref, v_ref, g_ref, xc_ref = (next(it) for _ in range(5))
        cw = cw_ref[...]
        x = x_ref[...]
        if sample:
            conv = cb_ref[...] + cw[0:1] * s0_ref[...] + cw[1:2] * s1_ref[...] + cw[2:3] * s2_ref[...] + cw[3:4] * x
        else:
            xs_scr = next(it)
            keep = (pl.program_id(0) % tiles_per_seq != 0).astype(f32)
            xs_scr[0:SUBLANES, :] = xh_ref[...] * keep
            xs_scr[SUBLANES:, :] = x
            conv = (cb_ref[...] + cw[0:1] * xs_scr[SUBLANES - 3:SUBLANES - 3 + tr, :]
                    + cw[1:2] * xs_scr[SUBLANES - 2:SUBLANES - 2 + tr, :]
                    + cw[2:3] * xs_scr[SUBLANES - 1:SUBLANES - 1 + tr, :] + cw[3:4] * x)
        xc = conv * jax.nn.sigmoid(conv)
        xc_ref[...] = xc
        xcb = xc.astype(bf16)
        xb = x.astype(bf16)
        qs, ks, vs = [], [], []
        for t in range(NTL):
            cs = slice(t * TL, (t + 1) * TL)
            qs.append(_dot(xcb[:, cs], wq_ref[t]))
            ks.append(_dot(xcb[:, cs], wk_ref[t]) * kscale)
            vs.append(_dot(xb[:, cs], wv_ref[t]))
        q = jnp.concatenate(qs, axis=1)
        k = jnp.concatenate(ks, axis=1)
        v = jnp.concatenate(vs, axis=1)
        q_ref[...] = q.astype(q_ref.dtype)
        k_ref[...] = k.astype(k_ref.dtype)
        v_ref[...] = v.astype(v_ref.dtype)
        gt = (_dot(q.astype(bf16), wg_ref[0:W, :]) + _dot(k.astype(bf16), wg_ref[W:2 * W, :])
              + _dot(v.astype(bf16), wg_ref[2 * W:3 * W, :]) + bg_ref[...])
        lane = lax.broadcasted_iota(jnp.int32, gt.shape, 1)
        g_ref[...] = jnp.where(jnp.logical_and(lane >= H, lane < 2 * H), jax.nn.log_sigmoid(gt), gt)

    row = lambda i: (i, 0)
    const2 = lambda i: (0, 0)
    const3 = lambda i: (0, 0, 0)
    in_specs = [pl.BlockSpec((tr, W), row)]
    args = [p1]
    if sample:
        in_specs += [pl.BlockSpec((tr, W), row)] * 3
        args += [conv_state[:, 0], conv_state[:, 1], conv_state[:, 2]]
    else:
        in_specs.append(pl.BlockSpec((SUBLANES, W), lambda i: (jnp.maximum(i * (tr // SUBLANES) - 1, 0), 0)))
        args.append(p1)
    in_specs += [pl.BlockSpec((4, W), const2), pl.BlockSpec((1, W), const2),
                 pl.BlockSpec((NTL, TL, TL), const3), pl.BlockSpec((NTL, TL, TL), const3),
                 pl.BlockSpec((NTL, TL, TL), const3),
                 pl.BlockSpec((3 * W, LANES), const2), pl.BlockSpec((1, LANES), const2)]
    args += [conv_w, conv_b.reshape(1, W), wq_t, wk_t, wv_t, wg, bg]
    return pl.pallas_call(
        body,
        grid=(M // tr,),
        in_specs=in_specs,
        out_specs=[pl.BlockSpec((tr, W), row)] * 3 + [pl.BlockSpec((tr, LANES), row), pl.BlockSpec((tr, W), row)],
        out_shape=[jax.ShapeDtypeStruct((M, W), f32 if sample else bf16)] * 3
        + [jax.ShapeDtypeStruct((M, LANES), f32), jax.ShapeDtypeStruct((M, W), f32)],
        scratch_shapes=[] if sample else [pltpu.VMEM((tr + SUBLANES, W), f32)],
        compiler_params=_params(("parallel",), 48),
        name="mlstm_stage1_sample" if sample else "mlstm_stage1",
    )(*args)


def _mlstm_prompt(q, k, v, gates, xc, p1, g_norm, skip, *, B, L, H, DH, tb):
    NT = L // tb
    NC = tb // CHUNK
    C = CHUNK

    def body(q_ref, k_ref, v_ref, g_ref, xc_ref, om_ref, gn_ref, sk_ref,
             y_ref, c_out, n_out, m_out, c_scr, n_scr, m_scr):
        hh = pl.program_id(1)
        t = pl.program_id(2)

        @pl.when(t == 0)
        def _():
            c_scr[...] = jnp.zeros_like(c_scr)
            n_scr[...] = jnp.zeros_like(n_scr)
            m_scr[...] = jnp.zeros_like(m_scr)

        gts = g_ref[...]
        lane = lax.broadcasted_iota(jnp.int32, (tb, LANES), 1)
        i_col = jnp.sum(jnp.where(lane == hh, gts, 0.0), axis=1, keepdims=True)
        b_col = jnp.sum(jnp.where(lane == H + hh, _chunk_cumsum(gts, C), 0.0), axis=1, keepdims=True)
        b3 = b_col.reshape(NC, C, 1)
        i3 = i_col.reshape(NC, C, 1)
        ii = lax.broadcasted_iota(jnp.int32, (1, C, C), 1)
        jj = lax.broadcasted_iota(jnp.int32, (1, C, C), 2)
        eye = ii == jj
        causal = ii >= jj
        as_row = lambda col3: jnp.sum(jnp.where(eye, jnp.broadcast_to(col3, (NC, C, C)), 0.0), axis=1, keepdims=True)
        dmat = jnp.where(causal, b3 - as_row(b3) + as_row(i3), -jnp.inf)
        rmax = jnp.max(dmat, axis=2, keepdims=True)
        b_last = b3[:, C - 1:C, :]
        m_prev = m_scr[...]
        m_in = []
        for c in range(NC):
            m_in.append(m_prev)
            m_prev = jnp.maximum(b_last[c] + m_prev, rmax[c][C - 1:C, :])
        m_scr[...] = m_prev
        inter = b3 + jnp.stack(m_in)
        m_col = jnp.maximum(inter, rmax)
        g_col = jnp.exp(inter - m_col)
        q3 = q_ref[...].reshape(NC, C, DH)
        k3 = k_ref[...].reshape(NC, C, DH)
        v3 = v_ref[...].reshape(NC, C, DH)
        s = jnp.einsum('cid,cjd->cij', q3, k3, preferred_element_type=f32) * jnp.exp(dmat - m_col)
        num = jnp.einsum('cij,cje->cie', s.astype(bf16), v3, preferred_element_type=f32)
        den = jnp.sum(s, axis=2, keepdims=True)
        m_new = m_col[:, C - 1:C, :]
        wk = jnp.exp(b_last - b3 + i3 - m_new)
        gc = jnp.exp(inter[:, C - 1:C, :] - m_new)
        kw = k3.astype(f32) * wk
        dc = jnp.einsum('cse,csd->ced', v3, kw.astype(bf16), preferred_element_type=f32)
        dn = jnp.sum(kw, axis=1, keepdims=True)
        cs = c_scr[...]
        ns = n_scr[...]
        c_in, n_in = [], []
        for c in range(NC):
            c_in.append(cs)
            n_in.append(ns)
            cs = gc[c] * cs + dc[c]
            ns = gc[c] * ns + dn[c]
        c_scr[...] = cs
        n_scr[...] = ns
        c_all = jnp.stack(c_in).astype(bf16)
        n_all = jnp.stack(n_in)
        num = num + g_col * jnp.einsum('cid,ced->cie', q3, c_all, preferred_element_type=f32)
        den = den + g_col * jnp.sum(q3.astype(f32) * n_all, axis=2, keepdims=True)
        hm = (num / jnp.maximum(jnp.abs(den), jnp.exp(-m_col))).reshape(tb, DH)
        y = (_rms(hm, gn_ref[...]) + sk_ref[...] * xc_ref[...]) * jax.nn.sigmoid(om_ref[...])
        y_ref[...] = y.astype(bf16)

        @pl.when(t == NT - 1)
        def _():
            c_out[0, 0] = cs
            n_out[0, 0] = ns
            m_out[0, 0] = m_prev

    blk = lambda b, h, t: (b * NT + t, h)
    W = H * DH
    y, c_new, n_new, m_new = pl.pallas_call(
        body,
        grid=(B, H, NT),
        in_specs=[pl.BlockSpec((tb, DH), blk), pl.BlockSpec((tb, DH), blk), pl.BlockSpec((tb, DH), blk),
                  pl.BlockSpec((tb, LANES), lambda b, h, t: (b * NT + t, 0)),
                  pl.BlockSpec((tb, DH), blk),
                  pl.BlockSpec((tb, DH), lambda b, h, t: (b * NT + t, H + h)),
                  pl.BlockSpec((1, DH), lambda b, h, t: (0, h)),
                  pl.BlockSpec((1, DH), lambda b, h, t: (0, h))],
        out_specs=[pl.BlockSpec((tb, DH), blk),
                   pl.BlockSpec((1, 1, DH, DH), lambda b, h, t: (b, h, 0, 0)),
                   pl.BlockSpec((1, 1, 1, DH), lambda b, h, t: (b, h, 0, 0)),
                   pl.BlockSpec((1, 1, 1, 1), lambda b, h, t: (b, h, 0, 0))],
        out_shape=[jax.ShapeDtypeStruct((B * L, W), bf16),
                   jax.ShapeDtypeStruct((B, H, DH, DH), f32),
                   jax.ShapeDtypeStruct((B, H, 1, DH), f32),
                   jax.ShapeDtypeStruct((B, H, 1, 1), f32)],
        scratch_shapes=[pltpu.VMEM((DH, DH), f32), pltpu.VMEM((1, DH), f32), pltpu.VMEM((1, 1), f32)],
        compiler_params=_params(("parallel", "parallel", "arbitrary"), 32),
        name="mlstm_prompt",
    )(q, k, v, gates, xc, p1, g_norm.reshape(1, W), skip.reshape(1, W))
    return y, c_new, n_new.reshape(B, H, DH), m_new.reshape(B, H)


def _mlstm_sample(q, k, v, gates, xc, p1, g_norm, skip, c0, n0, m0, *, H, DH):
    Bs = q.shape[0]
    W = H * DH
    SB = SUBLANES

    def body(q_ref, k_ref, v_ref, g_ref, xc_ref, om_ref, gn_ref, sk_ref, c_ref, n_ref, m_ref,
             y_ref, c_out, n_out, m_out):
        gts = g_ref[...]
        for h in range(H):
            cs = slice(h * DH, (h + 1) * DH)
            ig = gts[:, h:h + 1]
            fg = gts[:, H + h:H + h + 1]
            inter = fg + m_ref[:, h:h + 1]
            m = jnp.maximum(inter, ig)
            g = jnp.exp(inter - m)
            m_out[:, h:h + 1] = m
            q = q_ref[:, cs]
            kw = k_ref[:, cs] * jnp.exp(ig - m)
            nn = g * n_ref[:, cs] + kw
            n_out[:, cs] = nn
            den = jnp.sum(nn * q, axis=1, keepdims=True)
            v_t = v_ref[:, cs].T
            qb = q.astype(bf16)
            nums = []
            for s in range(SB):
                cn = g[s:s + 1, :] * c_ref[s, h] + v_t[:, s:s + 1] * kw[s:s + 1, :]
                c_out[s, h] = cn
                nums.append(_dot_nt(qb, cn.astype(bf16))[s:s + 1, :])
            num = jnp.concatenate(nums, axis=0)
            hm = num / jnp.maximum(jnp.abs(den), jnp.exp(-m))
            y_ref[:, cs] = ((_rms(hm, gn_ref[:, cs]) + sk_ref[:, cs] * xc_ref[:, cs])
                            * jax.nn.sigmoid(om_ref[:, cs]))

    per = lambda b: (b, 0)
    const2 = lambda b: (0, 0)
    y, c_new, n_new, m_new = pl.pallas_call(
        body,
        grid=(Bs // SB,),
        in_specs=[pl.BlockSpec((SB, W), per), pl.BlockSpec((SB, W), per), pl.BlockSpec((SB, W), per),
                  pl.BlockSpec((SB, LANES), per), pl.BlockSpec((SB, W), per),
                  pl.BlockSpec((SB, W), lambda b: (b, 1)),
                  pl.BlockSpec((1, W), const2), pl.BlockSpec((1, W), const2),
                  pl.BlockSpec((SB, H, DH, DH), lambda b: (b, 0, 0, 0)),
                  pl.BlockSpec((SB, W), per), pl.BlockSpec((SB, H), per)],
        out_specs=[pl.BlockSpec((SB, W), per),
                   pl.BlockSpec((SB, H, DH, DH), lambda b: (b, 0, 0, 0)),
                   pl.BlockSpec((SB, W), per), pl.BlockSpec((SB, H), per)],
        out_shape=[jax.ShapeDtypeStruct((Bs, W), f32), jax.ShapeDtypeStruct((Bs, H, DH, DH), f32),
                   jax.ShapeDtypeStruct((Bs, W), f32), jax.ShapeDtypeStruct((Bs, H), f32)],
        compiler_params=_params(("parallel",), 48),
        name="mlstm_sample",
    )(q, k, v, gates, xc, p1, g_norm.reshape(1, W), skip.reshape(1, W), c0, n0.reshape(Bs, W), m0)
    return y, c_new, n_new.reshape(Bs, H, DH), m_new


S5_TILES = 8


def _s5_layouts(lam_re, lam_im, log_dt, b_re, b_im, c_re, c_im):
    G, P = lam_re.shape
    GC = b_re.shape[2]
    T = S5_TILES
    gpt = G // T
    ns = G * P
    flat = lambda a: a.reshape(ns)
    ldt = jnp.broadcast_to(log_dt[:, None], (G, P))
    rows = [flat(a).reshape(T, 1, ns // T) for a in (lam_re, lam_im, ldt)]
    eye = jnp.eye(gpt, dtype=f32)
    bbd = [jnp.einsum('jgpc,gh->jgchp', a.reshape(T, gpt, P, GC), eye).reshape(T, gpt * GC, gpt * P) for a in (b_re, b_im)]
    cbd = [jnp.einsum('jgcp,gh->jgphc', a.reshape(T, gpt, GC, P), eye).reshape(T, gpt * P, gpt * GC) for a in (c_re, c_im)]
    return rows, bbd, cbd


def _s5_discretise(lre, lim, ldt):
    dt = jnp.exp(ldt)
    mag = jnp.exp(dt * lre)
    ar = mag * jnp.cos(dt * lim)
    ai = mag * jnp.sin(dt * lim)
    den = lre * lre + lim * lim
    cr = ((ar - 1.0) * lre + ai * lim) / den
    ci = (ai * lre - (ar - 1.0) * lim) / den
    return ar, ai, cr, ci


def _s5_mixer(p1, u_blk, rows, bbd, cbd, d_skip, w_glu, b_glu, *, B, L, tb, state=None):
    T = S5_TILES
    lre_r, lim_r, ldt_r = rows
    SW = lre_r.shape[2]
    CW = bbd[0].shape[1]
    W = T * CW
    NS = T * SW
    KT = SW // LANES
    sample = state is not None
    NT = 1 if sample else L // tb
    M = B * L

    def body(*refs):
        it = iter(refs)
        u_ref = next(it)
        if sample:
            x0r_ref, x0i_ref = next(it), next(it)
        lre_ref, lim_ref, ldt_ref = next(it), next(it), next(it)
        bre_ref, bim_ref, cre_ref, cim_ref = next(it), next(it), next(it), next(it)
        d_ref, wg_ref, bgl_ref = next(it), next(it), next(it)
        y_ref, xr_out, xi_out = next(it), next(it), next(it)
        bbr, bbi, cpair, ar_scr, ai_scr = (next(it) for _ in range(5))
        if not sample:
            sre, sim, xr_c, xi_c, yacc = (next(it) for _ in range(5))
        first = jnp.logical_and(pl.program_id(0) == 0, pl.program_id(1) == 0)

        @pl.when(first)
        def _():
            for j in range(T):
                ar, ai, cr, ci = _s5_discretise(lre_ref[j], lim_ref[j], ldt_ref[j])
                ar_scr[j] = ar
                ai_scr[j] = ai
                br = bre_ref[j]
                bi = bim_ref[j]
                bbr[j] = (cr * br - ci * bi).astype(bf16)
                bbi[j] = (cr * bi + ci * br).astype(bf16)
            for jp in range(T // 2):
                cpair[jp] = jnp.zeros((4 * SW, 2 * CW), bf16)
                for half in range(2):
                    j = 2 * jp + half
                    r0 = 2 * half * SW
                    cols = slice(half * CW, (half + 1) * CW)
                    cpair[jp, r0:r0 + SW, cols] = cre_ref[j].astype(bf16)
                    cpair[jp, r0 + SW:r0 + 2 * SW, cols] = (-cim_ref[j]).astype(bf16)

        u = u_ref[...]
        ub = u.astype(bf16)
        ys = []
        if sample:
            for j in range(T):
                cs = slice(j * SW, (j + 1) * SW)
                uj = ub[:, j * CW:(j + 1) * CW]
                ar = ar_scr[j]
                ai = ai_scr[j]
                x0r = x0r_ref[:, cs]
                x0i = x0i_ref[:, cs]
                xr = ar * x0r - ai * x0i + _dot(uj, bbr[j])
                xi = ar * x0i + ai * x0r + _dot(uj, bbi[j])
                xr_out[:, cs] = xr
                xi_out[:, cs] = xi
                ys += [xr.astype(bf16), xi.astype(bf16)]
            y = jnp.concatenate([_dot(jnp.concatenate(ys[4 * jp:4 * jp + 4], axis=1), cpair[jp]) for jp in range(T // 2)],
                                axis=1)
        else:
            t = pl.program_id(1)

            @pl.when(t == 0)
            def _():
                xr_c[...] = jnp.zeros_like(xr_c)
                xi_c[...] = jnp.zeros_like(xi_c)

            for j in range(T):
                uj = ub[:, j * CW:(j + 1) * CW]
                r = _dot(uj, bbr[j])
                im = _dot(uj, bbi[j])
                for kk in range(KT):
                    sre[kk, pl.ds(j, tb, stride=T), :] = r[:, kk * LANES:(kk + 1) * LANES]
                    sim[kk, pl.ds(j, tb, stride=T), :] = im[:, kk * LANES:(kk + 1) * LANES]
            a_r = [jnp.concatenate([ar_scr[j][:, kk * LANES:(kk + 1) * LANES] for j in range(T)], axis=0) for kk in range(KT)]
            a_i = [jnp.concatenate([ai_scr[j][:, kk * LANES:(kk + 1) * LANES] for j in range(T)], axis=0) for kk in range(KT)]

            def step(s, carry):
                xr, xi = carry
                row = pl.ds(pl.multiple_of(s * T, T), T)
                nr, ni = [], []
                for kk in range(KT):
                    r_ = a_r[kk] * xr[kk] - a_i[kk] * xi[kk] + sre[kk, row, :]
                    i_ = a_r[kk] * xi[kk] + a_i[kk] * xr[kk] + sim[kk, row, :]
                    sre[kk, row, :] = r_
                    sim[kk, row, :] = i_
                    nr.append(r_)
                    ni.append(i_)
                return tuple(nr), tuple(ni)
            xr0 = tuple(xr_c[kk] for kk in range(KT))
            xi0 = tuple(xi_c[kk] for kk in range(KT))
            xr, xi = lax.fori_loop(0, tb, step, (xr0, xi0), unroll=4)
            for kk in range(KT):
                xr_c[kk] = xr[kk]
                xi_c[kk] = xi[kk]

            @pl.when(t == NT - 1)
            def _():
                for kk in range(KT):
                    xr_out[kk] = xr[kk]
                    xi_out[kk] = xi[kk]

            for jp in range(T // 2):
                parts = []
                for j in (2 * jp, 2 * jp + 1):
                    parts += [sre[kk, pl.ds(j, tb, stride=T), :] for kk in range(KT)]
                    parts += [sim[kk, pl.ds(j, tb, stride=T), :] for kk in range(KT)]
                xp = jnp.concatenate(parts, axis=1).astype(bf16)
                yacc[:, 2 * jp * CW:(2 * jp + 2) * CW] = _dot(xp, cpair[jp])
            y = yacc[...]
        ysk = jax.nn.gelu(y + d_ref[...] * u)
        z = _dot(ysk.astype(bf16), wg_ref[...]) + bgl_ref[...]
        y_ref[...] = (ysk * jax.nn.sigmoid(z)).astype(bf16)

    c3 = lambda b, t: (0, 0, 0)
    c2 = lambda b, t: (0, 0)
    in_specs = [pl.BlockSpec((tb, W), lambda b, t: (b * NT + t, u_blk))]
    args = [p1]
    if sample:
        in_specs += [pl.BlockSpec((tb, NS), lambda b, t: (b, 0))] * 2
        args += [state[0], state[1]]
    in_specs += [pl.BlockSpec((T, 1, SW), c3)] * 3
    in_specs += [pl.BlockSpec((T, CW, SW), c3)] * 2 + [pl.BlockSpec((T, SW, CW), c3)] * 2
    in_specs += [pl.BlockSpec((1, W), c2), pl.BlockSpec((W, W), c2), pl.BlockSpec((1, W), c2)]
    args += [lre_r, lim_r, ldt_r, bbd[0], bbd[1], cbd[0], cbd[1], d_skip.reshape(1, W), w_glu, b_glu.reshape(1, W)]
    scratch = [pltpu.VMEM((T, CW, SW), bf16), pltpu.VMEM((T, CW, SW), bf16),
               pltpu.VMEM((T // 2, 4 * SW, 2 * CW), bf16),
               pltpu.VMEM((T, 1, SW), f32), pltpu.VMEM((T, 1, SW), f32)]
    if sample:
        grid = (M // tb, 1)
        st_spec = pl.BlockSpec((tb, NS), lambda b, t: (b, 0))
        st_shape = jax.ShapeDtypeStruct((M, NS), f32)
    else:
        grid = (B, NT)
        st_spec = pl.BlockSpec((None, KT, T, LANES), lambda b, t: (b, 0, 0, 0))
        st_shape = jax.ShapeDtypeStruct((B, KT, T, LANES), f32)
        scratch += [pltpu.VMEM((KT, tb * T, LANES), f32), pltpu.VMEM((KT, tb * T, LANES), f32),
                    pltpu.VMEM((KT, T, LANES), f32), pltpu.VMEM((KT, T, LANES), f32), pltpu.VMEM((tb, W), f32)]
    y, xr, xi = pl.pallas_call(
        body,
        grid=grid,
        in_specs=in_specs,
        out_specs=[pl.BlockSpec((tb, W), lambda b, t: (b * NT + t, 0)), st_spec, st_spec],
        out_shape=[jax.ShapeDtypeStruct((M, W), bf16), st_shape, st_shape],
        scratch_shapes=scratch,
        compiler_params=_params(("arbitrary", "arbitrary"), 56),
        name="s5_sample" if sample else "s5_prompt",
    )(*args)
    if not sample:
        xr = jnp.transpose(xr, (0, 2, 1, 3)).reshape(B, NS)
        xi = jnp.transpose(xi, (0, 2, 1, 3)).reshape(B, NS)
    return y, xr, xi


def _trunk(x3, st, w, *, sample):
    B, L, D = x3.shape
    M = B * L
    x = x3.reshape(M, D)
    H_g, DK, DV = st['gla_S'].shape[1:] if sample else w['gla_dims']
    H_m, DH = w['ml_dims']
    G, P = w['s5_dims']
    tm = M if sample else min(1024, L)
    tf = M if sample else min(1024, L)
    to = M if sample else min(2048, M)
    tb = min(1024, L)
    out = {}

    p0, alr = _norm_matmul(x, w['g_mix0'], w['w_in0_main'], tm=tm, tn=1024, w_side=w['w_in0_alr'])
    x_blk = (2 * H_g * DK + 2 * H_g * DV) // LANES
    W_rg = w['rg_lambda'].shape[0]
    g_blk = x_blk + W_rg // LANES
    rg_tail = p0.reshape(B, L, p0.shape[1])[:, max(L - 3, 0):, x_blk * LANES:x_blk * LANES + W_rg]
    if sample:
        ya, out['gla_S'] = _gla_sample(p0, alr, w['gla_w2p'], w['gla_b_alpha'], w['gla_g_norm'], st['gla_S'],
                                       H=H_g, DK=DK, DV=DV)
        yb, out['rg_h'] = _rglru_sample(p0, st['rg_conv'], st['rg_h'], w['rg_conv_w'], w['rg_conv_b'], w['rg_w_r'],
                                        w['rg_b_r'], w['rg_w_i'], w['rg_b_i'], w['rg_lambda'], x_blk=x_blk, g_blk=g_blk)
        out['rg_conv'] = jnp.concatenate([st['rg_conv'][:, 1:], rg_tail], axis=1)
    else:
        ya, out['gla_S'] = _gla_prompt(p0, alr, w['gla_w2p'], w['gla_b_alpha'], w['gla_g_norm'],
                                       B=B, L=L, H=H_g, DK=DK, DV=DV, tb=tb)
        yb, out['rg_h'] = _rglru_prompt(p0, w['rg_conv_w'], w['rg_conv_b'], w['rg_w_r'], w['rg_b_r'], w['rg_w_i'],
                                        w['rg_b_i'], w['rg_lambda'], B=B, L=L, x_blk=x_blk, g_blk=g_blk)
        out['rg_conv'] = rg_tail
    x = _out_proj(ya, yb, w['w_out0'], x, tm=to, tn=512)
    x, ffn0 = _conv_ffn(x, w['g_ffn'][0], w['ffn'][0], seq_len=L, tm=tf, prev=st['ffn_conv'] if sample else None)

    W_ml = H_m * DH
    p1 = _norm_matmul(x, w['g_mix1'], [w['w_in1']], tm=tm, tn=1024)
    xm_tail = p1.reshape(B, L, p1.shape[1])[:, max(L - 3, 0):, :W_ml]
    q, k, v, gates, xc = _mlstm_stage1(p1, w['ml_conv_w'], w['ml_conv_b'], w['ml_wq_t'], w['ml_wk_t'], w['ml_wv_t'],
                                       w['ml_wg'], w['ml_bg'], seq_len=L, tr=M if sample else min(256, L), H=H_m, DH=DH,
                                       conv_state=st['ml_conv'] if sample else None)
    u_blk = 2 * W_ml // (G * w['s5_gc'])
    if sample:
        yc, out['ml_C'], out['ml_n'], out['ml_m'] = _mlstm_sample(
            q, k, v, gates, xc, p1, w['ml_g_norm'], w['ml_skip'], st['ml_C'], st['ml_n'], st['ml_m'], H=H_m, DH=DH)
        out['ml_conv'] = jnp.concatenate([st['ml_conv'][:, 1:], xm_tail], axis=1)
        yd, s5r, s5i = _s5_mixer(p1, u_blk, w['s5_rows'], w['s5_bbd'], w['s5_cbd'], w['s5_D'], w['s5_w_glu'],
                                 w['s5_b_glu'], B=B, L=1, tb=B,
                                 state=(st['s5_re'].reshape(B, G * P), st['s5_im'].reshape(B, G * P)))
    else:
        yc, out['ml_C'], out['ml_n'], out['ml_m'] = _mlstm_prompt(
            q, k, v, gates, xc, p1, w['ml_g_norm'], w['ml_skip'], B=B, L=L, H=H_m, DH=DH, tb=tb)
        out['ml_conv'] = xm_tail
        yd, s5r, s5i = _s5_mixer(p1, u_blk, w['s5_rows'], w['s5_bbd'], w['s5_cbd'], w['s5_D'], w['s5_w_glu'],
                                 w['s5_b_glu'], B=B, L=L, tb=min(512, L))
    out['s5_re'] = s5r.reshape(B, G, P)
    out['s5_im'] = s5i.reshape(B, G, P)
    x = _out_proj(yc, yd, w['w_out1'], x, tm=to, tn=512)
    x, ffn1 = _conv_ffn(x, w['g_ffn'][1], w['ffn'][1], seq_len=L, tm=tf, prev=st['ffn_conv'] if sample else None,
                        final_g=w['g_final'])
    out['ffn_conv'] = jnp.stack([ffn0, ffn1], axis=0)
    return x.reshape(B, L, D), out


def kernel(x_prompt, x_sample, state_gla_S, state_rglru_h, state_rglru_conv, state_mlstm_C, state_mlstm_n, state_mlstm_m, state_mlstm_conv, state_s5_re, state_s5_im, state_ffn_conv, g_mix0, w_in0, gla_w_alpha2, gla_b_alpha, gla_g_norm, rg_conv_w, rg_conv_b, rg_w_r, rg_b_r, rg_w_i, rg_b_i, rg_lambda, w_out0, g_mix1, w_in1, ml_conv_w, ml_conv_b, ml_wq, ml_wk, ml_wv, ml_w_igate, ml_b_igate, ml_w_fgate, ml_b_fgate, ml_g_norm, ml_skip, s5_lam_re, s5_lam_im, s5_log_dt, s5_B_re, s5_B_im, s5_C_re, s5_C_im, s5_D, s5_w_glu, s5_b_glu, w_out1, g_ffn, ffn_w_up, ffn_conv_w, ffn_conv_b, ffn_w_down, g_final):
    _, H_g, DK, DV = state_gla_S.shape
    _, H_m, DH, _ = state_mlstm_C.shape
    G, P = s5_lam_re.shape
    rank = gla_w_alpha2.shape[0]
    n_main = 2 * H_g * DK + 2 * H_g * DV
    w_in0_main = [w_in0[:, :n_main].astype(bf16), w_in0[:, n_main + rank:].astype(bf16)]
    w_in0_alr = jnp.pad(w_in0[:, n_main:n_main + rank], ((0, 0), (0, LANES - rank))).astype(bf16)
    gla_w2p = jnp.pad(gla_w_alpha2, ((0, LANES - rank), (0, 0)))
    ml_tile = 256
    ml_wg = jnp.pad(jnp.concatenate([ml_w_igate, ml_w_fgate], axis=1), ((0, 0), (0, LANES - 2 * H_m))).astype(bf16)
    ml_bg = jnp.pad(jnp.concatenate([ml_b_igate, ml_b_fgate]), (0, LANES - 2 * H_m)).reshape(1, LANES)
    rows, bbd, cbd = _s5_layouts(s5_lam_re, s5_lam_im, s5_log_dt, s5_B_re, s5_B_im, s5_C_re, s5_C_im)
    w = dict(
        g_mix0=g_mix0, w_in0_main=w_in0_main, w_in0_alr=w_in0_alr, gla_w2p=gla_w2p, gla_b_alpha=gla_b_alpha,
        gla_g_norm=gla_g_norm, gla_dims=(H_g, DK, DV), rg_conv_w=rg_conv_w, rg_conv_b=rg_conv_b, rg_w_r=rg_w_r,
        rg_b_r=rg_b_r, rg_w_i=rg_w_i, rg_b_i=rg_b_i, rg_lambda=rg_lambda, w_out0=w_out0.astype(bf16),
        g_mix1=g_mix1, w_in1=w_in1.astype(bf16), ml_conv_w=ml_conv_w, ml_conv_b=ml_conv_b,
        ml_wq_t=_blockdiag_tiles(ml_wq, ml_tile).astype(bf16), ml_wk_t=_blockdiag_tiles(ml_wk, ml_tile).astype(bf16),
        ml_wv_t=_blockdiag_tiles(ml_wv, ml_tile).astype(bf16), ml_wg=ml_wg, ml_bg=ml_bg, ml_g_norm=ml_g_norm,
        ml_skip=ml_skip, ml_dims=(H_m, DH), s5_dims=(G, P), s5_gc=s5_B_re.shape[2], s5_rows=rows, s5_bbd=bbd,
        s5_cbd=cbd, s5_D=s5_D, s5_w_glu=s5_w_glu.astype(bf16), s5_b_glu=s5_b_glu, w_out1=w_out1.astype(bf16),
        g_ffn=g_ffn, g_final=g_final,
        ffn=_ffn_prepare(ffn_w_up, ffn_conv_w, ffn_conv_b, ffn_w_down))
    st_s = dict(gla_S=state_gla_S, rg_h=state_rglru_h, rg_conv=state_rglru_conv, ml_C=state_mlstm_C,
                ml_n=state_mlstm_n, ml_m=state_mlstm_m, ml_conv=state_mlstm_conv, s5_re=state_s5_re,
                s5_im=state_s5_im, ffn_conv=state_ffn_conv)
    y_p, np_ = _trunk(x_prompt, None, w, sample=False)
    y_s, ns_ = _trunk(x_sample, st_s, w, sample=True)
    names = ('gla_S', 'rg_h', 'rg_conv', 'ml_C', 'ml_n', 'ml_m', 'ml_conv', 's5_re', 's5_im', 'ffn_conv')
    outs = [y_p, y_s]
    for nme in names:
        outs += [np_[nme], ns_[nme]]
    return tuple(outs)
```

```python
import functools

import jax
import jax.numpy as jnp
from jax import lax
from jax.experimental import pallas as pl
from jax.experimental.pallas import tpu as pltpu

f32 = jnp.float32
bf16 = jnp.bfloat16

EPS = 1e-6
CHUNK = 64
GLA_TAU = 16.0
RG_C = 8.0
PAST_LEN = 16384
LANES = 128
SUBLANES = 8
HALO = 16
MIB = 1024 * 1024


def _params(sem, vmem_mib):
    return pltpu.CompilerParams(dimension_semantics=sem, vmem_limit_bytes=int(vmem_mib * MIB))


def _dot(a, b):
    return jnp.dot(a, b, preferred_element_type=f32)


def _dot_nt(a, b):
    return lax.dot_general(a, b, (((1,), (1,)), ((), ())), preferred_element_type=f32)


def _dot_tn(a, b):
    return lax.dot_general(a, b, (((0,), (0,)), ((), ())), preferred_element_type=f32)


def _rms(x, g):
    return x * lax.rsqrt(jnp.mean(x * x, axis=-1, keepdims=True) + EPS) * g


def _eye(n):
    return lax.broadcasted_iota(jnp.int32, (n, n), 0) == lax.broadcasted_iota(jnp.int32, (n, n), 1)


def _col_from_row(row):
    n = row.shape[1]
    return jnp.sum(jnp.where(_eye(n), jnp.broadcast_to(row, (n, n)), 0.0), axis=1, keepdims=True)


def _chunk_cumsum(x, chunk):
    pos = lax.broadcasted_iota(jnp.int32, (x.shape[0], 1), 0) % chunk
    step = 1
    while step < chunk:
        x = x + jnp.where(pos >= step, pltpu.roll(x, step, 0), 0.0)
        step *= 2
    return x


def _norm_matmul(x, g, ws, *, tm, tn, w_side=None, out_dtype=f32):
    M, D = x.shape
    nblk = [w.shape[1] // tn for w in ws]
    start = [sum(nblk[:s]) for s in range(len(ws) + 1)]
    N = tn * start[-1]
    rc = min(tm, 256)
    side = w_side is not None

    def body(*refs):
        x_ref, g_ref = refs[:2]
        w_refs = refs[2:2 + len(ws)]
        rest = refs[2 + len(ws):]
        if side:
            ws_ref, o_ref, os_ref, xn_ref = rest
        else:
            o_ref, xn_ref = rest
        j = pl.program_id(1)

        @pl.when(j == 0)
        def _():
            def chunk(r, c):
                rows = pl.ds(pl.multiple_of(r * rc, rc), rc)
                xn_ref[rows, :] = _rms(x_ref[rows, :], g_ref[...]).astype(bf16)
                return c
            lax.fori_loop(0, tm // rc, chunk, 0)
            if side:
                os_ref[...] = _dot(xn_ref[...], ws_ref[...])

        if len(ws) == 1:
            o_ref[...] = _dot(xn_ref[...], w_refs[0][...]).astype(out_dtype)
        else:
            for s, w_ref in enumerate(w_refs):
                @pl.when(jnp.logical_and(j >= start[s], j < start[s + 1]))
                def _(w_ref=w_ref):
                    o_ref[...] = _dot(xn_ref[...], w_ref[...]).astype(out_dtype)

    in_specs = [pl.BlockSpec((tm, D), lambda i, j: (i, 0)),
                pl.BlockSpec((1, D), lambda i, j: (0, 0))]
    in_specs += [pl.BlockSpec((D, tn), lambda i, j, s=s: (0, jnp.clip(j - start[s], 0, nblk[s] - 1)))
                 for s in range(len(ws))]
    out_specs = [pl.BlockSpec((tm, tn), lambda i, j: (i, j))]
    out_shape = [jax.ShapeDtypeStruct((M, N), out_dtype)]
    args = [x, g.reshape(1, D)] + list(ws)
    if side:
        ns = w_side.shape[1]
        in_specs.append(pl.BlockSpec((D, ns), lambda i, j: (0, 0)))
        out_specs.append(pl.BlockSpec((tm, ns), lambda i, j: (i, 0)))
        out_shape.append(jax.ShapeDtypeStruct((M, ns), f32))
        args.append(w_side)
    outs = pl.pallas_call(
        body,
        grid=(M // tm, N // tn),
        in_specs=in_specs,
        out_specs=out_specs,
        out_shape=out_shape,
        scratch_shapes=[pltpu.VMEM((tm, D), bf16)],
        compiler_params=_params(("parallel", "arbitrary"), 56),
        name="norm_matmul",
    )(*args)
    return outs if side else outs[0]


def _out_proj(ya, yb, w, res, *, tm, tn):
    M, Ka = ya.shape
    Kb = yb.shape[1]
    N = w.shape[1]
    assert Ka == Kb and w.shape[0] == Ka + Kb

    def body(ya_ref, yb_ref, wa_ref, wb_ref, r_ref, o_ref):
        o_ref[...] = (r_ref[...] + _dot(ya_ref[...].astype(bf16), wa_ref[...])
                      + _dot(yb_ref[...].astype(bf16), wb_ref[...]))

    return pl.pallas_call(
        body,
        grid=(M // tm, N // tn),
        in_specs=[pl.BlockSpec((tm, Ka), lambda i, j: (i, 0)),
                  pl.BlockSpec((tm, Kb), lambda i, j: (i, 0)),
                  pl.BlockSpec((Ka, tn), lambda i, j: (0, j)),
                  pl.BlockSpec((Kb, tn), lambda i, j: (1, j)),
                  pl.BlockSpec((tm, tn), lambda i, j: (i, j))],
        out_specs=pl.BlockSpec((tm, tn), lambda i, j: (i, j)),
        out_shape=jax.ShapeDtypeStruct((M, N), f32),
        compiler_params=_params(("parallel", "arbitrary"), 48),
        name="out_proj",
    )(ya, yb, w, w, res)


FFN_STEP = 512


def _ffn_to_steps(a, F):
    nsteps = -(-F // FFN_STEP)
    lead = a.shape[:-1]
    nd = len(lead)
    gv = a.reshape(lead + (2, F))
    gv = jnp.pad(gv, [(0, 0)] * (nd + 1) + [(0, nsteps * FFN_STEP - F)])
    gv = gv.reshape(lead + (2, nsteps, FFN_STEP))
    gv = jnp.transpose(gv, (nd + 1,) + tuple(range(nd)) + (nd, nd + 2))
    return gv.reshape((nsteps,) + lead + (2 * FFN_STEP,))


def _ffn_prepare(w_up, conv_w, conv_b, w_down):
    NL, F, D = w_down.shape
    assert F % LANES == 0 and FFN_STEP % LANES == 0
    nt = F // LANES
    tp = FFN_STEP // LANES
    nsteps = -(-F // FFN_STEP)

    def body(*refs):
        g_in, v_in, d_in = refs[:tp], refs[tp:2 * tp], refs[2 * tp:3 * tp]
        wu_o, wd_o = refs[3 * tp:]
        for t in range(tp):
            keep = tp * pl.program_id(1) + t < nt
            cols = slice(t * LANES, (t + 1) * LANES)
            wu_o[:, cols] = jnp.where(keep, g_in[t][...], 0.0).astype(bf16)
            wu_o[:, FFN_STEP + t * LANES:FFN_STEP + (t + 1) * LANES] = jnp.where(keep, v_in[t][...], 0.0).astype(bf16)
            wd_o[cols, :] = jnp.where(keep, d_in[t][...], 0.0).astype(bf16)

    tile = lambda j, t: jnp.minimum(tp * j + t, nt - 1)
    in_specs = ([pl.BlockSpec((None, D, LANES), lambda l, j, t=t: (l, 0, tile(j, t))) for t in range(tp)]
                + [pl.BlockSpec((None, D, LANES), lambda l, j, t=t: (l, 0, nt + tile(j, t))) for t in range(tp)]
                + [pl.BlockSpec((None, LANES, D), lambda l, j, t=t: (l, tile(j, t), 0)) for t in range(tp)])
    wu, wd = pl.pallas_call(
        body,
        grid=(NL, nsteps),
        in_specs=in_specs,
        out_specs=[pl.BlockSpec((None, None, D, 2 * FFN_STEP), lambda l, j: (l, j, 0, 0)),
                   pl.BlockSpec((None, None, FFN_STEP, D), lambda l, j: (l, j, 0, 0))],
        out_shape=[jax.ShapeDtypeStruct((NL, nsteps, D, 2 * FFN_STEP), bf16),
                   jax.ShapeDtypeStruct((NL, nsteps, FFN_STEP, D), bf16)],
        compiler_params=_params(("parallel", "parallel"), 40),
        name="ffn_weight_layout",
    )(*([w_up] * (2 * tp) + [w_down] * tp))
    return [dict(wu=wu, wd=wd, layer=l, F=F, cw=_ffn_to_steps(conv_w[l], F),
                 cb=_ffn_to_steps(conv_b[l], F).reshape(nsteps, 1, 2 * FFN_STEP)) for l in range(NL)]


def _conv_ffn(x, g, fw, *, seq_len, tm, prev=None, final_g=None):
    M, D = x.shape
    F = fw['F']
    layer = fw['layer']
    nsteps = fw['wu'].shape[1]
    sample = prev is not None
    rc = min(tm, 256)
    tiles_per_seq = max(seq_len // tm, 1)
    W2 = 2 * FFN_STEP

    def body(*refs):
        it = iter(refs)
        x_ref = next(it)
        xh_ref = None if sample else next(it)
        prev_refs = [next(it) for _ in range(2 * FFN_STEP // LANES)] if sample else None
        g_ref = next(it)
        wu_ref, cw_ref, cb_ref, wd_ref = next(it), next(it), next(it), next(it)
        fg_ref = next(it) if final_g is not None else None
        o_ref, tail_g_ref, tail_v_ref = next(it), next(it), next(it)
        xn_ref, up_scr = next(it), next(it)
        i = pl.program_id(0)
        j = pl.program_id(1)

        @pl.when(j == 0)
        def _():
            if sample:
                xn_ref[0:HALO, :] = jnp.zeros((HALO, D), bf16)
            else:
                keep = (i % tiles_per_seq != 0).astype(f32)
                hist = _rms(xh_ref[...], g_ref[...]) * keep
                xn_ref[0:HALO, :] = jnp.concatenate([jnp.zeros_like(hist), hist], axis=0).astype(bf16)

            def chunk(r, c):
                rows = pl.ds(pl.multiple_of(r * rc, rc), rc)
                xr = x_ref[rows, :]
                o_ref[rows, :] = xr
                xn_ref[pl.ds(pl.multiple_of(HALO + r * rc, HALO), rc), :] = _rms(xr, g_ref[...]).astype(bf16)
                return c
            lax.fori_loop(0, tm // rc, chunk, 0)

        cw = cw_ref[0]
        cb = cb_ref[0]
        nchunk = tm // rc
        for r in range(nchunk):
            if sample:
                up_scr[r, HALO:, :] = _dot(xn_ref[pl.ds(HALO + r * rc, rc), :], wu_ref[...])
            else:
                up_scr[r] = _dot(xn_ref[pl.ds(r * rc, rc + HALO), :], wu_ref[...])
        for r in range(nchunk):
            rows = pl.ds(r * rc, rc)
            up = up_scr[r, HALO:, :]
            if sample:
                p0 = jnp.concatenate([p[rows, 0, :] for p in prev_refs], axis=1)
                p1 = jnp.concatenate([p[rows, 1, :] for p in prev_refs], axis=1)
                conv = cb + cw[0:1] * p0 + cw[1:2] * p1 + cw[2:3] * up
                tail_g_ref[rows, :] = up[:, :FFN_STEP]
                tail_v_ref[rows, :] = up[:, FFN_STEP:]
            else:
                conv = (cb + cw[0:1] * up_scr[r, HALO - 2:HALO - 2 + rc, :]
                        + cw[1:2] * up_scr[r, HALO - 1:HALO - 1 + rc, :] + cw[2:3] * up)
                if r == nchunk - 1:
                    tail_g_ref[...] = up[rc - SUBLANES:, :FFN_STEP]
                    tail_v_ref[...] = up[rc - SUBLANES:, FFN_STEP:]
            h = jax.nn.gelu(conv[:, :FFN_STEP]) * conv[:, FFN_STEP:]
            o_ref[rows, :] += _dot(h.astype(bf16), wd_ref[...])

        if final_g is not None:
            @pl.when(j == nsteps - 1)
            def _():
                def chunk2(r, c):
                    rows = pl.ds(pl.multiple_of(r * rc, rc), rc)
                    o_ref[rows, :] = _rms(o_ref[rows, :], fg_ref[...])
                    return c
                lax.fori_loop(0, tm // rc, chunk2, 0)

    in_specs = [pl.BlockSpec((tm, D), lambda i, j: (i, 0))]
    args = [x]
    if sample:
        nt = F // LANES
        tp = FFN_STEP // LANES
        last = 2 * nt - 1
        for off in list(range(tp)) + [nt + t for t in range(tp)]:
            in_specs.append(pl.BlockSpec((None, tm, 2, LANES),
                                         lambda i, j, off=off: (layer, i, 0, jnp.minimum(tp * j + off, last))))
            args.append(prev)
    else:
        in_specs.append(pl.BlockSpec((SUBLANES, D), lambda i, j: (jnp.maximum(i * (tm // SUBLANES) - 1, 0), 0)))
        args.append(x)
    in_specs += [pl.BlockSpec((1, D), lambda i, j: (0, 0)),
                 pl.BlockSpec((None, None, D, W2), lambda i, j: (layer, j, 0, 0)),
                 pl.BlockSpec((1, 3, W2), lambda i, j: (j, 0, 0)),
                 pl.BlockSpec((1, 1, W2), lambda i, j: (j, 0, 0)),
                 pl.BlockSpec((None, None, FFN_STEP, D), lambda i, j: (layer, j, 0, 0))]
    args += [g.reshape(1, D), fw['wu'], fw['cw'], fw['cb'], fw['wd']]
    if final_g is not None:
        in_specs.append(pl.BlockSpec((1, D), lambda i, j: (0, 0)))
        args.append(final_g.reshape(1, D))
    FP = nsteps * FFN_STEP
    if sample:
        tail_shape = (M, FP)
        tail_spec = pl.BlockSpec((tm, FFN_STEP), lambda i, j: (i, j))
    else:
        tail_shape = (M // tm, SUBLANES, FP)
        tail_spec = pl.BlockSpec((None, SUBLANES, FFN_STEP), lambda i, j: (i, 0, j))
    out, tail_g, tail_v = pl.pallas_call(
        body,
        grid=(M // tm, nsteps),
        in_specs=in_specs,
        out_specs=[pl.BlockSpec((tm, D), lambda i, j: (i, 0)), tail_spec, tail_spec],
        out_shape=[jax.ShapeDtypeStruct((M, D), f32), jax.ShapeDtypeStruct(tail_shape, f32),
                   jax.ShapeDtypeStruct(tail_shape, f32)],
        scratch_shapes=[pltpu.VMEM((tm + HALO, D), bf16), pltpu.VMEM((tm // rc, rc + HALO, W2), f32)],
        compiler_params=_params(("parallel", "arbitrary"), 56),
        name="conv_ffn_sample" if sample else "conv_ffn",
    )(*args)
    if sample:
        up_rows = jnp.concatenate([tail_g[:, :F], tail_v[:, :F]], axis=1)
        new_buf = jnp.stack([prev[layer, :, 1, :], up_rows], axis=1)
    else:
        nseq = M // seq_len
        pick = lambda t: t.reshape(nseq, tiles_per_seq, SUBLANES, FP)[:, -1, SUBLANES - 2:, :F]
        new_buf = jnp.concatenate([pick(tail_g), pick(tail_v)], axis=-1)
    return out, new_buf


def _gla_prompt(p0, alr, w2p, b_alpha, g_norm, *, B, L, H, DK, DV, tb):
    NT = L // tb
    NC = tb // CHUNK
    C = CHUNK
    scale = DK ** -0.5
    qk_blocks = H
    v_off = 2 * H * DK // DV

    def body(q_ref, k_ref, v_ref, r_ref, a_ref, w2_ref, ba_ref, gn_ref, y_ref, s_out_ref, s_scr):
        t = pl.program_id(2)

        @pl.when(t == 0)
        def _():
            s_scr[...] = jnp.zeros_like(s_scr)

        z = _dot(a_ref[...].astype(bf16), w2_ref[...].astype(bf16)) + ba_ref[...]
        gl = jax.nn.log_sigmoid(z) * (1.0 / GLA_TAU)
        bc3 = _chunk_cumsum(gl, C).reshape(NC, C, DK)
        bl3 = bc3[:, C - 1:C, :]
        q3 = (q_ref[...].astype(f32) * scale).reshape(NC, C, DK)
        k3 = k_ref[...].astype(f32).reshape(NC, C, DK)
        v3 = v_ref[...].astype(bf16).reshape(NC, C, DV)
        qd3 = (q3 * jnp.exp(bc3)).astype(bf16)
        kd3 = (k3 * jnp.exp(-bc3)).astype(bf16)
        kdec3 = (k3 * jnp.exp(bl3 - bc3)).astype(bf16)
        causal = (lax.broadcasted_iota(jnp.int32, (1, C, C), 1) >= lax.broadcasted_iota(jnp.int32, (1, C, C), 2))
        att = jnp.where(causal, jnp.einsum('cik,cjk->cij', qd3, kd3, preferred_element_type=f32), 0.0)
        intra = jnp.einsum('cij,cjv->civ', att.astype(bf16), v3, preferred_element_type=f32)
        ds = jnp.einsum('cjk,cjv->ckv', kdec3, v3, preferred_element_type=f32)
        s = s_scr[...]
        s_in = []
        for c in range(NC):
            s_in.append(s)
            s = _col_from_row(jnp.exp(bl3[c])) * s + ds[c]
        s_scr[...] = s
        s_all = jnp.stack(s_in).astype(bf16)
        o = (intra + jnp.einsum('cik,ckv->civ', qd3, s_all, preferred_element_type=f32)).reshape(tb, DV)
        rr = r_ref[...].astype(f32)
        y_ref[...] = (_rms(o, gn_ref[...]) * (rr * jax.nn.sigmoid(rr))).astype(bf16)

        @pl.when(t == NT - 1)
        def _():
            s_out_ref[0, 0] = s
    return pl.pallas_call(
        body,
        grid=(B, H, NT),
        in_specs=[pl.BlockSpec((tb, DK), lambda b, h, t: (b * NT + t, h)),
                  pl.BlockSpec((tb, DK), lambda b, h, t: (b * NT + t, qk_blocks + h)),
                  pl.BlockSpec((tb, DV), lambda b, h, t: (b * NT + t, v_off + h)),
                  pl.BlockSpec((tb, DV), lambda b, h, t: (b * NT + t, v_off + H + h)),
                  pl.BlockSpec((tb, LANES), lambda b, h, t: (b * NT + t, 0)),
                  pl.BlockSpec((LANES, DK), lambda b, h, t: (0, h)),
                  pl.BlockSpec((1, DK), lambda b, h, t: (0, h)),
                  pl.BlockSpec((1, DV), lambda b, h, t: (0, h))],
        out_specs=[pl.BlockSpec((tb, DV), lambda b, h, t: (b * NT + t, h)),
                   pl.BlockSpec((1, 1, DK, DV), lambda b, h, t: (b, h, 0, 0))],
        out_shape=[jax.ShapeDtypeStruct((B * L, H * DV), bf16),
                   jax.ShapeDtypeStruct((B, H, DK, DV), f32)],
        scratch_shapes=[pltpu.VMEM((DK, DV), f32)],
        compiler_params=_params(("parallel", "parallel", "arbitrary"), 32),
        name="gla_prompt",
    )(p0, p0, p0, p0, alr, w2p, b_alpha.reshape(1, -1), g_norm.reshape(1, -1))


def _gla_sample(p0, alr, w2p, b_alpha, g_norm, s0, *, H, DK, DV):
    Bs = p0.shape[0]
    scale = DK ** -0.5
    qkw = H * DK
    vw = H * DV
    assert vw % qkw == 0
    SB = SUBLANES

    def body(q_ref, k_ref, v_ref, r_ref, a_ref, w2_ref, ba_ref, gn_ref, s_ref, y_ref, so_ref, gl_scr):
        z = _dot(a_ref[...].astype(bf16), w2_ref[...].astype(bf16)) + ba_ref[...]
        gl_scr[...] = jax.nn.log_sigmoid(z) * (1.0 / GLA_TAU)

        for h in range(H):
            ks = slice(h * DK, (h + 1) * DK)
            vs = slice(h * DV, (h + 1) * DV)
            a_t = jnp.exp(gl_scr[:, ks]).T
            k_t = k_ref[:, ks].T
            q_t = (q_ref[:, ks] * scale).T
            v = v_ref[:, vs]
            outs = []
            for s in range(SB):
                sn = a_t[:, s:s + 1] * s_ref[s, h] + k_t[:, s:s + 1] * v[s:s + 1, :]
                so_ref[s, h] = sn
                outs.append(jnp.sum(q_t[:, s:s + 1] * sn, axis=0, keepdims=True))
            o = jnp.concatenate(outs, axis=0)
            rr = r_ref[:, vs]
            y_ref[:, vs] = _rms(o, gn_ref[:, vs]) * (rr * jax.nn.sigmoid(rr))

    v_blk = 2 * qkw // vw
    return pl.pallas_call(
        body,
        grid=(Bs // SB,),
        in_specs=[pl.BlockSpec((SB, qkw), lambda b: (b, 0)),
                  pl.BlockSpec((SB, qkw), lambda b: (b, 1)),
                  pl.BlockSpec((SB, vw), lambda b: (b, v_blk)),
                  pl.BlockSpec((SB, vw), lambda b: (b, v_blk + 1)),
                  pl.BlockSpec((SB, LANES), lambda b: (b, 0)),
                  pl.BlockSpec((LANES, qkw), lambda b: (0, 0)),
                  pl.BlockSpec((1, qkw), lambda b: (0, 0)),
                  pl.BlockSpec((1, vw), lambda b: (0, 0)),
                  pl.BlockSpec((SB, H, DK, DV), lambda b: (b, 0, 0, 0))],
        out_specs=[pl.BlockSpec((SB, vw), lambda b: (b, 0)),
                   pl.BlockSpec((SB, H, DK, DV), lambda b: (b, 0, 0, 0))],
        out_shape=[jax.ShapeDtypeStruct((Bs, vw), f32),
                   jax.ShapeDtypeStruct((Bs, H, DK, DV), f32)],
        scratch_shapes=[pltpu.VMEM((SB, qkw), f32)],
        compiler_params=_params(("parallel",), 32),
        name="gla_sample",
    )(p0, p0, p0, p0, alr, w2p, b_alpha.reshape(1, -1), g_norm.reshape(1, -1), s0)


def _rg_gates(xc, wr, br, wi, bi, sp):
    xb = xc.astype(bf16)
    r = jax.nn.sigmoid(_dot(xb, wr) + br)
    i = jax.nn.sigmoid(_dot(xb, wi) + bi)
    log_a = -RG_C * r * sp
    a = jnp.exp(log_a)
    mult = jnp.sqrt(1.0 - a * a)
    return a, mult, i


def _rglru_prompt(p0, conv_w, conv_b, w_r, b_r, w_i, b_i, lam, *, B, L, x_blk, g_blk):
    M = B * L
    NB, BS, _ = w_r.shape
    assert BS == LANES
    rc = min(256, L)

    def body(x_ref, gg_ref, cw_ref, cb_ref, wr_ref, br_ref, wi_ref, bi_ref, lam_ref, y_ref, hl_ref, a_scr, b_scr, xs_scr):
        wr = wr_ref[...].astype(bf16)
        wi = wi_ref[...].astype(bf16)
        sp = jax.nn.softplus(-lam_ref[...])
        cw = cw_ref[...]

        def chunk(c, carry):
            r0 = pl.multiple_of(c * rc, rc)
            rows = pl.ds(r0, rc)
            x = x_ref[rows, :].astype(f32)
            start = (r0 % L) == 0
            prev = x_ref[pl.ds(pl.multiple_of(jnp.maximum(r0 - HALO, 0), HALO), HALO), :].astype(f32)[HALO - SUBLANES:]
            xs_scr[0:SUBLANES, :] = jnp.where(start, 0.0, prev)
            xs_scr[SUBLANES:, :] = x
            xc = (cb_ref[...] + cw[0:1] * xs_scr[SUBLANES - 3:SUBLANES - 3 + rc, :]
                  + cw[1:2] * xs_scr[SUBLANES - 2:SUBLANES - 2 + rc, :]
                  + cw[2:3] * xs_scr[SUBLANES - 1:SUBLANES - 1 + rc, :] + cw[3:4] * x)
            a, mult, ig = _rg_gates(xc, wr, br_ref[...], wi, bi_ref[...], sp)
            first = jnp.logical_and(start, lax.broadcasted_iota(jnp.int32, (rc, 1), 0) == 0)
            mult = jnp.where(first, 1.0, mult)
            a_scr[rows, :] = a
            b_scr[rows, :] = mult * (ig * xc)
            return carry
        lax.fori_loop(0, M // rc, chunk, 0)

        def step(t, hs):
            new = []
            for b in range(B):
                row = pl.ds(b * L + t, 1)
                h = a_scr[row, :] * hs[b] + b_scr[row, :]
                b_scr[row, :] = h
                new.append(h)
            return tuple(new)
        hs = lax.fori_loop(0, L, step, tuple(jnp.zeros((1, LANES), f32) for _ in range(B)), unroll=8)
        hl_ref[...] = jnp.concatenate(hs, axis=0)

        def outc(c, carry):
            rows = pl.ds(pl.multiple_of(c * rc, rc), rc)
            y_ref[rows, :] = (b_scr[rows, :] * jax.nn.gelu(gg_ref[rows, :].astype(f32))).astype(bf16)
            return carry
        lax.fori_loop(0, M // rc, outc, 0)

    W = NB * BS
    return pl.pallas_call(
        body,
        grid=(NB,),
        in_specs=[pl.BlockSpec((M, LANES), lambda n: (0, x_blk + n)),
                  pl.BlockSpec((M, LANES), lambda n: (0, g_blk + n)),
                  pl.BlockSpec((4, LANES), lambda n: (0, n)),
                  pl.BlockSpec((1, LANES), lambda n: (0, n)),
                  pl.BlockSpec((None, BS, BS), lambda n: (n, 0, 0)),
                  pl.BlockSpec((1, LANES), lambda n: (0, n)),
                  pl.BlockSpec((None, BS, BS), lambda n: (n, 0, 0)),
                  pl.BlockSpec((1, LANES), lambda n: (0, n)),
                  pl.BlockSpec((1, LANES), lambda n: (0, n))],
        out_specs=[pl.BlockSpec((M, LANES), lambda n: (0, n)),
                   pl.BlockSpec((B, LANES), lambda n: (0, n))],
        out_shape=[jax.ShapeDtypeStruct((M, W), bf16), jax.ShapeDtypeStruct((B, W), f32)],
        scratch_shapes=[pltpu.VMEM((M, LANES), f32), pltpu.VMEM((M, LANES), f32), pltpu.VMEM((rc + SUBLANES, LANES), f32)],
        compiler_params=_params(("parallel",), 48),
        name="rglru_prompt",
    )(p0, p0, conv_w, conv_b.reshape(1, W), w_r, b_r.reshape(1, W), w_i, b_i.reshape(1, W), lam.reshape(1, W))


def _rglru_sample(p0, conv_state, h0, conv_w, conv_b, w_r, b_r, w_i, b_i, lam, *, x_blk, g_blk):
    Bs = p0.shape[0]
    NB, BS, _ = w_r.shape
    W = NB * BS
    s0, s1, s2 = conv_state[:, 0], conv_state[:, 1], conv_state[:, 2]

    def body(x_ref, gg_ref, s0_ref, s1_ref, s2_ref, h0_ref, cw_ref, cb_ref, wr_ref, br_ref, wi_ref, bi_ref, lam_ref,
             y_ref, h_ref):
        cw = cw_ref[...]
        x = x_ref[...]
        xc = cb_ref[...] + cw[0:1] * s0_ref[...] + cw[1:2] * s1_ref[...] + cw[2:3] * s2_ref[...] + cw[3:4] * x
        sp = jax.nn.softplus(-lam_ref[...])
        a, mult, ig = _rg_gates(xc, wr_ref[...].astype(bf16), br_ref[...], wi_ref[...].astype(bf16), bi_ref[...], sp)
        if PAST_LEN == 0:
            mult = jnp.ones_like(mult)
        h = a * h0_ref[...] + mult * (ig * xc)
        h_ref[...] = h
        y_ref[...] = (h * jax.nn.gelu(gg_ref[...])).astype(bf16)

    blk = lambda n: (0, n)
    vec = pl.BlockSpec((1, LANES), blk)
    mat = pl.BlockSpec((Bs, LANES), blk)
    y, h = pl.pallas_call(
        body,
        grid=(NB,),
        in_specs=[pl.BlockSpec((Bs, LANES), lambda n: (0, x_blk + n)),
                  pl.BlockSpec((Bs, LANES), lambda n: (0, g_blk + n)),
                  mat, mat, mat, mat,
                  pl.BlockSpec((4, LANES), blk), vec,
                  pl.BlockSpec((None, BS, BS), lambda n: (n, 0, 0)), vec,
                  pl.BlockSpec((None, BS, BS), lambda n: (n, 0, 0)), vec, vec],
        out_specs=[mat, mat],
        out_shape=[jax.ShapeDtypeStruct((Bs, W), bf16), jax.ShapeDtypeStruct((Bs, W), f32)],
        compiler_params=_params(("parallel",), 32),
        name="rglru_sample",
    )(p0, p0, s0, s1, s2, h0, conv_w, conv_b.reshape(1, W), w_r, b_r.reshape(1, W), w_i, b_i.reshape(1, W),
      lam.reshape(1, W))
    return y, h


def _blockdiag_tiles(w, tile):
    nblk, bs, _ = w.shape
    per = tile // bs
    nt = nblk // per
    rows = jnp.tile(w.reshape(nt, tile, bs), (1, 1, per))
    on_diag = (jnp.arange(tile)[:, None] // bs) == (jnp.arange(tile)[None, :] // bs)
    return jnp.where(on_diag[None], rows, 0.0)


def _mlstm_stage1(p1, conv_w, conv_b, wq_t, wk_t, wv_t, wg, bg, *, seq_len, tr, H, DH, conv_state=None):
    M = p1.shape[0]
    W = H * DH
    NTL, TL, _ = wq_t.shape
    sample = conv_state is not None
    tiles_per_seq = max(seq_len // tr, 1)
    kscale = DH ** -0.5

    def body(*refs):
        it = iter(refs)
        x_ref = next(it)
        if sample:
            s0_ref, s1_ref, s2_ref = next(it), next(it), next(it)
        else:
            xh_ref = next(it)
        cw_ref, cb_ref, wq_ref, wk_ref, wv_ref, wg_ref, bg_ref = (next(it) for _ in range(7))
        q_ref, k_ref, v_ref, g_ref, xc_ref = (next(it) for _ in range(5))
        cw = cw_ref[...]
        x = x_ref[...].astype(f32)
        if sample:
            conv = cb_ref[...] + cw[0:1] * s0_ref[...] + cw[1:2] * s1_ref[...] + cw[2:3] * s2_ref[...] + cw[3:4] * x
        else:
            xs_scr = next(it)
            keep = (pl.program_id(0) % tiles_per_seq != 0).astype(f32)
            xs_scr[0:SUBLANES, :] = xh_ref[...].astype(f32)[HALO - SUBLANES:] * keep
            xs_scr[SUBLANES:, :] = x
            conv = (cb_ref[...] + cw[0:1] * xs_scr[SUBLANES - 3:SUBLANES - 3 + tr, :]
                    + cw[1:2] * xs_scr[SUBLANES - 2:SUBLANES - 2 + tr, :]
                    + cw[2:3] * xs_scr[SUBLANES - 1:SUBLANES - 1 + tr, :] + cw[3:4] * x)
        xc = conv * jax.nn.sigmoid(conv)
        xc_ref[...] = xc.astype(xc_ref.dtype)
        xcb = xc.astype(bf16)
        xb = x.astype(bf16)
        qs, ks, vs = [], [], []
        for t in range(NTL):
            cs = slice(t * TL, (t + 1) * TL)
            qs.append(_dot(xcb[:, cs], wq_ref[t]))
            ks.append(_dot(xcb[:, cs], wk_ref[t]) * kscale)
            vs.append(_dot(xb[:, cs], wv_ref[t]))
        q = jnp.concatenate(qs, axis=1)
        k = jnp.concatenate(ks, axis=1)
        v = jnp.concatenate(vs, axis=1)
        q_ref[...] = q.astype(q_ref.dtype)
        k_ref[...] = k.astype(k_ref.dtype)
        v_ref[...] = v.astype(v_ref.dtype)
        gt = (_dot(q.astype(bf16), wg_ref[0:W, :]) + _dot(k.astype(bf16), wg_ref[W:2 * W, :])
              + _dot(v.astype(bf16), wg_ref[2 * W:3 * W, :]) + bg_ref[...])
        lane = lax.broadcasted_iota(jnp.int32, gt.shape, 1)
        g_ref[...] = jnp.where(jnp.logical_and(lane >= H, lane < 2 * H), jax.nn.log_sigmoid(gt), gt)

    row = lambda i: (i, 0)
    const2 = lambda i: (0, 0)
    const3 = lambda i: (0, 0, 0)
    in_specs = [pl.BlockSpec((tr, W), row)]
    args = [p1]
    if sample:
        in_specs += [pl.BlockSpec((tr, W), row)] * 3
        args += [conv_state[:, 0], conv_state[:, 1], conv_state[:, 2]]
    else:
        in_specs.append(pl.BlockSpec((HALO, W), lambda i: (jnp.maximum(i * (tr // HALO) - 1, 0), 0)))
        args.append(p1)
    in_specs += [pl.BlockSpec((4, W), const2), pl.BlockSpec((1, W), const2),
                 pl.BlockSpec((NTL, TL, TL), const3), pl.BlockSpec((NTL, TL, TL), const3),
                 pl.BlockSpec((NTL, TL, TL), const3),
                 pl.BlockSpec((3 * W, LANES), const2), pl.BlockSpec((1, LANES), const2)]
    args += [conv_w, conv_b.reshape(1, W), wq_t, wk_t, wv_t, wg, bg]
    return pl.pallas_call(
        body,
        grid=(M // tr,),
        in_specs=in_specs,
        out_specs=[pl.BlockSpec((tr, W), row)] * 3 + [pl.BlockSpec((tr, LANES), row), pl.BlockSpec((tr, W), row)],
        out_shape=[jax.ShapeDtypeStruct((M, W), f32 if sample else bf16)] * 3
        + [jax.ShapeDtypeStruct((M, LANES), f32), jax.ShapeDtypeStruct((M, W), f32 if sample else bf16)],
        scratch_shapes=[] if sample else [pltpu.VMEM((tr + SUBLANES, W), f32)],
        compiler_params=_params(("parallel",), 48),
        name="mlstm_stage1_sample" if sample else "mlstm_stage1",
    )(*args)


def _mlstm_prompt(q, k, v, gates, xc, p1, g_norm, skip, *, B, L, H, DH, tb):
    NT = L // tb
    NC = tb // CHUNK
    C = CHUNK

    def body(q_ref, k_ref, v_ref, g_ref, xc_ref, om_ref, gn_ref, sk_ref,
             y_ref, c_out, n_out, m_out, c_scr, n_scr, m_scr):
        hh = pl.program_id(1)
        t = pl.program_id(2)

        @pl.when(t == 0)
        def _():
            c_scr[...] = jnp.zeros_like(c_scr)
            n_scr[...] = jnp.zeros_like(n_scr)
            m_scr[...] = jnp.zeros_like(m_scr)

        gts = g_ref[...]
        lane = lax.broadcasted_iota(jnp.int32, (tb, LANES), 1)
        i_col = jnp.sum(jnp.where(lane == hh, gts, 0.0), axis=1, keepdims=True)
        b_col = jnp.sum(jnp.where(lane == H + hh, _chunk_cumsum(gts, C), 0.0), axis=1, keepdims=True)
        b3 = b_col.reshape(NC, C, 1)
        i3 = i_col.reshape(NC, C, 1)
        ii = lax.broadcasted_iota(jnp.int32, (1, C, C), 1)
        jj = lax.broadcasted_iota(jnp.int32, (1, C, C), 2)
        eye = ii == jj
        causal = ii >= jj
        as_row = lambda col3: jnp.sum(jnp.where(eye, jnp.broadcast_to(col3, (NC, C, C)), 0.0), axis=1, keepdims=True)
        dmat = jnp.where(causal, b3 - as_row(b3) + as_row(i3), -jnp.inf)
        rmax = jnp.max(dmat, axis=2, keepdims=True)
        b_last = b3[:, C - 1:C, :]
        m_prev = m_scr[...]
        m_in = []
        for c in range(NC):
            m_in.append(m_prev)
            m_prev = jnp.maximum(b_last[c] + m_prev, rmax[c][C - 1:C, :])
        m_scr[...] = m_prev
        inter = b3 + jnp.stack(m_in)
        m_col = jnp.maximum(inter, rmax)
        g_col = jnp.exp(inter - m_col)
        q3 = q_ref[...].reshape(NC, C, DH)
        k3 = k_ref[...].reshape(NC, C, DH)
        v3 = v_ref[...].reshape(NC, C, DH)
        s = jnp.einsum('cid,cjd->cij', q3, k3, preferred_element_type=f32) * jnp.exp(dmat - m_col)
        num = jnp.einsum('cij,cje->cie', s.astype(bf16), v3, preferred_element_type=f32)
        den = jnp.sum(s, axis=2, keepdims=True)
        m_new = m_col[:, C - 1:C, :]
        wk = jnp.exp(b_last - b3 + i3 - m_new)
        gc = jnp.exp(inter[:, C - 1:C, :] - m_new)
        kw = k3.astype(f32) * wk
        dc = jnp.einsum('cse,csd->ced', v3, kw.astype(bf16), preferred_element_type=f32)
        dn = jnp.sum(kw, axis=1, keepdims=True)
        cs = c_scr[...]
        ns = n_scr[...]
        c_in, n_in = [], []
        for c in range(NC):
            c_in.append(cs)
            n_in.append(ns)
            cs = gc[c] * cs + dc[c]
            ns = gc[c] * ns + dn[c]
        c_scr[...] = cs
        n_scr[...] = ns
        c_all = jnp.stack(c_in).astype(bf16)
        n_all = jnp.stack(n_in)
        num = num + g_col * jnp.einsum('cid,ced->cie', q3, c_all, preferred_element_type=f32)
        den = den + g_col * jnp.sum(q3.astype(f32) * n_all, axis=2, keepdims=True)
        hm = (num / jnp.maximum(jnp.abs(den), jnp.exp(-m_col))).reshape(tb, DH)
        y = (_rms(hm, gn_ref[...]) + sk_ref[...] * xc_ref[...].astype(f32)) * jax.nn.sigmoid(om_ref[...].astype(f32))
        y_ref[...] = y.astype(bf16)

        @pl.when(t == NT - 1)
        def _():
            c_out[0, 0] = cs
            n_out[0, 0] = ns
            m_out[0, 0] = m_prev

    blk = lambda b, h, t: (b * NT + t, h)
    W = H * DH
    y, c_new, n_new, m_new = pl.pallas_call(
        body,
        grid=(B, H, NT),
        in_specs=[pl.BlockSpec((tb, DH), blk), pl.BlockSpec((tb, DH), blk), pl.BlockSpec((tb, DH), blk),
                  pl.BlockSpec((tb, LANES), lambda b, h, t: (b * NT + t, 0)),
                  pl.BlockSpec((tb, DH), blk),
                  pl.BlockSpec((tb, DH), lambda b, h, t: (b * NT + t, H + h)),
                  pl.BlockSpec((1, DH), lambda b, h, t: (0, h)),
                  pl.BlockSpec((1, DH), lambda b, h, t: (0, h))],
        out_specs=[pl.BlockSpec((tb, DH), blk),
                   pl.BlockSpec((1, 1, DH, DH), lambda b, h, t: (b, h, 0, 0)),
                   pl.BlockSpec((1, 1, 1, DH), lambda b, h, t: (b, h, 0, 0)),
                   pl.BlockSpec((1, 1, 1, 1), lambda b, h, t: (b, h, 0, 0))],
        out_shape=[jax.ShapeDtypeStruct((B * L, W), bf16),
                   jax.ShapeDtypeStruct((B, H, DH, DH), f32),
                   jax.ShapeDtypeStruct((B, H, 1, DH), f32),
                   jax.ShapeDtypeStruct((B, H, 1, 1), f32)],
        scratch_shapes=[pltpu.VMEM((DH, DH), f32), pltpu.VMEM((1, DH), f32), pltpu.VMEM((1, 1), f32)],
        compiler_params=_params(("parallel", "parallel", "arbitrary"), 32),
        name="mlstm_prompt",
    )(q, k, v, gates, xc, p1, g_norm.reshape(1, W), skip.reshape(1, W))
    return y, c_new, n_new.reshape(B, H, DH), m_new.reshape(B, H)


def _mlstm_sample(q, k, v, gates, xc, p1, g_norm, skip, c0, n0, m0, *, H, DH):
    Bs = q.shape[0]
    W = H * DH
    SB = SUBLANES

    def body(q_ref, k_ref, v_ref, g_ref, xc_ref, om_ref, gn_ref, sk_ref, c_ref, n_ref, m_ref,
             y_ref, c_out, n_out, m_out):
        gts = g_ref[...]
        for h in range(H):
            cs = slice(h * DH, (h + 1) * DH)
            ig = gts[:, h:h + 1]
            fg = gts[:, H + h:H + h + 1]
            inter = fg + m_ref[:, h:h + 1]
            m = jnp.maximum(inter, ig)
            g = jnp.exp(inter - m)
            m_out[:, h:h + 1] = m
            q = q_ref[:, cs]
            kw = k_ref[:, cs] * jnp.exp(ig - m)
            nn = g * n_ref[:, cs] + kw
            n_out[:, cs] = nn
            den = jnp.sum(nn * q, axis=1, keepdims=True)
            v_t = v_ref[:, cs].T
            qb = q.astype(bf16)
            nums = []
            for s in range(SB):
                cn = g[s:s + 1, :] * c_ref[s, h] + v_t[:, s:s + 1] * kw[s:s + 1, :]
                c_out[s, h] = cn
                nums.append(_dot_nt(qb, cn.astype(bf16))[s:s + 1, :])
            num = jnp.concatenate(nums, axis=0)
            hm = num / jnp.maximum(jnp.abs(den), jnp.exp(-m))
            y_ref[:, cs] = ((_rms(hm, gn_ref[:, cs]) + sk_ref[:, cs] * xc_ref[:, cs])
                            * jax.nn.sigmoid(om_ref[:, cs]))

    per = lambda b: (b, 0)
    const2 = lambda b: (0, 0)
    y, c_new, n_new, m_new = pl.pallas_call(
        body,
        grid=(Bs // SB,),
        in_specs=[pl.BlockSpec((SB, W), per), pl.BlockSpec((SB, W), per), pl.BlockSpec((SB, W), per),
                  pl.BlockSpec((SB, LANES), per), pl.BlockSpec((SB, W), per),
                  pl.BlockSpec((SB, W), lambda b: (b, 1)),
                  pl.BlockSpec((1, W), const2), pl.BlockSpec((1, W), const2),
                  pl.BlockSpec((SB, H, DH, DH), lambda b: (b, 0, 0, 0)),
                  pl.BlockSpec((SB, W), per), pl.BlockSpec((SB, H), per)],
        out_specs=[pl.BlockSpec((SB, W), per),
                   pl.BlockSpec((SB, H, DH, DH), lambda b: (b, 0, 0, 0)),
                   pl.BlockSpec((SB, W), per), pl.BlockSpec((SB, H), per)],
        out_shape=[jax.ShapeDtypeStruct((Bs, W), f32), jax.ShapeDtypeStruct((Bs, H, DH, DH), f32),
                   jax.ShapeDtypeStruct((Bs, W), f32), jax.ShapeDtypeStruct((Bs, H), f32)],
        compiler_params=_params(("parallel",), 48),
        name="mlstm_sample",
    )(q, k, v, gates, xc, p1, g_norm.reshape(1, W), skip.reshape(1, W), c0, n0.reshape(Bs, W), m0)
    return y, c_new, n_new.reshape(Bs, H, DH), m_new


S5_TILES = 8


def _s5_layouts(lam_re, lam_im, log_dt, b_re, b_im, c_re, c_im):
    G, P = lam_re.shape
    GC = b_re.shape[2]
    T = S5_TILES
    gpt = G // T
    ns = G * P
    flat = lambda a: a.reshape(ns)
    ldt = jnp.broadcast_to(log_dt[:, None], (G, P))
    rows = [flat(a).reshape(T, 1, ns // T) for a in (lam_re, lam_im, ldt)]
    eye = jnp.eye(gpt, dtype=f32)
    bbd = [jnp.einsum('jgpc,gh->jgchp', a.reshape(T, gpt, P, GC), eye).reshape(T, gpt * GC, gpt * P) for a in (b_re, b_im)]
    cbd = [jnp.einsum('jgcp,gh->jgphc', a.reshape(T, gpt, GC, P), eye).reshape(T, gpt * P, gpt * GC) for a in (c_re, c_im)]
    return rows, bbd, cbd


def _s5_discretise(lre, lim, ldt):
    dt = jnp.exp(ldt)
    mag = jnp.exp(dt * lre)
    ar = mag * jnp.cos(dt * lim)
    ai = mag * jnp.sin(dt * lim)
    den = lre * lre + lim * lim
    cr = ((ar - 1.0) * lre + ai * lim) / den
    ci = (ai * lre - (ar - 1.0) * lim) / den
    return ar, ai, cr, ci


def _s5_mixer(p1, u_blk, rows, bbd, cbd, d_skip, w_glu, b_glu, *, B, L, tb, state=None):
    T = S5_TILES
    lre_r, lim_r, ldt_r = rows
    SW = lre_r.shape[2]
    CW = bbd[0].shape[1]
    W = T * CW
    NS = T * SW
    KT = SW // LANES
    sample = state is not None
    NT = 1 if sample else L // tb
    M = B * L

    def body(*refs):
        it = iter(refs)
        u_ref = next(it)
        if sample:
            x0r_ref, x0i_ref = next(it), next(it)
        lre_ref, lim_ref, ldt_ref = next(it), next(it), next(it)
        bre_ref, bim_ref, cre_ref, cim_ref = next(it), next(it), next(it), next(it)
        d_ref, wg_ref, bgl_ref = next(it), next(it), next(it)
        y_ref, xr_out, xi_out = next(it), next(it), next(it)
        bbr, bbi, cpair, ar_scr, ai_scr = (next(it) for _ in range(5))
        if not sample:
            sre, sim, xr_c, xi_c, yacc = (next(it) for _ in range(5))
        first = jnp.logical_and(pl.program_id(0) == 0, pl.program_id(1) == 0)

        @pl.when(first)
        def _():
            for j in range(T):
                ar, ai, cr, ci = _s5_discretise(lre_ref[j], lim_ref[j], ldt_ref[j])
                ar_scr[j] = ar
                ai_scr[j] = ai
                br = bre_ref[j]
                bi = bim_ref[j]
                bbr[j] = (cr * br - ci * bi).astype(bf16)
                bbi[j] = (cr * bi + ci * br).astype(bf16)
            for jp in range(T // 2):
                cpair[jp] = jnp.zeros((4 * SW, 2 * CW), bf16)
                for half in range(2):
                    j = 2 * jp + half
                    r0 = 2 * half * SW
                    cols = slice(half * CW, (half + 1) * CW)
                    cpair[jp, r0:r0 + SW, cols] = cre_ref[j].astype(bf16)
                    cpair[jp, r0 + SW:r0 + 2 * SW, cols] = (-cim_ref[j]).astype(bf16)

        u = u_ref[...].astype(f32)
        ub = u_ref[...].astype(bf16)
        ys = []
        if sample:
            for j in range(T):
                cs = slice(j * SW, (j + 1) * SW)
                uj = ub[:, j * CW:(j + 1) * CW]
                ar = ar_scr[j]
                ai = ai_scr[j]
                x0r = x0r_ref[:, cs]
                x0i = x0i_ref[:, cs]
                xr = ar * x0r - ai * x0i + _dot(uj, bbr[j])
                xi = ar * x0i + ai * x0r + _dot(uj, bbi[j])
                xr_out[:, cs] = xr
                xi_out[:, cs] = xi
                ys += [xr.astype(bf16), xi.astype(bf16)]
            y = jnp.concatenate([_dot(jnp.concatenate(ys[4 * jp:4 * jp + 4], axis=1), cpair[jp]) for jp in range(T // 2)],
                                axis=1)
        else:
            t = pl.program_id(1)

            @pl.when(t == 0)
            def _():
                xr_c[...] = jnp.zeros_like(xr_c)
                xi_c[...] = jnp.zeros_like(xi_c)

            for j in range(T):
                uj = ub[:, j * CW:(j + 1) * CW]
                r = _dot(uj, bbr[j])
                im = _dot(uj, bbi[j])
                for kk in range(KT):
                    sre[kk, pl.ds(j, tb, stride=T), :] = r[:, kk * LANES:(kk + 1) * LANES]
                    sim[kk, pl.ds(j, tb, stride=T), :] = im[:, kk * LANES:(kk + 1) * LANES]
            a_r = [jnp.concatenate([ar_scr[j][:, kk * LANES:(kk + 1) * LANES] for j in range(T)], axis=0) for kk in range(KT)]
            a_i = [jnp.concatenate([ai_scr[j][:, kk * LANES:(kk + 1) * LANES] for j in range(T)], axis=0) for kk in range(KT)]

            def step(s, carry):
                xr, xi = carry
                row = pl.ds(pl.multiple_of(s * T, T), T)
                nr, ni = [], []
                for kk in range(KT):
                    r_ = a_r[kk] * xr[kk] - a_i[kk] * xi[kk] + sre[kk, row, :]
                    i_ = a_r[kk] * xi[kk] + a_i[kk] * xr[kk] + sim[kk, row, :]
                    sre[kk, row, :] = r_
                    sim[kk, row, :] = i_
                    nr.append(r_)
                    ni.append(i_)
                return tuple(nr), tuple(ni)
            xr0 = tuple(xr_c[kk] for kk in range(KT))
            xi0 = tuple(xi_c[kk] for kk in range(KT))
            xr, xi = lax.fori_loop(0, tb, step, (xr0, xi0), unroll=4)
            for kk in range(KT):
                xr_c[kk] = xr[kk]
                xi_c[kk] = xi[kk]

            @pl.when(t == NT - 1)
            def _():
                for kk in range(KT):
                    xr_out[kk] = xr[kk]
                    xi_out[kk] = xi[kk]

            for jp in range(T // 2):
                parts = []
                for j in (2 * jp, 2 * jp + 1):
                    parts += [sre[kk, pl.ds(j, tb, stride=T), :] for kk in range(KT)]
                    parts += [sim[kk, pl.ds(j, tb, stride=T), :] for kk in range(KT)]
                xp = jnp.concatenate(parts, axis=1).astype(bf16)
                yacc[:, 2 * jp * CW:(2 * jp + 2) * CW] = _dot(xp, cpair[jp])
            y = yacc[...]
        ysk = jax.nn.gelu(y + d_ref[...] * u)
        z = _dot(ysk.astype(bf16), wg_ref[...]) + bgl_ref[...]
        y_ref[...] = (ysk * jax.nn.sigmoid(z)).astype(bf16)

    c3 = lambda b, t: (0, 0, 0)
    c2 = lambda b, t: (0, 0)
    in_specs = [pl.BlockSpec((tb, W), lambda b, t: (b * NT + t, u_blk))]
    args = [p1]
    if sample:
        in_specs += [pl.BlockSpec((tb, NS), lambda b, t: (b, 0))] * 2
        args += [state[0], state[1]]
    in_specs += [pl.BlockSpec((T, 1, SW), c3)] * 3
    in_specs += [pl.BlockSpec((T, CW, SW), c3)] * 2 + [pl.BlockSpec((T, SW, CW), c3)] * 2
    in_specs += [pl.BlockSpec((1, W), c2), pl.BlockSpec((W, W), c2), pl.BlockSpec((1, W), c2)]
    args += [lre_r, lim_r, ldt_r, bbd[0], bbd[1], cbd[0], cbd[1], d_skip.reshape(1, W), w_glu, b_glu.reshape(1, W)]
    scratch = [pltpu.VMEM((T, CW, SW), bf16), pltpu.VMEM((T, CW, SW), bf16),
               pltpu.VMEM((T // 2, 4 * SW, 2 * CW), bf16),
               pltpu.VMEM((T, 1, SW), f32), pltpu.VMEM((T, 1, SW), f32)]
    if sample:
        grid = (M // tb, 1)
        st_spec = pl.BlockSpec((tb, NS), lambda b, t: (b, 0))
        st_shape = jax.ShapeDtypeStruct((M, NS), f32)
    else:
        grid = (B, NT)
        st_spec = pl.BlockSpec((None, KT, T, LANES), lambda b, t: (b, 0, 0, 0))
        st_shape = jax.ShapeDtypeStruct((B, KT, T, LANES), f32)
        scratch += [pltpu.VMEM((KT, tb * T, LANES), f32), pltpu.VMEM((KT, tb * T, LANES), f32),
                    pltpu.VMEM((KT, T, LANES), f32), pltpu.VMEM((KT, T, LANES), f32), pltpu.VMEM((tb, W), f32)]
    y, xr, xi = pl.pallas_call(
        body,
        grid=grid,
        in_specs=in_specs,
        out_specs=[pl.BlockSpec((tb, W), lambda b, t: (b * NT + t, 0)), st_spec, st_spec],
        out_shape=[jax.ShapeDtypeStruct((M, W), bf16), st_shape, st_shape],
        scratch_shapes=scratch,
        compiler_params=_params(("arbitrary", "arbitrary"), 56),
        name="s5_sample" if sample else "s5_prompt",
    )(*args)
    if not sample:
        xr = jnp.transpose(xr, (0, 2, 1, 3)).reshape(B, NS)
        xi = jnp.transpose(xi, (0, 2, 1, 3)).reshape(B, NS)
    return y, xr, xi


def _trunk(x3, st, w, *, sample):
    B, L, D = x3.shape
    M = B * L
    x = x3.reshape(M, D)
    H_g, DK, DV = st['gla_S'].shape[1:] if sample else w['gla_dims']
    H_m, DH = w['ml_dims']
    G, P = w['s5_dims']
    tm = M if sample else min(1024, L)
    tf = M if sample else min(1024, L)
    to = M if sample else min(2048, M)
    tb = min(1024, L)
    pdt = f32 if sample else bf16
    out = {}

    p0, alr = _norm_matmul(x, w['g_mix0'], w['w_in0_main'], tm=tm, tn=1024, w_side=w['w_in0_alr'], out_dtype=pdt)
    x_blk = (2 * H_g * DK + 2 * H_g * DV) // LANES
    W_rg = w['rg_lambda'].shape[0]
    g_blk = x_blk + W_rg // LANES
    rg_tail = p0.reshape(B, L, p0.shape[1])[:, max(L - 3, 0):, x_blk * LANES:x_blk * LANES + W_rg].astype(f32)
    if sample:
        ya, out['gla_S'] = _gla_sample(p0, alr, w['gla_w2p'], w['gla_b_alpha'], w['gla_g_norm'], st['gla_S'],
                                       H=H_g, DK=DK, DV=DV)
        yb, out['rg_h'] = _rglru_sample(p0, st['rg_conv'], st['rg_h'], w['rg_conv_w'], w['rg_conv_b'], w['rg_w_r'],
                                        w['rg_b_r'], w['rg_w_i'], w['rg_b_i'], w['rg_lambda'], x_blk=x_blk, g_blk=g_blk)
        out['rg_conv'] = jnp.concatenate([st['rg_conv'][:, 1:], rg_tail], axis=1)
    else:
        ya, out['gla_S'] = _gla_prompt(p0, alr, w['gla_w2p'], w['gla_b_alpha'], w['gla_g_norm'],
                                       B=B, L=L, H=H_g, DK=DK, DV=DV, tb=tb)
        yb, out['rg_h'] = _rglru_prompt(p0, w['rg_conv_w'], w['rg_conv_b'], w['rg_w_r'], w['rg_b_r'], w['rg_w_i'],
                                        w['rg_b_i'], w['rg_lambda'], B=B, L=L, x_blk=x_blk, g_blk=g_blk)
        out['rg_conv'] = rg_tail
    x = _out_proj(ya, yb, w['w_out0'], x, tm=to, tn=512)
    x, ffn0 = _conv_ffn(x, w['g_ffn'][0], w['ffn'][0], seq_len=L, tm=tf, prev=st['ffn_conv'] if sample else None)

    W_ml = H_m * DH
    p1 = _norm_matmul(x, w['g_mix1'], [w['w_in1']], tm=tm, tn=1024, out_dtype=pdt)
    xm_tail = p1.reshape(B, L, p1.shape[1])[:, max(L - 3, 0):, :W_ml].astype(f32)
    q, k, v, gates, xc = _mlstm_stage1(p1, w['ml_conv_w'], w['ml_conv_b'], w['ml_wq_t'], w['ml_wk_t'], w['ml_wv_t'],
                                       w['ml_wg'], w['ml_bg'], seq_len=L, tr=M if sample else min(256, L), H=H_m, DH=DH,
                                       conv_state=st['ml_conv'] if sample else None)
    u_blk = 2 * W_ml // (G * w['s5_gc'])
    if sample:
        yc, out['ml_C'], out['ml_n'], out['ml_m'] = _mlstm_sample(
            q, k, v, gates, xc, p1, w['ml_g_norm'], w['ml_skip'], st['ml_C'], st['ml_n'], st['ml_m'], H=H_m, DH=DH)
        out['ml_conv'] = jnp.concatenate([st['ml_conv'][:, 1:], xm_tail], axis=1)
        yd, s5r, s5i = _s5_mixer(p1, u_blk, w['s5_rows'], w['s5_bbd'], w['s5_cbd'], w['s5_D'], w['s5_w_glu'],
                                 w['s5_b_glu'], B=B, L=1, tb=B,
                                 state=(st['s5_re'].reshape(B, G * P), st['s5_im'].reshape(B, G * P)))
    else:
        yc, out['ml_C'], out['ml_n'], out['ml_m'] = _mlstm_prompt(
            q, k, v, gates, xc, p1, w['ml_g_norm'], w['ml_skip'], B=B, L=L, H=H_m, DH=DH, tb=tb)
        out['ml_conv'] = xm_tail
        yd, s5r, s5i = _s5_mixer(p1, u_blk, w['s5_rows'], w['s5_bbd'], w['s5_cbd'], w['s5_D'], w['s5_w_glu'],
                                 w['s5_b_glu'], B=B, L=L, tb=min(512, L))
    out['s5_re'] = s5r.reshape(B, G, P)
    out['s5_im'] = s5i.reshape(B, G, P)
    x = _out_proj(yc, yd, w['w_out1'], x, tm=to, tn=512)
    x, ffn1 = _conv_ffn(x, w['g_ffn'][1], w['ffn'][1], seq_len=L, tm=tf, prev=st['ffn_conv'] if sample else None,
                        final_g=w['g_final'])
    out['ffn_conv'] = jnp.stack([ffn0, ffn1], axis=0)
    return x.reshape(B, L, D), out


def kernel(x_prompt, x_sample, state_gla_S, state_rglru_h, state_rglru_conv, state_mlstm_C, state_mlstm_n, state_mlstm_m, state_mlstm_conv, state_s5_re, state_s5_im, state_ffn_conv, g_mix0, w_in0, gla_w_alpha2, gla_b_alpha, gla_g_norm, rg_conv_w, rg_conv_b, rg_w_r, rg_b_r, rg_w_i, rg_b_i, rg_lambda, w_out0, g_mix1, w_in1, ml_conv_w, ml_conv_b, ml_wq, ml_wk, ml_wv, ml_w_igate, ml_b_igate, ml_w_fgate, ml_b_fgate, ml_g_norm, ml_skip, s5_lam_re, s5_lam_im, s5_log_dt, s5_B_re, s5_B_im, s5_C_re, s5_C_im, s5_D, s5_w_glu, s5_b_glu, w_out1, g_ffn, ffn_w_up, ffn_conv_w, ffn_conv_b, ffn_w_down, g_final):
    _, H_g, DK, DV = state_gla_S.shape
    _, H_m, DH, _ = state_mlstm_C.shape
    G, P = s5_lam_re.shape
    rank = gla_w_alpha2.shape[0]
    n_main = 2 * H_g * DK + 2 * H_g * DV
    w_in0_main = [w_in0[:, :n_main].astype(bf16), w_in0[:, n_main + rank:].astype(bf16)]
    w_in0_alr = jnp.pad(w_in0[:, n_main:n_main + rank], ((0, 0), (0, LANES - rank))).astype(bf16)
    gla_w2p = jnp.pad(gla_w_alpha2, ((0, LANES - rank), (0, 0)))
    ml_tile = 256
    ml_wg = jnp.pad(jnp.concatenate([ml_w_igate, ml_w_fgate], axis=1), ((0, 0), (0, LANES - 2 * H_m))).astype(bf16)
    ml_bg = jnp.pad(jnp.concatenate([ml_b_igate, ml_b_fgate]), (0, LANES - 2 * H_m)).reshape(1, LANES)
    rows, bbd, cbd = _s5_layouts(s5_lam_re, s5_lam_im, s5_log_dt, s5_B_re, s5_B_im, s5_C_re, s5_C_im)
    w = dict(
        g_mix0=g_mix0, w_in0_main=w_in0_main, w_in0_alr=w_in0_alr, gla_w2p=gla_w2p, gla_b_alpha=gla_b_alpha,
        gla_g_norm=gla_g_norm, gla_dims=(H_g, DK, DV), rg_conv_w=rg_conv_w, rg_conv_b=rg_conv_b, rg_w_r=rg_w_r,
        rg_b_r=rg_b_r, rg_w_i=rg_w_i, rg_b_i=rg_b_i, rg_lambda=rg_lambda, w_out0=w_out0.astype(bf16),
        g_mix1=g_mix1, w_in1=w_in1.astype(bf16), ml_conv_w=ml_conv_w, ml_conv_b=ml_conv_b,
        ml_wq_t=_blockdiag_tiles(ml_wq, ml_tile).astype(bf16), ml_wk_t=_blockdiag_tiles(ml_wk, ml_tile).astype(bf16),
        ml_wv_t=_blockdiag_tiles(ml_wv, ml_tile).astype(bf16), ml_wg=ml_wg, ml_bg=ml_bg, ml_g_norm=ml_g_norm,
        ml_skip=ml_skip, ml_dims=(H_m, DH), s5_dims=(G, P), s5_gc=s5_B_re.shape[2], s5_rows=rows, s5_bbd=bbd,
        s5_cbd=cbd, s5_D=s5_D, s5_w_glu=s5_w_glu.astype(bf16), s5_b_glu=s5_b_glu, w_out1=w_out1.astype(bf16),
        g_ffn=g_ffn, g_final=g_final,
        ffn=_ffn_prepare(ffn_w_up, ffn_conv_w, ffn_conv_b, ffn_w_down))
    st_s = dict(gla_S=state_gla_S, rg_h=state_rglru_h, rg_conv=state_rglru_conv, ml_C=state_mlstm_C,
                ml_n=state_mlstm_n, ml_m=state_mlstm_m, ml_conv=state_mlstm_conv, s5_re=state_s5_re,
                s5_im=state_s5_im, ffn_conv=state_ffn_conv)
    y_p, np_ = _trunk(x_prompt, None, w, sample=False)
    y_s, ns_ = _trunk(x_sample, st_s, w, sample=True)
    names = ('gla_S', 'rg_h', 'rg_conv', 'ml_C', 'ml_n', 'ml_m', 'ml_conv', 's5_re', 's5_im', 'ffn_conv')
    outs = [y_p, y_s]
    for nme in names:
        outs += [np_[nme], ns_[nme]]
    return tuple(outs)
```

```python
import jax
import jax.numpy as jnp
from jax import lax
from jax.experimental import pallas as pl
from jax.experimental.pallas import tpu as pltpu

f32 = jnp.float32
bf16 = jnp.bfloat16

EPS = 1e-6
CHUNK = 64
GLA_TAU = 16.0
RG_C = 8.0
PAST_LEN = 16384
LANES = 128
SUBLANES = 8
HALO = 16
MIB = 1024 * 1024


def _params(sem, vmem_mib):
    return pltpu.CompilerParams(dimension_semantics=sem, vmem_limit_bytes=int(vmem_mib * MIB))


def _dot(a, b):
    return jnp.dot(a, b, preferred_element_type=f32)


def _dot_nt(a, b):
    return lax.dot_general(a, b, (((1,), (1,)), ((), ())), preferred_element_type=f32)


def _dot_tn(a, b):
    return lax.dot_general(a, b, (((0,), (0,)), ((), ())), preferred_element_type=f32)


def _rms(x, g):
    return x * lax.rsqrt(jnp.mean(x * x, axis=-1, keepdims=True) + EPS) * g


def _eye(n):
    return lax.broadcasted_iota(jnp.int32, (n, n), 0) == lax.broadcasted_iota(jnp.int32, (n, n), 1)


def _col_from_row(row):
    n = row.shape[1]
    return jnp.sum(jnp.where(_eye(n), jnp.broadcast_to(row, (n, n)), 0.0), axis=1, keepdims=True)


def _chunk_cumsum(x, chunk):
    pos = lax.broadcasted_iota(jnp.int32, (x.shape[0], 1), 0) % chunk
    step = 1
    while step < chunk:
        x = x + jnp.where(pos >= step, pltpu.roll(x, step, 0), 0.0)
        step *= 2
    return x


def _norm_matmul(x, g, ws, *, tm, tn, w_side=None, out_dtype=f32):
    M, D = x.shape
    nblk = [w.shape[1] // tn for w in ws]
    start = [sum(nblk[:s]) for s in range(len(ws) + 1)]
    N = tn * start[-1]
    rc = min(tm, 256)
    side = w_side is not None

    def body(*refs):
        x_ref, g_ref = refs[:2]
        w_refs = refs[2:2 + len(ws)]
        rest = refs[2 + len(ws):]
        if side:
            ws_ref, o_ref, os_ref, xn_ref = rest
        else:
            o_ref, xn_ref = rest
        j = pl.program_id(1)

        @pl.when(j == 0)
        def _():
            def chunk(r, c):
                rows = pl.ds(pl.multiple_of(r * rc, rc), rc)
                xn_ref[rows, :] = _rms(x_ref[rows, :], g_ref[...]).astype(bf16)
                return c
            lax.fori_loop(0, tm // rc, chunk, 0)
            if side:
                os_ref[...] = _dot(xn_ref[...], ws_ref[...])

        if len(ws) == 1:
            o_ref[...] = _dot(xn_ref[...], w_refs[0][...]).astype(out_dtype)
        else:
            for s, w_ref in enumerate(w_refs):
                @pl.when(jnp.logical_and(j >= start[s], j < start[s + 1]))
                def _(w_ref=w_ref):
                    o_ref[...] = _dot(xn_ref[...], w_ref[...]).astype(out_dtype)

    in_specs = [pl.BlockSpec((tm, D), lambda i, j: (i, 0)),
                pl.BlockSpec((1, D), lambda i, j: (0, 0))]
    in_specs += [pl.BlockSpec((D, tn), lambda i, j, s=s: (0, jnp.clip(j - start[s], 0, nblk[s] - 1)))
                 for s in range(len(ws))]
    out_specs = [pl.BlockSpec((tm, tn), lambda i, j: (i, j))]
    out_shape = [jax.ShapeDtypeStruct((M, N), out_dtype)]
    args = [x, g.reshape(1, D)] + list(ws)
    if side:
        ns = w_side.shape[1]
        in_specs.append(pl.BlockSpec((D, ns), lambda i, j: (0, 0)))
        out_specs.append(pl.BlockSpec((tm, ns), lambda i, j: (i, 0)))
        out_shape.append(jax.ShapeDtypeStruct((M, ns), f32))
        args.append(w_side)
    outs = pl.pallas_call(
        body,
        grid=(M // tm, N // tn),
        in_specs=in_specs,
        out_specs=out_specs,
        out_shape=out_shape,
        scratch_shapes=[pltpu.VMEM((tm, D), bf16)],
        compiler_params=_params(("parallel", "arbitrary"), 56),
        name="norm_matmul",
    )(*args)
    return outs if side else outs[0]


def _out_proj(ya, yb, w, res, *, tm, tn):
    M, Ka = ya.shape
    Kb = yb.shape[1]
    N = w.shape[1]
    assert Ka == Kb and w.shape[0] == Ka + Kb

    def body(ya_ref, yb_ref, wa_ref, wb_ref, r_ref, o_ref):
        o_ref[...] = (r_ref[...] + _dot(ya_ref[...].astype(bf16), wa_ref[...])
                      + _dot(yb_ref[...].astype(bf16), wb_ref[...]))

    return pl.pallas_call(
        body,
        grid=(M // tm, N // tn),
        in_specs=[pl.BlockSpec((tm, Ka), lambda i, j: (i, 0)),
                  pl.BlockSpec((tm, Kb), lambda i, j: (i, 0)),
                  pl.BlockSpec((Ka, tn), lambda i, j: (0, j)),
                  pl.BlockSpec((Kb, tn), lambda i, j: (1, j)),
                  pl.BlockSpec((tm, tn), lambda i, j: (i, j))],
        out_specs=pl.BlockSpec((tm, tn), lambda i, j: (i, j)),
        out_shape=jax.ShapeDtypeStruct((M, N), f32),
        compiler_params=_params(("parallel", "arbitrary"), 48),
        name="out_proj",
    )(ya, yb, w, w, res)


FFN_STEP = 512
FFN_ROW_CHUNK = 512


def _ffn_to_steps(a, F):
    nsteps = -(-F // FFN_STEP)
    lead = a.shape[:-1]
    nd = len(lead)
    gv = a.reshape(lead + (2, F))
    gv = jnp.pad(gv, [(0, 0)] * (nd + 1) + [(0, nsteps * FFN_STEP - F)])
    gv = gv.reshape(lead + (2, nsteps, FFN_STEP))
    gv = jnp.transpose(gv, (nd + 1,) + tuple(range(nd)) + (nd, nd + 2))
    return gv.reshape((nsteps,) + lead + (2 * FFN_STEP,))


def _ffn_prepare(w_up, conv_w, conv_b, w_down):
    NL, F, D = w_down.shape
    assert F % LANES == 0 and FFN_STEP % LANES == 0
    nt = F // LANES
    tp = FFN_STEP // LANES
    nsteps = -(-F // FFN_STEP)

    def body(*refs):
        g_in, v_in, d_in = refs[:tp], refs[tp:2 * tp], refs[2 * tp:3 * tp]
        wu_o, wd_o = refs[3 * tp:]
        for t in range(tp):
            keep = tp * pl.program_id(1) + t < nt
            cols = slice(t * LANES, (t + 1) * LANES)
            wu_o[:, cols] = jnp.where(keep, g_in[t][...], 0.0).astype(bf16)
            wu_o[:, FFN_STEP + t * LANES:FFN_STEP + (t + 1) * LANES] = jnp.where(keep, v_in[t][...], 0.0).astype(bf16)
            wd_o[cols, :] = jnp.where(keep, d_in[t][...], 0.0).astype(bf16)

    tile = lambda j, t: jnp.minimum(tp * j + t, nt - 1)
    in_specs = ([pl.BlockSpec((None, D, LANES), lambda l, j, t=t: (l, 0, tile(j, t))) for t in range(tp)]
                + [pl.BlockSpec((None, D, LANES), lambda l, j, t=t: (l, 0, nt + tile(j, t))) for t in range(tp)]
                + [pl.BlockSpec((None, LANES, D), lambda l, j, t=t: (l, tile(j, t), 0)) for t in range(tp)])
    wu, wd = pl.pallas_call(
        body,
        grid=(NL, nsteps),
        in_specs=in_specs,
        out_specs=[pl.BlockSpec((None, None, D, 2 * FFN_STEP), lambda l, j: (l, j, 0, 0)),
                   pl.BlockSpec((None, None, FFN_STEP, D), lambda l, j: (l, j, 0, 0))],
        out_shape=[jax.ShapeDtypeStruct((NL, nsteps, D, 2 * FFN_STEP), bf16),
                   jax.ShapeDtypeStruct((NL, nsteps, FFN_STEP, D), bf16)],
        compiler_params=_params(("parallel", "parallel"), 40),
        name="ffn_weight_layout",
    )(*([w_up] * (2 * tp) + [w_down] * tp))
    return [dict(wu=wu, wd=wd, layer=l, F=F, cw=_ffn_to_steps(conv_w[l], F),
                 cb=_ffn_to_steps(conv_b[l], F).reshape(nsteps, 1, 2 * FFN_STEP)) for l in range(NL)]


def _conv_ffn(x, g, fw, *, seq_len, tm, prev=None, final_g=None):
    M, D = x.shape
    F = fw['F']
    layer = fw['layer']
    nsteps = fw['wu'].shape[1]
    sample = prev is not None
    rc = min(tm, FFN_ROW_CHUNK)
    tiles_per_seq = max(seq_len // tm, 1)
    W2 = 2 * FFN_STEP

    def body(*refs):
        it = iter(refs)
        x_ref = next(it)
        xh_ref = None if sample else next(it)
        prev_refs = [next(it) for _ in range(2 * FFN_STEP // LANES)] if sample else None
        g_ref = next(it)
        wu_ref, cw_ref, cb_ref, wd_ref = next(it), next(it), next(it), next(it)
        fg_ref = next(it) if final_g is not None else None
        o_ref, tail_g_ref, tail_v_ref = next(it), next(it), next(it)
        xn_ref, up_scr = next(it), next(it)
        i = pl.program_id(0)
        j = pl.program_id(1)

        @pl.when(j == 0)
        def _():
            if sample:
                xn_ref[0:HALO, :] = jnp.zeros((HALO, D), bf16)
            else:
                keep = (i % tiles_per_seq != 0).astype(f32)
                hist = _rms(xh_ref[...], g_ref[...]) * keep
                xn_ref[0:HALO, :] = jnp.concatenate([jnp.zeros_like(hist), hist], axis=0).astype(bf16)

            def chunk(r, c):
                rows = pl.ds(pl.multiple_of(r * rc, rc), rc)
                xr = x_ref[rows, :]
                o_ref[rows, :] = xr
                xn_ref[pl.ds(pl.multiple_of(HALO + r * rc, HALO), rc), :] = _rms(xr, g_ref[...]).astype(bf16)
                return c
            lax.fori_loop(0, tm // rc, chunk, 0)

        cw = cw_ref[0]
        cb = cb_ref[0]
        nchunk = tm // rc
        for r in range(nchunk):
            if sample:
                up_scr[r, HALO:, :] = _dot(xn_ref[pl.ds(HALO + r * rc, rc), :], wu_ref[...])
            else:
                up_scr[r] = _dot(xn_ref[pl.ds(r * rc, rc + HALO), :], wu_ref[...])
        for r in range(nchunk):
            rows = pl.ds(r * rc, rc)
            up = up_scr[r, HALO:, :]
            if sample:
                p0 = jnp.concatenate([p[rows, 0, :] for p in prev_refs], axis=1)
                p1 = jnp.concatenate([p[rows, 1, :] for p in prev_refs], axis=1)
                conv = cb + cw[0:1] * p0 + cw[1:2] * p1 + cw[2:3] * up
                tail_g_ref[rows, :] = up[:, :FFN_STEP]
                tail_v_ref[rows, :] = up[:, FFN_STEP:]
            else:
                conv = (cb + cw[0:1] * up_scr[r, HALO - 2:HALO - 2 + rc, :]
                        + cw[1:2] * up_scr[r, HALO - 1:HALO - 1 + rc, :] + cw[2:3] * up)
                if r == nchunk - 1:
                    tail_g_ref[...] = up[rc - SUBLANES:, :FFN_STEP]
                    tail_v_ref[...] = up[rc - SUBLANES:, FFN_STEP:]
            h = jax.nn.gelu(conv[:, :FFN_STEP]) * conv[:, FFN_STEP:]
            o_ref[rows, :] += _dot(h.astype(bf16), wd_ref[...])

        if final_g is not None:
            @pl.when(j == nsteps - 1)
            def _():
                def chunk2(r, c):
                    rows = pl.ds(pl.multiple_of(r * rc, rc), rc)
                    o_ref[rows, :] = _rms(o_ref[rows, :], fg_ref[...])
                    return c
                lax.fori_loop(0, tm // rc, chunk2, 0)

    in_specs = [pl.BlockSpec((tm, D), lambda i, j: (i, 0))]
    args = [x]
    if sample:
        nt = F // LANES
        tp = FFN_STEP // LANES
        last = 2 * nt - 1
        for off in list(range(tp)) + [nt + t for t in range(tp)]:
            in_specs.append(pl.BlockSpec((None, tm, 2, LANES),
                                         lambda i, j, off=off: (layer, i, 0, jnp.minimum(tp * j + off, last))))
            args.append(prev)
    else:
        in_specs.append(pl.BlockSpec((SUBLANES, D), lambda i, j: (jnp.maximum(i * (tm // SUBLANES) - 1, 0), 0)))
        args.append(x)
    in_specs += [pl.BlockSpec((1, D), lambda i, j: (0, 0)),
                 pl.BlockSpec((None, None, D, W2), lambda i, j: (layer, j, 0, 0)),
                 pl.BlockSpec((1, 3, W2), lambda i, j: (j, 0, 0)),
                 pl.BlockSpec((1, 1, W2), lambda i, j: (j, 0, 0)),
                 pl.BlockSpec((None, None, FFN_STEP, D), lambda i, j: (layer, j, 0, 0))]
    args += [g.reshape(1, D), fw['wu'], fw['cw'], fw['cb'], fw['wd']]
    if final_g is not None:
        in_specs.append(pl.BlockSpec((1, D), lambda i, j: (0, 0)))
        args.append(final_g.reshape(1, D))
    FP = nsteps * FFN_STEP
    if sample:
        tail_shape = (M, FP)
        tail_spec = pl.BlockSpec((tm, FFN_STEP), lambda i, j: (i, j))
    else:
        tail_shape = (M // tm, SUBLANES, FP)
        tail_spec = pl.BlockSpec((None, SUBLANES, FFN_STEP), lambda i, j: (i, 0, j))
    out, tail_g, tail_v = pl.pallas_call(
        body,
        grid=(M // tm, nsteps),
        in_specs=in_specs,
        out_specs=[pl.BlockSpec((tm, D), lambda i, j: (i, 0)), tail_spec, tail_spec],
        out_shape=[jax.ShapeDtypeStruct((M, D), f32), jax.ShapeDtypeStruct(tail_shape, f32),
                   jax.ShapeDtypeStruct(tail_shape, f32)],
        scratch_shapes=[pltpu.VMEM((tm + HALO, D), bf16), pltpu.VMEM((tm // rc, rc + HALO, W2), f32)],
        compiler_params=_params(("parallel", "arbitrary"), 56),
        name="conv_ffn_sample" if sample else "conv_ffn",
    )(*args)
    if sample:
        up_rows = jnp.concatenate([tail_g[:, :F], tail_v[:, :F]], axis=1)
        new_buf = jnp.stack([prev[layer, :, 1, :], up_rows], axis=1)
    else:
        nseq = M // seq_len
        pick = lambda t: t.reshape(nseq, tiles_per_seq, SUBLANES, FP)[:, -1, SUBLANES - 2:, :F]
        new_buf = jnp.concatenate([pick(tail_g), pick(tail_v)], axis=-1)
    return out, new_buf


def _gla_prompt(p0, alr, w2p, b_alpha, g_norm, *, B, L, H, DK, DV, tb):
    NT = L // tb
    NC = tb // CHUNK
    C = CHUNK
    scale = DK ** -0.5
    qk_blocks = H
    v_off = 2 * H * DK // DV

    def body(q_ref, k_ref, v_ref, r_ref, a_ref, w2_ref, ba_ref, gn_ref, y_ref, s_out_ref, s_scr):
        t = pl.program_id(2)

        @pl.when(t == 0)
        def _():
            s_scr[...] = jnp.zeros_like(s_scr)

        z = _dot(a_ref[...].astype(bf16), w2_ref[...].astype(bf16)) + ba_ref[...]
        gl = jax.nn.log_sigmoid(z) * (1.0 / GLA_TAU)
        bc3 = _chunk_cumsum(gl, C).reshape(NC, C, DK)
        bl3 = bc3[:, C - 1:C, :]
        q3 = (q_ref[...].astype(f32) * scale).reshape(NC, C, DK)
        k3 = k_ref[...].astype(f32).reshape(NC, C, DK)
        v3 = v_ref[...].astype(bf16).reshape(NC, C, DV)
        qd3 = (q3 * jnp.exp(bc3)).astype(bf16)
        kd3 = (k3 * jnp.exp(-bc3)).astype(bf16)
        kdec3 = (k3 * jnp.exp(bl3 - bc3)).astype(bf16)
        causal = (lax.broadcasted_iota(jnp.int32, (1, C, C), 1) >= lax.broadcasted_iota(jnp.int32, (1, C, C), 2))
        att = jnp.where(causal, jnp.einsum('cik,cjk->cij', qd3, kd3, preferred_element_type=f32), 0.0)
        intra = jnp.einsum('cij,cjv->civ', att.astype(bf16), v3, preferred_element_type=f32)
        ds = jnp.einsum('cjk,cjv->ckv', kdec3, v3, preferred_element_type=f32)
        s = s_scr[...]
        s_in = []
        for c in range(NC):
            s_in.append(s)
            s = _col_from_row(jnp.exp(bl3[c])) * s + ds[c]
        s_scr[...] = s
        s_all = jnp.stack(s_in).astype(bf16)
        o = (intra + jnp.einsum('cik,ckv->civ', qd3, s_all, preferred_element_type=f32)).reshape(tb, DV)
        rr = r_ref[...].astype(f32)
        y_ref[...] = (_rms(o, gn_ref[...]) * (rr * jax.nn.sigmoid(rr))).astype(bf16)

        @pl.when(t == NT - 1)
        def _():
            s_out_ref[0, 0] = s
    return pl.pallas_call(
        body,
        grid=(B, H, NT),
        in_specs=[pl.BlockSpec((tb, DK), lambda b, h, t: (b * NT + t, h)),
                  pl.BlockSpec((tb, DK), lambda b, h, t: (b * NT + t, qk_blocks + h)),
                  pl.BlockSpec((tb, DV), lambda b, h, t: (b * NT + t, v_off + h)),
                  pl.BlockSpec((tb, DV), lambda b, h, t: (b * NT + t, v_off + H + h)),
                  pl.BlockSpec((tb, LANES), lambda b, h, t: (b * NT + t, 0)),
                  pl.BlockSpec((LANES, DK), lambda b, h, t: (0, h)),
                  pl.BlockSpec((1, DK), lambda b, h, t: (0, h)),
                  pl.BlockSpec((1, DV), lambda b, h, t: (0, h))],
        out_specs=[pl.BlockSpec((tb, DV), lambda b, h, t: (b * NT + t, h)),
                   pl.BlockSpec((1, 1, DK, DV), lambda b, h, t: (b, h, 0, 0))],
        out_shape=[jax.ShapeDtypeStruct((B * L, H * DV), bf16),
                   jax.ShapeDtypeStruct((B, H, DK, DV), f32)],
        scratch_shapes=[pltpu.VMEM((DK, DV), f32)],
        compiler_params=_params(("parallel", "parallel", "arbitrary"), 32),
        name="gla_prompt",
    )(p0, p0, p0, p0, alr, w2p, b_alpha.reshape(1, -1), g_norm.reshape(1, -1))


def _gla_sample(p0, alr, w2p, b_alpha, g_norm, s0, *, H, DK, DV):
    Bs = p0.shape[0]
    scale = DK ** -0.5
    qkw = H * DK
    vw = H * DV
    assert vw % qkw == 0
    SB = SUBLANES

    def body(q_ref, k_ref, v_ref, r_ref, a_ref, w2_ref, ba_ref, gn_ref, s_ref, y_ref, so_ref, gl_scr):
        z = _dot(a_ref[...].astype(bf16), w2_ref[...].astype(bf16)) + ba_ref[...]
        gl_scr[...] = jax.nn.log_sigmoid(z) * (1.0 / GLA_TAU)

        for h in range(H):
            ks = slice(h * DK, (h + 1) * DK)
            vs = slice(h * DV, (h + 1) * DV)
            a_t = jnp.exp(gl_scr[:, ks]).T
            k_t = k_ref[:, ks].T
            q_t = (q_ref[:, ks] * scale).T
            v = v_ref[:, vs]
            outs = []
            for s in range(SB):
                sn = a_t[:, s:s + 1] * s_ref[s, h] + k_t[:, s:s + 1] * v[s:s + 1, :]
                so_ref[s, h] = sn
                outs.append(jnp.sum(q_t[:, s:s + 1] * sn, axis=0, keepdims=True))
            o = jnp.concatenate(outs, axis=0)
            rr = r_ref[:, vs]
            y_ref[:, vs] = _rms(o, gn_ref[:, vs]) * (rr * jax.nn.sigmoid(rr))

    v_blk = 2 * qkw // vw
    return pl.pallas_call(
        body,
        grid=(Bs // SB,),
        in_specs=[pl.BlockSpec((SB, qkw), lambda b: (b, 0)),
                  pl.BlockSpec((SB, qkw), lambda b: (b, 1)),
                  pl.BlockSpec((SB, vw), lambda b: (b, v_blk)),
                  pl.BlockSpec((SB, vw), lambda b: (b, v_blk + 1)),
                  pl.BlockSpec((SB, LANES), lambda b: (b, 0)),
                  pl.BlockSpec((LANES, qkw), lambda b: (0, 0)),
                  pl.BlockSpec((1, qkw), lambda b: (0, 0)),
                  pl.BlockSpec((1, vw), lambda b: (0, 0)),
                  pl.BlockSpec((SB, H, DK, DV), lambda b: (b, 0, 0, 0))],
        out_specs=[pl.BlockSpec((SB, vw), lambda b: (b, 0)),
                   pl.BlockSpec((SB, H, DK, DV), lambda b: (b, 0, 0, 0))],
        out_shape=[jax.ShapeDtypeStruct((Bs, vw), f32),
                   jax.ShapeDtypeStruct((Bs, H, DK, DV), f32)],
        scratch_shapes=[pltpu.VMEM((SB, qkw), f32)],
        compiler_params=_params(("parallel",), 32),
        name="gla_sample",
    )(p0, p0, p0, p0, alr, w2p, b_alpha.reshape(1, -1), g_norm.reshape(1, -1), s0)


def _rg_gates(xc, wr, br, wi, bi, sp):
    xb = xc.astype(bf16)
    r = jax.nn.sigmoid(_dot(xb, wr) + br)
    i = jax.nn.sigmoid(_dot(xb, wi) + bi)
    log_a = -RG_C * r * sp
    a = jnp.exp(log_a)
    mult = jnp.sqrt(1.0 - a * a)
    return a, mult, i


def _rglru_prompt(p0, conv_w, conv_b, w_r, b_r, w_i, b_i, lam, *, B, L, x_blk, g_blk):
    M = B * L
    NB, BS, _ = w_r.shape
    assert BS == LANES
    rc = min(256, L)

    def body(x_ref, gg_ref, cw_ref, cb_ref, wr_ref, br_ref, wi_ref, bi_ref, lam_ref, y_ref, hl_ref, a_scr, b_scr, xs_scr):
        wr = wr_ref[...].astype(bf16)
        wi = wi_ref[...].astype(bf16)
        sp = jax.nn.softplus(-lam_ref[...])
        cw = cw_ref[...]

        def chunk(c, carry):
            r0 = pl.multiple_of(c * rc, rc)
            rows = pl.ds(r0, rc)
            x = x_ref[rows, :].astype(f32)
            start = (r0 % L) == 0
            prev = x_ref[pl.ds(pl.multiple_of(jnp.maximum(r0 - HALO, 0), HALO), HALO), :].astype(f32)[HALO - SUBLANES:]
            xs_scr[0:SUBLANES, :] = jnp.where(start, 0.0, prev)
            xs_scr[SUBLANES:, :] = x
            xc = (cb_ref[...] + cw[0:1] * xs_scr[SUBLANES - 3:SUBLANES - 3 + rc, :]
                  + cw[1:2] * xs_scr[SUBLANES - 2:SUBLANES - 2 + rc, :]
                  + cw[2:3] * xs_scr[SUBLANES - 1:SUBLANES - 1 + rc, :] + cw[3:4] * x)
            a, mult, ig = _rg_gates(xc, wr, br_ref[...], wi, bi_ref[...], sp)
            first = jnp.logical_and(start, lax.broadcasted_iota(jnp.int32, (rc, 1), 0) == 0)
            mult = jnp.where(first, 1.0, mult)
            a_scr[rows, :] = a
            b_scr[rows, :] = mult * (ig * xc)
            return carry
        lax.fori_loop(0, M // rc, chunk, 0, unroll=2)

        def step(t, hs):
            new = []
            for b in range(B):
                row = pl.ds(b * L + t, 1)
                h = a_scr[row, :] * hs[b] + b_scr[row, :]
                b_scr[row, :] = h
                new.append(h)
            return tuple(new)
        hs = lax.fori_loop(0, L, step, tuple(jnp.zeros((1, LANES), f32) for _ in range(B)), unroll=32)
        hl_ref[...] = jnp.concatenate(hs, axis=0)

        def outc(c, carry):
            rows = pl.ds(pl.multiple_of(c * rc, rc), rc)
            y_ref[rows, :] = (b_scr[rows, :] * jax.nn.gelu(gg_ref[rows, :].astype(f32))).astype(bf16)
            return carry
        lax.fori_loop(0, M // rc, outc, 0)

    W = NB * BS
    return pl.pallas_call(
        body,
        grid=(NB,),
        in_specs=[pl.BlockSpec((M, LANES), lambda n: (0, x_blk + n)),
                  pl.BlockSpec((M, LANES), lambda n: (0, g_blk + n)),
                  pl.BlockSpec((4, LANES), lambda n: (0, n)),
                  pl.BlockSpec((1, LANES), lambda n: (0, n)),
                  pl.BlockSpec((None, BS, BS), lambda n: (n, 0, 0)),
                  pl.BlockSpec((1, LANES), lambda n: (0, n)),
                  pl.BlockSpec((None, BS, BS), lambda n: (n, 0, 0)),
                  pl.BlockSpec((1, LANES), lambda n: (0, n)),
                  pl.BlockSpec((1, LANES), lambda n: (0, n))],
        out_specs=[pl.BlockSpec((M, LANES), lambda n: (0, n)),
                   pl.BlockSpec((B, LANES), lambda n: (0, n))],
        out_shape=[jax.ShapeDtypeStruct((M, W), bf16), jax.ShapeDtypeStruct((B, W), f32)],
        scratch_shapes=[pltpu.VMEM((M, LANES), f32), pltpu.VMEM((M, LANES), f32), pltpu.VMEM((rc + SUBLANES, LANES), f32)],
        compiler_params=_params(("parallel",), 48),
        name="rglru_prompt",
    )(p0, p0, conv_w, conv_b.reshape(1, W), w_r, b_r.reshape(1, W), w_i, b_i.reshape(1, W), lam.reshape(1, W))


def _rglru_sample(p0, conv_state, h0, conv_w, conv_b, w_r, b_r, w_i, b_i, lam, *, x_blk, g_blk):
    Bs = p0.shape[0]
    NB, BS, _ = w_r.shape
    W = NB * BS
    s0, s1, s2 = conv_state[:, 0], conv_state[:, 1], conv_state[:, 2]

    def body(x_ref, gg_ref, s0_ref, s1_ref, s2_ref, h0_ref, cw_ref, cb_ref, wr_ref, br_ref, wi_ref, bi_ref, lam_ref,
             y_ref, h_ref):
        cw = cw_ref[...]
        x = x_ref[...]
        xc = cb_ref[...] + cw[0:1] * s0_ref[...] + cw[1:2] * s1_ref[...] + cw[2:3] * s2_ref[...] + cw[3:4] * x
        sp = jax.nn.softplus(-lam_ref[...])
        a, mult, ig = _rg_gates(xc, wr_ref[...].astype(bf16), br_ref[...], wi_ref[...].astype(bf16), bi_ref[...], sp)
        if PAST_LEN == 0:
            mult = jnp.ones_like(mult)
        h = a * h0_ref[...] + mult * (ig * xc)
        h_ref[...] = h
        y_ref[...] = (h * jax.nn.gelu(gg_ref[...])).astype(bf16)

    blk = lambda n: (0, n)
    vec = pl.BlockSpec((1, LANES), blk)
    mat = pl.BlockSpec((Bs, LANES), blk)
    y, h = pl.pallas_call(
        body,
        grid=(NB,),
        in_specs=[pl.BlockSpec((Bs, LANES), lambda n: (0, x_blk + n)),
                  pl.BlockSpec((Bs, LANES), lambda n: (0, g_blk + n)),
                  mat, mat, mat, mat,
                  pl.BlockSpec((4, LANES), blk), vec,
                  pl.BlockSpec((None, BS, BS), lambda n: (n, 0, 0)), vec,
                  pl.BlockSpec((None, BS, BS), lambda n: (n, 0, 0)), vec, vec],
        out_specs=[mat, mat],
        out_shape=[jax.ShapeDtypeStruct((Bs, W), bf16), jax.ShapeDtypeStruct((Bs, W), f32)],
        compiler_params=_params(("parallel",), 32),
        name="rglru_sample",
    )(p0, p0, s0, s1, s2, h0, conv_w, conv_b.reshape(1, W), w_r, b_r.reshape(1, W), w_i, b_i.reshape(1, W),
      lam.reshape(1, W))
    return y, h


def _blockdiag_tiles(w, tile):
    nblk, bs, _ = w.shape
    per = tile // bs
    nt = nblk // per
    rows = jnp.tile(w.reshape(nt, tile, bs), (1, 1, per))
    on_diag = (jnp.arange(tile)[:, None] // bs) == (jnp.arange(tile)[None, :] // bs)
    return jnp.where(on_diag[None], rows, 0.0)


def _mlstm_stage1(p1, conv_w, conv_b, wq_t, wk_t, wv_t, wg, bg, *, seq_len, tr, H, DH, conv_state=None):
    M = p1.shape[0]
    W = H * DH
    NTL, TL, _ = wq_t.shape
    sample = conv_state is not None
    tiles_per_seq = max(seq_len // tr, 1)
    kscale = DH ** -0.5

    def body(*refs):
        it = iter(refs)
        x_ref = next(it)
        if sample:
            s0_ref, s1_ref, s2_ref = next(it), next(it), next(it)
        else:
            xh_ref = next(it)
        cw_ref, cb_ref, wq_ref, wk_ref, wv_ref, wg_ref, bg_ref = (next(it) for _ in range(7))
        q_ref, k_ref, v_ref, g_ref, xc_ref = (next(it) for _ in range(5))
        cw = cw_ref[...]
        x = x_ref[...].astype(f32)
        if sample:
            conv = cb_ref[...] + cw[0:1] * s0_ref[...] + cw[1:2] * s1_ref[...] + cw[2:3] * s2_ref[...] + cw[3:4] * x
        else:
            xs_scr = next(it)
            keep = (pl.program_id(0) % tiles_per_seq != 0).astype(f32)
            xs_scr[0:SUBLANES, :] = xh_ref[...].astype(f32)[HALO - SUBLANES:] * keep
            xs_scr[SUBLANES:, :] = x
            conv = (cb_ref[...] + cw[0:1] * xs_scr[SUBLANES - 3:SUBLANES - 3 + tr, :]
                    + cw[1:2] * xs_scr[SUBLANES - 2:SUBLANES - 2 + tr, :]
                    + cw[2:3] * xs_scr[SUBLANES - 1:SUBLANES - 1 + tr, :] + cw[3:4] * x)
        xc = conv * jax.nn.sigmoid(conv)
        xc_ref[...] = xc.astype(xc_ref.dtype)
        xcb = xc.astype(bf16)
        xb = x.astype(bf16)
        qs, ks, vs = [], [], []
        for t in range(NTL):
            cs = slice(t * TL, (t + 1) * TL)
            qs.append(_dot(xcb[:, cs], wq_ref[t]))
            ks.append(_dot(xcb[:, cs], wk_ref[t]) * kscale)
            vs.append(_dot(xb[:, cs], wv_ref[t]))
        q = jnp.concatenate(qs, axis=1)
        k = jnp.concatenate(ks, axis=1)
        v = jnp.concatenate(vs, axis=1)
        q_ref[...] = q.astype(q_ref.dtype)
        k_ref[...] = k.astype(k_ref.dtype)
        v_ref[...] = v.astype(v_ref.dtype)
        gt = (_dot(q.astype(bf16), wg_ref[0:W, :]) + _dot(k.astype(bf16), wg_ref[W:2 * W, :])
              + _dot(v.astype(bf16), wg_ref[2 * W:3 * W, :]) + bg_ref[...])
        lane = lax.broadcasted_iota(jnp.int32, gt.shape, 1)
        g_ref[...] = jnp.where(jnp.logical_and(lane >= H, lane < 2 * H), jax.nn.log_sigmoid(gt), gt)

    row = lambda i: (i, 0)
    const2 = lambda i: (0, 0)
    const3 = lambda i: (0, 0, 0)
    in_specs = [pl.BlockSpec((tr, W), row)]
    args = [p1]
    if sample:
        in_specs += [pl.BlockSpec((tr, W), row)] * 3
        args += [conv_state[:, 0], conv_state[:, 1], conv_state[:, 2]]
    else:
        in_specs.append(pl.BlockSpec((HALO, W), lambda i: (jnp.maximum(i * (tr // HALO) - 1, 0), 0)))
        args.append(p1)
    in_specs += [pl.BlockSpec((4, W), const2), pl.BlockSpec((1, W), const2),
                 pl.BlockSpec((NTL, TL, TL), const3), pl.BlockSpec((NTL, TL, TL), const3),
                 pl.BlockSpec((NTL, TL, TL), const3),
                 pl.BlockSpec((3 * W, LANES), const2), pl.BlockSpec((1, LANES), const2)]
    args += [conv_w, conv_b.reshape(1, W), wq_t, wk_t, wv_t, wg, bg]
    return pl.pallas_call(
        body,
        grid=(M // tr,),
        in_specs=in_specs,
        out_specs=[pl.BlockSpec((tr, W), row)] * 3 + [pl.BlockSpec((tr, LANES), row), pl.BlockSpec((tr, W), row)],
        out_shape=[jax.ShapeDtypeStruct((M, W), f32 if sample else bf16)] * 3
        + [jax.ShapeDtypeStruct((M, LANES), f32), jax.ShapeDtypeStruct((M, W), f32 if sample else bf16)],
        scratch_shapes=[] if sample else [pltpu.VMEM((tr + SUBLANES, W), f32)],
        compiler_params=_params(("parallel",), 48),
        name="mlstm_stage1_sample" if sample else "mlstm_stage1",
    )(*args)


def _mlstm_prompt(q, k, v, gates, xc, p1, g_norm, skip, *, B, L, H, DH, tb):
    NT = L // tb
    NC = tb // CHUNK
    C = CHUNK

    def body(q_ref, k_ref, v_ref, g_ref, xc_ref, om_ref, gn_ref, sk_ref,
             y_ref, c_out, n_out, m_out, c_scr, n_scr, m_scr):
        hh = pl.program_id(1)
        t = pl.program_id(2)

        @pl.when(t == 0)
        def _():
            c_scr[...] = jnp.zeros_like(c_scr)
            n_scr[...] = jnp.zeros_like(n_scr)
            m_scr[...] = jnp.zeros_like(m_scr)

        gts = g_ref[...]
        lane = lax.broadcasted_iota(jnp.int32, (tb, LANES), 1)
        i_col = jnp.sum(jnp.where(lane == hh, gts, 0.0), axis=1, keepdims=True)
        b_col = jnp.sum(jnp.where(lane == H + hh, _chunk_cumsum(gts, C), 0.0), axis=1, keepdims=True)
        b3 = b_col.reshape(NC, C, 1)
        i3 = i_col.reshape(NC, C, 1)
        ii = lax.broadcasted_iota(jnp.int32, (1, C, C), 1)
        jj = lax.broadcasted_iota(jnp.int32, (1, C, C), 2)
        eye = ii == jj
        causal = ii >= jj
        as_row = lambda col3: jnp.sum(jnp.where(eye, jnp.broadcast_to(col3, (NC, C, C)), 0.0), axis=1, keepdims=True)
        dmat = jnp.where(causal, b3 - as_row(b3) + as_row(i3), -jnp.inf)
        rmax = jnp.max(dmat, axis=2, keepdims=True)
        b_last = b3[:, C - 1:C, :]
        m_prev = m_scr[...]
        m_in = []
        for c in range(NC):
            m_in.append(m_prev)
            m_prev = jnp.maximum(b_last[c] + m_prev, rmax[c][C - 1:C, :])
        m_scr[...] = m_prev
        inter = b3 + jnp.stack(m_in)
        m_col = jnp.maximum(inter, rmax)
        g_col = jnp.exp(inter - m_col)
        q3 = q_ref[...].reshape(NC, C, DH)
        k3 = k_ref[...].reshape(NC, C, DH)
        v3 = v_ref[...].reshape(NC, C, DH)
        s = jnp.einsum('cid,cjd->cij', q3, k3, preferred_element_type=f32) * jnp.exp(dmat - m_col)
        num = jnp.einsum('cij,cje->cie', s.astype(bf16), v3, preferred_element_type=f32)
        den = jnp.sum(s, axis=2, keepdims=True)
        m_new = m_col[:, C - 1:C, :]
        wk = jnp.exp(b_last - b3 + i3 - m_new)
        gc = jnp.exp(inter[:, C - 1:C, :] - m_new)
        kw = k3.astype(f32) * wk
        dc = jnp.einsum('cse,csd->ced', v3, kw.astype(bf16), preferred_element_type=f32)
        dn = jnp.sum(kw, axis=1, keepdims=True)
        cs = c_scr[...]
        ns = n_scr[...]
        c_in, n_in = [], []
        for c in range(NC):
            c_in.append(cs)
            n_in.append(ns)
            cs = gc[c] * cs + dc[c]
            ns = gc[c] * ns + dn[c]
        c_scr[...] = cs
        n_scr[...] = ns
        c_all = jnp.stack(c_in).astype(bf16)
        n_all = jnp.stack(n_in)
        num = num + g_col * jnp.einsum('cid,ced->cie', q3, c_all, preferred_element_type=f32)
        den = den + g_col * jnp.sum(q3.astype(f32) * n_all, axis=2, keepdims=True)
        hm = (num / jnp.maximum(jnp.abs(den), jnp.exp(-m_col))).reshape(tb, DH)
        y = (_rms(hm, gn_ref[...]) + sk_ref[...] * xc_ref[...].astype(f32)) * jax.nn.sigmoid(om_ref[...].astype(f32))
        y_ref[...] = y.astype(bf16)

        @pl.when(t == NT - 1)
        def _():
            c_out[0, 0] = cs
            n_out[0, 0] = ns
            m_out[0, 0] = m_prev

    blk = lambda b, h, t: (b * NT + t, h)
    W = H * DH
    y, c_new, n_new, m_new = pl.pallas_call(
        body,
        grid=(B, H, NT),
        in_specs=[pl.BlockSpec((tb, DH), blk), pl.BlockSpec((tb, DH), blk), pl.BlockSpec((tb, DH), blk),
                  pl.BlockSpec((tb, LANES), lambda b, h, t: (b * NT + t, 0)),
                  pl.BlockSpec((tb, DH), blk),
                  pl.BlockSpec((tb, DH), lambda b, h, t: (b * NT + t, H + h)),
                  pl.BlockSpec((1, DH), lambda b, h, t: (0, h)),
                  pl.BlockSpec((1, DH), lambda b, h, t: (0, h))],
        out_specs=[pl.BlockSpec((tb, DH), blk),
                   pl.BlockSpec((1, 1, DH, DH), lambda b, h, t: (b, h, 0, 0)),
                   pl.BlockSpec((1, 1, 1, DH), lambda b, h, t: (b, h, 0, 0)),
                   pl.BlockSpec((1, 1, 1, 1), lambda b, h, t: (b, h, 0, 0))],
        out_shape=[jax.ShapeDtypeStruct((B * L, W), bf16),
                   jax.ShapeDtypeStruct((B, H, DH, DH), f32),
                   jax.ShapeDtypeStruct((B, H, 1, DH), f32),
                   jax.ShapeDtypeStruct((B, H, 1, 1), f32)],
        scratch_shapes=[pltpu.VMEM((DH, DH), f32), pltpu.VMEM((1, DH), f32), pltpu.VMEM((1, 1), f32)],
        compiler_params=_params(("parallel", "parallel", "arbitrary"), 32),
        name="mlstm_prompt",
    )(q, k, v, gates, xc, p1, g_norm.reshape(1, W), skip.reshape(1, W))
    return y, c_new, n_new.reshape(B, H, DH), m_new.reshape(B, H)


def _mlstm_sample(q, k, v, gates, xc, p1, g_norm, skip, c0, n0, m0, *, H, DH):
    Bs = q.shape[0]
    W = H * DH
    SB = SUBLANES

    def body(q_ref, k_ref, v_ref, g_ref, xc_ref, om_ref, gn_ref, sk_ref, c_ref, n_ref, m_ref,
             y_ref, c_out, n_out, m_out):
        gts = g_ref[...]
        for h in range(H):
            cs = slice(h * DH, (h + 1) * DH)
            ig = gts[:, h:h + 1]
            fg = gts[:, H + h:H + h + 1]
            inter = fg + m_ref[:, h:h + 1]
            m = jnp.maximum(inter, ig)
            g = jnp.exp(inter - m)
            m_out[:, h:h + 1] = m
            q = q_ref[:, cs]
            kw = k_ref[:, cs] * jnp.exp(ig - m)
            nn = g * n_ref[:, cs] + kw
            n_out[:, cs] = nn
            den = jnp.sum(nn * q, axis=1, keepdims=True)
            v_t = v_ref[:, cs].T
            qb = q.astype(bf16)
            nums = []
            for s in range(SB):
                cn = g[s:s + 1, :] * c_ref[s, h] + v_t[:, s:s + 1] * kw[s:s + 1, :]
                c_out[s, h] = cn
                nums.append(_dot_nt(qb, cn.astype(bf16))[s:s + 1, :])
            num = jnp.concatenate(nums, axis=0)
            hm = num / jnp.maximum(jnp.abs(den), jnp.exp(-m))
            y_ref[:, cs] = ((_rms(hm, gn_ref[:, cs]) + sk_ref[:, cs] * xc_ref[:, cs])
                            * jax.nn.sigmoid(om_ref[:, cs]))

    per = lambda b: (b, 0)
    const2 = lambda b: (0, 0)
    y, c_new, n_new, m_new = pl.pallas_call(
        body,
        grid=(Bs // SB,),
        in_specs=[pl.BlockSpec((SB, W), per), pl.BlockSpec((SB, W), per), pl.BlockSpec((SB, W), per),
                  pl.BlockSpec((SB, LANES), per), pl.BlockSpec((SB, W), per),
                  pl.BlockSpec((SB, W), lambda b: (b, 1)),
                  pl.BlockSpec((1, W), const2), pl.BlockSpec((1, W), const2),
                  pl.BlockSpec((SB, H, DH, DH), lambda b: (b, 0, 0, 0)),
                  pl.BlockSpec((SB, W), per), pl.BlockSpec((SB, H), per)],
        out_specs=[pl.BlockSpec((SB, W), per),
                   pl.BlockSpec((SB, H, DH, DH), lambda b: (b, 0, 0, 0)),
                   pl.BlockSpec((SB, W), per), pl.BlockSpec((SB, H), per)],
        out_shape=[jax.ShapeDtypeStruct((Bs, W), f32), jax.ShapeDtypeStruct((Bs, H, DH, DH), f32),
                   jax.ShapeDtypeStruct((Bs, W), f32), jax.ShapeDtypeStruct((Bs, H), f32)],
        compiler_params=_params(("parallel",), 48),
        name="mlstm_sample",
    )(q, k, v, gates, xc, p1, g_norm.reshape(1, W), skip.reshape(1, W), c0, n0.reshape(Bs, W), m0)
    return y, c_new, n_new.reshape(Bs, H, DH), m_new


S5_TILES = 8


def _s5_layouts(lam_re, lam_im, log_dt, b_re, b_im, c_re, c_im):
    G, P = lam_re.shape
    GC = b_re.shape[2]
    T = S5_TILES
    gpt = G // T
    ns = G * P
    flat = lambda a: a.reshape(ns)
    ldt = jnp.broadcast_to(log_dt[:, None], (G, P))
    rows = [flat(a).reshape(T, 1, ns // T) for a in (lam_re, lam_im, ldt)]
    eye = jnp.eye(gpt, dtype=f32)
    bbd = [jnp.einsum('jgpc,gh->jgchp', a.reshape(T, gpt, P, GC), eye).reshape(T, gpt * GC, gpt * P) for a in (b_re, b_im)]
    cbd = [jnp.einsum('jgcp,gh->jgphc', a.reshape(T, gpt, GC, P), eye).reshape(T, gpt * P, gpt * GC) for a in (c_re, c_im)]
    return rows, bbd, cbd


def _s5_discretise(lre, lim, ldt):
    dt = jnp.exp(ldt)
    mag = jnp.exp(dt * lre)
    ar = mag * jnp.cos(dt * lim)
    ai = mag * jnp.sin(dt * lim)
    den = lre * lre + lim * lim
    cr = ((ar - 1.0) * lre + ai * lim) / den
    ci = (ai * lre - (ar - 1.0) * lim) / den
    return ar, ai, cr, ci


def _s5_mixer(p1, u_blk, rows, bbd, cbd, d_skip, w_glu, b_glu, *, B, L, tb, state=None):
    T = S5_TILES
    lre_r, lim_r, ldt_r = rows
    SW = lre_r.shape[2]
    CW = bbd[0].shape[1]
    W = T * CW
    NS = T * SW
    KT = SW // LANES
    sample = state is not None
    NT = 1 if sample else L // tb
    M = B * L

    def body(*refs):
        it = iter(refs)
        u_ref = next(it)
        if sample:
            x0r_ref, x0i_ref = next(it), next(it)
        lre_ref, lim_ref, ldt_ref = next(it), next(it), next(it)
        bre_ref, bim_ref, cre_ref, cim_ref = next(it), next(it), next(it), next(it)
        d_ref, wg_ref, bgl_ref = next(it), next(it), next(it)
        y_ref, xr_out, xi_out = next(it), next(it), next(it)
        bbr, bbi, cpair, ar_scr, ai_scr = (next(it) for _ in range(5))
        if not sample:
            sre, sim, xr_c, xi_c, yacc = (next(it) for _ in range(5))
        first = jnp.logical_and(pl.program_id(0) == 0, pl.program_id(1) == 0)

        @pl.when(first)
        def _():
            for j in range(T):
                ar, ai, cr, ci = _s5_discretise(lre_ref[j], lim_ref[j], ldt_ref[j])
                ar_scr[j] = ar
                ai_scr[j] = ai
                br = bre_ref[j]
                bi = bim_ref[j]
                bbr[j] = (cr * br - ci * bi).astype(bf16)
                bbi[j] = (cr * bi + ci * br).astype(bf16)
            for jp in range(T // 2):
                cpair[jp] = jnp.zeros((4 * SW, 2 * CW), bf16)
                for half in range(2):
                    j = 2 * jp + half
                    r0 = 2 * half * SW
                    cols = slice(half * CW, (half + 1) * CW)
                    cpair[jp, r0:r0 + SW, cols] = cre_ref[j].astype(bf16)
                    cpair[jp, r0 + SW:r0 + 2 * SW, cols] = (-cim_ref[j]).astype(bf16)

        u = u_ref[...].astype(f32)
        ub = u_ref[...].astype(bf16)
        ys = []
        if sample:
            for j in range(T):
                cs = slice(j * SW, (j + 1) * SW)
                uj = ub[:, j * CW:(j + 1) * CW]
                ar = ar_scr[j]
                ai = ai_scr[j]
                x0r = x0r_ref[:, cs]
                x0i = x0i_ref[:, cs]
                xr = ar * x0r - ai * x0i + _dot(uj, bbr[j])
                xi = ar * x0i + ai * x0r + _dot(uj, bbi[j])
                xr_out[:, cs] = xr
                xi_out[:, cs] = xi
                ys += [xr.astype(bf16), xi.astype(bf16)]
            y = jnp.concatenate([_dot(jnp.concatenate(ys[4 * jp:4 * jp + 4], axis=1), cpair[jp]) for jp in range(T // 2)],
                                axis=1)
        else:
            t = pl.program_id(1)

            @pl.when(t == 0)
            def _():
                xr_c[...] = jnp.zeros_like(xr_c)
                xi_c[...] = jnp.zeros_like(xi_c)

            for j in range(T):
                uj = ub[:, j * CW:(j + 1) * CW]
                r = _dot(uj, bbr[j])
                im = _dot(uj, bbi[j])
                for kk in range(KT):
                    sre[kk, pl.ds(j, tb, stride=T), :] = r[:, kk * LANES:(kk + 1) * LANES]
                    sim[kk, pl.ds(j, tb, stride=T), :] = im[:, kk * LANES:(kk + 1) * LANES]
            a_r = [jnp.concatenate([ar_scr[j][:, kk * LANES:(kk + 1) * LANES] for j in range(T)], axis=0) for kk in range(KT)]
            a_i = [jnp.concatenate([ai_scr[j][:, kk * LANES:(kk + 1) * LANES] for j in range(T)], axis=0) for kk in range(KT)]

            def step(s, carry):
                xr, xi = carry
                row = pl.ds(pl.multiple_of(s * T, T), T)
                nr, ni = [], []
                for kk in range(KT):
                    r_ = a_r[kk] * xr[kk] - a_i[kk] * xi[kk] + sre[kk, row, :]
                    i_ = a_r[kk] * xi[kk] + a_i[kk] * xr[kk] + sim[kk, row, :]
                    sre[kk, row, :] = r_
                    sim[kk, row, :] = i_
                    nr.append(r_)
                    ni.append(i_)
                return tuple(nr), tuple(ni)
            xr0 = tuple(xr_c[kk] for kk in range(KT))
            xi0 = tuple(xi_c[kk] for kk in range(KT))
            xr, xi = lax.fori_loop(0, tb, step, (xr0, xi0), unroll=4)
            for kk in range(KT):
                xr_c[kk] = xr[kk]
                xi_c[kk] = xi[kk]

            @pl.when(t == NT - 1)
            def _():
                for kk in range(KT):
                    xr_out[kk] = xr[kk]
                    xi_out[kk] = xi[kk]

            for jp in range(T // 2):
                parts = []
                for j in (2 * jp, 2 * jp + 1):
                    parts += [sre[kk, pl.ds(j, tb, stride=T), :] for kk in range(KT)]
                    parts += [sim[kk, pl.ds(j, tb, stride=T), :] for kk in range(KT)]
                xp = jnp.concatenate(parts, axis=1).astype(bf16)
                yacc[:, 2 * jp * CW:(2 * jp + 2) * CW] = _dot(xp, cpair[jp])
            y = yacc[...]
        ysk = jax.nn.gelu(y + d_ref[...] * u)
        z = _dot(ysk.astype(bf16), wg_ref[...]) + bgl_ref[...]
        y_ref[...] = (ysk * jax.nn.sigmoid(z)).astype(bf16)

    c3 = lambda b, t: (0, 0, 0)
    c2 = lambda b, t: (0, 0)
    in_specs = [pl.BlockSpec((tb, W), lambda b, t: (b * NT + t, u_blk))]
    args = [p1]
    if sample:
        in_specs += [pl.BlockSpec((tb, NS), lambda b, t: (b, 0))] * 2
        args += [state[0], state[1]]
    in_specs += [pl.BlockSpec((T, 1, SW), c3)] * 3
    in_specs += [pl.BlockSpec((T, CW, SW), c3)] * 2 + [pl.BlockSpec((T, SW, CW), c3)] * 2
    in_specs += [pl.BlockSpec((1, W), c2), pl.BlockSpec((W, W), c2), pl.BlockSpec((1, W), c2)]
    args += [lre_r, lim_r, ldt_r, bbd[0], bbd[1], cbd[0], cbd[1], d_skip.reshape(1, W), w_glu, b_glu.reshape(1, W)]
    scratch = [pltpu.VMEM((T, CW, SW), bf16), pltpu.VMEM((T, CW, SW), bf16),
               pltpu.VMEM((T // 2, 4 * SW, 2 * CW), bf16),
               pltpu.VMEM((T, 1, SW), f32), pltpu.VMEM((T, 1, SW), f32)]
    if sample:
        grid = (M // tb, 1)
        st_spec = pl.BlockSpec((tb, NS), lambda b, t: (b, 0))
        st_shape = jax.ShapeDtypeStruct((M, NS), f32)
    else:
        grid = (B, NT)
        st_spec = pl.BlockSpec((None, KT, T, LANES), lambda b, t: (b, 0, 0, 0))
        st_shape = jax.ShapeDtypeStruct((B, KT, T, LANES), f32)
        scratch += [pltpu.VMEM((KT, tb * T, LANES), f32), pltpu.VMEM((KT, tb * T, LANES), f32),
                    pltpu.VMEM((KT, T, LANES), f32), pltpu.VMEM((KT, T, LANES), f32), pltpu.VMEM((tb, W), f32)]
    y, xr, xi = pl.pallas_call(
        body,
        grid=grid,
        in_specs=in_specs,
        out_specs=[pl.BlockSpec((tb, W), lambda b, t: (b * NT + t, 0)), st_spec, st_spec],
        out_shape=[jax.ShapeDtypeStruct((M, W), bf16), st_shape, st_shape],
        scratch_shapes=scratch,
        compiler_params=_params(("arbitrary", "arbitrary"), 56),
        name="s5_sample" if sample else "s5_prompt",
    )(*args)
    if not sample:
        xr = jnp.transpose(xr, (0, 2, 1, 3)).reshape(B, NS)
        xi = jnp.transpose(xi, (0, 2, 1, 3)).reshape(B, NS)
    return y, xr, xi


def _trunk(x3, st, w, *, sample):
    B, L, D = x3.shape
    M = B * L
    x = x3.reshape(M, D)
    H_g, DK, DV = st['gla_S'].shape[1:] if sample else w['gla_dims']
    H_m, DH = w['ml_dims']
    G, P = w['s5_dims']
    tm = M if sample else min(1024, L)
    tf = M if sample else min(1024, L)
    to = M if sample else min(2048, M)
    tb = min(1024, L)
    pdt = f32 if sample else bf16
    out = {}

    p0, alr = _norm_matmul(x, w['g_mix0'], w['w_in0_main'], tm=tm, tn=1024, w_side=w['w_in0_alr'], out_dtype=pdt)
    x_blk = (2 * H_g * DK + 2 * H_g * DV) // LANES
    W_rg = w['rg_lambda'].shape[0]
    g_blk = x_blk + W_rg // LANES
    rg_tail = p0.reshape(B, L, p0.shape[1])[:, max(L - 3, 0):, x_blk * LANES:x_blk * LANES + W_rg].astype(f32)
    if sample:
        ya, out['gla_S'] = _gla_sample(p0, alr, w['gla_w2p'], w['gla_b_alpha'], w['gla_g_norm'], st['gla_S'],
                                       H=H_g, DK=DK, DV=DV)
        yb, out['rg_h'] = _rglru_sample(p0, st['rg_conv'], st['rg_h'], w['rg_conv_w'], w['rg_conv_b'], w['rg_w_r'],
                                        w['rg_b_r'], w['rg_w_i'], w['rg_b_i'], w['rg_lambda'], x_blk=x_blk, g_blk=g_blk)
        out['rg_conv'] = jnp.concatenate([st['rg_conv'][:, 1:], rg_tail], axis=1)
    else:
        ya, out['gla_S'] = _gla_prompt(p0, alr, w['gla_w2p'], w['gla_b_alpha'], w['gla_g_norm'],
                                       B=B, L=L, H=H_g, DK=DK, DV=DV, tb=tb)
        yb, out['rg_h'] = _rglru_prompt(p0, w['rg_conv_w'], w['rg_conv_b'], w['rg_w_r'], w['rg_b_r'], w['rg_w_i'],
                                        w['rg_b_i'], w['rg_lambda'], B=B, L=L, x_blk=x_blk, g_blk=g_blk)
        out['rg_conv'] = rg_tail
    x = _out_proj(ya, yb, w['w_out0'], x, tm=to, tn=512)
    x, ffn0 = _conv_ffn(x, w['g_ffn'][0], w['ffn'][0], seq_len=L, tm=tf, prev=st['ffn_conv'] if sample else None)

    W_ml = H_m * DH
    p1 = _norm_matmul(x, w['g_mix1'], [w['w_in1']], tm=tm, tn=1024, out_dtype=pdt)
    xm_tail = p1.reshape(B, L, p1.shape[1])[:, max(L - 3, 0):, :W_ml].astype(f32)
    q, k, v, gates, xc = _mlstm_stage1(p1, w['ml_conv_w'], w['ml_conv_b'], w['ml_wq_t'], w['ml_wk_t'], w['ml_wv_t'],
                                       w['ml_wg'], w['ml_bg'], seq_len=L, tr=M if sample else min(256, L), H=H_m, DH=DH,
                                       conv_state=st['ml_conv'] if sample else None)
    u_blk = 2 * W_ml // (G * w['s5_gc'])
    if sample:
        yc, out['ml_C'], out['ml_n'], out['ml_m'] = _mlstm_sample(
            q, k, v, gates, xc, p1, w['ml_g_norm'], w['ml_skip'], st['ml_C'], st['ml_n'], st['ml_m'], H=H_m, DH=DH)
        out['ml_conv'] = jnp.concatenate([st['ml_conv'][:, 1:], xm_tail], axis=1)
        yd, s5r, s5i = _s5_mixer(p1, u_blk, w['s5_rows'], w['s5_bbd'], w['s5_cbd'], w['s5_D'], w['s5_w_glu'],
                                 w['s5_b_glu'], B=B, L=1, tb=B,
                                 state=(st['s5_re'].reshape(B, G * P), st['s5_im'].reshape(B, G * P)))
    else:
        yc, out['ml_C'], out['ml_n'], out['ml_m'] = _mlstm_prompt(
            q, k, v, gates, xc, p1, w['ml_g_norm'], w['ml_skip'], B=B, L=L, H=H_m, DH=DH, tb=tb)
        out['ml_conv'] = xm_tail
        yd, s5r, s5i = _s5_mixer(p1, u_blk, w['s5_rows'], w['s5_bbd'], w['s5_cbd'], w['s5_D'], w['s5_w_glu'],
                                 w['s5_b_glu'], B=B, L=L, tb=min(512, L))
    out['s5_re'] = s5r.reshape(B, G, P)
    out['s5_im'] = s5i.reshape(B, G, P)
    x = _out_proj(yc, yd, w['w_out1'], x, tm=to, tn=512)
    x, ffn1 = _conv_ffn(x, w['g_ffn'][1], w['ffn'][1], seq_len=L, tm=tf, prev=st['ffn_conv'] if sample else None,
                        final_g=w['g_final'])
    out['ffn_conv'] = jnp.stack([ffn0, ffn1], axis=0)
    return x.reshape(B, L, D), out


def kernel(x_prompt, x_sample, state_gla_S, state_rglru_h, state_rglru_conv, state_mlstm_C, state_mlstm_n, state_mlstm_m, state_mlstm_conv, state_s5_re, state_s5_im, state_ffn_conv, g_mix0, w_in0, gla_w_alpha2, gla_b_alpha, gla_g_norm, rg_conv_w, rg_conv_b, rg_w_r, rg_b_r, rg_w_i, rg_b_i, rg_lambda, w_out0, g_mix1, w_in1, ml_conv_w, ml_conv_b, ml_wq, ml_wk, ml_wv, ml_w_igate, ml_b_igate, ml_w_fgate, ml_b_fgate, ml_g_norm, ml_skip, s5_lam_re, s5_lam_im, s5_log_dt, s5_B_re, s5_B_im, s5_C_re, s5_C_im, s5_D, s5_w_glu, s5_b_glu, w_out1, g_ffn, ffn_w_up, ffn_conv_w, ffn_conv_b, ffn_w_down, g_final):
    _, H_g, DK, DV = state_gla_S.shape
    _, H_m, DH, _ = state_mlstm_C.shape
    G, P = s5_lam_re.shape
    rank = gla_w_alpha2.shape[0]
    n_main = 2 * H_g * DK + 2 * H_g * DV
    w_in0_main = [w_in0[:, :n_main].astype(bf16), w_in0[:, n_main + rank:].astype(bf16)]
    w_in0_alr = jnp.pad(w_in0[:, n_main:n_main + rank], ((0, 0), (0, LANES - rank))).astype(bf16)
    gla_w2p = jnp.pad(gla_w_alpha2, ((0, LANES - rank), (0, 0)))
    ml_tile = 256
    ml_wg = jnp.pad(jnp.concatenate([ml_w_igate, ml_w_fgate], axis=1), ((0, 0), (0, LANES - 2 * H_m))).astype(bf16)
    ml_bg = jnp.pad(jnp.concatenate([ml_b_igate, ml_b_fgate]), (0, LANES - 2 * H_m)).reshape(1, LANES)
    rows, bbd, cbd = _s5_layouts(s5_lam_re, s5_lam_im, s5_log_dt, s5_B_re, s5_B_im, s5_C_re, s5_C_im)
    w = dict(
        g_mix0=g_mix0, w_in0_main=w_in0_main, w_in0_alr=w_in0_alr, gla_w2p=gla_w2p, gla_b_alpha=gla_b_alpha,
        gla_g_norm=gla_g_norm, gla_dims=(H_g, DK, DV), rg_conv_w=rg_conv_w, rg_conv_b=rg_conv_b, rg_w_r=rg_w_r,
        rg_b_r=rg_b_r, rg_w_i=rg_w_i, rg_b_i=rg_b_i, rg_lambda=rg_lambda, w_out0=w_out0.astype(bf16),
        g_mix1=g_mix1, w_in1=w_in1.astype(bf16), ml_conv_w=ml_conv_w, ml_conv_b=ml_conv_b,
        ml_wq_t=_blockdiag_tiles(ml_wq, ml_tile).astype(bf16), ml_wk_t=_blockdiag_tiles(ml_wk, ml_tile).astype(bf16),
        ml_wv_t=_blockdiag_tiles(ml_wv, ml_tile).astype(bf16), ml_wg=ml_wg, ml_bg=ml_bg, ml_g_norm=ml_g_norm,
        ml_skip=ml_skip, ml_dims=(H_m, DH), s5_dims=(G, P), s5_gc=s5_B_re.shape[2], s5_rows=rows, s5_bbd=bbd,
        s5_cbd=cbd, s5_D=s5_D, s5_w_glu=s5_w_glu.astype(bf16), s5_b_glu=s5_b_glu, w_out1=w_out1.astype(bf16),
        g_ffn=g_ffn, g_final=g_final,
        ffn=_ffn_prepare(ffn_w_up, ffn_conv_w, ffn_conv_b, ffn_w_down))
    st_s = dict(gla_S=state_gla_S, rg_h=state_rglru_h, rg_conv=state_rglru_conv, ml_C=state_mlstm_C,
                ml_n=state_mlstm_n, ml_m=state_mlstm_m, ml_conv=state_mlstm_conv, s5_re=state_s5_re,
                s5_im=state_s5_im, ffn_conv=state_ffn_conv)
    y_p, np_ = _trunk(x_prompt, None, w, sample=False)
    y_s, ns_ = _trunk(x_sample, st_s, w, sample=True)
    names = ('gla_S', 'rg_h', 'rg_conv', 'ml_C', 'ml_n', 'ml_m', 'ml_conv', 's5_re', 's5_im', 'ffn_conv')
    outs = [y_p, y_s]
    for nme in names:
        outs += [np_[nme], ns_[nme]]
    return tuple(outs)
```

```python
import jax
import jax.numpy as jnp
from jax import lax
from jax.experimental import pallas as pl
from jax.experimental.pallas import tpu as pltpu

f32 = jnp.float32
bf16 = jnp.bfloat16

EPS = 1e-6
CHUNK = 64
GLA_TAU = 16.0
RG_C = 8.0
PAST_LEN = 16384
LANES = 128
SUBLANES = 8
HALO = 16
MIB = 1024 * 1024


def _params(sem, vmem_mib):
    return pltpu.CompilerParams(dimension_semantics=sem, vmem_limit_bytes=int(vmem_mib * MIB))


def _dot(a, b):
    return jnp.dot(a, b, preferred_element_type=f32)


def _dot_nt(a, b):
    return lax.dot_general(a, b, (((1,), (1,)), ((), ())), preferred_element_type=f32)


def _dot_tn(a, b):
    return lax.dot_general(a, b, (((0,), (0,)), ((), ())), preferred_element_type=f32)


def _rms(x, g):
    return x * lax.rsqrt(jnp.mean(x * x, axis=-1, keepdims=True) + EPS) * g


def _eye(n):
    return lax.broadcasted_iota(jnp.int32, (n, n), 0) == lax.broadcasted_iota(jnp.int32, (n, n), 1)


def _col_from_row(row):
    n = row.shape[1]
    return jnp.sum(jnp.where(_eye(n), jnp.broadcast_to(row, (n, n)), 0.0), axis=1, keepdims=True)


def _chunk_cumsum(x, chunk):
    pos = lax.broadcasted_iota(jnp.int32, (x.shape[0], 1), 0) % chunk
    step = 1
    while step < chunk:
        x = x + jnp.where(pos >= step, pltpu.roll(x, step, 0), 0.0)
        step *= 2
    return x


def _norm_matmul(x, g, ws, *, tm, tn, w_side=None, out_dtype=f32, w_t=False, seg_cols=None):
    M, D = x.shape
    mm = _dot_nt if w_t else _dot
    seg_cols = seg_cols or [w.shape[0 if w_t else 1] for w in ws]
    nblk = [c // tn for c in seg_cols]
    start = [sum(nblk[:s]) for s in range(len(ws) + 1)]
    N = tn * start[-1]
    rc = min(tm, 256)
    side = w_side is not None

    def body(*refs):
        x_ref, g_ref = refs[:2]
        w_refs = refs[2:2 + len(ws)]
        rest = refs[2 + len(ws):]
        if side:
            ws_ref, o_ref, os_ref, xn_ref = rest
        else:
            o_ref, xn_ref = rest
        j = pl.program_id(1)

        @pl.when(j == 0)
        def _():
            def chunk(r, c):
                rows = pl.ds(pl.multiple_of(r * rc, rc), rc)
                xn_ref[rows, :] = _rms(x_ref[rows, :], g_ref[...]).astype(bf16)
                return c
            lax.fori_loop(0, tm // rc, chunk, 0)
            if side:
                os_ref[...] = mm(xn_ref[...], ws_ref[...])

        if len(ws) == 1:
            o_ref[...] = mm(xn_ref[...], w_refs[0][...]).astype(out_dtype)
        else:
            for s, w_ref in enumerate(w_refs):
                @pl.when(jnp.logical_and(j >= start[s], j < start[s + 1]))
                def _(w_ref=w_ref):
                    o_ref[...] = mm(xn_ref[...], w_ref[...]).astype(out_dtype)

    in_specs = [pl.BlockSpec((tm, D), lambda i, j: (i, 0)),
                pl.BlockSpec((1, D), lambda i, j: (0, 0))]
    wblk = lambda j, s: jnp.clip(j - start[s], 0, nblk[s] - 1)
    if w_t:
        in_specs += [pl.BlockSpec((tn, D), lambda i, j, s=s: (wblk(j, s), 0)) for s in range(len(ws))]
    else:
        in_specs += [pl.BlockSpec((D, tn), lambda i, j, s=s: (0, wblk(j, s))) for s in range(len(ws))]
    out_specs = [pl.BlockSpec((tm, tn), lambda i, j: (i, j))]
    out_shape = [jax.ShapeDtypeStruct((M, N), out_dtype)]
    args = [x, g.reshape(1, D)] + list(ws)
    if side:
        ns = w_side.shape[0 if w_t else 1]
        in_specs.append(pl.BlockSpec((ns, D) if w_t else (D, ns), lambda i, j: (0, 0)))
        out_specs.append(pl.BlockSpec((tm, ns), lambda i, j: (i, 0)))
        out_shape.append(jax.ShapeDtypeStruct((M, ns), f32))
        args.append(w_side)
    outs = pl.pallas_call(
        body,
        grid=(M // tm, N // tn),
        in_specs=in_specs,
        out_specs=out_specs,
        out_shape=out_shape,
        scratch_shapes=[pltpu.VMEM((tm, D), bf16)],
        compiler_params=_params(("parallel", "arbitrary"), 56),
        name="norm_matmul",
    )(*args)
    return outs if side else outs[0]


def _out_proj(ya, yb, w, res, *, tm, tn):
    M, Ka = ya.shape
    Kb = yb.shape[1]
    N = w.shape[1]
    assert Ka == Kb and w.shape[0] == Ka + Kb

    def body(ya_ref, yb_ref, wa_ref, wb_ref, r_ref, o_ref):
        o_ref[...] = (r_ref[...] + _dot(ya_ref[...].astype(bf16), wa_ref[...])
                      + _dot(yb_ref[...].astype(bf16), wb_ref[...]))

    return pl.pallas_call(
        body,
        grid=(M // tm, N // tn),
        in_specs=[pl.BlockSpec((tm, Ka), lambda i, j: (i, 0)),
                  pl.BlockSpec((tm, Kb), lambda i, j: (i, 0)),
                  pl.BlockSpec((Ka, tn), lambda i, j: (0, j)),
                  pl.BlockSpec((Kb, tn), lambda i, j: (1, j)),
                  pl.BlockSpec((tm, tn), lambda i, j: (i, j))],
        out_specs=pl.BlockSpec((tm, tn), lambda i, j: (i, j)),
        out_shape=jax.ShapeDtypeStruct((M, N), f32),
        compiler_params=_params(("parallel", "arbitrary"), 48),
        name="out_proj",
    )(ya, yb, w, w, res)


FFN_STEP = 512
FFN_ROW_CHUNK = 512


def _ffn_to_steps(a, F):
    nsteps = -(-F // FFN_STEP)
    lead = a.shape[:-1]
    nd = len(lead)
    gv = a.reshape(lead + (2, F))
    gv = jnp.pad(gv, [(0, 0)] * (nd + 1) + [(0, nsteps * FFN_STEP - F)])
    gv = gv.reshape(lead + (2, nsteps, FFN_STEP))
    gv = jnp.transpose(gv, (nd + 1,) + tuple(range(nd)) + (nd, nd + 2))
    return gv.reshape((nsteps,) + lead + (2 * FFN_STEP,))


def _ffn_prepare(w_up, conv_w, conv_b, w_down):
    NL, F, D = w_down.shape
    assert F % LANES == 0 and FFN_STEP % LANES == 0
    nt = F // LANES
    tp = FFN_STEP // LANES
    nsteps = -(-F // FFN_STEP)

    def body(*refs):
        g_in, v_in, d_in = refs[:tp], refs[tp:2 * tp], refs[2 * tp:3 * tp]
        wu_o, wd_o = refs[3 * tp:]
        for t in range(tp):
            keep = tp * pl.program_id(1) + t < nt
            cols = slice(t * LANES, (t + 1) * LANES)
            wu_o[:, cols] = jnp.where(keep, g_in[t][...], 0.0).astype(bf16)
            wu_o[:, FFN_STEP + t * LANES:FFN_STEP + (t + 1) * LANES] = jnp.where(keep, v_in[t][...], 0.0).astype(bf16)
            wd_o[cols, :] = jnp.where(keep, d_in[t][...], 0.0).astype(bf16)

    tile = lambda j, t: jnp.minimum(tp * j + t, nt - 1)
    in_specs = ([pl.BlockSpec((None, D, LANES), lambda l, j, t=t: (l, 0, tile(j, t))) for t in range(tp)]
                + [pl.BlockSpec((None, D, LANES), lambda l, j, t=t: (l, 0, nt + tile(j, t))) for t in range(tp)]
                + [pl.BlockSpec((None, LANES, D), lambda l, j, t=t: (l, tile(j, t), 0)) for t in range(tp)])
    wu, wd = pl.pallas_call(
        body,
        grid=(NL, nsteps),
        in_specs=in_specs,
        out_specs=[pl.BlockSpec((None, None, D, 2 * FFN_STEP), lambda l, j: (l, j, 0, 0)),
                   pl.BlockSpec((None, None, FFN_STEP, D), lambda l, j: (l, j, 0, 0))],
        out_shape=[jax.ShapeDtypeStruct((NL, nsteps, D, 2 * FFN_STEP), bf16),
                   jax.ShapeDtypeStruct((NL, nsteps, FFN_STEP, D), bf16)],
        compiler_params=_params(("parallel", "parallel"), 40),
        name="ffn_weight_layout",
    )(*([w_up] * (2 * tp) + [w_down] * tp))
    return [dict(wu=wu, wd=wd, layer=l, F=F, cw=_ffn_to_steps(conv_w[l], F),
                 cb=_ffn_to_steps(conv_b[l], F).reshape(nsteps, 1, 2 * FFN_STEP)) for l in range(NL)]


def _conv_ffn(x, g, fw, *, seq_len, tm, prev=None, final_g=None):
    M, D = x.shape
    F = fw['F']
    layer = fw['layer']
    nsteps = fw['wu'].shape[1]
    sample = prev is not None
    rc = min(tm, FFN_ROW_CHUNK)
    tiles_per_seq = max(seq_len // tm, 1)
    W2 = 2 * FFN_STEP

    def body(*refs):
        it = iter(refs)
        x_ref = next(it)
        xh_ref = None if sample else next(it)
        prev_refs = [next(it) for _ in range(2 * FFN_STEP // LANES)] if sample else None
        g_ref = next(it)
        wu_ref, cw_ref, cb_ref, wd_ref = next(it), next(it), next(it), next(it)
        fg_ref = next(it) if final_g is not None else None
        o_ref, tail_g_ref, tail_v_ref = next(it), next(it), next(it)
        xn_ref, up_scr = next(it), next(it)
        i = pl.program_id(0)
        j = pl.program_id(1)

        @pl.when(j == 0)
        def _():
            if sample:
                xn_ref[0:HALO, :] = jnp.zeros((HALO, D), bf16)
            else:
                keep = (i % tiles_per_seq != 0).astype(f32)
                hist = _rms(xh_ref[...], g_ref[...]) * keep
                xn_ref[0:HALO, :] = jnp.concatenate([jnp.zeros_like(hist), hist], axis=0).astype(bf16)

            def chunk(r, c):
                rows = pl.ds(pl.multiple_of(r * rc, rc), rc)
                xr = x_ref[rows, :]
                o_ref[rows, :] = xr
                xn_ref[pl.ds(pl.multiple_of(HALO + r * rc, HALO), rc), :] = _rms(xr, g_ref[...]).astype(bf16)
                return c
            lax.fori_loop(0, tm // rc, chunk, 0)

        cw = cw_ref[0]
        cb = cb_ref[0]
        nchunk = tm // rc
        for r in range(nchunk):
            if sample:
                up_scr[r, HALO:, :] = _dot(xn_ref[pl.ds(HALO + r * rc, rc), :], wu_ref[...])
            else:
                up_scr[r] = _dot(xn_ref[pl.ds(r * rc, rc + HALO), :], wu_ref[...])
        for r in range(nchunk):
            rows = pl.ds(r * rc, rc)
            up = up_scr[r, HALO:, :]
            if sample:
                p0 = jnp.concatenate([p[rows, 0, :] for p in prev_refs], axis=1)
                p1 = jnp.concatenate([p[rows, 1, :] for p in prev_refs], axis=1)
                conv = cb + cw[0:1] * p0 + cw[1:2] * p1 + cw[2:3] * up
                tail_g_ref[rows, :] = up[:, :FFN_STEP]
                tail_v_ref[rows, :] = up[:, FFN_STEP:]
            else:
                conv = (cb + cw[0:1] * up_scr[r, HALO - 2:HALO - 2 + rc, :]
                        + cw[1:2] * up_scr[r, HALO - 1:HALO - 1 + rc, :] + cw[2:3] * up)
                if r == nchunk - 1:
                    tail_g_ref[...] = up[rc - SUBLANES:, :FFN_STEP]
                    tail_v_ref[...] = up[rc - SUBLANES:, FFN_STEP:]
            h = jax.nn.gelu(conv[:, :FFN_STEP]) * conv[:, FFN_STEP:]
            o_ref[rows, :] += _dot(h.astype(bf16), wd_ref[...])

        if final_g is not None:
            @pl.when(j == nsteps - 1)
            def _():
                def chunk2(r, c):
                    rows = pl.ds(pl.multiple_of(r * rc, rc), rc)
                    o_ref[rows, :] = _rms(o_ref[rows, :], fg_ref[...])
                    return c
                lax.fori_loop(0, tm // rc, chunk2, 0)

    in_specs = [pl.BlockSpec((tm, D), lambda i, j: (i, 0))]
    args = [x]
    if sample:
        nt = F // LANES
        tp = FFN_STEP // LANES
        last = 2 * nt - 1
        for off in list(range(tp)) + [nt + t for t in range(tp)]:
            in_specs.append(pl.BlockSpec((None, tm, 2, LANES),
                                         lambda i, j, off=off: (layer, i, 0, jnp.minimum(tp * j + off, last))))
            args.append(prev)
    else:
        in_specs.append(pl.BlockSpec((SUBLANES, D), lambda i, j: (jnp.maximum(i * (tm // SUBLANES) - 1, 0), 0)))
        args.append(x)
    in_specs += [pl.BlockSpec((1, D), lambda i, j: (0, 0)),
                 pl.BlockSpec((None, None, D, W2), lambda i, j: (layer, j, 0, 0)),
                 pl.BlockSpec((1, 3, W2), lambda i, j: (j, 0, 0)),
                 pl.BlockSpec((1, 1, W2), lambda i, j: (j, 0, 0)),
                 pl.BlockSpec((None, None, FFN_STEP, D), lambda i, j: (layer, j, 0, 0))]
    args += [g.reshape(1, D), fw['wu'], fw['cw'], fw['cb'], fw['wd']]
    if final_g is not None:
        in_specs.append(pl.BlockSpec((1, D), lambda i, j: (0, 0)))
        args.append(final_g.reshape(1, D))
    FP = nsteps * FFN_STEP
    if sample:
        tail_shape = (M, FP)
        tail_spec = pl.BlockSpec((tm, FFN_STEP), lambda i, j: (i, j))
    else:
        tail_shape = (M // tm, SUBLANES, FP)
        tail_spec = pl.BlockSpec((None, SUBLANES, FFN_STEP), lambda i, j: (i, 0, j))
    out, tail_g, tail_v = pl.pallas_call(
        body,
        grid=(M // tm, nsteps),
        in_specs=in_specs,
        out_specs=[pl.BlockSpec((tm, D), lambda i, j: (i, 0)), tail_spec, tail_spec],
        out_shape=[jax.ShapeDtypeStruct((M, D), f32), jax.ShapeDtypeStruct(tail_shape, f32),
                   jax.ShapeDtypeStruct(tail_shape, f32)],
        scratch_shapes=[pltpu.VMEM((tm + HALO, D), bf16), pltpu.VMEM((tm // rc, rc + HALO, W2), f32)],
        compiler_params=_params(("parallel", "arbitrary"), 56),
        name="conv_ffn_sample" if sample else "conv_ffn",
    )(*args)
    if sample:
        up_rows = jnp.concatenate([tail_g[:, :F], tail_v[:, :F]], axis=1)
        new_buf = jnp.stack([prev[layer, :, 1, :], up_rows], axis=1)
    else:
        nseq = M // seq_len
        pick = lambda t: t.reshape(nseq, tiles_per_seq, SUBLANES, FP)[:, -1, SUBLANES - 2:, :F]
        new_buf = jnp.concatenate([pick(tail_g), pick(tail_v)], axis=-1)
    return out, new_buf


def _gla_prompt(p0, alr, w2p, b_alpha, g_norm, *, B, L, H, DK, DV, tb):
    NT = L // tb
    NC = tb // CHUNK
    C = CHUNK
    scale = DK ** -0.5
    qk_blocks = H
    v_off = 2 * H * DK // DV

    def body(q_ref, k_ref, v_ref, r_ref, a_ref, w2_ref, ba_ref, gn_ref, y_ref, s_out_ref, s_scr):
        t = pl.program_id(2)

        @pl.when(t == 0)
        def _():
            s_scr[...] = jnp.zeros_like(s_scr)

        z = _dot(a_ref[...].astype(bf16), w2_ref[...].astype(bf16)) + ba_ref[...]
        gl = jax.nn.log_sigmoid(z) * (1.0 / GLA_TAU)
        bc3 = _chunk_cumsum(gl, C).reshape(NC, C, DK)
        bl3 = bc3[:, C - 1:C, :]
        q3 = (q_ref[...].astype(f32) * scale).reshape(NC, C, DK)
        k3 = k_ref[...].astype(f32).reshape(NC, C, DK)
        v3 = v_ref[...].astype(bf16).reshape(NC, C, DV)
        qd3 = (q3 * jnp.exp(bc3)).astype(bf16)
        kd3 = (k3 * jnp.exp(-bc3)).astype(bf16)
        kdec3 = (k3 * jnp.exp(bl3 - bc3)).astype(bf16)
        causal = (lax.broadcasted_iota(jnp.int32, (1, C, C), 1) >= lax.broadcasted_iota(jnp.int32, (1, C, C), 2))
        att = jnp.where(causal, jnp.einsum('cik,cjk->cij', qd3, kd3, preferred_element_type=f32), 0.0)
        intra = jnp.einsum('cij,cjv->civ', att.astype(bf16), v3, preferred_element_type=f32)
        ds = jnp.einsum('cjk,cjv->ckv', kdec3, v3, preferred_element_type=f32)
        s = s_scr[...]
        s_in = []
        for c in range(NC):
            s_in.append(s)
            s = _col_from_row(jnp.exp(bl3[c])) * s + ds[c]
        s_scr[...] = s
        s_all = jnp.stack(s_in).astype(bf16)
        o = (intra + jnp.einsum('cik,ckv->civ', qd3, s_all, preferred_element_type=f32)).reshape(tb, DV)
        rr = r_ref[...].astype(f32)
        y_ref[...] = (_rms(o, gn_ref[...]) * (rr * jax.nn.sigmoid(rr))).astype(bf16)

        @pl.when(t == NT - 1)
        def _():
            s_out_ref[0, 0] = s
    return pl.pallas_call(
        body,
        grid=(B, H, NT),
        in_specs=[pl.BlockSpec((tb, DK), lambda b, h, t: (b * NT + t, h)),
                  pl.BlockSpec((tb, DK), lambda b, h, t: (b * NT + t, qk_blocks + h)),
                  pl.BlockSpec((tb, DV), lambda b, h, t: (b * NT + t, v_off + h)),
                  pl.BlockSpec((tb, DV), lambda b, h, t: (b * NT + t, v_off + H + h)),
                  pl.BlockSpec((tb, LANES), lambda b, h, t: (b * NT + t, 0)),
                  pl.BlockSpec((LANES, DK), lambda b, h, t: (0, h)),
                  pl.BlockSpec((1, DK), lambda b, h, t: (0, h)),
                  pl.BlockSpec((1, DV), lambda b, h, t: (0, h))],
        out_specs=[pl.BlockSpec((tb, DV), lambda b, h, t: (b * NT + t, h)),
                   pl.BlockSpec((1, 1, DK, DV), lambda b, h, t: (b, h, 0, 0))],
        out_shape=[jax.ShapeDtypeStruct((B * L, H * DV), bf16),
                   jax.ShapeDtypeStruct((B, H, DK, DV), f32)],
        scratch_shapes=[pltpu.VMEM((DK, DV), f32)],
        compiler_params=_params(("parallel", "parallel", "arbitrary"), 32),
        name="gla_prompt",
    )(p0, p0, p0, p0, alr, w2p, b_alpha.reshape(1, -1), g_norm.reshape(1, -1))


def _gla_sample(p0, alr, w2p, b_alpha, g_norm, s0, *, H, DK, DV):
    Bs = p0.shape[0]
    scale = DK ** -0.5
    qkw = H * DK
    vw = H * DV
    assert vw % qkw == 0
    SB = SUBLANES

    def body(q_ref, k_ref, v_ref, r_ref, a_ref, w2_ref, ba_ref, gn_ref, s_ref, y_ref, so_ref, gl_scr):
        z = _dot(a_ref[...].astype(bf16), w2_ref[...].astype(bf16)) + ba_ref[...]
        gl_scr[...] = jax.nn.log_sigmoid(z) * (1.0 / GLA_TAU)

        for h in range(H):
            ks = slice(h * DK, (h + 1) * DK)
            vs = slice(h * DV, (h + 1) * DV)
            a_t = jnp.exp(gl_scr[:, ks]).T
            k_t = k_ref[:, ks].T
            q_t = (q_ref[:, ks] * scale).T
            v = v_ref[:, vs]
            outs = []
            for s in range(SB):
                sn = a_t[:, s:s + 1] * s_ref[s, h] + k_t[:, s:s + 1] * v[s:s + 1, :]
                so_ref[s, h] = sn
                outs.append(jnp.sum(q_t[:, s:s + 1] * sn, axis=0, keepdims=True))
            o = jnp.concatenate(outs, axis=0)
            rr = r_ref[:, vs]
            y_ref[:, vs] = _rms(o, gn_ref[:, vs]) * (rr * jax.nn.sigmoid(rr))

    v_blk = 2 * qkw // vw
    return pl.pallas_call(
        body,
        grid=(Bs // SB,),
        in_specs=[pl.BlockSpec((SB, qkw), lambda b: (b, 0)),
                  pl.BlockSpec((SB, qkw), lambda b: (b, 1)),
                  pl.BlockSpec((SB, vw), lambda b: (b, v_blk)),
                  pl.BlockSpec((SB, vw), lambda b: (b, v_blk + 1)),
                  pl.BlockSpec((SB, LANES), lambda b: (b, 0)),
                  pl.BlockSpec((LANES, qkw), lambda b: (0, 0)),
                  pl.BlockSpec((1, qkw), lambda b: (0, 0)),
                  pl.BlockSpec((1, vw), lambda b: (0, 0)),
                  pl.BlockSpec((SB, H, DK, DV), lambda b: (b, 0, 0, 0))],
        out_specs=[pl.BlockSpec((SB, vw), lambda b: (b, 0)),
                   pl.BlockSpec((SB, H, DK, DV), lambda b: (b, 0, 0, 0))],
        out_shape=[jax.ShapeDtypeStruct((Bs, vw), f32),
                   jax.ShapeDtypeStruct((Bs, H, DK, DV), f32)],
        scratch_shapes=[pltpu.VMEM((SB, qkw), f32)],
        compiler_params=_params(("parallel",), 32),
        name="gla_sample",
    )(p0, p0, p0, p0, alr, w2p, b_alpha.reshape(1, -1), g_norm.reshape(1, -1), s0)


def _rg_gates(xc, wr, br, wi, bi, sp):
    xb = xc.astype(bf16)
    r = jax.nn.sigmoid(_dot(xb, wr) + br)
    i = jax.nn.sigmoid(_dot(xb, wi) + bi)
    log_a = -RG_C * r * sp
    a = jnp.exp(log_a)
    mult = jnp.sqrt(1.0 - a * a)
    return a, mult, i


def _rglru_prompt(p0, conv_w, conv_b, w_r, b_r, w_i, b_i, lam, *, B, L, x_blk, g_blk):
    M = B * L
    NB, BS, _ = w_r.shape
    assert BS == LANES
    rc = min(256, L)

    def body(x_ref, gg_ref, cw_ref, cb_ref, wr_ref, br_ref, wi_ref, bi_ref, lam_ref, y_ref, hl_ref, a_scr, b_scr, xs_scr):
        wr = wr_ref[...].astype(bf16)
        wi = wi_ref[...].astype(bf16)
        sp = jax.nn.softplus(-lam_ref[...])
        cw = cw_ref[...]

        def chunk(c, carry):
            r0 = pl.multiple_of(c * rc, rc)
            rows = pl.ds(r0, rc)
            x = x_ref[rows, :].astype(f32)
            start = (r0 % L) == 0
            prev = x_ref[pl.ds(pl.multiple_of(jnp.maximum(r0 - HALO, 0), HALO), HALO), :].astype(f32)[HALO - SUBLANES:]
            xs_scr[0:SUBLANES, :] = jnp.where(start, 0.0, prev)
            xs_scr[SUBLANES:, :] = x
            xc = (cb_ref[...] + cw[0:1] * xs_scr[SUBLANES - 3:SUBLANES - 3 + rc, :]
                  + cw[1:2] * xs_scr[SUBLANES - 2:SUBLANES - 2 + rc, :]
                  + cw[2:3] * xs_scr[SUBLANES - 1:SUBLANES - 1 + rc, :] + cw[3:4] * x)
            a, mult, ig = _rg_gates(xc, wr, br_ref[...], wi, bi_ref[...], sp)
            first = jnp.logical_and(start, lax.broadcasted_iota(jnp.int32, (rc, 1), 0) == 0)
            mult = jnp.where(first, 1.0, mult)
            a_scr[rows, :] = a
            b_scr[rows, :] = mult * (ig * xc)
            return carry
        lax.fori_loop(0, M // rc, chunk, 0, unroll=2)

        def step(t, hs):
            new = []
            for b in range(B):
                row = pl.ds(b * L + t, 1)
                h = a_scr[row, :] * hs[b] + b_scr[row, :]
                b_scr[row, :] = h
                new.append(h)
            return tuple(new)
        hs = lax.fori_loop(0, L, step, tuple(jnp.zeros((1, LANES), f32) for _ in range(B)), unroll=32)
        hl_ref[...] = jnp.concatenate(hs, axis=0)

        def outc(c, carry):
            rows = pl.ds(pl.multiple_of(c * rc, rc), rc)
            y_ref[rows, :] = (b_scr[rows, :] * jax.nn.gelu(gg_ref[rows, :].astype(f32))).astype(bf16)
            return carry
        lax.fori_loop(0, M // rc, outc, 0)

    W = NB * BS
    return pl.pallas_call(
        body,
        grid=(NB,),
        in_specs=[pl.BlockSpec((M, LANES), lambda n: (0, x_blk + n)),
                  pl.BlockSpec((M, LANES), lambda n: (0, g_blk + n)),
                  pl.BlockSpec((4, LANES), lambda n: (0, n)),
                  pl.BlockSpec((1, LANES), lambda n: (0, n)),
                  pl.BlockSpec((None, BS, BS), lambda n: (n, 0, 0)),
                  pl.BlockSpec((1, LANES), lambda n: (0, n)),
                  pl.BlockSpec((None, BS, BS), lambda n: (n, 0, 0)),
                  pl.BlockSpec((1, LANES), lambda n: (0, n)),
                  pl.BlockSpec((1, LANES), lambda n: (0, n))],
        out_specs=[pl.BlockSpec((M, LANES), lambda n: (0, n)),
                   pl.BlockSpec((B, LANES), lambda n: (0, n))],
        out_shape=[jax.ShapeDtypeStruct((M, W), bf16), jax.ShapeDtypeStruct((B, W), f32)],
        scratch_shapes=[pltpu.VMEM((M, LANES), f32), pltpu.VMEM((M, LANES), f32), pltpu.VMEM((rc + SUBLANES, LANES), f32)],
        compiler_params=_params(("parallel",), 48),
        name="rglru_prompt",
    )(p0, p0, conv_w, conv_b.reshape(1, W), w_r, b_r.reshape(1, W), w_i, b_i.reshape(1, W), lam.reshape(1, W))


def _rglru_sample(p0, conv_state, h0, conv_w, conv_b, w_r, b_r, w_i, b_i, lam, *, x_blk, g_blk):
    Bs = p0.shape[0]
    NB, BS, _ = w_r.shape
    W = NB * BS
    s0, s1, s2 = conv_state[:, 0], conv_state[:, 1], conv_state[:, 2]

    def body(x_ref, gg_ref, s0_ref, s1_ref, s2_ref, h0_ref, cw_ref, cb_ref, wr_ref, br_ref, wi_ref, bi_ref, lam_ref,
             y_ref, h_ref):
        cw = cw_ref[...]
        x = x_ref[...]
        xc = cb_ref[...] + cw[0:1] * s0_ref[...] + cw[1:2] * s1_ref[...] + cw[2:3] * s2_ref[...] + cw[3:4] * x
        sp = jax.nn.softplus(-lam_ref[...])
        a, mult, ig = _rg_gates(xc, wr_ref[...].astype(bf16), br_ref[...], wi_ref[...].astype(bf16), bi_ref[...], sp)
        if PAST_LEN == 0:
            mult = jnp.ones_like(mult)
        h = a * h0_ref[...] + mult * (ig * xc)
        h_ref[...] = h
        y_ref[...] = (h * jax.nn.gelu(gg_ref[...])).astype(bf16)

    blk = lambda n: (0, n)
    vec = pl.BlockSpec((1, LANES), blk)
    mat = pl.BlockSpec((Bs, LANES), blk)
    y, h = pl.pallas_call(
        body,
        grid=(NB,),
        in_specs=[pl.BlockSpec((Bs, LANES), lambda n: (0, x_blk + n)),
                  pl.BlockSpec((Bs, LANES), lambda n: (0, g_blk + n)),
                  mat, mat, mat, mat,
                  pl.BlockSpec((4, LANES), blk), vec,
                  pl.BlockSpec((None, BS, BS), lambda n: (n, 0, 0)), vec,
                  pl.BlockSpec((None, BS, BS), lambda n: (n, 0, 0)), vec, vec],
        out_specs=[mat, mat],
        out_shape=[jax.ShapeDtypeStruct((Bs, W), bf16), jax.ShapeDtypeStruct((Bs, W), f32)],
        compiler_params=_params(("parallel",), 32),
        name="rglru_sample",
    )(p0, p0, s0, s1, s2, h0, conv_w, conv_b.reshape(1, W), w_r, b_r.reshape(1, W), w_i, b_i.reshape(1, W),
      lam.reshape(1, W))
    return y, h


def _blockdiag_tiles(w, tile):
    nblk, bs, _ = w.shape
    per = tile // bs
    nt = nblk // per
    rows = jnp.tile(w.reshape(nt, tile, bs), (1, 1, per))
    on_diag = (jnp.arange(tile)[:, None] // bs) == (jnp.arange(tile)[None, :] // bs)
    return jnp.where(on_diag[None], rows, 0.0)


def _mlstm_stage1(p1, conv_w, conv_b, wq_t, wk_t, wv_t, wg, bg, *, seq_len, tr, H, DH, conv_state=None):
    M = p1.shape[0]
    W = H * DH
    NTL, TL, _ = wq_t.shape
    sample = conv_state is not None
    tiles_per_seq = max(seq_len // tr, 1)
    kscale = DH ** -0.5

    def body(*refs):
        it = iter(refs)
        x_ref = next(it)
        if sample:
            s0_ref, s1_ref, s2_ref = next(it), next(it), next(it)
        else:
            xh_ref = next(it)
        cw_ref, cb_ref, wq_ref, wk_ref, wv_ref, wg_ref, bg_ref = (next(it) for _ in range(7))
        q_ref, k_ref, v_ref, g_ref, xc_ref = (next(it) for _ in range(5))
        cw = cw_ref[...]
        x = x_ref[...].astype(f32)
        if sample:
            conv = cb_ref[...] + cw[0:1] * s0_ref[...] + cw[1:2] * s1_ref[...] + cw[2:3] * s2_ref[...] + cw[3:4] * x
        else:
            xs_scr = next(it)
            keep = (pl.program_id(0) % tiles_per_seq != 0).astype(f32)
            xs_scr[0:SUBLANES, :] = xh_ref[...].astype(f32)[HALO - SUBLANES:] * keep
            xs_scr[SUBLANES:, :] = x
            conv = (cb_ref[...] + cw[0:1] * xs_scr[SUBLANES - 3:SUBLANES - 3 + tr, :]
                    + cw[1:2] * xs_scr[SUBLANES - 2:SUBLANES - 2 + tr, :]
                    + cw[2:3] * xs_scr[SUBLANES - 1:SUBLANES - 1 + tr, :] + cw[3:4] * x)
        xc = conv * jax.nn.sigmoid(conv)
        xc_ref[...] = xc.astype(xc_ref.dtype)
        xcb = xc.astype(bf16)
        xb = x.astype(bf16)
        qs, ks, vs = [], [], []
        for t in range(NTL):
            cs = slice(t * TL, (t + 1) * TL)
            qs.append(_dot(xcb[:, cs], wq_ref[t]))
            ks.append(_dot(xcb[:, cs], wk_ref[t]) * kscale)
            vs.append(_dot(xb[:, cs], wv_ref[t]))
        q = jnp.concatenate(qs, axis=1)
        k = jnp.concatenate(ks, axis=1)
        v = jnp.concatenate(vs, axis=1)
        q_ref[...] = q.astype(q_ref.dtype)
        k_ref[...] = k.astype(k_ref.dtype)
        v_ref[...] = v.astype(v_ref.dtype)
        gt = (_dot(q.astype(bf16), wg_ref[0:W, :]) + _dot(k.astype(bf16), wg_ref[W:2 * W, :])
              + _dot(v.astype(bf16), wg_ref[2 * W:3 * W, :]) + bg_ref[...])
        lane = lax.broadcasted_iota(jnp.int32, gt.shape, 1)
        g_ref[...] = jnp.where(jnp.logical_and(lane >= H, lane < 2 * H), jax.nn.log_sigmoid(gt), gt)

    row = lambda i: (i, 0)
    const2 = lambda i: (0, 0)
    const3 = lambda i: (0, 0, 0)
    in_specs = [pl.BlockSpec((tr, W), row)]
    args = [p1]
    if sample:
        in_specs += [pl.BlockSpec((tr, W), row)] * 3
        args += [conv_state[:, 0], conv_state[:, 1], conv_state[:, 2]]
    else:
        in_specs.append(pl.BlockSpec((HALO, W), lambda i: (jnp.maximum(i * (tr // HALO) - 1, 0), 0)))
        args.append(p1)
    in_specs += [pl.BlockSpec((4, W), const2), pl.BlockSpec((1, W), const2),
                 pl.BlockSpec((NTL, TL, TL), const3), pl.BlockSpec((NTL, TL, TL), const3),
                 pl.BlockSpec((NTL, TL, TL), const3),
                 pl.BlockSpec((3 * W, LANES), const2), pl.BlockSpec((1, LANES), const2)]
    args += [conv_w, conv_b.reshape(1, W), wq_t, wk_t, wv_t, wg, bg]
    return pl.pallas_call(
        body,
        grid=(M // tr,),
        in_specs=in_specs,
        out_specs=[pl.BlockSpec((tr, W), row)] * 3 + [pl.BlockSpec((tr, LANES), row), pl.BlockSpec((tr, W), row)],
        out_shape=[jax.ShapeDtypeStruct((M, W), f32 if sample else bf16)] * 3
        + [jax.ShapeDtypeStruct((M, LANES), f32), jax.ShapeDtypeStruct((M, W), f32 if sample else bf16)],
        scratch_shapes=[] if sample else [pltpu.VMEM((tr + SUBLANES, W), f32)],
        compiler_params=_params(("parallel",), 48),
        name="mlstm_stage1_sample" if sample else "mlstm_stage1",
    )(*args)


def _mlstm_prompt(q, k, v, gates, xc, p1, g_norm, skip, *, B, L, H, DH, tb):
    NT = L // tb
    NC = tb // CHUNK
    C = CHUNK

    def body(q_ref, k_ref, v_ref, g_ref, xc_ref, om_ref, gn_ref, sk_ref,
             y_ref, c_out, n_out, m_out, c_scr, n_scr, m_scr):
        hh = pl.program_id(1)
        t = pl.program_id(2)

        @pl.when(t == 0)
        def _():
            c_scr[...] = jnp.zeros_like(c_scr)
            n_scr[...] = jnp.zeros_like(n_scr)
            m_scr[...] = jnp.zeros_like(m_scr)

        gts = g_ref[...]
        lane = lax.broadcasted_iota(jnp.int32, (tb, LANES), 1)
        i_col = jnp.sum(jnp.where(lane == hh, gts, 0.0), axis=1, keepdims=True)
        b_col = jnp.sum(jnp.where(lane == H + hh, _chunk_cumsum(gts, C), 0.0), axis=1, keepdims=True)
        b3 = b_col.reshape(NC, C, 1)
        i3 = i_col.reshape(NC, C, 1)
        ii = lax.broadcasted_iota(jnp.int32, (1, C, C), 1)
        jj = lax.broadcasted_iota(jnp.int32, (1, C, C), 2)
        eye = ii == jj
        causal = ii >= jj
        as_row = lambda col3: jnp.sum(jnp.where(eye, jnp.broadcast_to(col3, (NC, C, C)), 0.0), axis=1, keepdims=True)
        dmat = jnp.where(causal, b3 - as_row(b3) + as_row(i3), -jnp.inf)
        rmax = jnp.max(dmat, axis=2, keepdims=True)
        b_last = b3[:, C - 1:C, :]
        m_prev = m_scr[...]
        m_in = []
        for c in range(NC):
            m_in.append(m_prev)
            m_prev = jnp.maximum(b_last[c] + m_prev, rmax[c][C - 1:C, :])
        m_scr[...] = m_prev
        inter = b3 + jnp.stack(m_in)
        m_col = jnp.maximum(inter, rmax)
        g_col = jnp.exp(inter - m_col)
        q3 = q_ref[...].reshape(NC, C, DH)
        k3 = k_ref[...].reshape(NC, C, DH)
        v3 = v_ref[...].reshape(NC, C, DH)
        s = jnp.einsum('cid,cjd->cij', q3, k3, preferred_element_type=f32) * jnp.exp(dmat - m_col)
        num = jnp.einsum('cij,cje->cie', s.astype(bf16), v3, preferred_element_type=f32)
        den = jnp.sum(s, axis=2, keepdims=True)
        m_new = m_col[:, C - 1:C, :]
        wk = jnp.exp(b_last - b3 + i3 - m_new)
        gc = jnp.exp(inter[:, C - 1:C, :] - m_new)
        kw = k3.astype(f32) * wk
        dc = jnp.einsum('cse,csd->ced', v3, kw.astype(bf16), preferred_element_type=f32)
        dn = jnp.sum(kw, axis=1, keepdims=True)
        cs = c_scr[...]
        ns = n_scr[...]
        c_in, n_in = [], []
        for c in range(NC):
            c_in.append(cs)
            n_in.append(ns)
            cs = gc[c] * cs + dc[c]
            ns = gc[c] * ns + dn[c]
        c_scr[...] = cs
        n_scr[...] = ns
        c_all = jnp.stack(c_in).astype(bf16)
        n_all = jnp.stack(n_in)
        num = num + g_col * jnp.einsum('cid,ced->cie', q3, c_all, preferred_element_type=f32)
        den = den + g_col * jnp.sum(q3.astype(f32) * n_all, axis=2, keepdims=True)
        hm = (num / jnp.maximum(jnp.abs(den), jnp.exp(-m_col))).reshape(tb, DH)
        y = (_rms(hm, gn_ref[...]) + sk_ref[...] * xc_ref[...].astype(f32)) * jax.nn.sigmoid(om_ref[...].astype(f32))
        y_ref[...] = y.astype(bf16)

        @pl.when(t == NT - 1)
        def _():
            c_out[0, 0] = cs
            n_out[0, 0] = ns
            m_out[0, 0] = m_prev

    blk = lambda b, h, t: (b * NT + t, h)
    W = H * DH
    y, c_new, n_new, m_new = pl.pallas_call(
        body,
        grid=(B, H, NT),
        in_specs=[pl.BlockSpec((tb, DH), blk), pl.BlockSpec((tb, DH), blk), pl.BlockSpec((tb, DH), blk),
                  pl.BlockSpec((tb, LANES), lambda b, h, t: (b * NT + t, 0)),
                  pl.BlockSpec((tb, DH), blk),
                  pl.BlockSpec((tb, DH), lambda b, h, t: (b * NT + t, H + h)),
                  pl.BlockSpec((1, DH), lambda b, h, t: (0, h)),
                  pl.BlockSpec((1, DH), lambda b, h, t: (0, h))],
        out_specs=[pl.BlockSpec((tb, DH), blk),
                   pl.BlockSpec((1, 1, DH, DH), lambda b, h, t: (b, h, 0, 0)),
                   pl.BlockSpec((1, 1, 1, DH), lambda b, h, t: (b, h, 0, 0)),
                   pl.BlockSpec((1, 1, 1, 1), lambda b, h, t: (b, h, 0, 0))],
        out_shape=[jax.ShapeDtypeStruct((B * L, W), bf16),
                   jax.ShapeDtypeStruct((B, H, DH, DH), f32),
                   jax.ShapeDtypeStruct((B, H, 1, DH), f32),
                   jax.ShapeDtypeStruct((B, H, 1, 1), f32)],
        scratch_shapes=[pltpu.VMEM((DH, DH), f32), pltpu.VMEM((1, DH), f32), pltpu.VMEM((1, 1), f32)],
        compiler_params=_params(("parallel", "parallel", "arbitrary"), 32),
        name="mlstm_prompt",
    )(q, k, v, gates, xc, p1, g_norm.reshape(1, W), skip.reshape(1, W))
    return y, c_new, n_new.reshape(B, H, DH), m_new.reshape(B, H)


def _mlstm_sample(q, k, v, gates, xc, p1, g_norm, skip, c0, n0, m0, *, H, DH):
    Bs = q.shape[0]
    W = H * DH
    SB = SUBLANES

    def body(q_ref, k_ref, v_ref, g_ref, xc_ref, om_ref, gn_ref, sk_ref, c_ref, n_ref, m_ref,
             y_ref, c_out, n_out, m_out):
        gts = g_ref[...]
        for h in range(H):
            cs = slice(h * DH, (h + 1) * DH)
            ig = gts[:, h:h + 1]
            fg = gts[:, H + h:H + h + 1]
            inter = fg + m_ref[:, h:h + 1]
            m = jnp.maximum(inter, ig)
            g = jnp.exp(inter - m)
            m_out[:, h:h + 1] = m
            q = q_ref[:, cs]
            kw = k_ref[:, cs] * jnp.exp(ig - m)
            nn = g * n_ref[:, cs] + kw
            n_out[:, cs] = nn
            den = jnp.sum(nn * q, axis=1, keepdims=True)
            v_t = v_ref[:, cs].T
            qb = q.astype(bf16)
            nums = []
            for s in range(SB):
                cn = g[s:s + 1, :] * c_ref[s, h] + v_t[:, s:s + 1] * kw[s:s + 1, :]
                c_out[s, h] = cn
                nums.append(_dot_nt(qb, cn.astype(bf16))[s:s + 1, :])
            num = jnp.concatenate(nums, axis=0)
            hm = num / jnp.maximum(jnp.abs(den), jnp.exp(-m))
            y_ref[:, cs] = ((_rms(hm, gn_ref[:, cs]) + sk_ref[:, cs] * xc_ref[:, cs])
                            * jax.nn.sigmoid(om_ref[:, cs]))

    per = lambda b: (b, 0)
    const2 = lambda b: (0, 0)
    y, c_new, n_new, m_new = pl.pallas_call(
        body,
        grid=(Bs // SB,),
        in_specs=[pl.BlockSpec((SB, W), per), pl.BlockSpec((SB, W), per), pl.BlockSpec((SB, W), per),
                  pl.BlockSpec((SB, LANES), per), pl.BlockSpec((SB, W), per),
                  pl.BlockSpec((SB, W), lambda b: (b, 1)),
                  pl.BlockSpec((1, W), const2), pl.BlockSpec((1, W), const2),
                  pl.BlockSpec((SB, H, DH, DH), lambda b: (b, 0, 0, 0)),
                  pl.BlockSpec((SB, W), per), pl.BlockSpec((SB, H), per)],
        out_specs=[pl.BlockSpec((SB, W), per),
                   pl.BlockSpec((SB, H, DH, DH), lambda b: (b, 0, 0, 0)),
                   pl.BlockSpec((SB, W), per), pl.BlockSpec((SB, H), per)],
        out_shape=[jax.ShapeDtypeStruct((Bs, W), f32), jax.ShapeDtypeStruct((Bs, H, DH, DH), f32),
                   jax.ShapeDtypeStruct((Bs, W), f32), jax.ShapeDtypeStruct((Bs, H), f32)],
        compiler_params=_params(("parallel",), 48),
        name="mlstm_sample",
    )(q, k, v, gates, xc, p1, g_norm.reshape(1, W), skip.reshape(1, W), c0, n0.reshape(Bs, W), m0)
    return y, c_new, n_new.reshape(Bs, H, DH), m_new


S5_TILES = 8


def _s5_layouts(lam_re, lam_im, log_dt, b_re, b_im, c_re, c_im):
    G, P = lam_re.shape
    GC = b_re.shape[2]
    T = S5_TILES
    gpt = G // T
    ns = G * P
    flat = lambda a: a.reshape(ns)
    ldt = jnp.broadcast_to(log_dt[:, None], (G, P))
    rows = [flat(a).reshape(T, 1, ns // T) for a in (lam_re, lam_im, ldt)]
    eye = jnp.eye(gpt, dtype=f32)
    bbd = [jnp.einsum('jgpc,gh->jgchp', a.reshape(T, gpt, P, GC), eye).reshape(T, gpt * GC, gpt * P) for a in (b_re, b_im)]
    cbd = [jnp.einsum('jgcp,gh->jgphc', a.reshape(T, gpt, GC, P), eye).reshape(T, gpt * P, gpt * GC) for a in (c_re, c_im)]
    return rows, bbd, cbd


def _s5_discretise(lre, lim, ldt):
    dt = jnp.exp(ldt)
    mag = jnp.exp(dt * lre)
    ar = mag * jnp.cos(dt * lim)
    ai = mag * jnp.sin(dt * lim)
    den = lre * lre + lim * lim
    cr = ((ar - 1.0) * lre + ai * lim) / den
    ci = (ai * lre - (ar - 1.0) * lim) / den
    return ar, ai, cr, ci


def _s5_mixer(p1, u_blk, rows, bbd, cbd, d_skip, w_glu, b_glu, *, B, L, tb, state=None):
    T = S5_TILES
    lre_r, lim_r, ldt_r = rows
    SW = lre_r.shape[2]
    CW = bbd[0].shape[1]
    W = T * CW
    NS = T * SW
    KT = SW // LANES
    sample = state is not None
    NT = 1 if sample else L // tb
    M = B * L

    def body(*refs):
        it = iter(refs)
        u_ref = next(it)
        if sample:
            x0r_ref, x0i_ref = next(it), next(it)
        lre_ref, lim_ref, ldt_ref = next(it), next(it), next(it)
        bre_ref, bim_ref, cre_ref, cim_ref = next(it), next(it), next(it), next(it)
        d_ref, wg_ref, bgl_ref = next(it), next(it), next(it)
        y_ref, xr_out, xi_out = next(it), next(it), next(it)
        bbr, bbi, cpair, ar_scr, ai_scr = (next(it) for _ in range(5))
        if not sample:
            sre, sim, xr_c, xi_c, yacc = (next(it) for _ in range(5))
        first = jnp.logical_and(pl.program_id(0) == 0, pl.program_id(1) == 0)

        @pl.when(first)
        def _():
            for j in range(T):
                ar, ai, cr, ci = _s5_discretise(lre_ref[j], lim_ref[j], ldt_ref[j])
                ar_scr[j] = ar
                ai_scr[j] = ai
                br = bre_ref[j]
                bi = bim_ref[j]
                bbr[j] = (cr * br - ci * bi).astype(bf16)
                bbi[j] = (cr * bi + ci * br).astype(bf16)
            for jp in range(T // 2):
                cpair[jp] = jnp.zeros((4 * SW, 2 * CW), bf16)
                for half in range(2):
                    j = 2 * jp + half
                    r0 = 2 * half * SW
                    cols = slice(half * CW, (half + 1) * CW)
                    cpair[jp, r0:r0 + SW, cols] = cre_ref[j].astype(bf16)
                    cpair[jp, r0 + SW:r0 + 2 * SW, cols] = (-cim_ref[j]).astype(bf16)

        u = u_ref[...].astype(f32)
        ub = u_ref[...].astype(bf16)
        ys = []
        if sample:
            for j in range(T):
                cs = slice(j * SW, (j + 1) * SW)
                uj = ub[:, j * CW:(j + 1) * CW]
                ar = ar_scr[j]
                ai = ai_scr[j]
                x0r = x0r_ref[:, cs]
                x0i = x0i_ref[:, cs]
                xr = ar * x0r - ai * x0i + _dot(uj, bbr[j])
                xi = ar * x0i + ai * x0r + _dot(uj, bbi[j])
                xr_out[:, cs] = xr
                xi_out[:, cs] = xi
                ys += [xr.astype(bf16), xi.astype(bf16)]
            y = jnp.concatenate([_dot(jnp.concatenate(ys[4 * jp:4 * jp + 4], axis=1), cpair[jp]) for jp in range(T // 2)],
                                axis=1)
        else:
            t = pl.program_id(1)

            @pl.when(t == 0)
            def _():
                xr_c[...] = jnp.zeros_like(xr_c)
                xi_c[...] = jnp.zeros_like(xi_c)

            for j in range(T):
                uj = ub[:, j * CW:(j + 1) * CW]
                r = _dot(uj, bbr[j])
                im = _dot(uj, bbi[j])
                for kk in range(KT):
                    sre[kk, pl.ds(j, tb, stride=T), :] = r[:, kk * LANES:(kk + 1) * LANES]
                    sim[kk, pl.ds(j, tb, stride=T), :] = im[:, kk * LANES:(kk + 1) * LANES]
            a_r = [jnp.concatenate([ar_scr[j][:, kk * LANES:(kk + 1) * LANES] for j in range(T)], axis=0) for kk in range(KT)]
            a_i = [jnp.concatenate([ai_scr[j][:, kk * LANES:(kk + 1) * LANES] for j in range(T)], axis=0) for kk in range(KT)]

            def step(s, carry):
                xr, xi = carry
                row = pl.ds(pl.multiple_of(s * T, T), T)
                nr, ni = [], []
                for kk in range(KT):
                    r_ = a_r[kk] * xr[kk] - a_i[kk] * xi[kk] + sre[kk, row, :]
                    i_ = a_r[kk] * xi[kk] + a_i[kk] * xr[kk] + sim[kk, row, :]
                    sre[kk, row, :] = r_
                    sim[kk, row, :] = i_
                    nr.append(r_)
                    ni.append(i_)
                return tuple(nr), tuple(ni)
            xr0 = tuple(xr_c[kk] for kk in range(KT))
            xi0 = tuple(xi_c[kk] for kk in range(KT))
            xr, xi = lax.fori_loop(0, tb, step, (xr0, xi0), unroll=4)
            for kk in range(KT):
                xr_c[kk] = xr[kk]
                xi_c[kk] = xi[kk]

            @pl.when(t == NT - 1)
            def _():
                for kk in range(KT):
                    xr_out[kk] = xr[kk]
                    xi_out[kk] = xi[kk]

            for jp in range(T // 2):
                parts = []
                for j in (2 * jp, 2 * jp + 1):
                    parts += [sre[kk, pl.ds(j, tb, stride=T), :] for kk in range(KT)]
                    parts += [sim[kk, pl.ds(j, tb, stride=T), :] for kk in range(KT)]
                xp = jnp.concatenate(parts, axis=1).astype(bf16)
                yacc[:, 2 * jp * CW:(2 * jp + 2) * CW] = _dot(xp, cpair[jp])
            y = yacc[...]
        ysk = jax.nn.gelu(y + d_ref[...] * u)
        z = _dot(ysk.astype(bf16), wg_ref[...]) + bgl_ref[...]
        y_ref[...] = (ysk * jax.nn.sigmoid(z)).astype(bf16)

    c3 = lambda b, t: (0, 0, 0)
    c2 = lambda b, t: (0, 0)
    in_specs = [pl.BlockSpec((tb, W), lambda b, t: (b * NT + t, u_blk))]
    args = [p1]
    if sample:
        in_specs += [pl.BlockSpec((tb, NS), lambda b, t: (b, 0))] * 2
        args += [state[0], state[1]]
    in_specs += [pl.BlockSpec((T, 1, SW), c3)] * 3
    in_specs += [pl.BlockSpec((T, CW, SW), c3)] * 2 + [pl.BlockSpec((T, SW, CW), c3)] * 2
    in_specs += [pl.BlockSpec((1, W), c2), pl.BlockSpec((W, W), c2), pl.BlockSpec((1, W), c2)]
    args += [lre_r, lim_r, ldt_r, bbd[0], bbd[1], cbd[0], cbd[1], d_skip.reshape(1, W), w_glu, b_glu.reshape(1, W)]
    scratch = [pltpu.VMEM((T, CW, SW), bf16), pltpu.VMEM((T, CW, SW), bf16),
               pltpu.VMEM((T // 2, 4 * SW, 2 * CW), bf16),
               pltpu.VMEM((T, 1, SW), f32), pltpu.VMEM((T, 1, SW), f32)]
    if sample:
        grid = (M // tb, 1)
        st_spec = pl.BlockSpec((tb, NS), lambda b, t: (b, 0))
        st_shape = jax.ShapeDtypeStruct((M, NS), f32)
    else:
        grid = (B, NT)
        st_spec = pl.BlockSpec((None, KT, T, LANES), lambda b, t: (b, 0, 0, 0))
        st_shape = jax.ShapeDtypeStruct((B, KT, T, LANES), f32)
        scratch += [pltpu.VMEM((KT, tb * T, LANES), f32), pltpu.VMEM((KT, tb * T, LANES), f32),
                    pltpu.VMEM((KT, T, LANES), f32), pltpu.VMEM((KT, T, LANES), f32), pltpu.VMEM((tb, W), f32)]
    y, xr, xi = pl.pallas_call(
        body,
        grid=grid,
        in_specs=in_specs,
        out_specs=[pl.BlockSpec((tb, W), lambda b, t: (b * NT + t, 0)), st_spec, st_spec],
        out_shape=[jax.ShapeDtypeStruct((M, W), bf16), st_shape, st_shape],
        scratch_shapes=scratch,
        compiler_params=_params(("arbitrary", "arbitrary"), 56),
        name="s5_sample" if sample else "s5_prompt",
    )(*args)
    if not sample:
        xr = jnp.transpose(xr, (0, 2, 1, 3)).reshape(B, NS)
        xi = jnp.transpose(xi, (0, 2, 1, 3)).reshape(B, NS)
    return y, xr, xi


def _trunk(x3, st, w, *, sample):
    B, L, D = x3.shape
    M = B * L
    x = x3.reshape(M, D)
    H_g, DK, DV = st['gla_S'].shape[1:] if sample else w['gla_dims']
    H_m, DH = w['ml_dims']
    G, P = w['s5_dims']
    tm = M if sample else min(1024, L)
    tf = M if sample else min(1024, L)
    to = M if sample else min(2048, M)
    tb = min(1024, L)
    pdt = f32 if sample else bf16
    out = {}

    p0, alr = _norm_matmul(x, w['g_mix0'], w['w_in0_main'], tm=tm, tn=1024, w_side=w['w_in0_alr'], out_dtype=pdt, w_t=True,
                            seg_cols=w['w_in0_cols'])
    x_blk = (2 * H_g * DK + 2 * H_g * DV) // LANES
    W_rg = w['rg_lambda'].shape[0]
    g_blk = x_blk + W_rg // LANES
    rg_tail = p0.reshape(B, L, p0.shape[1])[:, max(L - 3, 0):, x_blk * LANES:x_blk * LANES + W_rg].astype(f32)
    if sample:
        ya, out['gla_S'] = _gla_sample(p0, alr, w['gla_w2p'], w['gla_b_alpha'], w['gla_g_norm'], st['gla_S'],
                                       H=H_g, DK=DK, DV=DV)
        yb, out['rg_h'] = _rglru_sample(p0, st['rg_conv'], st['rg_h'], w['rg_conv_w'], w['rg_conv_b'], w['rg_w_r'],
                                        w['rg_b_r'], w['rg_w_i'], w['rg_b_i'], w['rg_lambda'], x_blk=x_blk, g_blk=g_blk)
        out['rg_conv'] = jnp.concatenate([st['rg_conv'][:, 1:], rg_tail], axis=1)
    else:
        ya, out['gla_S'] = _gla_prompt(p0, alr, w['gla_w2p'], w['gla_b_alpha'], w['gla_g_norm'],
                                       B=B, L=L, H=H_g, DK=DK, DV=DV, tb=tb)
        yb, out['rg_h'] = _rglru_prompt(p0, w['rg_conv_w'], w['rg_conv_b'], w['rg_w_r'], w['rg_b_r'], w['rg_w_i'],
                                        w['rg_b_i'], w['rg_lambda'], B=B, L=L, x_blk=x_blk, g_blk=g_blk)
        out['rg_conv'] = rg_tail
    x = _out_proj(ya, yb, w['w_out0'], x, tm=to, tn=512)
    x, ffn0 = _conv_ffn(x, w['g_ffn'][0], w['ffn'][0], seq_len=L, tm=tf, prev=st['ffn_conv'] if sample else None)

    W_ml = H_m * DH
    p1 = _norm_matmul(x, w['g_mix1'], [w['w_in1']], tm=tm, tn=1024, out_dtype=pdt)
    xm_tail = p1.reshape(B, L, p1.shape[1])[:, max(L - 3, 0):, :W_ml].astype(f32)
    q, k, v, gates, xc = _mlstm_stage1(p1, w['ml_conv_w'], w['ml_conv_b'], w['ml_wq_t'], w['ml_wk_t'], w['ml_wv_t'],
                                       w['ml_wg'], w['ml_bg'], seq_len=L, tr=M if sample else min(256, L), H=H_m, DH=DH,
                                       conv_state=st['ml_conv'] if sample else None)
    u_blk = 2 * W_ml // (G * w['s5_gc'])
    if sample:
        yc, out['ml_C'], out['ml_n'], out['ml_m'] = _mlstm_sample(
            q, k, v, gates, xc, p1, w['ml_g_norm'], w['ml_skip'], st['ml_C'], st['ml_n'], st['ml_m'], H=H_m, DH=DH)
        out['ml_conv'] = jnp.concatenate([st['ml_conv'][:, 1:], xm_tail], axis=1)
        yd, s5r, s5i = _s5_mixer(p1, u_blk, w['s5_rows'], w['s5_bbd'], w['s5_cbd'], w['s5_D'], w['s5_w_glu'],
                                 w['s5_b_glu'], B=B, L=1, tb=B,
                                 state=(st['s5_re'].reshape(B, G * P), st['s5_im'].reshape(B, G * P)))
    else:
        yc, out['ml_C'], out['ml_n'], out['ml_m'] = _mlstm_prompt(
            q, k, v, gates, xc, p1, w['ml_g_norm'], w['ml_skip'], B=B, L=L, H=H_m, DH=DH, tb=tb)
        out['ml_conv'] = xm_tail
        yd, s5r, s5i = _s5_mixer(p1, u_blk, w['s5_rows'], w['s5_bbd'], w['s5_cbd'], w['s5_D'], w['s5_w_glu'],
                                 w['s5_b_glu'], B=B, L=L, tb=min(512, L))
    out['s5_re'] = s5r.reshape(B, G, P)
    out['s5_im'] = s5i.reshape(B, G, P)
    x = _out_proj(yc, yd, w['w_out1'], x, tm=to, tn=512)
    x, ffn1 = _conv_ffn(x, w['g_ffn'][1], w['ffn'][1], seq_len=L, tm=tf, prev=st['ffn_conv'] if sample else None,
                        final_g=w['g_final'])
    out['ffn_conv'] = jnp.stack([ffn0, ffn1], axis=0)
    return x.reshape(B, L, D), out


def kernel(x_prompt, x_sample, state_gla_S, state_rglru_h, state_rglru_conv, state_mlstm_C, state_mlstm_n, state_mlstm_m, state_mlstm_conv, state_s5_re, state_s5_im, state_ffn_conv, g_mix0, w_in0, gla_w_alpha2, gla_b_alpha, gla_g_norm, rg_conv_w, rg_conv_b, rg_w_r, rg_b_r, rg_w_i, rg_b_i, rg_lambda, w_out0, g_mix1, w_in1, ml_conv_w, ml_conv_b, ml_wq, ml_wk, ml_wv, ml_w_igate, ml_b_igate, ml_w_fgate, ml_b_fgate, ml_g_norm, ml_skip, s5_lam_re, s5_lam_im, s5_log_dt, s5_B_re, s5_B_im, s5_C_re, s5_C_im, s5_D, s5_w_glu, s5_b_glu, w_out1, g_ffn, ffn_w_up, ffn_conv_w, ffn_conv_b, ffn_w_down, g_final):
    _, H_g, DK, DV = state_gla_S.shape
    _, H_m, DH, _ = state_mlstm_C.shape
    G, P = s5_lam_re.shape
    rank = gla_w_alpha2.shape[0]
    n_main = 2 * H_g * DK + 2 * H_g * DV
    w_in0_t = w_in0.T.astype(bf16)
    w_in0_main = [w_in0_t, w_in0_t[n_main + rank:]]
    w_in0_cols = [n_main, w_in0.shape[1] - n_main - rank]
    w_in0_alr = jnp.pad(w_in0_t[n_main:n_main + rank], ((0, LANES - rank), (0, 0)))
    gla_w2p = jnp.pad(gla_w_alpha2, ((0, LANES - rank), (0, 0)))
    ml_tile = 256
    ml_wg = jnp.pad(jnp.concatenate([ml_w_igate, ml_w_fgate], axis=1), ((0, 0), (0, LANES - 2 * H_m))).astype(bf16)
    ml_bg = jnp.pad(jnp.concatenate([ml_b_igate, ml_b_fgate]), (0, LANES - 2 * H_m)).reshape(1, LANES)
    rows, bbd, cbd = _s5_layouts(s5_lam_re, s5_lam_im, s5_log_dt, s5_B_re, s5_B_im, s5_C_re, s5_C_im)
    w = dict(
        g_mix0=g_mix0, w_in0_main=w_in0_main, w_in0_cols=w_in0_cols, w_in0_alr=w_in0_alr, gla_w2p=gla_w2p, gla_b_alpha=gla_b_alpha,
        gla_g_norm=gla_g_norm, gla_dims=(H_g, DK, DV), rg_conv_w=rg_conv_w, rg_conv_b=rg_conv_b, rg_w_r=rg_w_r,
        rg_b_r=rg_b_r, rg_w_i=rg_w_i, rg_b_i=rg_b_i, rg_lambda=rg_lambda, w_out0=w_out0.astype(bf16),
        g_mix1=g_mix1, w_in1=w_in1.astype(bf16), ml_conv_w=ml_conv_w, ml_conv_b=ml_conv_b,
        ml_wq_t=_blockdiag_tiles(ml_wq, ml_tile).astype(bf16), ml_wk_t=_blockdiag_tiles(ml_wk, ml_tile).astype(bf16),
        ml_wv_t=_blockdiag_tiles(ml_wv, ml_tile).astype(bf16), ml_wg=ml_wg, ml_bg=ml_bg, ml_g_norm=ml_g_norm,
        ml_skip=ml_skip, ml_dims=(H_m, DH), s5_dims=(G, P), s5_gc=s5_B_re.shape[2], s5_rows=rows, s5_bbd=bbd,
        s5_cbd=cbd, s5_D=s5_D, s5_w_glu=s5_w_glu.astype(bf16), s5_b_glu=s5_b_glu, w_out1=w_out1.astype(bf16),
        g_ffn=g_ffn, g_final=g_final,
        ffn=_ffn_prepare(ffn_w_up, ffn_conv_w, ffn_conv_b, ffn_w_down))
    st_s = dict(gla_S=state_gla_S, rg_h=state_rglru_h, rg_conv=state_rglru_conv, ml_C=state_mlstm_C,
                ml_n=state_mlstm_n, ml_m=state_mlstm_m, ml_conv=state_mlstm_conv, s5_re=state_s5_re,
                s5_im=state_s5_im, ffn_conv=state_ffn_conv)
    y_p, np_ = _trunk(x_prompt, None, w, sample=False)
    y_s, ns_ = _trunk(x_sample, st_s, w, sample=True)
    names = ('gla_S', 'rg_h', 'rg_conv', 'ml_C', 'ml_n', 'ml_m', 'ml_conv', 's5_re', 's5_im', 'ffn_conv')
    outs = [y_p, y_s]
    for nme in names:
        outs += [np_[nme], ns_[nme]]
    return tuple(outs)
```

```python
import jax
import jax.numpy as jnp
from jax import lax
from jax.experimental import pallas as pl
from jax.experimental.pallas import tpu as pltpu

f32 = jnp.float32
bf16 = jnp.bfloat16

EPS = 1e-6
CHUNK = 64
GLA_TAU = 16.0
RG_C = 8.0
PAST_LEN = 16384
LANES = 128
SUBLANES = 8
HALO = 16
MIB = 1024 * 1024


def _params(sem, vmem_mib):
    return pltpu.CompilerParams(dimension_semantics=sem, vmem_limit_bytes=int(vmem_mib * MIB))


def _dot(a, b):
    return jnp.dot(a, b, preferred_element_type=f32)


def _dot_nt(a, b):
    return lax.dot_general(a, b, (((1,), (1,)), ((), ())), preferred_element_type=f32)


def _dot_tn(a, b):
    return lax.dot_general(a, b, (((0,), (0,)), ((), ())), preferred_element_type=f32)


def _rms(x, g):
    return x * lax.rsqrt(jnp.mean(x * x, axis=-1, keepdims=True) + EPS) * g


def _eye(n):
    return lax.broadcasted_iota(jnp.int32, (n, n), 0) == lax.broadcasted_iota(jnp.int32, (n, n), 1)


def _col_from_row(row):
    n = row.shape[1]
    return jnp.sum(jnp.where(_eye(n), jnp.broadcast_to(row, (n, n)), 0.0), axis=1, keepdims=True)


def _chunk_cumsum(x, chunk):
    pos = lax.broadcasted_iota(jnp.int32, (x.shape[0], 1), 0) % chunk
    step = 1
    while step < chunk:
        x = x + jnp.where(pos >= step, pltpu.roll(x, step, 0), 0.0)
        step *= 2
    return x


def _norm_matmul(x, g, ws, *, tm, tn, w_side=None, out_dtype=f32, w_t=False, seg_cols=None):
    M, D = x.shape
    mm = _dot_nt if w_t else _dot
    seg_cols = seg_cols or [w.shape[0 if w_t else 1] for w in ws]
    nblk = [c // tn for c in seg_cols]
    start = [sum(nblk[:s]) for s in range(len(ws) + 1)]
    N = tn * start[-1]
    rc = min(tm, 256)
    side = w_side is not None

    def body(*refs):
        x_ref, g_ref = refs[:2]
        w_refs = refs[2:2 + len(ws)]
        rest = refs[2 + len(ws):]
        if side:
            ws_ref, o_ref, os_ref, xn_ref = rest
        else:
            o_ref, xn_ref = rest
        j = pl.program_id(1)

        @pl.when(j == 0)
        def _():
            def chunk(r, c):
                rows = pl.ds(pl.multiple_of(r * rc, rc), rc)
                xn_ref[rows, :] = _rms(x_ref[rows, :], g_ref[...]).astype(bf16)
                return c
            lax.fori_loop(0, tm // rc, chunk, 0)
            if side:
                os_ref[...] = mm(xn_ref[...], ws_ref[...])

        if len(ws) == 1:
            o_ref[...] = mm(xn_ref[...], w_refs[0][...]).astype(out_dtype)
        else:
            for s, w_ref in enumerate(w_refs):
                @pl.when(jnp.logical_and(j >= start[s], j < start[s + 1]))
                def _(w_ref=w_ref):
                    o_ref[...] = mm(xn_ref[...], w_ref[...]).astype(out_dtype)

    in_specs = [pl.BlockSpec((tm, D), lambda i, j: (i, 0)),
                pl.BlockSpec((1, D), lambda i, j: (0, 0))]
    wblk = lambda j, s: jnp.clip(j - start[s], 0, nblk[s] - 1)
    if w_t:
        in_specs += [pl.BlockSpec((tn, D), lambda i, j, s=s: (wblk(j, s), 0)) for s in range(len(ws))]
    else:
        in_specs += [pl.BlockSpec((D, tn), lambda i, j, s=s: (0, wblk(j, s))) for s in range(len(ws))]
    out_specs = [pl.BlockSpec((tm, tn), lambda i, j: (i, j))]
    out_shape = [jax.ShapeDtypeStruct((M, N), out_dtype)]
    args = [x, g.reshape(1, D)] + list(ws)
    if side:
        ns = w_side.shape[0 if w_t else 1]
        in_specs.append(pl.BlockSpec((ns, D) if w_t else (D, ns), lambda i, j: (0, 0)))
        out_specs.append(pl.BlockSpec((tm, ns), lambda i, j: (i, 0)))
        out_shape.append(jax.ShapeDtypeStruct((M, ns), f32))
        args.append(w_side)
    outs = pl.pallas_call(
        body,
        grid=(M // tm, N // tn),
        in_specs=in_specs,
        out_specs=out_specs,
        out_shape=out_shape,
        scratch_shapes=[pltpu.VMEM((tm, D), bf16)],
        compiler_params=_params(("parallel", "arbitrary"), 56),
        name="norm_matmul",
    )(*args)
    return outs if side else outs[0]


def _out_proj(ya, yb, w, res, *, tm, tn):
    M, Ka = ya.shape
    Kb = yb.shape[1]
    N = w.shape[1]
    assert Ka == Kb and w.shape[0] == Ka + Kb

    def body(ya_ref, yb_ref, wa_ref, wb_ref, r_ref, o_ref):
        o_ref[...] = (r_ref[...] + _dot(ya_ref[...].astype(bf16), wa_ref[...])
                      + _dot(yb_ref[...].astype(bf16), wb_ref[...]))

    return pl.pallas_call(
        body,
        grid=(M // tm, N // tn),
        in_specs=[pl.BlockSpec((tm, Ka), lambda i, j: (i, 0)),
                  pl.BlockSpec((tm, Kb), lambda i, j: (i, 0)),
                  pl.BlockSpec((Ka, tn), lambda i, j: (0, j)),
                  pl.BlockSpec((Kb, tn), lambda i, j: (1, j)),
                  pl.BlockSpec((tm, tn), lambda i, j: (i, j))],
        out_specs=pl.BlockSpec((tm, tn), lambda i, j: (i, j)),
        out_shape=jax.ShapeDtypeStruct((M, N), f32),
        compiler_params=_params(("parallel", "arbitrary"), 48),
        name="out_proj",
    )(ya, yb, w, w, res)


FFN_STEP = 512
FFN_ROW_CHUNK = 512


def _ffn_to_steps(a, F):
    nsteps = -(-F // FFN_STEP)
    lead = a.shape[:-1]
    nd = len(lead)
    gv = a.reshape(lead + (2, F))
    gv = jnp.pad(gv, [(0, 0)] * (nd + 1) + [(0, nsteps * FFN_STEP - F)])
    gv = gv.reshape(lead + (2, nsteps, FFN_STEP))
    gv = jnp.transpose(gv, (nd + 1,) + tuple(range(nd)) + (nd, nd + 2))
    return gv.reshape((nsteps,) + lead + (2 * FFN_STEP,))


def _ffn_prepare(w_up, conv_w, conv_b, w_down):
    NL, F, D = w_down.shape
    assert F % LANES == 0 and FFN_STEP % LANES == 0
    nt = F // LANES
    tp = FFN_STEP // LANES
    nsteps = -(-F // FFN_STEP)

    def body(*refs):
        g_in, v_in, d_in = refs[:tp], refs[tp:2 * tp], refs[2 * tp:3 * tp]
        wu_o, wd_o = refs[3 * tp:]
        for t in range(tp):
            keep = tp * pl.program_id(1) + t < nt
            cols = slice(t * LANES, (t + 1) * LANES)
            wu_o[:, cols] = jnp.where(keep, g_in[t][...], 0.0).astype(bf16)
            wu_o[:, FFN_STEP + t * LANES:FFN_STEP + (t + 1) * LANES] = jnp.where(keep, v_in[t][...], 0.0).astype(bf16)
            wd_o[cols, :] = jnp.where(keep, d_in[t][...], 0.0).astype(bf16)

    tile = lambda j, t: jnp.minimum(tp * j + t, nt - 1)
    in_specs = ([pl.BlockSpec((None, D, LANES), lambda l, j, t=t: (l, 0, tile(j, t))) for t in range(tp)]
                + [pl.BlockSpec((None, D, LANES), lambda l, j, t=t: (l, 0, nt + tile(j, t))) for t in range(tp)]
                + [pl.BlockSpec((None, LANES, D), lambda l, j, t=t: (l, tile(j, t), 0)) for t in range(tp)])
    wu, wd = pl.pallas_call(
        body,
        grid=(NL, nsteps),
        in_specs=in_specs,
        out_specs=[pl.BlockSpec((None, None, D, 2 * FFN_STEP), lambda l, j: (l, j, 0, 0)),
                   pl.BlockSpec((None, None, FFN_STEP, D), lambda l, j: (l, j, 0, 0))],
        out_shape=[jax.ShapeDtypeStruct((NL, nsteps, D, 2 * FFN_STEP), bf16),
                   jax.ShapeDtypeStruct((NL, nsteps, FFN_STEP, D), bf16)],
        compiler_params=_params(("parallel", "parallel"), 40),
        name="ffn_weight_layout",
    )(*([w_up] * (2 * tp) + [w_down] * tp))
    return [dict(wu=wu, wd=wd, layer=l, F=F, cw=_ffn_to_steps(conv_w[l], F),
                 cb=_ffn_to_steps(conv_b[l], F).reshape(nsteps, 1, 2 * FFN_STEP)) for l in range(NL)]


def _conv_ffn(x, g, fw, *, seq_len, tm, prev=None, final_g=None):
    M, D = x.shape
    F = fw['F']
    layer = fw['layer']
    nsteps = fw['wu'].shape[1]
    sample = prev is not None
    rc = min(tm, FFN_ROW_CHUNK)
    tiles_per_seq = max(seq_len // tm, 1)
    W2 = 2 * FFN_STEP

    def body(*refs):
        it = iter(refs)
        x_ref = next(it)
        xh_ref = None if sample else next(it)
        prev_refs = [next(it) for _ in range(2 * FFN_STEP // LANES)] if sample else None
        g_ref = next(it)
        wu_ref, cw_ref, cb_ref, wd_ref = next(it), next(it), next(it), next(it)
        fg_ref = next(it) if final_g is not None else None
        o_ref, tail_g_ref, tail_v_ref = next(it), next(it), next(it)
        xn_ref, up_scr = next(it), next(it)
        i = pl.program_id(0)
        j = pl.program_id(1)

        @pl.when(j == 0)
        def _():
            if sample:
                xn_ref[0:HALO, :] = jnp.zeros((HALO, D), bf16)
            else:
                keep = (i % tiles_per_seq != 0).astype(f32)
                hist = _rms(xh_ref[...], g_ref[...]) * keep
                xn_ref[0:HALO, :] = jnp.concatenate([jnp.zeros_like(hist), hist], axis=0).astype(bf16)

            def chunk(r, c):
                rows = pl.ds(pl.multiple_of(r * rc, rc), rc)
                xr = x_ref[rows, :]
                o_ref[rows, :] = xr
                xn_ref[pl.ds(pl.multiple_of(HALO + r * rc, HALO), rc), :] = _rms(xr, g_ref[...]).astype(bf16)
                return c
            lax.fori_loop(0, tm // rc, chunk, 0)

        cw = cw_ref[0]
        cb = cb_ref[0]
        nchunk = tm // rc
        for r in range(nchunk):
            if sample:
                up_scr[r, HALO:, :] = _dot(xn_ref[pl.ds(HALO + r * rc, rc), :], wu_ref[...])
            else:
                up_scr[r] = _dot(xn_ref[pl.ds(r * rc, rc + HALO), :], wu_ref[...])
        for r in range(nchunk):
            rows = pl.ds(r * rc, rc)
            up = up_scr[r, HALO:, :]
            if sample:
                p0 = jnp.concatenate([p[rows, 0, :] for p in prev_refs], axis=1)
                p1 = jnp.concatenate([p[rows, 1, :] for p in prev_refs], axis=1)
                conv = cb + cw[0:1] * p0 + cw[1:2] * p1 + cw[2:3] * up
                tail_g_ref[rows, 0, :] = p1[:, :FFN_STEP]
                tail_g_ref[rows, 1, :] = up[:, :FFN_STEP]
                tail_v_ref[rows, 0, :] = p1[:, FFN_STEP:]
                tail_v_ref[rows, 1, :] = up[:, FFN_STEP:]
            else:
                conv = (cb + cw[0:1] * up_scr[r, HALO - 2:HALO - 2 + rc, :]
                        + cw[1:2] * up_scr[r, HALO - 1:HALO - 1 + rc, :] + cw[2:3] * up)
                if r == nchunk - 1:
                    tail_g_ref[...] = up[rc - SUBLANES:, :FFN_STEP]
                    tail_v_ref[...] = up[rc - SUBLANES:, FFN_STEP:]
            h = jax.nn.gelu(conv[:, :FFN_STEP]) * conv[:, FFN_STEP:]
            o_ref[rows, :] += _dot(h.astype(bf16), wd_ref[...])

        if final_g is not None:
            @pl.when(j == nsteps - 1)
            def _():
                def chunk2(r, c):
                    rows = pl.ds(pl.multiple_of(r * rc, rc), rc)
                    o_ref[rows, :] = _rms(o_ref[rows, :], fg_ref[...])
                    return c
                lax.fori_loop(0, tm // rc, chunk2, 0)

    in_specs = [pl.BlockSpec((tm, D), lambda i, j: (i, 0))]
    args = [x]
    if sample:
        nt = F // LANES
        tp = FFN_STEP // LANES
        last = 2 * nt - 1
        for off in list(range(tp)) + [nt + t for t in range(tp)]:
            in_specs.append(pl.BlockSpec((None, tm, 2, LANES),
                                         lambda i, j, off=off: (layer, i, 0, jnp.minimum(tp * j + off, last))))
            args.append(prev)
    else:
        in_specs.append(pl.BlockSpec((SUBLANES, D), lambda i, j: (jnp.maximum(i * (tm // SUBLANES) - 1, 0), 0)))
        args.append(x)
    in_specs += [pl.BlockSpec((1, D), lambda i, j: (0, 0)),
                 pl.BlockSpec((None, None, D, W2), lambda i, j: (layer, j, 0, 0)),
                 pl.BlockSpec((1, 3, W2), lambda i, j: (j, 0, 0)),
                 pl.BlockSpec((1, 1, W2), lambda i, j: (j, 0, 0)),
                 pl.BlockSpec((None, None, FFN_STEP, D), lambda i, j: (layer, j, 0, 0))]
    args += [g.reshape(1, D), fw['wu'], fw['cw'], fw['cb'], fw['wd']]
    if final_g is not None:
        in_specs.append(pl.BlockSpec((1, D), lambda i, j: (0, 0)))
        args.append(final_g.reshape(1, D))
    FP = nsteps * FFN_STEP
    if sample:
        tail_shape = (M, 2, FP)
        tail_spec = pl.BlockSpec((tm, 2, FFN_STEP), lambda i, j: (i, 0, j))
    else:
        tail_shape = (M // tm, SUBLANES, FP)
        tail_spec = pl.BlockSpec((None, SUBLANES, FFN_STEP), lambda i, j: (i, 0, j))
    out, tail_g, tail_v = pl.pallas_call(
        body,
        grid=(M // tm, nsteps),
        in_specs=in_specs,
        out_specs=[pl.BlockSpec((tm, D), lambda i, j: (i, 0)), tail_spec, tail_spec],
        out_shape=[jax.ShapeDtypeStruct((M, D), f32), jax.ShapeDtypeStruct(tail_shape, f32),
                   jax.ShapeDtypeStruct(tail_shape, f32)],
        scratch_shapes=[pltpu.VMEM((tm + HALO, D), bf16), pltpu.VMEM((tm // rc, rc + HALO, W2), f32)],
        compiler_params=_params(("parallel", "arbitrary"), 56),
        name="conv_ffn_sample" if sample else "conv_ffn",
    )(*args)
    if sample:
        new_buf = jnp.concatenate([tail_g[:, :, :F], tail_v[:, :, :F]], axis=-1)
    else:
        nseq = M // seq_len
        pick = lambda t: t.reshape(nseq, tiles_per_seq, SUBLANES, FP)[:, -1, SUBLANES - 2:, :F]
        new_buf = jnp.concatenate([pick(tail_g), pick(tail_v)], axis=-1)
    return out, new_buf


def _gla_prompt(p0, alr, w2p, b_alpha, g_norm, *, B, L, H, DK, DV, tb):
    NT = L // tb
    NC = tb // CHUNK
    C = CHUNK
    scale = DK ** -0.5
    qk_blocks = H
    v_off = 2 * H * DK // DV

    def body(q_ref, k_ref, v_ref, r_ref, a_ref, w2_ref, ba_ref, gn_ref, y_ref, s_out_ref, s_scr):
        t = pl.program_id(2)

        @pl.when(t == 0)
        def _():
            s_scr[...] = jnp.zeros_like(s_scr)

        z = _dot(a_ref[...].astype(bf16), w2_ref[...].astype(bf16)) + ba_ref[...]
        gl = jax.nn.log_sigmoid(z) * (1.0 / GLA_TAU)
        bc3 = _chunk_cumsum(gl, C).reshape(NC, C, DK)
        bl3 = bc3[:, C - 1:C, :]
        q3 = (q_ref[...].astype(f32) * scale).reshape(NC, C, DK)
        k3 = k_ref[...].astype(f32).reshape(NC, C, DK)
        v3 = v_ref[...].astype(bf16).reshape(NC, C, DV)
        qd3 = (q3 * jnp.exp(bc3)).astype(bf16)
        kd3 = (k3 * jnp.exp(-bc3)).astype(bf16)
        kdec3 = (k3 * jnp.exp(bl3 - bc3)).astype(bf16)
        causal = (lax.broadcasted_iota(jnp.int32, (1, C, C), 1) >= lax.broadcasted_iota(jnp.int32, (1, C, C), 2))
        att = jnp.where(causal, jnp.einsum('cik,cjk->cij', qd3, kd3, preferred_element_type=f32), 0.0)
        intra = jnp.einsum('cij,cjv->civ', att.astype(bf16), v3, preferred_element_type=f32)
        ds = jnp.einsum('cjk,cjv->ckv', kdec3, v3, preferred_element_type=f32)
        s = s_scr[...]
        s_in = []
        for c in range(NC):
            s_in.append(s)
            s = _col_from_row(jnp.exp(bl3[c])) * s + ds[c]
        s_scr[...] = s
        s_all = jnp.stack(s_in).astype(bf16)
        o = (intra + jnp.einsum('cik,ckv->civ', qd3, s_all, preferred_element_type=f32)).reshape(tb, DV)
        rr = r_ref[...].astype(f32)
        y_ref[...] = (_rms(o, gn_ref[...]) * (rr * jax.nn.sigmoid(rr))).astype(bf16)

        @pl.when(t == NT - 1)
        def _():
            s_out_ref[0, 0] = s
    return pl.pallas_call(
        body,
        grid=(B, H, NT),
        in_specs=[pl.BlockSpec((tb, DK), lambda b, h, t: (b * NT + t, h)),
                  pl.BlockSpec((tb, DK), lambda b, h, t: (b * NT + t, qk_blocks + h)),
                  pl.BlockSpec((tb, DV), lambda b, h, t: (b * NT + t, v_off + h)),
                  pl.BlockSpec((tb, DV), lambda b, h, t: (b * NT + t, v_off + H + h)),
                  pl.BlockSpec((tb, LANES), lambda b, h, t: (b * NT + t, 0)),
                  pl.BlockSpec((LANES, DK), lambda b, h, t: (0, h)),
                  pl.BlockSpec((1, DK), lambda b, h, t: (0, h)),
                  pl.BlockSpec((1, DV), lambda b, h, t: (0, h))],
        out_specs=[pl.BlockSpec((tb, DV), lambda b, h, t: (b * NT + t, h)),
                   pl.BlockSpec((1, 1, DK, DV), lambda b, h, t: (b, h, 0, 0))],
        out_shape=[jax.ShapeDtypeStruct((B * L, H * DV), bf16),
                   jax.ShapeDtypeStruct((B, H, DK, DV), f32)],
        scratch_shapes=[pltpu.VMEM((DK, DV), f32)],
        compiler_params=_params(("parallel", "parallel", "arbitrary"), 32),
        name="gla_prompt",
    )(p0, p0, p0, p0, alr, w2p, b_alpha.reshape(1, -1), g_norm.reshape(1, -1))


def _gla_sample(p0, alr, w2p, b_alpha, g_norm, s0, *, H, DK, DV):
    Bs = p0.shape[0]
    scale = DK ** -0.5
    qkw = H * DK
    vw = H * DV
    assert vw % qkw == 0
    SB = SUBLANES

    def body(q_ref, k_ref, v_ref, r_ref, a_ref, w2_ref, ba_ref, gn_ref, s_ref, y_ref, so_ref, gl_scr):
        z = _dot(a_ref[...].astype(bf16), w2_ref[...].astype(bf16)) + ba_ref[...]
        gl_scr[...] = jax.nn.log_sigmoid(z) * (1.0 / GLA_TAU)

        for h in range(H):
            ks = slice(h * DK, (h + 1) * DK)
            vs = slice(h * DV, (h + 1) * DV)
            a_t = jnp.exp(gl_scr[:, ks]).T
            k_t = k_ref[:, ks].T
            q_t = (q_ref[:, ks] * scale).T
            v = v_ref[:, vs]
            outs = []
            for s in range(SB):
                sn = a_t[:, s:s + 1] * s_ref[s, h] + k_t[:, s:s + 1] * v[s:s + 1, :]
                so_ref[s, h] = sn
                outs.append(jnp.sum(q_t[:, s:s + 1] * sn, axis=0, keepdims=True))
            o = jnp.concatenate(outs, axis=0)
            rr = r_ref[:, vs]
            y_ref[:, vs] = _rms(o, gn_ref[:, vs]) * (rr * jax.nn.sigmoid(rr))

    v_blk = 2 * qkw // vw
    return pl.pallas_call(
        body,
        grid=(Bs // SB,),
        in_specs=[pl.BlockSpec((SB, qkw), lambda b: (b, 0)),
                  pl.BlockSpec((SB, qkw), lambda b: (b, 1)),
                  pl.BlockSpec((SB, vw), lambda b: (b, v_blk)),
                  pl.BlockSpec((SB, vw), lambda b: (b, v_blk + 1)),
                  pl.BlockSpec((SB, LANES), lambda b: (b, 0)),
                  pl.BlockSpec((LANES, qkw), lambda b: (0, 0)),
                  pl.BlockSpec((1, qkw), lambda b: (0, 0)),
                  pl.BlockSpec((1, vw), lambda b: (0, 0)),
                  pl.BlockSpec((SB, H, DK, DV), lambda b: (b, 0, 0, 0))],
        out_specs=[pl.BlockSpec((SB, vw), lambda b: (b, 0)),
                   pl.BlockSpec((SB, H, DK, DV), lambda b: (b, 0, 0, 0))],
        out_shape=[jax.ShapeDtypeStruct((Bs, vw), f32),
                   jax.ShapeDtypeStruct((Bs, H, DK, DV), f32)],
        scratch_shapes=[pltpu.VMEM((SB, qkw), f32)],
        compiler_params=_params(("parallel",), 32),
        name="gla_sample",
    )(p0, p0, p0, p0, alr, w2p, b_alpha.reshape(1, -1), g_norm.reshape(1, -1), s0)


def _rg_gates(xc, wr, br, wi, bi, sp):
    xb = xc.astype(bf16)
    r = jax.nn.sigmoid(_dot(xb, wr) + br)
    i = jax.nn.sigmoid(_dot(xb, wi) + bi)
    log_a = -RG_C * r * sp
    a = jnp.exp(log_a)
    mult = jnp.sqrt(1.0 - a * a)
    return a, mult, i


def _rglru_prompt(p0, conv_w, conv_b, w_r, b_r, w_i, b_i, lam, *, B, L, x_blk, g_blk):
    M = B * L
    NB, BS, _ = w_r.shape
    assert BS == LANES
    rc = min(256, L)

    def body(x_ref, gg_ref, cw_ref, cb_ref, wr_ref, br_ref, wi_ref, bi_ref, lam_ref, y_ref, hl_ref, a_scr, b_scr, xs_scr):
        wr = wr_ref[...].astype(bf16)
        wi = wi_ref[...].astype(bf16)
        sp = jax.nn.softplus(-lam_ref[...])
        cw = cw_ref[...]

        def chunk(c, carry):
            r0 = pl.multiple_of(c * rc, rc)
            rows = pl.ds(r0, rc)
            x = x_ref[rows, :].astype(f32)
            start = (r0 % L) == 0
            prev = x_ref[pl.ds(pl.multiple_of(jnp.maximum(r0 - HALO, 0), HALO), HALO), :].astype(f32)[HALO - SUBLANES:]
            xs_scr[0:SUBLANES, :] = jnp.where(start, 0.0, prev)
            xs_scr[SUBLANES:, :] = x
            xc = (cb_ref[...] + cw[0:1] * xs_scr[SUBLANES - 3:SUBLANES - 3 + rc, :]
                  + cw[1:2] * xs_scr[SUBLANES - 2:SUBLANES - 2 + rc, :]
                  + cw[2:3] * xs_scr[SUBLANES - 1:SUBLANES - 1 + rc, :] + cw[3:4] * x)
            a, mult, ig = _rg_gates(xc, wr, br_ref[...], wi, bi_ref[...], sp)
            first = jnp.logical_and(start, lax.broadcasted_iota(jnp.int32, (rc, 1), 0) == 0)
            mult = jnp.where(first, 1.0, mult)
            a_scr[rows, :] = a
            b_scr[rows, :] = mult * (ig * xc)
            return carry
        lax.fori_loop(0, M // rc, chunk, 0, unroll=2)

        def step(t, hs):
            new = []
            for b in range(B):
                row = pl.ds(b * L + t, 1)
                h = a_scr[row, :] * hs[b] + b_scr[row, :]
                b_scr[row, :] = h
                new.append(h)
            return tuple(new)
        hs = lax.fori_loop(0, L, step, tuple(jnp.zeros((1, LANES), f32) for _ in range(B)), unroll=32)
        hl_ref[...] = jnp.concatenate(hs, axis=0)

        def outc(c, carry):
            rows = pl.ds(pl.multiple_of(c * rc, rc), rc)
            y_ref[rows, :] = (b_scr[rows, :] * jax.nn.gelu(gg_ref[rows, :].astype(f32))).astype(bf16)
            return carry
        lax.fori_loop(0, M // rc, outc, 0)

    W = NB * BS
    return pl.pallas_call(
        body,
        grid=(NB,),
        in_specs=[pl.BlockSpec((M, LANES), lambda n: (0, x_blk + n)),
                  pl.BlockSpec((M, LANES), lambda n: (0, g_blk + n)),
                  pl.BlockSpec((4, LANES), lambda n: (0, n)),
                  pl.BlockSpec((1, LANES), lambda n: (0, n)),
                  pl.BlockSpec((None, BS, BS), lambda n: (n, 0, 0)),
                  pl.BlockSpec((1, LANES), lambda n: (0, n)),
                  pl.BlockSpec((None, BS, BS), lambda n: (n, 0, 0)),
                  pl.BlockSpec((1, LANES), lambda n: (0, n)),
                  pl.BlockSpec((1, LANES), lambda n: (0, n))],
        out_specs=[pl.BlockSpec((M, LANES), lambda n: (0, n)),
                   pl.BlockSpec((B, LANES), lambda n: (0, n))],
        out_shape=[jax.ShapeDtypeStruct((M, W), bf16), jax.ShapeDtypeStruct((B, W), f32)],
        scratch_shapes=[pltpu.VMEM((M, LANES), f32), pltpu.VMEM((M, LANES), f32), pltpu.VMEM((rc + SUBLANES, LANES), f32)],
        compiler_params=_params(("parallel",), 48),
        name="rglru_prompt",
    )(p0, p0, conv_w, conv_b.reshape(1, W), w_r, b_r.reshape(1, W), w_i, b_i.reshape(1, W), lam.reshape(1, W))


def _rglru_sample(p0, conv_state, h0, conv_w, conv_b, w_r, b_r, w_i, b_i, lam, *, x_blk, g_blk):
    Bs = p0.shape[0]
    NB, BS, _ = w_r.shape
    W = NB * BS
    s0, s1, s2 = conv_state[:, 0], conv_state[:, 1], conv_state[:, 2]

    def body(x_ref, gg_ref, s0_ref, s1_ref, s2_ref, h0_ref, cw_ref, cb_ref, wr_ref, br_ref, wi_ref, bi_ref, lam_ref,
             y_ref, h_ref):
        cw = cw_ref[...]
        x = x_ref[...]
        xc = cb_ref[...] + cw[0:1] * s0_ref[...] + cw[1:2] * s1_ref[...] + cw[2:3] * s2_ref[...] + cw[3:4] * x
        sp = jax.nn.softplus(-lam_ref[...])
        a, mult, ig = _rg_gates(xc, wr_ref[...].astype(bf16), br_ref[...], wi_ref[...].astype(bf16), bi_ref[...], sp)
        if PAST_LEN == 0:
            mult = jnp.ones_like(mult)
        h = a * h0_ref[...] + mult * (ig * xc)
        h_ref[...] = h
        y_ref[...] = (h * jax.nn.gelu(gg_ref[...])).astype(bf16)

    blk = lambda n: (0, n)
    vec = pl.BlockSpec((1, LANES), blk)
    mat = pl.BlockSpec((Bs, LANES), blk)
    y, h = pl.pallas_call(
        body,
        grid=(NB,),
        in_specs=[pl.BlockSpec((Bs, LANES), lambda n: (0, x_blk + n)),
                  pl.BlockSpec((Bs, LANES), lambda n: (0, g_blk + n)),
                  mat, mat, mat, mat,
                  pl.BlockSpec((4, LANES), blk), vec,
                  pl.BlockSpec((None, BS, BS), lambda n: (n, 0, 0)), vec,
                  pl.BlockSpec((None, BS, BS), lambda n: (n, 0, 0)), vec, vec],
        out_specs=[mat, mat],
        out_shape=[jax.ShapeDtypeStruct((Bs, W), bf16), jax.ShapeDtypeStruct((Bs, W), f32)],
        compiler_params=_params(("parallel",), 32),
        name="rglru_sample",
    )(p0, p0, s0, s1, s2, h0, conv_w, conv_b.reshape(1, W), w_r, b_r.reshape(1, W), w_i, b_i.reshape(1, W),
      lam.reshape(1, W))
    return y, h


def _blockdiag_tiles(w, tile):
    nblk, bs, _ = w.shape
    per = tile // bs
    nt = nblk // per
    rows = jnp.tile(w.reshape(nt, tile, bs), (1, 1, per))
    on_diag = (jnp.arange(tile)[:, None] // bs) == (jnp.arange(tile)[None, :] // bs)
    return jnp.where(on_diag[None], rows, 0.0)


def _mlstm_stage1(p1, conv_w, conv_b, wq_t, wk_t, wv_t, wg, bg, *, seq_len, tr, H, DH, conv_state=None):
    M = p1.shape[0]
    W = H * DH
    NTL, TL, _ = wq_t.shape
    sample = conv_state is not None
    tiles_per_seq = max(seq_len // tr, 1)
    kscale = DH ** -0.5

    def body(*refs):
        it = iter(refs)
        x_ref = next(it)
        if sample:
            s0_ref, s1_ref, s2_ref = next(it), next(it), next(it)
        else:
            xh_ref = next(it)
        cw_ref, cb_ref, wq_ref, wk_ref, wv_ref, wg_ref, bg_ref = (next(it) for _ in range(7))
        q_ref, k_ref, v_ref, g_ref, xc_ref = (next(it) for _ in range(5))
        cw = cw_ref[...]
        x = x_ref[...].astype(f32)
        if sample:
            conv = cb_ref[...] + cw[0:1] * s0_ref[...] + cw[1:2] * s1_ref[...] + cw[2:3] * s2_ref[...] + cw[3:4] * x
        else:
            xs_scr = next(it)
            keep = (pl.program_id(0) % tiles_per_seq != 0).astype(f32)
            xs_scr[0:SUBLANES, :] = xh_ref[...].astype(f32)[HALO - SUBLANES:] * keep
            xs_scr[SUBLANES:, :] = x
            conv = (cb_ref[...] + cw[0:1] * xs_scr[SUBLANES - 3:SUBLANES - 3 + tr, :]
                    + cw[1:2] * xs_scr[SUBLANES - 2:SUBLANES - 2 + tr, :]
                    + cw[2:3] * xs_scr[SUBLANES - 1:SUBLANES - 1 + tr, :] + cw[3:4] * x)
        xc = conv * jax.nn.sigmoid(conv)
        xc_ref[...] = xc.astype(xc_ref.dtype)
        xcb = xc.astype(bf16)
        xb = x.astype(bf16)
        qs, ks, vs = [], [], []
        for t in range(NTL):
            cs = slice(t * TL, (t + 1) * TL)
            qs.append(_dot(xcb[:, cs], wq_ref[t]))
            ks.append(_dot(xcb[:, cs], wk_ref[t]) * kscale)
            vs.append(_dot(xb[:, cs], wv_ref[t]))
        q = jnp.concatenate(qs, axis=1)
        k = jnp.concatenate(ks, axis=1)
        v = jnp.concatenate(vs, axis=1)
        q_ref[...] = q.astype(q_ref.dtype)
        k_ref[...] = k.astype(k_ref.dtype)
        v_ref[...] = v.astype(v_ref.dtype)
        gt = (_dot(q.astype(bf16), wg_ref[0:W, :]) + _dot(k.astype(bf16), wg_ref[W:2 * W, :])
              + _dot(v.astype(bf16), wg_ref[2 * W:3 * W, :]) + bg_ref[...])
        lane = lax.broadcasted_iota(jnp.int32, gt.shape, 1)
        g_ref[...] = jnp.where(jnp.logical_and(lane >= H, lane < 2 * H), jax.nn.log_sigmoid(gt), gt)

    row = lambda i: (i, 0)
    const2 = lambda i: (0, 0)
    const3 = lambda i: (0, 0, 0)
    in_specs = [pl.BlockSpec((tr, W), row)]
    args = [p1]
    if sample:
        in_specs += [pl.BlockSpec((tr, W), row)] * 3
        args += [conv_state[:, 0], conv_state[:, 1], conv_state[:, 2]]
    else:
        in_specs.append(pl.BlockSpec((HALO, W), lambda i: (jnp.maximum(i * (tr // HALO) - 1, 0), 0)))
        args.append(p1)
    in_specs += [pl.BlockSpec((4, W), const2), pl.BlockSpec((1, W), const2),
                 pl.BlockSpec((NTL, TL, TL), const3), pl.BlockSpec((NTL, TL, TL), const3),
                 pl.BlockSpec((NTL, TL, TL), const3),
                 pl.BlockSpec((3 * W, LANES), const2), pl.BlockSpec((1, LANES), const2)]
    args += [conv_w, conv_b.reshape(1, W), wq_t, wk_t, wv_t, wg, bg]
    return pl.pallas_call(
        body,
        grid=(M // tr,),
        in_specs=in_specs,
        out_specs=[pl.BlockSpec((tr, W), row)] * 3 + [pl.BlockSpec((tr, LANES), row), pl.BlockSpec((tr, W), row)],
        out_shape=[jax.ShapeDtypeStruct((M, W), f32 if sample else bf16)] * 3
        + [jax.ShapeDtypeStruct((M, LANES), f32), jax.ShapeDtypeStruct((M, W), f32 if sample else bf16)],
        scratch_shapes=[] if sample else [pltpu.VMEM((tr + SUBLANES, W), f32)],
        compiler_params=_params(("parallel",), 48),
        name="mlstm_stage1_sample" if sample else "mlstm_stage1",
    )(*args)


def _mlstm_prompt(q, k, v, gates, xc, p1, g_norm, skip, *, B, L, H, DH, tb):
    NT = L // tb
    NC = tb // CHUNK
    C = CHUNK

    def body(q_ref, k_ref, v_ref, g_ref, xc_ref, om_ref, gn_ref, sk_ref,
             y_ref, c_out, n_out, m_out, c_scr, n_scr, m_scr):
        hh = pl.program_id(1)
        t = pl.program_id(2)

        @pl.when(t == 0)
        def _():
            c_scr[...] = jnp.zeros_like(c_scr)
            n_scr[...] = jnp.zeros_like(n_scr)
            m_scr[...] = jnp.zeros_like(m_scr)

        gts = g_ref[...]
        lane = lax.broadcasted_iota(jnp.int32, (tb, LANES), 1)
        i_col = jnp.sum(jnp.where(lane == hh, gts, 0.0), axis=1, keepdims=True)
        b_col = jnp.sum(jnp.where(lane == H + hh, _chunk_cumsum(gts, C), 0.0), axis=1, keepdims=True)
        b3 = b_col.reshape(NC, C, 1)
        i3 = i_col.reshape(NC, C, 1)
        ii = lax.broadcasted_iota(jnp.int32, (1, C, C), 1)
        jj = lax.broadcasted_iota(jnp.int32, (1, C, C), 2)
        eye = ii == jj
        causal = ii >= jj
        as_row = lambda col3: jnp.sum(jnp.where(eye, jnp.broadcast_to(col3, (NC, C, C)), 0.0), axis=1, keepdims=True)
        dmat = jnp.where(causal, b3 - as_row(b3) + as_row(i3), -jnp.inf)
        rmax = jnp.max(dmat, axis=2, keepdims=True)
        b_last = b3[:, C - 1:C, :]
        m_prev = m_scr[...]
        m_in = []
        for c in range(NC):
            m_in.append(m_prev)
            m_prev = jnp.maximum(b_last[c] + m_prev, rmax[c][C - 1:C, :])
        m_scr[...] = m_prev
        inter = b3 + jnp.stack(m_in)
        m_col = jnp.maximum(inter, rmax)
        g_col = jnp.exp(inter - m_col)
        q3 = q_ref[...].reshape(NC, C, DH)
        k3 = k_ref[...].reshape(NC, C, DH)
        v3 = v_ref[...].reshape(NC, C, DH)
        s = jnp.einsum('cid,cjd->cij', q3, k3, preferred_element_type=f32) * jnp.exp(dmat - m_col)
        num = jnp.einsum('cij,cje->cie', s.astype(bf16), v3, preferred_element_type=f32)
        den = jnp.sum(s, axis=2, keepdims=True)
        m_new = m_col[:, C - 1:C, :]
        wk = jnp.exp(b_last - b3 + i3 - m_new)
        gc = jnp.exp(inter[:, C - 1:C, :] - m_new)
        kw = k3.astype(f32) * wk
        dc = jnp.einsum('cse,csd->ced', v3, kw.astype(bf16), preferred_element_type=f32)
        dn = jnp.sum(kw, axis=1, keepdims=True)
        cs = c_scr[...]
        ns = n_scr[...]
        c_in, n_in = [], []
        for c in range(NC):
            c_in.append(cs)
            n_in.append(ns)
            cs = gc[c] * cs + dc[c]
            ns = gc[c] * ns + dn[c]
        c_scr[...] = cs
        n_scr[...] = ns
        c_all = jnp.stack(c_in).astype(bf16)
        n_all = jnp.stack(n_in)
        num = num + g_col * jnp.einsum('cid,ced->cie', q3, c_all, preferred_element_type=f32)
        den = den + g_col * jnp.sum(q3.astype(f32) * n_all, axis=2, keepdims=True)
        hm = (num / jnp.maximum(jnp.abs(den), jnp.exp(-m_col))).reshape(tb, DH)
        y = (_rms(hm, gn_ref[...]) + sk_ref[...] * xc_ref[...].astype(f32)) * jax.nn.sigmoid(om_ref[...].astype(f32))
        y_ref[...] = y.astype(bf16)

        @pl.when(t == NT - 1)
        def _():
            c_out[0, 0] = cs
            n_out[0, 0] = ns
            m_out[0, 0] = m_prev

    blk = lambda b, h, t: (b * NT + t, h)
    W = H * DH
    y, c_new, n_new, m_new = pl.pallas_call(
        body,
        grid=(B, H, NT),
        in_specs=[pl.BlockSpec((tb, DH), blk), pl.BlockSpec((tb, DH), blk), pl.BlockSpec((tb, DH), blk),
                  pl.BlockSpec((tb, LANES), lambda b, h, t: (b * NT + t, 0)),
                  pl.BlockSpec((tb, DH), blk),
                  pl.BlockSpec((tb, DH), lambda b, h, t: (b * NT + t, H + h)),
                  pl.BlockSpec((1, DH), lambda b, h, t: (0, h)),
                  pl.BlockSpec((1, DH), lambda b, h, t: (0, h))],
        out_specs=[pl.BlockSpec((tb, DH), blk),
                   pl.BlockSpec((1, 1, DH, DH), lambda b, h, t: (b, h, 0, 0)),
                   pl.BlockSpec((1, 1, 1, DH), lambda b, h, t: (b, h, 0, 0)),
                   pl.BlockSpec((1, 1, 1, 1), lambda b, h, t: (b, h, 0, 0))],
        out_shape=[jax.ShapeDtypeStruct((B * L, W), bf16),
                   jax.ShapeDtypeStruct((B, H, DH, DH), f32),
                   jax.ShapeDtypeStruct((B, H, 1, DH), f32),
                   jax.ShapeDtypeStruct((B, H, 1, 1), f32)],
        scratch_shapes=[pltpu.VMEM((DH, DH), f32), pltpu.VMEM((1, DH), f32), pltpu.VMEM((1, 1), f32)],
        compiler_params=_params(("parallel", "parallel", "arbitrary"), 32),
        name="mlstm_prompt",
    )(q, k, v, gates, xc, p1, g_norm.reshape(1, W), skip.reshape(1, W))
    return y, c_new, n_new.reshape(B, H, DH), m_new.reshape(B, H)


def _mlstm_sample(q, k, v, gates, xc, p1, g_norm, skip, c0, n0, m0, *, H, DH):
    Bs = q.shape[0]
    W = H * DH
    SB = SUBLANES

    def body(q_ref, k_ref, v_ref, g_ref, xc_ref, om_ref, gn_ref, sk_ref, c_ref, n_ref, m_ref,
             y_ref, c_out, n_out, m_out):
        gts = g_ref[...]
        for h in range(H):
            cs = slice(h * DH, (h + 1) * DH)
            ig = gts[:, h:h + 1]
            fg = gts[:, H + h:H + h + 1]
            inter = fg + m_ref[:, h:h + 1]
            m = jnp.maximum(inter, ig)
            g = jnp.exp(inter - m)
            m_out[:, h:h + 1] = m
            q = q_ref[:, cs]
            kw = k_ref[:, cs] * jnp.exp(ig - m)
            nn = g * n_ref[:, cs] + kw
            n_out[:, cs] = nn
            den = jnp.sum(nn * q, axis=1, keepdims=True)
            v_t = v_ref[:, cs].T
            qb = q.astype(bf16)
            nums = []
            for s in range(SB):
                cn = g[s:s + 1, :] * c_ref[s, h] + v_t[:, s:s + 1] * kw[s:s + 1, :]
                c_out[s, h] = cn
                nums.append(_dot_nt(qb, cn.astype(bf16))[s:s + 1, :])
            num = jnp.concatenate(nums, axis=0)
            hm = num / jnp.maximum(jnp.abs(den), jnp.exp(-m))
            y_ref[:, cs] = ((_rms(hm, gn_ref[:, cs]) + sk_ref[:, cs] * xc_ref[:, cs])
                            * jax.nn.sigmoid(om_ref[:, cs]))

    per = lambda b: (b, 0)
    const2 = lambda b: (0, 0)
    y, c_new, n_new, m_new = pl.pallas_call(
        body,
        grid=(Bs // SB,),
        in_specs=[pl.BlockSpec((SB, W), per), pl.BlockSpec((SB, W), per), pl.BlockSpec((SB, W), per),
                  pl.BlockSpec((SB, LANES), per), pl.BlockSpec((SB, W), per),
                  pl.BlockSpec((SB, W), lambda b: (b, 1)),
                  pl.BlockSpec((1, W), const2), pl.BlockSpec((1, W), const2),
                  pl.BlockSpec((SB, H, DH, DH), lambda b: (b, 0, 0, 0)),
                  pl.BlockSpec((SB, W), per), pl.BlockSpec((SB, H), per)],
        out_specs=[pl.BlockSpec((SB, W), per),
                   pl.BlockSpec((SB, H, DH, DH), lambda b: (b, 0, 0, 0)),
                   pl.BlockSpec((SB, W), per), pl.BlockSpec((SB, H), per)],
        out_shape=[jax.ShapeDtypeStruct((Bs, W), f32), jax.ShapeDtypeStruct((Bs, H, DH, DH), f32),
                   jax.ShapeDtypeStruct((Bs, W), f32), jax.ShapeDtypeStruct((Bs, H), f32)],
        compiler_params=_params(("parallel",), 48),
        name="mlstm_sample",
    )(q, k, v, gates, xc, p1, g_norm.reshape(1, W), skip.reshape(1, W), c0, n0.reshape(Bs, W), m0)
    return y, c_new, n_new.reshape(Bs, H, DH), m_new


S5_TILES = 8


def _s5_layouts(lam_re, lam_im, log_dt, b_re, b_im, c_re, c_im):
    G, P = lam_re.shape
    GC = b_re.shape[2]
    T = S5_TILES
    gpt = G // T
    ns = G * P
    flat = lambda a: a.reshape(ns)
    ldt = jnp.broadcast_to(log_dt[:, None], (G, P))
    rows = [flat(a).reshape(T, 1, ns // T) for a in (lam_re, lam_im, ldt)]
    eye = jnp.eye(gpt, dtype=f32)
    bbd = [jnp.einsum('jgpc,gh->jgchp', a.reshape(T, gpt, P, GC), eye).reshape(T, gpt * GC, gpt * P) for a in (b_re, b_im)]
    cbd = [jnp.einsum('jgcp,gh->jgphc', a.reshape(T, gpt, GC, P), eye).reshape(T, gpt * P, gpt * GC) for a in (c_re, c_im)]
    return rows, bbd, cbd


def _s5_discretise(lre, lim, ldt):
    dt = jnp.exp(ldt)
    mag = jnp.exp(dt * lre)
    ar = mag * jnp.cos(dt * lim)
    ai = mag * jnp.sin(dt * lim)
    den = lre * lre + lim * lim
    cr = ((ar - 1.0) * lre + ai * lim) / den
    ci = (ai * lre - (ar - 1.0) * lim) / den
    return ar, ai, cr, ci


def _s5_mixer(p1, u_blk, rows, bbd, cbd, d_skip, w_glu, b_glu, *, B, L, tb, state=None):
    T = S5_TILES
    lre_r, lim_r, ldt_r = rows
    SW = lre_r.shape[2]
    CW = bbd[0].shape[1]
    W = T * CW
    NS = T * SW
    KT = SW // LANES
    sample = state is not None
    NT = 1 if sample else L // tb
    M = B * L

    def body(*refs):
        it = iter(refs)
        u_ref = next(it)
        if sample:
            x0r_ref, x0i_ref = next(it), next(it)
        lre_ref, lim_ref, ldt_ref = next(it), next(it), next(it)
        bre_ref, bim_ref, cre_ref, cim_ref = next(it), next(it), next(it), next(it)
        d_ref, wg_ref, bgl_ref = next(it), next(it), next(it)
        y_ref, xr_out, xi_out = next(it), next(it), next(it)
        bbr, bbi, cpair, ar_scr, ai_scr = (next(it) for _ in range(5))
        if not sample:
            sre, sim, xr_c, xi_c, yacc = (next(it) for _ in range(5))
        first = jnp.logical_and(pl.program_id(0) == 0, pl.program_id(1) == 0)

        @pl.when(first)
        def _():
            for j in range(T):
                ar, ai, cr, ci = _s5_discretise(lre_ref[j], lim_ref[j], ldt_ref[j])
                ar_scr[j] = ar
                ai_scr[j] = ai
                br = bre_ref[j]
                bi = bim_ref[j]
                bbr[j] = (cr * br - ci * bi).astype(bf16)
                bbi[j] = (cr * bi + ci * br).astype(bf16)
            for jp in range(T // 2):
                cpair[jp] = jnp.zeros((4 * SW, 2 * CW), bf16)
                for half in range(2):
                    j = 2 * jp + half
                    r0 = 2 * half * SW
                    cols = slice(half * CW, (half + 1) * CW)
                    cpair[jp, r0:r0 + SW, cols] = cre_ref[j].astype(bf16)
                    cpair[jp, r0 + SW:r0 + 2 * SW, cols] = (-cim_ref[j]).astype(bf16)

        u = u_ref[...].astype(f32)
        ub = u_ref[...].astype(bf16)
        ys = []
        if sample:
            for j in range(T):
                cs = slice(j * SW, (j + 1) * SW)
                uj = ub[:, j * CW:(j + 1) * CW]
                ar = ar_scr[j]
                ai = ai_scr[j]
                x0r = x0r_ref[:, cs]
                x0i = x0i_ref[:, cs]
                xr = ar * x0r - ai * x0i + _dot(uj, bbr[j])
                xi = ar * x0i + ai * x0r + _dot(uj, bbi[j])
                xr_out[:, cs] = xr
                xi_out[:, cs] = xi
                ys += [xr.astype(bf16), xi.astype(bf16)]
            y = jnp.concatenate([_dot(jnp.concatenate(ys[4 * jp:4 * jp + 4], axis=1), cpair[jp]) for jp in range(T // 2)],
                                axis=1)
        else:
            t = pl.program_id(1)

            @pl.when(t == 0)
            def _():
                xr_c[...] = jnp.zeros_like(xr_c)
                xi_c[...] = jnp.zeros_like(xi_c)

            for j in range(T):
                uj = ub[:, j * CW:(j + 1) * CW]
                r = _dot(uj, bbr[j])
                im = _dot(uj, bbi[j])
                for kk in range(KT):
                    sre[kk, pl.ds(j, tb, stride=T), :] = r[:, kk * LANES:(kk + 1) * LANES]
                    sim[kk, pl.ds(j, tb, stride=T), :] = im[:, kk * LANES:(kk + 1) * LANES]
            a_r = [jnp.concatenate([ar_scr[j][:, kk * LANES:(kk + 1) * LANES] for j in range(T)], axis=0) for kk in range(KT)]
            a_i = [jnp.concatenate([ai_scr[j][:, kk * LANES:(kk + 1) * LANES] for j in range(T)], axis=0) for kk in range(KT)]

            def step(s, carry):
                xr, xi = carry
                row = pl.ds(pl.multiple_of(s * T, T), T)
                nr, ni = [], []
                for kk in range(KT):
                    r_ = a_r[kk] * xr[kk] - a_i[kk] * xi[kk] + sre[kk, row, :]
                    i_ = a_r[kk] * xi[kk] + a_i[kk] * xr[kk] + sim[kk, row, :]
                    sre[kk, row, :] = r_
                    sim[kk, row, :] = i_
                    nr.append(r_)
                    ni.append(i_)
                return tuple(nr), tuple(ni)
            xr0 = tuple(xr_c[kk] for kk in range(KT))
            xi0 = tuple(xi_c[kk] for kk in range(KT))
            xr, xi = lax.fori_loop(0, tb, step, (xr0, xi0), unroll=4)
            for kk in range(KT):
                xr_c[kk] = xr[kk]
                xi_c[kk] = xi[kk]

            @pl.when(t == NT - 1)
            def _():
                for kk in range(KT):
                    xr_out[kk] = xr[kk]
                    xi_out[kk] = xi[kk]

            for jp in range(T // 2):
                parts = []
                for j in (2 * jp, 2 * jp + 1):
                    parts += [sre[kk, pl.ds(j, tb, stride=T), :] for kk in range(KT)]
                    parts += [sim[kk, pl.ds(j, tb, stride=T), :] for kk in range(KT)]
                xp = jnp.concatenate(parts, axis=1).astype(bf16)
                yacc[:, 2 * jp * CW:(2 * jp + 2) * CW] = _dot(xp, cpair[jp])
            y = yacc[...]
        ysk = jax.nn.gelu(y + d_ref[...] * u)
        z = _dot(ysk.astype(bf16), wg_ref[...]) + bgl_ref[...]
        y_ref[...] = (ysk * jax.nn.sigmoid(z)).astype(bf16)

    c3 = lambda b, t: (0, 0, 0)
    c2 = lambda b, t: (0, 0)
    in_specs = [pl.BlockSpec((tb, W), lambda b, t: (b * NT + t, u_blk))]
    args = [p1]
    if sample:
        in_specs += [pl.BlockSpec((tb, NS), lambda b, t: (b, 0))] * 2
        args += [state[0], state[1]]
    in_specs += [pl.BlockSpec((T, 1, SW), c3)] * 3
    in_specs += [pl.BlockSpec((T, CW, SW), c3)] * 2 + [pl.BlockSpec((T, SW, CW), c3)] * 2
    in_specs += [pl.BlockSpec((1, W), c2), pl.BlockSpec((W, W), c2), pl.BlockSpec((1, W), c2)]
    args += [lre_r, lim_r, ldt_r, bbd[0], bbd[1], cbd[0], cbd[1], d_skip.reshape(1, W), w_glu, b_glu.reshape(1, W)]
    scratch = [pltpu.VMEM((T, CW, SW), bf16), pltpu.VMEM((T, CW, SW), bf16),
               pltpu.VMEM((T // 2, 4 * SW, 2 * CW), bf16),
               pltpu.VMEM((T, 1, SW), f32), pltpu.VMEM((T, 1, SW), f32)]
    if sample:
        grid = (M // tb, 1)
        st_spec = pl.BlockSpec((tb, NS), lambda b, t: (b, 0))
        st_shape = jax.ShapeDtypeStruct((M, NS), f32)
    else:
        grid = (B, NT)
        st_spec = pl.BlockSpec((None, KT, T, LANES), lambda b, t: (b, 0, 0, 0))
        st_shape = jax.ShapeDtypeStruct((B, KT, T, LANES), f32)
        scratch += [pltpu.VMEM((KT, tb * T, LANES), f32), pltpu.VMEM((KT, tb * T, LANES), f32),
                    pltpu.VMEM((KT, T, LANES), f32), pltpu.VMEM((KT, T, LANES), f32), pltpu.VMEM((tb, W), f32)]
    y, xr, xi = pl.pallas_call(
        body,
        grid=grid,
        in_specs=in_specs,
        out_specs=[pl.BlockSpec((tb, W), lambda b, t: (b * NT + t, 0)), st_spec, st_spec],
        out_shape=[jax.ShapeDtypeStruct((M, W), bf16), st_shape, st_shape],
        scratch_shapes=scratch,
        compiler_params=_params(("arbitrary", "arbitrary"), 56),
        name="s5_sample" if sample else "s5_prompt",
    )(*args)
    if not sample:
        xr = jnp.transpose(xr, (0, 2, 1, 3)).reshape(B, NS)
        xi = jnp.transpose(xi, (0, 2, 1, 3)).reshape(B, NS)
    return y, xr, xi


def _trunk(x3, st, w, *, sample):
    B, L, D = x3.shape
    M = B * L
    x = x3.reshape(M, D)
    H_g, DK, DV = st['gla_S'].shape[1:] if sample else w['gla_dims']
    H_m, DH = w['ml_dims']
    G, P = w['s5_dims']
    tm = M if sample else min(1024, L)
    tf = M if sample else min(1024, L)
    to = M if sample else min(2048, M)
    tb = min(1024, L)
    pdt = f32 if sample else bf16
    out = {}

    p0, alr = _norm_matmul(x, w['g_mix0'], w['w_in0_main'], tm=tm, tn=1024, w_side=w['w_in0_alr'], out_dtype=pdt, w_t=True,
                            seg_cols=w['w_in0_cols'])
    x_blk = (2 * H_g * DK + 2 * H_g * DV) // LANES
    W_rg = w['rg_lambda'].shape[0]
    g_blk = x_blk + W_rg // LANES
    rg_tail = p0.reshape(B, L, p0.shape[1])[:, max(L - 3, 0):, x_blk * LANES:x_blk * LANES + W_rg].astype(f32)
    if sample:
        ya, out['gla_S'] = _gla_sample(p0, alr, w['gla_w2p'], w['gla_b_alpha'], w['gla_g_norm'], st['gla_S'],
                                       H=H_g, DK=DK, DV=DV)
        yb, out['rg_h'] = _rglru_sample(p0, st['rg_conv'], st['rg_h'], w['rg_conv_w'], w['rg_conv_b'], w['rg_w_r'],
                                        w['rg_b_r'], w['rg_w_i'], w['rg_b_i'], w['rg_lambda'], x_blk=x_blk, g_blk=g_blk)
        out['rg_conv'] = jnp.concatenate([st['rg_conv'][:, 1:], rg_tail], axis=1)
    else:
        ya, out['gla_S'] = _gla_prompt(p0, alr, w['gla_w2p'], w['gla_b_alpha'], w['gla_g_norm'],
                                       B=B, L=L, H=H_g, DK=DK, DV=DV, tb=tb)
        yb, out['rg_h'] = _rglru_prompt(p0, w['rg_conv_w'], w['rg_conv_b'], w['rg_w_r'], w['rg_b_r'], w['rg_w_i'],
                                        w['rg_b_i'], w['rg_lambda'], B=B, L=L, x_blk=x_blk, g_blk=g_blk)
        out['rg_conv'] = rg_tail
    x = _out_proj(ya, yb, w['w_out0'], x, tm=to, tn=512)
    x, ffn0 = _conv_ffn(x, w['g_ffn'][0], w['ffn'][0], seq_len=L, tm=tf, prev=st['ffn_conv'] if sample else None)

    W_ml = H_m * DH
    p1 = _norm_matmul(x, w['g_mix1'], [w['w_in1']], tm=tm, tn=1024, out_dtype=pdt)
    xm_tail = p1.reshape(B, L, p1.shape[1])[:, max(L - 3, 0):, :W_ml].astype(f32)
    q, k, v, gates, xc = _mlstm_stage1(p1, w['ml_conv_w'], w['ml_conv_b'], w['ml_wq_t'], w['ml_wk_t'], w['ml_wv_t'],
                                       w['ml_wg'], w['ml_bg'], seq_len=L, tr=M if sample else min(256, L), H=H_m, DH=DH,
                                       conv_state=st['ml_conv'] if sample else None)
    u_blk = 2 * W_ml // (G * w['s5_gc'])
    if sample:
        yc, out['ml_C'], out['ml_n'], out['ml_m'] = _mlstm_sample(
            q, k, v, gates, xc, p1, w['ml_g_norm'], w['ml_skip'], st['ml_C'], st['ml_n'], st['ml_m'], H=H_m, DH=DH)
        out['ml_conv'] = jnp.concatenate([st['ml_conv'][:, 1:], xm_tail], axis=1)
        yd, s5r, s5i = _s5_mixer(p1, u_blk, w['s5_rows'], w['s5_bbd'], w['s5_cbd'], w['s5_D'], w['s5_w_glu'],
                                 w['s5_b_glu'], B=B, L=1, tb=B,
                                 state=(st['s5_re'].reshape(B, G * P), st['s5_im'].reshape(B, G * P)))
    else:
        yc, out['ml_C'], out['ml_n'], out['ml_m'] = _mlstm_prompt(
            q, k, v, gates, xc, p1, w['ml_g_norm'], w['ml_skip'], B=B, L=L, H=H_m, DH=DH, tb=tb)
        out['ml_conv'] = xm_tail
        yd, s5r, s5i = _s5_mixer(p1, u_blk, w['s5_rows'], w['s5_bbd'], w['s5_cbd'], w['s5_D'], w['s5_w_glu'],
                                 w['s5_b_glu'], B=B, L=L, tb=min(512, L))
    out['s5_re'] = s5r.reshape(B, G, P)
    out['s5_im'] = s5i.reshape(B, G, P)
    x = _out_proj(yc, yd, w['w_out1'], x, tm=to, tn=512)
    x, ffn1 = _conv_ffn(x, w['g_ffn'][1], w['ffn'][1], seq_len=L, tm=tf, prev=st['ffn_conv'] if sample else None,
                        final_g=w['g_final'])
    out['ffn_conv'] = jnp.stack([ffn0, ffn1], axis=0)
    return x.reshape(B, L, D), out


def kernel(x_prompt, x_sample, state_gla_S, state_rglru_h, state_rglru_conv, state_mlstm_C, state_mlstm_n, state_mlstm_m, state_mlstm_conv, state_s5_re, state_s5_im, state_ffn_conv, g_mix0, w_in0, gla_w_alpha2, gla_b_alpha, gla_g_norm, rg_conv_w, rg_conv_b, rg_w_r, rg_b_r, rg_w_i, rg_b_i, rg_lambda, w_out0, g_mix1, w_in1, ml_conv_w, ml_conv_b, ml_wq, ml_wk, ml_wv, ml_w_igate, ml_b_igate, ml_w_fgate, ml_b_fgate, ml_g_norm, ml_skip, s5_lam_re, s5_lam_im, s5_log_dt, s5_B_re, s5_B_im, s5_C_re, s5_C_im, s5_D, s5_w_glu, s5_b_glu, w_out1, g_ffn, ffn_w_up, ffn_conv_w, ffn_conv_b, ffn_w_down, g_final):
    _, H_g, DK, DV = state_gla_S.shape
    _, H_m, DH, _ = state_mlstm_C.shape
    G, P = s5_lam_re.shape
    rank = gla_w_alpha2.shape[0]
    n_main = 2 * H_g * DK + 2 * H_g * DV
    w_in0_t = w_in0.T.astype(bf16)
    w_in0_main = [w_in0_t, w_in0_t[n_main + rank:]]
    w_in0_cols = [n_main, w_in0.shape[1] - n_main - rank]
    w_in0_alr = jnp.pad(w_in0_t[n_main:n_main + rank], ((0, LANES - rank), (0, 0)))
    gla_w2p = jnp.pad(gla_w_alpha2, ((0, LANES - rank), (0, 0)))
    ml_tile = 256
    ml_wg = jnp.pad(jnp.concatenate([ml_w_igate, ml_w_fgate], axis=1), ((0, 0), (0, LANES - 2 * H_m))).astype(bf16)
    ml_bg = jnp.pad(jnp.concatenate([ml_b_igate, ml_b_fgate]), (0, LANES - 2 * H_m)).reshape(1, LANES)
    rows, bbd, cbd = _s5_layouts(s5_lam_re, s5_lam_im, s5_log_dt, s5_B_re, s5_B_im, s5_C_re, s5_C_im)
    w = dict(
        g_mix0=g_mix0, w_in0_main=w_in0_main, w_in0_cols=w_in0_cols, w_in0_alr=w_in0_alr, gla_w2p=gla_w2p, gla_b_alpha=gla_b_alpha,
        gla_g_norm=gla_g_norm, gla_dims=(H_g, DK, DV), rg_conv_w=rg_conv_w, rg_conv_b=rg_conv_b, rg_w_r=rg_w_r,
        rg_b_r=rg_b_r, rg_w_i=rg_w_i, rg_b_i=rg_b_i, rg_lambda=rg_lambda, w_out0=w_out0.astype(bf16),
        g_mix1=g_mix1, w_in1=w_in1.astype(bf16), ml_conv_w=ml_conv_w, ml_conv_b=ml_conv_b,
        ml_wq_t=_blockdiag_tiles(ml_wq, ml_tile).astype(bf16), ml_wk_t=_blockdiag_tiles(ml_wk, ml_tile).astype(bf16),
        ml_wv_t=_blockdiag_tiles(ml_wv, ml_tile).astype(bf16), ml_wg=ml_wg, ml_bg=ml_bg, ml_g_norm=ml_g_norm,
        ml_skip=ml_skip, ml_dims=(H_m, DH), s5_dims=(G, P), s5_gc=s5_B_re.shape[2], s5_rows=rows, s5_bbd=bbd,
        s5_cbd=cbd, s5_D=s5_D, s5_w_glu=s5_w_glu.astype(bf16), s5_b_glu=s5_b_glu, w_out1=w_out1.astype(bf16),
        g_ffn=g_ffn, g_final=g_final,
        ffn=_ffn_prepare(ffn_w_up, ffn_conv_w, ffn_conv_b, ffn_w_down))
    st_s = dict(gla_S=state_gla_S, rg_h=state_rglru_h, rg_conv=state_rglru_conv, ml_C=state_mlstm_C,
                ml_n=state_mlstm_n, ml_m=state_mlstm_m, ml_conv=state_mlstm_conv, s5_re=state_s5_re,
                s5_im=state_s5_im, ffn_conv=state_ffn_conv)
    y_p, np_ = _trunk(x_prompt, None, w, sample=False)
    y_s, ns_ = _trunk(x_sample, st_s, w, sample=True)
    names = ('gla_S', 'rg_h', 'rg_conv', 'ml_C', 'ml_n', 'ml_m', 'ml_conv', 's5_re', 's5_im', 'ffn_conv')
    outs = [y_p, y_s]
    for nme in names:
        outs += [np_[nme], ns_[nme]]
    return tuple(outs)
```

```python
import jax
import jax.numpy as jnp
from jax import lax
from jax.experimental import pallas as pl
from jax.experimental.pallas import tpu as pltpu

f32 = jnp.float32
bf16 = jnp.bfloat16

EPS = 1e-6
CHUNK = 64
GLA_TAU = 16.0
RG_C = 8.0
PAST_LEN = 16384
LANES = 128
SUBLANES = 8
HALO = 16
MIB = 1024 * 1024


def _params(sem, vmem_mib):
    return pltpu.CompilerParams(dimension_semantics=sem, vmem_limit_bytes=int(vmem_mib * MIB))


def _dot(a, b):
    return jnp.dot(a, b, preferred_element_type=f32)


def _dot_nt(a, b):
    return lax.dot_general(a, b, (((1,), (1,)), ((), ())), preferred_element_type=f32)


def _dot_tn(a, b):
    return lax.dot_general(a, b, (((0,), (0,)), ((), ())), preferred_element_type=f32)


def _rms(x, g):
    return x * lax.rsqrt(jnp.mean(x * x, axis=-1, keepdims=True) + EPS) * g


def _eye(n):
    return lax.broadcasted_iota(jnp.int32, (n, n), 0) == lax.broadcasted_iota(jnp.int32, (n, n), 1)


def _col_from_row(row):
    n = row.shape[1]
    return jnp.sum(jnp.where(_eye(n), jnp.broadcast_to(row, (n, n)), 0.0), axis=1, keepdims=True)


def _chunk_cumsum(x, chunk):
    pos = lax.broadcasted_iota(jnp.int32, (x.shape[0], 1), 0) % chunk
    step = 1
    while step < chunk:
        x = x + jnp.where(pos >= step, pltpu.roll(x, step, 0), 0.0)
        step *= 2
    return x


def _norm_matmul(x, g, ws, *, tm, tn, w_side=None, out_dtype=f32, w_t=False, seg_cols=None):
    M, D = x.shape
    mm = _dot_nt if w_t else _dot
    seg_cols = seg_cols or [w.shape[0 if w_t else 1] for w in ws]
    nblk = [c // tn for c in seg_cols]
    start = [sum(nblk[:s]) for s in range(len(ws) + 1)]
    N = tn * start[-1]
    rc = min(tm, 256)
    side = w_side is not None

    def body(*refs):
        x_ref, g_ref = refs[:2]
        w_refs = refs[2:2 + len(ws)]
        rest = refs[2 + len(ws):]
        if side:
            ws_ref, o_ref, os_ref, xn_ref = rest
        else:
            o_ref, xn_ref = rest
        j = pl.program_id(1)

        @pl.when(j == 0)
        def _():
            def chunk(r, c):
                rows = pl.ds(pl.multiple_of(r * rc, rc), rc)
                xn_ref[rows, :] = _rms(x_ref[rows, :], g_ref[...]).astype(bf16)
                return c
            lax.fori_loop(0, tm // rc, chunk, 0)
            if side:
                os_ref[...] = mm(xn_ref[...], ws_ref[...])

        if len(ws) == 1:
            o_ref[...] = mm(xn_ref[...], w_refs[0][...]).astype(out_dtype)
        else:
            for s, w_ref in enumerate(w_refs):
                @pl.when(jnp.logical_and(j >= start[s], j < start[s + 1]))
                def _(w_ref=w_ref):
                    o_ref[...] = mm(xn_ref[...], w_ref[...]).astype(out_dtype)

    in_specs = [pl.BlockSpec((tm, D), lambda i, j: (i, 0)),
                pl.BlockSpec((1, D), lambda i, j: (0, 0))]
    wblk = lambda j, s: jnp.clip(j - start[s], 0, nblk[s] - 1)
    if w_t:
        in_specs += [pl.BlockSpec((tn, D), lambda i, j, s=s: (wblk(j, s), 0)) for s in range(len(ws))]
    else:
        in_specs += [pl.BlockSpec((D, tn), lambda i, j, s=s: (0, wblk(j, s))) for s in range(len(ws))]
    out_specs = [pl.BlockSpec((tm, tn), lambda i, j: (i, j))]
    out_shape = [jax.ShapeDtypeStruct((M, N), out_dtype)]
    args = [x, g.reshape(1, D)] + list(ws)
    if side:
        ns = w_side.shape[0 if w_t else 1]
        in_specs.append(pl.BlockSpec((ns, D) if w_t else (D, ns), lambda i, j: (0, 0)))
        out_specs.append(pl.BlockSpec((tm, ns), lambda i, j: (i, 0)))
        out_shape.append(jax.ShapeDtypeStruct((M, ns), f32))
        args.append(w_side)
    outs = pl.pallas_call(
        body,
        grid=(M // tm, N // tn),
        in_specs=in_specs,
        out_specs=out_specs,
        out_shape=out_shape,
        scratch_shapes=[pltpu.VMEM((tm, D), bf16)],
        compiler_params=_params(("parallel", "arbitrary"), 56),
        name="norm_matmul",
    )(*args)
    return outs if side else outs[0]


def _out_proj(ya, yb, w, res, *, tm, tn):
    M, Ka = ya.shape
    Kb = yb.shape[1]
    N = w.shape[1]
    assert Ka == Kb and w.shape[0] == Ka + Kb

    def body(ya_ref, yb_ref, wa_ref, wb_ref, r_ref, o_ref):
        o_ref[...] = (r_ref[...] + _dot(ya_ref[...].astype(bf16), wa_ref[...])
                      + _dot(yb_ref[...].astype(bf16), wb_ref[...]))

    return pl.pallas_call(
        body,
        grid=(M // tm, N // tn),
        in_specs=[pl.BlockSpec((tm, Ka), lambda i, j: (i, 0)),
                  pl.BlockSpec((tm, Kb), lambda i, j: (i, 0)),
                  pl.BlockSpec((Ka, tn), lambda i, j: (0, j)),
                  pl.BlockSpec((Kb, tn), lambda i, j: (1, j)),
                  pl.BlockSpec((tm, tn), lambda i, j: (i, j))],
        out_specs=pl.BlockSpec((tm, tn), lambda i, j: (i, j)),
        out_shape=jax.ShapeDtypeStruct((M, N), f32),
        compiler_params=_params(("parallel", "arbitrary"), 48),
        name="out_proj",
    )(ya, yb, w, w, res)


FFN_STEP = 512
FFN_ROW_CHUNK = 512


def _ffn_to_steps(a, F):
    nsteps = -(-F // FFN_STEP)
    lead = a.shape[:-1]
    nd = len(lead)
    gv = a.reshape(lead + (2, F))
    gv = jnp.pad(gv, [(0, 0)] * (nd + 1) + [(0, nsteps * FFN_STEP - F)])
    gv = gv.reshape(lead + (2, nsteps, FFN_STEP))
    gv = jnp.transpose(gv, (nd + 1,) + tuple(range(nd)) + (nd, nd + 2))
    return gv.reshape((nsteps,) + lead + (2 * FFN_STEP,))


def _ffn_prepare(w_up, conv_w, conv_b, w_down):
    NL, F, D = w_down.shape
    assert F % LANES == 0 and FFN_STEP % LANES == 0
    nt = F // LANES
    tp = FFN_STEP // LANES
    nsteps = -(-F // FFN_STEP)

    def body(*refs):
        g_in, v_in, d_in = refs[:tp], refs[tp:2 * tp], refs[2 * tp:3 * tp]
        wu_o, wd_o = refs[3 * tp:]
        for t in range(tp):
            keep = tp * pl.program_id(1) + t < nt
            cols = slice(t * LANES, (t + 1) * LANES)
            wu_o[:, cols] = jnp.where(keep, g_in[t][...], 0.0).astype(bf16)
            wu_o[:, FFN_STEP + t * LANES:FFN_STEP + (t + 1) * LANES] = jnp.where(keep, v_in[t][...], 0.0).astype(bf16)
            wd_o[cols, :] = jnp.where(keep, d_in[t][...], 0.0).astype(bf16)

    tile = lambda j, t: jnp.minimum(tp * j + t, nt - 1)
    in_specs = ([pl.BlockSpec((None, D, LANES), lambda l, j, t=t: (l, 0, tile(j, t))) for t in range(tp)]
                + [pl.BlockSpec((None, D, LANES), lambda l, j, t=t: (l, 0, nt + tile(j, t))) for t in range(tp)]
                + [pl.BlockSpec((None, LANES, D), lambda l, j, t=t: (l, tile(j, t), 0)) for t in range(tp)])
    wu, wd = pl.pallas_call(
        body,
        grid=(NL, nsteps),
        in_specs=in_specs,
        out_specs=[pl.BlockSpec((None, None, D, 2 * FFN_STEP), lambda l, j: (l, j, 0, 0)),
                   pl.BlockSpec((None, None, FFN_STEP, D), lambda l, j: (l, j, 0, 0))],
        out_shape=[jax.ShapeDtypeStruct((NL, nsteps, D, 2 * FFN_STEP), bf16),
                   jax.ShapeDtypeStruct((NL, nsteps, FFN_STEP, D), bf16)],
        compiler_params=_params(("parallel", "parallel"), 40),
        name="ffn_weight_layout",
    )(*([w_up] * (2 * tp) + [w_down] * tp))
    return [dict(wu=wu, wd=wd, layer=l, F=F, cw=_ffn_to_steps(conv_w[l], F),
                 cb=_ffn_to_steps(conv_b[l], F).reshape(nsteps, 1, 2 * FFN_STEP)) for l in range(NL)]


def _conv_ffn(x, g, fw, *, seq_len, tm, prev=None, final_g=None):
    M, D = x.shape
    F = fw['F']
    layer = fw['layer']
    nsteps = fw['wu'].shape[1]
    sample = prev is not None
    rc = min(tm, FFN_ROW_CHUNK)
    tiles_per_seq = max(seq_len // tm, 1)
    W2 = 2 * FFN_STEP

    def body(*refs):
        it = iter(refs)
        x_ref = next(it)
        xh_ref = None if sample else next(it)
        prev_refs = [next(it) for _ in range(2 * FFN_STEP // LANES)] if sample else None
        g_ref = next(it)
        wu_ref, cw_ref, cb_ref, wd_ref = next(it), next(it), next(it), next(it)
        fg_ref = next(it) if final_g is not None else None
        o_ref, tail_g_ref, tail_v_ref = next(it), next(it), next(it)
        xn_ref, up_scr = next(it), next(it)
        i = pl.program_id(0)
        j = pl.program_id(1)

        @pl.when(j == 0)
        def _():
            if sample:
                xn_ref[0:HALO, :] = jnp.zeros((HALO, D), bf16)
            else:
                keep = (i % tiles_per_seq != 0).astype(f32)
                hist = _rms(xh_ref[...], g_ref[...]) * keep
                xn_ref[0:HALO, :] = jnp.concatenate([jnp.zeros_like(hist), hist], axis=0).astype(bf16)

            def chunk(r, c):
                rows = pl.ds(pl.multiple_of(r * rc, rc), rc)
                xr = x_ref[rows, :]
                o_ref[rows, :] = xr
                xn_ref[pl.ds(pl.multiple_of(HALO + r * rc, HALO), rc), :] = _rms(xr, g_ref[...]).astype(bf16)
                return c
            lax.fori_loop(0, tm // rc, chunk, 0)

        cw = cw_ref[0]
        cb = cb_ref[0]
        nchunk = tm // rc
        for r in range(nchunk):
            if sample:
                up_scr[r, HALO:, :] = _dot(xn_ref[pl.ds(HALO + r * rc, rc), :], wu_ref[...])
            else:
                up_scr[r] = _dot(xn_ref[pl.ds(r * rc, rc + HALO), :], wu_ref[...])
        for r in range(nchunk):
            rows = pl.ds(r * rc, rc)
            up = up_scr[r, HALO:, :]
            if sample:
                p0 = jnp.concatenate([p[rows, 0, :] for p in prev_refs], axis=1)
                p1 = jnp.concatenate([p[rows, 1, :] for p in prev_refs], axis=1)
                conv = cb + cw[0:1] * p0 + cw[1:2] * p1 + cw[2:3] * up
                tail_g_ref[rows, 0, :] = p1[:, :FFN_STEP]
                tail_g_ref[rows, 1, :] = up[:, :FFN_STEP]
                tail_v_ref[rows, 0, :] = p1[:, FFN_STEP:]
                tail_v_ref[rows, 1, :] = up[:, FFN_STEP:]
            else:
                conv = (cb + cw[0:1] * up_scr[r, HALO - 2:HALO - 2 + rc, :]
                        + cw[1:2] * up_scr[r, HALO - 1:HALO - 1 + rc, :] + cw[2:3] * up)
                if r == nchunk - 1:
                    tail_g_ref[...] = up[rc - SUBLANES:, :FFN_STEP]
                    tail_v_ref[...] = up[rc - SUBLANES:, FFN_STEP:]
            h = jax.nn.gelu(conv[:, :FFN_STEP]) * conv[:, FFN_STEP:]
            o_ref[rows, :] += _dot(h.astype(bf16), wd_ref[...])

        if final_g is not None:
            @pl.when(j == nsteps - 1)
            def _():
                def chunk2(r, c):
                    rows = pl.ds(pl.multiple_of(r * rc, rc), rc)
                    o_ref[rows, :] = _rms(o_ref[rows, :], fg_ref[...])
                    return c
                lax.fori_loop(0, tm // rc, chunk2, 0)

    in_specs = [pl.BlockSpec((tm, D), lambda i, j: (i, 0))]
    args = [x]
    if sample:
        nt = F // LANES
        tp = FFN_STEP // LANES
        last = 2 * nt - 1
        for off in list(range(tp)) + [nt + t for t in range(tp)]:
            in_specs.append(pl.BlockSpec((None, tm, 2, LANES),
                                         lambda i, j, off=off: (layer, i, 0, jnp.minimum(tp * j + off, last))))
            args.append(prev)
    else:
        in_specs.append(pl.BlockSpec((SUBLANES, D), lambda i, j: (jnp.maximum(i * (tm // SUBLANES) - 1, 0), 0)))
        args.append(x)
    in_specs += [pl.BlockSpec((1, D), lambda i, j: (0, 0)),
                 pl.BlockSpec((None, None, D, W2), lambda i, j: (layer, j, 0, 0)),
                 pl.BlockSpec((1, 3, W2), lambda i, j: (j, 0, 0)),
                 pl.BlockSpec((1, 1, W2), lambda i, j: (j, 0, 0)),
                 pl.BlockSpec((None, None, FFN_STEP, D), lambda i, j: (layer, j, 0, 0))]
    args += [g.reshape(1, D), fw['wu'], fw['cw'], fw['cb'], fw['wd']]
    if final_g is not None:
        in_specs.append(pl.BlockSpec((1, D), lambda i, j: (0, 0)))
        args.append(final_g.reshape(1, D))
    FP = nsteps * FFN_STEP
    if sample:
        tail_shape = (M, 2, FP)
        tail_spec = pl.BlockSpec((tm, 2, FFN_STEP), lambda i, j: (i, 0, j))
    else:
        tail_shape = (M // tm, SUBLANES, FP)
        tail_spec = pl.BlockSpec((None, SUBLANES, FFN_STEP), lambda i, j: (i, 0, j))
    out, tail_g, tail_v = pl.pallas_call(
        body,
        grid=(M // tm, nsteps),
        in_specs=in_specs,
        out_specs=[pl.BlockSpec((tm, D), lambda i, j: (i, 0)), tail_spec, tail_spec],
        out_shape=[jax.ShapeDtypeStruct((M, D), f32), jax.ShapeDtypeStruct(tail_shape, f32),
                   jax.ShapeDtypeStruct(tail_shape, f32)],
        scratch_shapes=[pltpu.VMEM((tm + HALO, D), bf16), pltpu.VMEM((tm // rc, rc + HALO, W2), f32)],
        compiler_params=_params(("parallel", "arbitrary"), 56),
        name="conv_ffn_sample" if sample else "conv_ffn",
    )(*args)
    if sample:
        new_buf = jnp.concatenate([tail_g[:, :, :F], tail_v[:, :, :F]], axis=-1)
    else:
        nseq = M // seq_len
        pick = lambda t: t.reshape(nseq, tiles_per_seq, SUBLANES, FP)[:, -1, SUBLANES - 2:, :F]
        new_buf = jnp.concatenate([pick(tail_g), pick(tail_v)], axis=-1)
    return out, new_buf


def _gla_prompt(p0, alr, w2p, b_alpha, g_norm, *, B, L, H, DK, DV, tb):
    NT = L // tb
    NC = tb // CHUNK
    C = CHUNK
    scale = DK ** -0.5
    qk_blocks = H
    v_off = 2 * H * DK // DV

    def body(q_ref, k_ref, v_ref, r_ref, a_ref, w2_ref, ba_ref, gn_ref, y_ref, s_out_ref, s_scr):
        t = pl.program_id(2)

        @pl.when(t == 0)
        def _():
            s_scr[...] = jnp.zeros_like(s_scr)

        z = _dot(a_ref[...].astype(bf16), w2_ref[...].astype(bf16)) + ba_ref[...]
        gl = jax.nn.log_sigmoid(z) * (1.0 / GLA_TAU)
        bc3 = _chunk_cumsum(gl, C).reshape(NC, C, DK)
        bl3 = bc3[:, C - 1:C, :]
        q3 = (q_ref[...].astype(f32) * scale).reshape(NC, C, DK)
        k3 = k_ref[...].astype(f32).reshape(NC, C, DK)
        v3 = v_ref[...].astype(bf16).reshape(NC, C, DV)
        qd3 = (q3 * jnp.exp(bc3)).astype(bf16)
        kd3 = (k3 * jnp.exp(-bc3)).astype(bf16)
        kdec3 = (k3 * jnp.exp(bl3 - bc3)).astype(bf16)
        causal = (lax.broadcasted_iota(jnp.int32, (1, C, C), 1) >= lax.broadcasted_iota(jnp.int32, (1, C, C), 2))
        att = jnp.where(causal, jnp.einsum('cik,cjk->cij', qd3, kd3, preferred_element_type=f32), 0.0)
        intra = jnp.einsum('cij,cjv->civ', att.astype(bf16), v3, preferred_element_type=f32)
        ds = jnp.einsum('cjk,cjv->ckv', kdec3, v3, preferred_element_type=f32)
        s = s_scr[...]
        s_in = []
        for c in range(NC):
            s_in.append(s)
            s = _col_from_row(jnp.exp(bl3[c])) * s + ds[c]
        s_scr[...] = s
        s_all = jnp.stack(s_in).astype(bf16)
        o = (intra + jnp.einsum('cik,ckv->civ', qd3, s_all, preferred_element_type=f32)).reshape(tb, DV)
        rr = r_ref[...].astype(f32)
        y_ref[...] = (_rms(o, gn_ref[...]) * (rr * jax.nn.sigmoid(rr))).astype(bf16)

        @pl.when(t == NT - 1)
        def _():
            s_out_ref[0, 0] = s
    return pl.pallas_call(
        body,
        grid=(B, H, NT),
        in_specs=[pl.BlockSpec((tb, DK), lambda b, h, t: (b * NT + t, h)),
                  pl.BlockSpec((tb, DK), lambda b, h, t: (b * NT + t, qk_blocks + h)),
                  pl.BlockSpec((tb, DV), lambda b, h, t: (b * NT + t, v_off + h)),
                  pl.BlockSpec((tb, DV), lambda b, h, t: (b * NT + t, v_off + H + h)),
                  pl.BlockSpec((tb, LANES), lambda b, h, t: (b * NT + t, 0)),
                  pl.BlockSpec((LANES, DK), lambda b, h, t: (0, h)),
                  pl.BlockSpec((1, DK), lambda b, h, t: (0, h)),
                  pl.BlockSpec((1, DV), lambda b, h, t: (0, h))],
        out_specs=[pl.BlockSpec((tb, DV), lambda b, h, t: (b * NT + t, h)),
                   pl.BlockSpec((1, 1, DK, DV), lambda b, h, t: (b, h, 0, 0))],
        out_shape=[jax.ShapeDtypeStruct((B * L, H * DV), bf16),
                   jax.ShapeDtypeStruct((B, H, DK, DV), f32)],
        scratch_shapes=[pltpu.VMEM((DK, DV), f32)],
        compiler_params=_params(("parallel", "parallel", "arbitrary"), 32),
        name="gla_prompt",
    )(p0, p0, p0, p0, alr, w2p, b_alpha.reshape(1, -1), g_norm.reshape(1, -1))


def _gla_sample(p0, alr, w2p, b_alpha, g_norm, s0, *, H, DK, DV):
    Bs = p0.shape[0]
    scale = DK ** -0.5
    qkw = H * DK
    vw = H * DV
    assert vw % qkw == 0
    SB = SUBLANES

    def body(q_ref, k_ref, v_ref, r_ref, a_ref, w2_ref, ba_ref, gn_ref, s_ref, y_ref, so_ref, gl_scr):
        z = _dot(a_ref[...].astype(bf16), w2_ref[...].astype(bf16)) + ba_ref[...]
        gl_scr[...] = jax.nn.log_sigmoid(z) * (1.0 / GLA_TAU)

        for h in range(H):
            ks = slice(h * DK, (h + 1) * DK)
            vs = slice(h * DV, (h + 1) * DV)
            a_t = jnp.exp(gl_scr[:, ks]).T
            k_t = k_ref[:, ks].T
            q_t = (q_ref[:, ks] * scale).T
            v = v_ref[:, vs]
            outs = []
            for s in range(SB):
                sn = a_t[:, s:s + 1] * s_ref[s, h] + k_t[:, s:s + 1] * v[s:s + 1, :]
                so_ref[s, h] = sn
                outs.append(jnp.sum(q_t[:, s:s + 1] * sn, axis=0, keepdims=True))
            o = jnp.concatenate(outs, axis=0)
            rr = r_ref[:, vs]
            y_ref[:, vs] = _rms(o, gn_ref[:, vs]) * (rr * jax.nn.sigmoid(rr))

    v_blk = 2 * qkw // vw
    return pl.pallas_call(
        body,
        grid=(Bs // SB,),
        in_specs=[pl.BlockSpec((SB, qkw), lambda b: (b, 0)),
                  pl.BlockSpec((SB, qkw), lambda b: (b, 1)),
                  pl.BlockSpec((SB, vw), lambda b: (b, v_blk)),
                  pl.BlockSpec((SB, vw), lambda b: (b, v_blk + 1)),
                  pl.BlockSpec((SB, LANES), lambda b: (b, 0)),
                  pl.BlockSpec((LANES, qkw), lambda b: (0, 0)),
                  pl.BlockSpec((1, qkw), lambda b: (0, 0)),
                  pl.BlockSpec((1, vw), lambda b: (0, 0)),
                  pl.BlockSpec((SB, H, DK, DV), lambda b: (b, 0, 0, 0))],
        out_specs=[pl.BlockSpec((SB, vw), lambda b: (b, 0)),
                   pl.BlockSpec((SB, H, DK, DV), lambda b: (b, 0, 0, 0))],
        out_shape=[jax.ShapeDtypeStruct((Bs, vw), f32),
                   jax.ShapeDtypeStruct((Bs, H, DK, DV), f32)],
        scratch_shapes=[pltpu.VMEM((SB, qkw), f32)],
        compiler_params=_params(("parallel",), 32),
        name="gla_sample",
    )(p0, p0, p0, p0, alr, w2p, b_alpha.reshape(1, -1), g_norm.reshape(1, -1), s0)


def _rg_gates(xc, wr, br, wi, bi, sp):
    xb = xc.astype(bf16)
    r = jax.nn.sigmoid(_dot(xb, wr) + br)
    i = jax.nn.sigmoid(_dot(xb, wi) + bi)
    log_a = -RG_C * r * sp
    a = jnp.exp(log_a)
    mult = jnp.sqrt(1.0 - a * a)
    return a, mult, i


def _rglru_prompt(p0, conv_w, conv_b, w_r, b_r, w_i, b_i, lam, *, B, L, x_blk, g_blk):
    M = B * L
    NB, BS, _ = w_r.shape
    assert BS == LANES
    rc = min(256, L)

    def body(x_ref, gg_ref, cw_ref, cb_ref, wr_ref, br_ref, wi_ref, bi_ref, lam_ref, y_ref, hl_ref, a_scr, b_scr, xs_scr):
        wr = wr_ref[...].astype(bf16)
        wi = wi_ref[...].astype(bf16)
        sp = jax.nn.softplus(-lam_ref[...])
        cw = cw_ref[...]

        def chunk(c, carry):
            r0 = pl.multiple_of(c * rc, rc)
            rows = pl.ds(r0, rc)
            x = x_ref[rows, :].astype(f32)
            start = (r0 % L) == 0
            prev = x_ref[pl.ds(pl.multiple_of(jnp.maximum(r0 - HALO, 0), HALO), HALO), :].astype(f32)[HALO - SUBLANES:]
            xs_scr[0:SUBLANES, :] = jnp.where(start, 0.0, prev)
            xs_scr[SUBLANES:, :] = x
            xc = (cb_ref[...] + cw[0:1] * xs_scr[SUBLANES - 3:SUBLANES - 3 + rc, :]
                  + cw[1:2] * xs_scr[SUBLANES - 2:SUBLANES - 2 + rc, :]
                  + cw[2:3] * xs_scr[SUBLANES - 1:SUBLANES - 1 + rc, :] + cw[3:4] * x)
            a, mult, ig = _rg_gates(xc, wr, br_ref[...], wi, bi_ref[...], sp)
            first = jnp.logical_and(start, lax.broadcasted_iota(jnp.int32, (rc, 1), 0) == 0)
            mult = jnp.where(first, 1.0, mult)
            a_scr[rows, :] = a
            b_scr[rows, :] = mult * (ig * xc)
            return carry
        lax.fori_loop(0, M // rc, chunk, 0, unroll=2)

        def step(t, hs):
            new = []
            for b in range(B):
                row = pl.ds(b * L + t, 1)
                h = a_scr[row, :] * hs[b] + b_scr[row, :]
                b_scr[row, :] = h
                new.append(h)
            return tuple(new)
        hs = lax.fori_loop(0, L, step, tuple(jnp.zeros((1, LANES), f32) for _ in range(B)), unroll=32)
        hl_ref[...] = jnp.concatenate(hs, axis=0)

        def outc(c, carry):
            rows = pl.ds(pl.multiple_of(c * rc, rc), rc)
            y_ref[rows, :] = (b_scr[rows, :] * jax.nn.gelu(gg_ref[rows, :].astype(f32))).astype(bf16)
            return carry
        lax.fori_loop(0, M // rc, outc, 0)

    W = NB * BS
    return pl.pallas_call(
        body,
        grid=(NB,),
        in_specs=[pl.BlockSpec((M, LANES), lambda n: (0, x_blk + n)),
                  pl.BlockSpec((M, LANES), lambda n: (0, g_blk + n)),
                  pl.BlockSpec((4, LANES), lambda n: (0, n)),
                  pl.BlockSpec((1, LANES), lambda n: (0, n)),
                  pl.BlockSpec((None, BS, BS), lambda n: (n, 0, 0)),
                  pl.BlockSpec((1, LANES), lambda n: (0, n)),
                  pl.BlockSpec((None, BS, BS), lambda n: (n, 0, 0)),
                  pl.BlockSpec((1, LANES), lambda n: (0, n)),
                  pl.BlockSpec((1, LANES), lambda n: (0, n))],
        out_specs=[pl.BlockSpec((M, LANES), lambda n: (0, n)),
                   pl.BlockSpec((B, LANES), lambda n: (0, n))],
        out_shape=[jax.ShapeDtypeStruct((M, W), bf16), jax.ShapeDtypeStruct((B, W), f32)],
        scratch_shapes=[pltpu.VMEM((M, LANES), f32), pltpu.VMEM((M, LANES), f32), pltpu.VMEM((rc + SUBLANES, LANES), f32)],
        compiler_params=_params(("parallel",), 48),
        name="rglru_prompt",
    )(p0, p0, conv_w, conv_b.reshape(1, W), w_r, b_r.reshape(1, W), w_i, b_i.reshape(1, W), lam.reshape(1, W))


def _rglru_sample(p0, conv_state, h0, conv_w, conv_b, w_r, b_r, w_i, b_i, lam, *, x_blk, g_blk):
    Bs = p0.shape[0]
    NB, BS, _ = w_r.shape
    W = NB * BS
    s0, s1, s2 = conv_state[:, 0], conv_state[:, 1], conv_state[:, 2]

    def body(x_ref, gg_ref, s0_ref, s1_ref, s2_ref, h0_ref, cw_ref, cb_ref, wr_ref, br_ref, wi_ref, bi_ref, lam_ref,
             y_ref, h_ref):
        cw = cw_ref[...]
        x = x_ref[...]
        xc = cb_ref[...] + cw[0:1] * s0_ref[...] + cw[1:2] * s1_ref[...] + cw[2:3] * s2_ref[...] + cw[3:4] * x
        sp = jax.nn.softplus(-lam_ref[...])
        a, mult, ig = _rg_gates(xc, wr_ref[...].astype(bf16), br_ref[...], wi_ref[...].astype(bf16), bi_ref[...], sp)
        if PAST_LEN == 0:
            mult = jnp.ones_like(mult)
        h = a * h0_ref[...] + mult * (ig * xc)
        h_ref[...] = h
        y_ref[...] = (h * jax.nn.gelu(gg_ref[...])).astype(bf16)

    blk = lambda n: (0, n)
    vec = pl.BlockSpec((1, LANES), blk)
    mat = pl.BlockSpec((Bs, LANES), blk)
    y, h = pl.pallas_call(
        body,
        grid=(NB,),
        in_specs=[pl.BlockSpec((Bs, LANES), lambda n: (0, x_blk + n)),
                  pl.BlockSpec((Bs, LANES), lambda n: (0, g_blk + n)),
                  mat, mat, mat, mat,
                  pl.BlockSpec((4, LANES), blk), vec,
                  pl.BlockSpec((None, BS, BS), lambda n: (n, 0, 0)), vec,
                  pl.BlockSpec((None, BS, BS), lambda n: (n, 0, 0)), vec, vec],
        out_specs=[mat, mat],
        out_shape=[jax.ShapeDtypeStruct((Bs, W), bf16), jax.ShapeDtypeStruct((Bs, W), f32)],
        compiler_params=_params(("parallel",), 32),
        name="rglru_sample",
    )(p0, p0, s0, s1, s2, h0, conv_w, conv_b.reshape(1, W), w_r, b_r.reshape(1, W), w_i, b_i.reshape(1, W),
      lam.reshape(1, W))
    return y, h


def _blockdiag_tiles(w, tile):
    nblk, bs, _ = w.shape
    per = tile // bs
    nt = nblk // per
    rows = jnp.tile(w.reshape(nt, tile, bs), (1, 1, per))
    on_diag = (jnp.arange(tile)[:, None] // bs) == (jnp.arange(tile)[None, :] // bs)
    return jnp.where(on_diag[None], rows, 0.0)


def _mlstm_stage1(p1, conv_w, conv_b, wq_t, wk_t, wv_t, wg, bg, *, seq_len, tr, H, DH, conv_state=None):
    M = p1.shape[0]
    W = H * DH
    NTL, TL, _ = wq_t.shape
    sample = conv_state is not None
    tiles_per_seq = max(seq_len // tr, 1)
    kscale = DH ** -0.5

    def body(*refs):
        it = iter(refs)
        x_ref = next(it)
        if sample:
            s0_ref, s1_ref, s2_ref = next(it), next(it), next(it)
        else:
            xh_ref = next(it)
        cw_ref, cb_ref, wq_ref, wk_ref, wv_ref, wg_ref, bg_ref = (next(it) for _ in range(7))
        q_ref, k_ref, v_ref, g_ref, xc_ref = (next(it) for _ in range(5))
        cw = cw_ref[...]
        x = x_ref[...].astype(f32)
        if sample:
            conv = cb_ref[...] + cw[0:1] * s0_ref[...] + cw[1:2] * s1_ref[...] + cw[2:3] * s2_ref[...] + cw[3:4] * x
        else:
            xs_scr = next(it)
            keep = (pl.program_id(0) % tiles_per_seq != 0).astype(f32)
            xs_scr[0:SUBLANES, :] = xh_ref[...].astype(f32)[HALO - SUBLANES:] * keep
            xs_scr[SUBLANES:, :] = x
            conv = (cb_ref[...] + cw[0:1] * xs_scr[SUBLANES - 3:SUBLANES - 3 + tr, :]
                    + cw[1:2] * xs_scr[SUBLANES - 2:SUBLANES - 2 + tr, :]
                    + cw[2:3] * xs_scr[SUBLANES - 1:SUBLANES - 1 + tr, :] + cw[3:4] * x)
        xc = conv * jax.nn.sigmoid(conv)
        xc_ref[...] = xc.astype(xc_ref.dtype)
        xcb = xc.astype(bf16)
        xb = x.astype(bf16)
        qs, ks, vs = [], [], []
        for t in range(NTL):
            cs = slice(t * TL, (t + 1) * TL)
            qs.append(_dot(xcb[:, cs], wq_ref[t]))
            ks.append(_dot(xcb[:, cs], wk_ref[t]) * kscale)
            vs.append(_dot(xb[:, cs], wv_ref[t]))
        q = jnp.concatenate(qs, axis=1)
        k = jnp.concatenate(ks, axis=1)
        v = jnp.concatenate(vs, axis=1)
        q_ref[...] = q.astype(q_ref.dtype)
        k_ref[...] = k.astype(k_ref.dtype)
        v_ref[...] = v.astype(v_ref.dtype)
        gt = (_dot(q.astype(bf16), wg_ref[0:W, :]) + _dot(k.astype(bf16), wg_ref[W:2 * W, :])
              + _dot(v.astype(bf16), wg_ref[2 * W:3 * W, :]) + bg_ref[...])
        lane = lax.broadcasted_iota(jnp.int32, gt.shape, 1)
        g_ref[...] = jnp.where(jnp.logical_and(lane >= H, lane < 2 * H), jax.nn.log_sigmoid(gt), gt)

    row = lambda i: (i, 0)
    const2 = lambda i: (0, 0)
    const3 = lambda i: (0, 0, 0)
    in_specs = [pl.BlockSpec((tr, W), row)]
    args = [p1]
    if sample:
        in_specs += [pl.BlockSpec((tr, W), row)] * 3
        args += [conv_state[:, 0], conv_state[:, 1], conv_state[:, 2]]
    else:
        in_specs.append(pl.BlockSpec((HALO, W), lambda i: (jnp.maximum(i * (tr // HALO) - 1, 0), 0)))
        args.append(p1)
    in_specs += [pl.BlockSpec((4, W), const2), pl.BlockSpec((1, W), const2),
                 pl.BlockSpec((NTL, TL, TL), const3), pl.BlockSpec((NTL, TL, TL), const3),
                 pl.BlockSpec((NTL, TL, TL), const3),
                 pl.BlockSpec((3 * W, LANES), const2), pl.BlockSpec((1, LANES), const2)]
    args += [conv_w, conv_b.reshape(1, W), wq_t, wk_t, wv_t, wg, bg]
    return pl.pallas_call(
        body,
        grid=(M // tr,),
        in_specs=in_specs,
        out_specs=[pl.BlockSpec((tr, W), row)] * 3 + [pl.BlockSpec((tr, LANES), row), pl.BlockSpec((tr, W), row)],
        out_shape=[jax.ShapeDtypeStruct((M, W), f32 if sample else bf16)] * 3
        + [jax.ShapeDtypeStruct((M, LANES), f32), jax.ShapeDtypeStruct((M, W), f32 if sample else bf16)],
        scratch_shapes=[] if sample else [pltpu.VMEM((tr + SUBLANES, W), f32)],
        compiler_params=_params(("parallel",), 48),
        name="mlstm_stage1_sample" if sample else "mlstm_stage1",
    )(*args)


def _mlstm_prompt(q, k, v, gates, xc, p1, g_norm, skip, *, B, L, H, DH, tb):
    NT = L // tb
    NC = tb // CHUNK
    C = CHUNK

    def body(q_ref, k_ref, v_ref, g_ref, xc_ref, om_ref, gn_ref, sk_ref,
             y_ref, c_out, n_out, m_out, c_scr, n_scr, m_scr):
        hh = pl.program_id(1)
        t = pl.program_id(2)

        @pl.when(t == 0)
        def _():
            c_scr[...] = jnp.zeros_like(c_scr)
            n_scr[...] = jnp.zeros_like(n_scr)
            m_scr[...] = jnp.zeros_like(m_scr)

        gts = g_ref[...]
        lane = lax.broadcasted_iota(jnp.int32, (tb, LANES), 1)
        i_col = jnp.sum(jnp.where(lane == hh, gts, 0.0), axis=1, keepdims=True)
        b_col = jnp.sum(jnp.where(lane == H + hh, _chunk_cumsum(gts, C), 0.0), axis=1, keepdims=True)
        b3 = b_col.reshape(NC, C, 1)
        i3 = i_col.reshape(NC, C, 1)
        ii = lax.broadcasted_iota(jnp.int32, (1, C, C), 1)
        jj = lax.broadcasted_iota(jnp.int32, (1, C, C), 2)
        eye = ii == jj
        causal = ii >= jj
        as_row = lambda col3: jnp.sum(jnp.where(eye, jnp.broadcast_to(col3, (NC, C, C)), 0.0), axis=1, keepdims=True)
        dmat = jnp.where(causal, b3 - as_row(b3) + as_row(i3), -jnp.inf)
        rmax = jnp.max(dmat, axis=2, keepdims=True)
        b_last = b3[:, C - 1:C, :]
        m_prev = m_scr[...]
        m_in = []
        for c in range(NC):
            m_in.append(m_prev)
            m_prev = jnp.maximum(b_last[c] + m_prev, rmax[c][C - 1:C, :])
        m_scr[...] = m_prev
        inter = b3 + jnp.stack(m_in)
        m_col = jnp.maximum(inter, rmax)
        g_col = jnp.exp(inter - m_col)
        q3 = q_ref[...].reshape(NC, C, DH)
        k3 = k_ref[...].reshape(NC, C, DH)
        v3 = v_ref[...].reshape(NC, C, DH)
        s = jnp.einsum('cid,cjd->cij', q3, k3, preferred_element_type=f32) * jnp.exp(dmat - m_col)
        num = jnp.einsum('cij,cje->cie', s.astype(bf16), v3, preferred_element_type=f32)
        den = jnp.sum(s, axis=2, keepdims=True)
        m_new = m_col[:, C - 1:C, :]
        wk = jnp.exp(b_last - b3 + i3 - m_new)
        gc = jnp.exp(inter[:, C - 1:C, :] - m_new)
        kw = k3.astype(f32) * wk
        dc = jnp.einsum('cse,csd->ced', v3, kw.astype(bf16), preferred_element_type=f32)
        dn = jnp.sum(kw, axis=1, keepdims=True)
        cs = c_scr[...]
        ns = n_scr[...]
        c_in, n_in = [], []
        for c in range(NC):
            c_in.append(cs)
            n_in.append(ns)
            cs = gc[c] * cs + dc[c]
            ns = gc[c] * ns + dn[c]
        c_scr[...] = cs
        n_scr[...] = ns
        c_all = jnp.stack(c_in).astype(bf16)
        n_all = jnp.stack(n_in)
        num = num + g_col * jnp.einsum('cid,ced->cie', q3, c_all, preferred_element_type=f32)
        den = den + g_col * jnp.sum(q3.astype(f32) * n_all, axis=2, keepdims=True)
        hm = (num / jnp.maximum(jnp.abs(den), jnp.exp(-m_col))).reshape(tb, DH)
        y = (_rms(hm, gn_ref[...]) + sk_ref[...] * xc_ref[...].astype(f32)) * jax.nn.sigmoid(om_ref[...].astype(f32))
        y_ref[...] = y.astype(bf16)

        @pl.when(t == NT - 1)
        def _():
            c_out[0, 0] = cs
            n_out[0, 0] = ns
            m_out[0, 0] = m_prev

    blk = lambda b, h, t: (b * NT + t, h)
    W = H * DH
    y, c_new, n_new, m_new = pl.pallas_call(
        body,
        grid=(B, H, NT),
        in_specs=[pl.BlockSpec((tb, DH), blk), pl.BlockSpec((tb, DH), blk), pl.BlockSpec((tb, DH), blk),
                  pl.BlockSpec((tb, LANES), lambda b, h, t: (b * NT + t, 0)),
                  pl.BlockSpec((tb, DH), blk),
                  pl.BlockSpec((tb, DH), lambda b, h, t: (b * NT + t, H + h)),
                  pl.BlockSpec((1, DH), lambda b, h, t: (0, h)),
                  pl.BlockSpec((1, DH), lambda b, h, t: (0, h))],
        out_specs=[pl.BlockSpec((tb, DH), blk),
                   pl.BlockSpec((1, 1, DH, DH), lambda b, h, t: (b, h, 0, 0)),
                   pl.BlockSpec((1, 1, 1, DH), lambda b, h, t: (b, h, 0, 0)),
                   pl.BlockSpec((1, 1, 1, 1), lambda b, h, t: (b, h, 0, 0))],
        out_shape=[jax.ShapeDtypeStruct((B * L, W), bf16),
                   jax.ShapeDtypeStruct((B, H, DH, DH), f32),
                   jax.ShapeDtypeStruct((B, H, 1, DH), f32),
                   jax.ShapeDtypeStruct((B, H, 1, 1), f32)],
        scratch_shapes=[pltpu.VMEM((DH, DH), f32), pltpu.VMEM((1, DH), f32), pltpu.VMEM((1, 1), f32)],
        compiler_params=_params(("parallel", "parallel", "arbitrary"), 32),
        name="mlstm_prompt",
    )(q, k, v, gates, xc, p1, g_norm.reshape(1, W), skip.reshape(1, W))
    return y, c_new, n_new.reshape(B, H, DH), m_new.reshape(B, H)


def _mlstm_sample(q, k, v, gates, xc, p1, g_norm, skip, c0, n0, m0, *, H, DH):
    Bs = q.shape[0]
    W = H * DH
    SB = SUBLANES

    def body(q_ref, k_ref, v_ref, g_ref, xc_ref, om_ref, gn_ref, sk_ref, c_ref, n_ref, m_ref,
             y_ref, c_out, n_out, m_out):
        gts = g_ref[...]
        for h in range(H):
            cs = slice(h * DH, (h + 1) * DH)
            ig = gts[:, h:h + 1]
            fg = gts[:, H + h:H + h + 1]
            inter = fg + m_ref[:, h:h + 1]
            m = jnp.maximum(inter, ig)
            g = jnp.exp(inter - m)
            m_out[:, h:h + 1] = m
            q = q_ref[:, cs]
            kw = k_ref[:, cs] * jnp.exp(ig - m)
            nn = g * n_ref[:, cs] + kw
            n_out[:, cs] = nn
            den = jnp.sum(nn * q, axis=1, keepdims=True)
            v_t = v_ref[:, cs].T
            qb = q.astype(bf16)
            nums = []
            for s in range(SB):
                cn = g[s:s + 1, :] * c_ref[s, h] + v_t[:, s:s + 1] * kw[s:s + 1, :]
                c_out[s, h] = cn
                nums.append(_dot_nt(qb, cn.astype(bf16))[s:s + 1, :])
            num = jnp.concatenate(nums, axis=0)
            hm = num / jnp.maximum(jnp.abs(den), jnp.exp(-m))
            y_ref[:, cs] = ((_rms(hm, gn_ref[:, cs]) + sk_ref[:, cs] * xc_ref[:, cs])
                            * jax.nn.sigmoid(om_ref[:, cs]))

    per = lambda b: (b, 0)
    const2 = lambda b: (0, 0)
    y, c_new, n_new, m_new = pl.pallas_call(
        body,
        grid=(Bs // SB,),
        in_specs=[pl.BlockSpec((SB, W), per), pl.BlockSpec((SB, W), per), pl.BlockSpec((SB, W), per),
                  pl.BlockSpec((SB, LANES), per), pl.BlockSpec((SB, W), per),
                  pl.BlockSpec((SB, W), lambda b: (b, 1)),
                  pl.BlockSpec((1, W), const2), pl.BlockSpec((1, W), const2),
                  pl.BlockSpec((SB, H, DH, DH), lambda b: (b, 0, 0, 0)),
                  pl.BlockSpec((SB, W), per), pl.BlockSpec((SB, H), per)],
        out_specs=[pl.BlockSpec((SB, W), per),
                   pl.BlockSpec((SB, H, DH, DH), lambda b: (b, 0, 0, 0)),
                   pl.BlockSpec((SB, W), per), pl.BlockSpec((SB, H), per)],
        out_shape=[jax.ShapeDtypeStruct((Bs, W), f32), jax.ShapeDtypeStruct((Bs, H, DH, DH), f32),
                   jax.ShapeDtypeStruct((Bs, W), f32), jax.ShapeDtypeStruct((Bs, H), f32)],
        compiler_params=_params(("parallel",), 48),
        name="mlstm_sample",
    )(q, k, v, gates, xc, p1, g_norm.reshape(1, W), skip.reshape(1, W), c0, n0.reshape(Bs, W), m0)
    return y, c_new, n_new.reshape(Bs, H, DH), m_new


S5_TILES = 8


def _s5_layouts(lam_re, lam_im, log_dt, b_re, b_im, c_re, c_im):
    G, P = lam_re.shape
    GC = b_re.shape[2]
    T = S5_TILES
    gpt = G // T
    ns = G * P
    flat = lambda a: a.reshape(ns)
    ldt = jnp.broadcast_to(log_dt[:, None], (G, P))
    rows = [flat(a).reshape(T, 1, ns // T) for a in (lam_re, lam_im, ldt)]
    eye = jnp.eye(gpt, dtype=f32)
    bbd = [jnp.einsum('jgpc,gh->jgchp', a.reshape(T, gpt, P, GC), eye).reshape(T, gpt * GC, gpt * P) for a in (b_re, b_im)]
    cbd = [jnp.einsum('jgcp,gh->jgphc', a.reshape(T, gpt, GC, P), eye).reshape(T, gpt * P, gpt * GC) for a in (c_re, c_im)]
    return rows, bbd, cbd


def _s5_discretise(lre, lim, ldt):
    dt = jnp.exp(ldt)
    mag = jnp.exp(dt * lre)
    ar = mag * jnp.cos(dt * lim)
    ai = mag * jnp.sin(dt * lim)
    den = lre * lre + lim * lim
    cr = ((ar - 1.0) * lre + ai * lim) / den
    ci = (ai * lre - (ar - 1.0) * lim) / den
    return ar, ai, cr, ci


def _s5_mixer(p1, u_blk, rows, bbd, cbd, d_skip, w_glu, b_glu, *, B, L, tb, state=None):
    T = S5_TILES
    lre_r, lim_r, ldt_r = rows
    SW = lre_r.shape[2]
    CW = bbd[0].shape[1]
    W = T * CW
    NS = T * SW
    KT = SW // LANES
    sample = state is not None
    NT = 1 if sample else L // tb
    M = B * L

    def body(*refs):
        it = iter(refs)
        u_ref = next(it)
        if sample:
            x0r_ref, x0i_ref = next(it), next(it)
        lre_ref, lim_ref, ldt_ref = next(it), next(it), next(it)
        bre_ref, bim_ref, cre_ref, cim_ref = next(it), next(it), next(it), next(it)
        d_ref, wg_ref, bgl_ref = next(it), next(it), next(it)
        y_ref, xr_out, xi_out = next(it), next(it), next(it)
        bbr, bbi, cpair, ar_scr, ai_scr = (next(it) for _ in range(5))
        if not sample:
            sre, sim, xr_c, xi_c, yacc = (next(it) for _ in range(5))
        first = jnp.logical_and(pl.program_id(0) == 0, pl.program_id(1) == 0)

        @pl.when(first)
        def _():
            for j in range(T):
                ar, ai, cr, ci = _s5_discretise(lre_ref[j], lim_ref[j], ldt_ref[j])
                ar_scr[j] = ar
                ai_scr[j] = ai
                br = bre_ref[j]
                bi = bim_ref[j]
                bbr[j] = (cr * br - ci * bi).astype(bf16)
                bbi[j] = (cr * bi + ci * br).astype(bf16)
            for jp in range(T // 2):
                cpair[jp] = jnp.zeros((4 * SW, 2 * CW), bf16)
                for half in range(2):
                    j = 2 * jp + half
                    r0 = 2 * half * SW
                    cols = slice(half * CW, (half + 1) * CW)
                    cpair[jp, r0:r0 + SW, cols] = cre_ref[j].astype(bf16)
                    cpair[jp, r0 + SW:r0 + 2 * SW, cols] = (-cim_ref[j]).astype(bf16)

        u = u_ref[...].astype(f32)
        ub = u_ref[...].astype(bf16)
        ys = []
        if sample:
            for j in range(T):
                cs = slice(j * SW, (j + 1) * SW)
                uj = ub[:, j * CW:(j + 1) * CW]
                ar = ar_scr[j]
                ai = ai_scr[j]
                x0r = x0r_ref[:, cs]
                x0i = x0i_ref[:, cs]
                xr = ar * x0r - ai * x0i + _dot(uj, bbr[j])
                xi = ar * x0i + ai * x0r + _dot(uj, bbi[j])
                xr_out[:, cs] = xr
                xi_out[:, cs] = xi
                ys += [xr.astype(bf16), xi.astype(bf16)]
            y = jnp.concatenate([_dot(jnp.concatenate(ys[4 * jp:4 * jp + 4], axis=1), cpair[jp]) for jp in range(T // 2)],
                                axis=1)
        else:
            t = pl.program_id(1)

            @pl.when(t == 0)
            def _():
                xr_c[...] = jnp.zeros_like(xr_c)
                xi_c[...] = jnp.zeros_like(xi_c)

            for j in range(T):
                uj = ub[:, j * CW:(j + 1) * CW]
                r = _dot(uj, bbr[j])
                im = _dot(uj, bbi[j])
                for kk in range(KT):
                    sre[kk, pl.ds(j, tb, stride=T), :] = r[:, kk * LANES:(kk + 1) * LANES]
                    sim[kk, pl.ds(j, tb, stride=T), :] = im[:, kk * LANES:(kk + 1) * LANES]
            a_r = [jnp.concatenate([ar_scr[j][:, kk * LANES:(kk + 1) * LANES] for j in range(T)], axis=0) for kk in range(KT)]
            a_i = [jnp.concatenate([ai_scr[j][:, kk * LANES:(kk + 1) * LANES] for j in range(T)], axis=0) for kk in range(KT)]

            def step(s, carry):
                xr, xi = carry
                row = pl.ds(pl.multiple_of(s * T, T), T)
                nr, ni = [], []
                for kk in range(KT):
                    r_ = a_r[kk] * xr[kk] - a_i[kk] * xi[kk] + sre[kk, row, :]
                    i_ = a_r[kk] * xi[kk] + a_i[kk] * xr[kk] + sim[kk, row, :]
                    sre[kk, row, :] = r_
                    sim[kk, row, :] = i_
                    nr.append(r_)
                    ni.append(i_)
                return tuple(nr), tuple(ni)
            xr0 = tuple(xr_c[kk] for kk in range(KT))
            xi0 = tuple(xi_c[kk] for kk in range(KT))
            xr, xi = lax.fori_loop(0, tb, step, (xr0, xi0), unroll=4)
            for kk in range(KT):
                xr_c[kk] = xr[kk]
                xi_c[kk] = xi[kk]

            @pl.when(t == NT - 1)
            def _():
                for kk in range(KT):
                    xr_out[kk] = xr[kk]
                    xi_out[kk] = xi[kk]

            for jp in range(T // 2):
                parts = []
                for j in (2 * jp, 2 * jp + 1):
                    parts += [sre[kk, pl.ds(j, tb, stride=T), :] for kk in range(KT)]
                    parts += [sim[kk, pl.ds(j, tb, stride=T), :] for kk in range(KT)]
                xp = jnp.concatenate(parts, axis=1).astype(bf16)
                yacc[:, 2 * jp * CW:(2 * jp + 2) * CW] = _dot(xp, cpair[jp])
            y = yacc[...]
        ysk = jax.nn.gelu(y + d_ref[...] * u)
        z = _dot(ysk.astype(bf16), wg_ref[...]) + bgl_ref[...]
        y_ref[...] = (ysk * jax.nn.sigmoid(z)).astype(bf16)

    c3 = lambda b, t: (0, 0, 0)
    c2 = lambda b, t: (0, 0)
    in_specs = [pl.BlockSpec((tb, W), lambda b, t: (b * NT + t, u_blk))]
    args = [p1]
    if sample:
        in_specs += [pl.BlockSpec((tb, NS), lambda b, t: (b, 0))] * 2
        args += [state[0], state[1]]
    in_specs += [pl.BlockSpec((T, 1, SW), c3)] * 3
    in_specs += [pl.BlockSpec((T, CW, SW), c3)] * 2 + [pl.BlockSpec((T, SW, CW), c3)] * 2
    in_specs += [pl.BlockSpec((1, W), c2), pl.BlockSpec((W, W), c2), pl.BlockSpec((1, W), c2)]
    args += [lre_r, lim_r, ldt_r, bbd[0], bbd[1], cbd[0], cbd[1], d_skip.reshape(1, W), w_glu, b_glu.reshape(1, W)]
    scratch = [pltpu.VMEM((T, CW, SW), bf16), pltpu.VMEM((T, CW, SW), bf16),
               pltpu.VMEM((T // 2, 4 * SW, 2 * CW), bf16),
               pltpu.VMEM((T, 1, SW), f32), pltpu.VMEM((T, 1, SW), f32)]
    if sample:
        grid = (M // tb, 1)
        st_spec = pl.BlockSpec((tb, NS), lambda b, t: (b, 0))
        st_shape = jax.ShapeDtypeStruct((M, NS), f32)
    else:
        grid = (B, NT)
        st_spec = pl.BlockSpec((None, KT, T, LANES), lambda b, t: (b, 0, 0, 0))
        st_shape = jax.ShapeDtypeStruct((B, KT, T, LANES), f32)
        scratch += [pltpu.VMEM((KT, tb * T, LANES), f32), pltpu.VMEM((KT, tb * T, LANES), f32),
                    pltpu.VMEM((KT, T, LANES), f32), pltpu.VMEM((KT, T, LANES), f32), pltpu.VMEM((tb, W), f32)]
    y, xr, xi = pl.pallas_call(
        body,
        grid=grid,
        in_specs=in_specs,
        out_specs=[pl.BlockSpec((tb, W), lambda b, t: (b * NT + t, 0)), st_spec, st_spec],
        out_shape=[jax.ShapeDtypeStruct((M, W), bf16), st_shape, st_shape],
        scratch_shapes=scratch,
        compiler_params=_params(("arbitrary", "arbitrary"), 56),
        name="s5_sample" if sample else "s5_prompt",
    )(*args)
    if not sample:
        xr = jnp.transpose(xr, (0, 2, 1, 3)).reshape(B, NS)
        xi = jnp.transpose(xi, (0, 2, 1, 3)).reshape(B, NS)
    return y, xr, xi


def _trunk(x3, st, w, *, sample):
    B, L, D = x3.shape
    M = B * L
    x = x3.reshape(M, D)
    H_g, DK, DV = st['gla_S'].shape[1:] if sample else w['gla_dims']
    H_m, DH = w['ml_dims']
    G, P = w['s5_dims']
    tm = M if sample else min(1024, L)
    tf = M if sample else min(1024, L)
    to = M if sample else min(2048, M)
    tb = min(2048, L)
    ts = min(512, L)
    tr = M if sample else min(256, L)
    tn_in, tn_out = 1024, 512
    pdt = f32 if sample else bf16
    out = {}

    p0, alr = _norm_matmul(x, w['g_mix0'], w['w_in0_main'], tm=tm, tn=tn_in, w_side=w['w_in0_alr'], out_dtype=pdt, w_t=True,
                            seg_cols=w['w_in0_cols'])
    x_blk = (2 * H_g * DK + 2 * H_g * DV) // LANES
    W_rg = w['rg_lambda'].shape[0]
    g_blk = x_blk + W_rg // LANES
    rg_tail = p0.reshape(B, L, p0.shape[1])[:, max(L - 3, 0):, x_blk * LANES:x_blk * LANES + W_rg].astype(f32)
    if sample:
        ya, out['gla_S'] = _gla_sample(p0, alr, w['gla_w2p'], w['gla_b_alpha'], w['gla_g_norm'], st['gla_S'],
                                       H=H_g, DK=DK, DV=DV)
        yb, out['rg_h'] = _rglru_sample(p0, st['rg_conv'], st['rg_h'], w['rg_conv_w'], w['rg_conv_b'], w['rg_w_r'],
                                        w['rg_b_r'], w['rg_w_i'], w['rg_b_i'], w['rg_lambda'], x_blk=x_blk, g_blk=g_blk)
        out['rg_conv'] = jnp.concatenate([st['rg_conv'][:, 1:], rg_tail], axis=1)
    else:
        ya, out['gla_S'] = _gla_prompt(p0, alr, w['gla_w2p'], w['gla_b_alpha'], w['gla_g_norm'],
                                       B=B, L=L, H=H_g, DK=DK, DV=DV, tb=tb)
        yb, out['rg_h'] = _rglru_prompt(p0, w['rg_conv_w'], w['rg_conv_b'], w['rg_w_r'], w['rg_b_r'], w['rg_w_i'],
                                        w['rg_b_i'], w['rg_lambda'], B=B, L=L, x_blk=x_blk, g_blk=g_blk)
        out['rg_conv'] = rg_tail
    x = _out_proj(ya, yb, w['w_out0'], x, tm=to, tn=tn_out)
    x, ffn0 = _conv_ffn(x, w['g_ffn'][0], w['ffn'][0], seq_len=L, tm=tf, prev=st['ffn_conv'] if sample else None)

    W_ml = H_m * DH
    p1 = _norm_matmul(x, w['g_mix1'], [w['w_in1']], tm=tm, tn=tn_in, out_dtype=pdt)
    xm_tail = p1.reshape(B, L, p1.shape[1])[:, max(L - 3, 0):, :W_ml].astype(f32)
    q, k, v, gates, xc = _mlstm_stage1(p1, w['ml_conv_w'], w['ml_conv_b'], w['ml_wq_t'], w['ml_wk_t'], w['ml_wv_t'],
                                       w['ml_wg'], w['ml_bg'], seq_len=L, tr=tr, H=H_m, DH=DH,
                                       conv_state=st['ml_conv'] if sample else None)
    u_blk = 2 * W_ml // (G * w['s5_gc'])
    if sample:
        yc, out['ml_C'], out['ml_n'], out['ml_m'] = _mlstm_sample(
            q, k, v, gates, xc, p1, w['ml_g_norm'], w['ml_skip'], st['ml_C'], st['ml_n'], st['ml_m'], H=H_m, DH=DH)
        out['ml_conv'] = jnp.concatenate([st['ml_conv'][:, 1:], xm_tail], axis=1)
        yd, s5r, s5i = _s5_mixer(p1, u_blk, w['s5_rows'], w['s5_bbd'], w['s5_cbd'], w['s5_D'], w['s5_w_glu'],
                                 w['s5_b_glu'], B=B, L=1, tb=B,
                                 state=(st['s5_re'].reshape(B, G * P), st['s5_im'].reshape(B, G * P)))
    else:
        yc, out['ml_C'], out['ml_n'], out['ml_m'] = _mlstm_prompt(
            q, k, v, gates, xc, p1, w['ml_g_norm'], w['ml_skip'], B=B, L=L, H=H_m, DH=DH, tb=tb)
        out['ml_conv'] = xm_tail
        yd, s5r, s5i = _s5_mixer(p1, u_blk, w['s5_rows'], w['s5_bbd'], w['s5_cbd'], w['s5_D'], w['s5_w_glu'],
                                 w['s5_b_glu'], B=B, L=L, tb=ts)
    out['s5_re'] = s5r.reshape(B, G, P)
    out['s5_im'] = s5i.reshape(B, G, P)
    x = _out_proj(yc, yd, w['w_out1'], x, tm=to, tn=tn_out)
    x, ffn1 = _conv_ffn(x, w['g_ffn'][1], w['ffn'][1], seq_len=L, tm=tf, prev=st['ffn_conv'] if sample else None,
                        final_g=w['g_final'])
    out['ffn_conv'] = jnp.stack([ffn0, ffn1], axis=0)
    return x.reshape(B, L, D), out


def kernel(x_prompt, x_sample, state_gla_S, state_rglru_h, state_rglru_conv, state_mlstm_C, state_mlstm_n, state_mlstm_m, state_mlstm_conv, state_s5_re, state_s5_im, state_ffn_conv, g_mix0, w_in0, gla_w_alpha2, gla_b_alpha, gla_g_norm, rg_conv_w, rg_conv_b, rg_w_r, rg_b_r, rg_w_i, rg_b_i, rg_lambda, w_out0, g_mix1, w_in1, ml_conv_w, ml_conv_b, ml_wq, ml_wk, ml_wv, ml_w_igate, ml_b_igate, ml_w_fgate, ml_b_fgate, ml_g_norm, ml_skip, s5_lam_re, s5_lam_im, s5_log_dt, s5_B_re, s5_B_im, s5_C_re, s5_C_im, s5_D, s5_w_glu, s5_b_glu, w_out1, g_ffn, ffn_w_up, ffn_conv_w, ffn_conv_b, ffn_w_down, g_final):
    _, H_g, DK, DV = state_gla_S.shape
    _, H_m, DH, _ = state_mlstm_C.shape
    G, P = s5_lam_re.shape
    rank = gla_w_alpha2.shape[0]
    n_main = 2 * H_g * DK + 2 * H_g * DV
    w_in0_t = w_in0.T.astype(bf16)
    w_in0_main = [w_in0_t, w_in0_t[n_main + rank:]]
    w_in0_cols = [n_main, w_in0.shape[1] - n_main - rank]
    w_in0_alr = jnp.pad(w_in0_t[n_main:n_main + rank], ((0, LANES - rank), (0, 0)))
    gla_w2p = jnp.pad(gla_w_alpha2, ((0, LANES - rank), (0, 0)))
    ml_tile = 256
    ml_wg = jnp.pad(jnp.concatenate([ml_w_igate, ml_w_fgate], axis=1), ((0, 0), (0, LANES - 2 * H_m))).astype(bf16)
    ml_bg = jnp.pad(jnp.concatenate([ml_b_igate, ml_b_fgate]), (0, LANES - 2 * H_m)).reshape(1, LANES)
    rows, bbd, cbd = _s5_layouts(s5_lam_re, s5_lam_im, s5_log_dt, s5_B_re, s5_B_im, s5_C_re, s5_C_im)
    w = dict(
        g_mix0=g_mix0, w_in0_main=w_in0_main, w_in0_cols=w_in0_cols, w_in0_alr=w_in0_alr, gla_w2p=gla_w2p, gla_b_alpha=gla_b_alpha,
        gla_g_norm=gla_g_norm, gla_dims=(H_g, DK, DV), rg_conv_w=rg_conv_w, rg_conv_b=rg_conv_b, rg_w_r=rg_w_r,
        rg_b_r=rg_b_r, rg_w_i=rg_w_i, rg_b_i=rg_b_i, rg_lambda=rg_lambda, w_out0=w_out0.astype(bf16),
        g_mix1=g_mix1, w_in1=w_in1.astype(bf16), ml_conv_w=ml_conv_w, ml_conv_b=ml_conv_b,
        ml_wq_t=_blockdiag_tiles(ml_wq, ml_tile).astype(bf16), ml_wk_t=_blockdiag_tiles(ml_wk, ml_tile).astype(bf16),
        ml_wv_t=_blockdiag_tiles(ml_wv, ml_tile).astype(bf16), ml_wg=ml_wg, ml_bg=ml_bg, ml_g_norm=ml_g_norm,
        ml_skip=ml_skip, ml_dims=(H_m, DH), s5_dims=(G, P), s5_gc=s5_B_re.shape[2], s5_rows=rows, s5_bbd=bbd,
        s5_cbd=cbd, s5_D=s5_D, s5_w_glu=s5_w_glu.astype(bf16), s5_b_glu=s5_b_glu, w_out1=w_out1.astype(bf16),
        g_ffn=g_ffn, g_final=g_final,
        ffn=_ffn_prepare(ffn_w_up, ffn_conv_w, ffn_conv_b, ffn_w_down))
    st_s = dict(gla_S=state_gla_S, rg_h=state_rglru_h, rg_conv=state_rglru_conv, ml_C=state_mlstm_C,
                ml_n=state_mlstm_n, ml_m=state_mlstm_m, ml_conv=state_mlstm_conv, s5_re=state_s5_re,
                s5_im=state_s5_im, ffn_conv=state_ffn_conv)
    y_p, np_ = _trunk(x_prompt, None, w, sample=False)
    y_s, ns_ = _trunk(x_sample, st_s, w, sample=True)
    names = ('gla_S', 'rg_h', 'rg_conv', 'ml_C', 'ml_n', 'ml_m', 'ml_conv', 's5_re', 's5_im', 'ffn_conv')
    outs = [y_p, y_s]
    for nme in names:
        outs += [np_[nme], ns_[nme]]
    return tuple(outs)
```

```python
import jax
import jax.numpy as jnp
from jax import lax
from jax.experimental import pallas as pl
from jax.experimental.pallas import tpu as pltpu

f32 = jnp.float32
bf16 = jnp.bfloat16

EPS = 1e-6
CHUNK = 64
GLA_TAU = 16.0
RG_C = 8.0
PAST_LEN = 16384
LANES = 128
SUBLANES = 8
HALO = 16
MIB = 1024 * 1024


def _params(sem, vmem_mib):
    return pltpu.CompilerParams(dimension_semantics=sem, vmem_limit_bytes=int(vmem_mib * MIB))


def _dot(a, b):
    return jnp.dot(a, b, preferred_element_type=f32)


def _dot_nt(a, b):
    return lax.dot_general(a, b, (((1,), (1,)), ((), ())), preferred_element_type=f32)


def _dot_tn(a, b):
    return lax.dot_general(a, b, (((0,), (0,)), ((), ())), preferred_element_type=f32)


def _rms(x, g):
    return x * lax.rsqrt(jnp.mean(x * x, axis=-1, keepdims=True) + EPS) * g


def _eye(n):
    return lax.broadcasted_iota(jnp.int32, (n, n), 0) == lax.broadcasted_iota(jnp.int32, (n, n), 1)


def _col_from_row(row):
    n = row.shape[1]
    return jnp.sum(jnp.where(_eye(n), jnp.broadcast_to(row, (n, n)), 0.0), axis=1, keepdims=True)


def _chunk_cumsum(x, chunk):
    pos = lax.broadcasted_iota(jnp.int32, (x.shape[0], 1), 0) % chunk
    step = 1
    while step < chunk:
        x = x + jnp.where(pos >= step, pltpu.roll(x, step, 0), 0.0)
        step *= 2
    return x


def _norm_matmul(x, g, ws, *, tm, tn, w_side=None, out_dtype=f32, w_t=False, seg_cols=None):
    M, D = x.shape
    mm = _dot_nt if w_t else _dot
    seg_cols = seg_cols or [w.shape[0 if w_t else 1] for w in ws]
    nblk = [c // tn for c in seg_cols]
    start = [sum(nblk[:s]) for s in range(len(ws) + 1)]
    N = tn * start[-1]
    rc = min(tm, 256)
    side = w_side is not None

    def body(*refs):
        x_ref, g_ref = refs[:2]
        w_refs = refs[2:2 + len(ws)]
        rest = refs[2 + len(ws):]
        if side:
            ws_ref, o_ref, os_ref, xn_ref = rest
        else:
            o_ref, xn_ref = rest
        j = pl.program_id(1)

        @pl.when(j == 0)
        def _():
            def chunk(r, c):
                rows = pl.ds(pl.multiple_of(r * rc, rc), rc)
                xn_ref[rows, :] = _rms(x_ref[rows, :], g_ref[...]).astype(bf16)
                return c
            lax.fori_loop(0, tm // rc, chunk, 0)
            if side:
                os_ref[...] = mm(xn_ref[...], ws_ref[...])

        if len(ws) == 1:
            o_ref[...] = mm(xn_ref[...], w_refs[0][...]).astype(out_dtype)
        else:
            for s, w_ref in enumerate(w_refs):
                @pl.when(jnp.logical_and(j >= start[s], j < start[s + 1]))
                def _(w_ref=w_ref):
                    o_ref[...] = mm(xn_ref[...], w_ref[...]).astype(out_dtype)

    in_specs = [pl.BlockSpec((tm, D), lambda i, j: (i, 0)),
                pl.BlockSpec((1, D), lambda i, j: (0, 0))]
    wblk = lambda j, s: jnp.clip(j - start[s], 0, nblk[s] - 1)
    if w_t:
        in_specs += [pl.BlockSpec((tn, D), lambda i, j, s=s: (wblk(j, s), 0)) for s in range(len(ws))]
    else:
        in_specs += [pl.BlockSpec((D, tn), lambda i, j, s=s: (0, wblk(j, s))) for s in range(len(ws))]
    out_specs = [pl.BlockSpec((tm, tn), lambda i, j: (i, j))]
    out_shape = [jax.ShapeDtypeStruct((M, N), out_dtype)]
    args = [x, g.reshape(1, D)] + list(ws)
    if side:
        ns = w_side.shape[0 if w_t else 1]
        in_specs.append(pl.BlockSpec((ns, D) if w_t else (D, ns), lambda i, j: (0, 0)))
        out_specs.append(pl.BlockSpec((tm, ns), lambda i, j: (i, 0)))
        out_shape.append(jax.ShapeDtypeStruct((M, ns), f32))
        args.append(w_side)
    outs = pl.pallas_call(
        body,
        grid=(M // tm, N // tn),
        in_specs=in_specs,
        out_specs=out_specs,
        out_shape=out_shape,
        scratch_shapes=[pltpu.VMEM((tm, D), bf16)],
        compiler_params=_params(("parallel", "arbitrary"), 56),
        name="norm_matmul",
    )(*args)
    return outs if side else outs[0]


def _out_proj(ya, yb, w, res, *, tm, tn):
    M, Ka = ya.shape
    Kb = yb.shape[1]
    N = w.shape[1]
    assert Ka == Kb and w.shape[0] == Ka + Kb

    def body(ya_ref, yb_ref, wa_ref, wb_ref, r_ref, o_ref):
        o_ref[...] = (r_ref[...] + _dot(ya_ref[...].astype(bf16), wa_ref[...])
                      + _dot(yb_ref[...].astype(bf16), wb_ref[...]))

    return pl.pallas_call(
        body,
        grid=(M // tm, N // tn),
        in_specs=[pl.BlockSpec((tm, Ka), lambda i, j: (i, 0)),
                  pl.BlockSpec((tm, Kb), lambda i, j: (i, 0)),
                  pl.BlockSpec((Ka, tn), lambda i, j: (0, j)),
                  pl.BlockSpec((Kb, tn), lambda i, j: (1, j)),
                  pl.BlockSpec((tm, tn), lambda i, j: (i, j))],
        out_specs=pl.BlockSpec((tm, tn), lambda i, j: (i, j)),
        out_shape=jax.ShapeDtypeStruct((M, N), f32),
        compiler_params=_params(("parallel", "arbitrary"), 48),
        name="out_proj",
    )(ya, yb, w, w, res)


FFN_STEP = 512
FFN_ROW_CHUNK = 512


def _ffn_to_steps(a, F):
    nsteps = -(-F // FFN_STEP)
    lead = a.shape[:-1]
    nd = len(lead)
    gv = a.reshape(lead + (2, F))
    gv = jnp.pad(gv, [(0, 0)] * (nd + 1) + [(0, nsteps * FFN_STEP - F)])
    gv = gv.reshape(lead + (2, nsteps, FFN_STEP))
    gv = jnp.transpose(gv, (nd + 1,) + tuple(range(nd)) + (nd, nd + 2))
    return gv.reshape((nsteps,) + lead + (2 * FFN_STEP,))


def _ffn_prepare(w_up, conv_w, conv_b, w_down):
    NL, F, D = w_down.shape
    assert F % LANES == 0 and FFN_STEP % LANES == 0
    nt = F // LANES
    tp = FFN_STEP // LANES
    nsteps = -(-F // FFN_STEP)

    def body(*refs):
        g_in, v_in, d_in = refs[:tp], refs[tp:2 * tp], refs[2 * tp:3 * tp]
        wu_o, wd_o = refs[3 * tp:]
        for t in range(tp):
            keep = tp * pl.program_id(1) + t < nt
            cols = slice(t * LANES, (t + 1) * LANES)
            wu_o[:, cols] = jnp.where(keep, g_in[t][...], 0.0).astype(bf16)
            wu_o[:, FFN_STEP + t * LANES:FFN_STEP + (t + 1) * LANES] = jnp.where(keep, v_in[t][...], 0.0).astype(bf16)
            wd_o[cols, :] = jnp.where(keep, d_in[t][...], 0.0).astype(bf16)

    tile = lambda j, t: jnp.minimum(tp * j + t, nt - 1)
    in_specs = ([pl.BlockSpec((None, D, LANES), lambda l, j, t=t: (l, 0, tile(j, t))) for t in range(tp)]
                + [pl.BlockSpec((None, D, LANES), lambda l, j, t=t: (l, 0, nt + tile(j, t))) for t in range(tp)]
                + [pl.BlockSpec((None, LANES, D), lambda l, j, t=t: (l, tile(j, t), 0)) for t in range(tp)])
    wu, wd = pl.pallas_call(
        body,
        grid=(NL, nsteps),
        in_specs=in_specs,
        out_specs=[pl.BlockSpec((None, None, D, 2 * FFN_STEP), lambda l, j: (l, j, 0, 0)),
                   pl.BlockSpec((None, None, FFN_STEP, D), lambda l, j: (l, j, 0, 0))],
        out_shape=[jax.ShapeDtypeStruct((NL, nsteps, D, 2 * FFN_STEP), bf16),
                   jax.ShapeDtypeStruct((NL, nsteps, FFN_STEP, D), bf16)],
        compiler_params=_params(("parallel", "parallel"), 40),
        name="ffn_weight_layout",
    )(*([w_up] * (2 * tp) + [w_down] * tp))
    return [dict(wu=wu, wd=wd, layer=l, F=F, cw=_ffn_to_steps(conv_w[l], F),
                 cb=_ffn_to_steps(conv_b[l], F).reshape(nsteps, 1, 2 * FFN_STEP)) for l in range(NL)]


def _conv_ffn(x, g, fw, *, seq_len, tm, prev=None, final_g=None):
    M, D = x.shape
    F = fw['F']
    layer = fw['layer']
    nsteps = fw['wu'].shape[1]
    sample = prev is not None
    rc = min(tm, FFN_ROW_CHUNK)
    tiles_per_seq = max(seq_len // tm, 1)
    W2 = 2 * FFN_STEP

    def body(*refs):
        it = iter(refs)
        x_ref = next(it)
        xh_ref = None if sample else next(it)
        prev_refs = [next(it) for _ in range(2 * FFN_STEP // LANES)] if sample else None
        g_ref = next(it)
        wu_ref, cw_ref, cb_ref, wd_ref = next(it), next(it), next(it), next(it)
        fg_ref = next(it) if final_g is not None else None
        o_ref, tail_g_ref, tail_v_ref = next(it), next(it), next(it)
        xn_ref, up_scr = next(it), next(it)
        i = pl.program_id(0)
        j = pl.program_id(1)

        @pl.when(j == 0)
        def _():
            if sample:
                xn_ref[0:HALO, :] = jnp.zeros((HALO, D), bf16)
            else:
                keep = (i % tiles_per_seq != 0).astype(f32)
                hist = _rms(xh_ref[...], g_ref[...]) * keep
                xn_ref[0:HALO, :] = jnp.concatenate([jnp.zeros_like(hist), hist], axis=0).astype(bf16)

            def chunk(r, c):
                rows = pl.ds(pl.multiple_of(r * rc, rc), rc)
                xr = x_ref[rows, :]
                o_ref[rows, :] = xr
                xn_ref[pl.ds(pl.multiple_of(HALO + r * rc, HALO), rc), :] = _rms(xr, g_ref[...]).astype(bf16)
                return c
            lax.fori_loop(0, tm // rc, chunk, 0)

        cw = cw_ref[0]
        cb = cb_ref[0]
        nchunk = tm // rc
        for r in range(nchunk):
            if sample:
                up_scr[r, HALO:, :] = _dot(xn_ref[pl.ds(HALO + r * rc, rc), :], wu_ref[...])
            else:
                up_scr[r] = _dot(xn_ref[pl.ds(r * rc, rc + HALO), :], wu_ref[...])
        for r in range(nchunk):
            rows = pl.ds(r * rc, rc)
            up = up_scr[r, HALO:, :]
            if sample:
                p0 = jnp.concatenate([p[rows, 0, :] for p in prev_refs], axis=1)
                p1 = jnp.concatenate([p[rows, 1, :] for p in prev_refs], axis=1)
                conv = cb + cw[0:1] * p0 + cw[1:2] * p1 + cw[2:3] * up
                tail_g_ref[rows, 0, :] = p1[:, :FFN_STEP]
                tail_g_ref[rows, 1, :] = up[:, :FFN_STEP]
                tail_v_ref[rows, 0, :] = p1[:, FFN_STEP:]
                tail_v_ref[rows, 1, :] = up[:, FFN_STEP:]
            else:
                conv = (cb + cw[0:1] * up_scr[r, HALO - 2:HALO - 2 + rc, :]
                        + cw[1:2] * up_scr[r, HALO - 1:HALO - 1 + rc, :] + cw[2:3] * up)
                if r == nchunk - 1:
                    tail_g_ref[...] = up[rc - SUBLANES:, :FFN_STEP]
                    tail_v_ref[...] = up[rc - SUBLANES:, FFN_STEP:]
            h = jax.nn.gelu(conv[:, :FFN_STEP]) * conv[:, FFN_STEP:]
            o_ref[rows, :] += _dot(h.astype(bf16), wd_ref[...])

        if final_g is not None:
            @pl.when(j == nsteps - 1)
            def _():
                def chunk2(r, c):
                    rows = pl.ds(pl.multiple_of(r * rc, rc), rc)
                    o_ref[rows, :] = _rms(o_ref[rows, :], fg_ref[...])
                    return c
                lax.fori_loop(0, tm // rc, chunk2, 0)

    in_specs = [pl.BlockSpec((tm, D), lambda i, j: (i, 0))]
    args = [x]
    if sample:
        nt = F // LANES
        tp = FFN_STEP // LANES
        last = 2 * nt - 1
        for off in list(range(tp)) + [nt + t for t in range(tp)]:
            in_specs.append(pl.BlockSpec((None, tm, 2, LANES),
                                         lambda i, j, off=off: (layer, i, 0, jnp.minimum(tp * j + off, last))))
            args.append(prev)
    else:
        in_specs.append(pl.BlockSpec((SUBLANES, D), lambda i, j: (jnp.maximum(i * (tm // SUBLANES) - 1, 0), 0)))
        args.append(x)
    in_specs += [pl.BlockSpec((1, D), lambda i, j: (0, 0)),
                 pl.BlockSpec((None, None, D, W2), lambda i, j: (layer, j, 0, 0)),
                 pl.BlockSpec((1, 3, W2), lambda i, j: (j, 0, 0)),
                 pl.BlockSpec((1, 1, W2), lambda i, j: (j, 0, 0)),
                 pl.BlockSpec((None, None, FFN_STEP, D), lambda i, j: (layer, j, 0, 0))]
    args += [g.reshape(1, D), fw['wu'], fw['cw'], fw['cb'], fw['wd']]
    if final_g is not None:
        in_specs.append(pl.BlockSpec((1, D), lambda i, j: (0, 0)))
        args.append(final_g.reshape(1, D))
    FP = nsteps * FFN_STEP
    if sample:
        tail_shape = (M, 2, FP)
        tail_spec = pl.BlockSpec((tm, 2, FFN_STEP), lambda i, j: (i, 0, j))
    else:
        tail_shape = (M // tm, SUBLANES, FP)
        tail_spec = pl.BlockSpec((None, SUBLANES, FFN_STEP), lambda i, j: (i, 0, j))
    out, tail_g, tail_v = pl.pallas_call(
        body,
        grid=(M // tm, nsteps),
        in_specs=in_specs,
        out_specs=[pl.BlockSpec((tm, D), lambda i, j: (i, 0)), tail_spec, tail_spec],
        out_shape=[jax.ShapeDtypeStruct((M, D), f32), jax.ShapeDtypeStruct(tail_shape, f32),
                   jax.ShapeDtypeStruct(tail_shape, f32)],
        scratch_shapes=[pltpu.VMEM((tm + HALO, D), bf16), pltpu.VMEM((tm // rc, rc + HALO, W2), f32)],
        compiler_params=_params(("parallel", "arbitrary"), 56),
        name="conv_ffn_sample" if sample else "conv_ffn",
    )(*args)
    if sample:
        new_buf = jnp.concatenate([tail_g[:, :, :F], tail_v[:, :, :F]], axis=-1)
    else:
        nseq = M // seq_len
        pick = lambda t: t.reshape(nseq, tiles_per_seq, SUBLANES, FP)[:, -1, SUBLANES - 2:, :F]
        new_buf = jnp.concatenate([pick(tail_g), pick(tail_v)], axis=-1)
    return out, new_buf


def _gla_prompt(p0, alr, w2p, b_alpha, g_norm, *, B, L, H, DK, DV, tb):
    NT = L // tb
    NC = tb // CHUNK
    C = CHUNK
    scale = DK ** -0.5
    qk_blocks = H
    v_off = 2 * H * DK // DV

    def body(q_ref, k_ref, v_ref, r_ref, a_ref, w2_ref, ba_ref, gn_ref, y_ref, s_out_ref, s_scr):
        t = pl.program_id(2)

        @pl.when(t == 0)
        def _():
            s_scr[...] = jnp.zeros_like(s_scr)

        z = _dot(a_ref[...].astype(bf16), w2_ref[...].astype(bf16)) + ba_ref[...]
        gl = jax.nn.log_sigmoid(z) * (1.0 / GLA_TAU)
        bc3 = _chunk_cumsum(gl, C).reshape(NC, C, DK)
        bl3 = bc3[:, C - 1:C, :]
        q3 = (q_ref[...].astype(f32) * scale).reshape(NC, C, DK)
        k3 = k_ref[...].astype(f32).reshape(NC, C, DK)
        v3 = v_ref[...].astype(bf16).reshape(NC, C, DV)
        qd3 = (q3 * jnp.exp(bc3)).astype(bf16)
        kd3 = (k3 * jnp.exp(-bc3)).astype(bf16)
        kdec3 = (k3 * jnp.exp(bl3 - bc3)).astype(bf16)
        causal = (lax.broadcasted_iota(jnp.int32, (1, C, C), 1) >= lax.broadcasted_iota(jnp.int32, (1, C, C), 2))
        att = jnp.where(causal, jnp.einsum('cik,cjk->cij', qd3, kd3, preferred_element_type=f32), 0.0)
        intra = jnp.einsum('cij,cjv->civ', att.astype(bf16), v3, preferred_element_type=f32)
        ds = jnp.einsum('cjk,cjv->ckv', kdec3, v3, preferred_element_type=f32)
        s = s_scr[...]
        s_in = []
        for c in range(NC):
            s_in.append(s)
            s = _col_from_row(jnp.exp(bl3[c])) * s + ds[c]
        s_scr[...] = s
        s_all = jnp.stack(s_in).astype(bf16)
        o = (intra + jnp.einsum('cik,ckv->civ', qd3, s_all, preferred_element_type=f32)).reshape(tb, DV)
        rr = r_ref[...].astype(f32)
        y_ref[...] = (_rms(o, gn_ref[...]) * (rr * jax.nn.sigmoid(rr))).astype(bf16)

        @pl.when(t == NT - 1)
        def _():
            s_out_ref[0, 0] = s
    return pl.pallas_call(
        body,
        grid=(B, H, NT),
        in_specs=[pl.BlockSpec((tb, DK), lambda b, h, t: (b * NT + t, h)),
                  pl.BlockSpec((tb, DK), lambda b, h, t: (b * NT + t, qk_blocks + h)),
                  pl.BlockSpec((tb, DV), lambda b, h, t: (b * NT + t, v_off + h)),
                  pl.BlockSpec((tb, DV), lambda b, h, t: (b * NT + t, v_off + H + h)),
                  pl.BlockSpec((tb, LANES), lambda b, h, t: (b * NT + t, 0)),
                  pl.BlockSpec((LANES, DK), lambda b, h, t: (0, h)),
                  pl.BlockSpec((1, DK), lambda b, h, t: (0, h)),
                  pl.BlockSpec((1, DV), lambda b, h, t: (0, h))],
        out_specs=[pl.BlockSpec((tb, DV), lambda b, h, t: (b * NT + t, h)),
                   pl.BlockSpec((1, 1, DK, DV), lambda b, h, t: (b, h, 0, 0))],
        out_shape=[jax.ShapeDtypeStruct((B * L, H * DV), bf16),
                   jax.ShapeDtypeStruct((B, H, DK, DV), f32)],
        scratch_shapes=[pltpu.VMEM((DK, DV), f32)],
        compiler_params=_params(("parallel", "parallel", "arbitrary"), 32),
        name="gla_prompt",
    )(p0, p0, p0, p0, alr, w2p, b_alpha.reshape(1, -1), g_norm.reshape(1, -1))


def _gla_sample(p0, alr, w2p, b_alpha, g_norm, s0, *, H, DK, DV):
    Bs = p0.shape[0]
    scale = DK ** -0.5
    qkw = H * DK
    vw = H * DV
    assert vw % qkw == 0
    SB = SUBLANES

    def body(q_ref, k_ref, v_ref, r_ref, a_ref, w2_ref, ba_ref, gn_ref, s_ref, y_ref, so_ref, gl_scr):
        z = _dot(a_ref[...].astype(bf16), w2_ref[...].astype(bf16)) + ba_ref[...]
        gl_scr[...] = jax.nn.log_sigmoid(z) * (1.0 / GLA_TAU)

        for h in range(H):
            ks = slice(h * DK, (h + 1) * DK)
            vs = slice(h * DV, (h + 1) * DV)
            a_t = jnp.exp(gl_scr[:, ks]).T
            k_t = k_ref[:, ks].T
            q_t = (q_ref[:, ks] * scale).T
            v = v_ref[:, vs]
            outs = []
            for s in range(SB):
                sn = a_t[:, s:s + 1] * s_ref[s, h] + k_t[:, s:s + 1] * v[s:s + 1, :]
                so_ref[s, h] = sn
                outs.append(jnp.sum(q_t[:, s:s + 1] * sn, axis=0, keepdims=True))
            o = jnp.concatenate(outs, axis=0)
            rr = r_ref[:, vs]
            y_ref[:, vs] = _rms(o, gn_ref[:, vs]) * (rr * jax.nn.sigmoid(rr))

    v_blk = 2 * qkw // vw
    return pl.pallas_call(
        body,
        grid=(Bs // SB,),
        in_specs=[pl.BlockSpec((SB, qkw), lambda b: (b, 0)),
                  pl.BlockSpec((SB, qkw), lambda b: (b, 1)),
                  pl.BlockSpec((SB, vw), lambda b: (b, v_blk)),
                  pl.BlockSpec((SB, vw), lambda b: (b, v_blk + 1)),
                  pl.BlockSpec((SB, LANES), lambda b: (b, 0)),
                  pl.BlockSpec((LANES, qkw), lambda b: (0, 0)),
                  pl.BlockSpec((1, qkw), lambda b: (0, 0)),
                  pl.BlockSpec((1, vw), lambda b: (0, 0)),
                  pl.BlockSpec((SB, H, DK, DV), lambda b: (b, 0, 0, 0))],
        out_specs=[pl.BlockSpec((SB, vw), lambda b: (b, 0)),
                   pl.BlockSpec((SB, H, DK, DV), lambda b: (b, 0, 0, 0))],
        out_shape=[jax.ShapeDtypeStruct((Bs, vw), f32),
                   jax.ShapeDtypeStruct((Bs, H, DK, DV), f32)],
        scratch_shapes=[pltpu.VMEM((SB, qkw), f32)],
        compiler_params=_params(("parallel",), 32),
        name="gla_sample",
    )(p0, p0, p0, p0, alr, w2p, b_alpha.reshape(1, -1), g_norm.reshape(1, -1), s0)


def _rg_gates(xc, wr, br, wi, bi, sp):
    xb = xc.astype(bf16)
    r = jax.nn.sigmoid(_dot(xb, wr) + br)
    i = jax.nn.sigmoid(_dot(xb, wi) + bi)
    log_a = -RG_C * r * sp
    a = jnp.exp(log_a)
    mult = jnp.sqrt(1.0 - a * a)
    return a, mult, i


def _rglru_prompt(p0, conv_w, conv_b, w_r, b_r, w_i, b_i, lam, *, B, L, x_blk, g_blk):
    M = B * L
    NB, BS, _ = w_r.shape
    assert BS == LANES
    rc = min(256, L)

    def body(x_ref, gg_ref, cw_ref, cb_ref, wr_ref, br_ref, wi_ref, bi_ref, lam_ref, y_ref, hl_ref, a_scr, b_scr, xs_scr):
        wr = wr_ref[...].astype(bf16)
        wi = wi_ref[...].astype(bf16)
        sp = jax.nn.softplus(-lam_ref[...])
        cw = cw_ref[...]

        def chunk(c, carry):
            r0 = pl.multiple_of(c * rc, rc)
            rows = pl.ds(r0, rc)
            x = x_ref[rows, :].astype(f32)
            start = (r0 % L) == 0
            prev = x_ref[pl.ds(pl.multiple_of(jnp.maximum(r0 - HALO, 0), HALO), HALO), :].astype(f32)[HALO - SUBLANES:]
            xs_scr[0:SUBLANES, :] = jnp.where(start, 0.0, prev)
            xs_scr[SUBLANES:, :] = x
            xc = (cb_ref[...] + cw[0:1] * xs_scr[SUBLANES - 3:SUBLANES - 3 + rc, :]
                  + cw[1:2] * xs_scr[SUBLANES - 2:SUBLANES - 2 + rc, :]
                  + cw[2:3] * xs_scr[SUBLANES - 1:SUBLANES - 1 + rc, :] + cw[3:4] * x)
            a, mult, ig = _rg_gates(xc, wr, br_ref[...], wi, bi_ref[...], sp)
            first = jnp.logical_and(start, lax.broadcasted_iota(jnp.int32, (rc, 1), 0) == 0)
            mult = jnp.where(first, 1.0, mult)
            a_scr[rows, :] = a
            b_scr[rows, :] = mult * (ig * xc)
            return carry
        lax.fori_loop(0, M // rc, chunk, 0, unroll=2)

        def step(t, hs):
            new = []
            for b in range(B):
                row = pl.ds(b * L + t, 1)
                h = a_scr[row, :] * hs[b] + b_scr[row, :]
                b_scr[row, :] = h
                new.append(h)
            return tuple(new)
        hs = lax.fori_loop(0, L, step, tuple(jnp.zeros((1, LANES), f32) for _ in range(B)), unroll=32)
        hl_ref[...] = jnp.concatenate(hs, axis=0)

        def outc(c, carry):
            rows = pl.ds(pl.multiple_of(c * rc, rc), rc)
            y_ref[rows, :] = (b_scr[rows, :] * jax.nn.gelu(gg_ref[rows, :].astype(f32))).astype(bf16)
            return carry
        lax.fori_loop(0, M // rc, outc, 0, unroll=2)

    W = NB * BS
    return pl.pallas_call(
        body,
        grid=(NB,),
        in_specs=[pl.BlockSpec((M, LANES), lambda n: (0, x_blk + n)),
                  pl.BlockSpec((M, LANES), lambda n: (0, g_blk + n)),
                  pl.BlockSpec((4, LANES), lambda n: (0, n)),
                  pl.BlockSpec((1, LANES), lambda n: (0, n)),
                  pl.BlockSpec((None, BS, BS), lambda n: (n, 0, 0)),
                  pl.BlockSpec((1, LANES), lambda n: (0, n)),
                  pl.BlockSpec((None, BS, BS), lambda n: (n, 0, 0)),
                  pl.BlockSpec((1, LANES), lambda n: (0, n)),
                  pl.BlockSpec((1, LANES), lambda n: (0, n))],
        out_specs=[pl.BlockSpec((M, LANES), lambda n: (0, n)),
                   pl.BlockSpec((B, LANES), lambda n: (0, n))],
        out_shape=[jax.ShapeDtypeStruct((M, W), bf16), jax.ShapeDtypeStruct((B, W), f32)],
        scratch_shapes=[pltpu.VMEM((M, LANES), f32), pltpu.VMEM((M, LANES), f32), pltpu.VMEM((rc + SUBLANES, LANES), f32)],
        compiler_params=_params(("parallel",), 48),
        name="rglru_prompt",
    )(p0, p0, conv_w, conv_b.reshape(1, W), w_r, b_r.reshape(1, W), w_i, b_i.reshape(1, W), lam.reshape(1, W))


def _rglru_sample(p0, conv_state, h0, conv_w, conv_b, w_r, b_r, w_i, b_i, lam, *, x_blk, g_blk):
    Bs = p0.shape[0]
    NB, BS, _ = w_r.shape
    W = NB * BS
    s0, s1, s2 = conv_state[:, 0], conv_state[:, 1], conv_state[:, 2]

    def body(x_ref, gg_ref, s0_ref, s1_ref, s2_ref, h0_ref, cw_ref, cb_ref, wr_ref, br_ref, wi_ref, bi_ref, lam_ref,
             y_ref, h_ref):
        cw = cw_ref[...]
        x = x_ref[...]
        xc = cb_ref[...] + cw[0:1] * s0_ref[...] + cw[1:2] * s1_ref[...] + cw[2:3] * s2_ref[...] + cw[3:4] * x
        sp = jax.nn.softplus(-lam_ref[...])
        a, mult, ig = _rg_gates(xc, wr_ref[...].astype(bf16), br_ref[...], wi_ref[...].astype(bf16), bi_ref[...], sp)
        if PAST_LEN == 0:
            mult = jnp.ones_like(mult)
        h = a * h0_ref[...] + mult * (ig * xc)
        h_ref[...] = h
        y_ref[...] = (h * jax.nn.gelu(gg_ref[...])).astype(bf16)

    blk = lambda n: (0, n)
    vec = pl.BlockSpec((1, LANES), blk)
    mat = pl.BlockSpec((Bs, LANES), blk)
    y, h = pl.pallas_call(
        body,
        grid=(NB,),
        in_specs=[pl.BlockSpec((Bs, LANES), lambda n: (0, x_blk + n)),
                  pl.BlockSpec((Bs, LANES), lambda n: (0, g_blk + n)),
                  mat, mat, mat, mat,
                  pl.BlockSpec((4, LANES), blk), vec,
                  pl.BlockSpec((None, BS, BS), lambda n: (n, 0, 0)), vec,
                  pl.BlockSpec((None, BS, BS), lambda n: (n, 0, 0)), vec, vec],
        out_specs=[mat, mat],
        out_shape=[jax.ShapeDtypeStruct((Bs, W), bf16), jax.ShapeDtypeStruct((Bs, W), f32)],
        compiler_params=_params(("parallel",), 32),
        name="rglru_sample",
    )(p0, p0, s0, s1, s2, h0, conv_w, conv_b.reshape(1, W), w_r, b_r.reshape(1, W), w_i, b_i.reshape(1, W),
      lam.reshape(1, W))
    return y, h


def _blockdiag_tiles(w, tile):
    nblk, bs, _ = w.shape
    per = tile // bs
    nt = nblk // per
    rows = jnp.tile(w.reshape(nt, tile, bs), (1, 1, per))
    on_diag = (jnp.arange(tile)[:, None] // bs) == (jnp.arange(tile)[None, :] // bs)
    return jnp.where(on_diag[None], rows, 0.0)


def _mlstm_stage1(p1, conv_w, conv_b, wq_t, wk_t, wv_t, wg, bg, *, seq_len, tr, H, DH, conv_state=None):
    M = p1.shape[0]
    W = H * DH
    NTL, TL, _ = wq_t.shape
    sample = conv_state is not None
    tiles_per_seq = max(seq_len // tr, 1)
    kscale = DH ** -0.5

    def body(*refs):
        it = iter(refs)
        x_ref = next(it)
        if sample:
            s0_ref, s1_ref, s2_ref = next(it), next(it), next(it)
        else:
            xh_ref = next(it)
        cw_ref, cb_ref, wq_ref, wk_ref, wv_ref, wg_ref, bg_ref = (next(it) for _ in range(7))
        q_ref, k_ref, v_ref, g_ref, xc_ref = (next(it) for _ in range(5))
        cw = cw_ref[...]
        x = x_ref[...].astype(f32)
        if sample:
            conv = cb_ref[...] + cw[0:1] * s0_ref[...] + cw[1:2] * s1_ref[...] + cw[2:3] * s2_ref[...] + cw[3:4] * x
        else:
            xs_scr = next(it)
            keep = (pl.program_id(0) % tiles_per_seq != 0).astype(f32)
            xs_scr[0:SUBLANES, :] = xh_ref[...].astype(f32)[HALO - SUBLANES:] * keep
            xs_scr[SUBLANES:, :] = x
            conv = (cb_ref[...] + cw[0:1] * xs_scr[SUBLANES - 3:SUBLANES - 3 + tr, :]
                    + cw[1:2] * xs_scr[SUBLANES - 2:SUBLANES - 2 + tr, :]
                    + cw[2:3] * xs_scr[SUBLANES - 1:SUBLANES - 1 + tr, :] + cw[3:4] * x)
        xc = conv * jax.nn.sigmoid(conv)
        xc_ref[...] = xc.astype(xc_ref.dtype)
        xcb = xc.astype(bf16)
        xb = x.astype(bf16)
        qs, ks, vs = [], [], []
        for t in range(NTL):
            cs = slice(t * TL, (t + 1) * TL)
            qs.append(_dot(xcb[:, cs], wq_ref[t]))
            ks.append(_dot(xcb[:, cs], wk_ref[t]) * kscale)
            vs.append(_dot(xb[:, cs], wv_ref[t]))
        q = jnp.concatenate(qs, axis=1)
        k = jnp.concatenate(ks, axis=1)
        v = jnp.concatenate(vs, axis=1)
        q_ref[...] = q.astype(q_ref.dtype)
        k_ref[...] = k.astype(k_ref.dtype)
        v_ref[...] = v.astype(v_ref.dtype)
        gt = (_dot(q.astype(bf16), wg_ref[0:W, :]) + _dot(k.astype(bf16), wg_ref[W:2 * W, :])
              + _dot(v.astype(bf16), wg_ref[2 * W:3 * W, :]) + bg_ref[...])
        lane = lax.broadcasted_iota(jnp.int32, gt.shape, 1)
        g_ref[...] = jnp.where(jnp.logical_and(lane >= H, lane < 2 * H), jax.nn.log_sigmoid(gt), gt)

    row = lambda i: (i, 0)
    const2 = lambda i: (0, 0)
    const3 = lambda i: (0, 0, 0)
    in_specs = [pl.BlockSpec((tr, W), row)]
    args = [p1]
    if sample:
        in_specs += [pl.BlockSpec((tr, W), row)] * 3
        args += [conv_state[:, 0], conv_state[:, 1], conv_state[:, 2]]
    else:
        in_specs.append(pl.BlockSpec((HALO, W), lambda i: (jnp.maximum(i * (tr // HALO) - 1, 0), 0)))
        args.append(p1)
    in_specs += [pl.BlockSpec((4, W), const2), pl.BlockSpec((1, W), const2),
                 pl.BlockSpec((NTL, TL, TL), const3), pl.BlockSpec((NTL, TL, TL), const3),
                 pl.BlockSpec((NTL, TL, TL), const3),
                 pl.BlockSpec((3 * W, LANES), const2), pl.BlockSpec((1, LANES), const2)]
    args += [conv_w, conv_b.reshape(1, W), wq_t, wk_t, wv_t, wg, bg]
    return pl.pallas_call(
        body,
        grid=(M // tr,),
        in_specs=in_specs,
        out_specs=[pl.BlockSpec((tr, W), row)] * 3 + [pl.BlockSpec((tr, LANES), row), pl.BlockSpec((tr, W), row)],
        out_shape=[jax.ShapeDtypeStruct((M, W), f32 if sample else bf16)] * 3
        + [jax.ShapeDtypeStruct((M, LANES), f32), jax.ShapeDtypeStruct((M, W), f32 if sample else bf16)],
        scratch_shapes=[] if sample else [pltpu.VMEM((tr + SUBLANES, W), f32)],
        compiler_params=_params(("parallel",), 48),
        name="mlstm_stage1_sample" if sample else "mlstm_stage1",
    )(*args)


def _mlstm_prompt(q, k, v, gates, xc, p1, g_norm, skip, *, B, L, H, DH, tb):
    NT = L // tb
    NC = tb // CHUNK
    C = CHUNK

    def body(q_ref, k_ref, v_ref, g_ref, xc_ref, om_ref, gn_ref, sk_ref,
             y_ref, c_out, n_out, m_out, c_scr, n_scr, m_scr):
        hh = pl.program_id(1)
        t = pl.program_id(2)

        @pl.when(t == 0)
        def _():
            c_scr[...] = jnp.zeros_like(c_scr)
            n_scr[...] = jnp.zeros_like(n_scr)
            m_scr[...] = jnp.zeros_like(m_scr)

        gts = g_ref[...]
        lane = lax.broadcasted_iota(jnp.int32, (tb, LANES), 1)
        i_col = jnp.sum(jnp.where(lane == hh, gts, 0.0), axis=1, keepdims=True)
        b_col = jnp.sum(jnp.where(lane == H + hh, _chunk_cumsum(gts, C), 0.0), axis=1, keepdims=True)
        b3 = b_col.reshape(NC, C, 1)
        i3 = i_col.reshape(NC, C, 1)
        ii = lax.broadcasted_iota(jnp.int32, (1, C, C), 1)
        jj = lax.broadcasted_iota(jnp.int32, (1, C, C), 2)
        eye = ii == jj
        causal = ii >= jj
        as_row = lambda col3: jnp.sum(jnp.where(eye, jnp.broadcast_to(col3, (NC, C, C)), 0.0), axis=1, keepdims=True)
        dmat = jnp.where(causal, b3 - as_row(b3) + as_row(i3), -jnp.inf)
        rmax = jnp.max(dmat, axis=2, keepdims=True)
        b_last = b3[:, C - 1:C, :]
        m_prev = m_scr[...]
        m_in = []
        for c in range(NC):
            m_in.append(m_prev)
            m_prev = jnp.maximum(b_last[c] + m_prev, rmax[c][C - 1:C, :])
        m_scr[...] = m_prev
        inter = b3 + jnp.stack(m_in)
        m_col = jnp.maximum(inter, rmax)
        g_col = jnp.exp(inter - m_col)
        q3 = q_ref[...].reshape(NC, C, DH)
        k3 = k_ref[...].reshape(NC, C, DH)
        v3 = v_ref[...].reshape(NC, C, DH)
        s = jnp.einsum('cid,cjd->cij', q3, k3, preferred_element_type=f32) * jnp.exp(dmat - m_col)
        num = jnp.einsum('cij,cje->cie', s.astype(bf16), v3, preferred_element_type=f32)
        den = jnp.sum(s, axis=2, keepdims=True)
        m_new = m_col[:, C - 1:C, :]
        wk = jnp.exp(b_last - b3 + i3 - m_new)
        gc = jnp.exp(inter[:, C - 1:C, :] - m_new)
        kw = k3.astype(f32) * wk
        dc = jnp.einsum('cse,csd->ced', v3, kw.astype(bf16), preferred_element_type=f32)
        dn = jnp.sum(kw, axis=1, keepdims=True)
        cs = c_scr[...]
        ns = n_scr[...]
        c_in, n_in = [], []
        for c in range(NC):
            c_in.append(cs)
            n_in.append(ns)
            cs = gc[c] * cs + dc[c]
            ns = gc[c] * ns + dn[c]
        c_scr[...] = cs
        n_scr[...] = ns
        c_all = jnp.stack(c_in).astype(bf16)
        n_all = jnp.stack(n_in)
        num = num + g_col * jnp.einsum('cid,ced->cie', q3, c_all, preferred_element_type=f32)
        den = den + g_col * jnp.sum(q3.astype(f32) * n_all, axis=2, keepdims=True)
        hm = (num / jnp.maximum(jnp.abs(den), jnp.exp(-m_col))).reshape(tb, DH)
        y = (_rms(hm, gn_ref[...]) + sk_ref[...] * xc_ref[...].astype(f32)) * jax.nn.sigmoid(om_ref[...].astype(f32))
        y_ref[...] = y.astype(bf16)

        @pl.when(t == NT - 1)
        def _():
            c_out[0, 0] = cs
            n_out[0, 0] = ns
            m_out[0, 0] = m_prev

    blk = lambda b, h, t: (b * NT + t, h)
    W = H * DH
    y, c_new, n_new, m_new = pl.pallas_call(
        body,
        grid=(B, H, NT),
        in_specs=[pl.BlockSpec((tb, DH), blk), pl.BlockSpec((tb, DH), blk), pl.BlockSpec((tb, DH), blk),
                  pl.BlockSpec((tb, LANES), lambda b, h, t: (b * NT + t, 0)),
                  pl.BlockSpec((tb, DH), blk),
                  pl.BlockSpec((tb, DH), lambda b, h, t: (b * NT + t, H + h)),
                  pl.BlockSpec((1, DH), lambda b, h, t: (0, h)),
                  pl.BlockSpec((1, DH), lambda b, h, t: (0, h))],
        out_specs=[pl.BlockSpec((tb, DH), blk),
                   pl.BlockSpec((1, 1, DH, DH), lambda b, h, t: (b, h, 0, 0)),
                   pl.BlockSpec((1, 1, 1, DH), lambda b, h, t: (b, h, 0, 0)),
                   pl.BlockSpec((1, 1, 1, 1), lambda b, h, t: (b, h, 0, 0))],
        out_shape=[jax.ShapeDtypeStruct((B * L, W), bf16),
                   jax.ShapeDtypeStruct((B, H, DH, DH), f32),
                   jax.ShapeDtypeStruct((B, H, 1, DH), f32),
                   jax.ShapeDtypeStruct((B, H, 1, 1), f32)],
        scratch_shapes=[pltpu.VMEM((DH, DH), f32), pltpu.VMEM((1, DH), f32), pltpu.VMEM((1, 1), f32)],
        compiler_params=_params(("parallel", "parallel", "arbitrary"), 32),
        name="mlstm_prompt",
    )(q, k, v, gates, xc, p1, g_norm.reshape(1, W), skip.reshape(1, W))
    return y, c_new, n_new.reshape(B, H, DH), m_new.reshape(B, H)


def _mlstm_sample(q, k, v, gates, xc, p1, g_norm, skip, c0, n0, m0, *, H, DH):
    Bs = q.shape[0]
    W = H * DH
    SB = SUBLANES

    def body(q_ref, k_ref, v_ref, g_ref, xc_ref, om_ref, gn_ref, sk_ref, c_ref, n_ref, m_ref,
             y_ref, c_out, n_out, m_out):
        gts = g_ref[...]
        for h in range(H):
            cs = slice(h * DH, (h + 1) * DH)
            ig = gts[:, h:h + 1]
            fg = gts[:, H + h:H + h + 1]
            inter = fg + m_ref[:, h:h + 1]
            m = jnp.maximum(inter, ig)
            g = jnp.exp(inter - m)
            m_out[:, h:h + 1] = m
            q = q_ref[:, cs]
            kw = k_ref[:, cs] * jnp.exp(ig - m)
            nn = g * n_ref[:, cs] + kw
            n_out[:, cs] = nn
            den = jnp.sum(nn * q, axis=1, keepdims=True)
            v_t = v_ref[:, cs].T
            qb = q.astype(bf16)
            nums = []
            for s in range(SB):
                cn = g[s:s + 1, :] * c_ref[s, h] + v_t[:, s:s + 1] * kw[s:s + 1, :]
                c_out[s, h] = cn
                nums.append(_dot_nt(qb, cn.astype(bf16))[s:s + 1, :])
            num = jnp.concatenate(nums, axis=0)
            hm = num / jnp.maximum(jnp.abs(den), jnp.exp(-m))
            y_ref[:, cs] = ((_rms(hm, gn_ref[:, cs]) + sk_ref[:, cs] * xc_ref[:, cs])
                            * jax.nn.sigmoid(om_ref[:, cs]))

    per = lambda b: (b, 0)
    const2 = lambda b: (0, 0)
    y, c_new, n_new, m_new = pl.pallas_call(
        body,
        grid=(Bs // SB,),
        in_specs=[pl.BlockSpec((SB, W), per), pl.BlockSpec((SB, W), per), pl.BlockSpec((SB, W), per),
                  pl.BlockSpec((SB, LANES), per), pl.BlockSpec((SB, W), per),
                  pl.BlockSpec((SB, W), lambda b: (b, 1)),
                  pl.BlockSpec((1, W), const2), pl.BlockSpec((1, W), const2),
                  pl.BlockSpec((SB, H, DH, DH), lambda b: (b, 0, 0, 0)),
                  pl.BlockSpec((SB, W), per), pl.BlockSpec((SB, H), per)],
        out_specs=[pl.BlockSpec((SB, W), per),
                   pl.BlockSpec((SB, H, DH, DH), lambda b: (b, 0, 0, 0)),
                   pl.BlockSpec((SB, W), per), pl.BlockSpec((SB, H), per)],
        out_shape=[jax.ShapeDtypeStruct((Bs, W), f32), jax.ShapeDtypeStruct((Bs, H, DH, DH), f32),
                   jax.ShapeDtypeStruct((Bs, W), f32), jax.ShapeDtypeStruct((Bs, H), f32)],
        compiler_params=_params(("parallel",), 48),
        name="mlstm_sample",
    )(q, k, v, gates, xc, p1, g_norm.reshape(1, W), skip.reshape(1, W), c0, n0.reshape(Bs, W), m0)
    return y, c_new, n_new.reshape(Bs, H, DH), m_new


S5_TILES = 8


def _s5_layouts(lam_re, lam_im, log_dt, b_re, b_im, c_re, c_im):
    G, P = lam_re.shape
    GC = b_re.shape[2]
    T = S5_TILES
    gpt = G // T
    ns = G * P
    flat = lambda a: a.reshape(ns)
    ldt = jnp.broadcast_to(log_dt[:, None], (G, P))
    rows = [flat(a).reshape(T, 1, ns // T) for a in (lam_re, lam_im, ldt)]
    eye = jnp.eye(gpt, dtype=f32)
    bbd = [jnp.einsum('jgpc,gh->jgchp', a.reshape(T, gpt, P, GC), eye).reshape(T, gpt * GC, gpt * P) for a in (b_re, b_im)]
    cbd = [jnp.einsum('jgcp,gh->jgphc', a.reshape(T, gpt, GC, P), eye).reshape(T, gpt * P, gpt * GC) for a in (c_re, c_im)]
    return rows, bbd, cbd


def _s5_discretise(lre, lim, ldt):
    dt = jnp.exp(ldt)
    mag = jnp.exp(dt * lre)
    ar = mag * jnp.cos(dt * lim)
    ai = mag * jnp.sin(dt * lim)
    den = lre * lre + lim * lim
    cr = ((ar - 1.0) * lre + ai * lim) / den
    ci = (ai * lre - (ar - 1.0) * lim) / den
    return ar, ai, cr, ci


def _s5_mixer(p1, u_blk, rows, bbd, cbd, d_skip, w_glu, b_glu, *, B, L, tb, state=None):
    T = S5_TILES
    lre_r, lim_r, ldt_r = rows
    SW = lre_r.shape[2]
    CW = bbd[0].shape[1]
    W = T * CW
    NS = T * SW
    KT = SW // LANES
    sample = state is not None
    NT = 1 if sample else L // tb
    M = B * L

    def body(*refs):
        it = iter(refs)
        u_ref = next(it)
        if sample:
            x0r_ref, x0i_ref = next(it), next(it)
        lre_ref, lim_ref, ldt_ref = next(it), next(it), next(it)
        bre_ref, bim_ref, cre_ref, cim_ref = next(it), next(it), next(it), next(it)
        d_ref, wg_ref, bgl_ref = next(it), next(it), next(it)
        y_ref, xr_out, xi_out = next(it), next(it), next(it)
        bbr, bbi, cpair, ar_scr, ai_scr = (next(it) for _ in range(5))
        if not sample:
            sre, sim, xr_c, xi_c, yacc = (next(it) for _ in range(5))
        first = jnp.logical_and(pl.program_id(0) == 0, pl.program_id(1) == 0)

        @pl.when(first)
        def _():
            for j in range(T):
                ar, ai, cr, ci = _s5_discretise(lre_ref[j], lim_ref[j], ldt_ref[j])
                ar_scr[j] = ar
                ai_scr[j] = ai
                br = bre_ref[j]
                bi = bim_ref[j]
                bbr[j] = (cr * br - ci * bi).astype(bf16)
                bbi[j] = (cr * bi + ci * br).astype(bf16)
            for jp in range(T // 2):
                cpair[jp] = jnp.zeros((4 * SW, 2 * CW), bf16)
                for half in range(2):
                    j = 2 * jp + half
                    r0 = 2 * half * SW
                    cols = slice(half * CW, (half + 1) * CW)
                    cpair[jp, r0:r0 + SW, cols] = cre_ref[j].astype(bf16)
                    cpair[jp, r0 + SW:r0 + 2 * SW, cols] = (-cim_ref[j]).astype(bf16)

        u = u_ref[...].astype(f32)
        ub = u_ref[...].astype(bf16)
        ys = []
        if sample:
            for j in range(T):
                cs = slice(j * SW, (j + 1) * SW)
                uj = ub[:, j * CW:(j + 1) * CW]
                ar = ar_scr[j]
                ai = ai_scr[j]
                x0r = x0r_ref[:, cs]
                x0i = x0i_ref[:, cs]
                xr = ar * x0r - ai * x0i + _dot(uj, bbr[j])
                xi = ar * x0i + ai * x0r + _dot(uj, bbi[j])
                xr_out[:, cs] = xr
                xi_out[:, cs] = xi
                ys += [xr.astype(bf16), xi.astype(bf16)]
            y = jnp.concatenate([_dot(jnp.concatenate(ys[4 * jp:4 * jp + 4], axis=1), cpair[jp]) for jp in range(T // 2)],
                                axis=1)
        else:
            t = pl.program_id(1)

            @pl.when(t == 0)
            def _():
                xr_c[...] = jnp.zeros_like(xr_c)
                xi_c[...] = jnp.zeros_like(xi_c)

            for j in range(T):
                uj = ub[:, j * CW:(j + 1) * CW]
                r = _dot(uj, bbr[j])
                im = _dot(uj, bbi[j])
                for kk in range(KT):
                    sre[kk, pl.ds(j, tb, stride=T), :] = r[:, kk * LANES:(kk + 1) * LANES]
                    sim[kk, pl.ds(j, tb, stride=T), :] = im[:, kk * LANES:(kk + 1) * LANES]
            a_r = [jnp.concatenate([ar_scr[j][:, kk * LANES:(kk + 1) * LANES] for j in range(T)], axis=0) for kk in range(KT)]
            a_i = [jnp.concatenate([ai_scr[j][:, kk * LANES:(kk + 1) * LANES] for j in range(T)], axis=0) for kk in range(KT)]

            def step(s, carry):
                xr, xi = carry
                row = pl.ds(pl.multiple_of(s * T, T), T)
                nr, ni = [], []
                for kk in range(KT):
                    r_ = a_r[kk] * xr[kk] - a_i[kk] * xi[kk] + sre[kk, row, :]
                    i_ = a_r[kk] * xi[kk] + a_i[kk] * xr[kk] + sim[kk, row, :]
                    sre[kk, row, :] = r_
                    sim[kk, row, :] = i_
                    nr.append(r_)
                    ni.append(i_)
                return tuple(nr), tuple(ni)
            xr0 = tuple(xr_c[kk] for kk in range(KT))
            xi0 = tuple(xi_c[kk] for kk in range(KT))
            xr, xi = lax.fori_loop(0, tb, step, (xr0, xi0), unroll=4)
            for kk in range(KT):
                xr_c[kk] = xr[kk]
                xi_c[kk] = xi[kk]

            @pl.when(t == NT - 1)
            def _():
                for kk in range(KT):
                    xr_out[kk] = xr[kk]
                    xi_out[kk] = xi[kk]

            for jp in range(T // 2):
                parts = []
                for j in (2 * jp, 2 * jp + 1):
                    parts += [sre[kk, pl.ds(j, tb, stride=T), :] for kk in range(KT)]
                    parts += [sim[kk, pl.ds(j, tb, stride=T), :] for kk in range(KT)]
                xp = jnp.concatenate(parts, axis=1).astype(bf16)
                yacc[:, 2 * jp * CW:(2 * jp + 2) * CW] = _dot(xp, cpair[jp])
            y = yacc[...]
        ysk = jax.nn.gelu(y + d_ref[...] * u)
        z = _dot(ysk.astype(bf16), wg_ref[...]) + bgl_ref[...]
        y_ref[...] = (ysk * jax.nn.sigmoid(z)).astype(bf16)

    c3 = lambda b, t: (0, 0, 0)
    c2 = lambda b, t: (0, 0)
    in_specs = [pl.BlockSpec((tb, W), lambda b, t: (b * NT + t, u_blk))]
    args = [p1]
    if sample:
        in_specs += [pl.BlockSpec((tb, NS), lambda b, t: (b, 0))] * 2
        args += [state[0], state[1]]
    in_specs += [pl.BlockSpec((T, 1, SW), c3)] * 3
    in_specs += [pl.BlockSpec((T, CW, SW), c3)] * 2 + [pl.BlockSpec((T, SW, CW), c3)] * 2
    in_specs += [pl.BlockSpec((1, W), c2), pl.BlockSpec((W, W), c2), pl.BlockSpec((1, W), c2)]
    args += [lre_r, lim_r, ldt_r, bbd[0], bbd[1], cbd[0], cbd[1], d_skip.reshape(1, W), w_glu, b_glu.reshape(1, W)]
    scratch = [pltpu.VMEM((T, CW, SW), bf16), pltpu.VMEM((T, CW, SW), bf16),
               pltpu.VMEM((T // 2, 4 * SW, 2 * CW), bf16),
               pltpu.VMEM((T, 1, SW), f32), pltpu.VMEM((T, 1, SW), f32)]
    if sample:
        grid = (M // tb, 1)
        st_spec = pl.BlockSpec((tb, NS), lambda b, t: (b, 0))
        st_shape = jax.ShapeDtypeStruct((M, NS), f32)
    else:
        grid = (B, NT)
        st_spec = pl.BlockSpec((None, KT, T, LANES), lambda b, t: (b, 0, 0, 0))
        st_shape = jax.ShapeDtypeStruct((B, KT, T, LANES), f32)
        scratch += [pltpu.VMEM((KT, tb * T, LANES), f32), pltpu.VMEM((KT, tb * T, LANES), f32),
                    pltpu.VMEM((KT, T, LANES), f32), pltpu.VMEM((KT, T, LANES), f32), pltpu.VMEM((tb, W), f32)]
    y, xr, xi = pl.pallas_call(
        body,
        grid=grid,
        in_specs=in_specs,
        out_specs=[pl.BlockSpec((tb, W), lambda b, t: (b * NT + t, 0)), st_spec, st_spec],
        out_shape=[jax.ShapeDtypeStruct((M, W), bf16), st_shape, st_shape],
        scratch_shapes=scratch,
        compiler_params=_params(("arbitrary", "arbitrary"), 56),
        name="s5_sample" if sample else "s5_prompt",
    )(*args)
    if not sample:
        xr = jnp.transpose(xr, (0, 2, 1, 3)).reshape(B, NS)
        xi = jnp.transpose(xi, (0, 2, 1, 3)).reshape(B, NS)
    return y, xr, xi


def _trunk(x3, st, w, *, sample):
    B, L, D = x3.shape
    M = B * L
    x = x3.reshape(M, D)
    H_g, DK, DV = st['gla_S'].shape[1:] if sample else w['gla_dims']
    H_m, DH = w['ml_dims']
    G, P = w['s5_dims']
    tm = M if sample else min(1024, L)
    tf = M if sample else min(1024, L)
    to = M if sample else min(2048, M)
    tb = min(2048, L)
    ts = min(512, L)
    tr = M if sample else min(512, L)
    tn_in, tn_out = 1024, 512
    pdt = f32 if sample else bf16
    out = {}

    p0, alr = _norm_matmul(x, w['g_mix0'], w['w_in0_main'], tm=tm, tn=tn_in, w_side=w['w_in0_alr'], out_dtype=pdt, w_t=True,
                            seg_cols=w['w_in0_cols'])
    x_blk = (2 * H_g * DK + 2 * H_g * DV) // LANES
    W_rg = w['rg_lambda'].shape[0]
    g_blk = x_blk + W_rg // LANES
    rg_tail = p0.reshape(B, L, p0.shape[1])[:, max(L - 3, 0):, x_blk * LANES:x_blk * LANES + W_rg].astype(f32)
    if sample:
        ya, out['gla_S'] = _gla_sample(p0, alr, w['gla_w2p'], w['gla_b_alpha'], w['gla_g_norm'], st['gla_S'],
                                       H=H_g, DK=DK, DV=DV)
        yb, out['rg_h'] = _rglru_sample(p0, st['rg_conv'], st['rg_h'], w['rg_conv_w'], w['rg_conv_b'], w['rg_w_r'],
                                        w['rg_b_r'], w['rg_w_i'], w['rg_b_i'], w['rg_lambda'], x_blk=x_blk, g_blk=g_blk)
        out['rg_conv'] = jnp.concatenate([st['rg_conv'][:, 1:], rg_tail], axis=1)
    else:
        ya, out['gla_S'] = _gla_prompt(p0, alr, w['gla_w2p'], w['gla_b_alpha'], w['gla_g_norm'],
                                       B=B, L=L, H=H_g, DK=DK, DV=DV, tb=tb)
        yb, out['rg_h'] = _rglru_prompt(p0, w['rg_conv_w'], w['rg_conv_b'], w['rg_w_r'], w['rg_b_r'], w['rg_w_i'],
                                        w['rg_b_i'], w['rg_lambda'], B=B, L=L, x_blk=x_blk, g_blk=g_blk)
        out['rg_conv'] = rg_tail
    x = _out_proj(ya, yb, w['w_out0'], x, tm=to, tn=tn_out)
    x, ffn0 = _conv_ffn(x, w['g_ffn'][0], w['ffn'][0], seq_len=L, tm=tf, prev=st['ffn_conv'] if sample else None)

    W_ml = H_m * DH
    p1 = _norm_matmul(x, w['g_mix1'], [w['w_in1']], tm=tm, tn=tn_in, out_dtype=pdt)
    xm_tail = p1.reshape(B, L, p1.shape[1])[:, max(L - 3, 0):, :W_ml].astype(f32)
    q, k, v, gates, xc = _mlstm_stage1(p1, w['ml_conv_w'], w['ml_conv_b'], w['ml_wq_t'], w['ml_wk_t'], w['ml_wv_t'],
                                       w['ml_wg'], w['ml_bg'], seq_len=L, tr=tr, H=H_m, DH=DH,
                                       conv_state=st['ml_conv'] if sample else None)
    u_blk = 2 * W_ml // (G * w['s5_gc'])
    if sample:
        yc, out['ml_C'], out['ml_n'], out['ml_m'] = _mlstm_sample(
            q, k, v, gates, xc, p1, w['ml_g_norm'], w['ml_skip'], st['ml_C'], st['ml_n'], st['ml_m'], H=H_m, DH=DH)
        out['ml_conv'] = jnp.concatenate([st['ml_conv'][:, 1:], xm_tail], axis=1)
        yd, s5r, s5i = _s5_mixer(p1, u_blk, w['s5_rows'], w['s5_bbd'], w['s5_cbd'], w['s5_D'], w['s5_w_glu'],
                                 w['s5_b_glu'], B=B, L=1, tb=B,
                                 state=(st['s5_re'].reshape(B, G * P), st['s5_im'].reshape(B, G * P)))
    else:
        yc, out['ml_C'], out['ml_n'], out['ml_m'] = _mlstm_prompt(
            q, k, v, gates, xc, p1, w['ml_g_norm'], w['ml_skip'], B=B, L=L, H=H_m, DH=DH, tb=tb)
        out['ml_conv'] = xm_tail
        yd, s5r, s5i = _s5_mixer(p1, u_blk, w['s5_rows'], w['s5_bbd'], w['s5_cbd'], w['s5_D'], w['s5_w_glu'],
                                 w['s5_b_glu'], B=B, L=L, tb=ts)
    out['s5_re'] = s5r.reshape(B, G, P)
    out['s5_im'] = s5i.reshape(B, G, P)
    x = _out_proj(yc, yd, w['w_out1'], x, tm=to, tn=tn_out)
    x, ffn1 = _conv_ffn(x, w['g_ffn'][1], w['ffn'][1], seq_len=L, tm=tf, prev=st['ffn_conv'] if sample else None,
                        final_g=w['g_final'])
    out['ffn_conv'] = jnp.stack([ffn0, ffn1], axis=0)
    return x.reshape(B, L, D), out


def kernel(x_prompt, x_sample, state_gla_S, state_rglru_h, state_rglru_conv, state_mlstm_C, state_mlstm_n, state_mlstm_m, state_mlstm_conv, state_s5_re, state_s5_im, state_ffn_conv, g_mix0, w_in0, gla_w_alpha2, gla_b_alpha, gla_g_norm, rg_conv_w, rg_conv_b, rg_w_r, rg_b_r, rg_w_i, rg_b_i, rg_lambda, w_out0, g_mix1, w_in1, ml_conv_w, ml_conv_b, ml_wq, ml_wk, ml_wv, ml_w_igate, ml_b_igate, ml_w_fgate, ml_b_fgate, ml_g_norm, ml_skip, s5_lam_re, s5_lam_im, s5_log_dt, s5_B_re, s5_B_im, s5_C_re, s5_C_im, s5_D, s5_w_glu, s5_b_glu, w_out1, g_ffn, ffn_w_up, ffn_conv_w, ffn_conv_b, ffn_w_down, g_final):
    _, H_g, DK, DV = state_gla_S.shape
    _, H_m, DH, _ = state_mlstm_C.shape
    G, P = s5_lam_re.shape
    rank = gla_w_alpha2.shape[0]
    n_main = 2 * H_g * DK + 2 * H_g * DV
    w_in0_t = w_in0.T.astype(bf16)
    w_in0_main = [w_in0_t, w_in0_t[n_main + rank:]]
    w_in0_cols = [n_main, w_in0.shape[1] - n_main - rank]
    w_in0_alr = jnp.pad(w_in0_t[n_main:n_main + rank], ((0, LANES - rank), (0, 0)))
    gla_w2p = jnp.pad(gla_w_alpha2, ((0, LANES - rank), (0, 0)))
    ml_tile = 256
    ml_wg = jnp.pad(jnp.concatenate([ml_w_igate, ml_w_fgate], axis=1), ((0, 0), (0, LANES - 2 * H_m))).astype(bf16)
    ml_bg = jnp.pad(jnp.concatenate([ml_b_igate, ml_b_fgate]), (0, LANES - 2 * H_m)).reshape(1, LANES)
    rows, bbd, cbd = _s5_layouts(s5_lam_re, s5_lam_im, s5_log_dt, s5_B_re, s5_B_im, s5_C_re, s5_C_im)
    w = dict(
        g_mix0=g_mix0, w_in0_main=w_in0_main, w_in0_cols=w_in0_cols, w_in0_alr=w_in0_alr, gla_w2p=gla_w2p, gla_b_alpha=gla_b_alpha,
        gla_g_norm=gla_g_norm, gla_dims=(H_g, DK, DV), rg_conv_w=rg_conv_w, rg_conv_b=rg_conv_b, rg_w_r=rg_w_r,
        rg_b_r=rg_b_r, rg_w_i=rg_w_i, rg_b_i=rg_b_i, rg_lambda=rg_lambda, w_out0=w_out0.astype(bf16),
        g_mix1=g_mix1, w_in1=w_in1.astype(bf16), ml_conv_w=ml_conv_w, ml_conv_b=ml_conv_b,
        ml_wq_t=_blockdiag_tiles(ml_wq, ml_tile).astype(bf16), ml_wk_t=_blockdiag_tiles(ml_wk, ml_tile).astype(bf16),
        ml_wv_t=_blockdiag_tiles(ml_wv, ml_tile).astype(bf16), ml_wg=ml_wg, ml_bg=ml_bg, ml_g_norm=ml_g_norm,
        ml_skip=ml_skip, ml_dims=(H_m, DH), s5_dims=(G, P), s5_gc=s5_B_re.shape[2], s5_rows=rows, s5_bbd=bbd,
        s5_cbd=cbd, s5_D=s5_D, s5_w_glu=s5_w_glu.astype(bf16), s5_b_glu=s5_b_glu, w_out1=w_out1.astype(bf16),
        g_ffn=g_ffn, g_final=g_final,
        ffn=_ffn_prepare(ffn_w_up, ffn_conv_w, ffn_conv_b, ffn_w_down))
    st_s = dict(gla_S=state_gla_S, rg_h=state_rglru_h, rg_conv=state_rglru_conv, ml_C=state_mlstm_C,
                ml_n=state_mlstm_n, ml_m=state_mlstm_m, ml_conv=state_mlstm_conv, s5_re=state_s5_re,
                s5_im=state_s5_im, ffn_conv=state_ffn_conv)
    y_p, np_ = _trunk(x_prompt, None, w, sample=False)
    y_s, ns_ = _trunk(x_sample, st_s, w, sample=True)
    names = ('gla_S', 'rg_h', 'rg_conv', 'ml_C', 'ml_n', 'ml_m', 'ml_conv', 's5_re', 's5_im', 'ffn_conv')
    outs = [y_p, y_s]
    for nme in names:
        outs += [np_[nme], ns_[nme]]
    return tuple(outs)
```

```python
import jax
import jax.numpy as jnp
from jax import lax
from jax.experimental import pallas as pl
from jax.experimental.pallas import tpu as pltpu

f32 = jnp.float32
bf16 = jnp.bfloat16

EPS = 1e-6
CHUNK = 64
GLA_TAU = 16.0
RG_C = 8.0
PAST_LEN = 16384
LANES = 128
SUBLANES = 8
HALO = 16
MIB = 1024 * 1024


def _params(sem, vmem_mib):
    return pltpu.CompilerParams(dimension_semantics=sem, vmem_limit_bytes=int(vmem_mib * MIB))


def _dot(a, b):
    return jnp.dot(a, b, preferred_element_type=f32)


def _dot_nt(a, b):
    return lax.dot_general(a, b, (((1,), (1,)), ((), ())), preferred_element_type=f32)


def _dot_tn(a, b):
    return lax.dot_general(a, b, (((0,), (0,)), ((), ())), preferred_element_type=f32)


def _rms(x, g):
    return x * lax.rsqrt(jnp.mean(x * x, axis=-1, keepdims=True) + EPS) * g


def _eye(n):
    return lax.broadcasted_iota(jnp.int32, (n, n), 0) == lax.broadcasted_iota(jnp.int32, (n, n), 1)


def _col_from_row(row):
    n = row.shape[1]
    return jnp.sum(jnp.where(_eye(n), jnp.broadcast_to(row, (n, n)), 0.0), axis=1, keepdims=True)


def _chunk_cumsum(x, chunk):
    pos = lax.broadcasted_iota(jnp.int32, (x.shape[0], 1), 0) % chunk
    step = 1
    while step < chunk:
        x = x + jnp.where(pos >= step, pltpu.roll(x, step, 0), 0.0)
        step *= 2
    return x


def _norm_matmul(x, g, ws, *, tm, tn, w_side=None, out_dtype=f32, w_t=False, seg_cols=None):
    M, D = x.shape
    mm = _dot_nt if w_t else _dot
    seg_cols = seg_cols or [w.shape[0 if w_t else 1] for w in ws]
    nblk = [c // tn for c in seg_cols]
    start = [sum(nblk[:s]) for s in range(len(ws) + 1)]
    N = tn * start[-1]
    rc = min(tm, 256)
    side = w_side is not None

    def body(*refs):
        x_ref, g_ref = refs[:2]
        w_refs = refs[2:2 + len(ws)]
        rest = refs[2 + len(ws):]
        if side:
            ws_ref, o_ref, os_ref, xn_ref = rest
        else:
            o_ref, xn_ref = rest
        j = pl.program_id(1)

        @pl.when(j == 0)
        def _():
            def chunk(r, c):
                rows = pl.ds(pl.multiple_of(r * rc, rc), rc)
                xn_ref[rows, :] = _rms(x_ref[rows, :], g_ref[...]).astype(bf16)
                return c
            lax.fori_loop(0, tm // rc, chunk, 0)
            if side:
                os_ref[...] = mm(xn_ref[...], ws_ref[...])

        if len(ws) == 1:
            o_ref[...] = mm(xn_ref[...], w_refs[0][...]).astype(out_dtype)
        else:
            for s, w_ref in enumerate(w_refs):
                @pl.when(jnp.logical_and(j >= start[s], j < start[s + 1]))
                def _(w_ref=w_ref):
                    o_ref[...] = mm(xn_ref[...], w_ref[...]).astype(out_dtype)

    in_specs = [pl.BlockSpec((tm, D), lambda i, j: (i, 0)),
                pl.BlockSpec((1, D), lambda i, j: (0, 0))]
    wblk = lambda j, s: jnp.clip(j - start[s], 0, nblk[s] - 1)
    if w_t:
        in_specs += [pl.BlockSpec((tn, D), lambda i, j, s=s: (wblk(j, s), 0)) for s in range(len(ws))]
    else:
        in_specs += [pl.BlockSpec((D, tn), lambda i, j, s=s: (0, wblk(j, s))) for s in range(len(ws))]
    out_specs = [pl.BlockSpec((tm, tn), lambda i, j: (i, j))]
    out_shape = [jax.ShapeDtypeStruct((M, N), out_dtype)]
    args = [x, g.reshape(1, D)] + list(ws)
    if side:
        ns = w_side.shape[0 if w_t else 1]
        in_specs.append(pl.BlockSpec((ns, D) if w_t else (D, ns), lambda i, j: (0, 0)))
        out_specs.append(pl.BlockSpec((tm, ns), lambda i, j: (i, 0)))
        out_shape.append(jax.ShapeDtypeStruct((M, ns), f32))
        args.append(w_side)
    outs = pl.pallas_call(
        body,
        grid=(M // tm, N // tn),
        in_specs=in_specs,
        out_specs=out_specs,
        out_shape=out_shape,
        scratch_shapes=[pltpu.VMEM((tm, D), bf16)],
        compiler_params=_params(("parallel", "arbitrary"), 56),
        name="norm_matmul",
    )(*args)
    return outs if side else outs[0]


def _out_proj(ya, yb, w, res, *, tm, tn):
    M, Ka = ya.shape
    Kb = yb.shape[1]
    N = w.shape[1]
    assert Ka == Kb and w.shape[0] == Ka + Kb

    def body(ya_ref, yb_ref, wa_ref, wb_ref, r_ref, o_ref):
        o_ref[...] = (r_ref[...] + _dot(ya_ref[...].astype(bf16), wa_ref[...])
                      + _dot(yb_ref[...].astype(bf16), wb_ref[...]))

    return pl.pallas_call(
        body,
        grid=(M // tm, N // tn),
        in_specs=[pl.BlockSpec((tm, Ka), lambda i, j: (i, 0)),
                  pl.BlockSpec((tm, Kb), lambda i, j: (i, 0)),
                  pl.BlockSpec((Ka, tn), lambda i, j: (0, j)),
                  pl.BlockSpec((Kb, tn), lambda i, j: (1, j)),
                  pl.BlockSpec((tm, tn), lambda i, j: (i, j))],
        out_specs=pl.BlockSpec((tm, tn), lambda i, j: (i, j)),
        out_shape=jax.ShapeDtypeStruct((M, N), f32),
        compiler_params=_params(("parallel", "arbitrary"), 48),
        name="out_proj",
    )(ya, yb, w, w, res)


FFN_STEP = 512
FFN_ROW_CHUNK = 512


def _ffn_to_steps(a, F):
    nsteps = -(-F // FFN_STEP)
    lead = a.shape[:-1]
    nd = len(lead)
    gv = a.reshape(lead + (2, F))
    gv = jnp.pad(gv, [(0, 0)] * (nd + 1) + [(0, nsteps * FFN_STEP - F)])
    gv = gv.reshape(lead + (2, nsteps, FFN_STEP))
    gv = jnp.transpose(gv, (nd + 1,) + tuple(range(nd)) + (nd, nd + 2))
    return gv.reshape((nsteps,) + lead + (2 * FFN_STEP,))


def _ffn_prepare(w_up, conv_w, conv_b, w_down):
    NL, F, D = w_down.shape
    assert F % LANES == 0 and FFN_STEP % LANES == 0
    nt = F // LANES
    tp = FFN_STEP // LANES
    nsteps = -(-F // FFN_STEP)

    def body(*refs):
        g_in, v_in, d_in = refs[:tp], refs[tp:2 * tp], refs[2 * tp:3 * tp]
        wu_o, wd_o = refs[3 * tp:]
        for t in range(tp):
            keep = tp * pl.program_id(1) + t < nt
            cols = slice(t * LANES, (t + 1) * LANES)
            wu_o[:, cols] = jnp.where(keep, g_in[t][...], 0.0).astype(bf16)
            wu_o[:, FFN_STEP + t * LANES:FFN_STEP + (t + 1) * LANES] = jnp.where(keep, v_in[t][...], 0.0).astype(bf16)
            wd_o[cols, :] = jnp.where(keep, d_in[t][...], 0.0).astype(bf16)

    tile = lambda j, t: jnp.minimum(tp * j + t, nt - 1)
    in_specs = ([pl.BlockSpec((None, D, LANES), lambda l, j, t=t: (l, 0, tile(j, t))) for t in range(tp)]
                + [pl.BlockSpec((None, D, LANES), lambda l, j, t=t: (l, 0, nt + tile(j, t))) for t in range(tp)]
                + [pl.BlockSpec((None, LANES, D), lambda l, j, t=t: (l, tile(j, t), 0)) for t in range(tp)])
    wu, wd = pl.pallas_call(
        body,
        grid=(NL, nsteps),
        in_specs=in_specs,
        out_specs=[pl.BlockSpec((None, None, D, 2 * FFN_STEP), lambda l, j: (l, j, 0, 0)),
                   pl.BlockSpec((None, None, FFN_STEP, D), lambda l, j: (l, j, 0, 0))],
        out_shape=[jax.ShapeDtypeStruct((NL, nsteps, D, 2 * FFN_STEP), bf16),
                   jax.ShapeDtypeStruct((NL, nsteps, FFN_STEP, D), bf16)],
        compiler_params=_params(("parallel", "parallel"), 40),
        name="ffn_weight_layout",
    )(*([w_up] * (2 * tp) + [w_down] * tp))
    return [dict(wu=wu, wd=wd, layer=l, F=F, cw=_ffn_to_steps(conv_w[l], F),
                 cb=_ffn_to_steps(conv_b[l], F).reshape(nsteps, 1, 2 * FFN_STEP)) for l in range(NL)]


def _conv_ffn(x, g, fw, *, seq_len, tm, prev=None, final_g=None):
    M, D = x.shape
    F = fw['F']
    layer = fw['layer']
    nsteps = fw['wu'].shape[1]
    sample = prev is not None
    rc = min(tm, FFN_ROW_CHUNK)
    tiles_per_seq = max(seq_len // tm, 1)
    W2 = 2 * FFN_STEP

    def body(*refs):
        it = iter(refs)
        x_ref = next(it)
        xh_ref = None if sample else next(it)
        prev_refs = [next(it) for _ in range(2 * FFN_STEP // LANES)] if sample else None
        g_ref = next(it)
        wu_ref, cw_ref, cb_ref, wd_ref = next(it), next(it), next(it), next(it)
        fg_ref = next(it) if final_g is not None else None
        o_ref, tail_g_ref, tail_v_ref = next(it), next(it), next(it)
        xn_ref, up_scr = next(it), next(it)
        i = pl.program_id(0)
        j = pl.program_id(1)

        @pl.when(j == 0)
        def _():
            if sample:
                xn_ref[0:HALO, :] = jnp.zeros((HALO, D), bf16)
            else:
                keep = (i % tiles_per_seq != 0).astype(f32)
                hist = _rms(xh_ref[...], g_ref[...]) * keep
                xn_ref[0:HALO, :] = jnp.concatenate([jnp.zeros_like(hist), hist], axis=0).astype(bf16)

            def chunk(r, c):
                rows = pl.ds(pl.multiple_of(r * rc, rc), rc)
                xr = x_ref[rows, :]
                o_ref[rows, :] = xr
                xn_ref[pl.ds(pl.multiple_of(HALO + r * rc, HALO), rc), :] = _rms(xr, g_ref[...]).astype(bf16)
                return c
            lax.fori_loop(0, tm // rc, chunk, 0)

        cw = cw_ref[0]
        cb = cb_ref[0]
        nchunk = tm // rc
        for r in range(nchunk):
            if sample:
                up_scr[r, HALO:, :] = _dot(xn_ref[pl.ds(HALO + r * rc, rc), :], wu_ref[...])
            else:
                up_scr[r] = _dot(xn_ref[pl.ds(r * rc, rc + HALO), :], wu_ref[...])
        for r in range(nchunk):
            rows = pl.ds(r * rc, rc)
            up = up_scr[r, HALO:, :]
            if sample:
                p0 = jnp.concatenate([p[rows, 0, :] for p in prev_refs], axis=1)
                p1 = jnp.concatenate([p[rows, 1, :] for p in prev_refs], axis=1)
                conv = cb + cw[0:1] * p0 + cw[1:2] * p1 + cw[2:3] * up
                tail_g_ref[rows, 0, :] = p1[:, :FFN_STEP]
                tail_g_ref[rows, 1, :] = up[:, :FFN_STEP]
                tail_v_ref[rows, 0, :] = p1[:, FFN_STEP:]
                tail_v_ref[rows, 1, :] = up[:, FFN_STEP:]
            else:
                conv = (cb + cw[0:1] * up_scr[r, HALO - 2:HALO - 2 + rc, :]
                        + cw[1:2] * up_scr[r, HALO - 1:HALO - 1 + rc, :] + cw[2:3] * up)
                if r == nchunk - 1:
                    tail_g_ref[...] = up[rc - SUBLANES:, :FFN_STEP]
                    tail_v_ref[...] = up[rc - SUBLANES:, FFN_STEP:]
            h = jax.nn.gelu(conv[:, :FFN_STEP]) * conv[:, FFN_STEP:]
            o_ref[rows, :] += _dot(h.astype(bf16), wd_ref[...])

        if final_g is not None:
            @pl.when(j == nsteps - 1)
            def _():
                def chunk2(r, c):
                    rows = pl.ds(pl.multiple_of(r * rc, rc), rc)
                    o_ref[rows, :] = _rms(o_ref[rows, :], fg_ref[...])
                    return c
                lax.fori_loop(0, tm // rc, chunk2, 0)

    in_specs = [pl.BlockSpec((tm, D), lambda i, j: (i, 0))]
    args = [x]
    if sample:
        nt = F // LANES
        tp = FFN_STEP // LANES
        last = 2 * nt - 1
        for off in list(range(tp)) + [nt + t for t in range(tp)]:
            in_specs.append(pl.BlockSpec((None, tm, 2, LANES),
                                         lambda i, j, off=off: (layer, i, 0, jnp.minimum(tp * j + off, last))))
            args.append(prev)
    else:
        in_specs.append(pl.BlockSpec((SUBLANES, D), lambda i, j: (jnp.maximum(i * (tm // SUBLANES) - 1, 0), 0)))
        args.append(x)
    in_specs += [pl.BlockSpec((1, D), lambda i, j: (0, 0)),
                 pl.BlockSpec((None, None, D, W2), lambda i, j: (layer, j, 0, 0)),
                 pl.BlockSpec((1, 3, W2), lambda i, j: (j, 0, 0)),
                 pl.BlockSpec((1, 1, W2), lambda i, j: (j, 0, 0)),
                 pl.BlockSpec((None, None, FFN_STEP, D), lambda i, j: (layer, j, 0, 0))]
    args += [g.reshape(1, D), fw['wu'], fw['cw'], fw['cb'], fw['wd']]
    if final_g is not None:
        in_specs.append(pl.BlockSpec((1, D), lambda i, j: (0, 0)))
        args.append(final_g.reshape(1, D))
    FP = nsteps * FFN_STEP
    if sample:
        tail_shape = (M, 2, FP)
        tail_spec = pl.BlockSpec((tm, 2, FFN_STEP), lambda i, j: (i, 0, j))
    else:
        tail_shape = (M // tm, SUBLANES, FP)
        tail_spec = pl.BlockSpec((None, SUBLANES, FFN_STEP), lambda i, j: (i, 0, j))
    out, tail_g, tail_v = pl.pallas_call(
        body,
        grid=(M // tm, nsteps),
        in_specs=in_specs,
        out_specs=[pl.BlockSpec((tm, D), lambda i, j: (i, 0)), tail_spec, tail_spec],
        out_shape=[jax.ShapeDtypeStruct((M, D), f32), jax.ShapeDtypeStruct(tail_shape, f32),
                   jax.ShapeDtypeStruct(tail_shape, f32)],
        scratch_shapes=[pltpu.VMEM((tm + HALO, D), bf16), pltpu.VMEM((tm // rc, rc + HALO, W2), f32)],
        compiler_params=_params(("parallel", "arbitrary"), 56),
        name="conv_ffn_sample" if sample else "conv_ffn",
    )(*args)
    if sample:
        new_buf = jnp.concatenate([tail_g[:, :, :F], tail_v[:, :, :F]], axis=-1)
    else:
        nseq = M // seq_len
        pick = lambda t: t.reshape(nseq, tiles_per_seq, SUBLANES, FP)[:, -1, SUBLANES - 2:, :F]
        new_buf = jnp.concatenate([pick(tail_g), pick(tail_v)], axis=-1)
    return out, new_buf


def _gla_prompt(p0, alr, w2p, b_alpha, g_norm, *, B, L, H, DK, DV, tb):
    NT = L // tb
    NC = tb // CHUNK
    C = CHUNK
    scale = DK ** -0.5
    qk_blocks = H
    v_off = 2 * H * DK // DV

    def body(q_ref, k_ref, v_ref, r_ref, a_ref, w2_ref, ba_ref, gn_ref, y_ref, s_out_ref, s_scr):
        t = pl.program_id(2)

        @pl.when(t == 0)
        def _():
            s_scr[...] = jnp.zeros_like(s_scr)

        z = _dot(a_ref[...].astype(bf16), w2_ref[...].astype(bf16)) + ba_ref[...]
        gl = jax.nn.log_sigmoid(z) * (1.0 / GLA_TAU)
        bc3 = _chunk_cumsum(gl, C).reshape(NC, C, DK)
        bl3 = bc3[:, C - 1:C, :]
        q3 = (q_ref[...].astype(f32) * scale).reshape(NC, C, DK)
        k3 = k_ref[...].astype(f32).reshape(NC, C, DK)
        v3 = v_ref[...].astype(bf16).reshape(NC, C, DV)
        qd3 = (q3 * jnp.exp(bc3)).astype(bf16)
        kd3 = (k3 * jnp.exp(-bc3)).astype(bf16)
        kdec3 = (k3 * jnp.exp(bl3 - bc3)).astype(bf16)
        causal = (lax.broadcasted_iota(jnp.int32, (1, C, C), 1) >= lax.broadcasted_iota(jnp.int32, (1, C, C), 2))
        att = jnp.where(causal, jnp.einsum('cik,cjk->cij', qd3, kd3, preferred_element_type=f32), 0.0)
        intra = jnp.einsum('cij,cjv->civ', att.astype(bf16), v3, preferred_element_type=f32)
        ds = jnp.einsum('cjk,cjv->ckv', kdec3, v3, preferred_element_type=f32)
        s = s_scr[...]
        s_in = []
        for c in range(NC):
            s_in.append(s)
            s = _col_from_row(jnp.exp(bl3[c])) * s + ds[c]
        s_scr[...] = s
        s_all = jnp.stack(s_in).astype(bf16)
        o = (intra + jnp.einsum('cik,ckv->civ', qd3, s_all, preferred_element_type=f32)).reshape(tb, DV)
        rr = r_ref[...].astype(f32)
        y_ref[...] = (_rms(o, gn_ref[...]) * (rr * jax.nn.sigmoid(rr))).astype(bf16)

        @pl.when(t == NT - 1)
        def _():
            s_out_ref[0, 0] = s
    return pl.pallas_call(
        body,
        grid=(B, H, NT),
        in_specs=[pl.BlockSpec((tb, DK), lambda b, h, t: (b * NT + t, h)),
                  pl.BlockSpec((tb, DK), lambda b, h, t: (b * NT + t, qk_blocks + h)),
                  pl.BlockSpec((tb, DV), lambda b, h, t: (b * NT + t, v_off + h)),
                  pl.BlockSpec((tb, DV), lambda b, h, t: (b * NT + t, v_off + H + h)),
                  pl.BlockSpec((tb, LANES), lambda b, h, t: (b * NT + t, 0)),
                  pl.BlockSpec((LANES, DK), lambda b, h, t: (0, h)),
                  pl.BlockSpec((1, DK), lambda b, h, t: (0, h)),
                  pl.BlockSpec((1, DV), lambda b, h, t: (0, h))],
        out_specs=[pl.BlockSpec((tb, DV), lambda b, h, t: (b * NT + t, h)),
                   pl.BlockSpec((1, 1, DK, DV), lambda b, h, t: (b, h, 0, 0))],
        out_shape=[jax.ShapeDtypeStruct((B * L, H * DV), bf16),
                   jax.ShapeDtypeStruct((B, H, DK, DV), f32)],
        scratch_shapes=[pltpu.VMEM((DK, DV), f32)],
        compiler_params=_params(("parallel", "parallel", "arbitrary"), 32),
        name="gla_prompt",
    )(p0, p0, p0, p0, alr, w2p, b_alpha.reshape(1, -1), g_norm.reshape(1, -1))


def _gla_sample(p0, alr, w2p, b_alpha, g_norm, s0, *, H, DK, DV):
    Bs = p0.shape[0]
    scale = DK ** -0.5
    qkw = H * DK
    vw = H * DV
    assert vw % qkw == 0
    SB = SUBLANES

    def body(q_ref, k_ref, v_ref, r_ref, a_ref, w2_ref, ba_ref, gn_ref, s_ref, y_ref, so_ref, gl_scr):
        z = _dot(a_ref[...].astype(bf16), w2_ref[...].astype(bf16)) + ba_ref[...]
        gl_scr[...] = jax.nn.log_sigmoid(z) * (1.0 / GLA_TAU)

        for h in range(H):
            ks = slice(h * DK, (h + 1) * DK)
            vs = slice(h * DV, (h + 1) * DV)
            a_t = jnp.exp(gl_scr[:, ks]).T
            k_t = k_ref[:, ks].T
            q_t = (q_ref[:, ks] * scale).T
            v = v_ref[:, vs]
            outs = []
            for s in range(SB):
                sn = a_t[:, s:s + 1] * s_ref[s, h] + k_t[:, s:s + 1] * v[s:s + 1, :]
                so_ref[s, h] = sn
                outs.append(jnp.sum(q_t[:, s:s + 1] * sn, axis=0, keepdims=True))
            o = jnp.concatenate(outs, axis=0)
            rr = r_ref[:, vs]
            y_ref[:, vs] = _rms(o, gn_ref[:, vs]) * (rr * jax.nn.sigmoid(rr))

    v_blk = 2 * qkw // vw
    return pl.pallas_call(
        body,
        grid=(Bs // SB,),
        in_specs=[pl.BlockSpec((SB, qkw), lambda b: (b, 0)),
                  pl.BlockSpec((SB, qkw), lambda b: (b, 1)),
                  pl.BlockSpec((SB, vw), lambda b: (b, v_blk)),
                  pl.BlockSpec((SB, vw), lambda b: (b, v_blk + 1)),
                  pl.BlockSpec((SB, LANES), lambda b: (b, 0)),
                  pl.BlockSpec((LANES, qkw), lambda b: (0, 0)),
                  pl.BlockSpec((1, qkw), lambda b: (0, 0)),
                  pl.BlockSpec((1, vw), lambda b: (0, 0)),
                  pl.BlockSpec((SB, H, DK, DV), lambda b: (b, 0, 0, 0))],
        out_specs=[pl.BlockSpec((SB, vw), lambda b: (b, 0)),
                   pl.BlockSpec((SB, H, DK, DV), lambda b: (b, 0, 0, 0))],
        out_shape=[jax.ShapeDtypeStruct((Bs, vw), f32),
                   jax.ShapeDtypeStruct((Bs, H, DK, DV), f32)],
        scratch_shapes=[pltpu.VMEM((SB, qkw), f32)],
        compiler_params=_params(("parallel",), 32),
        name="gla_sample",
    )(p0, p0, p0, p0, alr, w2p, b_alpha.reshape(1, -1), g_norm.reshape(1, -1), s0)


def _rg_gates(xc, wr, br, wi, bi, sp):
    xb = xc.astype(bf16)
    r = jax.nn.sigmoid(_dot(xb, wr) + br)
    i = jax.nn.sigmoid(_dot(xb, wi) + bi)
    log_a = -RG_C * r * sp
    a = jnp.exp(log_a)
    mult = jnp.sqrt(1.0 - a * a)
    return a, mult, i


def _rglru_prompt(p0, conv_w, conv_b, w_r, b_r, w_i, b_i, lam, *, B, L, x_blk, g_blk):
    M = B * L
    NB, BS, _ = w_r.shape
    assert BS == LANES
    rc = min(256, L)

    def body(x_ref, gg_ref, cw_ref, cb_ref, wr_ref, br_ref, wi_ref, bi_ref, lam_ref, y_ref, hl_ref, a_scr, b_scr, xs_scr):
        wr = wr_ref[...].astype(bf16)
        wi = wi_ref[...].astype(bf16)
        sp = jax.nn.softplus(-lam_ref[...])
        cw = cw_ref[...]

        def chunk(c, carry):
            r0 = pl.multiple_of(c * rc, rc)
            rows = pl.ds(r0, rc)
            x = x_ref[rows, :].astype(f32)
            start = (r0 % L) == 0
            prev = x_ref[pl.ds(pl.multiple_of(jnp.maximum(r0 - HALO, 0), HALO), HALO), :].astype(f32)[HALO - SUBLANES:]
            xs_scr[0:SUBLANES, :] = jnp.where(start, 0.0, prev)
            xs_scr[SUBLANES:, :] = x
            xc = (cb_ref[...] + cw[0:1] * xs_scr[SUBLANES - 3:SUBLANES - 3 + rc, :]
                  + cw[1:2] * xs_scr[SUBLANES - 2:SUBLANES - 2 + rc, :]
                  + cw[2:3] * xs_scr[SUBLANES - 1:SUBLANES - 1 + rc, :] + cw[3:4] * x)
            a, mult, ig = _rg_gates(xc, wr, br_ref[...], wi, bi_ref[...], sp)
            first = jnp.logical_and(start, lax.broadcasted_iota(jnp.int32, (rc, 1), 0) == 0)
            mult = jnp.where(first, 1.0, mult)
            a_scr[rows, :] = a
            b_scr[rows, :] = mult * (ig * xc)
            return carry
        lax.fori_loop(0, M // rc, chunk, 0, unroll=4)

        def step(t, hs):
            new = []
            for b in range(B):
                row = pl.ds(b * L + t, 1)
                h = a_scr[row, :] * hs[b] + b_scr[row, :]
                b_scr[row, :] = h
                new.append(h)
            return tuple(new)
        hs = lax.fori_loop(0, L, step, tuple(jnp.zeros((1, LANES), f32) for _ in range(B)), unroll=32)
        hl_ref[...] = jnp.concatenate(hs, axis=0)

        def outc(c, carry):
            rows = pl.ds(pl.multiple_of(c * rc, rc), rc)
            y_ref[rows, :] = (b_scr[rows, :] * jax.nn.gelu(gg_ref[rows, :].astype(f32))).astype(bf16)
            return carry
        lax.fori_loop(0, M // rc, outc, 0, unroll=2)

    W = NB * BS
    return pl.pallas_call(
        body,
        grid=(NB,),
        in_specs=[pl.BlockSpec((M, LANES), lambda n: (0, x_blk + n)),
                  pl.BlockSpec((M, LANES), lambda n: (0, g_blk + n)),
                  pl.BlockSpec((4, LANES), lambda n: (0, n)),
                  pl.BlockSpec((1, LANES), lambda n: (0, n)),
                  pl.BlockSpec((None, BS, BS), lambda n: (n, 0, 0)),
                  pl.BlockSpec((1, LANES), lambda n: (0, n)),
                  pl.BlockSpec((None, BS, BS), lambda n: (n, 0, 0)),
                  pl.BlockSpec((1, LANES), lambda n: (0, n)),
                  pl.BlockSpec((1, LANES), lambda n: (0, n))],
        out_specs=[pl.BlockSpec((M, LANES), lambda n: (0, n)),
                   pl.BlockSpec((B, LANES), lambda n: (0, n))],
        out_shape=[jax.ShapeDtypeStruct((M, W), bf16), jax.ShapeDtypeStruct((B, W), f32)],
        scratch_shapes=[pltpu.VMEM((M, LANES), f32), pltpu.VMEM((M, LANES), f32), pltpu.VMEM((rc + SUBLANES, LANES), f32)],
        compiler_params=_params(("parallel",), 48),
        name="rglru_prompt",
    )(p0, p0, conv_w, conv_b.reshape(1, W), w_r, b_r.reshape(1, W), w_i, b_i.reshape(1, W), lam.reshape(1, W))


def _rglru_sample(p0, conv_state, h0, conv_w, conv_b, w_r, b_r, w_i, b_i, lam, *, x_blk, g_blk):
    Bs = p0.shape[0]
    NB, BS, _ = w_r.shape
    W = NB * BS
    s0, s1, s2 = conv_state[:, 0], conv_state[:, 1], conv_state[:, 2]

    def body(x_ref, gg_ref, s0_ref, s1_ref, s2_ref, h0_ref, cw_ref, cb_ref, wr_ref, br_ref, wi_ref, bi_ref, lam_ref,
             y_ref, h_ref):
        cw = cw_ref[...]
        x = x_ref[...]
        xc = cb_ref[...] + cw[0:1] * s0_ref[...] + cw[1:2] * s1_ref[...] + cw[2:3] * s2_ref[...] + cw[3:4] * x
        sp = jax.nn.softplus(-lam_ref[...])
        a, mult, ig = _rg_gates(xc, wr_ref[...].astype(bf16), br_ref[...], wi_ref[...].astype(bf16), bi_ref[...], sp)
        if PAST_LEN == 0:
            mult = jnp.ones_like(mult)
        h = a * h0_ref[...] + mult * (ig * xc)
        h_ref[...] = h
        y_ref[...] = (h * jax.nn.gelu(gg_ref[...])).astype(bf16)

    blk = lambda n: (0, n)
    vec = pl.BlockSpec((1, LANES), blk)
    mat = pl.BlockSpec((Bs, LANES), blk)
    y, h = pl.pallas_call(
        body,
        grid=(NB,),
        in_specs=[pl.BlockSpec((Bs, LANES), lambda n: (0, x_blk + n)),
                  pl.BlockSpec((Bs, LANES), lambda n: (0, g_blk + n)),
                  mat, mat, mat, mat,
                  pl.BlockSpec((4, LANES), blk), vec,
                  pl.BlockSpec((None, BS, BS), lambda n: (n, 0, 0)), vec,
                  pl.BlockSpec((None, BS, BS), lambda n: (n, 0, 0)), vec, vec],
        out_specs=[mat, mat],
        out_shape=[jax.ShapeDtypeStruct((Bs, W), bf16), jax.ShapeDtypeStruct((Bs, W), f32)],
        compiler_params=_params(("parallel",), 32),
        name="rglru_sample",
    )(p0, p0, s0, s1, s2, h0, conv_w, conv_b.reshape(1, W), w_r, b_r.reshape(1, W), w_i, b_i.reshape(1, W),
      lam.reshape(1, W))
    return y, h


def _blockdiag_tiles(w, tile):
    nblk, bs, _ = w.shape
    per = tile // bs
    nt = nblk // per
    rows = jnp.tile(w.reshape(nt, tile, bs), (1, 1, per))
    on_diag = (jnp.arange(tile)[:, None] // bs) == (jnp.arange(tile)[None, :] // bs)
    return jnp.where(on_diag[None], rows, 0.0)


def _mlstm_stage1(p1, conv_w, conv_b, wq_t, wk_t, wv_t, wg, bg, *, seq_len, tr, H, DH, conv_state=None):
    M = p1.shape[0]
    W = H * DH
    NTL, TL, _ = wq_t.shape
    sample = conv_state is not None
    tiles_per_seq = max(seq_len // tr, 1)
    kscale = DH ** -0.5

    def body(*refs):
        it = iter(refs)
        x_ref = next(it)
        if sample:
            s0_ref, s1_ref, s2_ref = next(it), next(it), next(it)
        else:
            xh_ref = next(it)
        cw_ref, cb_ref, wq_ref, wk_ref, wv_ref, wg_ref, bg_ref = (next(it) for _ in range(7))
        q_ref, k_ref, v_ref, g_ref, xc_ref = (next(it) for _ in range(5))
        cw = cw_ref[...]
        x = x_ref[...].astype(f32)
        if sample:
            conv = cb_ref[...] + cw[0:1] * s0_ref[...] + cw[1:2] * s1_ref[...] + cw[2:3] * s2_ref[...] + cw[3:4] * x
        else:
            xs_scr = next(it)
            keep = (pl.program_id(0) % tiles_per_seq != 0).astype(f32)
            xs_scr[0:SUBLANES, :] = xh_ref[...].astype(f32)[HALO - SUBLANES:] * keep
            xs_scr[SUBLANES:, :] = x
            conv = (cb_ref[...] + cw[0:1] * xs_scr[SUBLANES - 3:SUBLANES - 3 + tr, :]
                    + cw[1:2] * xs_scr[SUBLANES - 2:SUBLANES - 2 + tr, :]
                    + cw[2:3] * xs_scr[SUBLANES - 1:SUBLANES - 1 + tr, :] + cw[3:4] * x)
        xc = conv * jax.nn.sigmoid(conv)
        xc_ref[...] = xc.astype(xc_ref.dtype)
        xcb = xc.astype(bf16)
        xb = x.astype(bf16)
        qs, ks, vs = [], [], []
        for t in range(NTL):
            cs = slice(t * TL, (t + 1) * TL)
            qs.append(_dot(xcb[:, cs], wq_ref[t]))
            ks.append(_dot(xcb[:, cs], wk_ref[t]) * kscale)
            vs.append(_dot(xb[:, cs], wv_ref[t]))
        q = jnp.concatenate(qs, axis=1)
        k = jnp.concatenate(ks, axis=1)
        v = jnp.concatenate(vs, axis=1)
        q_ref[...] = q.astype(q_ref.dtype)
        k_ref[...] = k.astype(k_ref.dtype)
        v_ref[...] = v.astype(v_ref.dtype)
        gt = (_dot(q.astype(bf16), wg_ref[0:W, :]) + _dot(k.astype(bf16), wg_ref[W:2 * W, :])
              + _dot(v.astype(bf16), wg_ref[2 * W:3 * W, :]) + bg_ref[...])
        lane = lax.broadcasted_iota(jnp.int32, gt.shape, 1)
        g_ref[...] = jnp.where(jnp.logical_and(lane >= H, lane < 2 * H), jax.nn.log_sigmoid(gt), gt)

    row = lambda i: (i, 0)
    const2 = lambda i: (0, 0)
    const3 = lambda i: (0, 0, 0)
    in_specs = [pl.BlockSpec((tr, W), row)]
    args = [p1]
    if sample:
        in_specs += [pl.BlockSpec((tr, W), row)] * 3
        args += [conv_state[:, 0], conv_state[:, 1], conv_state[:, 2]]
    else:
        in_specs.append(pl.BlockSpec((HALO, W), lambda i: (jnp.maximum(i * (tr // HALO) - 1, 0), 0)))
        args.append(p1)
    in_specs += [pl.BlockSpec((4, W), const2), pl.BlockSpec((1, W), const2),
                 pl.BlockSpec((NTL, TL, TL), const3), pl.BlockSpec((NTL, TL, TL), const3),
                 pl.BlockSpec((NTL, TL, TL), const3),
                 pl.BlockSpec((3 * W, LANES), const2), pl.BlockSpec((1, LANES), const2)]
    args += [conv_w, conv_b.reshape(1, W), wq_t, wk_t, wv_t, wg, bg]
    return pl.pallas_call(
        body,
        grid=(M // tr,),
        in_specs=in_specs,
        out_specs=[pl.BlockSpec((tr, W), row)] * 3 + [pl.BlockSpec((tr, LANES), row), pl.BlockSpec((tr, W), row)],
        out_shape=[jax.ShapeDtypeStruct((M, W), f32 if sample else bf16)] * 3
        + [jax.ShapeDtypeStruct((M, LANES), f32), jax.ShapeDtypeStruct((M, W), f32 if sample else bf16)],
        scratch_shapes=[] if sample else [pltpu.VMEM((tr + SUBLANES, W), f32)],
        compiler_params=_params(("parallel",), 48),
        name="mlstm_stage1_sample" if sample else "mlstm_stage1",
    )(*args)


def _mlstm_prompt(q, k, v, gates, xc, p1, g_norm, skip, *, B, L, H, DH, tb):
    NT = L // tb
    NC = tb // CHUNK
    C = CHUNK

    def body(q_ref, k_ref, v_ref, g_ref, xc_ref, om_ref, gn_ref, sk_ref,
             y_ref, c_out, n_out, m_out, c_scr, n_scr, m_scr):
        hh = pl.program_id(1)
        t = pl.program_id(2)

        @pl.when(t == 0)
        def _():
            c_scr[...] = jnp.zeros_like(c_scr)
            n_scr[...] = jnp.zeros_like(n_scr)
            m_scr[...] = jnp.zeros_like(m_scr)

        gts = g_ref[...]
        lane = lax.broadcasted_iota(jnp.int32, (tb, LANES), 1)
        i_col = jnp.sum(jnp.where(lane == hh, gts, 0.0), axis=1, keepdims=True)
        b_col = jnp.sum(jnp.where(lane == H + hh, _chunk_cumsum(gts, C), 0.0), axis=1, keepdims=True)
        b3 = b_col.reshape(NC, C, 1)
        i3 = i_col.reshape(NC, C, 1)
        ii = lax.broadcasted_iota(jnp.int32, (1, C, C), 1)
        jj = lax.broadcasted_iota(jnp.int32, (1, C, C), 2)
        eye = ii == jj
        causal = ii >= jj
        as_row = lambda col3: jnp.sum(jnp.where(eye, jnp.broadcast_to(col3, (NC, C, C)), 0.0), axis=1, keepdims=True)
        dmat = jnp.where(causal, b3 - as_row(b3) + as_row(i3), -jnp.inf)
        rmax = jnp.max(dmat, axis=2, keepdims=True)
        b_last = b3[:, C - 1:C, :]
        m_prev = m_scr[...]
        m_in = []
        for c in range(NC):
            m_in.append(m_prev)
            m_prev = jnp.maximum(b_last[c] + m_prev, rmax[c][C - 1:C, :])
        m_scr[...] = m_prev
        inter = b3 + jnp.stack(m_in)
        m_col = jnp.maximum(inter, rmax)
        g_col = jnp.exp(inter - m_col)
        q3 = q_ref[...].reshape(NC, C, DH)
        k3 = k_ref[...].reshape(NC, C, DH)
        v3 = v_ref[...].reshape(NC, C, DH)
        s = jnp.einsum('cid,cjd->cij', q3, k3, preferred_element_type=f32) * jnp.exp(dmat - m_col)
        num = jnp.einsum('cij,cje->cie', s.astype(bf16), v3, preferred_element_type=f32)
        den = jnp.sum(s, axis=2, keepdims=True)
        m_new = m_col[:, C - 1:C, :]
        wk = jnp.exp(b_last - b3 + i3 - m_new)
        gc = jnp.exp(inter[:, C - 1:C, :] - m_new)
        kw = k3.astype(f32) * wk
        dc = jnp.einsum('cse,csd->ced', v3, kw.astype(bf16), preferred_element_type=f32)
        dn = jnp.sum(kw, axis=1, keepdims=True)
        cs = c_scr[...]
        ns = n_scr[...]
        c_in, n_in = [], []
        for c in range(NC):
            c_in.append(cs)
            n_in.append(ns)
            cs = gc[c] * cs + dc[c]
            ns = gc[c] * ns + dn[c]
        c_scr[...] = cs
        n_scr[...] = ns
        c_all = jnp.stack(c_in).astype(bf16)
        n_all = jnp.stack(n_in)
        num = num + g_col * jnp.einsum('cid,ced->cie', q3, c_all, preferred_element_type=f32)
        den = den + g_col * jnp.sum(q3.astype(f32) * n_all, axis=2, keepdims=True)
        hm = (num / jnp.maximum(jnp.abs(den), jnp.exp(-m_col))).reshape(tb, DH)
        y = (_rms(hm, gn_ref[...]) + sk_ref[...] * xc_ref[...].astype(f32)) * jax.nn.sigmoid(om_ref[...].astype(f32))
        y_ref[...] = y.astype(bf16)

        @pl.when(t == NT - 1)
        def _():
            c_out[0, 0] = cs
            n_out[0, 0] = ns
            m_out[0, 0] = m_prev

    blk = lambda b, h, t: (b * NT + t, h)
    W = H * DH
    y, c_new, n_new, m_new = pl.pallas_call(
        body,
        grid=(B, H, NT),
        in_specs=[pl.BlockSpec((tb, DH), blk), pl.BlockSpec((tb, DH), blk), pl.BlockSpec((tb, DH), blk),
                  pl.BlockSpec((tb, LANES), lambda b, h, t: (b * NT + t, 0)),
                  pl.BlockSpec((tb, DH), blk),
                  pl.BlockSpec((tb, DH), lambda b, h, t: (b * NT + t, H + h)),
                  pl.BlockSpec((1, DH), lambda b, h, t: (0, h)),
                  pl.BlockSpec((1, DH), lambda b, h, t: (0, h))],
        out_specs=[pl.BlockSpec((tb, DH), blk),
                   pl.BlockSpec((1, 1, DH, DH), lambda b, h, t: (b, h, 0, 0)),
                   pl.BlockSpec((1, 1, 1, DH), lambda b, h, t: (b, h, 0, 0)),
                   pl.BlockSpec((1, 1, 1, 1), lambda b, h, t: (b, h, 0, 0))],
        out_shape=[jax.ShapeDtypeStruct((B * L, W), bf16),
                   jax.ShapeDtypeStruct((B, H, DH, DH), f32),
                   jax.ShapeDtypeStruct((B, H, 1, DH), f32),
                   jax.ShapeDtypeStruct((B, H, 1, 1), f32)],
        scratch_shapes=[pltpu.VMEM((DH, DH), f32), pltpu.VMEM((1, DH), f32), pltpu.VMEM((1, 1), f32)],
        compiler_params=_params(("parallel", "parallel", "arbitrary"), 32),
        name="mlstm_prompt",
    )(q, k, v, gates, xc, p1, g_norm.reshape(1, W), skip.reshape(1, W))
    return y, c_new, n_new.reshape(B, H, DH), m_new.reshape(B, H)


def _mlstm_sample(q, k, v, gates, xc, p1, g_norm, skip, c0, n0, m0, *, H, DH):
    Bs = q.shape[0]
    W = H * DH
    SB = SUBLANES

    def body(q_ref, k_ref, v_ref, g_ref, xc_ref, om_ref, gn_ref, sk_ref, c_ref, n_ref, m_ref,
             y_ref, c_out, n_out, m_out):
        gts = g_ref[...]
        for h in range(H):
            cs = slice(h * DH, (h + 1) * DH)
            ig = gts[:, h:h + 1]
            fg = gts[:, H + h:H + h + 1]
            inter = fg + m_ref[:, h:h + 1]
            m = jnp.maximum(inter, ig)
            g = jnp.exp(inter - m)
            m_out[:, h:h + 1] = m
            q = q_ref[:, cs]
            kw = k_ref[:, cs] * jnp.exp(ig - m)
            nn = g * n_ref[:, cs] + kw
            n_out[:, cs] = nn
            den = jnp.sum(nn * q, axis=1, keepdims=True)
            v_t = v_ref[:, cs].T
            qb = q.astype(bf16)
            nums = []
            for s in range(SB):
                cn = g[s:s + 1, :] * c_ref[s, h] + v_t[:, s:s + 1] * kw[s:s + 1, :]
                c_out[s, h] = cn
                nums.append(_dot_nt(qb, cn.astype(bf16))[s:s + 1, :])
            num = jnp.concatenate(nums, axis=0)
            hm = num / jnp.maximum(jnp.abs(den), jnp.exp(-m))
            y_ref[:, cs] = ((_rms(hm, gn_ref[:, cs]) + sk_ref[:, cs] * xc_ref[:, cs])
                            * jax.nn.sigmoid(om_ref[:, cs]))

    per = lambda b: (b, 0)
    const2 = lambda b: (0, 0)
    y, c_new, n_new, m_new = pl.pallas_call(
        body,
        grid=(Bs // SB,),
        in_specs=[pl.BlockSpec((SB, W), per), pl.BlockSpec((SB, W), per), pl.BlockSpec((SB, W), per),
                  pl.BlockSpec((SB, LANES), per), pl.BlockSpec((SB, W), per),
                  pl.BlockSpec((SB, W), lambda b: (b, 1)),
                  pl.BlockSpec((1, W), const2), pl.BlockSpec((1, W), const2),
                  pl.BlockSpec((SB, H, DH, DH), lambda b: (b, 0, 0, 0)),
                  pl.BlockSpec((SB, W), per), pl.BlockSpec((SB, H), per)],
        out_specs=[pl.BlockSpec((SB, W), per),
                   pl.BlockSpec((SB, H, DH, DH), lambda b: (b, 0, 0, 0)),
                   pl.BlockSpec((SB, W), per), pl.BlockSpec((SB, H), per)],
        out_shape=[jax.ShapeDtypeStruct((Bs, W), f32), jax.ShapeDtypeStruct((Bs, H, DH, DH), f32),
                   jax.ShapeDtypeStruct((Bs, W), f32), jax.ShapeDtypeStruct((Bs, H), f32)],
        compiler_params=_params(("parallel",), 48),
        name="mlstm_sample",
    )(q, k, v, gates, xc, p1, g_norm.reshape(1, W), skip.reshape(1, W), c0, n0.reshape(Bs, W), m0)
    return y, c_new, n_new.reshape(Bs, H, DH), m_new


S5_TILES = 8


def _s5_layouts(lam_re, lam_im, log_dt, b_re, b_im, c_re, c_im):
    G, P = lam_re.shape
    GC = b_re.shape[2]
    T = S5_TILES
    gpt = G // T
    ns = G * P
    flat = lambda a: a.reshape(ns)
    ldt = jnp.broadcast_to(log_dt[:, None], (G, P))
    rows = [flat(a).reshape(T, 1, ns // T) for a in (lam_re, lam_im, ldt)]
    eye = jnp.eye(gpt, dtype=f32)
    bbd = [jnp.einsum('jgpc,gh->jgchp', a.reshape(T, gpt, P, GC), eye).reshape(T, gpt * GC, gpt * P) for a in (b_re, b_im)]
    cbd = [jnp.einsum('jgcp,gh->jgphc', a.reshape(T, gpt, GC, P), eye).reshape(T, gpt * P, gpt * GC) for a in (c_re, c_im)]
    return rows, bbd, cbd


def _s5_discretise(lre, lim, ldt):
    dt = jnp.exp(ldt)
    mag = jnp.exp(dt * lre)
    ar = mag * jnp.cos(dt * lim)
    ai = mag * jnp.sin(dt * lim)
    den = lre * lre + lim * lim
    cr = ((ar - 1.0) * lre + ai * lim) / den
    ci = (ai * lre - (ar - 1.0) * lim) / den
    return ar, ai, cr, ci


def _s5_mixer(p1, u_blk, rows, bbd, cbd, d_skip, w_glu, b_glu, *, B, L, tb, state=None):
    T = S5_TILES
    lre_r, lim_r, ldt_r = rows
    SW = lre_r.shape[2]
    CW = bbd[0].shape[1]
    W = T * CW
    NS = T * SW
    KT = SW // LANES
    sample = state is not None
    NT = 1 if sample else L // tb
    M = B * L

    def body(*refs):
        it = iter(refs)
        u_ref = next(it)
        if sample:
            x0r_ref, x0i_ref = next(it), next(it)
        lre_ref, lim_ref, ldt_ref = next(it), next(it), next(it)
        bre_ref, bim_ref, cre_ref, cim_ref = next(it), next(it), next(it), next(it)
        d_ref, wg_ref, bgl_ref = next(it), next(it), next(it)
        y_ref, xr_out, xi_out = next(it), next(it), next(it)
        bbr, bbi, cpair, ar_scr, ai_scr = (next(it) for _ in range(5))
        if not sample:
            sre, sim, xr_c, xi_c, yacc = (next(it) for _ in range(5))
        first = jnp.logical_and(pl.program_id(0) == 0, pl.program_id(1) == 0)

        @pl.when(first)
        def _():
            for j in range(T):
                ar, ai, cr, ci = _s5_discretise(lre_ref[j], lim_ref[j], ldt_ref[j])
                ar_scr[j] = ar
                ai_scr[j] = ai
                br = bre_ref[j]
                bi = bim_ref[j]
                bbr[j] = (cr * br - ci * bi).astype(bf16)
                bbi[j] = (cr * bi + ci * br).astype(bf16)
            for jp in range(T // 2):
                cpair[jp] = jnp.zeros((4 * SW, 2 * CW), bf16)
                for half in range(2):
                    j = 2 * jp + half
                    r0 = 2 * half * SW
                    cols = slice(half * CW, (half + 1) * CW)
                    cpair[jp, r0:r0 + SW, cols] = cre_ref[j].astype(bf16)
                    cpair[jp, r0 + SW:r0 + 2 * SW, cols] = (-cim_ref[j]).astype(bf16)

        u = u_ref[...].astype(f32)
        ub = u_ref[...].astype(bf16)
        ys = []
        if sample:
            for j in range(T):
                cs = slice(j * SW, (j + 1) * SW)
                uj = ub[:, j * CW:(j + 1) * CW]
                ar = ar_scr[j]
                ai = ai_scr[j]
                x0r = x0r_ref[:, cs]
                x0i = x0i_ref[:, cs]
                xr = ar * x0r - ai * x0i + _dot(uj, bbr[j])
                xi = ar * x0i + ai * x0r + _dot(uj, bbi[j])
                xr_out[:, cs] = xr
                xi_out[:, cs] = xi
                ys += [xr.astype(bf16), xi.astype(bf16)]
            y = jnp.concatenate([_dot(jnp.concatenate(ys[4 * jp:4 * jp + 4], axis=1), cpair[jp]) for jp in range(T // 2)],
                                axis=1)
        else:
            t = pl.program_id(1)

            @pl.when(t == 0)
            def _():
                xr_c[...] = jnp.zeros_like(xr_c)
                xi_c[...] = jnp.zeros_like(xi_c)

            for j in range(T):
                uj = ub[:, j * CW:(j + 1) * CW]
                r = _dot(uj, bbr[j])
                im = _dot(uj, bbi[j])
                for kk in range(KT):
                    sre[kk, pl.ds(j, tb, stride=T), :] = r[:, kk * LANES:(kk + 1) * LANES]
                    sim[kk, pl.ds(j, tb, stride=T), :] = im[:, kk * LANES:(kk + 1) * LANES]
            a_r = [jnp.concatenate([ar_scr[j][:, kk * LANES:(kk + 1) * LANES] for j in range(T)], axis=0) for kk in range(KT)]
            a_i = [jnp.concatenate([ai_scr[j][:, kk * LANES:(kk + 1) * LANES] for j in range(T)], axis=0) for kk in range(KT)]

            def step(s, carry):
                xr, xi = carry
                row = pl.ds(pl.multiple_of(s * T, T), T)
                nr, ni = [], []
                for kk in range(KT):
                    r_ = a_r[kk] * xr[kk] - a_i[kk] * xi[kk] + sre[kk, row, :]
                    i_ = a_r[kk] * xi[kk] + a_i[kk] * xr[kk] + sim[kk, row, :]
                    sre[kk, row, :] = r_
                    sim[kk, row, :] = i_
                    nr.append(r_)
                    ni.append(i_)
                return tuple(nr), tuple(ni)
            xr0 = tuple(xr_c[kk] for kk in range(KT))
            xi0 = tuple(xi_c[kk] for kk in range(KT))
            xr, xi = lax.fori_loop(0, tb, step, (xr0, xi0), unroll=8)
            for kk in range(KT):
                xr_c[kk] = xr[kk]
                xi_c[kk] = xi[kk]

            @pl.when(t == NT - 1)
            def _():
                for kk in range(KT):
                    xr_out[kk] = xr[kk]
                    xi_out[kk] = xi[kk]

            for jp in range(T // 2):
                parts = []
                for j in (2 * jp, 2 * jp + 1):
                    parts += [sre[kk, pl.ds(j, tb, stride=T), :] for kk in range(KT)]
                    parts += [sim[kk, pl.ds(j, tb, stride=T), :] for kk in range(KT)]
                xp = jnp.concatenate(parts, axis=1).astype(bf16)
                yacc[:, 2 * jp * CW:(2 * jp + 2) * CW] = _dot(xp, cpair[jp])
            y = yacc[...]
        ysk = jax.nn.gelu(y + d_ref[...] * u)
        z = _dot(ysk.astype(bf16), wg_ref[...]) + bgl_ref[...]
        y_ref[...] = (ysk * jax.nn.sigmoid(z)).astype(bf16)

    c3 = lambda b, t: (0, 0, 0)
    c2 = lambda b, t: (0, 0)
    in_specs = [pl.BlockSpec((tb, W), lambda b, t: (b * NT + t, u_blk))]
    args = [p1]
    if sample:
        in_specs += [pl.BlockSpec((tb, NS), lambda b, t: (b, 0))] * 2
        args += [state[0], state[1]]
    in_specs += [pl.BlockSpec((T, 1, SW), c3)] * 3
    in_specs += [pl.BlockSpec((T, CW, SW), c3)] * 2 + [pl.BlockSpec((T, SW, CW), c3)] * 2
    in_specs += [pl.BlockSpec((1, W), c2), pl.BlockSpec((W, W), c2), pl.BlockSpec((1, W), c2)]
    args += [lre_r, lim_r, ldt_r, bbd[0], bbd[1], cbd[0], cbd[1], d_skip.reshape(1, W), w_glu, b_glu.reshape(1, W)]
    scratch = [pltpu.VMEM((T, CW, SW), bf16), pltpu.VMEM((T, CW, SW), bf16),
               pltpu.VMEM((T // 2, 4 * SW, 2 * CW), bf16),
               pltpu.VMEM((T, 1, SW), f32), pltpu.VMEM((T, 1, SW), f32)]
    if sample:
        grid = (M // tb, 1)
        st_spec = pl.BlockSpec((tb, NS), lambda b, t: (b, 0))
        st_shape = jax.ShapeDtypeStruct((M, NS), f32)
    else:
        grid = (B, NT)
        st_spec = pl.BlockSpec((None, KT, T, LANES), lambda b, t: (b, 0, 0, 0))
        st_shape = jax.ShapeDtypeStruct((B, KT, T, LANES), f32)
        scratch += [pltpu.VMEM((KT, tb * T, LANES), f32), pltpu.VMEM((KT, tb * T, LANES), f32),
                    pltpu.VMEM((KT, T, LANES), f32), pltpu.VMEM((KT, T, LANES), f32), pltpu.VMEM((tb, W), f32)]
    y, xr, xi = pl.pallas_call(
        body,
        grid=grid,
        in_specs=in_specs,
        out_specs=[pl.BlockSpec((tb, W), lambda b, t: (b * NT + t, 0)), st_spec, st_spec],
        out_shape=[jax.ShapeDtypeStruct((M, W), bf16), st_shape, st_shape],
        scratch_shapes=scratch,
        compiler_params=_params(("arbitrary", "arbitrary"), 56),
        name="s5_sample" if sample else "s5_prompt",
    )(*args)
    if not sample:
        xr = jnp.transpose(xr, (0, 2, 1, 3)).reshape(B, NS)
        xi = jnp.transpose(xi, (0, 2, 1, 3)).reshape(B, NS)
    return y, xr, xi


def _trunk(x3, st, w, *, sample):
    B, L, D = x3.shape
    M = B * L
    x = x3.reshape(M, D)
    H_g, DK, DV = st['gla_S'].shape[1:] if sample else w['gla_dims']
    H_m, DH = w['ml_dims']
    G, P = w['s5_dims']
    tm = M if sample else min(1024, L)
    tf = M if sample else min(1024, L)
    to = M if sample else min(2048, M)
    tb = min(2048, L)
    ts = min(512, L)
    tr = M if sample else min(512, L)
    tn_in, tn_out = 1024, 512
    pdt = f32 if sample else bf16
    out = {}

    p0, alr = _norm_matmul(x, w['g_mix0'], w['w_in0_main'], tm=tm, tn=tn_in, w_side=w['w_in0_alr'], out_dtype=pdt, w_t=True,
                            seg_cols=w['w_in0_cols'])
    x_blk = (2 * H_g * DK + 2 * H_g * DV) // LANES
    W_rg = w['rg_lambda'].shape[0]
    g_blk = x_blk + W_rg // LANES
    rg_tail = p0.reshape(B, L, p0.shape[1])[:, max(L - 3, 0):, x_blk * LANES:x_blk * LANES + W_rg].astype(f32)
    if sample:
        ya, out['gla_S'] = _gla_sample(p0, alr, w['gla_w2p'], w['gla_b_alpha'], w['gla_g_norm'], st['gla_S'],
                                       H=H_g, DK=DK, DV=DV)
        yb, out['rg_h'] = _rglru_sample(p0, st['rg_conv'], st['rg_h'], w['rg_conv_w'], w['rg_conv_b'], w['rg_w_r'],
                                        w['rg_b_r'], w['rg_w_i'], w['rg_b_i'], w['rg_lambda'], x_blk=x_blk, g_blk=g_blk)
        out['rg_conv'] = jnp.concatenate([st['rg_conv'][:, 1:], rg_tail], axis=1)
    else:
        ya, out['gla_S'] = _gla_prompt(p0, alr, w['gla_w2p'], w['gla_b_alpha'], w['gla_g_norm'],
                                       B=B, L=L, H=H_g, DK=DK, DV=DV, tb=tb)
        yb, out['rg_h'] = _rglru_prompt(p0, w['rg_conv_w'], w['rg_conv_b'], w['rg_w_r'], w['rg_b_r'], w['rg_w_i'],
                                        w['rg_b_i'], w['rg_lambda'], B=B, L=L, x_blk=x_blk, g_blk=g_blk)
        out['rg_conv'] = rg_tail
    x = _out_proj(ya, yb, w['w_out0'], x, tm=to, tn=tn_out)
    x, ffn0 = _conv_ffn(x, w['g_ffn'][0], w['ffn'][0], seq_len=L, tm=tf, prev=st['ffn_conv'] if sample else None)

    W_ml = H_m * DH
    p1 = _norm_matmul(x, w['g_mix1'], [w['w_in1']], tm=tm, tn=tn_in, out_dtype=pdt)
    xm_tail = p1.reshape(B, L, p1.shape[1])[:, max(L - 3, 0):, :W_ml].astype(f32)
    q, k, v, gates, xc = _mlstm_stage1(p1, w['ml_conv_w'], w['ml_conv_b'], w['ml_wq_t'], w['ml_wk_t'], w['ml_wv_t'],
                                       w['ml_wg'], w['ml_bg'], seq_len=L, tr=tr, H=H_m, DH=DH,
                                       conv_state=st['ml_conv'] if sample else None)
    u_blk = 2 * W_ml // (G * w['s5_gc'])
    if sample:
        yc, out['ml_C'], out['ml_n'], out['ml_m'] = _mlstm_sample(
            q, k, v, gates, xc, p1, w['ml_g_norm'], w['ml_skip'], st['ml_C'], st['ml_n'], st['ml_m'], H=H_m, DH=DH)
        out['ml_conv'] = jnp.concatenate([st['ml_conv'][:, 1:], xm_tail], axis=1)
        yd, s5r, s5i = _s5_mixer(p1, u_blk, w['s5_rows'], w['s5_bbd'], w['s5_cbd'], w['s5_D'], w['s5_w_glu'],
                                 w['s5_b_glu'], B=B, L=1, tb=B,
                                 state=(st['s5_re'].reshape(B, G * P), st['s5_im'].reshape(B, G * P)))
    else:
        yc, out['ml_C'], out['ml_n'], out['ml_m'] = _mlstm_prompt(
            q, k, v, gates, xc, p1, w['ml_g_norm'], w['ml_skip'], B=B, L=L, H=H_m, DH=DH, tb=tb)
        out['ml_conv'] = xm_tail
        yd, s5r, s5i = _s5_mixer(p1, u_blk, w['s5_rows'], w['s5_bbd'], w['s5_cbd'], w['s5_D'], w['s5_w_glu'],
                                 w['s5_b_glu'], B=B, L=L, tb=ts)
    out['s5_re'] = s5r.reshape(B, G, P)
    out['s5_im'] = s5i.reshape(B, G, P)
    x = _out_proj(yc, yd, w['w_out1'], x, tm=to, tn=tn_out)
    x, ffn1 = _conv_ffn(x, w['g_ffn'][1], w['ffn'][1], seq_len=L, tm=tf, prev=st['ffn_conv'] if sample else None,
                        final_g=w['g_final'])
    out['ffn_conv'] = jnp.stack([ffn0, ffn1], axis=0)
    return x.reshape(B, L, D), out


def kernel(x_prompt, x_sample, state_gla_S, state_rglru_h, state_rglru_conv, state_mlstm_C, state_mlstm_n, state_mlstm_m, state_mlstm_conv, state_s5_re, state_s5_im, state_ffn_conv, g_mix0, w_in0, gla_w_alpha2, gla_b_alpha, gla_g_norm, rg_conv_w, rg_conv_b, rg_w_r, rg_b_r, rg_w_i, rg_b_i, rg_lambda, w_out0, g_mix1, w_in1, ml_conv_w, ml_conv_b, ml_wq, ml_wk, ml_wv, ml_w_igate, ml_b_igate, ml_w_fgate, ml_b_fgate, ml_g_norm, ml_skip, s5_lam_re, s5_lam_im, s5_log_dt, s5_B_re, s5_B_im, s5_C_re, s5_C_im, s5_D, s5_w_glu, s5_b_glu, w_out1, g_ffn, ffn_w_up, ffn_conv_w, ffn_conv_b, ffn_w_down, g_final):
    _, H_g, DK, DV = state_gla_S.shape
    _, H_m, DH, _ = state_mlstm_C.shape
    G, P = s5_lam_re.shape
    rank = gla_w_alpha2.shape[0]
    n_main = 2 * H_g * DK + 2 * H_g * DV
    w_in0_t = w_in0.T.astype(bf16)
    w_in0_main = [w_in0_t, w_in0_t[n_main + rank:]]
    w_in0_cols = [n_main, w_in0.shape[1] - n_main - rank]
    w_in0_alr = jnp.pad(w_in0_t[n_main:n_main + rank], ((0, LANES - rank), (0, 0)))
    gla_w2p = jnp.pad(gla_w_alpha2, ((0, LANES - rank), (0, 0)))
    ml_tile = 256
    ml_wg = jnp.pad(jnp.concatenate([ml_w_igate, ml_w_fgate], axis=1), ((0, 0), (0, LANES - 2 * H_m))).astype(bf16)
    ml_bg = jnp.pad(jnp.concatenate([ml_b_igate, ml_b_fgate]), (0, LANES - 2 * H_m)).reshape(1, LANES)
    rows, bbd, cbd = _s5_layouts(s5_lam_re, s5_lam_im, s5_log_dt, s5_B_re, s5_B_im, s5_C_re, s5_C_im)
    w = dict(
        g_mix0=g_mix0, w_in0_main=w_in0_main, w_in0_cols=w_in0_cols, w_in0_alr=w_in0_alr, gla_w2p=gla_w2p, gla_b_alpha=gla_b_alpha,
        gla_g_norm=gla_g_norm, gla_dims=(H_g, DK, DV), rg_conv_w=rg_conv_w, rg_conv_b=rg_conv_b, rg_w_r=rg_w_r,
        rg_b_r=rg_b_r, rg_w_i=rg_w_i, rg_b_i=rg_b_i, rg_lambda=rg_lambda, w_out0=w_out0.astype(bf16),
        g_mix1=g_mix1, w_in1=w_in1.astype(bf16), ml_conv_w=ml_conv_w, ml_conv_b=ml_conv_b,
        ml_wq_t=_blockdiag_tiles(ml_wq, ml_tile).astype(bf16), ml_wk_t=_blockdiag_tiles(ml_wk, ml_tile).astype(bf16),
        ml_wv_t=_blockdiag_tiles(ml_wv, ml_tile).astype(bf16), ml_wg=ml_wg, ml_bg=ml_bg, ml_g_norm=ml_g_norm,
        ml_skip=ml_skip, ml_dims=(H_m, DH), s5_dims=(G, P), s5_gc=s5_B_re.shape[2], s5_rows=rows, s5_bbd=bbd,
        s5_cbd=cbd, s5_D=s5_D, s5_w_glu=s5_w_glu.astype(bf16), s5_b_glu=s5_b_glu, w_out1=w_out1.astype(bf16),
        g_ffn=g_ffn, g_final=g_final,
        ffn=_ffn_prepare(ffn_w_up, ffn_conv_w, ffn_conv_b, ffn_w_down))
    st_s = dict(gla_S=state_gla_S, rg_h=state_rglru_h, rg_conv=state_rglru_conv, ml_C=state_mlstm_C,
                ml_n=state_mlstm_n, ml_m=state_mlstm_m, ml_conv=state_mlstm_conv, s5_re=state_s5_re,
                s5_im=state_s5_im, ffn_conv=state_ffn_conv)
    y_p, np_ = _trunk(x_prompt, None, w, sample=False)
    y_s, ns_ = _trunk(x_sample, st_s, w, sample=True)
    names = ('gla_S', 'rg_h', 'rg_conv', 'ml_C', 'ml_n', 'ml_m', 'ml_conv', 's5_re', 's5_im', 'ffn_conv')
    outs = [y_p, y_s]
    for nme in names:
        outs += [np_[nme], ns_[nme]]
    return tuple(outs)
```

```python
import jax
import jax.numpy as jnp
from jax import lax
from jax.experimental import pallas as pl
from jax.experimental.pallas import tpu as pltpu

f32 = jnp.float32
bf16 = jnp.bfloat16

EPS = 1e-6
CHUNK = 64
GLA_TAU = 16.0
RG_C = 8.0
PAST_LEN = 16384
LANES = 128
SUBLANES = 8
HALO = 16
MIB = 1024 * 1024


def _params(sem, vmem_mib):
    return pltpu.CompilerParams(dimension_semantics=sem, vmem_limit_bytes=int(vmem_mib * MIB))


def _dot(a, b):
    return jnp.dot(a, b, preferred_element_type=f32)


def _dot_nt(a, b):
    return lax.dot_general(a, b, (((1,), (1,)), ((), ())), preferred_element_type=f32)


def _dot_tn(a, b):
    return lax.dot_general(a, b, (((0,), (0,)), ((), ())), preferred_element_type=f32)


def _rms(x, g):
    return x * lax.rsqrt(jnp.mean(x * x, axis=-1, keepdims=True) + EPS) * g


def _eye(n):
    return lax.broadcasted_iota(jnp.int32, (n, n), 0) == lax.broadcasted_iota(jnp.int32, (n, n), 1)


def _col_from_row(row):
    n = row.shape[1]
    return jnp.sum(jnp.where(_eye(n), jnp.broadcast_to(row, (n, n)), 0.0), axis=1, keepdims=True)


def _chunk_cumsum(x, chunk):
    pos = lax.broadcasted_iota(jnp.int32, (x.shape[0], 1), 0) % chunk
    step = 1
    while step < chunk:
        x = x + jnp.where(pos >= step, pltpu.roll(x, step, 0), 0.0)
        step *= 2
    return x


def _norm_matmul(x, g, ws, *, tm, tn, w_side=None, out_dtype=f32, w_t=False, seg_cols=None):
    M, D = x.shape
    mm = _dot_nt if w_t else _dot
    seg_cols = seg_cols or [w.shape[0 if w_t else 1] for w in ws]
    nblk = [c // tn for c in seg_cols]
    start = [sum(nblk[:s]) for s in range(len(ws) + 1)]
    N = tn * start[-1]
    rc = min(tm, 256)
    side = w_side is not None

    def body(*refs):
        x_ref, g_ref = refs[:2]
        w_refs = refs[2:2 + len(ws)]
        rest = refs[2 + len(ws):]
        if side:
            ws_ref, o_ref, os_ref, xn_ref = rest
        else:
            o_ref, xn_ref = rest
        j = pl.program_id(1)

        @pl.when(j == 0)
        def _():
            def chunk(r, c):
                rows = pl.ds(pl.multiple_of(r * rc, rc), rc)
                xn_ref[rows, :] = _rms(x_ref[rows, :], g_ref[...]).astype(bf16)
                return c
            lax.fori_loop(0, tm // rc, chunk, 0)
            if side:
                os_ref[...] = mm(xn_ref[...], ws_ref[...])

        if len(ws) == 1:
            o_ref[...] = mm(xn_ref[...], w_refs[0][...]).astype(out_dtype)
        else:
            for s, w_ref in enumerate(w_refs):
                @pl.when(jnp.logical_and(j >= start[s], j < start[s + 1]))
                def _(w_ref=w_ref):
                    o_ref[...] = mm(xn_ref[...], w_ref[...]).astype(out_dtype)

    in_specs = [pl.BlockSpec((tm, D), lambda i, j: (i, 0), pipeline_mode=pl.Buffered(1)),
                pl.BlockSpec((1, D), lambda i, j: (0, 0))]
    wblk = lambda j, s: jnp.clip(j - start[s], 0, nblk[s] - 1)
    if w_t:
        in_specs += [pl.BlockSpec((tn, D), lambda i, j, s=s: (wblk(j, s), 0)) for s in range(len(ws))]
    else:
        in_specs += [pl.BlockSpec((D, tn), lambda i, j, s=s: (0, wblk(j, s))) for s in range(len(ws))]
    out_specs = [pl.BlockSpec((tm, tn), lambda i, j: (i, j))]
    out_shape = [jax.ShapeDtypeStruct((M, N), out_dtype)]
    args = [x, g.reshape(1, D)] + list(ws)
    if side:
        ns = w_side.shape[0 if w_t else 1]
        in_specs.append(pl.BlockSpec((ns, D) if w_t else (D, ns), lambda i, j: (0, 0)))
        out_specs.append(pl.BlockSpec((tm, ns), lambda i, j: (i, 0)))
        out_shape.append(jax.ShapeDtypeStruct((M, ns), f32))
        args.append(w_side)
    outs = pl.pallas_call(
        body,
        grid=(M // tm, N // tn),
        in_specs=in_specs,
        out_specs=out_specs,
        out_shape=out_shape,
        scratch_shapes=[pltpu.VMEM((tm, D), bf16)],
        compiler_params=_params(("parallel", "arbitrary"), 56),
        name="norm_matmul",
    )(*args)
    return outs if side else outs[0]


def _out_proj(ya, yb, w, res, *, tm, tn):
    M, Ka = ya.shape
    Kb = yb.shape[1]
    N = w.shape[1]
    assert Ka == Kb and w.shape[0] == Ka + Kb

    def body(ya_ref, yb_ref, wa_ref, wb_ref, r_ref, o_ref):
        o_ref[...] = (r_ref[...] + _dot(ya_ref[...].astype(bf16), wa_ref[...])
                      + _dot(yb_ref[...].astype(bf16), wb_ref[...]))

    return pl.pallas_call(
        body,
        grid=(M // tm, N // tn),
        in_specs=[pl.BlockSpec((tm, Ka), lambda i, j: (i, 0)),
                  pl.BlockSpec((tm, Kb), lambda i, j: (i, 0)),
                  pl.BlockSpec((Ka, tn), lambda i, j: (0, j)),
                  pl.BlockSpec((Kb, tn), lambda i, j: (1, j)),
                  pl.BlockSpec((tm, tn), lambda i, j: (i, j))],
        out_specs=pl.BlockSpec((tm, tn), lambda i, j: (i, j)),
        out_shape=jax.ShapeDtypeStruct((M, N), f32),
        compiler_params=_params(("parallel", "arbitrary"), 48),
        name="out_proj",
    )(ya, yb, w, w, res)


FFN_STEP = 512
FFN_ROW_CHUNK = 512


def _ffn_to_steps(a, F):
    nsteps = -(-F // FFN_STEP)
    lead = a.shape[:-1]
    nd = len(lead)
    gv = a.reshape(lead + (2, F))
    gv = jnp.pad(gv, [(0, 0)] * (nd + 1) + [(0, nsteps * FFN_STEP - F)])
    gv = gv.reshape(lead + (2, nsteps, FFN_STEP))
    gv = jnp.transpose(gv, (nd + 1,) + tuple(range(nd)) + (nd, nd + 2))
    return gv.reshape((nsteps,) + lead + (2 * FFN_STEP,))


def _ffn_prepare(w_up, conv_w, conv_b, w_down):
    NL, F, D = w_down.shape
    assert F % LANES == 0 and FFN_STEP % LANES == 0
    nt = F // LANES
    tp = FFN_STEP // LANES
    nsteps = -(-F // FFN_STEP)

    def body(*refs):
        g_in, v_in, d_in = refs[:tp], refs[tp:2 * tp], refs[2 * tp:3 * tp]
        wu_o, wd_o = refs[3 * tp:]
        for t in range(tp):
            keep = tp * pl.program_id(1) + t < nt
            cols = slice(t * LANES, (t + 1) * LANES)
            wu_o[:, cols] = jnp.where(keep, g_in[t][...], 0.0).astype(bf16)
            wu_o[:, FFN_STEP + t * LANES:FFN_STEP + (t + 1) * LANES] = jnp.where(keep, v_in[t][...], 0.0).astype(bf16)
            wd_o[cols, :] = jnp.where(keep, d_in[t][...], 0.0).astype(bf16)

    tile = lambda j, t: jnp.minimum(tp * j + t, nt - 1)
    in_specs = ([pl.BlockSpec((None, D, LANES), lambda l, j, t=t: (l, 0, tile(j, t))) for t in range(tp)]
                + [pl.BlockSpec((None, D, LANES), lambda l, j, t=t: (l, 0, nt + tile(j, t))) for t in range(tp)]
                + [pl.BlockSpec((None, LANES, D), lambda l, j, t=t: (l, tile(j, t), 0)) for t in range(tp)])
    wu, wd = pl.pallas_call(
        body,
        grid=(NL, nsteps),
        in_specs=in_specs,
        out_specs=[pl.BlockSpec((None, None, D, 2 * FFN_STEP), lambda l, j: (l, j, 0, 0)),
                   pl.BlockSpec((None, None, FFN_STEP, D), lambda l, j: (l, j, 0, 0))],
        out_shape=[jax.ShapeDtypeStruct((NL, nsteps, D, 2 * FFN_STEP), bf16),
                   jax.ShapeDtypeStruct((NL, nsteps, FFN_STEP, D), bf16)],
        compiler_params=_params(("parallel", "parallel"), 40),
        name="ffn_weight_layout",
    )(*([w_up] * (2 * tp) + [w_down] * tp))
    return [dict(wu=wu, wd=wd, layer=l, F=F, cw=_ffn_to_steps(conv_w[l], F),
                 cb=_ffn_to_steps(conv_b[l], F).reshape(nsteps, 1, 2 * FFN_STEP)) for l in range(NL)]


def _conv_ffn(x, g, fw, *, seq_len, tm, prev=None, final_g=None):
    M, D = x.shape
    F = fw['F']
    layer = fw['layer']
    nsteps = fw['wu'].shape[1]
    sample = prev is not None
    rc = min(tm, FFN_ROW_CHUNK)
    tiles_per_seq = max(seq_len // tm, 1)
    W2 = 2 * FFN_STEP

    def body(*refs):
        it = iter(refs)
        x_ref = next(it)
        xh_ref = None if sample else next(it)
        prev_refs = [next(it) for _ in range(2 * FFN_STEP // LANES)] if sample else None
        g_ref = next(it)
        wu_ref, cw_ref, cb_ref, wd_ref = next(it), next(it), next(it), next(it)
        fg_ref = next(it) if final_g is not None else None
        o_ref, tail_g_ref, tail_v_ref = next(it), next(it), next(it)
        xn_ref, up_scr = next(it), next(it)
        i = pl.program_id(0)
        j = pl.program_id(1)

        @pl.when(j == 0)
        def _():
            if sample:
                xn_ref[0:HALO, :] = jnp.zeros((HALO, D), bf16)
            else:
                keep = (i % tiles_per_seq != 0).astype(f32)
                hist = _rms(xh_ref[...], g_ref[...]) * keep
                xn_ref[0:HALO, :] = jnp.concatenate([jnp.zeros_like(hist), hist], axis=0).astype(bf16)

            def chunk(r, c):
                rows = pl.ds(pl.multiple_of(r * rc, rc), rc)
                xr = x_ref[rows, :]
                o_ref[rows, :] = xr
                xn_ref[pl.ds(pl.multiple_of(HALO + r * rc, HALO), rc), :] = _rms(xr, g_ref[...]).astype(bf16)
                return c
            lax.fori_loop(0, tm // rc, chunk, 0)

        cw = cw_ref[0]
        cb = cb_ref[0]
        nchunk = tm // rc
        for r in range(nchunk):
            if sample:
                up_scr[r, HALO:, :] = _dot(xn_ref[pl.ds(HALO + r * rc, rc), :], wu_ref[...])
            else:
                up_scr[r] = _dot(xn_ref[pl.ds(r * rc, rc + HALO), :], wu_ref[...])
        for r in range(nchunk):
            rows = pl.ds(r * rc, rc)
            up = up_scr[r, HALO:, :]
            if sample:
                p0 = jnp.concatenate([p[rows, 0, :] for p in prev_refs], axis=1)
                p1 = jnp.concatenate([p[rows, 1, :] for p in prev_refs], axis=1)
                conv = cb + cw[0:1] * p0 + cw[1:2] * p1 + cw[2:3] * up
                tail_g_ref[rows, 0, :] = p1[:, :FFN_STEP]
                tail_g_ref[rows, 1, :] = up[:, :FFN_STEP]
                tail_v_ref[rows, 0, :] = p1[:, FFN_STEP:]
                tail_v_ref[rows, 1, :] = up[:, FFN_STEP:]
            else:
                conv = (cb + cw[0:1] * up_scr[r, HALO - 2:HALO - 2 + rc, :]
                        + cw[1:2] * up_scr[r, HALO - 1:HALO - 1 + rc, :] + cw[2:3] * up)
                if r == nchunk - 1:
                    tail_g_ref[...] = up[rc - SUBLANES:, :FFN_STEP]
                    tail_v_ref[...] = up[rc - SUBLANES:, FFN_STEP:]
            h = jax.nn.gelu(conv[:, :FFN_STEP]) * conv[:, FFN_STEP:]
            o_ref[rows, :] += _dot(h.astype(bf16), wd_ref[...])

        if final_g is not None:
            @pl.when(j == nsteps - 1)
            def _():
                def chunk2(r, c):
                    rows = pl.ds(pl.multiple_of(r * rc, rc), rc)
                    o_ref[rows, :] = _rms(o_ref[rows, :], fg_ref[...])
                    return c
                lax.fori_loop(0, tm // rc, chunk2, 0)

    in_specs = [pl.BlockSpec((tm, D), lambda i, j: (i, 0))]
    args = [x]
    if sample:
        nt = F // LANES
        tp = FFN_STEP // LANES
        last = 2 * nt - 1
        for off in list(range(tp)) + [nt + t for t in range(tp)]:
            in_specs.append(pl.BlockSpec((None, tm, 2, LANES),
                                         lambda i, j, off=off: (layer, i, 0, jnp.minimum(tp * j + off, last))))
            args.append(prev)
    else:
        in_specs.append(pl.BlockSpec((SUBLANES, D), lambda i, j: (jnp.maximum(i * (tm // SUBLANES) - 1, 0), 0)))
        args.append(x)
    in_specs += [pl.BlockSpec((1, D), lambda i, j: (0, 0)),
                 pl.BlockSpec((None, None, D, W2), lambda i, j: (layer, j, 0, 0)),
                 pl.BlockSpec((1, 3, W2), lambda i, j: (j, 0, 0)),
                 pl.BlockSpec((1, 1, W2), lambda i, j: (j, 0, 0)),
                 pl.BlockSpec((None, None, FFN_STEP, D), lambda i, j: (layer, j, 0, 0))]
    args += [g.reshape(1, D), fw['wu'], fw['cw'], fw['cb'], fw['wd']]
    if final_g is not None:
        in_specs.append(pl.BlockSpec((1, D), lambda i, j: (0, 0)))
        args.append(final_g.reshape(1, D))
    FP = nsteps * FFN_STEP
    if sample:
        tail_shape = (M, 2, FP)
        tail_spec = pl.BlockSpec((tm, 2, FFN_STEP), lambda i, j: (i, 0, j))
    else:
        tail_shape = (M // tm, SUBLANES, FP)
        tail_spec = pl.BlockSpec((None, SUBLANES, FFN_STEP), lambda i, j: (i, 0, j))
    out, tail_g, tail_v = pl.pallas_call(
        body,
        grid=(M // tm, nsteps),
        in_specs=in_specs,
        out_specs=[pl.BlockSpec((tm, D), lambda i, j: (i, 0)), tail_spec, tail_spec],
        out_shape=[jax.ShapeDtypeStruct((M, D), f32), jax.ShapeDtypeStruct(tail_shape, f32),
                   jax.ShapeDtypeStruct(tail_shape, f32)],
        scratch_shapes=[pltpu.VMEM((tm + HALO, D), bf16), pltpu.VMEM((tm // rc, rc + HALO, W2), f32)],
        compiler_params=_params(("parallel", "arbitrary"), 56),
        name="conv_ffn_sample" if sample else "conv_ffn",
    )(*args)
    if sample:
        new_buf = jnp.concatenate([tail_g[:, :, :F], tail_v[:, :, :F]], axis=-1)
    else:
        nseq = M // seq_len
        pick = lambda t: t.reshape(nseq, tiles_per_seq, SUBLANES, FP)[:, -1, SUBLANES - 2:, :F]
        new_buf = jnp.concatenate([pick(tail_g), pick(tail_v)], axis=-1)
    return out, new_buf


def _gla_prompt(p0, alr, w2p, b_alpha, g_norm, *, B, L, H, DK, DV, tb):
    NT = L // tb
    NC = tb // CHUNK
    C = CHUNK
    scale = DK ** -0.5
    qk_blocks = H
    v_off = 2 * H * DK // DV

    def body(q_ref, k_ref, v_ref, r_ref, a_ref, w2_ref, ba_ref, gn_ref, y_ref, s_out_ref, s_scr):
        t = pl.program_id(2)

        @pl.when(t == 0)
        def _():
            s_scr[...] = jnp.zeros_like(s_scr)

        z = _dot(a_ref[...].astype(bf16), w2_ref[...].astype(bf16)) + ba_ref[...]
        gl = jax.nn.log_sigmoid(z) * (1.0 / GLA_TAU)
        bc3 = _chunk_cumsum(gl, C).reshape(NC, C, DK)
        bl3 = bc3[:, C - 1:C, :]
        q3 = (q_ref[...].astype(f32) * scale).reshape(NC, C, DK)
        k3 = k_ref[...].astype(f32).reshape(NC, C, DK)
        v3 = v_ref[...].astype(bf16).reshape(NC, C, DV)
        qd3 = (q3 * jnp.exp(bc3)).astype(bf16)
        kd3 = (k3 * jnp.exp(-bc3)).astype(bf16)
        kdec3 = (k3 * jnp.exp(bl3 - bc3)).astype(bf16)
        causal = (lax.broadcasted_iota(jnp.int32, (1, C, C), 1) >= lax.broadcasted_iota(jnp.int32, (1, C, C), 2))
        att = jnp.where(causal, jnp.einsum('cik,cjk->cij', qd3, kd3, preferred_element_type=f32), 0.0)
        intra = jnp.einsum('cij,cjv->civ', att.astype(bf16), v3, preferred_element_type=f32)
        ds = jnp.einsum('cjk,cjv->ckv', kdec3, v3, preferred_element_type=f32)
        s = s_scr[...]
        s_in = []
        for c in range(NC):
            s_in.append(s)
            s = _col_from_row(jnp.exp(bl3[c])) * s + ds[c]
        s_scr[...] = s
        s_all = jnp.stack(s_in).astype(bf16)
        o = (intra + jnp.einsum('cik,ckv->civ', qd3, s_all, preferred_element_type=f32)).reshape(tb, DV)
        rr = r_ref[...].astype(f32)
        y_ref[...] = (_rms(o, gn_ref[...]) * (rr * jax.nn.sigmoid(rr))).astype(bf16)

        @pl.when(t == NT - 1)
        def _():
            s_out_ref[0, 0] = s
    return pl.pallas_call(
        body,
        grid=(B, H, NT),
        in_specs=[pl.BlockSpec((tb, DK), lambda b, h, t: (b * NT + t, h)),
                  pl.BlockSpec((tb, DK), lambda b, h, t: (b * NT + t, qk_blocks + h)),
                  pl.BlockSpec((tb, DV), lambda b, h, t: (b * NT + t, v_off + h)),
                  pl.BlockSpec((tb, DV), lambda b, h, t: (b * NT + t, v_off + H + h)),
                  pl.BlockSpec((tb, LANES), lambda b, h, t: (b * NT + t, 0)),
                  pl.BlockSpec((LANES, DK), lambda b, h, t: (0, h)),
                  pl.BlockSpec((1, DK), lambda b, h, t: (0, h)),
                  pl.BlockSpec((1, DV), lambda b, h, t: (0, h))],
        out_specs=[pl.BlockSpec((tb, DV), lambda b, h, t: (b * NT + t, h)),
                   pl.BlockSpec((1, 1, DK, DV), lambda b, h, t: (b, h, 0, 0))],
        out_shape=[jax.ShapeDtypeStruct((B * L, H * DV), bf16),
                   jax.ShapeDtypeStruct((B, H, DK, DV), f32)],
        scratch_shapes=[pltpu.VMEM((DK, DV), f32)],
        compiler_params=_params(("parallel", "parallel", "arbitrary"), 32),
        name="gla_prompt",
    )(p0, p0, p0, p0, alr, w2p, b_alpha.reshape(1, -1), g_norm.reshape(1, -1))


def _gla_sample(p0, alr, w2p, b_alpha, g_norm, s0, *, H, DK, DV):
    Bs = p0.shape[0]
    scale = DK ** -0.5
    qkw = H * DK
    vw = H * DV
    assert vw % qkw == 0
    SB = SUBLANES

    def body(q_ref, k_ref, v_ref, r_ref, a_ref, w2_ref, ba_ref, gn_ref, s_ref, y_ref, so_ref, gl_scr):
        z = _dot(a_ref[...].astype(bf16), w2_ref[...].astype(bf16)) + ba_ref[...]
        gl_scr[...] = jax.nn.log_sigmoid(z) * (1.0 / GLA_TAU)

        for h in range(H):
            ks = slice(h * DK, (h + 1) * DK)
            vs = slice(h * DV, (h + 1) * DV)
            a_t = jnp.exp(gl_scr[:, ks]).T
            k_t = k_ref[:, ks].T
            q_t = (q_ref[:, ks] * scale).T
            v = v_ref[:, vs]
            outs = []
            for s in range(SB):
                sn = a_t[:, s:s + 1] * s_ref[s, h] + k_t[:, s:s + 1] * v[s:s + 1, :]
                so_ref[s, h] = sn
                outs.append(jnp.sum(q_t[:, s:s + 1] * sn, axis=0, keepdims=True))
            o = jnp.concatenate(outs, axis=0)
            rr = r_ref[:, vs]
            y_ref[:, vs] = _rms(o, gn_ref[:, vs]) * (rr * jax.nn.sigmoid(rr))

    v_blk = 2 * qkw // vw
    return pl.pallas_call(
        body,
        grid=(Bs // SB,),
        in_specs=[pl.BlockSpec((SB, qkw), lambda b: (b, 0)),
                  pl.BlockSpec((SB, qkw), lambda b: (b, 1)),
                  pl.BlockSpec((SB, vw), lambda b: (b, v_blk)),
                  pl.BlockSpec((SB, vw), lambda b: (b, v_blk + 1)),
                  pl.BlockSpec((SB, LANES), lambda b: (b, 0)),
                  pl.BlockSpec((LANES, qkw), lambda b: (0, 0)),
                  pl.BlockSpec((1, qkw), lambda b: (0, 0)),
                  pl.BlockSpec((1, vw), lambda b: (0, 0)),
                  pl.BlockSpec((SB, H, DK, DV), lambda b: (b, 0, 0, 0))],
        out_specs=[pl.BlockSpec((SB, vw), lambda b: (b, 0)),
                   pl.BlockSpec((SB, H, DK, DV), lambda b: (b, 0, 0, 0))],
        out_shape=[jax.ShapeDtypeStruct((Bs, vw), f32),
                   jax.ShapeDtypeStruct((Bs, H, DK, DV), f32)],
        scratch_shapes=[pltpu.VMEM((SB, qkw), f32)],
        compiler_params=_params(("parallel",), 32),
        name="gla_sample",
    )(p0, p0, p0, p0, alr, w2p, b_alpha.reshape(1, -1), g_norm.reshape(1, -1), s0)


def _rg_gates(xc, wr, br, wi, bi, sp):
    xb = xc.astype(bf16)
    r = jax.nn.sigmoid(_dot(xb, wr) + br)
    i = jax.nn.sigmoid(_dot(xb, wi) + bi)
    log_a = -RG_C * r * sp
    a = jnp.exp(log_a)
    mult = jnp.sqrt(1.0 - a * a)
    return a, mult, i


def _rglru_prompt(p0, conv_w, conv_b, w_r, b_r, w_i, b_i, lam, *, B, L, x_blk, g_blk):
    M = B * L
    NB, BS, _ = w_r.shape
    assert BS == LANES
    rc = min(256, L)

    def body(x_ref, gg_ref, cw_ref, cb_ref, wr_ref, br_ref, wi_ref, bi_ref, lam_ref, y_ref, hl_ref, a_scr, b_scr, xs_scr):
        wr = wr_ref[...].astype(bf16)
        wi = wi_ref[...].astype(bf16)
        sp = jax.nn.softplus(-lam_ref[...])
        cw = cw_ref[...]

        def chunk(c, carry):
            r0 = pl.multiple_of(c * rc, rc)
            rows = pl.ds(r0, rc)
            x = x_ref[rows, :].astype(f32)
            start = (r0 % L) == 0
            prev = x_ref[pl.ds(pl.multiple_of(jnp.maximum(r0 - HALO, 0), HALO), HALO), :].astype(f32)[HALO - SUBLANES:]
            xs_scr[0:SUBLANES, :] = jnp.where(start, 0.0, prev)
            xs_scr[SUBLANES:, :] = x
            xc = (cb_ref[...] + cw[0:1] * xs_scr[SUBLANES - 3:SUBLANES - 3 + rc, :]
                  + cw[1:2] * xs_scr[SUBLANES - 2:SUBLANES - 2 + rc, :]
                  + cw[2:3] * xs_scr[SUBLANES - 1:SUBLANES - 1 + rc, :] + cw[3:4] * x)
            a, mult, ig = _rg_gates(xc, wr, br_ref[...], wi, bi_ref[...], sp)
            first = jnp.logical_and(start, lax.broadcasted_iota(jnp.int32, (rc, 1), 0) == 0)
            mult = jnp.where(first, 1.0, mult)
            a_scr[rows, :] = a
            b_scr[rows, :] = mult * (ig * xc)
            return carry
        lax.fori_loop(0, M // rc, chunk, 0, unroll=4)

        def step(t, hs):
            new = []
            for b in range(B):
                row = pl.ds(b * L + t, 1)
                h = a_scr[row, :] * hs[b] + b_scr[row, :]
                b_scr[row, :] = h
                new.append(h)
            return tuple(new)
        hs = lax.fori_loop(0, L, step, tuple(jnp.zeros((1, LANES), f32) for _ in range(B)), unroll=32)
        hl_ref[...] = jnp.concatenate(hs, axis=0)

        def outc(c, carry):
            rows = pl.ds(pl.multiple_of(c * rc, rc), rc)
            y_ref[rows, :] = (b_scr[rows, :] * jax.nn.gelu(gg_ref[rows, :].astype(f32))).astype(bf16)
            return carry
        lax.fori_loop(0, M // rc, outc, 0, unroll=2)

    W = NB * BS
    return pl.pallas_call(
        body,
        grid=(NB,),
        in_specs=[pl.BlockSpec((M, LANES), lambda n: (0, x_blk + n)),
                  pl.BlockSpec((M, LANES), lambda n: (0, g_blk + n)),
                  pl.BlockSpec((4, LANES), lambda n: (0, n)),
                  pl.BlockSpec((1, LANES), lambda n: (0, n)),
                  pl.BlockSpec((None, BS, BS), lambda n: (n, 0, 0)),
                  pl.BlockSpec((1, LANES), lambda n: (0, n)),
                  pl.BlockSpec((None, BS, BS), lambda n: (n, 0, 0)),
                  pl.BlockSpec((1, LANES), lambda n: (0, n)),
                  pl.BlockSpec((1, LANES), lambda n: (0, n))],
        out_specs=[pl.BlockSpec((M, LANES), lambda n: (0, n)),
                   pl.BlockSpec((B, LANES), lambda n: (0, n))],
        out_shape=[jax.ShapeDtypeStruct((M, W), bf16), jax.ShapeDtypeStruct((B, W), f32)],
        scratch_shapes=[pltpu.VMEM((M, LANES), f32), pltpu.VMEM((M, LANES), f32), pltpu.VMEM((rc + SUBLANES, LANES), f32)],
        compiler_params=_params(("parallel",), 48),
        name="rglru_prompt",
    )(p0, p0, conv_w, conv_b.reshape(1, W), w_r, b_r.reshape(1, W), w_i, b_i.reshape(1, W), lam.reshape(1, W))


def _rglru_sample(p0, conv_state, h0, conv_w, conv_b, w_r, b_r, w_i, b_i, lam, *, x_blk, g_blk):
    Bs = p0.shape[0]
    NB, BS, _ = w_r.shape
    W = NB * BS
    s0, s1, s2 = conv_state[:, 0], conv_state[:, 1], conv_state[:, 2]

    def body(x_ref, gg_ref, s0_ref, s1_ref, s2_ref, h0_ref, cw_ref, cb_ref, wr_ref, br_ref, wi_ref, bi_ref, lam_ref,
             y_ref, h_ref):
        cw = cw_ref[...]
        x = x_ref[...]
        xc = cb_ref[...] + cw[0:1] * s0_ref[...] + cw[1:2] * s1_ref[...] + cw[2:3] * s2_ref[...] + cw[3:4] * x
        sp = jax.nn.softplus(-lam_ref[...])
        a, mult, ig = _rg_gates(xc, wr_ref[...].astype(bf16), br_ref[...], wi_ref[...].astype(bf16), bi_ref[...], sp)
        if PAST_LEN == 0:
            mult = jnp.ones_like(mult)
        h = a * h0_ref[...] + mult * (ig * xc)
        h_ref[...] = h
        y_ref[...] = (h * jax.nn.gelu(gg_ref[...])).astype(bf16)

    blk = lambda n: (0, n)
    vec = pl.BlockSpec((1, LANES), blk)
    mat = pl.BlockSpec((Bs, LANES), blk)
    y, h = pl.pallas_call(
        body,
        grid=(NB,),
        in_specs=[pl.BlockSpec((Bs, LANES), lambda n: (0, x_blk + n)),
                  pl.BlockSpec((Bs, LANES), lambda n: (0, g_blk + n)),
                  mat, mat, mat, mat,
                  pl.BlockSpec((4, LANES), blk), vec,
                  pl.BlockSpec((None, BS, BS), lambda n: (n, 0, 0)), vec,
                  pl.BlockSpec((None, BS, BS), lambda n: (n, 0, 0)), vec, vec],
        out_specs=[mat, mat],
        out_shape=[jax.ShapeDtypeStruct((Bs, W), bf16), jax.ShapeDtypeStruct((Bs, W), f32)],
        compiler_params=_params(("parallel",), 32),
        name="rglru_sample",
    )(p0, p0, s0, s1, s2, h0, conv_w, conv_b.reshape(1, W), w_r, b_r.reshape(1, W), w_i, b_i.reshape(1, W),
      lam.reshape(1, W))
    return y, h


def _blockdiag_tiles(w, tile):
    nblk, bs, _ = w.shape
    per = tile // bs
    nt = nblk // per
    rows = jnp.tile(w.reshape(nt, tile, bs), (1, 1, per))
    on_diag = (jnp.arange(tile)[:, None] // bs) == (jnp.arange(tile)[None, :] // bs)
    return jnp.where(on_diag[None], rows, 0.0)


def _mlstm_stage1(p1, conv_w, conv_b, wq_t, wk_t, wv_t, wg, bg, *, seq_len, tr, H, DH, conv_state=None):
    M = p1.shape[0]
    W = H * DH
    NTL, TL, _ = wq_t.shape
    sample = conv_state is not None
    tiles_per_seq = max(seq_len // tr, 1)
    kscale = DH ** -0.5

    def body(*refs):
        it = iter(refs)
        x_ref = next(it)
        if sample:
            s0_ref, s1_ref, s2_ref = next(it), next(it), next(it)
        else:
            xh_ref = next(it)
        cw_ref, cb_ref, wq_ref, wk_ref, wv_ref, wg_ref, bg_ref = (next(it) for _ in range(7))
        q_ref, k_ref, v_ref, g_ref, xc_ref = (next(it) for _ in range(5))
        cw = cw_ref[...]
        x = x_ref[...].astype(f32)
        if sample:
            conv = cb_ref[...] + cw[0:1] * s0_ref[...] + cw[1:2] * s1_ref[...] + cw[2:3] * s2_ref[...] + cw[3:4] * x
        else:
            xs_scr = next(it)
            keep = (pl.program_id(0) % tiles_per_seq != 0).astype(f32)
            xs_scr[0:SUBLANES, :] = xh_ref[...].astype(f32)[HALO - SUBLANES:] * keep
            xs_scr[SUBLANES:, :] = x
            conv = (cb_ref[...] + cw[0:1] * xs_scr[SUBLANES - 3:SUBLANES - 3 + tr, :]
                    + cw[1:2] * xs_scr[SUBLANES - 2:SUBLANES - 2 + tr, :]
                    + cw[2:3] * xs_scr[SUBLANES - 1:SUBLANES - 1 + tr, :] + cw[3:4] * x)
        xc = conv * jax.nn.sigmoid(conv)
        xc_ref[...] = xc.astype(xc_ref.dtype)
        xcb = xc.astype(bf16)
        xb = x.astype(bf16)
        qs, ks, vs = [], [], []
        for t in range(NTL):
            cs = slice(t * TL, (t + 1) * TL)
            qs.append(_dot(xcb[:, cs], wq_ref[t]))
            ks.append(_dot(xcb[:, cs], wk_ref[t]) * kscale)
            vs.append(_dot(xb[:, cs], wv_ref[t]))
        q = jnp.concatenate(qs, axis=1)
        k = jnp.concatenate(ks, axis=1)
        v = jnp.concatenate(vs, axis=1)
        q_ref[...] = q.astype(q_ref.dtype)
        k_ref[...] = k.astype(k_ref.dtype)
        v_ref[...] = v.astype(v_ref.dtype)
        gt = (_dot(q.astype(bf16), wg_ref[0:W, :]) + _dot(k.astype(bf16), wg_ref[W:2 * W, :])
              + _dot(v.astype(bf16), wg_ref[2 * W:3 * W, :]) + bg_ref[...])
        lane = lax.broadcasted_iota(jnp.int32, gt.shape, 1)
        g_ref[...] = jnp.where(jnp.logical_and(lane >= H, lane < 2 * H), jax.nn.log_sigmoid(gt), gt)

    row = lambda i: (i, 0)
    const2 = lambda i: (0, 0)
    const3 = lambda i: (0, 0, 0)
    in_specs = [pl.BlockSpec((tr, W), row)]
    args = [p1]
    if sample:
        in_specs += [pl.BlockSpec((tr, W), row)] * 3
        args += [conv_state[:, 0], conv_state[:, 1], conv_state[:, 2]]
    else:
        in_specs.append(pl.BlockSpec((HALO, W), lambda i: (jnp.maximum(i * (tr // HALO) - 1, 0), 0)))
        args.append(p1)
    in_specs += [pl.BlockSpec((4, W), const2), pl.BlockSpec((1, W), const2),
                 pl.BlockSpec((NTL, TL, TL), const3), pl.BlockSpec((NTL, TL, TL), const3),
                 pl.BlockSpec((NTL, TL, TL), const3),
                 pl.BlockSpec((3 * W, LANES), const2), pl.BlockSpec((1, LANES), const2)]
    args += [conv_w, conv_b.reshape(1, W), wq_t, wk_t, wv_t, wg, bg]
    return pl.pallas_call(
        body,
        grid=(M // tr,),
        in_specs=in_specs,
        out_specs=[pl.BlockSpec((tr, W), row)] * 3 + [pl.BlockSpec((tr, LANES), row), pl.BlockSpec((tr, W), row)],
        out_shape=[jax.ShapeDtypeStruct((M, W), f32 if sample else bf16)] * 3
        + [jax.ShapeDtypeStruct((M, LANES), f32), jax.ShapeDtypeStruct((M, W), f32 if sample else bf16)],
        scratch_shapes=[] if sample else [pltpu.VMEM((tr + SUBLANES, W), f32)],
        compiler_params=_params(("parallel",), 48),
        name="mlstm_stage1_sample" if sample else "mlstm_stage1",
    )(*args)


def _mlstm_prompt(q, k, v, gates, xc, p1, g_norm, skip, *, B, L, H, DH, tb):
    NT = L // tb
    NC = tb // CHUNK
    C = CHUNK

    def body(q_ref, k_ref, v_ref, g_ref, xc_ref, om_ref, gn_ref, sk_ref,
             y_ref, c_out, n_out, m_out, c_scr, n_scr, m_scr):
        hh = pl.program_id(1)
        t = pl.program_id(2)

        @pl.when(t == 0)
        def _():
            c_scr[...] = jnp.zeros_like(c_scr)
            n_scr[...] = jnp.zeros_like(n_scr)
            m_scr[...] = jnp.zeros_like(m_scr)

        gts = g_ref[...]
        lane = lax.broadcasted_iota(jnp.int32, (tb, LANES), 1)
        i_col = jnp.sum(jnp.where(lane == hh, gts, 0.0), axis=1, keepdims=True)
        b_col = jnp.sum(jnp.where(lane == H + hh, _chunk_cumsum(gts, C), 0.0), axis=1, keepdims=True)
        b3 = b_col.reshape(NC, C, 1)
        i3 = i_col.reshape(NC, C, 1)
        ii = lax.broadcasted_iota(jnp.int32, (1, C, C), 1)
        jj = lax.broadcasted_iota(jnp.int32, (1, C, C), 2)
        eye = ii == jj
        causal = ii >= jj
        as_row = lambda col3: jnp.sum(jnp.where(eye, jnp.broadcast_to(col3, (NC, C, C)), 0.0), axis=1, keepdims=True)
        dmat = jnp.where(causal, b3 - as_row(b3) + as_row(i3), -jnp.inf)
        rmax = jnp.max(dmat, axis=2, keepdims=True)
        b_last = b3[:, C - 1:C, :]
        m_prev = m_scr[...]
        m_in = []
        for c in range(NC):
            m_in.append(m_prev)
            m_prev = jnp.maximum(b_last[c] + m_prev, rmax[c][C - 1:C, :])
        m_scr[...] = m_prev
        inter = b3 + jnp.stack(m_in)
        m_col = jnp.maximum(inter, rmax)
        g_col = jnp.exp(inter - m_col)
        q3 = q_ref[...].reshape(NC, C, DH)
        k3 = k_ref[...].reshape(NC, C, DH)
        v3 = v_ref[...].reshape(NC, C, DH)
        s = jnp.einsum('cid,cjd->cij', q3, k3, preferred_element_type=f32) * jnp.exp(dmat - m_col)
        num = jnp.einsum('cij,cje->cie', s.astype(bf16), v3, preferred_element_type=f32)
        den = jnp.sum(s, axis=2, keepdims=True)
        m_new = m_col[:, C - 1:C, :]
        wk = jnp.exp(b_last - b3 + i3 - m_new)
        gc = jnp.exp(inter[:, C - 1:C, :] - m_new)
        kw = k3.astype(f32) * wk
        dc = jnp.einsum('cse,csd->ced', v3, kw.astype(bf16), preferred_element_type=f32)
        dn = jnp.sum(kw, axis=1, keepdims=True)
        cs = c_scr[...]
        ns = n_scr[...]
        c_in, n_in = [], []
        for c in range(NC):
            c_in.append(cs)
            n_in.append(ns)
            cs = gc[c] * cs + dc[c]
            ns = gc[c] * ns + dn[c]
        c_scr[...] = cs
        n_scr[...] = ns
        c_all = jnp.stack(c_in).astype(bf16)
        n_all = jnp.stack(n_in)
        num = num + g_col * jnp.einsum('cid,ced->cie', q3, c_all, preferred_element_type=f32)
        den = den + g_col * jnp.sum(q3.astype(f32) * n_all, axis=2, keepdims=True)
        hm = (num / jnp.maximum(jnp.abs(den), jnp.exp(-m_col))).reshape(tb, DH)
        y = (_rms(hm, gn_ref[...]) + sk_ref[...] * xc_ref[...].astype(f32)) * jax.nn.sigmoid(om_ref[...].astype(f32))
        y_ref[...] = y.astype(bf16)

        @pl.when(t == NT - 1)
        def _():
            c_out[0, 0] = cs
            n_out[0, 0] = ns
            m_out[0, 0] = m_prev

    blk = lambda b, h, t: (b * NT + t, h)
    W = H * DH
    y, c_new, n_new, m_new = pl.pallas_call(
        body,
        grid=(B, H, NT),
        in_specs=[pl.BlockSpec((tb, DH), blk), pl.BlockSpec((tb, DH), blk), pl.BlockSpec((tb, DH), blk),
                  pl.BlockSpec((tb, LANES), lambda b, h, t: (b * NT + t, 0)),
                  pl.BlockSpec((tb, DH), blk),
                  pl.BlockSpec((tb, DH), lambda b, h, t: (b * NT + t, H + h)),
                  pl.BlockSpec((1, DH), lambda b, h, t: (0, h)),
                  pl.BlockSpec((1, DH), lambda b, h, t: (0, h))],
        out_specs=[pl.BlockSpec((tb, DH), blk),
                   pl.BlockSpec((1, 1, DH, DH), lambda b, h, t: (b, h, 0, 0)),
                   pl.BlockSpec((1, 1, 1, DH), lambda b, h, t: (b, h, 0, 0)),
                   pl.BlockSpec((1, 1, 1, 1), lambda b, h, t: (b, h, 0, 0))],
        out_shape=[jax.ShapeDtypeStruct((B * L, W), bf16),
                   jax.ShapeDtypeStruct((B, H, DH, DH), f32),
                   jax.ShapeDtypeStruct((B, H, 1, DH), f32),
                   jax.ShapeDtypeStruct((B, H, 1, 1), f32)],
        scratch_shapes=[pltpu.VMEM((DH, DH), f32), pltpu.VMEM((1, DH), f32), pltpu.VMEM((1, 1), f32)],
        compiler_params=_params(("parallel", "parallel", "arbitrary"), 32),
        name="mlstm_prompt",
    )(q, k, v, gates, xc, p1, g_norm.reshape(1, W), skip.reshape(1, W))
    return y, c_new, n_new.reshape(B, H, DH), m_new.reshape(B, H)


def _mlstm_sample(q, k, v, gates, xc, p1, g_norm, skip, c0, n0, m0, *, H, DH):
    Bs = q.shape[0]
    W = H * DH
    SB = SUBLANES

    def body(q_ref, k_ref, v_ref, g_ref, xc_ref, om_ref, gn_ref, sk_ref, c_ref, n_ref, m_ref,
             y_ref, c_out, n_out, m_out):
        gts = g_ref[...]
        for h in range(H):
            cs = slice(h * DH, (h + 1) * DH)
            ig = gts[:, h:h + 1]
            fg = gts[:, H + h:H + h + 1]
            inter = fg + m_ref[:, h:h + 1]
            m = jnp.maximum(inter, ig)
            g = jnp.exp(inter - m)
            m_out[:, h:h + 1] = m
            q = q_ref[:, cs]
            kw = k_ref[:, cs] * jnp.exp(ig - m)
            nn = g * n_ref[:, cs] + kw
            n_out[:, cs] = nn
            den = jnp.sum(nn * q, axis=1, keepdims=True)
            v_t = v_ref[:, cs].T
            qb = q.astype(bf16)
            nums = []
            for s in range(SB):
                cn = g[s:s + 1, :] * c_ref[s, h] + v_t[:, s:s + 1] * kw[s:s + 1, :]
                c_out[s, h] = cn
                nums.append(_dot_nt(qb, cn.astype(bf16))[s:s + 1, :])
            num = jnp.concatenate(nums, axis=0)
            hm = num / jnp.maximum(jnp.abs(den), jnp.exp(-m))
            y_ref[:, cs] = ((_rms(hm, gn_ref[:, cs]) + sk_ref[:, cs] * xc_ref[:, cs])
                            * jax.nn.sigmoid(om_ref[:, cs]))

    per = lambda b: (b, 0)
    const2 = lambda b: (0, 0)
    y, c_new, n_new, m_new = pl.pallas_call(
        body,
        grid=(Bs // SB,),
        in_specs=[pl.BlockSpec((SB, W), per), pl.BlockSpec((SB, W), per), pl.BlockSpec((SB, W), per),
                  pl.BlockSpec((SB, LANES), per), pl.BlockSpec((SB, W), per),
                  pl.BlockSpec((SB, W), lambda b: (b, 1)),
                  pl.BlockSpec((1, W), const2), pl.BlockSpec((1, W), const2),
                  pl.BlockSpec((SB, H, DH, DH), lambda b: (b, 0, 0, 0)),
                  pl.BlockSpec((SB, W), per), pl.BlockSpec((SB, H), per)],
        out_specs=[pl.BlockSpec((SB, W), per),
                   pl.BlockSpec((SB, H, DH, DH), lambda b: (b, 0, 0, 0)),
                   pl.BlockSpec((SB, W), per), pl.BlockSpec((SB, H), per)],
        out_shape=[jax.ShapeDtypeStruct((Bs, W), f32), jax.ShapeDtypeStruct((Bs, H, DH, DH), f32),
                   jax.ShapeDtypeStruct((Bs, W), f32), jax.ShapeDtypeStruct((Bs, H), f32)],
        compiler_params=_params(("parallel",), 48),
        name="mlstm_sample",
    )(q, k, v, gates, xc, p1, g_norm.reshape(1, W), skip.reshape(1, W), c0, n0.reshape(Bs, W), m0)
    return y, c_new, n_new.reshape(Bs, H, DH), m_new


S5_TILES = 8


def _s5_layouts(lam_re, lam_im, log_dt, b_re, b_im, c_re, c_im):
    G, P = lam_re.shape
    GC = b_re.shape[2]
    T = S5_TILES
    gpt = G // T
    ns = G * P
    flat = lambda a: a.reshape(ns)
    ldt = jnp.broadcast_to(log_dt[:, None], (G, P))
    rows = [flat(a).reshape(T, 1, ns // T) for a in (lam_re, lam_im, ldt)]
    eye = jnp.eye(gpt, dtype=f32)
    bbd = [jnp.einsum('jgpc,gh->jgchp', a.reshape(T, gpt, P, GC), eye).reshape(T, gpt * GC, gpt * P) for a in (b_re, b_im)]
    cbd = [jnp.einsum('jgcp,gh->jgphc', a.reshape(T, gpt, GC, P), eye).reshape(T, gpt * P, gpt * GC) for a in (c_re, c_im)]
    return rows, bbd, cbd


def _s5_discretise(lre, lim, ldt):
    dt = jnp.exp(ldt)
    mag = jnp.exp(dt * lre)
    ar = mag * jnp.cos(dt * lim)
    ai = mag * jnp.sin(dt * lim)
    den = lre * lre + lim * lim
    cr = ((ar - 1.0) * lre + ai * lim) / den
    ci = (ai * lre - (ar - 1.0) * lim) / den
    return ar, ai, cr, ci


def _s5_mixer(p1, u_blk, rows, bbd, cbd, d_skip, w_glu, b_glu, *, B, L, tb, state=None):
    T = S5_TILES
    lre_r, lim_r, ldt_r = rows
    SW = lre_r.shape[2]
    CW = bbd[0].shape[1]
    W = T * CW
    NS = T * SW
    KT = SW // LANES
    sample = state is not None
    NT = 1 if sample else L // tb
    M = B * L

    def body(*refs):
        it = iter(refs)
        u_ref = next(it)
        if sample:
            x0r_ref, x0i_ref = next(it), next(it)
        lre_ref, lim_ref, ldt_ref = next(it), next(it), next(it)
        bre_ref, bim_ref, cre_ref, cim_ref = next(it), next(it), next(it), next(it)
        d_ref, wg_ref, bgl_ref = next(it), next(it), next(it)
        y_ref, xr_out, xi_out = next(it), next(it), next(it)
        bbr, bbi, cpair, ar_scr, ai_scr = (next(it) for _ in range(5))
        if not sample:
            sre, sim, xr_c, xi_c, yacc = (next(it) for _ in range(5))
        first = jnp.logical_and(pl.program_id(0) == 0, pl.program_id(1) == 0)

        @pl.when(first)
        def _():
            for j in range(T):
                ar, ai, cr, ci = _s5_discretise(lre_ref[j], lim_ref[j], ldt_ref[j])
                ar_scr[j] = ar
                ai_scr[j] = ai
                br = bre_ref[j]
                bi = bim_ref[j]
                bbr[j] = (cr * br - ci * bi).astype(bf16)
                bbi[j] = (cr * bi + ci * br).astype(bf16)
            for jp in range(T // 2):
                cpair[jp] = jnp.zeros((4 * SW, 2 * CW), bf16)
                for half in range(2):
                    j = 2 * jp + half
                    r0 = 2 * half * SW
                    cols = slice(half * CW, (half + 1) * CW)
                    cpair[jp, r0:r0 + SW, cols] = cre_ref[j].astype(bf16)
                    cpair[jp, r0 + SW:r0 + 2 * SW, cols] = (-cim_ref[j]).astype(bf16)

        u = u_ref[...].astype(f32)
        ub = u_ref[...].astype(bf16)
        ys = []
        if sample:
            for j in range(T):
                cs = slice(j * SW, (j + 1) * SW)
                uj = ub[:, j * CW:(j + 1) * CW]
                ar = ar_scr[j]
                ai = ai_scr[j]
                x0r = x0r_ref[:, cs]
                x0i = x0i_ref[:, cs]
                xr = ar * x0r - ai * x0i + _dot(uj, bbr[j])
                xi = ar * x0i + ai * x0r + _dot(uj, bbi[j])
                xr_out[:, cs] = xr
                xi_out[:, cs] = xi
                ys += [xr.astype(bf16), xi.astype(bf16)]
            y = jnp.concatenate([_dot(jnp.concatenate(ys[4 * jp:4 * jp + 4], axis=1), cpair[jp]) for jp in range(T // 2)],
                                axis=1)
        else:
            t = pl.program_id(1)

            @pl.when(t == 0)
            def _():
                xr_c[...] = jnp.zeros_like(xr_c)
                xi_c[...] = jnp.zeros_like(xi_c)

            for j in range(T):
                uj = ub[:, j * CW:(j + 1) * CW]
                r = _dot(uj, bbr[j])
                im = _dot(uj, bbi[j])
                for kk in range(KT):
                    sre[kk, pl.ds(j, tb, stride=T), :] = r[:, kk * LANES:(kk + 1) * LANES]
                    sim[kk, pl.ds(j, tb, stride=T), :] = im[:, kk * LANES:(kk + 1) * LANES]
            a_r = [jnp.concatenate([ar_scr[j][:, kk * LANES:(kk + 1) * LANES] for j in range(T)], axis=0) for kk in range(KT)]
            a_i = [jnp.concatenate([ai_scr[j][:, kk * LANES:(kk + 1) * LANES] for j in range(T)], axis=0) for kk in range(KT)]

            def step(s, carry):
                xr, xi = carry
                row = pl.ds(pl.multiple_of(s * T, T), T)
                nr, ni = [], []
                for kk in range(KT):
                    r_ = a_r[kk] * xr[kk] - a_i[kk] * xi[kk] + sre[kk, row, :]
                    i_ = a_r[kk] * xi[kk] + a_i[kk] * xr[kk] + sim[kk, row, :]
                    sre[kk, row, :] = r_
                    sim[kk, row, :] = i_
                    nr.append(r_)
                    ni.append(i_)
                return tuple(nr), tuple(ni)
            xr0 = tuple(xr_c[kk] for kk in range(KT))
            xi0 = tuple(xi_c[kk] for kk in range(KT))
            xr, xi = lax.fori_loop(0, tb, step, (xr0, xi0), unroll=8)
            for kk in range(KT):
                xr_c[kk] = xr[kk]
                xi_c[kk] = xi[kk]

            @pl.when(t == NT - 1)
            def _():
                for kk in range(KT):
                    xr_out[kk] = xr[kk]
                    xi_out[kk] = xi[kk]

            for jp in range(T // 2):
                parts = []
                for j in (2 * jp, 2 * jp + 1):
                    parts += [sre[kk, pl.ds(j, tb, stride=T), :] for kk in range(KT)]
                    parts += [sim[kk, pl.ds(j, tb, stride=T), :] for kk in range(KT)]
                xp = jnp.concatenate(parts, axis=1).astype(bf16)
                yacc[:, 2 * jp * CW:(2 * jp + 2) * CW] = _dot(xp, cpair[jp])
            y = yacc[...]
        ysk = jax.nn.gelu(y + d_ref[...] * u)
        z = _dot(ysk.astype(bf16), wg_ref[...]) + bgl_ref[...]
        y_ref[...] = (ysk * jax.nn.sigmoid(z)).astype(bf16)

    c3 = lambda b, t: (0, 0, 0)
    c2 = lambda b, t: (0, 0)
    in_specs = [pl.BlockSpec((tb, W), lambda b, t: (b * NT + t, u_blk))]
    args = [p1]
    if sample:
        in_specs += [pl.BlockSpec((tb, NS), lambda b, t: (b, 0))] * 2
        args += [state[0], state[1]]
    in_specs += [pl.BlockSpec((T, 1, SW), c3)] * 3
    in_specs += [pl.BlockSpec((T, CW, SW), c3)] * 2 + [pl.BlockSpec((T, SW, CW), c3)] * 2
    in_specs += [pl.BlockSpec((1, W), c2), pl.BlockSpec((W, W), c2), pl.BlockSpec((1, W), c2)]
    args += [lre_r, lim_r, ldt_r, bbd[0], bbd[1], cbd[0], cbd[1], d_skip.reshape(1, W), w_glu, b_glu.reshape(1, W)]
    scratch = [pltpu.VMEM((T, CW, SW), bf16), pltpu.VMEM((T, CW, SW), bf16),
               pltpu.VMEM((T // 2, 4 * SW, 2 * CW), bf16),
               pltpu.VMEM((T, 1, SW), f32), pltpu.VMEM((T, 1, SW), f32)]
    if sample:
        grid = (M // tb, 1)
        st_spec = pl.BlockSpec((tb, NS), lambda b, t: (b, 0))
        st_shape = jax.ShapeDtypeStruct((M, NS), f32)
    else:
        grid = (B, NT)
        st_spec = pl.BlockSpec((None, KT, T, LANES), lambda b, t: (b, 0, 0, 0))
        st_shape = jax.ShapeDtypeStruct((B, KT, T, LANES), f32)
        scratch += [pltpu.VMEM((KT, tb * T, LANES), f32), pltpu.VMEM((KT, tb * T, LANES), f32),
                    pltpu.VMEM((KT, T, LANES), f32), pltpu.VMEM((KT, T, LANES), f32), pltpu.VMEM((tb, W), f32)]
    y, xr, xi = pl.pallas_call(
        body,
        grid=grid,
        in_specs=in_specs,
        out_specs=[pl.BlockSpec((tb, W), lambda b, t: (b * NT + t, 0)), st_spec, st_spec],
        out_shape=[jax.ShapeDtypeStruct((M, W), bf16), st_shape, st_shape],
        scratch_shapes=scratch,
        compiler_params=_params(("arbitrary", "arbitrary"), 56),
        name="s5_sample" if sample else "s5_prompt",
    )(*args)
    if not sample:
        xr = jnp.transpose(xr, (0, 2, 1, 3)).reshape(B, NS)
        xi = jnp.transpose(xi, (0, 2, 1, 3)).reshape(B, NS)
    return y, xr, xi


def _trunk(x3, st, w, *, sample):
    B, L, D = x3.shape
    M = B * L
    x = x3.reshape(M, D)
    H_g, DK, DV = st['gla_S'].shape[1:] if sample else w['gla_dims']
    H_m, DH = w['ml_dims']
    G, P = w['s5_dims']
    tm = M if sample else min(2048, L)
    tf = M if sample else min(1024, L)
    to = M if sample else min(2048, M)
    tb = min(2048, L)
    ts = min(512, L)
    tr = M if sample else min(512, L)
    tn_in, tn_out = 512, 512
    pdt = f32 if sample else bf16
    out = {}

    p0, alr = _norm_matmul(x, w['g_mix0'], w['w_in0_main'], tm=tm, tn=tn_in, w_side=w['w_in0_alr'], out_dtype=pdt, w_t=True,
                            seg_cols=w['w_in0_cols'])
    x_blk = (2 * H_g * DK + 2 * H_g * DV) // LANES
    W_rg = w['rg_lambda'].shape[0]
    g_blk = x_blk + W_rg // LANES
    rg_tail = p0.reshape(B, L, p0.shape[1])[:, max(L - 3, 0):, x_blk * LANES:x_blk * LANES + W_rg].astype(f32)
    if sample:
        ya, out['gla_S'] = _gla_sample(p0, alr, w['gla_w2p'], w['gla_b_alpha'], w['gla_g_norm'], st['gla_S'],
                                       H=H_g, DK=DK, DV=DV)
        yb, out['rg_h'] = _rglru_sample(p0, st['rg_conv'], st['rg_h'], w['rg_conv_w'], w['rg_conv_b'], w['rg_w_r'],
                                        w['rg_b_r'], w['rg_w_i'], w['rg_b_i'], w['rg_lambda'], x_blk=x_blk, g_blk=g_blk)
        out['rg_conv'] = jnp.concatenate([st['rg_conv'][:, 1:], rg_tail], axis=1)
    else:
        ya, out['gla_S'] = _gla_prompt(p0, alr, w['gla_w2p'], w['gla_b_alpha'], w['gla_g_norm'],
                                       B=B, L=L, H=H_g, DK=DK, DV=DV, tb=tb)
        yb, out['rg_h'] = _rglru_prompt(p0, w['rg_conv_w'], w['rg_conv_b'], w['rg_w_r'], w['rg_b_r'], w['rg_w_i'],
                                        w['rg_b_i'], w['rg_lambda'], B=B, L=L, x_blk=x_blk, g_blk=g_blk)
        out['rg_conv'] = rg_tail
    x = _out_proj(ya, yb, w['w_out0'], x, tm=to, tn=tn_out)
    x, ffn0 = _conv_ffn(x, w['g_ffn'][0], w['ffn'][0], seq_len=L, tm=tf, prev=st['ffn_conv'] if sample else None)

    W_ml = H_m * DH
    p1 = _norm_matmul(x, w['g_mix1'], [w['w_in1']], tm=tm, tn=tn_in, out_dtype=pdt)
    xm_tail = p1.reshape(B, L, p1.shape[1])[:, max(L - 3, 0):, :W_ml].astype(f32)
    q, k, v, gates, xc = _mlstm_stage1(p1, w['ml_conv_w'], w['ml_conv_b'], w['ml_wq_t'], w['ml_wk_t'], w['ml_wv_t'],
                                       w['ml_wg'], w['ml_bg'], seq_len=L, tr=tr, H=H_m, DH=DH,
                                       conv_state=st['ml_conv'] if sample else None)
    u_blk = 2 * W_ml // (G * w['s5_gc'])
    if sample:
        yc, out['ml_C'], out['ml_n'], out['ml_m'] = _mlstm_sample(
            q, k, v, gates, xc, p1, w['ml_g_norm'], w['ml_skip'], st['ml_C'], st['ml_n'], st['ml_m'], H=H_m, DH=DH)
        out['ml_conv'] = jnp.concatenate([st['ml_conv'][:, 1:], xm_tail], axis=1)
        yd, s5r, s5i = _s5_mixer(p1, u_blk, w['s5_rows'], w['s5_bbd'], w['s5_cbd'], w['s5_D'], w['s5_w_glu'],
                                 w['s5_b_glu'], B=B, L=1, tb=B,
                                 state=(st['s5_re'].reshape(B, G * P), st['s5_im'].reshape(B, G * P)))
    else:
        yc, out['ml_C'], out['ml_n'], out['ml_m'] = _mlstm_prompt(
            q, k, v, gates, xc, p1, w['ml_g_norm'], w['ml_skip'], B=B, L=L, H=H_m, DH=DH, tb=tb)
        out['ml_conv'] = xm_tail
        yd, s5r, s5i = _s5_mixer(p1, u_blk, w['s5_rows'], w['s5_bbd'], w['s5_cbd'], w['s5_D'], w['s5_w_glu'],
                                 w['s5_b_glu'], B=B, L=L, tb=ts)
    out['s5_re'] = s5r.reshape(B, G, P)
    out['s5_im'] = s5i.reshape(B, G, P)
    x = _out_proj(yc, yd, w['w_out1'], x, tm=to, tn=tn_out)
    x, ffn1 = _conv_ffn(x, w['g_ffn'][1], w['ffn'][1], seq_len=L, tm=tf, prev=st['ffn_conv'] if sample else None,
                        final_g=w['g_final'])
    out['ffn_conv'] = jnp.stack([ffn0, ffn1], axis=0)
    return x.reshape(B, L, D), out


def kernel(x_prompt, x_sample, state_gla_S, state_rglru_h, state_rglru_conv, state_mlstm_C, state_mlstm_n, state_mlstm_m, state_mlstm_conv, state_s5_re, state_s5_im, state_ffn_conv, g_mix0, w_in0, gla_w_alpha2, gla_b_alpha, gla_g_norm, rg_conv_w, rg_conv_b, rg_w_r, rg_b_r, rg_w_i, rg_b_i, rg_lambda, w_out0, g_mix1, w_in1, ml_conv_w, ml_conv_b, ml_wq, ml_wk, ml_wv, ml_w_igate, ml_b_igate, ml_w_fgate, ml_b_fgate, ml_g_norm, ml_skip, s5_lam_re, s5_lam_im, s5_log_dt, s5_B_re, s5_B_im, s5_C_re, s5_C_im, s5_D, s5_w_glu, s5_b_glu, w_out1, g_ffn, ffn_w_up, ffn_conv_w, ffn_conv_b, ffn_w_down, g_final):
    _, H_g, DK, DV = state_gla_S.shape
    _, H_m, DH, _ = state_mlstm_C.shape
    G, P = s5_lam_re.shape
    rank = gla_w_alpha2.shape[0]
    n_main = 2 * H_g * DK + 2 * H_g * DV
    w_in0_t = w_in0.T.astype(bf16)
    w_in0_main = [w_in0_t, w_in0_t[n_main + rank:]]
    w_in0_cols = [n_main, w_in0.shape[1] - n_main - rank]
    w_in0_alr = jnp.pad(w_in0_t[n_main:n_main + rank], ((0, LANES - rank), (0, 0)))
    gla_w2p = jnp.pad(gla_w_alpha2, ((0, LANES - rank), (0, 0)))
    ml_tile = 256
    ml_wg = jnp.pad(jnp.concatenate([ml_w_igate, ml_w_fgate], axis=1), ((0, 0), (0, LANES - 2 * H_m))).astype(bf16)
    ml_bg = jnp.pad(jnp.concatenate([ml_b_igate, ml_b_fgate]), (0, LANES - 2 * H_m)).reshape(1, LANES)
    rows, bbd, cbd = _s5_layouts(s5_lam_re, s5_lam_im, s5_log_dt, s5_B_re, s5_B_im, s5_C_re, s5_C_im)
    w = dict(
        g_mix0=g_mix0, w_in0_main=w_in0_main, w_in0_cols=w_in0_cols, w_in0_alr=w_in0_alr, gla_w2p=gla_w2p, gla_b_alpha=gla_b_alpha,
        gla_g_norm=gla_g_norm, gla_dims=(H_g, DK, DV), rg_conv_w=rg_conv_w, rg_conv_b=rg_conv_b, rg_w_r=rg_w_r,
        rg_b_r=rg_b_r, rg_w_i=rg_w_i, rg_b_i=rg_b_i, rg_lambda=rg_lambda, w_out0=w_out0.astype(bf16),
        g_mix1=g_mix1, w_in1=w_in1.astype(bf16), ml_conv_w=ml_conv_w, ml_conv_b=ml_conv_b,
        ml_wq_t=_blockdiag_tiles(ml_wq, ml_tile).astype(bf16), ml_wk_t=_blockdiag_tiles(ml_wk, ml_tile).astype(bf16),
        ml_wv_t=_blockdiag_tiles(ml_wv, ml_tile).astype(bf16), ml_wg=ml_wg, ml_bg=ml_bg, ml_g_norm=ml_g_norm,
        ml_skip=ml_skip, ml_dims=(H_m, DH), s5_dims=(G, P), s5_gc=s5_B_re.shape[2], s5_rows=rows, s5_bbd=bbd,
        s5_cbd=cbd, s5_D=s5_D, s5_w_glu=s5_w_glu.astype(bf16), s5_b_glu=s5_b_glu, w_out1=w_out1.astype(bf16),
        g_ffn=g_ffn, g_final=g_final,
        ffn=_ffn_prepare(ffn_w_up, ffn_conv_w, ffn_conv_b, ffn_w_down))
    st_s = dict(gla_S=state_gla_S, rg_h=state_rglru_h, rg_conv=state_rglru_conv, ml_C=state_mlstm_C,
                ml_n=state_mlstm_n, ml_m=state_mlstm_m, ml_conv=state_mlstm_conv, s5_re=state_s5_re,
                s5_im=state_s5_im, ffn_conv=state_ffn_conv)
    y_p, np_ = _trunk(x_prompt, None, w, sample=False)
    y_s, ns_ = _trunk(x_sample, st_s, w, sample=True)
    names = ('gla_S', 'rg_h', 'rg_conv', 'ml_C', 'ml_n', 'ml_m', 'ml_conv', 's5_re', 's5_im', 'ffn_conv')
    outs = [y_p, y_s]
    for nme in names:
        outs += [np_[nme], ns_[nme]]
    return tuple(outs)
```
